```python
import jax, jax.numpy as jnp
from jax import lax
import numpy as np

D_MODEL = 1024
BATCH = 8
SEQ = 4096
DEPTH = 1

PLE_DIM = 256
RET_HEADS = 4
RET_HEAD_DIM = 128
RET_WIDTH = RET_HEADS * RET_HEAD_DIM
FOX_HEADS = 8
FOX_HEAD_DIM = 64
FOX_WIDTH = FOX_HEADS * FOX_HEAD_DIM
D_FF = 2816
RET_CHUNK = 128
Q_BLOCK = 128
ROPE_BASE = 10000.0
EPS = 1e-6
IN_SIZES = (RET_WIDTH,) * 4 + (FOX_WIDTH,) * 3 + (FOX_HEADS,)
IN_COLS = 4 * RET_WIDTH + 3 * FOX_WIDTH + FOX_HEADS

kernel_name = 'hybrid_retention_forgetting_macaron_block'


def rms_norm(x, g):
    x32 = x.astype(jnp.float32)
    y = x32 * lax.rsqrt(jnp.mean(x32 * x32, axis=-1, keepdims=True) + EPS)
    return (y * g.astype(jnp.float32)).astype(x.dtype)


def swiglu(x, w_gate, w_up, w_down):
    return (jax.nn.silu(x @ w_gate) * (x @ w_up)) @ w_down


def rotary(x, pos):
    half = x.shape[-1] // 2
    inv_freq = 1.0 / (ROPE_BASE ** (jnp.arange(half, dtype=jnp.float32) / half))
    ang = pos.astype(jnp.float32)[..., None] * inv_freq
    cos = jnp.cos(ang)[:, :, None, :]
    sin = jnp.sin(ang)[:, :, None, :]
    x1 = x[..., 0::2]
    x2 = x[..., 1::2]
    return jnp.stack([x1 * cos - x2 * sin, x1 * sin + x2 * cos], axis=-1).reshape(x.shape)


def chunkwise_retention(q, k, v):
    b, s, h, dk = q.shape
    dv = v.shape[-1]
    c = RET_CHUNK
    n = s // c
    to_chunks = lambda t: t.reshape(b, n, c, h, t.shape[-1]).transpose(0, 3, 1, 2, 4)
    q = to_chunks(q)
    k = to_chunks(k) * (dk ** -0.5)
    v = to_chunks(v)
    log_gamma = jnp.log1p(-jnp.exp2(-5.0 - jnp.arange(h, dtype=jnp.float32)))
    idx = jnp.arange(c, dtype=jnp.float32)
    diff = idx[:, None] - idx[None, :]
    dmask = jnp.where(diff >= 0, jnp.exp(log_gamma[:, None, None] * jnp.maximum(diff, 0.0)), 0.0)
    scores = jnp.einsum('bhncd,bhnmd->bhncm', q, k) * dmask[None, :, None]
    y_inner = jnp.einsum('bhncm,bhnme->bhnce', scores, v)
    k_dec = k * jnp.exp(log_gamma[:, None] * (c - 1 - idx))[None, :, None, :, None]
    kv = jnp.einsum('bhnmd,bhnme->nbhde', k_dec, v)
    chunk_decay = jnp.exp(log_gamma * c)[None, :, None, None]

    def step(state, kv_n):
        return chunk_decay * state + kv_n, state

    _, states = lax.scan(step, jnp.zeros((b, h, dk, dv), jnp.float32), kv)
    q_dec = q * jnp.exp(log_gamma[:, None] * (idx + 1.0))[None, :, None, :, None]
    y_cross = jnp.einsum('bhncd,nbhde->bhnce', q_dec, states)
    y = y_inner + y_cross
    return y.transpose(0, 2, 3, 1, 4).reshape(b, s, h, dv)


def head_group_norm(y):
    mu = jnp.mean(y, axis=-1, keepdims=True)
    var = jnp.mean(jnp.square(y - mu), axis=-1, keepdims=True)
    return (y - mu) * lax.rsqrt(var + EPS)


def forgetting_attention(q, k, v, log_f):
    b, s, h, d = q.shape
    nb = s // Q_BLOCK
    cum = jnp.cumsum(log_f, axis=1).transpose(0, 2, 1)
    qh = q.transpose(0, 2, 1, 3) * (d ** -0.5)
    kh = k.transpose(0, 2, 1, 3)
    vh = v.transpose(0, 2, 1, 3)
    q_blocks = qh.reshape(b, h, nb, Q_BLOCK, d).transpose(2, 0, 1, 3, 4)
    c_blocks = cum.reshape(b, h, nb, Q_BLOCK).transpose(2, 0, 1, 3)
    k_pos = jnp.arange(s)

    def block(args):
        q_n, c_n, n = args
        logits = jnp.einsum('bhqd,bhkd->bhqk', q_n, kh) + c_n[..., None] - cum[:, :, None, :]
        q_pos = n * Q_BLOCK + jnp.arange(Q_BLOCK)
        logits = jnp.where(k_pos[None, :] <= q_pos[:, None], logits, -jnp.inf)
        probs = jax.nn.softmax(logits, axis=-1)
        return jnp.einsum('bhqk,bhkd->bhqd', probs, vh)

    o = lax.map(block, (q_blocks, c_blocks, jnp.arange(nb)))
    return o.transpose(1, 0, 3, 2, 4).reshape(b, s, h, d)


def _fwd_setup_inputs(seed: int = 0) -> dict:
    key = jax.random.key(seed)
    ks = jax.random.split(key, 24)
    f32 = jnp.float32

    def nrm(k, shape, fan_in):
        return jax.random.normal(k, shape, f32) * (fan_in ** -0.5)

    def gain(k, shape):
        return 1.0 + 0.05 * jax.random.normal(k, shape, f32)

    x = jax.random.normal(ks[0], (BATCH, SEQ, D_MODEL), f32)
    p = jax.random.normal(ks[1], (DEPTH, BATCH, SEQ, PLE_DIM), f32)
    positions = jnp.broadcast_to(jnp.arange(SEQ, dtype=jnp.int32)[None, :], (BATCH, SEQ))
    return {
        'x': x,
        'p': p,
        'positions': positions,
        'ln_ffn1': gain(ks[2], (DEPTH, D_MODEL)),
        'w_ffn1_gate': nrm(ks[3], (DEPTH, D_MODEL, D_FF), D_MODEL),
        'w_ffn1_up': nrm(ks[4], (DEPTH, D_MODEL, D_FF), D_MODEL),
        'w_ffn1_down': nrm(ks[5], (DEPTH, D_FF, D_MODEL), D_FF),
        'ln_mix': gain(ks[6], (DEPTH, D_MODEL)),
        'w_in': nrm(ks[7], (DEPTH, D_MODEL, IN_COLS), D_MODEL),
        'b_forget': jax.random.uniform(ks[8], (DEPTH, FOX_HEADS), f32, 1.0, 6.0),
        'w_merge': nrm(ks[9], (DEPTH, D_MODEL, 2 * D_MODEL), D_MODEL),
        'b_merge': 0.02 * jax.random.normal(ks[10], (DEPTH, 2 * D_MODEL), f32),
        'w_ret_out': nrm(ks[11], (DEPTH, RET_WIDTH, D_MODEL), RET_WIDTH),
        'w_fox_out': nrm(ks[12], (DEPTH, FOX_WIDTH, D_MODEL), FOX_WIDTH),
        'w_out': nrm(ks[13], (DEPTH, D_MODEL, D_MODEL), D_MODEL),
        'ln_ffn2': gain(ks[14], (DEPTH, D_MODEL)),
        'w_ffn2_gate': nrm(ks[15], (DEPTH, D_MODEL, D_FF), D_MODEL),
        'w_ffn2_up': nrm(ks[16], (DEPTH, D_MODEL, D_FF), D_MODEL),
        'w_ffn2_down': nrm(ks[17], (DEPTH, D_FF, D_MODEL), D_FF),
        'ln_ple': gain(ks[18], (DEPTH, D_MODEL)),
        'w_ple': nrm(ks[19], (DEPTH, PLE_DIM, D_MODEL), PLE_DIM),
        'w_ple_gate': nrm(ks[20], (DEPTH, D_MODEL, D_MODEL), D_MODEL),
        'ln_final': gain(ks[21], (D_MODEL,)),
    }


def _fwd_reference(x, p, positions, ln_ffn1, w_ffn1_gate, w_ffn1_up, w_ffn1_down, ln_mix, w_in,
              b_forget, w_merge, b_merge, w_ret_out, w_fox_out, w_out, ln_ffn2, w_ffn2_gate,
              w_ffn2_up, w_ffn2_down, ln_ple, w_ple, w_ple_gate, ln_final):
    dt = x.dtype
    b, s, _ = x.shape
    split_at = np.cumsum(IN_SIZES)[:-1].tolist()
    h = x
    for i in range(DEPTH):
        h = h + 0.5 * swiglu(rms_norm(h, ln_ffn1[i]), w_ffn1_gate[i], w_ffn1_up[i], w_ffn1_down[i])

        u = rms_norm(h, ln_mix[i])
        proj = u @ w_in[i]
        r_q, r_k, r_v, r_g, f_q, f_k, f_v, f_f = jnp.split(proj, split_at, axis=-1)

        rq = rotary(r_q.astype(jnp.float32).reshape(b, s, RET_HEADS, RET_HEAD_DIM), positions)
        rk = rotary(r_k.astype(jnp.float32).reshape(b, s, RET_HEADS, RET_HEAD_DIM), positions)
        rv = r_v.astype(jnp.float32).reshape(b, s, RET_HEADS, RET_HEAD_DIM)
        y_ret = head_group_norm(chunkwise_retention(rq, rk, rv)).reshape(b, s, RET_WIDTH)
        y_ret = (y_ret * jax.nn.silu(r_g.astype(jnp.float32))).astype(dt)
        z_a = y_ret @ w_ret_out[i]

        log_f = jax.nn.log_sigmoid(f_f.astype(jnp.float32) + b_forget[i].astype(jnp.float32))
        fq = f_q.astype(jnp.float32).reshape(b, s, FOX_HEADS, FOX_HEAD_DIM)
        fk = f_k.astype(jnp.float32).reshape(b, s, FOX_HEADS, FOX_HEAD_DIM)
        fv = f_v.astype(jnp.float32).reshape(b, s, FOX_HEADS, FOX_HEAD_DIM)
        y_fox = forgetting_attention(fq, fk, fv, log_f).reshape(b, s, FOX_WIDTH).astype(dt)
        z_b = y_fox @ w_fox_out[i]

        gates = jax.nn.sigmoid(u @ w_merge[i] + b_merge[i])
        g_a, g_b = jnp.split(gates, 2, axis=-1)
        h = h + (g_a * z_a + g_b * z_b) @ w_out[i]

        h = h + 0.5 * swiglu(rms_norm(h, ln_ffn2[i]), w_ffn2_gate[i], w_ffn2_up[i], w_ffn2_down[i])

        ple_gate = jax.nn.sigmoid(rms_norm(h, ln_ple[i]) @ w_ple_gate[i])
        h = h + ple_gate * (p[i].astype(dt) @ w_ple[i])
    return rms_norm(h, ln_final)


import jax as _jax
import jax.numpy as _jnp

TWIN_FORMAT = 'train_step'
FWD_PARAMS = ['x', 'p', 'positions', 'ln_ffn1', 'w_ffn1_gate', 'w_ffn1_up', 'w_ffn1_down', 'ln_mix', 'w_in', 'b_forget', 'w_merge', 'b_merge', 'w_ret_out', 'w_fox_out', 'w_out', 'ln_ffn2', 'w_ffn2_gate', 'w_ffn2_up', 'w_ffn2_down', 'ln_ple', 'w_ple', 'w_ple_gate', 'ln_final']
TWIN_WEIGHTS = ['ln_ffn1', 'w_ffn1_gate', 'w_ffn1_up', 'w_ffn1_down', 'ln_mix', 'w_in', 'b_forget', 'w_merge', 'b_merge', 'w_ret_out', 'w_fox_out', 'w_out', 'ln_ffn2', 'w_ffn2_gate', 'w_ffn2_up', 'w_ffn2_down', 'ln_ple', 'w_ple', 'w_ple_gate', 'ln_final']
TWIN_DIFF_INPUT = 'x'
TWIN_INPUTS = ['x', 'p', 'positions', 'ln_ffn1', 'w_ffn1_gate', 'w_ffn1_up', 'w_ffn1_down', 'ln_mix', 'w_in', 'b_forget', 'w_merge', 'b_merge', 'w_ret_out', 'w_fox_out', 'w_out', 'ln_ffn2', 'w_ffn2_gate', 'w_ffn2_up', 'w_ffn2_down', 'ln_ple', 'w_ple', 'w_ple_gate', 'ln_final', 'loss_target', 'm_ln_ffn1', 'm_w_ffn1_gate', 'm_w_ffn1_up', 'm_w_ffn1_down', 'm_ln_mix', 'm_w_in', 'm_b_forget', 'm_w_merge', 'm_b_merge', 'm_w_ret_out', 'm_w_fox_out', 'm_w_out', 'm_ln_ffn2', 'm_w_ffn2_gate', 'm_w_ffn2_up', 'm_w_ffn2_down', 'm_ln_ple', 'm_w_ple', 'm_w_ple_gate', 'm_ln_final', 'v_ln_ffn1', 'v_w_ffn1_gate', 'v_w_ffn1_up', 'v_w_ffn1_down', 'v_ln_mix', 'v_w_in', 'v_b_forget', 'v_w_merge', 'v_b_merge', 'v_w_ret_out', 'v_w_fox_out', 'v_w_out', 'v_ln_ffn2', 'v_w_ffn2_gate', 'v_w_ffn2_up', 'v_w_ffn2_down', 'v_ln_ple', 'v_w_ple', 'v_w_ple_gate', 'v_ln_final']
TWIN_OUTPUTS = ['loss', 'grad_x', 'grad_ln_ffn1', 'grad_w_ffn1_gate', 'grad_w_ffn1_up', 'grad_w_ffn1_down', 'grad_ln_mix', 'grad_w_in', 'grad_b_forget', 'grad_w_merge', 'grad_b_merge', 'grad_w_ret_out', 'grad_w_fox_out', 'grad_w_out', 'grad_ln_ffn2', 'grad_w_ffn2_gate', 'grad_w_ffn2_up', 'grad_w_ffn2_down', 'grad_ln_ple', 'grad_w_ple', 'grad_w_ple_gate', 'grad_ln_final', 'delta_ln_ffn1', 'delta_w_ffn1_gate', 'delta_w_ffn1_up', 'delta_w_ffn1_down', 'delta_ln_mix', 'delta_w_in', 'delta_b_forget', 'delta_w_merge', 'delta_b_merge', 'delta_w_ret_out', 'delta_w_fox_out', 'delta_w_out', 'delta_ln_ffn2', 'delta_w_ffn2_gate', 'delta_w_ffn2_up', 'delta_w_ffn2_down', 'delta_ln_ple', 'delta_w_ple', 'delta_w_ple_gate', 'delta_ln_final', 'new_m_ln_ffn1', 'new_m_w_ffn1_gate', 'new_m_w_ffn1_up', 'new_m_w_ffn1_down', 'new_m_ln_mix', 'new_m_w_in', 'new_m_b_forget', 'new_m_w_merge', 'new_m_b_merge', 'new_m_w_ret_out', 'new_m_w_fox_out', 'new_m_w_out', 'new_m_ln_ffn2', 'new_m_w_ffn2_gate', 'new_m_w_ffn2_up', 'new_m_w_ffn2_down', 'new_m_ln_ple', 'new_m_w_ple', 'new_m_w_ple_gate', 'new_m_ln_final', 'new_v_ln_ffn1', 'new_v_w_ffn1_gate', 'new_v_w_ffn1_up', 'new_v_w_ffn1_down', 'new_v_ln_mix', 'new_v_w_in', 'new_v_b_forget', 'new_v_w_merge', 'new_v_b_merge', 'new_v_w_ret_out', 'new_v_w_fox_out', 'new_v_w_out', 'new_v_ln_ffn2', 'new_v_w_ffn2_gate', 'new_v_w_ffn2_up', 'new_v_w_ffn2_down', 'new_v_ln_ple', 'new_v_w_ple', 'new_v_w_ple_gate', 'new_v_ln_final']
TWIN_LEAF_KINDS = {'loss': 'loss', 'grad_x': 'grad_x', 'grad_ln_ffn1': 'grad_w', 'grad_w_ffn1_gate': 'grad_w', 'grad_w_ffn1_up': 'grad_w', 'grad_w_ffn1_down': 'grad_w', 'grad_ln_mix': 'grad_w', 'grad_w_in': 'grad_w', 'grad_b_forget': 'grad_w', 'grad_w_merge': 'grad_w', 'grad_b_merge': 'grad_w', 'grad_w_ret_out': 'grad_w', 'grad_w_fox_out': 'grad_w', 'grad_w_out': 'grad_w', 'grad_ln_ffn2': 'grad_w', 'grad_w_ffn2_gate': 'grad_w', 'grad_w_ffn2_up': 'grad_w', 'grad_w_ffn2_down': 'grad_w', 'grad_ln_ple': 'grad_w', 'grad_w_ple': 'grad_w', 'grad_w_ple_gate': 'grad_w', 'grad_ln_final': 'grad_w', 'delta_ln_ffn1': 'delta_w', 'delta_w_ffn1_gate': 'delta_w', 'delta_w_ffn1_up': 'delta_w', 'delta_w_ffn1_down': 'delta_w', 'delta_ln_mix': 'delta_w', 'delta_w_in': 'delta_w', 'delta_b_forget': 'delta_w', 'delta_w_merge': 'delta_w', 'delta_b_merge': 'delta_w', 'delta_w_ret_out': 'delta_w', 'delta_w_fox_out': 'delta_w', 'delta_w_out': 'delta_w', 'delta_ln_ffn2': 'delta_w', 'delta_w_ffn2_gate': 'delta_w', 'delta_w_ffn2_up': 'delta_w', 'delta_w_ffn2_down': 'delta_w', 'delta_ln_ple': 'delta_w', 'delta_w_ple': 'delta_w', 'delta_w_ple_gate': 'delta_w', 'delta_ln_final': 'delta_w', 'new_m_ln_ffn1': 'new_m', 'new_m_w_ffn1_gate': 'new_m', 'new_m_w_ffn1_up': 'new_m', 'new_m_w_ffn1_down': 'new_m', 'new_m_ln_mix': 'new_m', 'new_m_w_in': 'new_m', 'new_m_b_forget': 'new_m', 'new_m_w_merge': 'new_m', 'new_m_b_merge': 'new_m', 'new_m_w_ret_out': 'new_m', 'new_m_w_fox_out': 'new_m', 'new_m_w_out': 'new_m', 'new_m_ln_ffn2': 'new_m', 'new_m_w_ffn2_gate': 'new_m', 'new_m_w_ffn2_up': 'new_m', 'new_m_w_ffn2_down': 'new_m', 'new_m_ln_ple': 'new_m', 'new_m_w_ple': 'new_m', 'new_m_w_ple_gate': 'new_m', 'new_m_ln_final': 'new_m', 'new_v_ln_ffn1': 'new_v', 'new_v_w_ffn1_gate': 'new_v', 'new_v_w_ffn1_up': 'new_v', 'new_v_w_ffn1_down': 'new_v', 'new_v_ln_mix': 'new_v', 'new_v_w_in': 'new_v', 'new_v_b_forget': 'new_v', 'new_v_w_merge': 'new_v', 'new_v_b_merge': 'new_v', 'new_v_w_ret_out': 'new_v', 'new_v_w_fox_out': 'new_v', 'new_v_w_out': 'new_v', 'new_v_ln_ffn2': 'new_v', 'new_v_w_ffn2_gate': 'new_v', 'new_v_w_ffn2_up': 'new_v', 'new_v_w_ffn2_down': 'new_v', 'new_v_ln_ple': 'new_v', 'new_v_w_ple': 'new_v', 'new_v_w_ple_gate': 'new_v', 'new_v_ln_final': 'new_v'}


def _forward(args):
    return _fwd_reference(*[args[k] for k in FWD_PARAMS])


def _output_shape():
    out = _jax.eval_shape(lambda: _forward(_fwd_setup_inputs(0)))
    return out.shape, out.dtype

N_MICROBATCH = 1
ADAM_LR = 0.001
ADAM_B1 = 0.9
ADAM_B2 = 0.999
ADAM_EPS = 1e-08
ADAM_WD = 0.01
ADAM_STEP = 10
PER_EXAMPLE_BATCH_AXIS = {'x': 0, 'p': 1, 'positions': 0, 'loss_target': 0}
SHARED_INPUTS = []
_WEIGHT_DTYPES = {'ln_ffn1': _jnp.float32, 'w_ffn1_gate': _jnp.float32, 'w_ffn1_up': _jnp.float32, 'w_ffn1_down': _jnp.float32, 'ln_mix': _jnp.float32, 'w_in': _jnp.float32, 'b_forget': _jnp.float32, 'w_merge': _jnp.float32, 'b_merge': _jnp.float32, 'w_ret_out': _jnp.float32, 'w_fox_out': _jnp.float32, 'w_out': _jnp.float32, 'ln_ffn2': _jnp.float32, 'w_ffn2_gate': _jnp.float32, 'w_ffn2_up': _jnp.float32, 'w_ffn2_down': _jnp.float32, 'ln_ple': _jnp.float32, 'w_ple': _jnp.float32, 'w_ple_gate': _jnp.float32, 'ln_final': _jnp.float32}
MOMENT_SCALE = {'ln_ffn1': 8.191263e-02, 'w_ffn1_gate': 3.477113e-02, 'w_ffn1_up': 3.371559e-02, 'w_ffn1_down': 5.580169e-02, 'ln_mix': 1.243272e-01, 'w_in': 5.925264e-02, 'b_forget': 1.929562e-01, 'w_merge': 1.635322e-02, 'b_merge': 1.696072e-02, 'w_ret_out': 4.970168e-02, 'w_fox_out': 3.226679e-02, 'w_out': 5.919003e-02, 'ln_ffn2': 6.004174e-02, 'w_ffn2_gate': 2.662172e-02, 'w_ffn2_up': 2.588796e-02, 'w_ffn2_down': 4.289396e-02, 'ln_ple': 2.992953e-02, 'w_ple': 7.773561e-02, 'w_ple_gate': 2.982800e-02, 'ln_final': 3.202848e+01}


def _to_microbatches(a, axis):
    t = _jnp.moveaxis(a, axis, 0)
    t = t.reshape((N_MICROBATCH, t.shape[0] // N_MICROBATCH) + t.shape[1:])
    return _jnp.moveaxis(t, 1, axis + 1)


def setup_inputs(seed: int = 0) -> dict:
    inp = _fwd_setup_inputs(seed)
    key = _jax.random.fold_in(_jax.random.key(seed), 7919)
    shape, _ = _output_shape()
    out = dict(inp)
    out["loss_target"] = _jax.random.normal(_jax.random.fold_in(key, 0), shape, _jnp.float32)
    for i, name in enumerate(TWIN_WEIGHTS):
        w = inp[name].astype(_jnp.float32)
        if MOMENT_SCALE is None:
            s = _jnp.sqrt(_jnp.mean(_jnp.square(w)) + 1e-30)
        else:
            s = MOMENT_SCALE[name]
        km, kv = _jax.random.split(_jax.random.fold_in(key, i + 1))
        out[name] = w
        out["m_" + name] = s * _jax.random.normal(km, w.shape, _jnp.float32)
        out["v_" + name] = (s * s) * _jax.random.uniform(kv, w.shape, _jnp.float32, 0.5, 1.5)
    if N_MICROBATCH > 1:
        for name, axis in PER_EXAMPLE_BATCH_AXIS.items():
            out[name] = _to_microbatches(out[name], axis)
    return {'x': out['x'], 'p': out['p'], 'positions': out['positions'], 'ln_ffn1': out['ln_ffn1'], 'w_ffn1_gate': out['w_ffn1_gate'], 'w_ffn1_up': out['w_ffn1_up'], 'w_ffn1_down': out['w_ffn1_down'], 'ln_mix': out['ln_mix'], 'w_in': out['w_in'], 'b_forget': out['b_forget'], 'w_merge': out['w_merge'], 'b_merge': out['b_merge'], 'w_ret_out': out['w_ret_out'], 'w_fox_out': out['w_fox_out'], 'w_out': out['w_out'], 'ln_ffn2': out['ln_ffn2'], 'w_ffn2_gate': out['w_ffn2_gate'], 'w_ffn2_up': out['w_ffn2_up'], 'w_ffn2_down': out['w_ffn2_down'], 'ln_ple': out['ln_ple'], 'w_ple': out['w_ple'], 'w_ple_gate': out['w_ple_gate'], 'ln_final': out['ln_final'], 'loss_target': out['loss_target'], 'm_ln_ffn1': out['m_ln_ffn1'], 'm_w_ffn1_gate': out['m_w_ffn1_gate'], 'm_w_ffn1_up': out['m_w_ffn1_up'], 'm_w_ffn1_down': out['m_w_ffn1_down'], 'm_ln_mix': out['m_ln_mix'], 'm_w_in': out['m_w_in'], 'm_b_forget': out['m_b_forget'], 'm_w_merge': out['m_w_merge'], 'm_b_merge': out['m_b_merge'], 'm_w_ret_out': out['m_w_ret_out'], 'm_w_fox_out': out['m_w_fox_out'], 'm_w_out': out['m_w_out'], 'm_ln_ffn2': out['m_ln_ffn2'], 'm_w_ffn2_gate': out['m_w_ffn2_gate'], 'm_w_ffn2_up': out['m_w_ffn2_up'], 'm_w_ffn2_down': out['m_w_ffn2_down'], 'm_ln_ple': out['m_ln_ple'], 'm_w_ple': out['m_w_ple'], 'm_w_ple_gate': out['m_w_ple_gate'], 'm_ln_final': out['m_ln_final'], 'v_ln_ffn1': out['v_ln_ffn1'], 'v_w_ffn1_gate': out['v_w_ffn1_gate'], 'v_w_ffn1_up': out['v_w_ffn1_up'], 'v_w_ffn1_down': out['v_w_ffn1_down'], 'v_ln_mix': out['v_ln_mix'], 'v_w_in': out['v_w_in'], 'v_b_forget': out['v_b_forget'], 'v_w_merge': out['v_w_merge'], 'v_b_merge': out['v_b_merge'], 'v_w_ret_out': out['v_w_ret_out'], 'v_w_fox_out': out['v_w_fox_out'], 'v_w_out': out['v_w_out'], 'v_ln_ffn2': out['v_ln_ffn2'], 'v_w_ffn2_gate': out['v_w_ffn2_gate'], 'v_w_ffn2_up': out['v_w_ffn2_up'], 'v_w_ffn2_down': out['v_w_ffn2_down'], 'v_ln_ple': out['v_ln_ple'], 'v_w_ple': out['v_w_ple'], 'v_w_ple_gate': out['v_w_ple_gate'], 'v_ln_final': out['v_ln_final']}


def _loss(weights, diff, rest, loss_target):
    with _jax.named_scope("forward"):
        args = {**rest, TWIN_DIFF_INPUT: diff, **{k: w.astype(_WEIGHT_DTYPES[k]) for k, w in weights.items()}}
        y = _forward(args)
    with _jax.named_scope("loss_head"):
        err = _jnp.square(y.astype(_jnp.float32) - loss_target)
        return 0.5 * _jnp.sum(_jnp.mean(err, axis=-1)) if err.ndim else 0.5 * err


def _adamw(w, g, m, v):
    m = ADAM_B1 * m + (1.0 - ADAM_B1) * g
    v = ADAM_B2 * v + (1.0 - ADAM_B2) * _jnp.square(g)
    m_hat = m / (1.0 - ADAM_B1 ** ADAM_STEP)
    v_hat = v / (1.0 - ADAM_B2 ** ADAM_STEP)
    delta = -ADAM_LR * (m_hat / (_jnp.sqrt(v_hat) + ADAM_EPS) + ADAM_WD * w)
    return delta, m, v


def reference(x, p, positions, ln_ffn1, w_ffn1_gate, w_ffn1_up, w_ffn1_down, ln_mix, w_in, b_forget, w_merge, b_merge, w_ret_out, w_fox_out, w_out, ln_ffn2, w_ffn2_gate, w_ffn2_up, w_ffn2_down, ln_ple, w_ple, w_ple_gate, ln_final, loss_target, m_ln_ffn1, m_w_ffn1_gate, m_w_ffn1_up, m_w_ffn1_down, m_ln_mix, m_w_in, m_b_forget, m_w_merge, m_b_merge, m_w_ret_out, m_w_fox_out, m_w_out, m_ln_ffn2, m_w_ffn2_gate, m_w_ffn2_up, m_w_ffn2_down, m_ln_ple, m_w_ple, m_w_ple_gate, m_ln_final, v_ln_ffn1, v_w_ffn1_gate, v_w_ffn1_up, v_w_ffn1_down, v_ln_mix, v_w_in, v_b_forget, v_w_merge, v_b_merge, v_w_ret_out, v_w_fox_out, v_w_out, v_ln_ffn2, v_w_ffn2_gate, v_w_ffn2_up, v_w_ffn2_down, v_ln_ple, v_w_ple, v_w_ple_gate, v_ln_final):
    given = dict(x=x, p=p, positions=positions, ln_ffn1=ln_ffn1, w_ffn1_gate=w_ffn1_gate, w_ffn1_up=w_ffn1_up, w_ffn1_down=w_ffn1_down, ln_mix=ln_mix, w_in=w_in, b_forget=b_forget, w_merge=w_merge, b_merge=b_merge, w_ret_out=w_ret_out, w_fox_out=w_fox_out, w_out=w_out, ln_ffn2=ln_ffn2, w_ffn2_gate=w_ffn2_gate, w_ffn2_up=w_ffn2_up, w_ffn2_down=w_ffn2_down, ln_ple=ln_ple, w_ple=w_ple, w_ple_gate=w_ple_gate, ln_final=ln_final, loss_target=loss_target, m_ln_ffn1=m_ln_ffn1, m_w_ffn1_gate=m_w_ffn1_gate, m_w_ffn1_up=m_w_ffn1_up, m_w_ffn1_down=m_w_ffn1_down, m_ln_mix=m_ln_mix, m_w_in=m_w_in, m_b_forget=m_b_forget, m_w_merge=m_w_merge, m_b_merge=m_b_merge, m_w_ret_out=m_w_ret_out, m_w_fox_out=m_w_fox_out, m_w_out=m_w_out, m_ln_ffn2=m_ln_ffn2, m_w_ffn2_gate=m_w_ffn2_gate, m_w_ffn2_up=m_w_ffn2_up, m_w_ffn2_down=m_w_ffn2_down, m_ln_ple=m_ln_ple, m_w_ple=m_w_ple, m_w_ple_gate=m_w_ple_gate, m_ln_final=m_ln_final, v_ln_ffn1=v_ln_ffn1, v_w_ffn1_gate=v_w_ffn1_gate, v_w_ffn1_up=v_w_ffn1_up, v_w_ffn1_down=v_w_ffn1_down, v_ln_mix=v_ln_mix, v_w_in=v_w_in, v_b_forget=v_b_forget, v_w_merge=v_w_merge, v_b_merge=v_b_merge, v_w_ret_out=v_w_ret_out, v_w_fox_out=v_w_fox_out, v_w_out=v_w_out, v_ln_ffn2=v_ln_ffn2, v_w_ffn2_gate=v_w_ffn2_gate, v_w_ffn2_up=v_w_ffn2_up, v_w_ffn2_down=v_w_ffn2_down, v_ln_ple=v_ln_ple, v_w_ple=v_w_ple, v_w_ple_gate=v_w_ple_gate, v_ln_final=v_ln_final)
    weights = {n: given[n] for n in TWIN_WEIGHTS}
    shared = {n: given[n] for n in SHARED_INPUTS}
    per_example = {n: given[n] for n in ['x', 'p', 'positions']}
    grad_fn = _jax.value_and_grad(_loss, argnums=(0, 1))

    def one_microbatch(ex, loss_target):
        ex = dict(ex)
        diff = ex.pop(TWIN_DIFF_INPUT)
        return grad_fn(weights, diff, {**shared, **ex}, loss_target)

    if N_MICROBATCH == 1:
        loss, (grad_w, grad_x) = one_microbatch(per_example, given["loss_target"])
    else:
        def body(carry, xs):
            loss_sum, grad_sum = carry
            l_k, (gw_k, gx_k) = one_microbatch(xs[0], xs[1])
            with _jax.named_scope("update"):
                return (loss_sum + l_k, _jax.tree.map(_jnp.add, grad_sum, gw_k)), gx_k

        init = (_jnp.zeros((), _jnp.float32), _jax.tree.map(_jnp.zeros_like, weights))
        (loss, grad_w), grad_x = _jax.lax.scan(body, init, (per_example, given["loss_target"]))
    with _jax.named_scope("update"):
        delta_w, new_m, new_v = {}, {}, {}
        for n in TWIN_WEIGHTS:
            delta_w[n], new_m[n], new_v[n] = _adamw(weights[n], grad_w[n], given["m_" + n], given["v_" + n])
    return (loss, grad_x, *[grad_w[n] for n in TWIN_WEIGHTS], *[delta_w[n] for n in TWIN_WEIGHTS],
            *[new_m[n] for n in TWIN_WEIGHTS], *[new_v[n] for n in TWIN_WEIGHTS])
```

```python
import functools

import numpy as np
import jax
import jax.numpy as jnp
from jax import lax
from jax.experimental import pallas as pl
from jax.experimental.pallas import tpu as pltpu

F32 = jnp.float32
BF16 = jnp.bfloat16

N_DEV = 8
EPS = 1e-6
RET_HEADS = 4
RET_DIM = 128
RET_WIDTH = RET_HEADS * RET_DIM
FOX_HEADS = 8
FOX_DIM = 64
FOX_WIDTH = FOX_HEADS * FOX_DIM
CHUNK = 128
ROPE_BASE = 10000.0
IN_COLS = 4 * RET_WIDTH + 3 * FOX_WIDTH + FOX_HEADS
IN_PAD = 4 * RET_WIDTH + 3 * FOX_WIDTH + 128
LANES = 128
NEG = -1e30

ADAM_LR = 0.001
ADAM_B1 = 0.9
ADAM_B2 = 0.999
ADAM_EPS = 1e-08
ADAM_WD = 0.01
ADAM_STEP = 10

VMEM_LIMIT_BYTES = 56 * 1024 * 1024

MESH = pl.DeviceIdType.MESH


def _tile(dim, pref, mult):
    if dim <= pref:
        return dim
    t = (pref // mult) * mult
    while t >= mult:
        if dim % t == 0:
            return t
        t -= mult
    return dim


def _pcall(body, *, name, out_shape, grid, in_specs, out_specs, scratch_shapes=(), dims=None):
    return pl.pallas_call(
        body,
        name=name,
        out_shape=out_shape,
        grid=grid,
        in_specs=in_specs,
        out_specs=out_specs,
        scratch_shapes=list(scratch_shapes),
        compiler_params=pltpu.CompilerParams(dimension_semantics=dims, vmem_limit_bytes=VMEM_LIMIT_BYTES),
    )


def _dot(a, b, ca, cb):
    return lax.dot_general(a, b, (((ca,), (cb,)), ((), ())), preferred_element_type=F32)


def _sigmoid(x):
    return 1.0 / (1.0 + jnp.exp(-x))


def _mm(a, b, mode, out_dtype, name, tm=1024, tn=1024, tk=512):
    if mode == "nn":
        (m, k), (k2, n) = a.shape, b.shape
    elif mode == "nt":
        (m, k), (n, k2) = a.shape, b.shape
    else:
        (k, m), (k2, n) = a.shape, b.shape
    assert k == k2, (name, a.shape, b.shape)
    tm = _tile(m, tm, 128 if mode == "tn" else 16)
    tn = _tile(n, tn, 128)
    tk = _tile(k, tk, 128)
    nk = k // tk
    if mode == "tn":
        a_spec = pl.BlockSpec((tk, tm), lambda i, j, kk: (kk, i))
        ca = 0
    else:
        a_spec = pl.BlockSpec((tm, tk), lambda i, j, kk: (i, kk))
        ca = 1
    if mode == "nt":
        b_spec = pl.BlockSpec((tn, tk), lambda i, j, kk: (j, kk))
        cb = 1
    else:
        b_spec = pl.BlockSpec((tk, tn), lambda i, j, kk: (kk, j))
        cb = 0

    def body(a_ref, b_ref, o_ref, acc_ref):
        kk = pl.program_id(2)
        part = _dot(a_ref[...], b_ref[...], ca, cb)

        @pl.when(kk == 0)
        def _():
            acc_ref[...] = part

        @pl.when(kk > 0)
        def _():
            acc_ref[...] += part

        @pl.when(kk == nk - 1)
        def _():
            o_ref[...] = acc_ref[...].astype(out_dtype)

    return _pcall(
        body,
        name=name,
        out_shape=jax.ShapeDtypeStruct((m, n), out_dtype),
        grid=(m // tm, n // tn, nk),
        in_specs=[a_spec, b_spec],
        out_specs=pl.BlockSpec((tm, tn), lambda i, j, kk: (i, j)),
        scratch_shapes=[pltpu.VMEM((tm, tn), F32)],
        dims=("parallel", "parallel", "arbitrary"),
    )(a, b)


def _rms_fwd(h, gain, name, f=None, scale=0.5):
    t, d = h.shape
    tt = _tile(t, 512, 16)
    row = pl.BlockSpec((tt, d), lambda i: (i, 0))
    vec = pl.BlockSpec((1, d), lambda i: (0, 0))

    def norm(hv, g_ref, n_ref):
        r = lax.rsqrt(jnp.mean(hv * hv, axis=-1, keepdims=True) + EPS)
        n_ref[...] = (hv * r * g_ref[...]).astype(BF16)

    if f is None:

        def body(h_ref, g_ref, n_ref):
            norm(h_ref[...], g_ref, n_ref)

        return _pcall(body, name=name, out_shape=jax.ShapeDtypeStruct((t, d), BF16), grid=(t // tt,),
                      in_specs=[row, vec], out_specs=row, dims=("parallel",))(h, gain)

    def body(h_ref, f_ref, g_ref, hn_ref, n_ref):
        hv = h_ref[...] + scale * f_ref[...]
        hn_ref[...] = hv
        norm(hv, g_ref, n_ref)

    return _pcall(body, name=name,
                  out_shape=(jax.ShapeDtypeStruct((t, d), F32), jax.ShapeDtypeStruct((t, d), BF16)),
                  grid=(t // tt,), in_specs=[row, row, vec], out_specs=(row, row), dims=("parallel",))(h, f, gain)


def _rms_bwd(dn, h, gain, dh_in, name, out_scale):
    t, d = h.shape
    tt = _tile(t, 512, 16)
    row = pl.BlockSpec((tt, d), lambda i: (i, 0))
    vec = pl.BlockSpec((1, d), lambda i: (0, 0))

    def body(dn_ref, h_ref, g_ref, dhin_ref, dh_ref, dhb_ref, dg_ref):
        hv = h_ref[...]
        dnv = dn_ref[...].astype(F32)
        r = lax.rsqrt(jnp.mean(hv * hv, axis=-1, keepdims=True) + EPS)
        dng = dnv * g_ref[...]
        dh = dhin_ref[...] + r * dng - hv * (r * r * r) * jnp.mean(dng * hv, axis=-1, keepdims=True)
        dh_ref[...] = dh
        dhb_ref[...] = (out_scale * dh).astype(BF16)
        part = jnp.sum(dnv * hv * r, axis=0, keepdims=True)

        @pl.when(pl.program_id(0) == 0)
        def _():
            dg_ref[...] = part

        @pl.when(pl.program_id(0) > 0)
        def _():
            dg_ref[...] += part

    return _pcall(body, name=name,
                  out_shape=(jax.ShapeDtypeStruct((t, d), F32), jax.ShapeDtypeStruct((t, d), BF16),
                             jax.ShapeDtypeStruct((1, d), F32)),
                  grid=(t // tt,), in_specs=[row, row, vec, row], out_specs=(row, row, vec),
                  dims=("arbitrary",))(dn, h, gain, dh_in)


def _swiglu_fwd(gu, name):
    t, f2 = gu.shape
    f = f2 // 2
    tt = _tile(t, 256, 16)

    def body(gu_ref, a_ref):
        g = gu_ref[:, :f]
        u = gu_ref[:, f:]
        a_ref[...] = (g * _sigmoid(g) * u).astype(BF16)

    return _pcall(body, name=name, out_shape=jax.ShapeDtypeStruct((t, f), BF16), grid=(t // tt,),
                  in_specs=[pl.BlockSpec((tt, f2), lambda i: (i, 0))],
                  out_specs=pl.BlockSpec((tt, f), lambda i: (i, 0)), dims=("parallel",))(gu)


def _swiglu_bwd(da, gu, name):
    t, f2 = gu.shape
    f = f2 // 2
    tt = _tile(t, 256, 16)

    def body(da_ref, gu_ref, d_ref):
        g = gu_ref[:, :f]
        u = gu_ref[:, f:]
        dav = da_ref[...].astype(F32)
        sg = _sigmoid(g)
        d_ref[:, :f] = (dav * u * (sg * (1.0 + g * (1.0 - sg)))).astype(BF16)
        d_ref[:, f:] = (dav * (g * sg)).astype(BF16)

    return _pcall(body, name=name, out_shape=jax.ShapeDtypeStruct((t, f2), BF16), grid=(t // tt,),
                  in_specs=[pl.BlockSpec((tt, f), lambda i: (i, 0)), pl.BlockSpec((tt, f2), lambda i: (i, 0))],
                  out_specs=pl.BlockSpec((tt, f2), lambda i: (i, 0)), dims=("parallel",))(da, gu)


def _rope_tables(pos_col, inv2, sign2):
    t = pos_col.shape[0]

    def body(p_ref, inv_ref, sg_ref, c_ref, s_ref):
        ang = p_ref[...].astype(F32) * inv_ref[...]
        c_ref[...] = jnp.cos(ang)
        s_ref[...] = jnp.sin(ang) * sg_ref[...]

    full = lambda shape: pl.BlockSpec(shape, lambda i: (0, 0))
    return _pcall(body, name="rope_tables", out_shape=(jax.ShapeDtypeStruct((t, RET_DIM), F32),) * 2, grid=(1,),
                  in_specs=[full((t, 1)), full((1, RET_DIM)), full((1, RET_DIM))],
                  out_specs=(full((t, RET_DIM)),) * 2, dims=("arbitrary",))(pos_col, inv2, sign2)


def _rot(x, c, s):
    return x * c + pltpu.roll(x, RET_DIM // 2, 1) * s


def _rot_t(g, c, s):
    return g * c + pltpu.roll(g * s, RET_DIM // 2, 1)


def _ret_consts():
    hh = np.arange(RET_HEADS, dtype=np.float32)
    log_gamma = np.log1p(-np.exp2(-5.0 - hh)).astype(np.float32)
    idx = np.arange(CHUNK, dtype=np.float32)
    diff = idx[:, None] - idx[None, :]
    dmask = np.where(diff >= 0, np.exp(log_gamma[:, None, None] * np.maximum(diff, 0.0)), 0.0).astype(np.float32)
    kdec = np.exp(log_gamma[:, None] * (CHUNK - 1 - idx)).astype(np.float32)
    qdec = np.exp(log_gamma[:, None] * (idx + 1.0)).astype(np.float32)
    cdec = np.exp(log_gamma * CHUNK).astype(np.float32)
    bc = lambda v: np.ascontiguousarray(np.broadcast_to(v[:, :, None], (RET_HEADS, CHUNK, RET_DIM)))
    cd = np.ascontiguousarray(np.broadcast_to(cdec[:, None, None], (RET_HEADS, 8, RET_DIM)))
    return jnp.asarray(dmask), jnp.asarray(bc(qdec)), jnp.asarray(bc(kdec)), jnp.asarray(cd)


def _ret_fwd(pm, off, cos2, sin2, consts):
    t = pm.shape[0]
    n_chunks = t // CHUNK
    dmask, qdec, kdec, cd = consts
    scale = RET_DIM ** -0.5

    def col(c0):
        return pl.BlockSpec((CHUNK, RET_DIM), lambda h, n: (n, off + c0 + h))

    tab = pl.BlockSpec((CHUNK, RET_DIM), lambda h, n: (n, 0))
    head3 = lambda r: pl.BlockSpec((None, r, RET_DIM), lambda h, n: (h, 0, 0))

    def body(q_ref, k_ref, v_ref, g_ref, c_ref, s_ref, dm_ref, qd_ref, kd_ref, cd_ref, y_ref, raw_ref, st_ref, s_acc):
        @pl.when(pl.program_id(1) == 0)
        def _():
            s_acc[...] = jnp.zeros_like(s_acc)

        c, s = c_ref[...], s_ref[...]
        q = _rot(q_ref[...], c, s)
        k = _rot(k_ref[...], c, s) * scale
        vb = v_ref[...].astype(BF16)
        g = g_ref[...]
        s_in = s_acc[...]
        st_ref[...] = s_in
        a = _dot(q.astype(BF16), k.astype(BF16), 1, 1) * dm_ref[...]
        y = _dot(a.astype(BF16), vb, 1, 0) + _dot((q * qd_ref[...]).astype(BF16), s_in.astype(BF16), 1, 0)
        s_acc[...] = cd_ref[0:1, :] * s_in + _dot((k * kd_ref[...]).astype(BF16), vb, 0, 0)
        raw_ref[...] = y
        mu = jnp.mean(y, axis=-1, keepdims=True)
        yc = y - mu
        rs = lax.rsqrt(jnp.mean(yc * yc, axis=-1, keepdims=True) + EPS)
        y_ref[...] = (yc * rs * (g * _sigmoid(g))).astype(BF16)

    out_blk = pl.BlockSpec((CHUNK, RET_DIM), lambda h, n: (n, h))
    return _pcall(
        body, name="retention_fwd",
        out_shape=(jax.ShapeDtypeStruct((t, RET_WIDTH), BF16), jax.ShapeDtypeStruct((t, RET_WIDTH), F32),
                   jax.ShapeDtypeStruct((RET_HEADS, n_chunks, RET_DIM, RET_DIM), F32)),
        grid=(RET_HEADS, n_chunks),
        in_specs=[col(0), col(4), col(8), col(12), tab, tab,
                  pl.BlockSpec((None, CHUNK, CHUNK), lambda h, n: (h, 0, 0)), head3(CHUNK), head3(CHUNK), head3(8)],
        out_specs=(out_blk, out_blk, pl.BlockSpec((None, None, RET_DIM, RET_DIM), lambda h, n: (h, n, 0, 0))),
        scratch_shapes=[pltpu.VMEM((RET_DIM, RET_DIM), F32)],
        dims=("parallel", "arbitrary"),
    )(pm, pm, pm, pm, cos2, sin2, dmask, qdec, kdec, cd)


def _ret_bwd(dy, pm, off, cos2, sin2, raw, states, consts):
    t = pm.shape[0]
    n_chunks = t // CHUNK
    dmask, qdec, kdec, cd = consts
    scale = RET_DIM ** -0.5
    rev = lambda n: n_chunks - 1 - n

    def col(c0):
        return pl.BlockSpec((CHUNK, RET_DIM), lambda h, n: (rev(n), off + c0 + h))

    tab = pl.BlockSpec((CHUNK, RET_DIM), lambda h, n: (rev(n), 0))
    blk = pl.BlockSpec((CHUNK, RET_DIM), lambda h, n: (rev(n), h))
    head3 = lambda r: pl.BlockSpec((None, r, RET_DIM), lambda h, n: (h, 0, 0))

    def body(dy_ref, q_ref, k_ref, v_ref, g_ref, c_ref, s_ref, raw_ref, st_ref, dm_ref, qd_ref, kd_ref, cd_ref,
             dq_ref, dk_ref, dv_ref, dg_ref, ds_acc):
        @pl.when(pl.program_id(1) == 0)
        def _():
            ds_acc[...] = jnp.zeros_like(ds_acc)

        c, s = c_ref[...], s_ref[...]
        q = _rot(q_ref[...], c, s)
        k = _rot(k_ref[...], c, s) * scale
        qb, kb, vb = q.astype(BF16), k.astype(BF16), v_ref[...].astype(BF16)
        g = g_ref[...]
        dm, qd, kd = dm_ref[...], qd_ref[...], kd_ref[...]
        y = raw_ref[...]
        mu = jnp.mean(y, axis=-1, keepdims=True)
        yc = y - mu
        rs = lax.rsqrt(jnp.mean(yc * yc, axis=-1, keepdims=True) + EPS)
        yn = yc * rs
        sg = _sigmoid(g)
        dyo = dy_ref[...]
        dg_ref[...] = (dyo * yn * (sg * (1.0 + g * (1.0 - sg)))).astype(BF16)
        dyn = dyo * (g * sg)
        dyr = rs * (dyn - jnp.mean(dyn, axis=-1, keepdims=True) - yn * jnp.mean(dyn * yn, axis=-1, keepdims=True))
        dyb = dyr.astype(BF16)
        s_in = st_ref[...].astype(BF16)
        ds_out = ds_acc[...]
        dsb = ds_out.astype(BF16)
        a = _dot(qb, kb, 1, 1) * dm
        da = (_dot(dyb, vb, 1, 1) * dm).astype(BF16)
        kdb = (k * kd).astype(BF16)
        qdb = (q * qd).astype(BF16)
        dv_ref[...] = (_dot(a.astype(BF16), dyb, 0, 0) + _dot(kdb, dsb, 1, 0)).astype(BF16)
        dqh = _dot(da, kb, 1, 0) + _dot(dyb, s_in, 1, 1) * qd
        dkh = _dot(da, qb, 0, 0) + _dot(vb, dsb, 1, 1) * kd
        ds_acc[...] = cd_ref[0:1, :] * ds_out + _dot(qdb, dyb, 0, 0)
        dq_ref[...] = _rot_t(dqh, c, s).astype(BF16)
        dk_ref[...] = (_rot_t(dkh, c, s) * scale).astype(BF16)

    return _pcall(
        body, name="retention_bwd",
        out_shape=(jax.ShapeDtypeStruct((t, RET_WIDTH), BF16),) * 4,
        grid=(RET_HEADS, n_chunks),
        in_specs=[blk, col(0), col(4), col(8), col(12), tab, tab, blk,
                  pl.BlockSpec((None, None, RET_DIM, RET_DIM), lambda h, n: (h, rev(n), 0, 0)),
                  pl.BlockSpec((None, CHUNK, CHUNK), lambda h, n: (h, 0, 0)), head3(CHUNK), head3(CHUNK), head3(8)],
        out_specs=(blk,) * 4,
        scratch_shapes=[pltpu.VMEM((RET_DIM, RET_DIM), F32)],
        dims=("parallel", "arbitrary"),
    )(dy, pm, pm, pm, pm, cos2, sin2, raw, states, dmask, qdec, kdec, cd)


def _split3(x):
    hi = x.astype(BF16)
    r1 = x - hi.astype(F32)
    mid = r1.astype(BF16)
    lo = (r1 - mid.astype(F32)).astype(BF16)
    return hi, mid, lo


def _tri_dot(tri, x):
    hi, mid, lo = _split3(x)
    return _dot(tri, lo, 1, 0) + _dot(tri, mid, 1, 0) + _dot(tri, hi, 1, 0)


def _log_sigmoid(z):
    return jnp.minimum(z, 0.0) - jnp.log1p(jnp.exp(-jnp.abs(z)))


def _fox_prep(pm, ff_tile, bpad):
    t = pm.shape[0]
    nb = t // LANES

    def body(ff_ref, b_ref, c_ref):
        r = lax.broadcasted_iota(jnp.int32, (LANES, LANES), 0)
        cc = lax.broadcasted_iota(jnp.int32, (LANES, LANES), 1)
        tri = jnp.where(cc <= r, 1.0, 0.0).astype(BF16)
        bias = b_ref[...]

        def step(b, carry):
            rows = pl.ds(pl.multiple_of(b * LANES, LANES), LANES)
            cs = _tri_dot(tri, _log_sigmoid(ff_ref[rows, :] + bias)) + carry
            c_ref[rows, :] = cs
            return cs[LANES - 1:LANES, :]

        lax.fori_loop(0, nb, step, jnp.zeros((1, LANES), F32))

    return _pcall(body, name="fox_cumsum", out_shape=jax.ShapeDtypeStruct((t, LANES), F32), grid=(1,),
                  in_specs=[pl.BlockSpec((t, LANES), lambda i: (0, ff_tile)), pl.BlockSpec((1, LANES), lambda i: (0, 0))],
                  out_specs=pl.BlockSpec((t, LANES), lambda i: (0, 0)), dims=("arbitrary",))(pm, bpad)


def _fox_post(dc, pm, ff_tile, bpad):
    t = pm.shape[0]
    nb = t // LANES

    def body(dc_ref, ff_ref, b_ref, d_ref, db_ref):
        r = lax.broadcasted_iota(jnp.int32, (LANES, LANES), 0)
        cc = lax.broadcasted_iota(jnp.int32, (LANES, LANES), 1)
        tri = jnp.where(cc >= r, 1.0, 0.0).astype(BF16)
        bias = b_ref[...]

        def step(i, carry):
            tail, acc = carry
            rows = pl.ds(pl.multiple_of((nb - 1 - i) * LANES, LANES), LANES)
            cs = _tri_dot(tri, dc_ref[rows, :]) + tail
            dff = cs * _sigmoid(-(ff_ref[rows, :] + bias))
            d_ref[rows, :] = dff.astype(BF16)
            return cs[0:1, :], acc + jnp.sum(dff, axis=0, keepdims=True)

        zero = jnp.zeros((1, LANES), F32)
        _, acc = lax.fori_loop(0, nb, step, (zero, zero))
        db_ref[...] = acc

    return _pcall(body, name="fox_forget_bwd",
                  out_shape=(jax.ShapeDtypeStruct((t, LANES), BF16), jax.ShapeDtypeStruct((1, LANES), F32)), grid=(1,),
                  in_specs=[pl.BlockSpec((t, LANES), lambda i: (0, 0)), pl.BlockSpec((t, LANES), lambda i: (0, ff_tile)),
                            pl.BlockSpec((1, LANES), lambda i: (0, 0))],
                  out_specs=(pl.BlockSpec((t, LANES), lambda i: (0, 0)), pl.BlockSpec((1, LANES), lambda i: (0, 0))),
                  dims=("arbitrary",))(dc, pm, bpad)


def _fox_logits(q_ref, k_ref, cq_ref, ck_ref, i, j, tq, tk):
    s = _dot(q_ref[...], k_ref[...], 1, 1) + cq_ref[...] - ck_ref[...]
    row = i * tq + lax.broadcasted_iota(jnp.int32, (tq, tk), 0)
    col = j * tk + lax.broadcasted_iota(jnp.int32, (tq, tk), 1)
    return jnp.where(col <= row, s, NEG)


def _fox_fwd(q, k, v, c_col, c_row, blk):
    hh, t, d = q.shape
    nb = t // blk
    qs = pl.BlockSpec((None, blk, d), lambda h, i, j: (h, i, 0))
    ks = pl.BlockSpec((None, blk, d), lambda h, i, j: (h, jnp.minimum(j, i), 0))
    cqs = pl.BlockSpec((None, blk, 1), lambda h, i, j: (h, i, 0))
    cks = pl.BlockSpec((None, 1, blk), lambda h, i, j: (h, 0, jnp.minimum(j, i)))

    def body(q_ref, k_ref, v_ref, cq_ref, ck_ref, o_ref, lse_ref, m_s, l_s, acc_s):
        i, j = pl.program_id(1), pl.program_id(2)

        @pl.when(j == 0)
        def _():
            m_s[...] = jnp.full_like(m_s, NEG)
            l_s[...] = jnp.zeros_like(l_s)
            acc_s[...] = jnp.zeros_like(acc_s)

        @pl.when(j <= i)
        def _():
            s = _fox_logits(q_ref, k_ref, cq_ref, ck_ref, i, j, blk, blk)
            m_old = m_s[...]
            m_new = jnp.maximum(m_old, jnp.max(s, axis=1, keepdims=True))
            alpha = jnp.exp(m_old - m_new)
            p = jnp.exp(s - m_new)
            l_s[...] = alpha * l_s[...] + jnp.sum(p, axis=1, keepdims=True)
            acc_s[...] = alpha * acc_s[...] + _dot(p.astype(BF16), v_ref[...], 1, 0)
            m_s[...] = m_new

        @pl.when(j == nb - 1)
        def _():
            o_ref[...] = acc_s[...] / l_s[...]
            lse_ref[...] = m_s[...] + jnp.log(l_s[...])

    return _pcall(
        body, name="fox_fwd",
        out_shape=(jax.ShapeDtypeStruct((hh, t, d), F32), jax.ShapeDtypeStruct((hh, t, 1), F32)),
        grid=(hh, nb, nb), in_specs=[qs, ks, ks, cqs, cks], out_specs=(qs, cqs),
        scratch_shapes=[pltpu.VMEM((blk, 1), F32), pltpu.VMEM((blk, 1), F32), pltpu.VMEM((blk, d), F32)],
        dims=("parallel", "parallel", "arbitrary"),
    )(q, k, v, c_col, c_row)


def _fox_bwd_kv(q, k, v, c_col, c_row, do, o, lse, blk):
    hh, t, d = q.shape
    nb = t // blk
    qi = lambda i, j: jnp.maximum(i, j)
    qs = pl.BlockSpec((None, blk, d), lambda h, j, i: (h, qi(i, j), 0))
    ks = pl.BlockSpec((None, blk, d), lambda h, j, i: (h, j, 0))
    cqs = pl.BlockSpec((None, blk, 1), lambda h, j, i: (h, qi(i, j), 0))
    cks = pl.BlockSpec((None, 1, blk), lambda h, j, i: (h, 0, j))

    def body(q_ref, k_ref, v_ref, cq_ref, ck_ref, do_ref, o_ref, lse_ref, dk_ref, dv_ref, dc_ref, dk_s, dv_s, dc_s):
        j, i = pl.program_id(1), pl.program_id(2)

        @pl.when(i == 0)
        def _():
            dk_s[...] = jnp.zeros_like(dk_s)
            dv_s[...] = jnp.zeros_like(dv_s)
            dc_s[...] = jnp.zeros_like(dc_s)

        @pl.when(i >= j)
        def _():
            s = _fox_logits(q_ref, k_ref, cq_ref, ck_ref, i, j, blk, blk)
            p = jnp.exp(s - lse_ref[...])
            dof = do_ref[...]
            dob = dof.astype(BF16)
            dv_s[...] += _dot(p.astype(BF16), dob, 0, 0)
            dp = _dot(dob, v_ref[...], 1, 1)
            delta = jnp.sum(dof * o_ref[...], axis=1, keepdims=True)
            ds = p * (dp - delta)
            dk_s[...] += _dot(ds.astype(BF16), q_ref[...], 0, 0)
            dc_s[...] -= jnp.sum(ds, axis=0, keepdims=True)

        @pl.when(i == nb - 1)
        def _():
            dk_ref[...] = dk_s[...].astype(BF16)
            dv_ref[...] = dv_s[...].astype(BF16)
            dc_ref[...] = dc_s[...]

    return _pcall(
        body, name="fox_bwd_kv",
        out_shape=(jax.ShapeDtypeStruct((hh, t, d), BF16), jax.ShapeDtypeStruct((hh, t, d), BF16),
                   jax.ShapeDtypeStruct((hh, 1, t), F32)),
        grid=(hh, nb, nb), in_specs=[qs, ks, ks, cqs, cks, qs, qs, cqs], out_specs=(ks, ks, cks),
        scratch_shapes=[pltpu.VMEM((blk, d), F32), pltpu.VMEM((blk, d), F32), pltpu.VMEM((1, blk), F32)],
        dims=("parallel", "parallel", "arbitrary"),
    )(q, k, v, c_col, c_row, do, o, lse)


def _fox_bwd_q(q, k, v, c_col, c_row, do, o, lse, blk, q_scale):
    hh, t, d = q.shape
    nb = t // blk
    qs = pl.BlockSpec((None, blk, d), lambda h, i, j: (h, i, 0))
    ks = pl.BlockSpec((None, blk, d), lambda h, i, j: (h, jnp.minimum(j, i), 0))
    cqs = pl.BlockSpec((None, blk, 1), lambda h, i, j: (h, i, 0))
    cks = pl.BlockSpec((None, 1, blk), lambda h, i, j: (h, 0, jnp.minimum(j, i)))

    def body(q_ref, k_ref, v_ref, cq_ref, ck_ref, do_ref, o_ref, lse_ref, dq_ref, dc_ref, dq_s, dc_s):
        i, j = pl.program_id(1), pl.program_id(2)

        @pl.when(j == 0)
        def _():
            dq_s[...] = jnp.zeros_like(dq_s)
            dc_s[...] = jnp.zeros_like(dc_s)

        @pl.when(j <= i)
        def _():
            s = _fox_logits(q_ref, k_ref, cq_ref, ck_ref, i, j, blk, blk)
            p = jnp.exp(s - lse_ref[...])
            dof = do_ref[...]
            dp = _dot(dof.astype(BF16), v_ref[...], 1, 1)
            delta = jnp.sum(dof * o_ref[...], axis=1, keepdims=True)
            ds = p * (dp - delta)
            dq_s[...] += _dot(ds.astype(BF16), k_ref[...], 1, 0)
            dc_s[...] += jnp.sum(ds, axis=1, keepdims=True)

        @pl.when(j == nb - 1)
        def _():
            dq_ref[...] = (dq_s[...] * q_scale).astype(BF16)
            dc_ref[...] = dc_s[...]

    return _pcall(
        body, name="fox_bwd_q",
        out_shape=(jax.ShapeDtypeStruct((hh, t, d), BF16), jax.ShapeDtypeStruct((hh, t, 1), F32)),
        grid=(hh, nb, nb), in_specs=[qs, ks, ks, cqs, cks, qs, qs, cqs], out_specs=(qs, cqs),
        scratch_shapes=[pltpu.VMEM((blk, d), F32), pltpu.VMEM((blk, 1), F32)],
        dims=("parallel", "parallel", "arbitrary"),
    )(q, k, v, c_col, c_row, do, o, lse)


def _merge_fwd(pm, bm, za, zb):
    t, d = za.shape
    tt = _tile(t, 256, 16)
    row = pl.BlockSpec((tt, d), lambda i: (i, 0))

    def body(gm_ref, b_ref, za_ref, zb_ref, o_ref):
        ga = _sigmoid(gm_ref[:, :d] + b_ref[:, :d])
        gb = _sigmoid(gm_ref[:, d:] + b_ref[:, d:])
        o_ref[...] = (ga * za_ref[...] + gb * zb_ref[...]).astype(BF16)

    return _pcall(body, name="merge_fwd", out_shape=jax.ShapeDtypeStruct((t, d), BF16), grid=(t // tt,),
                  in_specs=[pl.BlockSpec((tt, 2 * d), lambda i: (i, 0)), pl.BlockSpec((1, 2 * d), lambda i: (0, 0)), row, row],
                  out_specs=row, dims=("parallel",))(pm, bm, za, zb)


def _merge_bwd(dmix, pm, bm, za, zb):
    t, d = za.shape
    tt = _tile(t, 256, 16)
    row = pl.BlockSpec((tt, d), lambda i: (i, 0))
    wide = pl.BlockSpec((tt, 2 * d), lambda i: (i, 0))
    vec = pl.BlockSpec((1, 2 * d), lambda i: (0, 0))

    def body(dm_ref, gm_ref, b_ref, za_ref, zb_ref, dza_ref, dzb_ref, dgm_ref, db_ref):
        dm = dm_ref[...]
        ga = _sigmoid(gm_ref[:, :d] + b_ref[:, :d])
        gb = _sigmoid(gm_ref[:, d:] + b_ref[:, d:])
        dza_ref[...] = (dm * ga).astype(BF16)
        dzb_ref[...] = (dm * gb).astype(BF16)
        dla = dm * za_ref[...] * ga * (1.0 - ga)
        dlb = dm * zb_ref[...] * gb * (1.0 - gb)
        dgm_ref[:, :d] = dla.astype(BF16)
        dgm_ref[:, d:] = dlb.astype(BF16)
        pa = jnp.sum(dla, axis=0, keepdims=True)
        pb = jnp.sum(dlb, axis=0, keepdims=True)

        @pl.when(pl.program_id(0) == 0)
        def _():
            db_ref[:, :d] = pa
            db_ref[:, d:] = pb

        @pl.when(pl.program_id(0) > 0)
        def _():
            db_ref[:, :d] += pa
            db_ref[:, d:] += pb

    return _pcall(body, name="merge_bwd",
                  out_shape=(jax.ShapeDtypeStruct((t, d), BF16), jax.ShapeDtypeStruct((t, d), BF16),
                             jax.ShapeDtypeStruct((t, 2 * d), BF16), jax.ShapeDtypeStruct((1, 2 * d), F32)),
                  grid=(t // tt,), in_specs=[row, wide, vec, row, row], out_specs=(row, row, wide, vec),
                  dims=("arbitrary",))(dmix, pm, bm, za, zb)


def _ple_final(h3, pgl, pe, gain, target):
    t, d = h3.shape
    tt = _tile(t, 256, 16)
    row = pl.BlockSpec((tt, d), lambda i: (i, 0))
    vec = pl.BlockSpec((1, d), lambda i: (0, 0))
    lvec = pl.BlockSpec((1, LANES), lambda i: (0, 0))

    def body(h_ref, pgl_ref, pe_ref, g_ref, t_ref, dh_ref, dsg_ref, dpe_ref, loss_ref, dg_ref):
        pg = _sigmoid(pgl_ref[...])
        pe_v = pe_ref[...]
        h4 = h_ref[...] + pg * pe_v
        r = lax.rsqrt(jnp.mean(h4 * h4, axis=-1, keepdims=True) + EPS)
        gv = g_ref[...]
        err = h4 * r * gv - t_ref[...]
        part_loss = 0.5 * jnp.sum(jnp.mean(err * err, axis=-1, keepdims=True), axis=0, keepdims=True)
        dy = err * (1.0 / d)
        part_g = jnp.sum(dy * h4 * r, axis=0, keepdims=True)
        dyg = dy * gv
        dh = r * dyg - h4 * (r * r * r) * jnp.mean(dyg * h4, axis=-1, keepdims=True)
        dh_ref[...] = dh
        dsg_ref[...] = (dh * pe_v * pg * (1.0 - pg)).astype(BF16)
        dpe_ref[...] = (dh * pg).astype(BF16)

        @pl.when(pl.program_id(0) == 0)
        def _():
            loss_ref[...] = jnp.broadcast_to(part_loss, (1, LANES))
            dg_ref[...] = part_g

        @pl.when(pl.program_id(0) > 0)
        def _():
            loss_ref[...] += jnp.broadcast_to(part_loss, (1, LANES))
            dg_ref[...] += part_g

    return _pcall(body, name="ple_final",
                  out_shape=(jax.ShapeDtypeStruct((t, d), F32), jax.ShapeDtypeStruct((t, d), BF16),
                             jax.ShapeDtypeStruct((t, d), BF16), jax.ShapeDtypeStruct((1, LANES), F32),
                             jax.ShapeDtypeStruct((1, d), F32)),
                  grid=(t // tt,), in_specs=[row, row, row, vec, row], out_specs=(row, row, row, lvec, vec),
                  dims=("arbitrary",))(h3, pgl, pe, gain, target)


def _adamw_math(w, g, m, v):
    m = ADAM_B1 * m + (1.0 - ADAM_B1) * g
    v = ADAM_B2 * v + (1.0 - ADAM_B2) * (g * g)
    m_hat = m / (1.0 - ADAM_B1 ** ADAM_STEP)
    v_hat = v / (1.0 - ADAM_B2 ** ADAM_STEP)
    delta = -ADAM_LR * (m_hat / (jnp.sqrt(v_hat) + ADAM_EPS) + ADAM_WD * w)
    return delta, m, v


def _adamw(parts, w, m, v, name, tr):
    n, r, c = parts.shape
    tr = _tile(r, tr, 16)
    row = pl.BlockSpec((tr, c), lambda i: (i, 0))

    def body(p_ref, w_ref, m_ref, v_ref, g_ref, d_ref, mo_ref, vo_ref):
        g = p_ref[0].astype(F32)
        for s in range(1, n):
            g = g + p_ref[s].astype(F32)
        g_ref[...] = g
        d_ref[...], mo_ref[...], vo_ref[...] = _adamw_math(w_ref[...], g, m_ref[...], v_ref[...])

    return _pcall(body, name=name, out_shape=(jax.ShapeDtypeStruct((r, c), F32),) * 4, grid=(r // tr,),
                  in_specs=[pl.BlockSpec((n, tr, c), lambda i: (0, i, 0)), row, row, row], out_specs=(row,) * 4,
                  dims=("parallel",))(parts, w, m, v)


ANY = pl.BlockSpec(memory_space=pl.ANY)
VMEM = pl.BlockSpec(memory_space=pltpu.VMEM)


def _all_gather(shard):
    r, c = shard.shape

    def body(x_ref, out_ref, send_sems, recv_sems, local_sem):
        x, y, cc = lax.axis_index("x"), lax.axis_index("y"), lax.axis_index("c")
        me, sibling = (x, y, cc), (x, y, 1 - cc)
        chips = [(1 - x, y), (x, 1 - y), (1 - x, 1 - y)]

        def slot(px, py, pc):
            return out_ref.at[4 * px + 2 * py + pc]

        def copy(k, block, to, src=None):
            return pltpu.make_async_remote_copy(
                src_ref=slot(*block) if src is None else src, dst_ref=slot(*block),
                send_sem=send_sems.at[k], recv_sem=recv_sems.at[k], device_id=to, device_id_type=MESH)

        mine = pltpu.make_async_copy(x_ref, slot(*me), local_sem)
        mine.start()
        first = [copy(0, me, sibling, src=x_ref)]
        first += [copy(1 + j, me, (*chip, cc), src=x_ref) for j, chip in enumerate(chips)]
        for cp in first:
            cp.start()
        passed = [copy(4 + j, (*chip, cc), sibling) for j, chip in enumerate(chips)]
        for j, chip in enumerate(chips):
            copy(1 + j, (*chip, cc), me).wait_recv()
            passed[j].start()
        copy(0, sibling, me).wait_recv()
        for j, chip in enumerate(chips):
            copy(4 + j, (*chip, 1 - cc), me).wait_recv()
        for cp in first + passed:
            cp.wait_send()
        mine.wait()

    return pl.pallas_call(
        body, name="weights_all_gather", out_shape=jax.ShapeDtypeStruct((N_DEV, r, c), shard.dtype),
        in_specs=[ANY], out_specs=ANY,
        scratch_shapes=[pltpu.SemaphoreType.DMA((7,)), pltpu.SemaphoreType.DMA((7,)), pltpu.SemaphoreType.DMA],
    )(shard)


def _reduce_scatter_exchange(blocks, small):
    _, r, c = blocks.shape
    s_rows = small.shape[0]

    def body(g_ref, s_ref, recv_ref, sall_ref, send_sems, recv_sems, local_sems):
        x, y, cc = lax.axis_index("x"), lax.axis_index("y"), lax.axis_index("c")
        me = 4 * x + 2 * y + cc
        own_big = pltpu.make_async_copy(g_ref.at[me], recv_ref.at[me], local_sems.at[0])
        own_small = pltpu.make_async_copy(s_ref, sall_ref.at[me], local_sems.at[1])
        own_big.start()
        own_small.start()
        copies = []
        for k in range(1, N_DEV):
            px, py, pc = x ^ (k >> 2), y ^ ((k >> 1) & 1), cc ^ (k & 1)
            peer = 4 * px + 2 * py + pc
            copies.append(pltpu.make_async_remote_copy(
                src_ref=s_ref, dst_ref=sall_ref.at[me], send_sem=send_sems.at[7 + k - 1], recv_sem=recv_sems.at[7 + k - 1],
                device_id=(px, py, pc), device_id_type=MESH))
            copies.append(pltpu.make_async_remote_copy(
                src_ref=g_ref.at[peer], dst_ref=recv_ref.at[me], send_sem=send_sems.at[k - 1], recv_sem=recv_sems.at[k - 1],
                device_id=(px, py, pc), device_id_type=MESH))
        for cp in copies:
            cp.start()
        for cp in copies:
            cp.wait_recv()
        for cp in copies:
            cp.wait_send()
        own_big.wait()
        own_small.wait()

    return pl.pallas_call(
        body, name="grads_reduce_scatter_exchange",
        out_shape=(jax.ShapeDtypeStruct((N_DEV, r, c), blocks.dtype), jax.ShapeDtypeStruct((N_DEV, s_rows, LANES), F32)),
        in_specs=[ANY, VMEM], out_specs=(ANY, VMEM),
        scratch_shapes=[pltpu.SemaphoreType.DMA((14,)), pltpu.SemaphoreType.DMA((14,)), pltpu.SemaphoreType.DMA((2,))],
    )(blocks, small)


BIG = (("w_ffn1_gate", "col"), ("w_ffn1_up", "col"), ("w_ffn1_down", "row"), ("w_ffn2_gate", "col"),
       ("w_ffn2_up", "col"), ("w_ffn2_down", "row"), ("w_in", "col"), ("w_merge", "col"), ("w_ret_out", "col"),
       ("w_fox_out", "col"), ("w_out", "row"), ("w_ple", "col"), ("w_ple_gate", "row"))
ROW_ALIGN = 16


def _rows_of(shape, d):
    n = shape[0] * shape[1]
    assert n % d == 0
    rows = n // d
    return rows, -(-rows // ROW_ALIGN) * ROW_ALIGN


def _pack_shards(shards, d, dtype):
    parts = []
    for name, _ in BIG:
        s = shards[name]
        rows, padded = _rows_of(s.shape, d)
        p = s.reshape(rows, d).astype(dtype)
        if padded != rows:
            p = jnp.pad(p, ((0, padded - rows), (0, 0)))
        parts.append(p)
    return jnp.concatenate(parts, axis=0)


def _unpack_shards(packed, shapes, d):
    out, at = {}, 0
    for name, _ in BIG:
        rows, padded = _rows_of(shapes[name], d)
        out[name] = packed[at:at + rows].reshape(shapes[name])
        at += padded
    return out


def _full_from_gathered(g, shapes, d):
    out, at = {}, 0
    for name, kind in BIG:
        r, c = shapes[name]
        rows, padded = _rows_of(shapes[name], d)
        blk = g[:, at:at + rows].reshape(N_DEV, r, c)
        out[name] = blk.reshape(N_DEV * r, c) if kind == "row" else blk.transpose(1, 0, 2).reshape(r, N_DEV * c)
        at += padded
    return out


def _blocks_from_full(full, shapes, d, dtype):
    parts = []
    for name, kind in BIG:
        r, c = shapes[name]
        rows, padded = _rows_of(shapes[name], d)
        f = full[name].astype(dtype)
        blk = f.reshape(N_DEV, r, c) if kind == "row" else f.reshape(r, N_DEV, c).transpose(1, 0, 2)
        blk = blk.reshape(N_DEV, rows, d)
        if padded != rows:
            blk = jnp.pad(blk, ((0, 0), (0, padded - rows), (0, 0)))
        parts.append(blk)
    return jnp.concatenate(parts, axis=1)


def _deinterleave(w):
    r = w.shape[0]
    return w.reshape(r, RET_HEADS, RET_DIM // 2, 2).transpose(0, 1, 3, 2).reshape(r, RET_WIDTH)


def _interleave(w):
    r = w.shape[0]
    return w.reshape(r, RET_HEADS, 2, RET_DIM // 2).transpose(0, 1, 3, 2).reshape(r, RET_WIDTH)


SMALL = ("ln_ffn1", "ln_mix", "b_forget", "b_merge", "ln_ffn2", "ln_ple", "ln_final")


def _small_rows(n):
    rows = -(-n // LANES)
    return -(-rows // 8) * 8


def _pack_small(vals, with_loss=None):
    parts = []
    for name in SMALL:
        v = vals[name].reshape(-1).astype(F32)
        rows = _small_rows(v.shape[0])
        parts.append(jnp.pad(v, (0, rows * LANES - v.shape[0])).reshape(rows, LANES))
    if with_loss is not None:
        parts.append(jnp.pad(with_loss.reshape(1, LANES), ((0, 7), (0, 0))))
    else:
        parts.append(jnp.zeros((8, LANES), F32))
    return jnp.concatenate(parts, axis=0)


def _unpack_small(packed, shapes):
    out, at = {}, 0
    for name in SMALL:
        n = int(np.prod(shapes[name]))
        rows = _small_rows(n)
        out[name] = packed[at:at + rows].reshape(-1)[:n].reshape(shapes[name])
        at += rows
    return out, packed[at, 0]


def _heads(a, scale=None):
    t = a.shape[0]
    a = a.reshape(t, FOX_HEADS, FOX_DIM).transpose(1, 0, 2)
    if scale is not None:
        a = a * scale
    return a.astype(BF16)


def _unheads(a):
    return a.transpose(1, 0, 2).reshape(a.shape[1], FOX_WIDTH)


def _local_step(x, p, positions, target, w, small):
    t, d = x.shape
    gain = lambda n: small[n].reshape(1, d)
    wgu1 = jnp.concatenate([w["w_ffn1_gate"], w["w_ffn1_up"]], axis=1)
    wgu2 = jnp.concatenate([w["w_ffn2_gate"], w["w_ffn2_up"]], axis=1)
    w_in = w["w_in"]
    w_in_p = jnp.concatenate([_deinterleave(w_in[:, :RET_WIDTH]), _deinterleave(w_in[:, RET_WIDTH:2 * RET_WIDTH]),
                              w_in[:, 2 * RET_WIDTH:], jnp.zeros((d, IN_PAD - IN_COLS), BF16)], axis=1)
    w_um = jnp.concatenate([w["w_merge"], w_in_p], axis=1)
    off = 2 * d // LANES
    ff_tile = off + (4 * RET_WIDTH + 3 * FOX_WIDTH) // LANES
    bpad = jnp.pad(small["b_forget"].reshape(1, FOX_HEADS), ((0, 0), (0, LANES - FOX_HEADS)))
    bm = small["b_merge"].reshape(1, 2 * d)
    fox_blk = _tile(t, 512, 128)

    n1 = _rms_fwd(x, gain("ln_ffn1"), "rms_ffn1")
    gu1 = _mm(n1, wgu1, "nn", F32, "ffn1_gate_up", tn=512, tk=1024)
    a1 = _swiglu_fwd(gu1, "ffn1_swiglu")
    f1 = _mm(a1, w["w_ffn1_down"], "nn", F32, "ffn1_down")
    h1, u = _rms_fwd(x, gain("ln_mix"), "rms_mix", f=f1)
    pm = _mm(u, w_um, "nn", F32, "mixer_in", tn=640, tk=1024)

    half = jnp.arange(RET_DIM // 2, dtype=F32) / (RET_DIM // 2)
    inv = 1.0 / (ROPE_BASE ** half)
    inv2 = jnp.concatenate([inv, inv]).reshape(1, RET_DIM)
    sign2 = jnp.concatenate([-jnp.ones((RET_DIM // 2,), F32), jnp.ones((RET_DIM // 2,), F32)]).reshape(1, RET_DIM)
    cos2, sin2 = _rope_tables(positions.reshape(t, 1), inv2, sign2)
    consts = _ret_consts()
    y_ret, y_raw, states = _ret_fwd(pm, off, cos2, sin2, consts)
    za = _mm(y_ret, w["w_ret_out"], "nn", F32, "ret_out")

    c_all = _fox_prep(pm, ff_tile, bpad)
    c_heads = c_all[:, :FOX_HEADS].T
    c_col, c_row = c_heads.reshape(FOX_HEADS, t, 1), c_heads.reshape(FOX_HEADS, 1, t)
    fo = 2 * d + 4 * RET_WIDTH
    fq = _heads(pm[:, fo:fo + FOX_WIDTH], FOX_DIM ** -0.5)
    fk = _heads(pm[:, fo + FOX_WIDTH:fo + 2 * FOX_WIDTH])
    fv = _heads(pm[:, fo + 2 * FOX_WIDTH:fo + 3 * FOX_WIDTH])
    o_fox, lse = _fox_fwd(fq, fk, fv, c_col, c_row, fox_blk)
    y_fox = _unheads(o_fox).astype(BF16)
    zb = _mm(y_fox, w["w_fox_out"], "nn", F32, "fox_out")

    mix = _merge_fwd(pm, bm, za, zb)
    mo = _mm(mix, w["w_out"], "nn", F32, "mix_out")
    h2, n2 = _rms_fwd(h1, gain("ln_ffn2"), "rms_ffn2", f=mo, scale=1.0)
    gu2 = _mm(n2, wgu2, "nn", F32, "ffn2_gate_up", tn=512, tk=1024)
    a2 = _swiglu_fwd(gu2, "ffn2_swiglu")
    f2 = _mm(a2, w["w_ffn2_down"], "nn", F32, "ffn2_down")
    h3, n3 = _rms_fwd(h2, gain("ln_ple"), "rms_ple", f=f2)
    pgl = _mm(n3, w["w_ple_gate"], "nn", F32, "ple_gate")
    pb = p.astype(BF16)
    pe = _mm(pb, w["w_ple"], "nn", F32, "ple_embed")

    gw, gs = {}, {}
    dh4, dsg, dpe, loss, gs["ln_final"] = _ple_final(h3, pgl, pe, gain("ln_final"), target)
    gw["w_ple_gate"] = _mm(n3, dsg, "tn", BF16, "d_w_ple_gate")
    gw["w_ple"] = _mm(pb, dpe, "tn", BF16, "d_w_ple")
    dn3 = _mm(dsg, w["w_ple_gate"], "nt", F32, "d_n3")
    dh3, dh3_half, gs["ln_ple"] = _rms_bwd(dn3, h3, gain("ln_ple"), dh4, "rms_ple_bwd", 0.5)

    def ffn_bwd(dh_half, a, gu, n, wd, wgu, tag):
        g_wd = _mm(a, dh_half, "tn", BF16, f"d_w_{tag}_down")
        da = _mm(dh_half, wd, "nt", F32, f"d_a_{tag}")
        dgu = _swiglu_bwd(da, gu, f"{tag}_swiglu_bwd")
        g_wgu = _mm(n, dgu, "tn", BF16, f"d_w_{tag}_gate_up", tn=512)
        dn = _mm(dgu, wgu, "nt", F32, f"d_n_{tag}", tk=512)
        return g_wd, g_wgu, dn

    f_ff = w["w_ffn1_gate"].shape[1]
    gw["w_ffn2_down"], g_wgu2, dn2 = ffn_bwd(dh3_half, a2, gu2, n2, w["w_ffn2_down"], wgu2, "ffn2")
    gw["w_ffn2_gate"], gw["w_ffn2_up"] = g_wgu2[:, :f_ff], g_wgu2[:, f_ff:]
    dh2, dh2_b, gs["ln_ffn2"] = _rms_bwd(dn2, h2, gain("ln_ffn2"), dh3, "rms_ffn2_bwd", 1.0)

    gw["w_out"] = _mm(mix, dh2_b, "tn", BF16, "d_w_out")
    dmix = _mm(dh2_b, w["w_out"], "nt", F32, "d_mix")
    dza, dzb, dgm, gs["b_merge"] = _merge_bwd(dmix, pm, bm, za, zb)
    gw["w_ret_out"] = _mm(y_ret, dza, "tn", BF16, "d_w_ret_out")
    gw["w_fox_out"] = _mm(y_fox, dzb, "tn", BF16, "d_w_fox_out")
    dy_ret = _mm(dza, w["w_ret_out"], "nt", F32, "d_y_ret")
    dy_fox = _mm(dzb, w["w_fox_out"], "nt", F32, "d_y_fox")

    drq, drk, drv, drg = _ret_bwd(dy_ret, pm, off, cos2, sin2, y_raw, states, consts)

    do = dy_fox.reshape(t, FOX_HEADS, FOX_DIM).transpose(1, 0, 2)
    dfk, dfv, dck = _fox_bwd_kv(fq, fk, fv, c_col, c_row, do, o_fox, lse, fox_blk)
    dfq, dcq = _fox_bwd_q(fq, fk, fv, c_col, c_row, do, o_fox, lse, fox_blk, FOX_DIM ** -0.5)
    dc = jnp.pad((dck.reshape(FOX_HEADS, t) + dcq.reshape(FOX_HEADS, t)).T, ((0, 0), (0, LANES - FOX_HEADS)))
    dff, db_forget = _fox_post(dc, pm, ff_tile, bpad)
    gs["b_forget"] = db_forget[:, :FOX_HEADS]

    dpm = jnp.concatenate([dgm, drq, drk, drv, drg, _unheads(dfq), _unheads(dfk), _unheads(dfv), dff], axis=1)
    g_um = _mm(u, dpm, "tn", BF16, "d_w_mixer_in", tn=640)
    du = _mm(dpm, w_um, "nt", F32, "d_u", tk=640)
    gw["w_merge"] = g_um[:, :2 * d]
    g_in = g_um[:, 2 * d:2 * d + IN_COLS]
    gw["w_in"] = jnp.concatenate([_interleave(g_in[:, :RET_WIDTH]), _interleave(g_in[:, RET_WIDTH:2 * RET_WIDTH]),
                                  g_in[:, 2 * RET_WIDTH:]], axis=1)
    dh1, dh1_half, gs["ln_mix"] = _rms_bwd(du, h1, gain("ln_mix"), dh2, "rms_mix_bwd", 0.5)

    gw["w_ffn1_down"], g_wgu1, dn1 = ffn_bwd(dh1_half, a1, gu1, n1, w["w_ffn1_down"], wgu1, "ffn1")
    gw["w_ffn1_gate"], gw["w_ffn1_up"] = g_wgu1[:, :f_ff], g_wgu1[:, f_ff:]
    dx, _, gs["ln_ffn1"] = _rms_bwd(dn1, x, gain("ln_ffn1"), dh1, "rms_ffn1_bwd", 1.0)
    return loss, dx, gw, gs


WEIGHTS = ("ln_ffn1", "w_ffn1_gate", "w_ffn1_up", "w_ffn1_down", "ln_mix", "w_in", "b_forget", "w_merge", "b_merge",
           "w_ret_out", "w_fox_out", "w_out", "ln_ffn2", "w_ffn2_gate", "w_ffn2_up", "w_ffn2_down", "ln_ple", "w_ple",
           "w_ple_gate", "ln_final")


def kernel(x, p, positions, ln_ffn1, w_ffn1_gate, w_ffn1_up, w_ffn1_down, ln_mix, w_in, b_forget, w_merge, b_merge, w_ret_out, w_fox_out, w_out, ln_ffn2, w_ffn2_gate, w_ffn2_up, w_ffn2_down, ln_ple, w_ple, w_ple_gate, ln_final, loss_target, m_ln_ffn1, m_w_ffn1_gate, m_w_ffn1_up, m_w_ffn1_down, m_ln_mix, m_w_in, m_b_forget, m_w_merge, m_b_merge, m_w_ret_out, m_w_fox_out, m_w_out, m_ln_ffn2, m_w_ffn2_gate, m_w_ffn2_up, m_w_ffn2_down, m_ln_ple, m_w_ple, m_w_ple_gate, m_ln_final, v_ln_ffn1, v_w_ffn1_gate, v_w_ffn1_up, v_w_ffn1_down, v_ln_mix, v_w_in, v_b_forget, v_w_merge, v_b_merge, v_w_ret_out, v_w_fox_out, v_w_out, v_ln_ffn2, v_w_ffn2_gate, v_w_ffn2_up, v_w_ffn2_down, v_ln_ple, v_w_ple, v_w_ple_gate, v_ln_final):
    args = dict(ln_ffn1=ln_ffn1, w_ffn1_gate=w_ffn1_gate, w_ffn1_up=w_ffn1_up, w_ffn1_down=w_ffn1_down, ln_mix=ln_mix, w_in=w_in, b_forget=b_forget, w_merge=w_merge, b_merge=b_merge, w_ret_out=w_ret_out, w_fox_out=w_fox_out, w_out=w_out, ln_ffn2=ln_ffn2, w_ffn2_gate=w_ffn2_gate, w_ffn2_up=w_ffn2_up, w_ffn2_down=w_ffn2_down, ln_ple=ln_ple, w_ple=w_ple, w_ple_gate=w_ple_gate, ln_final=ln_final)
    moms = dict(ln_ffn1=m_ln_ffn1, w_ffn1_gate=m_w_ffn1_gate, w_ffn1_up=m_w_ffn1_up, w_ffn1_down=m_w_ffn1_down, ln_mix=m_ln_mix, w_in=m_w_in, b_forget=m_b_forget, w_merge=m_w_merge, b_merge=m_b_merge, w_ret_out=m_w_ret_out, w_fox_out=m_w_fox_out, w_out=m_w_out, ln_ffn2=m_ln_ffn2, w_ffn2_gate=m_w_ffn2_gate, w_ffn2_up=m_w_ffn2_up, w_ffn2_down=m_w_ffn2_down, ln_ple=m_ln_ple, w_ple=m_w_ple, w_ple_gate=m_w_ple_gate, ln_final=m_ln_final)
    vars_ = dict(ln_ffn1=v_ln_ffn1, w_ffn1_gate=v_w_ffn1_gate, w_ffn1_up=v_w_ffn1_up, w_ffn1_down=v_w_ffn1_down, ln_mix=v_ln_mix, w_in=v_w_in, b_forget=v_b_forget, w_merge=v_w_merge, b_merge=v_b_merge, w_ret_out=v_w_ret_out, w_fox_out=v_w_fox_out, w_out=v_w_out, ln_ffn2=v_ln_ffn2, w_ffn2_gate=v_w_ffn2_gate, w_ffn2_up=v_w_ffn2_up, w_ffn2_down=v_w_ffn2_down, ln_ple=v_ln_ple, w_ple=v_w_ple, w_ple_gate=v_w_ple_gate, v_ln_final=v_ln_final)
    vars_["ln_final"] = vars_.pop("v_ln_final")
    d = x.shape[-1]
    big_names = [n for n, _ in BIG]
    shard2d = lambda a: a.reshape(a.shape[-2:])
    w_sh = {n: shard2d(args[n]) for n in big_names}
    m_sh = {n: shard2d(moms[n]) for n in big_names}
    v_sh = {n: shard2d(vars_[n]) for n in big_names}
    shapes = {n: w_sh[n].shape for n in big_names}

    gathered = _all_gather(_pack_shards(w_sh, d, BF16))
    w_full = _full_from_gathered(gathered, shapes, d)

    small = {n: args[n] for n in SMALL}
    loss_part, dx, gw, gs = _local_step(x[0], p[0, 0], positions[0], loss_target[0], w_full, small)

    blocks = _blocks_from_full(gw, shapes, d, BF16)
    small_part = _pack_small(gs, with_loss=loss_part)
    recv, small_all = _reduce_scatter_exchange(blocks, small_part)

    outs = _adamw(recv, _pack_shards(w_sh, d, F32), _pack_shards(m_sh, d, F32), _pack_shards(v_sh, d, F32), "adamw_shards", 112)
    s_outs = _adamw(small_all, _pack_small(small), _pack_small({n: moms[n] for n in SMALL}),
                    _pack_small({n: vars_[n] for n in SMALL}), "adamw_small", 512)
    res = {}
    for kind, big, sm in zip(("grad", "delta", "new_m", "new_v"), outs, s_outs):
        vals = _unpack_shards(big, shapes, d)
        svals, extra = _unpack_small(sm, {n: args[n].shape for n in SMALL})
        if kind == "grad":
            loss = extra
        for n in WEIGHTS:
            res[(kind, n)] = vals[n].reshape(args[n].shape) if n in vals else svals[n]
    return (loss, dx[None], *[res[(kind, n)] for kind in ("grad", "delta", "new_m", "new_v") for n in WEIGHTS])
```

```python
import functools

import numpy as np
import jax
import jax.numpy as jnp
from jax import lax
from jax.experimental import pallas as pl
from jax.experimental.pallas import tpu as pltpu

F32 = jnp.float32
BF16 = jnp.bfloat16

N_DEV = 8
EPS = 1e-6
RET_HEADS = 4
RET_DIM = 128
RET_WIDTH = RET_HEADS * RET_DIM
FOX_HEADS = 8
FOX_DIM = 64
FOX_WIDTH = FOX_HEADS * FOX_DIM
CHUNK = 128
ROPE_BASE = 10000.0
IN_COLS = 4 * RET_WIDTH + 3 * FOX_WIDTH + FOX_HEADS
IN_PAD = 4 * RET_WIDTH + 3 * FOX_WIDTH + 128
LANES = 128
NEG = -1e30

ADAM_LR = 0.001
ADAM_B1 = 0.9
ADAM_B2 = 0.999
ADAM_EPS = 1e-08
ADAM_WD = 0.01
ADAM_STEP = 10

VMEM_LIMIT_BYTES = 56 * 1024 * 1024

MESH = pl.DeviceIdType.MESH


def _tile(dim, pref, mult):
    if dim <= pref:
        return dim
    t = (pref // mult) * mult
    while t >= mult:
        if dim % t == 0:
            return t
        t -= mult
    return dim


def _pcall(body, *, name, out_shape, grid, in_specs, out_specs, scratch_shapes=(), dims=None):
    return pl.pallas_call(
        body,
        name=name,
        out_shape=out_shape,
        grid=grid,
        in_specs=in_specs,
        out_specs=out_specs,
        scratch_shapes=list(scratch_shapes),
        compiler_params=pltpu.CompilerParams(dimension_semantics=dims, vmem_limit_bytes=VMEM_LIMIT_BYTES),
    )


def _dot(a, b, ca, cb):
    return lax.dot_general(a, b, (((ca,), (cb,)), ((), ())), preferred_element_type=F32)


def _sigmoid(x):
    return 1.0 / (1.0 + jnp.exp(-x))


def _mm(a, b, mode, out_dtype, name, tm=1024, tn=1024, tk=8192):
    if mode == "nn":
        (m, k), (k2, n) = a.shape, b.shape
    elif mode == "nt":
        (m, k), (n, k2) = a.shape, b.shape
    else:
        (k, m), (k2, n) = a.shape, b.shape
    assert k == k2, (name, a.shape, b.shape)
    tm = _tile(m, tm, 128 if mode == "tn" else 16)
    tn = _tile(n, tn, 128)
    tk = _tile(k, tk, 128)
    nk = k // tk
    if mode == "tn":
        a_spec = pl.BlockSpec((tk, tm), lambda i, j, kk: (kk, i))
        ca = 0
    else:
        a_spec = pl.BlockSpec((tm, tk), lambda i, j, kk: (i, kk))
        ca = 1
    if mode == "nt":
        b_spec = pl.BlockSpec((tn, tk), lambda i, j, kk: (j, kk))
        cb = 1
    else:
        b_spec = pl.BlockSpec((tk, tn), lambda i, j, kk: (kk, j))
        cb = 0

    if nk == 1:

        def body(a_ref, b_ref, o_ref):
            o_ref[...] = _dot(a_ref[...], b_ref[...], ca, cb).astype(out_dtype)

        scratch = []
    else:

        def body(a_ref, b_ref, o_ref, acc_ref):
            kk = pl.program_id(2)
            part = _dot(a_ref[...], b_ref[...], ca, cb)

            @pl.when(kk == 0)
            def _():
                acc_ref[...] = part

            @pl.when(kk > 0)
            def _():
                acc_ref[...] += part

            @pl.when(kk == nk - 1)
            def _():
                o_ref[...] = acc_ref[...].astype(out_dtype)

        scratch = [pltpu.VMEM((tm, tn), F32)]

    return _pcall(
        body,
        name=name,
        out_shape=jax.ShapeDtypeStruct((m, n), out_dtype),
        grid=(m // tm, n // tn, nk),
        in_specs=[a_spec, b_spec],
        out_specs=pl.BlockSpec((tm, tn), lambda i, j, kk: (i, j)),
        scratch_shapes=scratch,
        dims=("parallel", "parallel", "arbitrary"),
    )(a, b)


def _rms_fwd(h, gain, name, f=None, scale=0.5):
    t, d = h.shape
    tt = _tile(t, 512, 16)
    row = pl.BlockSpec((tt, d), lambda i: (i, 0))
    vec = pl.BlockSpec((1, d), lambda i: (0, 0))

    def norm(hv, g_ref, n_ref):
        r = lax.rsqrt(jnp.mean(hv * hv, axis=-1, keepdims=True) + EPS)
        n_ref[...] = (hv * r * g_ref[...]).astype(BF16)

    if f is None:

        def body(h_ref, g_ref, n_ref):
            norm(h_ref[...], g_ref, n_ref)

        return _pcall(body, name=name, out_shape=jax.ShapeDtypeStruct((t, d), BF16), grid=(t // tt,),
                      in_specs=[row, vec], out_specs=row, dims=("parallel",))(h, gain)

    def body(h_ref, f_ref, g_ref, hn_ref, n_ref):
        hv = h_ref[...] + scale * f_ref[...]
        hn_ref[...] = hv
        norm(hv, g_ref, n_ref)

    return _pcall(body, name=name,
                  out_shape=(jax.ShapeDtypeStruct((t, d), F32), jax.ShapeDtypeStruct((t, d), BF16)),
                  grid=(t // tt,), in_specs=[row, row, vec], out_specs=(row, row), dims=("parallel",))(h, f, gain)


def _rms_bwd(dn, h, gain, dh_in, name, out_scale):
    t, d = h.shape
    tt = _tile(t, 512, 16)
    row = pl.BlockSpec((tt, d), lambda i: (i, 0))
    vec = pl.BlockSpec((1, d), lambda i: (0, 0))

    def body(dn_ref, h_ref, g_ref, dhin_ref, dh_ref, dhb_ref, dg_ref):
        hv = h_ref[...]
        dnv = dn_ref[...].astype(F32)
        r = lax.rsqrt(jnp.mean(hv * hv, axis=-1, keepdims=True) + EPS)
        dng = dnv * g_ref[...]
        dh = dhin_ref[...] + r * dng - hv * (r * r * r) * jnp.mean(dng * hv, axis=-1, keepdims=True)
        dh_ref[...] = dh
        dhb_ref[...] = (out_scale * dh).astype(BF16)
        part = jnp.sum(dnv * hv * r, axis=0, keepdims=True)

        @pl.when(pl.program_id(0) == 0)
        def _():
            dg_ref[...] = part

        @pl.when(pl.program_id(0) > 0)
        def _():
            dg_ref[...] += part

    return _pcall(body, name=name,
                  out_shape=(jax.ShapeDtypeStruct((t, d), F32), jax.ShapeDtypeStruct((t, d), BF16),
                             jax.ShapeDtypeStruct((1, d), F32)),
                  grid=(t // tt,), in_specs=[row, row, vec, row], out_specs=(row, row, vec),
                  dims=("arbitrary",))(dn, h, gain, dh_in)


def _swiglu_fwd(gu, name):
    t, f2 = gu.shape
    f = f2 // 2
    tt = _tile(t, 256, 16)

    def body(gu_ref, a_ref):
        g = gu_ref[:, :f]
        u = gu_ref[:, f:]
        a_ref[...] = (g * _sigmoid(g) * u).astype(BF16)

    return _pcall(body, name=name, out_shape=jax.ShapeDtypeStruct((t, f), BF16), grid=(t // tt,),
                  in_specs=[pl.BlockSpec((tt, f2), lambda i: (i, 0))],
                  out_specs=pl.BlockSpec((tt, f), lambda i: (i, 0)), dims=("parallel",))(gu)


def _swiglu_bwd(da, gu, name):
    t, f2 = gu.shape
    f = f2 // 2
    tt = _tile(t, 256, 16)

    def body(da_ref, gu_ref, d_ref):
        g = gu_ref[:, :f]
        u = gu_ref[:, f:]
        dav = da_ref[...].astype(F32)
        sg = _sigmoid(g)
        d_ref[:, :f] = (dav * u * (sg * (1.0 + g * (1.0 - sg)))).astype(BF16)
        d_ref[:, f:] = (dav * (g * sg)).astype(BF16)

    return _pcall(body, name=name, out_shape=jax.ShapeDtypeStruct((t, f2), BF16), grid=(t // tt,),
                  in_specs=[pl.BlockSpec((tt, f), lambda i: (i, 0)), pl.BlockSpec((tt, f2), lambda i: (i, 0))],
                  out_specs=pl.BlockSpec((tt, f2), lambda i: (i, 0)), dims=("parallel",))(da, gu)


def _rope_tables(pos_col, inv2, sign2):
    t = pos_col.shape[0]

    def body(p_ref, inv_ref, sg_ref, c_ref, s_ref):
        ang = p_ref[...].astype(F32) * inv_ref[...]
        c_ref[...] = jnp.cos(ang)
        s_ref[...] = jnp.sin(ang) * sg_ref[...]

    full = lambda shape: pl.BlockSpec(shape, lambda i: (0, 0))
    return _pcall(body, name="rope_tables", out_shape=(jax.ShapeDtypeStruct((t, RET_DIM), F32),) * 2, grid=(1,),
                  in_specs=[full((t, 1)), full((1, RET_DIM)), full((1, RET_DIM))],
                  out_specs=(full((t, RET_DIM)),) * 2, dims=("arbitrary",))(pos_col, inv2, sign2)


def _rot(x, c, s):
    return x * c + pltpu.roll(x, RET_DIM // 2, 1) * s


def _rot_t(g, c, s):
    return g * c + pltpu.roll(g * s, RET_DIM // 2, 1)


def _ret_consts():
    hh = np.arange(RET_HEADS, dtype=np.float32)
    log_gamma = np.log1p(-np.exp2(-5.0 - hh)).astype(np.float32)
    idx = np.arange(CHUNK, dtype=np.float32)
    diff = idx[:, None] - idx[None, :]
    dmask = np.where(diff >= 0, np.exp(log_gamma[:, None, None] * np.maximum(diff, 0.0)), 0.0).astype(np.float32)
    kdec = np.exp(log_gamma[:, None] * (CHUNK - 1 - idx)).astype(np.float32)
    qdec = np.exp(log_gamma[:, None] * (idx + 1.0)).astype(np.float32)
    cdec = np.exp(log_gamma * CHUNK).astype(np.float32)
    bc = lambda v: np.ascontiguousarray(np.broadcast_to(v[:, :, None], (RET_HEADS, CHUNK, RET_DIM)))
    cd = np.ascontiguousarray(np.broadcast_to(cdec[:, None, None], (RET_HEADS, 8, RET_DIM)))
    return jnp.asarray(dmask), jnp.asarray(bc(qdec)), jnp.asarray(bc(kdec)), jnp.asarray(cd)


def _ret_fwd(pm, off, cos2, sin2, consts):
    t = pm.shape[0]
    n_chunks = t // CHUNK
    dmask, qdec, kdec, cd = consts
    scale = RET_DIM ** -0.5

    def col(c0):
        return pl.BlockSpec((CHUNK, RET_DIM), lambda h, n: (n, off + c0 + h))

    tab = pl.BlockSpec((CHUNK, RET_DIM), lambda h, n: (n, 0))
    head3 = lambda r: pl.BlockSpec((None, r, RET_DIM), lambda h, n: (h, 0, 0))

    def body(q_ref, k_ref, v_ref, g_ref, c_ref, s_ref, dm_ref, qd_ref, kd_ref, cd_ref, y_ref, raw_ref, st_ref, s_acc):
        @pl.when(pl.program_id(1) == 0)
        def _():
            s_acc[...] = jnp.zeros_like(s_acc)

        c, s = c_ref[...], s_ref[...]
        q = _rot(q_ref[...], c, s)
        k = _rot(k_ref[...], c, s) * scale
        vb = v_ref[...].astype(BF16)
        g = g_ref[...]
        s_in = s_acc[...]
        st_ref[...] = s_in
        a = _dot(q.astype(BF16), k.astype(BF16), 1, 1) * dm_ref[...]
        y = _dot(a.astype(BF16), vb, 1, 0) + _dot((q * qd_ref[...]).astype(BF16), s_in.astype(BF16), 1, 0)
        s_acc[...] = cd_ref[0:1, :] * s_in + _dot((k * kd_ref[...]).astype(BF16), vb, 0, 0)
        raw_ref[...] = y
        mu = jnp.mean(y, axis=-1, keepdims=True)
        yc = y - mu
        rs = lax.rsqrt(jnp.mean(yc * yc, axis=-1, keepdims=True) + EPS)
        y_ref[...] = (yc * rs * (g * _sigmoid(g))).astype(BF16)

    out_blk = pl.BlockSpec((CHUNK, RET_DIM), lambda h, n: (n, h))
    return _pcall(
        body, name="retention_fwd",
        out_shape=(jax.ShapeDtypeStruct((t, RET_WIDTH), BF16), jax.ShapeDtypeStruct((t, RET_WIDTH), F32),
                   jax.ShapeDtypeStruct((RET_HEADS, n_chunks, RET_DIM, RET_DIM), F32)),
        grid=(RET_HEADS, n_chunks),
        in_specs=[col(0), col(4), col(8), col(12), tab, tab,
                  pl.BlockSpec((None, CHUNK, CHUNK), lambda h, n: (h, 0, 0)), head3(CHUNK), head3(CHUNK), head3(8)],
        out_specs=(out_blk, out_blk, pl.BlockSpec((None, None, RET_DIM, RET_DIM), lambda h, n: (h, n, 0, 0))),
        scratch_shapes=[pltpu.VMEM((RET_DIM, RET_DIM), F32)],
        dims=("parallel", "arbitrary"),
    )(pm, pm, pm, pm, cos2, sin2, dmask, qdec, kdec, cd)


def _ret_bwd(dy, pm, off, cos2, sin2, raw, states, consts):
    t = pm.shape[0]
    n_chunks = t // CHUNK
    dmask, qdec, kdec, cd = consts
    scale = RET_DIM ** -0.5
    rev = lambda n: n_chunks - 1 - n

    def col(c0):
        return pl.BlockSpec((CHUNK, RET_DIM), lambda h, n: (rev(n), off + c0 + h))

    tab = pl.BlockSpec((CHUNK, RET_DIM), lambda h, n: (rev(n), 0))
    blk = pl.BlockSpec((CHUNK, RET_DIM), lambda h, n: (rev(n), h))
    head3 = lambda r: pl.BlockSpec((None, r, RET_DIM), lambda h, n: (h, 0, 0))

    def body(dy_ref, q_ref, k_ref, v_ref, g_ref, c_ref, s_ref, raw_ref, st_ref, dm_ref, qd_ref, kd_ref, cd_ref,
             dq_ref, dk_ref, dv_ref, dg_ref, ds_acc):
        @pl.when(pl.program_id(1) == 0)
        def _():
            ds_acc[...] = jnp.zeros_like(ds_acc)

        c, s = c_ref[...], s_ref[...]
        q = _rot(q_ref[...], c, s)
        k = _rot(k_ref[...], c, s) * scale
        qb, kb, vb = q.astype(BF16), k.astype(BF16), v_ref[...].astype(BF16)
        g = g_ref[...]
        dm, qd, kd = dm_ref[...], qd_ref[...], kd_ref[...]
        y = raw_ref[...]
        mu = jnp.mean(y, axis=-1, keepdims=True)
        yc = y - mu
        rs = lax.rsqrt(jnp.mean(yc * yc, axis=-1, keepdims=True) + EPS)
        yn = yc * rs
        sg = _sigmoid(g)
        dyo = dy_ref[...]
        dg_ref[...] = (dyo * yn * (sg * (1.0 + g * (1.0 - sg)))).astype(BF16)
        dyn = dyo * (g * sg)
        dyr = rs * (dyn - jnp.mean(dyn, axis=-1, keepdims=True) - yn * jnp.mean(dyn * yn, axis=-1, keepdims=True))
        dyb = dyr.astype(BF16)
        s_in = st_ref[...].astype(BF16)
        ds_out = ds_acc[...]
        dsb = ds_out.astype(BF16)
        a = _dot(qb, kb, 1, 1) * dm
        da = (_dot(dyb, vb, 1, 1) * dm).astype(BF16)
        kdb = (k * kd).astype(BF16)
        qdb = (q * qd).astype(BF16)
        dv_ref[...] = (_dot(a.astype(BF16), dyb, 0, 0) + _dot(kdb, dsb, 1, 0)).astype(BF16)
        dqh = _dot(da, kb, 1, 0) + _dot(dyb, s_in, 1, 1) * qd
        dkh = _dot(da, qb, 0, 0) + _dot(vb, dsb, 1, 1) * kd
        ds_acc[...] = cd_ref[0:1, :] * ds_out + _dot(qdb, dyb, 0, 0)
        dq_ref[...] = _rot_t(dqh, c, s).astype(BF16)
        dk_ref[...] = (_rot_t(dkh, c, s) * scale).astype(BF16)

    return _pcall(
        body, name="retention_bwd",
        out_shape=(jax.ShapeDtypeStruct((t, RET_WIDTH), BF16),) * 4,
        grid=(RET_HEADS, n_chunks),
        in_specs=[blk, col(0), col(4), col(8), col(12), tab, tab, blk,
                  pl.BlockSpec((None, None, RET_DIM, RET_DIM), lambda h, n: (h, rev(n), 0, 0)),
                  pl.BlockSpec((None, CHUNK, CHUNK), lambda h, n: (h, 0, 0)), head3(CHUNK), head3(CHUNK), head3(8)],
        out_specs=(blk,) * 4,
        scratch_shapes=[pltpu.VMEM((RET_DIM, RET_DIM), F32)],
        dims=("parallel", "arbitrary"),
    )(dy, pm, pm, pm, pm, cos2, sin2, raw, states, dmask, qdec, kdec, cd)


def _split3(x):
    hi = x.astype(BF16)
    r1 = x - hi.astype(F32)
    mid = r1.astype(BF16)
    lo = (r1 - mid.astype(F32)).astype(BF16)
    return hi, mid, lo


def _tri_dot(tri, x):
    hi, mid, lo = _split3(x)
    return _dot(tri, lo, 1, 0) + _dot(tri, mid, 1, 0) + _dot(tri, hi, 1, 0)


def _log_sigmoid(z):
    return jnp.minimum(z, 0.0) - jnp.log1p(jnp.exp(-jnp.abs(z)))


def _fox_prep(pm, ff_tile, bpad):
    t = pm.shape[0]
    nb = t // LANES

    def body(ff_ref, b_ref, c_ref):
        r = lax.broadcasted_iota(jnp.int32, (LANES, LANES), 0)
        cc = lax.broadcasted_iota(jnp.int32, (LANES, LANES), 1)
        tri = jnp.where(cc <= r, 1.0, 0.0).astype(BF16)
        bias = b_ref[...]

        lane = lax.broadcasted_iota(jnp.int32, (LANES, LANES), 1)

        def step(b, carry):
            rows = pl.ds(pl.multiple_of(b * LANES, LANES), LANES)
            cs = _tri_dot(tri, _log_sigmoid(ff_ref[rows, :] + bias)) + carry
            hi, mid, lo = (part.astype(F32) for part in _split3(cs))
            c_ref[rows, :] = jnp.where(
                lane < FOX_HEADS, hi,
                jnp.where(lane < 2 * FOX_HEADS, pltpu.roll(mid, FOX_HEADS, 1),
                          jnp.where(lane < 3 * FOX_HEADS, pltpu.roll(lo, 2 * FOX_HEADS, 1), 0.0)))
            return cs[LANES - 1:LANES, :]

        lax.fori_loop(0, nb, step, jnp.zeros((1, LANES), F32))

    return _pcall(body, name="fox_cumsum", out_shape=jax.ShapeDtypeStruct((t, LANES), F32), grid=(1,),
                  in_specs=[pl.BlockSpec((t, LANES), lambda i: (0, ff_tile)), pl.BlockSpec((1, LANES), lambda i: (0, 0))],
                  out_specs=pl.BlockSpec((t, LANES), lambda i: (0, 0)), dims=("arbitrary",))(pm, bpad)


def _fox_post(dc, pm, ff_tile, bpad):
    t = pm.shape[0]
    nb = t // LANES

    def body(dc_ref, ff_ref, b_ref, d_ref, db_ref):
        r = lax.broadcasted_iota(jnp.int32, (LANES, LANES), 0)
        cc = lax.broadcasted_iota(jnp.int32, (LANES, LANES), 1)
        tri = jnp.where(cc >= r, 1.0, 0.0).astype(BF16)
        bias = b_ref[...]

        def step(i, carry):
            tail, acc = carry
            rows = pl.ds(pl.multiple_of((nb - 1 - i) * LANES, LANES), LANES)
            cs = _tri_dot(tri, dc_ref[rows, :]) + tail
            dff = cs * _sigmoid(-(ff_ref[rows, :] + bias))
            d_ref[rows, :] = dff.astype(BF16)
            return cs[0:1, :], acc + jnp.sum(dff, axis=0, keepdims=True)

        zero = jnp.zeros((1, LANES), F32)
        _, acc = lax.fori_loop(0, nb, step, (zero, zero))
        db_ref[...] = acc

    return _pcall(body, name="fox_forget_bwd",
                  out_shape=(jax.ShapeDtypeStruct((t, LANES), BF16), jax.ShapeDtypeStruct((1, LANES), F32)), grid=(1,),
                  in_specs=[pl.BlockSpec((t, LANES), lambda i: (0, 0)), pl.BlockSpec((t, LANES), lambda i: (0, ff_tile)),
                            pl.BlockSpec((1, LANES), lambda i: (0, 0))],
                  out_specs=(pl.BlockSpec((t, LANES), lambda i: (0, 0)), pl.BlockSpec((1, LANES), lambda i: (0, 0))),
                  dims=("arbitrary",))(dc, pm, bpad)


FOX_L_LANE = FOX_DIM
FOX_LSE_LANE = FOX_DIM + 1
FOX_ROWSUM_LANE = FOX_DIM
FOX_COLSUM_LANE = FOX_DIM + 3


def _tri_tables(nb, q_major):
    pairs = [(i, j) for i in range(nb) for j in range(i + 1)] if q_major else \
            [(i, j) for j in range(nb) for i in range(j, nb)]
    return jnp.asarray([a for a, _ in pairs], jnp.int32), jnp.asarray([b for _, b in pairs], jnp.int32)


def _causal(s):
    n = s.shape[0]
    row = lax.broadcasted_iota(jnp.int32, (n, n), 0)
    col = lax.broadcasted_iota(jnp.int32, (n, n), 1)
    return jnp.where(col <= row, s, NEG)


def _lane_col(x, lane):
    sel = lax.broadcasted_iota(jnp.int32, x.shape, 1) == lane
    return jnp.sum(jnp.where(sel, x, 0.0), axis=1, keepdims=True)


def _fox_fwd(qa, ka, va, blk):
    t = qa.shape[0]
    nb = t // blk
    qi, kj = _tri_tables(nb, True)
    q_spec = pl.BlockSpec((blk, LANES), lambda h, s, qi_r, kj_r: (qi_r[s], h))
    k_spec = pl.BlockSpec((blk, LANES), lambda h, s, qi_r, kj_r: (kj_r[s], h))

    def body(qi_r, kj_r, q_ref, k_ref, v_ref, o_ref, m_s, acc_s):
        s_id = pl.program_id(1)
        i, j = qi_r[s_id], kj_r[s_id]

        @pl.when(j == 0)
        def _():
            m_s[...] = jnp.full_like(m_s, NEG)
            acc_s[...] = jnp.zeros_like(acc_s)

        def tile(diagonal):
            s = _dot(q_ref[...], k_ref[...], 1, 1)
            if diagonal:
                s = _causal(s)
            m_old = m_s[...]
            m_new = jnp.maximum(m_old, jnp.max(s, axis=1, keepdims=True))
            p = jnp.exp(s - jnp.tile(m_new, (1, blk // LANES)))
            acc_s[...] = jnp.exp(m_old - m_new) * acc_s[...] + _dot(p.astype(BF16), v_ref[...], 1, 0)
            m_s[...] = m_new

        @pl.when(j < i)
        def _():
            tile(False)

        @pl.when(j == i)
        def _():
            tile(True)
            acc = acc_s[...]
            l = _lane_col(acc, FOX_L_LANE)
            hi, mid, lo = (part.astype(F32) for part in _split3(-(m_s[:, 0:1] + jnp.log(l))))
            lane = lax.broadcasted_iota(jnp.int32, acc.shape, 1)
            o_ref[...] = jnp.where(lane == FOX_LSE_LANE, hi,
                                   jnp.where(lane == FOX_LSE_LANE + 1, mid,
                                             jnp.where(lane == FOX_LSE_LANE + 2, lo, acc / l)))

    return pl.pallas_call(
        body, name="fox_fwd", out_shape=jax.ShapeDtypeStruct((t, FOX_HEADS * LANES), F32),
        grid_spec=pltpu.PrefetchScalarGridSpec(
            num_scalar_prefetch=2, grid=(FOX_HEADS, qi.shape[0]), in_specs=[q_spec, k_spec, k_spec], out_specs=q_spec,
            scratch_shapes=[pltpu.VMEM((blk, LANES), F32), pltpu.VMEM((blk, LANES), F32)]),
        compiler_params=pltpu.CompilerParams(dimension_semantics=("parallel", "arbitrary"),
                                             vmem_limit_bytes=VMEM_LIMIT_BYTES),
    )(qi, kj, qa, ka, va)


def _fox_bwd(qa, ka, va, do, o, blk):
    t = qa.shape[0]
    nb = t // blk
    qi, kj = _tri_tables(nb, False)
    q_spec = pl.BlockSpec((blk, LANES), lambda h, s, qi_r, kj_r: (qi_r[s], h))
    k_spec = pl.BlockSpec((blk, LANES), lambda h, s, qi_r, kj_r: (kj_r[s], h))
    head_spec = pl.BlockSpec((t, LANES), lambda h, s, qi_r, kj_r: (0, h))
    first_spec = pl.BlockSpec((blk, LANES), lambda h, s, qi_r, kj_r: (jnp.where(kj_r[s] == 0, qi_r[s], nb - 1), h))

    def body(qi_r, kj_r, q_ref, k_ref, v_ref, do_ref, o_ref, dq_ref, dk_ref, dv_ref, doa_s, dk_s, dv_s):
        s_id = pl.program_id(1)
        i, j = qi_r[s_id], kj_r[s_id]
        rows = pl.ds(pl.multiple_of(i * blk, blk), blk)

        @pl.when(j == 0)
        def _():
            dof = do_ref[...]
            hi, mid, lo = _split3(-jnp.sum(dof * o_ref[...], axis=1, keepdims=True))
            lane = lax.broadcasted_iota(jnp.int32, dof.shape, 1)
            doa = jnp.where(lane == FOX_DIM, hi.astype(F32),
                            jnp.where(lane == FOX_DIM + 1, mid.astype(F32),
                                      jnp.where(lane == FOX_DIM + 2, lo.astype(F32), dof)))
            doa_s[rows, :] = doa.astype(BF16)
            dq_ref[rows, :] = jnp.zeros((blk, LANES), F32)

        @pl.when(i == j)
        def _():
            dk_s[...] = jnp.zeros_like(dk_s)
            dv_s[...] = jnp.zeros_like(dv_s)

        def tile(diagonal):
            q, k = q_ref[...], k_ref[...]
            s = _dot(q, k, 1, 1)
            if diagonal:
                s = _causal(s)
            p = jnp.exp(s)
            doa = doa_s[rows, :]
            ds = (p * _dot(doa, v_ref[...], 1, 1)).astype(BF16)
            dv_s[...] += _dot(p.astype(BF16), doa, 0, 0)
            dk_s[...] += _dot(ds, q, 0, 0)
            dq_ref[rows, :] += _dot(ds, k, 1, 0)

        @pl.when(i > j)
        def _():
            tile(False)

        @pl.when(i == j)
        def _():
            tile(True)

        @pl.when(i == nb - 1)
        def _():
            dk_ref[...] = dk_s[...]
            dv_ref[...] = dv_s[...].astype(BF16)

    wide = (t, FOX_HEADS * LANES)
    return pl.pallas_call(
        body, name="fox_bwd",
        out_shape=(jax.ShapeDtypeStruct(wide, F32), jax.ShapeDtypeStruct(wide, F32), jax.ShapeDtypeStruct(wide, BF16)),
        grid_spec=pltpu.PrefetchScalarGridSpec(
            num_scalar_prefetch=2, grid=(FOX_HEADS, qi.shape[0]),
            in_specs=[q_spec, k_spec, k_spec, first_spec, first_spec], out_specs=(head_spec, k_spec, k_spec),
            scratch_shapes=[pltpu.VMEM((t, LANES), BF16), pltpu.VMEM((blk, LANES), F32), pltpu.VMEM((blk, LANES), F32)]),
        compiler_params=pltpu.CompilerParams(dimension_semantics=("parallel", "arbitrary"),
                                             vmem_limit_bytes=VMEM_LIMIT_BYTES),
    )(qi, kj, qa, ka, va, do, o)


def _merge_fwd(pm, bm, za, zb):
    t, d = za.shape
    tt = _tile(t, 256, 16)
    row = pl.BlockSpec((tt, d), lambda i: (i, 0))

    def body(gm_ref, b_ref, za_ref, zb_ref, o_ref):
        ga = _sigmoid(gm_ref[:, :d] + b_ref[:, :d])
        gb = _sigmoid(gm_ref[:, d:] + b_ref[:, d:])
        o_ref[...] = (ga * za_ref[...] + gb * zb_ref[...]).astype(BF16)

    return _pcall(body, name="merge_fwd", out_shape=jax.ShapeDtypeStruct((t, d), BF16), grid=(t // tt,),
                  in_specs=[pl.BlockSpec((tt, 2 * d), lambda i: (i, 0)), pl.BlockSpec((1, 2 * d), lambda i: (0, 0)), row, row],
                  out_specs=row, dims=("parallel",))(pm, bm, za, zb)


def _merge_bwd(dmix, pm, bm, za, zb):
    t, d = za.shape
    tt = _tile(t, 256, 16)
    row = pl.BlockSpec((tt, d), lambda i: (i, 0))
    wide = pl.BlockSpec((tt, 2 * d), lambda i: (i, 0))
    vec = pl.BlockSpec((1, 2 * d), lambda i: (0, 0))

    def body(dm_ref, gm_ref, b_ref, za_ref, zb_ref, dza_ref, dzb_ref, dgm_ref, db_ref):
        dm = dm_ref[...]
        ga = _sigmoid(gm_ref[:, :d] + b_ref[:, :d])
        gb = _sigmoid(gm_ref[:, d:] + b_ref[:, d:])
        dza_ref[...] = (dm * ga).astype(BF16)
        dzb_ref[...] = (dm * gb).astype(BF16)
        dla = dm * za_ref[...] * ga * (1.0 - ga)
        dlb = dm * zb_ref[...] * gb * (1.0 - gb)
        dgm_ref[:, :d] = dla.astype(BF16)
        dgm_ref[:, d:] = dlb.astype(BF16)
        pa = jnp.sum(dla, axis=0, keepdims=True)
        pb = jnp.sum(dlb, axis=0, keepdims=True)

        @pl.when(pl.program_id(0) == 0)
        def _():
            db_ref[:, :d] = pa
            db_ref[:, d:] = pb

        @pl.when(pl.program_id(0) > 0)
        def _():
            db_ref[:, :d] += pa
            db_ref[:, d:] += pb

    return _pcall(body, name="merge_bwd",
                  out_shape=(jax.ShapeDtypeStruct((t, d), BF16), jax.ShapeDtypeStruct((t, d), BF16),
                             jax.ShapeDtypeStruct((t, 2 * d), BF16), jax.ShapeDtypeStruct((1, 2 * d), F32)),
                  grid=(t // tt,), in_specs=[row, wide, vec, row, row], out_specs=(row, row, wide, vec),
                  dims=("arbitrary",))(dmix, pm, bm, za, zb)


def _ple_final(h3, pgl, pe, gain, target):
    t, d = h3.shape
    tt = _tile(t, 256, 16)
    row = pl.BlockSpec((tt, d), lambda i: (i, 0))
    vec = pl.BlockSpec((1, d), lambda i: (0, 0))
    lvec = pl.BlockSpec((1, LANES), lambda i: (0, 0))

    def body(h_ref, pgl_ref, pe_ref, g_ref, t_ref, dh_ref, dsg_ref, dpe_ref, loss_ref, dg_ref):
        pg = _sigmoid(pgl_ref[...])
        pe_v = pe_ref[...]
        h4 = h_ref[...] + pg * pe_v
        r = lax.rsqrt(jnp.mean(h4 * h4, axis=-1, keepdims=True) + EPS)
        gv = g_ref[...]
        err = h4 * r * gv - t_ref[...]
        part_loss = 0.5 * jnp.sum(jnp.mean(err * err, axis=-1, keepdims=True), axis=0, keepdims=True)
        dy = err * (1.0 / d)
        part_g = jnp.sum(dy * h4 * r, axis=0, keepdims=True)
        dyg = dy * gv
        dh = r * dyg - h4 * (r * r * r) * jnp.mean(dyg * h4, axis=-1, keepdims=True)
        dh_ref[...] = dh
        dsg_ref[...] = (dh * pe_v * pg * (1.0 - pg)).astype(BF16)
        dpe_ref[...] = (dh * pg).astype(BF16)

        @pl.when(pl.program_id(0) == 0)
        def _():
            loss_ref[...] = jnp.broadcast_to(part_loss, (1, LANES))
            dg_ref[...] = part_g

        @pl.when(pl.program_id(0) > 0)
        def _():
            loss_ref[...] += jnp.broadcast_to(part_loss, (1, LANES))
            dg_ref[...] += part_g

    return _pcall(body, name="ple_final",
                  out_shape=(jax.ShapeDtypeStruct((t, d), F32), jax.ShapeDtypeStruct((t, d), BF16),
                             jax.ShapeDtypeStruct((t, d), BF16), jax.ShapeDtypeStruct((1, LANES), F32),
                             jax.ShapeDtypeStruct((1, d), F32)),
                  grid=(t // tt,), in_specs=[row, row, row, vec, row], out_specs=(row, row, row, lvec, vec),
                  dims=("arbitrary",))(h3, pgl, pe, gain, target)


def _adamw_math(w, g, m, v):
    m = ADAM_B1 * m + (1.0 - ADAM_B1) * g
    v = ADAM_B2 * v + (1.0 - ADAM_B2) * (g * g)
    m_hat = m / (1.0 - ADAM_B1 ** ADAM_STEP)
    v_hat = v / (1.0 - ADAM_B2 ** ADAM_STEP)
    delta = -ADAM_LR * (m_hat / (jnp.sqrt(v_hat) + ADAM_EPS) + ADAM_WD * w)
    return delta, m, v


def _adamw(parts, w, m, v, name, tr):
    n, r, c = parts.shape
    tr = _tile(r, tr, 16)
    row = pl.BlockSpec((tr, c), lambda i: (i, 0))

    def body(p_ref, w_ref, m_ref, v_ref, g_ref, d_ref, mo_ref, vo_ref):
        g = p_ref[0].astype(F32)
        for s in range(1, n):
            g = g + p_ref[s].astype(F32)
        g_ref[...] = g
        d_ref[...], mo_ref[...], vo_ref[...] = _adamw_math(w_ref[...], g, m_ref[...], v_ref[...])

    return _pcall(body, name=name, out_shape=(jax.ShapeDtypeStruct((r, c), F32),) * 4, grid=(r // tr,),
                  in_specs=[pl.BlockSpec((n, tr, c), lambda i: (0, i, 0)), row, row, row], out_specs=(row,) * 4,
                  dims=("parallel",))(parts, w, m, v)


ANY = pl.BlockSpec(memory_space=pl.ANY)
VMEM = pl.BlockSpec(memory_space=pltpu.VMEM)


def _all_gather(shard):
    r, c = shard.shape

    def body(x_ref, out_ref, send_sems, recv_sems, local_sem):
        x, y, cc = lax.axis_index("x"), lax.axis_index("y"), lax.axis_index("c")
        me, sibling = (x, y, cc), (x, y, 1 - cc)
        chips = [(1 - x, y), (x, 1 - y), (1 - x, 1 - y)]

        def slot(px, py, pc):
            return out_ref.at[4 * px + 2 * py + pc]

        def copy(k, block, to, src=None):
            return pltpu.make_async_remote_copy(
                src_ref=slot(*block) if src is None else src, dst_ref=slot(*block),
                send_sem=send_sems.at[k], recv_sem=recv_sems.at[k], device_id=to, device_id_type=MESH)

        mine = pltpu.make_async_copy(x_ref, slot(*me), local_sem)
        mine.start()
        first = [copy(0, me, sibling, src=x_ref)]
        first += [copy(1 + j, me, (*chip, cc), src=x_ref) for j, chip in enumerate(chips)]
        for cp in first:
            cp.start()
        passed = [copy(4 + j, (*chip, cc), sibling) for j, chip in enumerate(chips)]
        for j, chip in enumerate(chips):
            copy(1 + j, (*chip, cc), me).wait_recv()
            passed[j].start()
        copy(0, sibling, me).wait_recv()
        for j, chip in enumerate(chips):
            copy(4 + j, (*chip, 1 - cc), me).wait_recv()
        for cp in first + passed:
            cp.wait_send()
        mine.wait()

    return pl.pallas_call(
        body, name="weights_all_gather", out_shape=jax.ShapeDtypeStruct((N_DEV, r, c), shard.dtype),
        in_specs=[ANY], out_specs=ANY,
        scratch_shapes=[pltpu.SemaphoreType.DMA((7,)), pltpu.SemaphoreType.DMA((7,)), pltpu.SemaphoreType.DMA],
    )(shard)


def _reduce_scatter_exchange(blocks, small):
    _, r, c = blocks.shape
    s_rows = small.shape[0]

    def body(g_ref, s_ref, recv_ref, sall_ref, send_sems, recv_sems, local_sems):
        x, y, cc = lax.axis_index("x"), lax.axis_index("y"), lax.axis_index("c")
        me = 4 * x + 2 * y + cc
        own_big = pltpu.make_async_copy(g_ref.at[me], recv_ref.at[me], local_sems.at[0])
        own_small = pltpu.make_async_copy(s_ref, sall_ref.at[me], local_sems.at[1])
        own_big.start()
        own_small.start()
        copies = []
        for k in range(1, N_DEV):
            px, py, pc = x ^ (k >> 2), y ^ ((k >> 1) & 1), cc ^ (k & 1)
            peer = 4 * px + 2 * py + pc
            copies.append(pltpu.make_async_remote_copy(
                src_ref=s_ref, dst_ref=sall_ref.at[me], send_sem=send_sems.at[7 + k - 1], recv_sem=recv_sems.at[7 + k - 1],
                device_id=(px, py, pc), device_id_type=MESH))
            copies.append(pltpu.make_async_remote_copy(
                src_ref=g_ref.at[peer], dst_ref=recv_ref.at[me], send_sem=send_sems.at[k - 1], recv_sem=recv_sems.at[k - 1],
                device_id=(px, py, pc), device_id_type=MESH))
        for cp in copies:
            cp.start()
        for cp in copies:
            cp.wait_recv()
        for cp in copies:
            cp.wait_send()
        own_big.wait()
        own_small.wait()

    return pl.pallas_call(
        body, name="grads_reduce_scatter_exchange",
        out_shape=(jax.ShapeDtypeStruct((N_DEV, r, c), blocks.dtype), jax.ShapeDtypeStruct((N_DEV, s_rows, LANES), F32)),
        in_specs=[ANY, VMEM], out_specs=(ANY, VMEM),
        scratch_shapes=[pltpu.SemaphoreType.DMA((14,)), pltpu.SemaphoreType.DMA((14,)), pltpu.SemaphoreType.DMA((2,))],
    )(blocks, small)


BIG = (("w_ffn1_gate", "col"), ("w_ffn1_up", "col"), ("w_ffn1_down", "row"), ("w_ffn2_gate", "col"),
       ("w_ffn2_up", "col"), ("w_ffn2_down", "row"), ("w_in", "col"), ("w_merge", "col"), ("w_ret_out", "col"),
       ("w_fox_out", "col"), ("w_out", "row"), ("w_ple", "col"), ("w_ple_gate", "row"))
ROW_ALIGN = 16


def _rows_of(shape, d):
    n = shape[0] * shape[1]
    assert n % d == 0
    rows = n // d
    return rows, -(-rows // ROW_ALIGN) * ROW_ALIGN


def _pack_shards(shards, d, dtype):
    parts = []
    for name, _ in BIG:
        s = shards[name]
        rows, padded = _rows_of(s.shape, d)
        p = s.reshape(rows, d).astype(dtype)
        if padded != rows:
            p = jnp.pad(p, ((0, padded - rows), (0, 0)))
        parts.append(p)
    return jnp.concatenate(parts, axis=0)


def _unpack_shards(packed, shapes, d):
    out, at = {}, 0
    for name, _ in BIG:
        rows, padded = _rows_of(shapes[name], d)
        out[name] = packed[at:at + rows].reshape(shapes[name])
        at += padded
    return out


def _full_from_gathered(g, shapes, d):
    out, at = {}, 0
    for name, kind in BIG:
        r, c = shapes[name]
        rows, padded = _rows_of(shapes[name], d)
        blk = g[:, at:at + rows].reshape(N_DEV, r, c)
        out[name] = blk.reshape(N_DEV * r, c) if kind == "row" else blk.transpose(1, 0, 2).reshape(r, N_DEV * c)
        at += padded
    return out


def _blocks_from_full(full, shapes, d, dtype):
    parts = []
    for name, kind in BIG:
        r, c = shapes[name]
        rows, padded = _rows_of(shapes[name], d)
        f = full[name].astype(dtype)
        blk = f.reshape(N_DEV, r, c) if kind == "row" else f.reshape(r, N_DEV, c).transpose(1, 0, 2)
        blk = blk.reshape(N_DEV, rows, d)
        if padded != rows:
            blk = jnp.pad(blk, ((0, 0), (0, padded - rows), (0, 0)))
        parts.append(blk)
    return jnp.concatenate(parts, axis=1)


def _deinterleave(w):
    r = w.shape[0]
    return w.reshape(r, RET_HEADS, RET_DIM // 2, 2).transpose(0, 1, 3, 2).reshape(r, RET_WIDTH)


def _interleave(w):
    r = w.shape[0]
    return w.reshape(r, RET_HEADS, 2, RET_DIM // 2).transpose(0, 1, 3, 2).reshape(r, RET_WIDTH)


SMALL = ("ln_ffn1", "ln_mix", "b_forget", "b_merge", "ln_ffn2", "ln_ple", "ln_final")


def _small_rows(n):
    rows = -(-n // LANES)
    return -(-rows // 8) * 8


def _pack_small(vals, with_loss=None):
    parts = []
    for name in SMALL:
        v = vals[name].reshape(-1).astype(F32)
        rows = _small_rows(v.shape[0])
        parts.append(jnp.pad(v, (0, rows * LANES - v.shape[0])).reshape(rows, LANES))
    if with_loss is not None:
        parts.append(jnp.pad(with_loss.reshape(1, LANES), ((0, 7), (0, 0))))
    else:
        parts.append(jnp.zeros((8, LANES), F32))
    return jnp.concatenate(parts, axis=0)


def _unpack_small(packed, shapes):
    out, at = {}, 0
    for name in SMALL:
        n = int(np.prod(shapes[name]))
        rows = _small_rows(n)
        out[name] = packed[at:at + rows].reshape(-1)[:n].reshape(shapes[name])
        at += rows
    return out, packed[at, 0]


def _local_step(x, p, positions, target, w, small):
    t, d = x.shape
    gain = lambda n: small[n].reshape(1, d)
    wgu1 = jnp.concatenate([w["w_ffn1_gate"], w["w_ffn1_up"]], axis=1)
    wgu2 = jnp.concatenate([w["w_ffn2_gate"], w["w_ffn2_up"]], axis=1)
    w_in = w["w_in"]
    w_in_p = jnp.concatenate([_deinterleave(w_in[:, :RET_WIDTH]), _deinterleave(w_in[:, RET_WIDTH:2 * RET_WIDTH]),
                              w_in[:, 2 * RET_WIDTH:], jnp.zeros((d, IN_PAD - IN_COLS), BF16)], axis=1)
    w_um = jnp.concatenate([w["w_merge"], w_in_p], axis=1)
    off = 2 * d // LANES
    ff_tile = off + (4 * RET_WIDTH + 3 * FOX_WIDTH) // LANES
    bpad = jnp.pad(small["b_forget"].reshape(1, FOX_HEADS), ((0, 0), (0, LANES - FOX_HEADS)))
    bm = small["b_merge"].reshape(1, 2 * d)
    fox_blk = _tile(t, 512, 128)

    n1 = _rms_fwd(x, gain("ln_ffn1"), "rms_ffn1")
    gu1 = _mm(n1, wgu1, "nn", F32, "ffn1_gate_up", tn=1408)
    a1 = _swiglu_fwd(gu1, "ffn1_swiglu")
    f1 = _mm(a1, w["w_ffn1_down"], "nn", F32, "ffn1_down")
    h1, u = _rms_fwd(x, gain("ln_mix"), "rms_mix", f=f1)
    pm = _mm(u, w_um, "nn", F32, "mixer_in", tn=1920)

    half = jnp.arange(RET_DIM // 2, dtype=F32) / (RET_DIM // 2)
    inv = 1.0 / (ROPE_BASE ** half)
    inv2 = jnp.concatenate([inv, inv]).reshape(1, RET_DIM)
    sign2 = jnp.concatenate([-jnp.ones((RET_DIM // 2,), F32), jnp.ones((RET_DIM // 2,), F32)]).reshape(1, RET_DIM)
    cos2, sin2 = _rope_tables(positions.reshape(t, 1), inv2, sign2)
    consts = _ret_consts()
    y_ret, y_raw, states = _ret_fwd(pm, off, cos2, sin2, consts)
    za = _mm(y_ret, w["w_ret_out"], "nn", F32, "ret_out")

    c_all = _fox_prep(pm, ff_tile, bpad)
    c_parts = c_all[:, :3 * FOX_HEADS].reshape(t, 3, FOX_HEADS).transpose(0, 2, 1).astype(BF16)
    fo = 2 * d + 4 * RET_WIDTH
    head3 = lambda a: a.reshape(t, FOX_HEADS, -1)
    ones3 = jnp.ones((t, FOX_HEADS, 3), BF16)
    fox_tiles = lambda *cols: jnp.concatenate(
        cols + (jnp.zeros((t, FOX_HEADS, LANES - sum(c.shape[-1] for c in cols)), BF16),), axis=-1
    ).reshape(t, FOX_HEADS * LANES)
    q3 = (head3(pm[:, fo:fo + FOX_WIDTH]) * FOX_DIM ** -0.5).astype(BF16)
    k3 = head3(pm[:, fo + FOX_WIDTH:fo + 2 * FOX_WIDTH]).astype(BF16)
    v3 = head3(pm[:, fo + 2 * FOX_WIDTH:fo + 3 * FOX_WIDTH]).astype(BF16)
    qa = fox_tiles(q3, c_parts, ones3)
    ka = fox_tiles(k3, ones3, -c_parts, ones3)
    va = fox_tiles(v3, ones3)
    o_fox = _fox_fwd(qa, ka, va, fox_blk)
    y_fox = o_fox.astype(BF16)
    w_fox_pad = jnp.pad(w["w_fox_out"].reshape(FOX_HEADS, FOX_DIM, d), ((0, 0), (0, LANES - FOX_DIM), (0, 0)))
    w_fox_pad = w_fox_pad.reshape(FOX_HEADS * LANES, d)
    zb = _mm(y_fox, w_fox_pad, "nn", F32, "fox_out")

    mix = _merge_fwd(pm, bm, za, zb)
    mo = _mm(mix, w["w_out"], "nn", F32, "mix_out")
    h2, n2 = _rms_fwd(h1, gain("ln_ffn2"), "rms_ffn2", f=mo, scale=1.0)
    gu2 = _mm(n2, wgu2, "nn", F32, "ffn2_gate_up", tn=1408)
    a2 = _swiglu_fwd(gu2, "ffn2_swiglu")
    f2 = _mm(a2, w["w_ffn2_down"], "nn", F32, "ffn2_down")
    h3, n3 = _rms_fwd(h2, gain("ln_ple"), "rms_ple", f=f2)
    pgl = _mm(n3, w["w_ple_gate"], "nn", F32, "ple_gate")
    pb = p.astype(BF16)
    pe = _mm(pb, w["w_ple"], "nn", F32, "ple_embed")

    gw, gs = {}, {}
    dh4, dsg, dpe, loss, gs["ln_final"] = _ple_final(h3, pgl, pe, gain("ln_final"), target)
    gw["w_ple_gate"] = _mm(n3, dsg, "tn", BF16, "d_w_ple_gate", tn=256)
    gw["w_ple"] = _mm(pb, dpe, "tn", BF16, "d_w_ple", tn=256)
    dn3 = _mm(dsg, w["w_ple_gate"], "nt", F32, "d_n3")
    dh3, dh3_half, gs["ln_ple"] = _rms_bwd(dn3, h3, gain("ln_ple"), dh4, "rms_ple_bwd", 0.5)

    def ffn_bwd(dh_half, a, gu, n, wd, wgu, tag):
        g_wd = _mm(a, dh_half, "tn", BF16, f"d_w_{tag}_down", tm=1408, tn=256)
        da = _mm(dh_half, wd, "nt", F32, f"d_a_{tag}", tn=1408)
        dgu = _swiglu_bwd(da, gu, f"{tag}_swiglu_bwd")
        g_wgu = _mm(n, dgu, "tn", BF16, f"d_w_{tag}_gate_up", tn=512)
        dn = _mm(dgu, wgu, "nt", F32, f"d_n_{tag}", tm=512)
        return g_wd, g_wgu, dn

    f_ff = w["w_ffn1_gate"].shape[1]
    gw["w_ffn2_down"], g_wgu2, dn2 = ffn_bwd(dh3_half, a2, gu2, n2, w["w_ffn2_down"], wgu2, "ffn2")
    gw["w_ffn2_gate"], gw["w_ffn2_up"] = g_wgu2[:, :f_ff], g_wgu2[:, f_ff:]
    dh2, dh2_b, gs["ln_ffn2"] = _rms_bwd(dn2, h2, gain("ln_ffn2"), dh3, "rms_ffn2_bwd", 1.0)

    gw["w_out"] = _mm(mix, dh2_b, "tn", BF16, "d_w_out", tn=256)
    dmix = _mm(dh2_b, w["w_out"], "nt", F32, "d_mix")
    dza, dzb, dgm, gs["b_merge"] = _merge_bwd(dmix, pm, bm, za, zb)
    gw["w_ret_out"] = _mm(y_ret, dza, "tn", BF16, "d_w_ret_out", tn=256)
    g_fox_pad = _mm(y_fox, dzb, "tn", BF16, "d_w_fox_out", tn=256)
    gw["w_fox_out"] = g_fox_pad.reshape(FOX_HEADS, LANES, d)[:, :FOX_DIM].reshape(FOX_WIDTH, d)
    dy_ret = _mm(dza, w["w_ret_out"], "nt", F32, "d_y_ret")
    do_fox = _mm(dzb, w_fox_pad, "nt", F32, "d_y_fox")

    drq, drk, drv, drg = _ret_bwd(dy_ret, pm, off, cos2, sin2, y_raw, states, consts)

    neg_lse = o_fox.reshape(t, FOX_HEADS, LANES)[:, :, FOX_LSE_LANE:FOX_LSE_LANE + 3].astype(BF16)
    qa_b = fox_tiles(q3, c_parts, ones3, neg_lse)
    dqa, dka, dva = _fox_bwd(qa_b, ka, va, do_fox, o_fox, fox_blk)
    dqa, dka, dva = (a.reshape(t, FOX_HEADS, LANES) for a in (dqa, dka, dva))
    dfq = (dqa[:, :, :FOX_DIM] * FOX_DIM ** -0.5).astype(BF16).reshape(t, FOX_WIDTH)
    dfk = dka[:, :, :FOX_DIM].astype(BF16).reshape(t, FOX_WIDTH)
    dfv = dva[:, :, :FOX_DIM].reshape(t, FOX_WIDTH)
    dc = jnp.pad(dqa[:, :, FOX_ROWSUM_LANE] - dka[:, :, FOX_COLSUM_LANE], ((0, 0), (0, LANES - FOX_HEADS)))
    dff, db_forget = _fox_post(dc, pm, ff_tile, bpad)
    gs["b_forget"] = db_forget[:, :FOX_HEADS]

    dpm = jnp.concatenate([dgm, drq, drk, drv, drg, dfq, dfk, dfv, dff], axis=1)
    g_um = _mm(u, dpm, "tn", BF16, "d_w_mixer_in", tn=640)
    du = _mm(dpm, w_um, "nt", F32, "d_u", tm=512)
    gw["w_merge"] = g_um[:, :2 * d]
    g_in = g_um[:, 2 * d:2 * d + IN_COLS]
    gw["w_in"] = jnp.concatenate([_interleave(g_in[:, :RET_WIDTH]), _interleave(g_in[:, RET_WIDTH:2 * RET_WIDTH]),
                                  g_in[:, 2 * RET_WIDTH:]], axis=1)
    dh1, dh1_half, gs["ln_mix"] = _rms_bwd(du, h1, gain("ln_mix"), dh2, "rms_mix_bwd", 0.5)

    gw["w_ffn1_down"], g_wgu1, dn1 = ffn_bwd(dh1_half, a1, gu1, n1, w["w_ffn1_down"], wgu1, "ffn1")
    gw["w_ffn1_gate"], gw["w_ffn1_up"] = g_wgu1[:, :f_ff], g_wgu1[:, f_ff:]
    dx, _, gs["ln_ffn1"] = _rms_bwd(dn1, x, gain("ln_ffn1"), dh1, "rms_ffn1_bwd", 1.0)
    return loss, dx, gw, gs


WEIGHTS = ("ln_ffn1", "w_ffn1_gate", "w_ffn1_up", "w_ffn1_down", "ln_mix", "w_in", "b_forget", "w_merge", "b_merge",
           "w_ret_out", "w_fox_out", "w_out", "ln_ffn2", "w_ffn2_gate", "w_ffn2_up", "w_ffn2_down", "ln_ple", "w_ple",
           "w_ple_gate", "ln_final")


def kernel(x, p, positions, ln_ffn1, w_ffn1_gate, w_ffn1_up, w_ffn1_down, ln_mix, w_in, b_forget, w_merge, b_merge, w_ret_out, w_fox_out, w_out, ln_ffn2, w_ffn2_gate, w_ffn2_up, w_ffn2_down, ln_ple, w_ple, w_ple_gate, ln_final, loss_target, m_ln_ffn1, m_w_ffn1_gate, m_w_ffn1_up, m_w_ffn1_down, m_ln_mix, m_w_in, m_b_forget, m_w_merge, m_b_merge, m_w_ret_out, m_w_fox_out, m_w_out, m_ln_ffn2, m_w_ffn2_gate, m_w_ffn2_up, m_w_ffn2_down, m_ln_ple, m_w_ple, m_w_ple_gate, m_ln_final, v_ln_ffn1, v_w_ffn1_gate, v_w_ffn1_up, v_w_ffn1_down, v_ln_mix, v_w_in, v_b_forget, v_w_merge, v_b_merge, v_w_ret_out, v_w_fox_out, v_w_out, v_ln_ffn2, v_w_ffn2_gate, v_w_ffn2_up, v_w_ffn2_down, v_ln_ple, v_w_ple, v_w_ple_gate, v_ln_final):
    args = dict(ln_ffn1=ln_ffn1, w_ffn1_gate=w_ffn1_gate, w_ffn1_up=w_ffn1_up, w_ffn1_down=w_ffn1_down, ln_mix=ln_mix, w_in=w_in, b_forget=b_forget, w_merge=w_merge, b_merge=b_merge, w_ret_out=w_ret_out, w_fox_out=w_fox_out, w_out=w_out, ln_ffn2=ln_ffn2, w_ffn2_gate=w_ffn2_gate, w_ffn2_up=w_ffn2_up, w_ffn2_down=w_ffn2_down, ln_ple=ln_ple, w_ple=w_ple, w_ple_gate=w_ple_gate, ln_final=ln_final)
    moms = dict(ln_ffn1=m_ln_ffn1, w_ffn1_gate=m_w_ffn1_gate, w_ffn1_up=m_w_ffn1_up, w_ffn1_down=m_w_ffn1_down, ln_mix=m_ln_mix, w_in=m_w_in, b_forget=m_b_forget, w_merge=m_w_merge, b_merge=m_b_merge, w_ret_out=m_w_ret_out, w_fox_out=m_w_fox_out, w_out=m_w_out, ln_ffn2=m_ln_ffn2, w_ffn2_gate=m_w_ffn2_gate, w_ffn2_up=m_w_ffn2_up, w_ffn2_down=m_w_ffn2_down, ln_ple=m_ln_ple, w_ple=m_w_ple, w_ple_gate=m_w_ple_gate, ln_final=m_ln_final)
    vars_ = dict(ln_ffn1=v_ln_ffn1, w_ffn1_gate=v_w_ffn1_gate, w_ffn1_up=v_w_ffn1_up, w_ffn1_down=v_w_ffn1_down, ln_mix=v_ln_mix, w_in=v_w_in, b_forget=v_b_forget, w_merge=v_w_merge, b_merge=v_b_merge, w_ret_out=v_w_ret_out, w_fox_out=v_w_fox_out, w_out=v_w_out, ln_ffn2=v_ln_ffn2, w_ffn2_gate=v_w_ffn2_gate, w_ffn2_up=v_w_ffn2_up, w_ffn2_down=v_w_ffn2_down, ln_ple=v_ln_ple, w_ple=v_w_ple, w_ple_gate=v_w_ple_gate, v_ln_final=v_ln_final)
    vars_["ln_final"] = vars_.pop("v_ln_final")
    d = x.shape[-1]
    big_names = [n for n, _ in BIG]
    shard2d = lambda a: a.reshape(a.shape[-2:])
    w_sh = {n: shard2d(args[n]) for n in big_names}
    m_sh = {n: shard2d(moms[n]) for n in big_names}
    v_sh = {n: shard2d(vars_[n]) for n in big_names}
    shapes = {n: w_sh[n].shape for n in big_names}

    gathered = _all_gather(_pack_shards(w_sh, d, BF16))
    w_full = _full_from_gathered(gathered, shapes, d)

    small = {n: args[n] for n in SMALL}
    loss_part, dx, gw, gs = _local_step(x[0], p[0, 0], positions[0], loss_target[0], w_full, small)

    blocks = _blocks_from_full(gw, shapes, d, BF16)
    small_part = _pack_small(gs, with_loss=loss_part)
    recv, small_all = _reduce_scatter_exchange(blocks, small_part)

    outs = _adamw(recv, _pack_shards(w_sh, d, F32), _pack_shards(m_sh, d, F32), _pack_shards(v_sh, d, F32), "adamw_shards", 112)
    s_outs = _adamw(small_all, _pack_small(small), _pack_small({n: moms[n] for n in SMALL}),
                    _pack_small({n: vars_[n] for n in SMALL}), "adamw_small", 512)
    res = {}
    for kind, big, sm in zip(("grad", "delta", "new_m", "new_v"), outs, s_outs):
        vals = _unpack_shards(big, shapes, d)
        svals, extra = _unpack_small(sm, {n: args[n].shape for n in SMALL})
        if kind == "grad":
            loss = extra
        for n in WEIGHTS:
            res[(kind, n)] = vals[n].reshape(args[n].shape) if n in vals else svals[n]
    return (loss, dx[None], *[res[(kind, n)] for kind in ("grad", "delta", "new_m", "new_v") for n in WEIGHTS])
```

```python
import functools

import numpy as np
import jax
import jax.numpy as jnp
from jax import lax
from jax.experimental import pallas as pl
from jax.experimental.pallas import tpu as pltpu

F32 = jnp.float32
BF16 = jnp.bfloat16

N_DEV = 8
EPS = 1e-6
RET_HEADS = 4
RET_DIM = 128
RET_WIDTH = RET_HEADS * RET_DIM
FOX_HEADS = 8
FOX_DIM = 64
FOX_WIDTH = FOX_HEADS * FOX_DIM
CHUNK = 128
ROPE_BASE = 10000.0
IN_COLS = 4 * RET_WIDTH + 3 * FOX_WIDTH + FOX_HEADS
IN_PAD = 4 * RET_WIDTH + 3 * FOX_WIDTH + 128
LANES = 128
NEG = -1e30

ADAM_LR = 0.001
ADAM_B1 = 0.9
ADAM_B2 = 0.999
ADAM_EPS = 1e-08
ADAM_WD = 0.01
ADAM_STEP = 10

VMEM_LIMIT_BYTES = 56 * 1024 * 1024

MESH = pl.DeviceIdType.MESH


def _tile(dim, pref, mult):
    if dim <= pref:
        return dim
    t = (pref // mult) * mult
    while t >= mult:
        if dim % t == 0:
            return t
        t -= mult
    return dim


def _pcall(body, *, name, out_shape, grid, in_specs, out_specs, scratch_shapes=(), dims=None):
    return pl.pallas_call(
        body,
        name=name,
        out_shape=out_shape,
        grid=grid,
        in_specs=in_specs,
        out_specs=out_specs,
        scratch_shapes=list(scratch_shapes),
        compiler_params=pltpu.CompilerParams(dimension_semantics=dims, vmem_limit_bytes=VMEM_LIMIT_BYTES),
    )


def _dot(a, b, ca, cb):
    return lax.dot_general(a, b, (((ca,), (cb,)), ((), ())), preferred_element_type=F32)


def _sigmoid(x):
    return 1.0 / (1.0 + jnp.exp(-x))


def _mm(a, b, mode, out_dtype, name, tm=1024, tn=1024, tk=8192):
    if mode == "nn":
        (m, k), (k2, n) = a.shape, b.shape
    elif mode == "nt":
        (m, k), (n, k2) = a.shape, b.shape
    else:
        (k, m), (k2, n) = a.shape, b.shape
    assert k == k2, (name, a.shape, b.shape)
    tm = _tile(m, tm, 128 if mode == "tn" else 16)
    tn = _tile(n, tn, 128)
    tk = _tile(k, tk, 128)
    nk = k // tk
    if mode == "tn":
        a_spec = pl.BlockSpec((tk, tm), lambda i, j, kk: (kk, i))
        ca = 0
    else:
        a_spec = pl.BlockSpec((tm, tk), lambda i, j, kk: (i, kk))
        ca = 1
    if mode == "nt":
        b_spec = pl.BlockSpec((tn, tk), lambda i, j, kk: (j, kk))
        cb = 1
    else:
        b_spec = pl.BlockSpec((tk, tn), lambda i, j, kk: (kk, j))
        cb = 0

    if nk == 1:

        def body(a_ref, b_ref, o_ref):
            o_ref[...] = _dot(a_ref[...], b_ref[...], ca, cb).astype(out_dtype)

        scratch = []
    else:

        def body(a_ref, b_ref, o_ref, acc_ref):
            kk = pl.program_id(2)
            part = _dot(a_ref[...], b_ref[...], ca, cb)

            @pl.when(kk == 0)
            def _():
                acc_ref[...] = part

            @pl.when(kk > 0)
            def _():
                acc_ref[...] += part

            @pl.when(kk == nk - 1)
            def _():
                o_ref[...] = acc_ref[...].astype(out_dtype)

        scratch = [pltpu.VMEM((tm, tn), F32)]

    return _pcall(
        body,
        name=name,
        out_shape=jax.ShapeDtypeStruct((m, n), out_dtype),
        grid=(m // tm, n // tn, nk),
        in_specs=[a_spec, b_spec],
        out_specs=pl.BlockSpec((tm, tn), lambda i, j, kk: (i, j)),
        scratch_shapes=scratch,
        dims=("parallel", "parallel", "arbitrary"),
    )(a, b)


def _rms_fwd(h, gain, name, f=None, scale=0.5):
    t, d = h.shape
    tt = _tile(t, 512, 16)
    row = pl.BlockSpec((tt, d), lambda i: (i, 0))
    vec = pl.BlockSpec((1, d), lambda i: (0, 0))

    def norm(hv, g_ref, n_ref):
        r = lax.rsqrt(jnp.mean(hv * hv, axis=-1, keepdims=True) + EPS)
        n_ref[...] = (hv * r * g_ref[...]).astype(BF16)

    if f is None:

        def body(h_ref, g_ref, n_ref):
            norm(h_ref[...], g_ref, n_ref)

        return _pcall(body, name=name, out_shape=jax.ShapeDtypeStruct((t, d), BF16), grid=(t // tt,),
                      in_specs=[row, vec], out_specs=row, dims=("parallel",))(h, gain)

    def body(h_ref, f_ref, g_ref, hn_ref, n_ref):
        hv = h_ref[...] + scale * f_ref[...]
        hn_ref[...] = hv
        norm(hv, g_ref, n_ref)

    return _pcall(body, name=name,
                  out_shape=(jax.ShapeDtypeStruct((t, d), F32), jax.ShapeDtypeStruct((t, d), BF16)),
                  grid=(t // tt,), in_specs=[row, row, vec], out_specs=(row, row), dims=("parallel",))(h, f, gain)


def _rms_bwd(dn, h, gain, dh_in, name, out_scale):
    t, d = h.shape
    tt = _tile(t, 512, 16)
    row = pl.BlockSpec((tt, d), lambda i: (i, 0))
    vec = pl.BlockSpec((1, d), lambda i: (0, 0))

    def body(dn_ref, h_ref, g_ref, dhin_ref, dh_ref, dhb_ref, dg_ref):
        hv = h_ref[...]
        dnv = dn_ref[...].astype(F32)
        r = lax.rsqrt(jnp.mean(hv * hv, axis=-1, keepdims=True) + EPS)
        dng = dnv * g_ref[...]
        dh = dhin_ref[...] + r * dng - hv * (r * r * r) * jnp.mean(dng * hv, axis=-1, keepdims=True)
        dh_ref[...] = dh
        dhb_ref[...] = (out_scale * dh).astype(BF16)
        part = jnp.sum(dnv * hv * r, axis=0, keepdims=True)

        @pl.when(pl.program_id(0) == 0)
        def _():
            dg_ref[...] = part

        @pl.when(pl.program_id(0) > 0)
        def _():
            dg_ref[...] += part

    return _pcall(body, name=name,
                  out_shape=(jax.ShapeDtypeStruct((t, d), F32), jax.ShapeDtypeStruct((t, d), BF16),
                             jax.ShapeDtypeStruct((1, d), F32)),
                  grid=(t // tt,), in_specs=[row, row, vec, row], out_specs=(row, row, vec),
                  dims=("arbitrary",))(dn, h, gain, dh_in)


def _swiglu_fwd(gu, name):
    t, f2 = gu.shape
    f = f2 // 2
    tt = _tile(t, 256, 16)

    def body(gu_ref, a_ref):
        g = gu_ref[:, :f]
        u = gu_ref[:, f:]
        a_ref[...] = (g * _sigmoid(g) * u).astype(BF16)

    return _pcall(body, name=name, out_shape=jax.ShapeDtypeStruct((t, f), BF16), grid=(t // tt,),
                  in_specs=[pl.BlockSpec((tt, f2), lambda i: (i, 0))],
                  out_specs=pl.BlockSpec((tt, f), lambda i: (i, 0)), dims=("parallel",))(gu)


def _swiglu_bwd(da, gu, name):
    t, f2 = gu.shape
    f = f2 // 2
    tt = _tile(t, 256, 16)

    def body(da_ref, gu_ref, d_ref):
        g = gu_ref[:, :f]
        u = gu_ref[:, f:]
        dav = da_ref[...].astype(F32)
        sg = _sigmoid(g)
        d_ref[:, :f] = (dav * u * (sg * (1.0 + g * (1.0 - sg)))).astype(BF16)
        d_ref[:, f:] = (dav * (g * sg)).astype(BF16)

    return _pcall(body, name=name, out_shape=jax.ShapeDtypeStruct((t, f2), BF16), grid=(t // tt,),
                  in_specs=[pl.BlockSpec((tt, f), lambda i: (i, 0)), pl.BlockSpec((tt, f2), lambda i: (i, 0))],
                  out_specs=pl.BlockSpec((tt, f2), lambda i: (i, 0)), dims=("parallel",))(da, gu)


def _rope_tables(pos_col, inv2, sign2):
    t = pos_col.shape[0]

    def body(p_ref, inv_ref, sg_ref, c_ref, s_ref):
        ang = p_ref[...].astype(F32) * inv_ref[...]
        c_ref[...] = jnp.cos(ang)
        s_ref[...] = jnp.sin(ang) * sg_ref[...]

    full = lambda shape: pl.BlockSpec(shape, lambda i: (0, 0))
    return _pcall(body, name="rope_tables", out_shape=(jax.ShapeDtypeStruct((t, RET_DIM), F32),) * 2, grid=(1,),
                  in_specs=[full((t, 1)), full((1, RET_DIM)), full((1, RET_DIM))],
                  out_specs=(full((t, RET_DIM)),) * 2, dims=("arbitrary",))(pos_col, inv2, sign2)


def _rot(x, c, s):
    return x * c + pltpu.roll(x, RET_DIM // 2, 1) * s


def _rot_t(g, c, s):
    return g * c + pltpu.roll(g * s, RET_DIM // 2, 1)


def _ret_consts():
    hh = np.arange(RET_HEADS, dtype=np.float32)
    log_gamma = np.log1p(-np.exp2(-5.0 - hh)).astype(np.float32)
    idx = np.arange(CHUNK, dtype=np.float32)
    diff = idx[:, None] - idx[None, :]
    dmask = np.where(diff >= 0, np.exp(log_gamma[:, None, None] * np.maximum(diff, 0.0)), 0.0).astype(np.float32)
    kdec = np.exp(log_gamma[:, None] * (CHUNK - 1 - idx)).astype(np.float32)
    qdec = np.exp(log_gamma[:, None] * (idx + 1.0)).astype(np.float32)
    cdec = np.exp(log_gamma * CHUNK).astype(np.float32)
    bc = lambda v: np.ascontiguousarray(np.broadcast_to(v[:, :, None], (RET_HEADS, CHUNK, RET_DIM)))
    cd = np.ascontiguousarray(np.broadcast_to(cdec[:, None, None], (RET_HEADS, 8, RET_DIM)))
    return jnp.asarray(dmask), jnp.asarray(bc(qdec)), jnp.asarray(bc(kdec)), jnp.asarray(cd)


def _ret_fwd(pm, off, cos2, sin2, consts):
    t = pm.shape[0]
    n_chunks = t // CHUNK
    dmask, qdec, kdec, cd = consts
    scale = RET_DIM ** -0.5

    def col(c0):
        return pl.BlockSpec((CHUNK, RET_DIM), lambda h, n: (n, off + c0 + h))

    tab = pl.BlockSpec((CHUNK, RET_DIM), lambda h, n: (n, 0))
    head3 = lambda r: pl.BlockSpec((None, r, RET_DIM), lambda h, n: (h, 0, 0))

    def body(q_ref, k_ref, v_ref, g_ref, c_ref, s_ref, dm_ref, qd_ref, kd_ref, cd_ref, y_ref, raw_ref, st_ref, s_acc):
        @pl.when(pl.program_id(1) == 0)
        def _():
            s_acc[...] = jnp.zeros_like(s_acc)

        c, s = c_ref[...], s_ref[...]
        q = _rot(q_ref[...], c, s)
        k = _rot(k_ref[...], c, s) * scale
        vb = v_ref[...].astype(BF16)
        g = g_ref[...]
        s_in = s_acc[...]
        st_ref[...] = s_in
        a = _dot(q.astype(BF16), k.astype(BF16), 1, 1) * dm_ref[...]
        y = _dot(a.astype(BF16), vb, 1, 0) + _dot((q * qd_ref[...]).astype(BF16), s_in.astype(BF16), 1, 0)
        s_acc[...] = cd_ref[0:1, :] * s_in + _dot((k * kd_ref[...]).astype(BF16), vb, 0, 0)
        raw_ref[...] = y
        mu = jnp.mean(y, axis=-1, keepdims=True)
        yc = y - mu
        rs = lax.rsqrt(jnp.mean(yc * yc, axis=-1, keepdims=True) + EPS)
        y_ref[...] = (yc * rs * (g * _sigmoid(g))).astype(BF16)

    out_blk = pl.BlockSpec((CHUNK, RET_DIM), lambda h, n: (n, h))
    return _pcall(
        body, name="retention_fwd",
        out_shape=(jax.ShapeDtypeStruct((t, RET_WIDTH), BF16), jax.ShapeDtypeStruct((t, RET_WIDTH), F32),
                   jax.ShapeDtypeStruct((RET_HEADS, n_chunks, RET_DIM, RET_DIM), F32)),
        grid=(RET_HEADS, n_chunks),
        in_specs=[col(0), col(4), col(8), col(12), tab, tab,
                  pl.BlockSpec((None, CHUNK, CHUNK), lambda h, n: (h, 0, 0)), head3(CHUNK), head3(CHUNK), head3(8)],
        out_specs=(out_blk, out_blk, pl.BlockSpec((None, None, RET_DIM, RET_DIM), lambda h, n: (h, n, 0, 0))),
        scratch_shapes=[pltpu.VMEM((RET_DIM, RET_DIM), F32)],
        dims=("parallel", "arbitrary"),
    )(pm, pm, pm, pm, cos2, sin2, dmask, qdec, kdec, cd)


def _ret_bwd(dy, pm, off, cos2, sin2, raw, states, consts):
    t = pm.shape[0]
    n_chunks = t // CHUNK
    dmask, qdec, kdec, cd = consts
    scale = RET_DIM ** -0.5
    rev = lambda n: n_chunks - 1 - n

    def col(c0):
        return pl.BlockSpec((CHUNK, RET_DIM), lambda h, n: (rev(n), off + c0 + h))

    tab = pl.BlockSpec((CHUNK, RET_DIM), lambda h, n: (rev(n), 0))
    blk = pl.BlockSpec((CHUNK, RET_DIM), lambda h, n: (rev(n), h))
    head3 = lambda r: pl.BlockSpec((None, r, RET_DIM), lambda h, n: (h, 0, 0))

    def body(dy_ref, q_ref, k_ref, v_ref, g_ref, c_ref, s_ref, raw_ref, st_ref, dm_ref, qd_ref, kd_ref, cd_ref,
             dq_ref, dk_ref, dv_ref, dg_ref, ds_acc):
        @pl.when(pl.program_id(1) == 0)
        def _():
            ds_acc[...] = jnp.zeros_like(ds_acc)

        c, s = c_ref[...], s_ref[...]
        q = _rot(q_ref[...], c, s)
        k = _rot(k_ref[...], c, s) * scale
        qb, kb, vb = q.astype(BF16), k.astype(BF16), v_ref[...].astype(BF16)
        g = g_ref[...]
        dm, qd, kd = dm_ref[...], qd_ref[...], kd_ref[...]
        y = raw_ref[...]
        mu = jnp.mean(y, axis=-1, keepdims=True)
        yc = y - mu
        rs = lax.rsqrt(jnp.mean(yc * yc, axis=-1, keepdims=True) + EPS)
        yn = yc * rs
        sg = _sigmoid(g)
        dyo = dy_ref[...]
        dg_ref[...] = (dyo * yn * (sg * (1.0 + g * (1.0 - sg)))).astype(BF16)
        dyn = dyo * (g * sg)
        dyr = rs * (dyn - jnp.mean(dyn, axis=-1, keepdims=True) - yn * jnp.mean(dyn * yn, axis=-1, keepdims=True))
        dyb = dyr.astype(BF16)
        s_in = st_ref[...].astype(BF16)
        ds_out = ds_acc[...]
        dsb = ds_out.astype(BF16)
        a = _dot(qb, kb, 1, 1) * dm
        da = (_dot(dyb, vb, 1, 1) * dm).astype(BF16)
        kdb = (k * kd).astype(BF16)
        qdb = (q * qd).astype(BF16)
        dv_ref[...] = (_dot(a.astype(BF16), dyb, 0, 0) + _dot(kdb, dsb, 1, 0)).astype(BF16)
        dqh = _dot(da, kb, 1, 0) + _dot(dyb, s_in, 1, 1) * qd
        dkh = _dot(da, qb, 0, 0) + _dot(vb, dsb, 1, 1) * kd
        ds_acc[...] = cd_ref[0:1, :] * ds_out + _dot(qdb, dyb, 0, 0)
        dq_ref[...] = _rot_t(dqh, c, s).astype(BF16)
        dk_ref[...] = (_rot_t(dkh, c, s) * scale).astype(BF16)

    return _pcall(
        body, name="retention_bwd",
        out_shape=(jax.ShapeDtypeStruct((t, RET_WIDTH), BF16),) * 4,
        grid=(RET_HEADS, n_chunks),
        in_specs=[blk, col(0), col(4), col(8), col(12), tab, tab, blk,
                  pl.BlockSpec((None, None, RET_DIM, RET_DIM), lambda h, n: (h, rev(n), 0, 0)),
                  pl.BlockSpec((None, CHUNK, CHUNK), lambda h, n: (h, 0, 0)), head3(CHUNK), head3(CHUNK), head3(8)],
        out_specs=(blk,) * 4,
        scratch_shapes=[pltpu.VMEM((RET_DIM, RET_DIM), F32)],
        dims=("parallel", "arbitrary"),
    )(dy, pm, pm, pm, pm, cos2, sin2, raw, states, dmask, qdec, kdec, cd)


def _split3(x):
    hi = x.astype(BF16)
    r1 = x - hi.astype(F32)
    mid = r1.astype(BF16)
    lo = (r1 - mid.astype(F32)).astype(BF16)
    return hi, mid, lo


def _tri_dot(tri, x):
    hi, mid, lo = _split3(x)
    return _dot(tri, lo, 1, 0) + _dot(tri, mid, 1, 0) + _dot(tri, hi, 1, 0)


def _log_sigmoid(z):
    return jnp.minimum(z, 0.0) - jnp.log1p(jnp.exp(-jnp.abs(z)))


def _fox_prep(pm, ff_tile, bpad):
    t = pm.shape[0]
    nb = t // LANES

    def body(ff_ref, b_ref, c_ref):
        r = lax.broadcasted_iota(jnp.int32, (LANES, LANES), 0)
        cc = lax.broadcasted_iota(jnp.int32, (LANES, LANES), 1)
        tri = jnp.where(cc <= r, 1.0, 0.0).astype(BF16)
        bias = b_ref[...]

        lane = lax.broadcasted_iota(jnp.int32, (LANES, LANES), 1)

        def step(b, carry):
            rows = pl.ds(pl.multiple_of(b * LANES, LANES), LANES)
            cs = _tri_dot(tri, _log_sigmoid(ff_ref[rows, :] + bias)) + carry
            hi, mid, lo = (part.astype(F32) for part in _split3(cs))
            c_ref[rows, :] = jnp.where(
                lane < FOX_HEADS, hi,
                jnp.where(lane < 2 * FOX_HEADS, pltpu.roll(mid, FOX_HEADS, 1),
                          jnp.where(lane < 3 * FOX_HEADS, pltpu.roll(lo, 2 * FOX_HEADS, 1), 0.0)))
            return cs[LANES - 1:LANES, :]

        lax.fori_loop(0, nb, step, jnp.zeros((1, LANES), F32))

    return _pcall(body, name="fox_cumsum", out_shape=jax.ShapeDtypeStruct((t, LANES), F32), grid=(1,),
                  in_specs=[pl.BlockSpec((t, LANES), lambda i: (0, ff_tile)), pl.BlockSpec((1, LANES), lambda i: (0, 0))],
                  out_specs=pl.BlockSpec((t, LANES), lambda i: (0, 0)), dims=("arbitrary",))(pm, bpad)


def _fox_post(dc, pm, ff_tile, bpad):
    t = pm.shape[0]
    nb = t // LANES

    def body(dc_ref, ff_ref, b_ref, d_ref, db_ref):
        r = lax.broadcasted_iota(jnp.int32, (LANES, LANES), 0)
        cc = lax.broadcasted_iota(jnp.int32, (LANES, LANES), 1)
        tri = jnp.where(cc >= r, 1.0, 0.0).astype(BF16)
        bias = b_ref[...]

        def step(i, carry):
            tail, acc = carry
            rows = pl.ds(pl.multiple_of((nb - 1 - i) * LANES, LANES), LANES)
            cs = _tri_dot(tri, dc_ref[rows, :]) + tail
            dff = cs * _sigmoid(-(ff_ref[rows, :] + bias))
            d_ref[rows, :] = dff.astype(BF16)
            return cs[0:1, :], acc + jnp.sum(dff, axis=0, keepdims=True)

        zero = jnp.zeros((1, LANES), F32)
        _, acc = lax.fori_loop(0, nb, step, (zero, zero))
        db_ref[...] = acc

    return _pcall(body, name="fox_forget_bwd",
                  out_shape=(jax.ShapeDtypeStruct((t, LANES), BF16), jax.ShapeDtypeStruct((1, LANES), F32)), grid=(1,),
                  in_specs=[pl.BlockSpec((t, LANES), lambda i: (0, 0)), pl.BlockSpec((t, LANES), lambda i: (0, ff_tile)),
                            pl.BlockSpec((1, LANES), lambda i: (0, 0))],
                  out_specs=(pl.BlockSpec((t, LANES), lambda i: (0, 0)), pl.BlockSpec((1, LANES), lambda i: (0, 0))),
                  dims=("arbitrary",))(dc, pm, bpad)


FOX_L_LANE = FOX_DIM
FOX_LSE_LANE = FOX_DIM + 1
FOX_ROWSUM_LANE = FOX_DIM
FOX_COLSUM_LANE = FOX_DIM + 3


def _tri_tables(nb, q_major):
    pairs = [(i, j) for i in range(nb) for j in range(i + 1)] if q_major else \
            [(i, j) for j in range(nb) for i in range(j, nb)]
    return jnp.asarray([a for a, _ in pairs], jnp.int32), jnp.asarray([b for _, b in pairs], jnp.int32)


def _causal(s):
    n = s.shape[0]
    row = lax.broadcasted_iota(jnp.int32, (n, n), 0)
    col = lax.broadcasted_iota(jnp.int32, (n, n), 1)
    return jnp.where(col <= row, s, NEG)


def _lane_col(x, lane):
    sel = lax.broadcasted_iota(jnp.int32, x.shape, 1) == lane
    return jnp.sum(jnp.where(sel, x, 0.0), axis=1, keepdims=True)


def _fox_fwd(qa, ka, va, blk):
    t = qa.shape[0]
    nb = t // blk
    qi, kj = _tri_tables(nb, True)
    q_spec = pl.BlockSpec((blk, LANES), lambda h, s, qi_r, kj_r: (qi_r[s], h))
    k_spec = pl.BlockSpec((blk, LANES), lambda h, s, qi_r, kj_r: (kj_r[s], h))

    def body(qi_r, kj_r, q_ref, k_ref, v_ref, o_ref, m_s, acc_s):
        s_id = pl.program_id(1)
        i, j = qi_r[s_id], kj_r[s_id]

        @pl.when(j == 0)
        def _():
            m_s[...] = jnp.full_like(m_s, NEG)
            acc_s[...] = jnp.zeros_like(acc_s)

        def tile(diagonal):
            s = _dot(q_ref[...], k_ref[...], 1, 1)
            if diagonal:
                s = _causal(s)
            m_old = m_s[...]
            m_new = jnp.maximum(m_old, jnp.max(s, axis=1, keepdims=True))
            p = jnp.exp(s - jnp.tile(m_new, (1, blk // LANES)))
            acc_s[...] = jnp.exp(m_old - m_new) * acc_s[...] + _dot(p.astype(BF16), v_ref[...], 1, 0)
            m_s[...] = m_new

        @pl.when(j < i)
        def _():
            tile(False)

        @pl.when(j == i)
        def _():
            tile(True)
            acc = acc_s[...]
            l = _lane_col(acc, FOX_L_LANE)
            hi, mid, lo = (part.astype(F32) for part in _split3(-(m_s[:, 0:1] + jnp.log(l))))
            lane = lax.broadcasted_iota(jnp.int32, acc.shape, 1)
            o_ref[...] = jnp.where(lane == FOX_LSE_LANE, hi,
                                   jnp.where(lane == FOX_LSE_LANE + 1, mid,
                                             jnp.where(lane == FOX_LSE_LANE + 2, lo, acc / l)))

    return pl.pallas_call(
        body, name="fox_fwd", out_shape=jax.ShapeDtypeStruct((t, FOX_HEADS * LANES), F32),
        grid_spec=pltpu.PrefetchScalarGridSpec(
            num_scalar_prefetch=2, grid=(FOX_HEADS, qi.shape[0]), in_specs=[q_spec, k_spec, k_spec], out_specs=q_spec,
            scratch_shapes=[pltpu.VMEM((blk, LANES), F32), pltpu.VMEM((blk, LANES), F32)]),
        compiler_params=pltpu.CompilerParams(dimension_semantics=("parallel", "arbitrary"),
                                             vmem_limit_bytes=VMEM_LIMIT_BYTES),
    )(qi, kj, qa, ka, va)


def _fox_bwd(qa, ka, va, do, o, blk):
    t = qa.shape[0]
    nb = t // blk
    qi, kj = _tri_tables(nb, False)
    q_spec = pl.BlockSpec((blk, LANES), lambda h, s, qi_r, kj_r: (qi_r[s], h))
    k_spec = pl.BlockSpec((blk, LANES), lambda h, s, qi_r, kj_r: (kj_r[s], h))
    head_spec = pl.BlockSpec((t, LANES), lambda h, s, qi_r, kj_r: (0, h))
    first_spec = pl.BlockSpec((blk, LANES), lambda h, s, qi_r, kj_r: (jnp.where(kj_r[s] == 0, qi_r[s], nb - 1), h))

    def body(qi_r, kj_r, q_ref, k_ref, v_ref, do_ref, o_ref, dq_ref, dk_ref, dv_ref, doa_s, dk_s, dv_s):
        s_id = pl.program_id(1)
        i, j = qi_r[s_id], kj_r[s_id]
        rows = pl.ds(pl.multiple_of(i * blk, blk), blk)

        @pl.when(j == 0)
        def _():
            dof = do_ref[...]
            hi, mid, lo = _split3(-jnp.sum(dof * o_ref[...], axis=1, keepdims=True))
            lane = lax.broadcasted_iota(jnp.int32, dof.shape, 1)
            doa = jnp.where(lane == FOX_DIM, hi.astype(F32),
                            jnp.where(lane == FOX_DIM + 1, mid.astype(F32),
                                      jnp.where(lane == FOX_DIM + 2, lo.astype(F32), dof)))
            doa_s[rows, :] = doa.astype(BF16)
            dq_ref[rows, :] = jnp.zeros((blk, LANES), F32)

        @pl.when(i == j)
        def _():
            dk_s[...] = jnp.zeros_like(dk_s)
            dv_s[...] = jnp.zeros_like(dv_s)

        def tile(diagonal):
            q, k = q_ref[...], k_ref[...]
            s = _dot(q, k, 1, 1)
            if diagonal:
                s = _causal(s)
            p = jnp.exp(s)
            doa = doa_s[rows, :]
            ds = (p * _dot(doa, v_ref[...], 1, 1)).astype(BF16)
            dv_s[...] += _dot(p.astype(BF16), doa, 0, 0)
            dk_s[...] += _dot(ds, q, 0, 0)
            dq_ref[rows, :] += _dot(ds, k, 1, 0)

        @pl.when(i > j)
        def _():
            tile(False)

        @pl.when(i == j)
        def _():
            tile(True)

        @pl.when(i == nb - 1)
        def _():
            dk_ref[...] = dk_s[...]
            dv_ref[...] = dv_s[...].astype(BF16)

    wide = (t, FOX_HEADS * LANES)
    return pl.pallas_call(
        body, name="fox_bwd",
        out_shape=(jax.ShapeDtypeStruct(wide, F32), jax.ShapeDtypeStruct(wide, F32), jax.ShapeDtypeStruct(wide, BF16)),
        grid_spec=pltpu.PrefetchScalarGridSpec(
            num_scalar_prefetch=2, grid=(FOX_HEADS, qi.shape[0]),
            in_specs=[q_spec, k_spec, k_spec, first_spec, first_spec], out_specs=(head_spec, k_spec, k_spec),
            scratch_shapes=[pltpu.VMEM((t, LANES), BF16), pltpu.VMEM((blk, LANES), F32), pltpu.VMEM((blk, LANES), F32)]),
        compiler_params=pltpu.CompilerParams(dimension_semantics=("parallel", "arbitrary"),
                                             vmem_limit_bytes=VMEM_LIMIT_BYTES),
    )(qi, kj, qa, ka, va, do, o)


def _merge_fwd(pm, bm, za, zb):
    t, d = za.shape
    tt = _tile(t, 256, 16)
    row = pl.BlockSpec((tt, d), lambda i: (i, 0))

    def body(gm_ref, b_ref, za_ref, zb_ref, o_ref):
        ga = _sigmoid(gm_ref[:, :d] + b_ref[:, :d])
        gb = _sigmoid(gm_ref[:, d:] + b_ref[:, d:])
        o_ref[...] = (ga * za_ref[...] + gb * zb_ref[...]).astype(BF16)

    return _pcall(body, name="merge_fwd", out_shape=jax.ShapeDtypeStruct((t, d), BF16), grid=(t // tt,),
                  in_specs=[pl.BlockSpec((tt, 2 * d), lambda i: (i, 0)), pl.BlockSpec((1, 2 * d), lambda i: (0, 0)), row, row],
                  out_specs=row, dims=("parallel",))(pm, bm, za, zb)


def _merge_bwd(dmix, pm, bm, za, zb):
    t, d = za.shape
    tt = _tile(t, 256, 16)
    row = pl.BlockSpec((tt, d), lambda i: (i, 0))
    wide = pl.BlockSpec((tt, 2 * d), lambda i: (i, 0))
    vec = pl.BlockSpec((1, 2 * d), lambda i: (0, 0))

    def body(dm_ref, gm_ref, b_ref, za_ref, zb_ref, dza_ref, dzb_ref, dgm_ref, db_ref):
        dm = dm_ref[...]
        ga = _sigmoid(gm_ref[:, :d] + b_ref[:, :d])
        gb = _sigmoid(gm_ref[:, d:] + b_ref[:, d:])
        dza_ref[...] = (dm * ga).astype(BF16)
        dzb_ref[...] = (dm * gb).astype(BF16)
        dla = dm * za_ref[...] * ga * (1.0 - ga)
        dlb = dm * zb_ref[...] * gb * (1.0 - gb)
        dgm_ref[:, :d] = dla.astype(BF16)
        dgm_ref[:, d:] = dlb.astype(BF16)
        pa = jnp.sum(dla, axis=0, keepdims=True)
        pb = jnp.sum(dlb, axis=0, keepdims=True)

        @pl.when(pl.program_id(0) == 0)
        def _():
            db_ref[:, :d] = pa
            db_ref[:, d:] = pb

        @pl.when(pl.program_id(0) > 0)
        def _():
            db_ref[:, :d] += pa
            db_ref[:, d:] += pb

    return _pcall(body, name="merge_bwd",
                  out_shape=(jax.ShapeDtypeStruct((t, d), BF16), jax.ShapeDtypeStruct((t, d), BF16),
                             jax.ShapeDtypeStruct((t, 2 * d), BF16), jax.ShapeDtypeStruct((1, 2 * d), F32)),
                  grid=(t // tt,), in_specs=[row, wide, vec, row, row], out_specs=(row, row, wide, vec),
                  dims=("arbitrary",))(dmix, pm, bm, za, zb)


def _ple_final(h3, pgl, pe, gain, target):
    t, d = h3.shape
    tt = _tile(t, 256, 16)
    row = pl.BlockSpec((tt, d), lambda i: (i, 0))
    vec = pl.BlockSpec((1, d), lambda i: (0, 0))
    lvec = pl.BlockSpec((1, LANES), lambda i: (0, 0))

    def body(h_ref, pgl_ref, pe_ref, g_ref, t_ref, dh_ref, dsg_ref, dpe_ref, loss_ref, dg_ref):
        pg = _sigmoid(pgl_ref[...])
        pe_v = pe_ref[...]
        h4 = h_ref[...] + pg * pe_v
        r = lax.rsqrt(jnp.mean(h4 * h4, axis=-1, keepdims=True) + EPS)
        gv = g_ref[...]
        err = h4 * r * gv - t_ref[...]
        part_loss = 0.5 * jnp.sum(jnp.mean(err * err, axis=-1, keepdims=True), axis=0, keepdims=True)
        dy = err * (1.0 / d)
        part_g = jnp.sum(dy * h4 * r, axis=0, keepdims=True)
        dyg = dy * gv
        dh = r * dyg - h4 * (r * r * r) * jnp.mean(dyg * h4, axis=-1, keepdims=True)
        dh_ref[...] = dh
        dsg_ref[...] = (dh * pe_v * pg * (1.0 - pg)).astype(BF16)
        dpe_ref[...] = (dh * pg).astype(BF16)

        @pl.when(pl.program_id(0) == 0)
        def _():
            loss_ref[...] = jnp.broadcast_to(part_loss, (1, LANES))
            dg_ref[...] = part_g

        @pl.when(pl.program_id(0) > 0)
        def _():
            loss_ref[...] += jnp.broadcast_to(part_loss, (1, LANES))
            dg_ref[...] += part_g

    return _pcall(body, name="ple_final",
                  out_shape=(jax.ShapeDtypeStruct((t, d), F32), jax.ShapeDtypeStruct((t, d), BF16),
                             jax.ShapeDtypeStruct((t, d), BF16), jax.ShapeDtypeStruct((1, LANES), F32),
                             jax.ShapeDtypeStruct((1, d), F32)),
                  grid=(t // tt,), in_specs=[row, row, row, vec, row], out_specs=(row, row, row, lvec, vec),
                  dims=("arbitrary",))(h3, pgl, pe, gain, target)


def _adamw_math(w, g, m, v):
    m = ADAM_B1 * m + (1.0 - ADAM_B1) * g
    v = ADAM_B2 * v + (1.0 - ADAM_B2) * (g * g)
    m_hat = m / (1.0 - ADAM_B1 ** ADAM_STEP)
    v_hat = v / (1.0 - ADAM_B2 ** ADAM_STEP)
    delta = -ADAM_LR * (m_hat / (jnp.sqrt(v_hat) + ADAM_EPS) + ADAM_WD * w)
    return delta, m, v


def _adamw(parts, w, m, v, name):
    n, r, c = parts.shape
    tr = _tile(r, 256, 16)
    row = pl.BlockSpec((tr, c), lambda i: (i, 0))

    def body(p_ref, w_ref, m_ref, v_ref, g_ref, d_ref, mo_ref, vo_ref):
        g = p_ref[0].astype(F32)
        for s in range(1, n):
            g = g + p_ref[s].astype(F32)
        g_ref[...] = g
        d_ref[...], mo_ref[...], vo_ref[...] = _adamw_math(w_ref[...], g, m_ref[...], v_ref[...])

    return _pcall(body, name=name, out_shape=(jax.ShapeDtypeStruct((r, c), F32),) * 4, grid=(r // tr,),
                  in_specs=[pl.BlockSpec((n, tr, c), lambda i: (0, i, 0)), row, row, row], out_specs=(row,) * 4,
                  dims=("parallel",))(parts, w, m, v)


ANY = pl.BlockSpec(memory_space=pl.ANY)
VMEM = pl.BlockSpec(memory_space=pltpu.VMEM)


def _all_gather(shards):
    n = len(shards)

    def body(*refs):
        x_refs, out_refs = refs[:n], refs[n:2 * n]
        send_sems, recv_sems, local_sems = refs[2 * n:]
        x, y, cc = lax.axis_index("x"), lax.axis_index("y"), lax.axis_index("c")
        me, sibling = (x, y, cc), (x, y, 1 - cc)
        chips = [(1 - x, y), (x, 1 - y), (1 - x, 1 - y)]

        def slot(a, px, py, pc):
            return out_refs[a].at[4 * px + 2 * py + pc]

        def copy(a, k, block, to, src=None):
            return pltpu.make_async_remote_copy(
                src_ref=slot(a, *block) if src is None else src, dst_ref=slot(a, *block),
                send_sem=send_sems.at[7 * a + k], recv_sem=recv_sems.at[7 * a + k], device_id=to, device_id_type=MESH)

        local, sent = [], []
        for a in range(n):
            local.append(pltpu.make_async_copy(x_refs[a], slot(a, *me), local_sems.at[a]))
            sent.append(copy(a, 0, me, sibling, src=x_refs[a]))
            sent += [copy(a, 1 + j, me, (*chip, cc), src=x_refs[a]) for j, chip in enumerate(chips)]
        for cp in local + sent:
            cp.start()
        for j, chip in enumerate(chips):
            for a in range(n):
                copy(a, 1 + j, (*chip, cc), me).wait_recv()
                sent.append(copy(a, 4 + j, (*chip, cc), sibling))
                sent[-1].start()
        for a in range(n):
            copy(a, 0, sibling, me).wait_recv()
            for j, chip in enumerate(chips):
                copy(a, 4 + j, (*chip, 1 - cc), me).wait_recv()
        for cp in sent:
            cp.wait_send()
        for cp in local:
            cp.wait()

    return pl.pallas_call(
        body, name="weights_all_gather",
        out_shape=tuple(jax.ShapeDtypeStruct((N_DEV,) + s.shape, s.dtype) for s in shards),
        in_specs=[ANY] * n, out_specs=(ANY,) * n,
        scratch_shapes=[pltpu.SemaphoreType.DMA((7 * n,)), pltpu.SemaphoreType.DMA((7 * n,)),
                        pltpu.SemaphoreType.DMA((n,))],
    )(*shards)


def _reduce_scatter_exchange(blocks):
    n = len(blocks)

    def body(*refs):
        g_refs, recv_refs = refs[:n], refs[n:2 * n]
        send_sems, recv_sems, local_sems = refs[2 * n:]
        x, y, cc = lax.axis_index("x"), lax.axis_index("y"), lax.axis_index("c")
        me = 4 * x + 2 * y + cc
        local, sent, landing = [], [], []
        for a in range(n):
            local.append(pltpu.make_async_copy(g_refs[a].at[me], recv_refs[a].at[me], local_sems.at[a]))
        for k in range(1, N_DEV):
            px, py, pc = x ^ (k >> 2), y ^ ((k >> 1) & 1), cc ^ (k & 1)
            peer = 4 * px + 2 * py + pc
            for a in range(n):
                sems = dict(send_sem=send_sems.at[7 * a + k - 1], recv_sem=recv_sems.at[7 * a + k - 1],
                            device_id=(px, py, pc), device_id_type=MESH)
                sent.append(pltpu.make_async_remote_copy(src_ref=g_refs[a].at[peer], dst_ref=recv_refs[a].at[me], **sems))
                landing.append(pltpu.make_async_remote_copy(src_ref=g_refs[a].at[me], dst_ref=recv_refs[a].at[peer], **sems))
        for cp in local + sent:
            cp.start()
        for cp in landing:
            cp.wait_recv()
        for cp in sent:
            cp.wait_send()
        for cp in local:
            cp.wait()

    return pl.pallas_call(
        body, name="grads_reduce_scatter_exchange",
        out_shape=tuple(jax.ShapeDtypeStruct(b.shape, b.dtype) for b in blocks),
        in_specs=[ANY] * n, out_specs=(ANY,) * n,
        scratch_shapes=[pltpu.SemaphoreType.DMA((7 * n,)), pltpu.SemaphoreType.DMA((7 * n,)),
                        pltpu.SemaphoreType.DMA((n,))],
    )(*blocks)


BIG = (("w_ffn1_gate", "col"), ("w_ffn1_up", "col"), ("w_ffn1_down", "row"), ("w_ffn2_gate", "col"),
       ("w_ffn2_up", "col"), ("w_ffn2_down", "row"), ("w_in", "col"), ("w_merge", "col"), ("w_ret_out", "col"),
       ("w_fox_out", "col"), ("w_out", "row"), ("w_ple", "col"), ("w_ple_gate", "row"))


def _full_from_slots(g, kind):
    n, r, c = g.shape
    return g.reshape(n * r, c) if kind == "row" else g.transpose(1, 0, 2).reshape(r, n * c)


def _slots_from_full(f, kind):
    r, c = f.shape
    return f.reshape(N_DEV, r // N_DEV, c) if kind == "row" else f.reshape(r, N_DEV, c // N_DEV).transpose(1, 0, 2)


def _deinterleave(w):
    r = w.shape[0]
    return w.reshape(r, RET_HEADS, RET_DIM // 2, 2).transpose(0, 1, 3, 2).reshape(r, RET_WIDTH)


def _interleave(w):
    r = w.shape[0]
    return w.reshape(r, RET_HEADS, 2, RET_DIM // 2).transpose(0, 1, 3, 2).reshape(r, RET_WIDTH)


SMALL = ("ln_ffn1", "ln_mix", "b_forget", "b_merge", "ln_ffn2", "ln_ple", "ln_final")


def _small_rows(n):
    rows = -(-n // LANES)
    return -(-rows // 8) * 8


def _pack_small(vals, with_loss=None):
    parts = []
    for name in SMALL:
        v = vals[name].reshape(-1).astype(F32)
        rows = _small_rows(v.shape[0])
        parts.append(jnp.pad(v, (0, rows * LANES - v.shape[0])).reshape(rows, LANES))
    if with_loss is not None:
        parts.append(jnp.pad(with_loss.reshape(1, LANES), ((0, 7), (0, 0))))
    else:
        parts.append(jnp.zeros((8, LANES), F32))
    return jnp.concatenate(parts, axis=0)


def _unpack_small(packed, shapes):
    out, at = {}, 0
    for name in SMALL:
        n = int(np.prod(shapes[name]))
        rows = _small_rows(n)
        out[name] = packed[at:at + rows].reshape(-1)[:n].reshape(shapes[name])
        at += rows
    return out, packed[at, 0]


def _local_step(x, p, positions, target, w, small):
    t, d = x.shape
    gain = lambda n: small[n].reshape(1, d)
    wgu1 = jnp.concatenate([w["w_ffn1_gate"], w["w_ffn1_up"]], axis=1)
    wgu2 = jnp.concatenate([w["w_ffn2_gate"], w["w_ffn2_up"]], axis=1)
    w_in = w["w_in"]
    w_in_p = jnp.concatenate([_deinterleave(w_in[:, :RET_WIDTH]), _deinterleave(w_in[:, RET_WIDTH:2 * RET_WIDTH]),
                              w_in[:, 2 * RET_WIDTH:], jnp.zeros((d, IN_PAD - IN_COLS), BF16)], axis=1)
    w_um = jnp.concatenate([w["w_merge"], w_in_p], axis=1)
    off = 2 * d // LANES
    ff_tile = off + (4 * RET_WIDTH + 3 * FOX_WIDTH) // LANES
    bpad = jnp.pad(small["b_forget"].reshape(1, FOX_HEADS), ((0, 0), (0, LANES - FOX_HEADS)))
    bm = small["b_merge"].reshape(1, 2 * d)
    fox_blk = _tile(t, 512, 128)

    n1 = _rms_fwd(x, gain("ln_ffn1"), "rms_ffn1")
    gu1 = _mm(n1, wgu1, "nn", F32, "ffn1_gate_up", tn=1408)
    a1 = _swiglu_fwd(gu1, "ffn1_swiglu")
    f1 = _mm(a1, w["w_ffn1_down"], "nn", F32, "ffn1_down")
    h1, u = _rms_fwd(x, gain("ln_mix"), "rms_mix", f=f1)
    pm = _mm(u, w_um, "nn", F32, "mixer_in", tn=1920)

    half = jnp.arange(RET_DIM // 2, dtype=F32) / (RET_DIM // 2)
    inv = 1.0 / (ROPE_BASE ** half)
    inv2 = jnp.concatenate([inv, inv]).reshape(1, RET_DIM)
    sign2 = jnp.concatenate([-jnp.ones((RET_DIM // 2,), F32), jnp.ones((RET_DIM // 2,), F32)]).reshape(1, RET_DIM)
    cos2, sin2 = _rope_tables(positions.reshape(t, 1), inv2, sign2)
    consts = _ret_consts()
    y_ret, y_raw, states = _ret_fwd(pm, off, cos2, sin2, consts)
    za = _mm(y_ret, w["w_ret_out"], "nn", F32, "ret_out")

    c_all = _fox_prep(pm, ff_tile, bpad)
    c_parts = c_all[:, :3 * FOX_HEADS].reshape(t, 3, FOX_HEADS).transpose(0, 2, 1).astype(BF16)
    fo = 2 * d + 4 * RET_WIDTH
    head3 = lambda a: a.reshape(t, FOX_HEADS, -1)
    ones3 = jnp.ones((t, FOX_HEADS, 3), BF16)
    fox_tiles = lambda *cols: jnp.concatenate(
        cols + (jnp.zeros((t, FOX_HEADS, LANES - sum(c.shape[-1] for c in cols)), BF16),), axis=-1
    ).reshape(t, FOX_HEADS * LANES)
    q3 = (head3(pm[:, fo:fo + FOX_WIDTH]) * FOX_DIM ** -0.5).astype(BF16)
    k3 = head3(pm[:, fo + FOX_WIDTH:fo + 2 * FOX_WIDTH]).astype(BF16)
    v3 = head3(pm[:, fo + 2 * FOX_WIDTH:fo + 3 * FOX_WIDTH]).astype(BF16)
    qa = fox_tiles(q3, c_parts, ones3)
    ka = fox_tiles(k3, ones3, -c_parts, ones3)
    va = fox_tiles(v3, ones3)
    o_fox = _fox_fwd(qa, ka, va, fox_blk)
    y_fox = o_fox.astype(BF16)
    w_fox_pad = jnp.pad(w["w_fox_out"].reshape(FOX_HEADS, FOX_DIM, d), ((0, 0), (0, LANES - FOX_DIM), (0, 0)))
    w_fox_pad = w_fox_pad.reshape(FOX_HEADS * LANES, d)
    zb = _mm(y_fox, w_fox_pad, "nn", F32, "fox_out")

    mix = _merge_fwd(pm, bm, za, zb)
    mo = _mm(mix, w["w_out"], "nn", F32, "mix_out")
    h2, n2 = _rms_fwd(h1, gain("ln_ffn2"), "rms_ffn2", f=mo, scale=1.0)
    gu2 = _mm(n2, wgu2, "nn", F32, "ffn2_gate_up", tn=1408)
    a2 = _swiglu_fwd(gu2, "ffn2_swiglu")
    f2 = _mm(a2, w["w_ffn2_down"], "nn", F32, "ffn2_down")
    h3, n3 = _rms_fwd(h2, gain("ln_ple"), "rms_ple", f=f2)
    pgl = _mm(n3, w["w_ple_gate"], "nn", F32, "ple_gate")
    pb = p.astype(BF16)
    pe = _mm(pb, w["w_ple"], "nn", F32, "ple_embed")

    gw, gs = {}, {}
    dh4, dsg, dpe, loss, gs["ln_final"] = _ple_final(h3, pgl, pe, gain("ln_final"), target)
    gw["w_ple_gate"] = _mm(n3, dsg, "tn", BF16, "d_w_ple_gate", tn=256)
    gw["w_ple"] = _mm(pb, dpe, "tn", BF16, "d_w_ple", tn=256)
    dn3 = _mm(dsg, w["w_ple_gate"], "nt", F32, "d_n3")
    dh3, dh3_half, gs["ln_ple"] = _rms_bwd(dn3, h3, gain("ln_ple"), dh4, "rms_ple_bwd", 0.5)

    def ffn_bwd(dh_half, a, gu, n, wd, wgu, tag):
        g_wd = _mm(a, dh_half, "tn", BF16, f"d_w_{tag}_down", tm=1408, tn=256)
        da = _mm(dh_half, wd, "nt", F32, f"d_a_{tag}", tn=1408)
        dgu = _swiglu_bwd(da, gu, f"{tag}_swiglu_bwd")
        g_wgu = _mm(n, dgu, "tn", BF16, f"d_w_{tag}_gate_up", tn=512)
        dn = _mm(dgu, wgu, "nt", F32, f"d_n_{tag}", tm=512)
        return g_wd, g_wgu, dn

    f_ff = w["w_ffn1_gate"].shape[1]
    gw["w_ffn2_down"], g_wgu2, dn2 = ffn_bwd(dh3_half, a2, gu2, n2, w["w_ffn2_down"], wgu2, "ffn2")
    gw["w_ffn2_gate"], gw["w_ffn2_up"] = g_wgu2[:, :f_ff], g_wgu2[:, f_ff:]
    dh2, dh2_b, gs["ln_ffn2"] = _rms_bwd(dn2, h2, gain("ln_ffn2"), dh3, "rms_ffn2_bwd", 1.0)

    gw["w_out"] = _mm(mix, dh2_b, "tn", BF16, "d_w_out", tn=256)
    dmix = _mm(dh2_b, w["w_out"], "nt", F32, "d_mix")
    dza, dzb, dgm, gs["b_merge"] = _merge_bwd(dmix, pm, bm, za, zb)
    gw["w_ret_out"] = _mm(y_ret, dza, "tn", BF16, "d_w_ret_out", tn=256)
    g_fox_pad = _mm(y_fox, dzb, "tn", BF16, "d_w_fox_out", tn=256)
    gw["w_fox_out"] = g_fox_pad.reshape(FOX_HEADS, LANES, d)[:, :FOX_DIM].reshape(FOX_WIDTH, d)
    dy_ret = _mm(dza, w["w_ret_out"], "nt", F32, "d_y_ret")
    do_fox = _mm(dzb, w_fox_pad, "nt", F32, "d_y_fox")

    drq, drk, drv, drg = _ret_bwd(dy_ret, pm, off, cos2, sin2, y_raw, states, consts)

    neg_lse = o_fox.reshape(t, FOX_HEADS, LANES)[:, :, FOX_LSE_LANE:FOX_LSE_LANE + 3].astype(BF16)
    qa_b = fox_tiles(q3, c_parts, ones3, neg_lse)
    dqa, dka, dva = _fox_bwd(qa_b, ka, va, do_fox, o_fox, fox_blk)
    dqa, dka, dva = (a.reshape(t, FOX_HEADS, LANES) for a in (dqa, dka, dva))
    dfq = (dqa[:, :, :FOX_DIM] * FOX_DIM ** -0.5).astype(BF16).reshape(t, FOX_WIDTH)
    dfk = dka[:, :, :FOX_DIM].astype(BF16).reshape(t, FOX_WIDTH)
    dfv = dva[:, :, :FOX_DIM].reshape(t, FOX_WIDTH)
    dc = jnp.pad(dqa[:, :, FOX_ROWSUM_LANE] - dka[:, :, FOX_COLSUM_LANE], ((0, 0), (0, LANES - FOX_HEADS)))
    dff, db_forget = _fox_post(dc, pm, ff_tile, bpad)
    gs["b_forget"] = db_forget[:, :FOX_HEADS]

    dpm = jnp.concatenate([dgm, drq, drk, drv, drg, dfq, dfk, dfv, dff], axis=1)
    g_um = _mm(u, dpm, "tn", BF16, "d_w_mixer_in", tn=640)
    du = _mm(dpm, w_um, "nt", F32, "d_u", tm=512)
    gw["w_merge"] = g_um[:, :2 * d]
    g_in = g_um[:, 2 * d:2 * d + IN_COLS]
    gw["w_in"] = jnp.concatenate([_interleave(g_in[:, :RET_WIDTH]), _interleave(g_in[:, RET_WIDTH:2 * RET_WIDTH]),
                                  g_in[:, 2 * RET_WIDTH:]], axis=1)
    dh1, dh1_half, gs["ln_mix"] = _rms_bwd(du, h1, gain("ln_mix"), dh2, "rms_mix_bwd", 0.5)

    gw["w_ffn1_down"], g_wgu1, dn1 = ffn_bwd(dh1_half, a1, gu1, n1, w["w_ffn1_down"], wgu1, "ffn1")
    gw["w_ffn1_gate"], gw["w_ffn1_up"] = g_wgu1[:, :f_ff], g_wgu1[:, f_ff:]
    dx, _, gs["ln_ffn1"] = _rms_bwd(dn1, x, gain("ln_ffn1"), dh1, "rms_ffn1_bwd", 1.0)
    return loss, dx, gw, gs


WEIGHTS = ("ln_ffn1", "w_ffn1_gate", "w_ffn1_up", "w_ffn1_down", "ln_mix", "w_in", "b_forget", "w_merge", "b_merge",
           "w_ret_out", "w_fox_out", "w_out", "ln_ffn2", "w_ffn2_gate", "w_ffn2_up", "w_ffn2_down", "ln_ple", "w_ple",
           "w_ple_gate", "ln_final")


def kernel(x, p, positions, ln_ffn1, w_ffn1_gate, w_ffn1_up, w_ffn1_down, ln_mix, w_in, b_forget, w_merge, b_merge, w_ret_out, w_fox_out, w_out, ln_ffn2, w_ffn2_gate, w_ffn2_up, w_ffn2_down, ln_ple, w_ple, w_ple_gate, ln_final, loss_target, m_ln_ffn1, m_w_ffn1_gate, m_w_ffn1_up, m_w_ffn1_down, m_ln_mix, m_w_in, m_b_forget, m_w_merge, m_b_merge, m_w_ret_out, m_w_fox_out, m_w_out, m_ln_ffn2, m_w_ffn2_gate, m_w_ffn2_up, m_w_ffn2_down, m_ln_ple, m_w_ple, m_w_ple_gate, m_ln_final, v_ln_ffn1, v_w_ffn1_gate, v_w_ffn1_up, v_w_ffn1_down, v_ln_mix, v_w_in, v_b_forget, v_w_merge, v_b_merge, v_w_ret_out, v_w_fox_out, v_w_out, v_ln_ffn2, v_w_ffn2_gate, v_w_ffn2_up, v_w_ffn2_down, v_ln_ple, v_w_ple, v_w_ple_gate, v_ln_final):
    args = dict(ln_ffn1=ln_ffn1, w_ffn1_gate=w_ffn1_gate, w_ffn1_up=w_ffn1_up, w_ffn1_down=w_ffn1_down, ln_mix=ln_mix, w_in=w_in, b_forget=b_forget, w_merge=w_merge, b_merge=b_merge, w_ret_out=w_ret_out, w_fox_out=w_fox_out, w_out=w_out, ln_ffn2=ln_ffn2, w_ffn2_gate=w_ffn2_gate, w_ffn2_up=w_ffn2_up, w_ffn2_down=w_ffn2_down, ln_ple=ln_ple, w_ple=w_ple, w_ple_gate=w_ple_gate, ln_final=ln_final)
    moms = dict(ln_ffn1=m_ln_ffn1, w_ffn1_gate=m_w_ffn1_gate, w_ffn1_up=m_w_ffn1_up, w_ffn1_down=m_w_ffn1_down, ln_mix=m_ln_mix, w_in=m_w_in, b_forget=m_b_forget, w_merge=m_w_merge, b_merge=m_b_merge, w_ret_out=m_w_ret_out, w_fox_out=m_w_fox_out, w_out=m_w_out, ln_ffn2=m_ln_ffn2, w_ffn2_gate=m_w_ffn2_gate, w_ffn2_up=m_w_ffn2_up, w_ffn2_down=m_w_ffn2_down, ln_ple=m_ln_ple, w_ple=m_w_ple, w_ple_gate=m_w_ple_gate, ln_final=m_ln_final)
    vars_ = dict(ln_ffn1=v_ln_ffn1, w_ffn1_gate=v_w_ffn1_gate, w_ffn1_up=v_w_ffn1_up, w_ffn1_down=v_w_ffn1_down, ln_mix=v_ln_mix, w_in=v_w_in, b_forget=v_b_forget, w_merge=v_w_merge, b_merge=v_b_merge, w_ret_out=v_w_ret_out, w_fox_out=v_w_fox_out, w_out=v_w_out, ln_ffn2=v_ln_ffn2, w_ffn2_gate=v_w_ffn2_gate, w_ffn2_up=v_w_ffn2_up, w_ffn2_down=v_w_ffn2_down, ln_ple=v_ln_ple, w_ple=v_w_ple, w_ple_gate=v_w_ple_gate, v_ln_final=v_ln_final)
    vars_["ln_final"] = vars_.pop("v_ln_final")
    shard2d = lambda a: a.reshape(a.shape[-2:])
    kinds = ("grad", "delta", "new_m", "new_v")

    gathered = _all_gather([shard2d(args[n]).astype(BF16) for n, _ in BIG])
    w_full = {n: _full_from_slots(g, kind) for (n, kind), g in zip(BIG, gathered)}

    small = {n: args[n] for n in SMALL}
    loss_part, dx, gw, gs = _local_step(x[0], p[0, 0], positions[0], loss_target[0], w_full, small)

    small_part = _pack_small(gs, with_loss=loss_part)
    blocks = [_slots_from_full(gw[n], kind) for n, kind in BIG]
    blocks.append(jnp.broadcast_to(small_part, (N_DEV,) + small_part.shape))
    recv = _reduce_scatter_exchange(blocks)

    res = {}
    for (n, _), parts in zip(BIG, recv):
        outs = _adamw(parts, shard2d(args[n]), shard2d(moms[n]), shard2d(vars_[n]), f"adamw_{n}")
        for kind, o in zip(kinds, outs):
            res[(kind, n)] = o.reshape(args[n].shape)
    s_outs = _adamw(recv[-1], _pack_small(small), _pack_small({n: moms[n] for n in SMALL}),
                    _pack_small({n: vars_[n] for n in SMALL}), "adamw_small")
    for kind, sm in zip(kinds, s_outs):
        svals, extra = _unpack_small(sm, {n: args[n].shape for n in SMALL})
        if kind == "grad":
            loss = extra
        for n in SMALL:
            res[(kind, n)] = svals[n]
    return (loss, dx[None], *[res[(kind, n)] for kind in kinds for n in WEIGHTS])
```

```python
import numpy as np
import jax
import jax.numpy as jnp
from jax import lax
from jax.experimental import pallas as pl
from jax.experimental.pallas import tpu as pltpu

F32 = jnp.float32
BF16 = jnp.bfloat16

N_DEV = 8
EPS = 1e-6
RET_HEADS = 4
RET_DIM = 128
RET_WIDTH = RET_HEADS * RET_DIM
FOX_HEADS = 8
FOX_DIM = 64
FOX_WIDTH = FOX_HEADS * FOX_DIM
CHUNK = 128
ROPE_BASE = 10000.0
LANES = 128
FOX_TILES = FOX_HEADS * LANES
IN_COLS = 4 * RET_WIDTH + 3 * FOX_WIDTH + FOX_HEADS
IN_PAD = 4 * RET_WIDTH + 3 * FOX_TILES + 2 * LANES
TILE_RQ, TILE_RK, TILE_RV, TILE_RG = 0, 4, 8, 12
TILE_FQ, TILE_FK, TILE_FV, TILE_FF = 16, 24, 32, 40
NEG = -1e30

ADAM_LR = 0.001
ADAM_B1 = 0.9
ADAM_B2 = 0.999
ADAM_EPS = 1e-08
ADAM_WD = 0.01
ADAM_STEP = 10

VMEM_LIMIT_BYTES = 56 * 1024 * 1024

MESH = pl.DeviceIdType.MESH


def _tile(dim, pref, mult):
    if dim <= pref:
        return dim
    t = (pref // mult) * mult
    while t >= mult:
        if dim % t == 0:
            return t
        t -= mult
    return dim


def _params(dims):
    return pltpu.CompilerParams(dimension_semantics=dims, vmem_limit_bytes=VMEM_LIMIT_BYTES)


def _pcall(body, *, name, out_shape, grid, in_specs, out_specs, scratch_shapes=(), dims=None):
    return pl.pallas_call(body, name=name, out_shape=out_shape, grid=grid, in_specs=in_specs, out_specs=out_specs,
                          scratch_shapes=list(scratch_shapes), compiler_params=_params(dims))


def _dot(a, b, ca, cb):
    return lax.dot_general(a, b, (((ca,), (cb,)), ((), ())), preferred_element_type=F32)


def _sigmoid(x):
    return 1.0 / (1.0 + jnp.exp(-x))


def _mm(pairs, out_dtype, name, tm=1024, tn=1024):
    dims = []
    for a, b, mode in pairs:
        m, k = (a.shape[1], a.shape[0]) if mode == "tn" else a.shape
        n, k2 = b.shape if mode == "nt" else (b.shape[1], b.shape[0])
        assert k == k2, (name, a.shape, b.shape, mode)
        dims.append((m, n))
    assert all(d == dims[0] for d in dims), (name, dims)
    m, n = dims[0]
    tm = _tile(m, tm, 128 if any(mode == "tn" for _, _, mode in pairs) else 16)
    tn = _tile(n, tn, 128)
    in_specs, contract, operands = [], [], []
    for a, b, mode in pairs:
        k = a.shape[0] if mode == "tn" else a.shape[1]
        in_specs.append(pl.BlockSpec((k, tm), lambda i, j: (0, i)) if mode == "tn" else
                        pl.BlockSpec((tm, k), lambda i, j: (i, 0)))
        in_specs.append(pl.BlockSpec((tn, k), lambda i, j: (j, 0)) if mode == "nt" else
                        pl.BlockSpec((k, tn), lambda i, j: (0, j)))
        contract.append((0 if mode == "tn" else 1, 1 if mode == "nt" else 0))
        operands += [a, b]

    def body(*refs):
        o_ref = refs[-1]
        acc = None
        for p, (ca, cb) in enumerate(contract):
            part = _dot(refs[2 * p][...], refs[2 * p + 1][...], ca, cb)
            acc = part if acc is None else acc + part
        o_ref[...] = acc.astype(out_dtype)

    return _pcall(body, name=name, out_shape=jax.ShapeDtypeStruct((m, n), out_dtype), grid=(m // tm, n // tn),
                  in_specs=in_specs, out_specs=pl.BlockSpec((tm, tn), lambda i, j: (i, j)),
                  dims=("parallel", "parallel"))(*operands)


def _rms_fwd(h, gain, name, f=None, scale=0.5):
    t, d = h.shape
    tt = _tile(t, 512, 16)
    row = pl.BlockSpec((tt, d), lambda i: (i, 0))
    vec = pl.BlockSpec((1, d), lambda i: (0, 0))

    def norm(hv, g_ref, n_ref):
        r = lax.rsqrt(jnp.mean(hv * hv, axis=-1, keepdims=True) + EPS)
        n_ref[...] = (hv * r * g_ref[...]).astype(BF16)

    if f is None:

        def body(h_ref, g_ref, n_ref):
            norm(h_ref[...], g_ref, n_ref)

        return _pcall(body, name=name, out_shape=jax.ShapeDtypeStruct((t, d), BF16), grid=(t // tt,),
                      in_specs=[row, vec], out_specs=row, dims=("parallel",))(h, gain)

    def body(h_ref, f_ref, g_ref, hn_ref, n_ref):
        hv = h_ref[...] + scale * f_ref[...]
        hn_ref[...] = hv
        norm(hv, g_ref, n_ref)

    return _pcall(body, name=name,
                  out_shape=(jax.ShapeDtypeStruct((t, d), F32), jax.ShapeDtypeStruct((t, d), BF16)),
                  grid=(t // tt,), in_specs=[row, row, vec], out_specs=(row, row), dims=("parallel",))(h, f, gain)


def _rms_bwd(dn, h, gain, dh_in, name, out_scale):
    t, d = h.shape
    tt = _tile(t, 512, 16)
    row = pl.BlockSpec((tt, d), lambda i: (i, 0))
    vec = pl.BlockSpec((1, d), lambda i: (0, 0))

    def body(dn_ref, h_ref, g_ref, dhin_ref, dh_ref, dhb_ref, dg_ref):
        hv = h_ref[...]
        dnv = dn_ref[...].astype(F32)
        r = lax.rsqrt(jnp.mean(hv * hv, axis=-1, keepdims=True) + EPS)
        dng = dnv * g_ref[...]
        dh = dhin_ref[...] + r * dng - hv * (r * r * r) * jnp.mean(dng * hv, axis=-1, keepdims=True)
        dh_ref[...] = dh
        dhb_ref[...] = (out_scale * dh).astype(BF16)
        part = jnp.sum(dnv * hv * r, axis=0, keepdims=True)

        @pl.when(pl.program_id(0) == 0)
        def _():
            dg_ref[...] = part

        @pl.when(pl.program_id(0) > 0)
        def _():
            dg_ref[...] += part

    return _pcall(body, name=name,
                  out_shape=(jax.ShapeDtypeStruct((t, d), F32), jax.ShapeDtypeStruct((t, d), BF16),
                             jax.ShapeDtypeStruct((1, d), F32)),
                  grid=(t // tt,), in_specs=[row, row, vec, row], out_specs=(row, row, vec),
                  dims=("arbitrary",))(dn, h, gain, dh_in)


def _swiglu_fwd(g, u, name):
    t, f = g.shape
    tt = _tile(t, 256, 16)
    row = pl.BlockSpec((tt, f), lambda i: (i, 0))

    def body(g_ref, u_ref, a_ref):
        gv = g_ref[...]
        a_ref[...] = (gv * _sigmoid(gv) * u_ref[...]).astype(BF16)

    return _pcall(body, name=name, out_shape=jax.ShapeDtypeStruct((t, f), BF16), grid=(t // tt,),
                  in_specs=[row, row], out_specs=row, dims=("parallel",))(g, u)


def _swiglu_bwd(da, g, u, name):
    t, f = g.shape
    tt = _tile(t, 256, 16)
    row = pl.BlockSpec((tt, f), lambda i: (i, 0))

    def body(da_ref, g_ref, u_ref, dg_ref, du_ref):
        gv = g_ref[...]
        dav = da_ref[...].astype(F32)
        sg = _sigmoid(gv)
        dg_ref[...] = (dav * u_ref[...] * (sg * (1.0 + gv * (1.0 - sg)))).astype(BF16)
        du_ref[...] = (dav * (gv * sg)).astype(BF16)

    return _pcall(body, name=name, out_shape=(jax.ShapeDtypeStruct((t, f), BF16),) * 2, grid=(t // tt,),
                  in_specs=[row, row, row], out_specs=(row, row), dims=("parallel",))(da, g, u)


def _rope_tables(pos_col, inv2, sign2):
    t = pos_col.shape[0]

    def body(p_ref, inv_ref, sg_ref, c_ref, s_ref):
        ang = p_ref[...].astype(F32) * inv_ref[...]
        c_ref[...] = jnp.cos(ang)
        s_ref[...] = jnp.sin(ang) * sg_ref[...]

    full = lambda shape: pl.BlockSpec(shape, lambda i: (0, 0))
    return _pcall(body, name="rope_tables", out_shape=(jax.ShapeDtypeStruct((t, RET_DIM), F32),) * 2, grid=(1,),
                  in_specs=[full((t, 1)), full((1, RET_DIM)), full((1, RET_DIM))],
                  out_specs=(full((t, RET_DIM)),) * 2, dims=("arbitrary",))(pos_col, inv2, sign2)


def _rot(x, c, s):
    return x * c + pltpu.roll(x, RET_DIM // 2, 1) * s


def _rot_t(g, c, s):
    return g * c + pltpu.roll(g * s, RET_DIM // 2, 1)


def _ret_consts():
    hh = np.arange(RET_HEADS, dtype=np.float32)
    log_gamma = np.log1p(-np.exp2(-5.0 - hh)).astype(np.float32)
    idx = np.arange(CHUNK, dtype=np.float32)
    diff = idx[:, None] - idx[None, :]
    dmask = np.where(diff >= 0, np.exp(log_gamma[:, None, None] * np.maximum(diff, 0.0)), 0.0).astype(np.float32)
    kdec = np.exp(log_gamma[:, None] * (CHUNK - 1 - idx)).astype(np.float32)
    qdec = np.exp(log_gamma[:, None] * (idx + 1.0)).astype(np.float32)
    cdec = np.exp(log_gamma * CHUNK).astype(np.float32)
    bc = lambda v: np.ascontiguousarray(np.broadcast_to(v[:, :, None], (RET_HEADS, CHUNK, RET_DIM)))
    cd = np.ascontiguousarray(np.broadcast_to(cdec[:, None, None], (RET_HEADS, 8, RET_DIM)))
    return jnp.asarray(dmask), jnp.asarray(bc(qdec)), jnp.asarray(bc(kdec)), jnp.asarray(cd)


def _ret_fwd(pm, cos2, sin2, consts):
    t = pm.shape[0]
    n_chunks = t // CHUNK
    dmask, qdec, kdec, cd = consts
    scale = RET_DIM ** -0.5

    def col(c0):
        return pl.BlockSpec((CHUNK, RET_DIM), lambda h, n: (n, c0 + h))

    tab = pl.BlockSpec((CHUNK, RET_DIM), lambda h, n: (n, 0))
    head3 = lambda r: pl.BlockSpec((None, r, RET_DIM), lambda h, n: (h, 0, 0))

    def body(q_ref, k_ref, v_ref, g_ref, c_ref, s_ref, dm_ref, qd_ref, kd_ref, cd_ref, y_ref, raw_ref, st_ref, s_acc):
        @pl.when(pl.program_id(1) == 0)
        def _():
            s_acc[...] = jnp.zeros_like(s_acc)

        c, s = c_ref[...], s_ref[...]
        q = _rot(q_ref[...], c, s)
        k = _rot(k_ref[...], c, s) * scale
        vb = v_ref[...].astype(BF16)
        g = g_ref[...]
        s_in = s_acc[...]
        st_ref[...] = s_in
        a = _dot(q.astype(BF16), k.astype(BF16), 1, 1) * dm_ref[...]
        y = _dot(a.astype(BF16), vb, 1, 0) + _dot((q * qd_ref[...]).astype(BF16), s_in.astype(BF16), 1, 0)
        s_acc[...] = cd_ref[0:1, :] * s_in + _dot((k * kd_ref[...]).astype(BF16), vb, 0, 0)
        raw_ref[...] = y
        mu = jnp.mean(y, axis=-1, keepdims=True)
        yc = y - mu
        rs = lax.rsqrt(jnp.mean(yc * yc, axis=-1, keepdims=True) + EPS)
        y_ref[...] = (yc * rs * (g * _sigmoid(g))).astype(BF16)

    out_blk = pl.BlockSpec((CHUNK, RET_DIM), lambda h, n: (n, h))
    return _pcall(
        body, name="retention_fwd",
        out_shape=(jax.ShapeDtypeStruct((t, RET_WIDTH), BF16), jax.ShapeDtypeStruct((t, RET_WIDTH), F32),
                   jax.ShapeDtypeStruct((RET_HEADS, n_chunks, RET_DIM, RET_DIM), F32)),
        grid=(RET_HEADS, n_chunks),
        in_specs=[col(TILE_RQ), col(TILE_RK), col(TILE_RV), col(TILE_RG), tab, tab,
                  pl.BlockSpec((None, CHUNK, CHUNK), lambda h, n: (h, 0, 0)), head3(CHUNK), head3(CHUNK), head3(8)],
        out_specs=(out_blk, out_blk, pl.BlockSpec((None, None, RET_DIM, RET_DIM), lambda h, n: (h, n, 0, 0))),
        scratch_shapes=[pltpu.VMEM((RET_DIM, RET_DIM), F32)],
        dims=("parallel", "arbitrary"),
    )(pm, pm, pm, pm, cos2, sin2, dmask, qdec, kdec, cd)


def _ret_bwd(dy, pm, cos2, sin2, raw, states, consts):
    t = pm.shape[0]
    n_chunks = t // CHUNK
    dmask, qdec, kdec, cd = consts
    scale = RET_DIM ** -0.5
    rev = lambda n: n_chunks - 1 - n

    def col(c0):
        return pl.BlockSpec((CHUNK, RET_DIM), lambda h, n: (rev(n), c0 + h))

    tab = pl.BlockSpec((CHUNK, RET_DIM), lambda h, n: (rev(n), 0))
    blk = pl.BlockSpec((CHUNK, RET_DIM), lambda h, n: (rev(n), h))
    head3 = lambda r: pl.BlockSpec((None, r, RET_DIM), lambda h, n: (h, 0, 0))

    def body(dy_ref, q_ref, k_ref, v_ref, g_ref, c_ref, s_ref, raw_ref, st_ref, dm_ref, qd_ref, kd_ref, cd_ref,
             dq_ref, dk_ref, dv_ref, dg_ref, ds_acc):
        @pl.when(pl.program_id(1) == 0)
        def _():
            ds_acc[...] = jnp.zeros_like(ds_acc)

        c, s = c_ref[...], s_ref[...]
        q = _rot(q_ref[...], c, s)
        k = _rot(k_ref[...], c, s) * scale
        qb, kb, vb = q.astype(BF16), k.astype(BF16), v_ref[...].astype(BF16)
        g = g_ref[...]
        dm, qd, kd = dm_ref[...], qd_ref[...], kd_ref[...]
        y = raw_ref[...]
        mu = jnp.mean(y, axis=-1, keepdims=True)
        yc = y - mu
        rs = lax.rsqrt(jnp.mean(yc * yc, axis=-1, keepdims=True) + EPS)
        yn = yc * rs
        sg = _sigmoid(g)
        dyo = dy_ref[...]
        dg_ref[...] = (dyo * yn * (sg * (1.0 + g * (1.0 - sg)))).astype(BF16)
        dyn = dyo * (g * sg)
        dyr = rs * (dyn - jnp.mean(dyn, axis=-1, keepdims=True) - yn * jnp.mean(dyn * yn, axis=-1, keepdims=True))
        dyb = dyr.astype(BF16)
        s_in = st_ref[...].astype(BF16)
        ds_out = ds_acc[...]
        dsb = ds_out.astype(BF16)
        a = _dot(qb, kb, 1, 1) * dm
        da = (_dot(dyb, vb, 1, 1) * dm).astype(BF16)
        kdb = (k * kd).astype(BF16)
        qdb = (q * qd).astype(BF16)
        dv_ref[...] = (_dot(a.astype(BF16), dyb, 0, 0) + _dot(kdb, dsb, 1, 0)).astype(BF16)
        dqh = _dot(da, kb, 1, 0) + _dot(dyb, s_in, 1, 1) * qd
        dkh = _dot(da, qb, 0, 0) + _dot(vb, dsb, 1, 1) * kd
        ds_acc[...] = cd_ref[0:1, :] * ds_out + _dot(qdb, dyb, 0, 0)
        dq_ref[...] = _rot_t(dqh, c, s).astype(BF16)
        dk_ref[...] = (_rot_t(dkh, c, s) * scale).astype(BF16)

    return _pcall(
        body, name="retention_bwd",
        out_shape=(jax.ShapeDtypeStruct((t, RET_WIDTH), BF16),) * 4,
        grid=(RET_HEADS, n_chunks),
        in_specs=[blk, col(TILE_RQ), col(TILE_RK), col(TILE_RV), col(TILE_RG), tab, tab, blk,
                  pl.BlockSpec((None, None, RET_DIM, RET_DIM), lambda h, n: (h, rev(n), 0, 0)),
                  pl.BlockSpec((None, CHUNK, CHUNK), lambda h, n: (h, 0, 0)), head3(CHUNK), head3(CHUNK), head3(8)],
        out_specs=(blk,) * 4,
        scratch_shapes=[pltpu.VMEM((RET_DIM, RET_DIM), F32)],
        dims=("parallel", "arbitrary"),
    )(dy, pm, pm, pm, pm, cos2, sin2, raw, states, dmask, qdec, kdec, cd)


FOX_C_LANE = FOX_DIM
FOX_NEGC_LANE = FOX_DIM + 3
FOX_LSE_LANE = FOX_DIM + 6
FOX_L_LANE = FOX_C_LANE
FOX_ROWSUM_LANE = FOX_C_LANE
FOX_COLSUM_LANE = FOX_NEGC_LANE


def _split3(x):
    hi = x.astype(BF16)
    r1 = x - hi.astype(F32)
    mid = r1.astype(BF16)
    lo = (r1 - mid.astype(F32)).astype(BF16)
    return hi, mid, lo


def _tri_dot(tri, x):
    hi, mid, lo = _split3(x)
    return _dot(tri, lo, 1, 0) + _dot(tri, mid, 1, 0) + _dot(tri, hi, 1, 0)


def _log_sigmoid(z):
    return jnp.minimum(z, 0.0) - jnp.log1p(jnp.exp(-jnp.abs(z)))


def _fox_consts():
    place = np.zeros((2, 3, LANES, FOX_TILES), np.float32)
    ones = np.zeros((3, 1, FOX_TILES), np.float32)
    for h in range(FOX_HEADS):
        for part in range(3):
            place[0, part, h, LANES * h + FOX_C_LANE + part] = 1.0
            place[1, part, h, LANES * h + FOX_NEGC_LANE + part] = -1.0
            ones[0, 0, LANES * h + FOX_NEGC_LANE + part] = 1.0
            ones[1, 0, LANES * h + FOX_C_LANE + part] = 1.0
            ones[1, 0, LANES * h + FOX_LSE_LANE + part] = 1.0
            ones[2, 0, LANES * h + FOX_C_LANE + part] = 1.0
    return jnp.asarray(place, BF16), jnp.asarray(ones, F32)


def _fox_prep(pm, bpad):
    t = pm.shape[0]
    tt = _tile(t, 512, LANES)
    place, ones = _fox_consts()
    wide = lambda c0: pl.BlockSpec((tt, FOX_TILES), lambda i: (i, c0 // FOX_HEADS))
    const = lambda a: pl.BlockSpec(a.shape, lambda i: (0,) * a.ndim)

    def body(q_ref, k_ref, v_ref, ff_ref, b_ref, pl_ref, on_ref, qa_ref, ka_ref, va_ref, carry_s):
        @pl.when(pl.program_id(0) == 0)
        def _():
            carry_s[...] = jnp.zeros_like(carry_s)

        r = lax.broadcasted_iota(jnp.int32, (LANES, LANES), 0)
        cc = lax.broadcasted_iota(jnp.int32, (LANES, LANES), 1)
        tri = jnp.where(cc <= r, 1.0, 0.0).astype(BF16)
        bias = b_ref[...]
        for sub in range(tt // LANES):
            rows = pl.ds(sub * LANES, LANES)
            cs = _tri_dot(tri, _log_sigmoid(ff_ref[rows, :] + bias)) + carry_s[...]
            carry_s[...] = cs[LANES - 1:LANES, :]
            parts = _split3(cs)
            eq = sum(_dot(part, pl_ref[0, i], 1, 0) for i, part in enumerate(parts))
            ek = sum(_dot(part, pl_ref[1, i], 1, 0) for i, part in enumerate(parts))
            qa_ref[rows, :] = (q_ref[rows, :] * FOX_DIM ** -0.5 + eq + on_ref[0]).astype(BF16)
            ka_ref[rows, :] = (k_ref[rows, :] + ek + on_ref[1]).astype(BF16)
            va_ref[rows, :] = (v_ref[rows, :] + on_ref[2]).astype(BF16)

    out = pl.BlockSpec((tt, FOX_TILES), lambda i: (i, 0))
    return _pcall(body, name="fox_prep", out_shape=(jax.ShapeDtypeStruct((t, FOX_TILES), BF16),) * 3, grid=(t // tt,),
                  in_specs=[wide(TILE_FQ), wide(TILE_FK), wide(TILE_FV), pl.BlockSpec((tt, LANES), lambda i: (i, TILE_FF)),
                            pl.BlockSpec((1, LANES), lambda i: (0, 0)), const(place), const(ones)],
                  out_specs=(out,) * 3, scratch_shapes=[pltpu.VMEM((1, LANES), F32)],
                  dims=("arbitrary",))(pm, pm, pm, pm, bpad, place, ones)


def _fox_post(dc, pm, bpad):
    t = pm.shape[0]
    nb = t // LANES

    def body(dc_ref, ff_ref, b_ref, d_ref, db_ref):
        r = lax.broadcasted_iota(jnp.int32, (LANES, LANES), 0)
        cc = lax.broadcasted_iota(jnp.int32, (LANES, LANES), 1)
        tri = jnp.where(cc >= r, 1.0, 0.0).astype(BF16)
        bias = b_ref[...]

        def step(i, carry):
            tail, acc = carry
            rows = pl.ds(pl.multiple_of((nb - 1 - i) * LANES, LANES), LANES)
            cs = _tri_dot(tri, dc_ref[rows, :]) + tail
            dff = cs * _sigmoid(-(ff_ref[rows, :] + bias))
            d_ref[rows, :] = dff.astype(BF16)
            return cs[0:1, :], acc + jnp.sum(dff, axis=0, keepdims=True)

        zero = jnp.zeros((1, LANES), F32)
        _, acc = lax.fori_loop(0, nb, step, (zero, zero))
        db_ref[...] = acc

    return _pcall(body, name="fox_forget_bwd",
                  out_shape=(jax.ShapeDtypeStruct((t, LANES), BF16), jax.ShapeDtypeStruct((1, LANES), F32)), grid=(1,),
                  in_specs=[pl.BlockSpec((t, LANES), lambda i: (0, 0)), pl.BlockSpec((t, LANES), lambda i: (0, TILE_FF)),
                            pl.BlockSpec((1, LANES), lambda i: (0, 0))],
                  out_specs=(pl.BlockSpec((t, LANES), lambda i: (0, 0)), pl.BlockSpec((1, LANES), lambda i: (0, 0))),
                  dims=("arbitrary",))(dc, pm, bpad)


def _tri_tables(nb, q_major):
    pairs = [(i, j) for i in range(nb) for j in range(i + 1)] if q_major else \
            [(i, j) for j in range(nb) for i in range(j, nb)]
    return jnp.asarray([a for a, _ in pairs], jnp.int32), jnp.asarray([b for _, b in pairs], jnp.int32)


def _causal(s):
    n = s.shape[0]
    row = lax.broadcasted_iota(jnp.int32, (n, n), 0)
    col = lax.broadcasted_iota(jnp.int32, (n, n), 1)
    return jnp.where(col <= row, s, NEG)


def _lane_col(x, lane):
    sel = lax.broadcasted_iota(jnp.int32, x.shape, 1) == lane
    return jnp.sum(jnp.where(sel, x, 0.0), axis=1, keepdims=True)


def _fox_fwd(qa, ka, va, blk):
    t = qa.shape[0]
    nb = t // blk
    qi, kj = _tri_tables(nb, True)
    q_spec = pl.BlockSpec((blk, LANES), lambda h, s, qi_r, kj_r: (qi_r[s], h))
    k_spec = pl.BlockSpec((blk, LANES), lambda h, s, qi_r, kj_r: (kj_r[s], h))

    def body(qi_r, kj_r, q_ref, k_ref, v_ref, o_ref, ob_ref, qb_ref, m_s, acc_s):
        s_id = pl.program_id(1)
        i, j = qi_r[s_id], kj_r[s_id]

        @pl.when(j == 0)
        def _():
            m_s[...] = jnp.full_like(m_s, NEG)
            acc_s[...] = jnp.zeros_like(acc_s)

        def tile(diagonal):
            s = _dot(q_ref[...], k_ref[...], 1, 1)
            if diagonal:
                s = _causal(s)
            m_old = m_s[...]
            m_new = jnp.maximum(m_old, jnp.max(s, axis=1, keepdims=True))
            p = jnp.exp(s - jnp.tile(m_new, (1, blk // LANES)))
            acc_s[...] = jnp.exp(m_old - m_new) * acc_s[...] + _dot(p.astype(BF16), v_ref[...], 1, 0)
            m_s[...] = m_new

        @pl.when(j < i)
        def _():
            tile(False)

        @pl.when(j == i)
        def _():
            tile(True)
            acc = acc_s[...]
            l = _lane_col(acc, FOX_L_LANE)
            o = acc / l
            o_ref[...] = o
            ob_ref[...] = o.astype(BF16)
            hi, mid, lo = _split3(-(m_s[:, 0:1] + jnp.log(l)))
            lane = lax.broadcasted_iota(jnp.int32, acc.shape, 1)
            qb_ref[...] = jnp.where(lane == FOX_LSE_LANE, hi,
                                    jnp.where(lane == FOX_LSE_LANE + 1, mid,
                                              jnp.where(lane == FOX_LSE_LANE + 2, lo, q_ref[...])))

    wide = (t, FOX_TILES)
    return pl.pallas_call(
        body, name="fox_fwd",
        out_shape=(jax.ShapeDtypeStruct(wide, F32), jax.ShapeDtypeStruct(wide, BF16), jax.ShapeDtypeStruct(wide, BF16)),
        grid_spec=pltpu.PrefetchScalarGridSpec(
            num_scalar_prefetch=2, grid=(FOX_HEADS, qi.shape[0]), in_specs=[q_spec, k_spec, k_spec],
            out_specs=(q_spec,) * 3,
            scratch_shapes=[pltpu.VMEM((blk, LANES), F32), pltpu.VMEM((blk, LANES), F32)]),
        compiler_params=_params(("parallel", "arbitrary")),
    )(qi, kj, qa, ka, va)


def _fox_bwd(qa, ka, va, do, o, blk):
    t = qa.shape[0]
    nb = t // blk
    qi, kj = _tri_tables(nb, False)
    q_spec = pl.BlockSpec((blk, LANES), lambda h, s, qi_r, kj_r: (qi_r[s], h))
    k_spec = pl.BlockSpec((blk, LANES), lambda h, s, qi_r, kj_r: (kj_r[s], h))
    head_spec = pl.BlockSpec((t, LANES), lambda h, s, qi_r, kj_r: (0, h))
    head_col = pl.BlockSpec((None, t, 1), lambda h, s, qi_r, kj_r: (h, 0, 0))
    k_col = pl.BlockSpec((None, blk, 1), lambda h, s, qi_r, kj_r: (h, kj_r[s], 0))
    first_spec = pl.BlockSpec((blk, LANES), lambda h, s, qi_r, kj_r: (jnp.where(kj_r[s] == 0, qi_r[s], nb - 1), h))
    n_steps = int(qi.shape[0])

    def body(qi_r, kj_r, q_ref, k_ref, v_ref, do_ref, o_ref, dq_ref, dk_ref, dv_ref, rs_ref, cs_ref,
             doa_s, dq_s, dk_s, dv_s):
        s_id = pl.program_id(1)
        i, j = qi_r[s_id], kj_r[s_id]
        rows = pl.ds(pl.multiple_of(i * blk, blk), blk)

        @pl.when(j == 0)
        def _():
            dof = do_ref[...]
            hi, mid, lo = _split3(-jnp.sum(dof * o_ref[...], axis=1, keepdims=True))
            lane = lax.broadcasted_iota(jnp.int32, dof.shape, 1)
            doa = jnp.where(lane == FOX_C_LANE, hi.astype(F32),
                            jnp.where(lane == FOX_C_LANE + 1, mid.astype(F32),
                                      jnp.where(lane == FOX_C_LANE + 2, lo.astype(F32), dof)))
            doa_s[rows, :] = doa.astype(BF16)
            dq_s[rows, :] = jnp.zeros((blk, LANES), F32)

        @pl.when(i == j)
        def _():
            dk_s[...] = jnp.zeros_like(dk_s)
            dv_s[...] = jnp.zeros_like(dv_s)

        def tile(diagonal):
            q, k = q_ref[...], k_ref[...]
            s = _dot(q, k, 1, 1)
            if diagonal:
                s = _causal(s)
            p = jnp.exp(s)
            doa = doa_s[rows, :]
            ds = (p * _dot(doa, v_ref[...], 1, 1)).astype(BF16)
            dv_s[...] += _dot(p.astype(BF16), doa, 0, 0)
            dk_s[...] += _dot(ds, q, 0, 0)
            dq_s[rows, :] += _dot(ds, k, 1, 0)

        @pl.when(i > j)
        def _():
            tile(False)

        @pl.when(i == j)
        def _():
            tile(True)

        @pl.when(i == nb - 1)
        def _():
            dk = dk_s[...]
            dk_ref[...] = dk.astype(BF16)
            dv_ref[...] = dv_s[...].astype(BF16)
            cs_ref[...] = -_lane_col(dk, FOX_COLSUM_LANE)

        @pl.when(s_id == n_steps - 1)
        def _():
            dq = dq_s[...]
            dq_ref[...] = (dq * FOX_DIM ** -0.5).astype(BF16)
            rs_ref[...] = _lane_col(dq, FOX_ROWSUM_LANE)

    wide = jax.ShapeDtypeStruct((t, FOX_TILES), BF16)
    cols = jax.ShapeDtypeStruct((FOX_HEADS, t, 1), F32)
    return pl.pallas_call(
        body, name="fox_bwd", out_shape=(wide, wide, wide, cols, cols),
        grid_spec=pltpu.PrefetchScalarGridSpec(
            num_scalar_prefetch=2, grid=(FOX_HEADS, n_steps),
            in_specs=[q_spec, k_spec, k_spec, first_spec, first_spec],
            out_specs=(head_spec, k_spec, k_spec, head_col, k_col),
            scratch_shapes=[pltpu.VMEM((t, LANES), BF16), pltpu.VMEM((t, LANES), F32), pltpu.VMEM((blk, LANES), F32),
                            pltpu.VMEM((blk, LANES), F32)]),
        compiler_params=_params(("parallel", "arbitrary")),
    )(qi, kj, qa, ka, va, do, o)


def _merge_fwd(gm, bm, za, zb):
    t, d = za.shape
    tt = _tile(t, 256, 16)
    row = pl.BlockSpec((tt, d), lambda i: (i, 0))

    def body(gm_ref, b_ref, za_ref, zb_ref, o_ref):
        ga = _sigmoid(gm_ref[:, :d] + b_ref[:, :d])
        gb = _sigmoid(gm_ref[:, d:] + b_ref[:, d:])
        o_ref[...] = (ga * za_ref[...] + gb * zb_ref[...]).astype(BF16)

    return _pcall(body, name="merge_fwd", out_shape=jax.ShapeDtypeStruct((t, d), BF16), grid=(t // tt,),
                  in_specs=[pl.BlockSpec((tt, 2 * d), lambda i: (i, 0)), pl.BlockSpec((1, 2 * d), lambda i: (0, 0)), row, row],
                  out_specs=row, dims=("parallel",))(gm, bm, za, zb)


def _merge_bwd(dmix, gm, bm, za, zb):
    t, d = za.shape
    tt = _tile(t, 256, 16)
    row = pl.BlockSpec((tt, d), lambda i: (i, 0))
    wide = pl.BlockSpec((tt, 2 * d), lambda i: (i, 0))
    vec = pl.BlockSpec((1, 2 * d), lambda i: (0, 0))

    def body(dm_ref, gm_ref, b_ref, za_ref, zb_ref, dza_ref, dzb_ref, dgm_ref, db_ref):
        dm = dm_ref[...]
        ga = _sigmoid(gm_ref[:, :d] + b_ref[:, :d])
        gb = _sigmoid(gm_ref[:, d:] + b_ref[:, d:])
        dza_ref[...] = (dm * ga).astype(BF16)
        dzb_ref[...] = (dm * gb).astype(BF16)
        dla = dm * za_ref[...] * ga * (1.0 - ga)
        dlb = dm * zb_ref[...] * gb * (1.0 - gb)
        dgm_ref[:, :d] = dla.astype(BF16)
        dgm_ref[:, d:] = dlb.astype(BF16)
        pa = jnp.sum(dla, axis=0, keepdims=True)
        pb = jnp.sum(dlb, axis=0, keepdims=True)

        @pl.when(pl.program_id(0) == 0)
        def _():
            db_ref[:, :d] = pa
            db_ref[:, d:] = pb

        @pl.when(pl.program_id(0) > 0)
        def _():
            db_ref[:, :d] += pa
            db_ref[:, d:] += pb

    return _pcall(body, name="merge_bwd",
                  out_shape=(jax.ShapeDtypeStruct((t, d), BF16), jax.ShapeDtypeStruct((t, d), BF16),
                             jax.ShapeDtypeStruct((t, 2 * d), BF16), jax.ShapeDtypeStruct((1, 2 * d), F32)),
                  grid=(t // tt,), in_specs=[row, wide, vec, row, row], out_specs=(row, row, wide, vec),
                  dims=("arbitrary",))(dmix, gm, bm, za, zb)


def _ple_final(h3, pgl, pe, gain, target):
    t, d = h3.shape
    tt = _tile(t, 256, 16)
    row = pl.BlockSpec((tt, d), lambda i: (i, 0))
    vec = pl.BlockSpec((1, d), lambda i: (0, 0))
    lvec = pl.BlockSpec((1, LANES), lambda i: (0, 0))

    def body(h_ref, pgl_ref, pe_ref, g_ref, t_ref, dh_ref, dsg_ref, dpe_ref, loss_ref, dg_ref):
        pg = _sigmoid(pgl_ref[...])
        pe_v = pe_ref[...]
        h4 = h_ref[...] + pg * pe_v
        r = lax.rsqrt(jnp.mean(h4 * h4, axis=-1, keepdims=True) + EPS)
        gv = g_ref[...]
        err = h4 * r * gv - t_ref[...]
        part_loss = 0.5 * jnp.sum(jnp.mean(err * err, axis=-1, keepdims=True), axis=0, keepdims=True)
        dy = err * (1.0 / d)
        part_g = jnp.sum(dy * h4 * r, axis=0, keepdims=True)
        dyg = dy * gv
        dh = r * dyg - h4 * (r * r * r) * jnp.mean(dyg * h4, axis=-1, keepdims=True)
        dh_ref[...] = dh
        dsg_ref[...] = (dh * pe_v * pg * (1.0 - pg)).astype(BF16)
        dpe_ref[...] = (dh * pg).astype(BF16)

        @pl.when(pl.program_id(0) == 0)
        def _():
            loss_ref[...] = jnp.broadcast_to(part_loss, (1, LANES))
            dg_ref[...] = part_g

        @pl.when(pl.program_id(0) > 0)
        def _():
            loss_ref[...] += jnp.broadcast_to(part_loss, (1, LANES))
            dg_ref[...] += part_g

    return _pcall(body, name="ple_final",
                  out_shape=(jax.ShapeDtypeStruct((t, d), F32), jax.ShapeDtypeStruct((t, d), BF16),
                             jax.ShapeDtypeStruct((t, d), BF16), jax.ShapeDtypeStruct((1, LANES), F32),
                             jax.ShapeDtypeStruct((1, d), F32)),
                  grid=(t // tt,), in_specs=[row, row, row, vec, row], out_specs=(row, row, row, lvec, vec),
                  dims=("arbitrary",))(h3, pgl, pe, gain, target)


def _adamw_math(w, g, m, v):
    m = ADAM_B1 * m + (1.0 - ADAM_B1) * g
    v = ADAM_B2 * v + (1.0 - ADAM_B2) * (g * g)
    m_hat = m / (1.0 - ADAM_B1 ** ADAM_STEP)
    v_hat = v / (1.0 - ADAM_B2 ** ADAM_STEP)
    delta = -ADAM_LR * (m_hat / (jnp.sqrt(v_hat) + ADAM_EPS) + ADAM_WD * w)
    return delta, m, v


def _adamw(parts, w, m, v, name):
    n, r, c = parts.shape
    tr = _tile(r, 256, 16)
    row = pl.BlockSpec((tr, c), lambda i: (i, 0))

    def body(p_ref, w_ref, m_ref, v_ref, g_ref, d_ref, mo_ref, vo_ref):
        g = p_ref[0].astype(F32)
        for s in range(1, n):
            g = g + p_ref[s].astype(F32)
        g_ref[...] = g
        d_ref[...], mo_ref[...], vo_ref[...] = _adamw_math(w_ref[...], g, m_ref[...], v_ref[...])

    return _pcall(body, name=name, out_shape=(jax.ShapeDtypeStruct((r, c), F32),) * 4, grid=(r // tr,),
                  in_specs=[pl.BlockSpec((n, tr, c), lambda i: (0, i, 0)), row, row, row], out_specs=(row,) * 4,
                  dims=("parallel",))(parts, w, m, v)


ANY = pl.BlockSpec(memory_space=pl.ANY)


def _all_gather(shards):
    n = len(shards)

    def body(*refs):
        x_refs, out_refs = refs[:n], refs[n:2 * n]
        send_sems, recv_sems, local_sems = refs[2 * n:]
        x, y, cc = lax.axis_index("x"), lax.axis_index("y"), lax.axis_index("c")
        me, sibling = (x, y, cc), (x, y, 1 - cc)
        chips = [(1 - x, y), (x, 1 - y), (1 - x, 1 - y)]

        def slot(a, px, py, pc):
            return out_refs[a].at[4 * px + 2 * py + pc]

        def copy(a, k, block, to, src=None):
            return pltpu.make_async_remote_copy(
                src_ref=slot(a, *block) if src is None else src, dst_ref=slot(a, *block),
                send_sem=send_sems.at[7 * a + k], recv_sem=recv_sems.at[7 * a + k], device_id=to, device_id_type=MESH)

        local, sent = [], []
        for a in range(n):
            local.append(pltpu.make_async_copy(x_refs[a], slot(a, *me), local_sems.at[a]))
            sent.append(copy(a, 0, me, sibling, src=x_refs[a]))
            sent += [copy(a, 1 + j, me, (*chip, cc), src=x_refs[a]) for j, chip in enumerate(chips)]
        for cp in local + sent:
            cp.start()
        for j, chip in enumerate(chips):
            for a in range(n):
                copy(a, 1 + j, (*chip, cc), me).wait_recv()
                sent.append(copy(a, 4 + j, (*chip, cc), sibling))
                sent[-1].start()
        for a in range(n):
            copy(a, 0, sibling, me).wait_recv()
            for j, chip in enumerate(chips):
                copy(a, 4 + j, (*chip, 1 - cc), me).wait_recv()
        for cp in sent:
            cp.wait_send()
        for cp in local:
            cp.wait()

    return pl.pallas_call(
        body, name="weights_all_gather",
        out_shape=tuple(jax.ShapeDtypeStruct((N_DEV,) + s.shape, s.dtype) for s in shards),
        in_specs=[ANY] * n, out_specs=(ANY,) * n,
        scratch_shapes=[pltpu.SemaphoreType.DMA((7 * n,)), pltpu.SemaphoreType.DMA((7 * n,)),
                        pltpu.SemaphoreType.DMA((n,))],
    )(*shards)


def _reduce_scatter_exchange(blocks):
    n = len(blocks)

    def body(*refs):
        g_refs, recv_refs = refs[:n], refs[n:2 * n]
        send_sems, recv_sems, local_sems = refs[2 * n:]
        x, y, cc = lax.axis_index("x"), lax.axis_index("y"), lax.axis_index("c")
        me = 4 * x + 2 * y + cc
        local, sent, landing = [], [], []
        for a in range(n):
            local.append(pltpu.make_async_copy(g_refs[a].at[me], recv_refs[a].at[me], local_sems.at[a]))
        for k in range(1, N_DEV):
            px, py, pc = x ^ (k >> 2), y ^ ((k >> 1) & 1), cc ^ (k & 1)
            peer = 4 * px + 2 * py + pc
            for a in range(n):
                sems = dict(send_sem=send_sems.at[7 * a + k - 1], recv_sem=recv_sems.at[7 * a + k - 1],
                            device_id=(px, py, pc), device_id_type=MESH)
                sent.append(pltpu.make_async_remote_copy(src_ref=g_refs[a].at[peer], dst_ref=recv_refs[a].at[me], **sems))
                landing.append(pltpu.make_async_remote_copy(src_ref=g_refs[a].at[me], dst_ref=recv_refs[a].at[peer], **sems))
        for cp in local + sent:
            cp.start()
        for cp in landing:
            cp.wait_recv()
        for cp in sent:
            cp.wait_send()
        for cp in local:
            cp.wait()

    return pl.pallas_call(
        body, name="grads_reduce_scatter_exchange",
        out_shape=tuple(jax.ShapeDtypeStruct(b.shape, b.dtype) for b in blocks),
        in_specs=[ANY] * n, out_specs=(ANY,) * n,
        scratch_shapes=[pltpu.SemaphoreType.DMA((7 * n,)), pltpu.SemaphoreType.DMA((7 * n,)),
                        pltpu.SemaphoreType.DMA((n,))],
    )(*blocks)


BIG = (("w_ffn1_gate", "colT"), ("w_ffn1_up", "colT"), ("w_ffn1_down", "row"), ("w_ffn2_gate", "colT"),
       ("w_ffn2_up", "colT"), ("w_ffn2_down", "row"), ("w_in", "colT"), ("w_merge", "col"), ("w_ret_out", "col"),
       ("w_fox_out", "col"), ("w_out", "row"), ("w_ple", "col"), ("w_ple_gate", "row"))


def _shard_view(a, kind):
    a = a.reshape(a.shape[-2:])
    return a.T if kind == "colT" else a


def _unview(a, kind, shape):
    return (a.T if kind == "colT" else a).reshape(shape)


def _full_from_slots(g, kind):
    n, r, c = g.shape
    return g.transpose(1, 0, 2).reshape(r, n * c) if kind == "col" else g.reshape(n * r, c)


def _slots_from_full(f, kind):
    r, c = f.shape
    return f.reshape(r, N_DEV, c // N_DEV).transpose(1, 0, 2) if kind == "col" else f.reshape(N_DEV, r // N_DEV, c)


def _pad_heads(w):
    d = w.shape[1]
    return jnp.pad(w.reshape(FOX_HEADS, FOX_DIM, d), ((0, 0), (0, LANES - FOX_DIM), (0, 0))).reshape(FOX_TILES, d)


def _unpad_heads(w):
    d = w.shape[1]
    return w.reshape(FOX_HEADS, LANES, d)[:, :FOX_DIM].reshape(FOX_WIDTH, d)


def _deinterleave_rows(w):
    d = w.shape[1]
    return w.reshape(RET_HEADS, RET_DIM // 2, 2, d).transpose(0, 2, 1, 3).reshape(RET_WIDTH, d)


def _interleave_rows(w):
    d = w.shape[1]
    return w.reshape(RET_HEADS, 2, RET_DIM // 2, d).transpose(0, 2, 1, 3).reshape(RET_WIDTH, d)


def _pad_w_in(wt):
    d = wt.shape[1]
    rw, fw = RET_WIDTH, FOX_WIDTH
    fo = 4 * rw
    return jnp.concatenate([
        _deinterleave_rows(wt[:rw]), _deinterleave_rows(wt[rw:2 * rw]), wt[2 * rw:4 * rw],
        _pad_heads(wt[fo:fo + fw]), _pad_heads(wt[fo + fw:fo + 2 * fw]), _pad_heads(wt[fo + 2 * fw:fo + 3 * fw]),
        wt[fo + 3 * fw:], jnp.zeros((2 * LANES - FOX_HEADS, d), wt.dtype)], axis=0)


def _unpad_w_in(g):
    rw = RET_WIDTH
    f0 = 4 * rw
    return jnp.concatenate([
        _interleave_rows(g[:rw]), _interleave_rows(g[rw:2 * rw]), g[2 * rw:4 * rw],
        _unpad_heads(g[f0:f0 + FOX_TILES]), _unpad_heads(g[f0 + FOX_TILES:f0 + 2 * FOX_TILES]),
        _unpad_heads(g[f0 + 2 * FOX_TILES:f0 + 3 * FOX_TILES]),
        g[f0 + 3 * FOX_TILES:f0 + 3 * FOX_TILES + FOX_HEADS]], axis=0)


SMALL = ("ln_ffn1", "ln_mix", "b_forget", "b_merge", "ln_ffn2", "ln_ple", "ln_final")


def _small_rows(n):
    rows = -(-n // LANES)
    return -(-rows // 8) * 8


def _pack_small(vals, with_loss=None):
    parts = []
    for name in SMALL:
        v = vals[name].reshape(-1).astype(F32)
        rows = _small_rows(v.shape[0])
        parts.append(jnp.pad(v, (0, rows * LANES - v.shape[0])).reshape(rows, LANES))
    if with_loss is not None:
        parts.append(jnp.pad(with_loss.reshape(1, LANES), ((0, 7), (0, 0))))
    else:
        parts.append(jnp.zeros((8, LANES), F32))
    return jnp.concatenate(parts, axis=0)


def _unpack_small(packed, shapes):
    out, at = {}, 0
    for name in SMALL:
        n = int(np.prod(shapes[name]))
        rows = _small_rows(n)
        out[name] = packed[at:at + rows].reshape(-1)[:n].reshape(shapes[name])
        at += rows
    return out, packed[at, 0]


def _local_step(x, p, positions, target, w, small):
    t, d = x.shape
    gain = lambda n: small[n].reshape(1, d)
    w_in_t = _pad_w_in(w["w_in"])
    w_fox_pad = _pad_heads(w["w_fox_out"])
    bpad = jnp.pad(small["b_forget"].reshape(1, FOX_HEADS), ((0, 0), (0, LANES - FOX_HEADS)))
    bm = small["b_merge"].reshape(1, 2 * d)
    fox_blk = _tile(t, 512, 128)

    def ffn_fwd(n, tag):
        g = _mm([(n, w[f"w_{tag}_gate"], "nt")], F32, f"{tag}_gate", tn=1408)
        u = _mm([(n, w[f"w_{tag}_up"], "nt")], F32, f"{tag}_up", tn=1408)
        a = _swiglu_fwd(g, u, f"{tag}_swiglu")
        return g, u, a, _mm([(a, w[f"w_{tag}_down"], "nn")], F32, f"{tag}_down")

    n1 = _rms_fwd(x, gain("ln_ffn1"), "rms_ffn1")
    g1, u1, a1, f1 = ffn_fwd(n1, "ffn1")
    h1, u = _rms_fwd(x, gain("ln_mix"), "rms_mix", f=f1)
    gm = _mm([(u, w["w_merge"], "nn")], F32, "mixer_gates")
    pm = _mm([(u, w_in_t, "nt")], F32, "mixer_in", tn=1792)

    half = jnp.arange(RET_DIM // 2, dtype=F32) / (RET_DIM // 2)
    inv = 1.0 / (ROPE_BASE ** half)
    inv2 = jnp.concatenate([inv, inv]).reshape(1, RET_DIM)
    sign2 = jnp.concatenate([-jnp.ones((RET_DIM // 2,), F32), jnp.ones((RET_DIM // 2,), F32)]).reshape(1, RET_DIM)
    cos2, sin2 = _rope_tables(positions.reshape(t, 1), inv2, sign2)
    consts = _ret_consts()
    y_ret, y_raw, states = _ret_fwd(pm, cos2, sin2, consts)
    za = _mm([(y_ret, w["w_ret_out"], "nn")], F32, "ret_out")

    qa, ka, va = _fox_prep(pm, bpad)
    o_fox, y_fox, qa_b = _fox_fwd(qa, ka, va, fox_blk)
    zb = _mm([(y_fox, w_fox_pad, "nn")], F32, "fox_out")

    mix = _merge_fwd(gm, bm, za, zb)
    mo = _mm([(mix, w["w_out"], "nn")], F32, "mix_out")
    h2, n2 = _rms_fwd(h1, gain("ln_ffn2"), "rms_ffn2", f=mo, scale=1.0)
    g2, u2, a2, f2 = ffn_fwd(n2, "ffn2")
    h3, n3 = _rms_fwd(h2, gain("ln_ple"), "rms_ple", f=f2)
    pgl = _mm([(n3, w["w_ple_gate"], "nn")], F32, "ple_gate")
    pb = p.astype(BF16)
    pe = _mm([(pb, w["w_ple"], "nn")], F32, "ple_embed")

    gw, gs = {}, {}
    dh4, dsg, dpe, loss, gs["ln_final"] = _ple_final(h3, pgl, pe, gain("ln_final"), target)
    gw["w_ple_gate"] = _mm([(n3, dsg, "tn")], BF16, "d_w_ple_gate", tn=256)
    gw["w_ple"] = _mm([(pb, dpe, "tn")], BF16, "d_w_ple", tn=256)
    dn3 = _mm([(dsg, w["w_ple_gate"], "nt")], F32, "d_n3")
    dh3, dh3_half, gs["ln_ple"] = _rms_bwd(dn3, h3, gain("ln_ple"), dh4, "rms_ple_bwd", 0.5)

    def ffn_bwd(dh_half, g, u_, a, n, tag):
        gw[f"w_{tag}_down"] = _mm([(a, dh_half, "tn")], BF16, f"d_w_{tag}_down", tm=1408, tn=256)
        da = _mm([(dh_half, w[f"w_{tag}_down"], "nt")], F32, f"d_a_{tag}", tn=1408)
        dg, du_ = _swiglu_bwd(da, g, u_, f"{tag}_swiglu_bwd")
        gw[f"w_{tag}_gate"] = _mm([(dg, n, "tn")], BF16, f"d_w_{tag}_gate", tm=1408, tn=256)
        gw[f"w_{tag}_up"] = _mm([(du_, n, "tn")], BF16, f"d_w_{tag}_up", tm=1408, tn=256)
        return _mm([(dg, w[f"w_{tag}_gate"], "nn"), (du_, w[f"w_{tag}_up"], "nn")], F32, f"d_n_{tag}", tm=512)

    dn2 = ffn_bwd(dh3_half, g2, u2, a2, n2, "ffn2")
    dh2, dh2_b, gs["ln_ffn2"] = _rms_bwd(dn2, h2, gain("ln_ffn2"), dh3, "rms_ffn2_bwd", 1.0)

    gw["w_out"] = _mm([(mix, dh2_b, "tn")], BF16, "d_w_out", tn=256)
    dmix = _mm([(dh2_b, w["w_out"], "nt")], F32, "d_mix")
    dza, dzb, dgm, gs["b_merge"] = _merge_bwd(dmix, gm, bm, za, zb)
    gw["w_ret_out"] = _mm([(y_ret, dza, "tn")], BF16, "d_w_ret_out", tn=256)
    gw["w_fox_out"] = _unpad_heads(_mm([(y_fox, dzb, "tn")], BF16, "d_w_fox_out", tn=256))
    dy_ret = _mm([(dza, w["w_ret_out"], "nt")], F32, "d_y_ret")
    do_fox = _mm([(dzb, w_fox_pad, "nt")], F32, "d_y_fox")

    drq, drk, drv, drg = _ret_bwd(dy_ret, pm, cos2, sin2, y_raw, states, consts)

    dqa, dka, dva, ds_rows, ds_cols = _fox_bwd(qa_b, ka, va, do_fox, o_fox, fox_blk)
    dc = jnp.pad((ds_rows + ds_cols).reshape(FOX_HEADS, t).T, ((0, 0), (0, LANES - FOX_HEADS)))
    dff, db_forget = _fox_post(dc, pm, bpad)
    gs["b_forget"] = db_forget[:, :FOX_HEADS]

    dpm = jnp.concatenate([drq, drk, drv, drg, dqa, dka, dva, dff, jnp.zeros((t, LANES), BF16)], axis=1)
    gw["w_merge"] = _mm([(u, dgm, "tn")], BF16, "d_w_merge", tn=512)
    gw["w_in"] = _unpad_w_in(_mm([(dpm, u, "tn")], BF16, "d_w_in", tm=1792, tn=256))
    du = _mm([(dpm, w_in_t, "nn"), (dgm, w["w_merge"], "nt")], F32, "d_u", tm=512, tn=512)
    dh1, dh1_half, gs["ln_mix"] = _rms_bwd(du, h1, gain("ln_mix"), dh2, "rms_mix_bwd", 0.5)

    dn1 = ffn_bwd(dh1_half, g1, u1, a1, n1, "ffn1")
    dx, _, gs["ln_ffn1"] = _rms_bwd(dn1, x, gain("ln_ffn1"), dh1, "rms_ffn1_bwd", 1.0)
    return loss, dx, gw, gs


WEIGHTS = ("ln_ffn1", "w_ffn1_gate", "w_ffn1_up", "w_ffn1_down", "ln_mix", "w_in", "b_forget", "w_merge", "b_merge",
           "w_ret_out", "w_fox_out", "w_out", "ln_ffn2", "w_ffn2_gate", "w_ffn2_up", "w_ffn2_down", "ln_ple", "w_ple",
           "w_ple_gate", "ln_final")


def kernel(x, p, positions, ln_ffn1, w_ffn1_gate, w_ffn1_up, w_ffn1_down, ln_mix, w_in, b_forget, w_merge, b_merge, w_ret_out, w_fox_out, w_out, ln_ffn2, w_ffn2_gate, w_ffn2_up, w_ffn2_down, ln_ple, w_ple, w_ple_gate, ln_final, loss_target, m_ln_ffn1, m_w_ffn1_gate, m_w_ffn1_up, m_w_ffn1_down, m_ln_mix, m_w_in, m_b_forget, m_w_merge, m_b_merge, m_w_ret_out, m_w_fox_out, m_w_out, m_ln_ffn2, m_w_ffn2_gate, m_w_ffn2_up, m_w_ffn2_down, m_ln_ple, m_w_ple, m_w_ple_gate, m_ln_final, v_ln_ffn1, v_w_ffn1_gate, v_w_ffn1_up, v_w_ffn1_down, v_ln_mix, v_w_in, v_b_forget, v_w_merge, v_b_merge, v_w_ret_out, v_w_fox_out, v_w_out, v_ln_ffn2, v_w_ffn2_gate, v_w_ffn2_up, v_w_ffn2_down, v_ln_ple, v_w_ple, v_w_ple_gate, v_ln_final):
    args = dict(ln_ffn1=ln_ffn1, w_ffn1_gate=w_ffn1_gate, w_ffn1_up=w_ffn1_up, w_ffn1_down=w_ffn1_down, ln_mix=ln_mix, w_in=w_in, b_forget=b_forget, w_merge=w_merge, b_merge=b_merge, w_ret_out=w_ret_out, w_fox_out=w_fox_out, w_out=w_out, ln_ffn2=ln_ffn2, w_ffn2_gate=w_ffn2_gate, w_ffn2_up=w_ffn2_up, w_ffn2_down=w_ffn2_down, ln_ple=ln_ple, w_ple=w_ple, w_ple_gate=w_ple_gate, ln_final=ln_final)
    moms = dict(ln_ffn1=m_ln_ffn1, w_ffn1_gate=m_w_ffn1_gate, w_ffn1_up=m_w_ffn1_up, w_ffn1_down=m_w_ffn1_down, ln_mix=m_ln_mix, w_in=m_w_in, b_forget=m_b_forget, w_merge=m_w_merge, b_merge=m_b_merge, w_ret_out=m_w_ret_out, w_fox_out=m_w_fox_out, w_out=m_w_out, ln_ffn2=m_ln_ffn2, w_ffn2_gate=m_w_ffn2_gate, w_ffn2_up=m_w_ffn2_up, w_ffn2_down=m_w_ffn2_down, ln_ple=m_ln_ple, w_ple=m_w_ple, w_ple_gate=m_w_ple_gate, ln_final=m_ln_final)
    vars_ = dict(ln_ffn1=v_ln_ffn1, w_ffn1_gate=v_w_ffn1_gate, w_ffn1_up=v_w_ffn1_up, w_ffn1_down=v_w_ffn1_down, ln_mix=v_ln_mix, w_in=v_w_in, b_forget=v_b_forget, w_merge=v_w_merge, b_merge=v_b_merge, w_ret_out=v_w_ret_out, w_fox_out=v_w_fox_out, w_out=v_w_out, ln_ffn2=v_ln_ffn2, w_ffn2_gate=v_w_ffn2_gate, w_ffn2_up=v_w_ffn2_up, w_ffn2_down=v_w_ffn2_down, ln_ple=v_ln_ple, w_ple=v_w_ple, w_ple_gate=v_w_ple_gate, ln_final=v_ln_final)
    kinds = ("grad", "delta", "new_m", "new_v")

    gathered = _all_gather([_shard_view(args[n], kind).astype(BF16) for n, kind in BIG])
    w_full = {n: _full_from_slots(g, kind) for (n, kind), g in zip(BIG, gathered)}

    small = {n: args[n] for n in SMALL}
    loss_part, dx, gw, gs = _local_step(x[0], p[0, 0], positions[0], loss_target[0], w_full, small)

    small_part = _pack_small(gs, with_loss=loss_part)
    blocks = [_slots_from_full(gw[n], kind) for n, kind in BIG]
    blocks.append(jnp.broadcast_to(small_part, (N_DEV,) + small_part.shape))
    recv = _reduce_scatter_exchange(blocks)

    res = {}
    for (n, kind), parts in zip(BIG, recv):
        outs = _adamw(parts, _shard_view(args[n], kind), _shard_view(moms[n], kind), _shard_view(vars_[n], kind),
                      f"adamw_{n}")
        for what, o in zip(kinds, outs):
            res[(what, n)] = _unview(o, kind, args[n].shape)
    s_outs = _adamw(recv[-1], _pack_small(small), _pack_small({n: moms[n] for n in SMALL}),
                    _pack_small({n: vars_[n] for n in SMALL}), "adamw_small")
    for what, sm in zip(kinds, s_outs):
        svals, extra = _unpack_small(sm, {n: args[n].shape for n in SMALL})
        if what == "grad":
            loss = extra
        for n in SMALL:
            res[(what, n)] = svals[n]
    return (loss, dx[None], *[res[(what, n)] for what in kinds for n in WEIGHTS])
```

```python
import numpy as np
import jax
import jax.numpy as jnp
from jax import lax
from jax.experimental import pallas as pl
from jax.experimental.pallas import tpu as pltpu

F32 = jnp.float32
BF16 = jnp.bfloat16

N_DEV = 8
EPS = 1e-6
RET_HEADS = 4
RET_DIM = 128
RET_WIDTH = RET_HEADS * RET_DIM
FOX_HEADS = 8
FOX_DIM = 64
FOX_WIDTH = FOX_HEADS * FOX_DIM
CHUNK = 128
ROPE_BASE = 10000.0
LANES = 128
FOX_TILES = FOX_HEADS * LANES
IN_COLS = 4 * RET_WIDTH + 3 * FOX_WIDTH + FOX_HEADS
IN_PAD = 4 * RET_WIDTH + 3 * FOX_TILES + 2 * LANES
TILE_RQ, TILE_RK, TILE_RV, TILE_RG = 0, 4, 8, 12
TILE_FQ, TILE_FK, TILE_FV, TILE_FF = 16, 24, 32, 40
NEG = -1e30

ADAM_LR = 0.001
ADAM_B1 = 0.9
ADAM_B2 = 0.999
ADAM_EPS = 1e-08
ADAM_WD = 0.01
ADAM_STEP = 10

VMEM_LIMIT_BYTES = 56 * 1024 * 1024

MESH = pl.DeviceIdType.MESH


def _tile(dim, pref, mult):
    if dim <= pref:
        return dim
    t = (pref // mult) * mult
    while t >= mult:
        if dim % t == 0:
            return t
        t -= mult
    return dim


def _params(dims):
    return pltpu.CompilerParams(dimension_semantics=dims, vmem_limit_bytes=VMEM_LIMIT_BYTES)


def _pcall(body, *, name, out_shape, grid, in_specs, out_specs, scratch_shapes=(), dims=None):
    return pl.pallas_call(body, name=name, out_shape=out_shape, grid=grid, in_specs=in_specs, out_specs=out_specs,
                          scratch_shapes=list(scratch_shapes), compiler_params=_params(dims))


def _dot(a, b, ca, cb):
    return lax.dot_general(a, b, (((ca,), (cb,)), ((), ())), preferred_element_type=F32)


def _sigmoid(x):
    return 1.0 / (1.0 + jnp.exp(-x))


def _mm(pairs, out_dtype, name, tm=1024, tn=1024):
    dims = []
    for a, b, mode in pairs:
        m, k = (a.shape[1], a.shape[0]) if mode == "tn" else a.shape
        n, k2 = b.shape if mode == "nt" else (b.shape[1], b.shape[0])
        assert k == k2, (name, a.shape, b.shape, mode)
        dims.append((m, n))
    assert all(d == dims[0] for d in dims), (name, dims)
    m, n = dims[0]
    tm = _tile(m, tm, 128 if any(mode == "tn" for _, _, mode in pairs) else 16)
    tn = _tile(n, tn, 128)
    in_specs, contract, operands = [], [], []
    for a, b, mode in pairs:
        k = a.shape[0] if mode == "tn" else a.shape[1]
        in_specs.append(pl.BlockSpec((k, tm), lambda i, j: (0, i)) if mode == "tn" else
                        pl.BlockSpec((tm, k), lambda i, j: (i, 0)))
        in_specs.append(pl.BlockSpec((tn, k), lambda i, j: (j, 0)) if mode == "nt" else
                        pl.BlockSpec((k, tn), lambda i, j: (0, j)))
        contract.append((0 if mode == "tn" else 1, 1 if mode == "nt" else 0))
        operands += [a, b]

    def body(*refs):
        o_ref = refs[-1]
        acc = None
        for p, (ca, cb) in enumerate(contract):
            part = _dot(refs[2 * p][...], refs[2 * p + 1][...], ca, cb)
            acc = part if acc is None else acc + part
        o_ref[...] = acc.astype(out_dtype)

    return _pcall(body, name=name, out_shape=jax.ShapeDtypeStruct((m, n), out_dtype), grid=(m // tm, n // tn),
                  in_specs=in_specs, out_specs=pl.BlockSpec((tm, tn), lambda i, j: (i, j)),
                  dims=("parallel", "parallel"))(*operands)


def _rms_fwd(h, gain, name, f=None, scale=0.5):
    t, d = h.shape
    tt = _tile(t, 512, 16)
    row = pl.BlockSpec((tt, d), lambda i: (i, 0))
    vec = pl.BlockSpec((1, d), lambda i: (0, 0))

    def norm(hv, g_ref, n_ref):
        r = lax.rsqrt(jnp.mean(hv * hv, axis=-1, keepdims=True) + EPS)
        n_ref[...] = (hv * r * g_ref[...]).astype(BF16)

    if f is None:

        def body(h_ref, g_ref, n_ref):
            norm(h_ref[...], g_ref, n_ref)

        return _pcall(body, name=name, out_shape=jax.ShapeDtypeStruct((t, d), BF16), grid=(t // tt,),
                      in_specs=[row, vec], out_specs=row, dims=("parallel",))(h, gain)

    def body(h_ref, f_ref, g_ref, hn_ref, n_ref):
        hv = h_ref[...] + scale * f_ref[...]
        hn_ref[...] = hv
        norm(hv, g_ref, n_ref)

    return _pcall(body, name=name,
                  out_shape=(jax.ShapeDtypeStruct((t, d), F32), jax.ShapeDtypeStruct((t, d), BF16)),
                  grid=(t // tt,), in_specs=[row, row, vec], out_specs=(row, row), dims=("parallel",))(h, f, gain)


def _rms_bwd(dn, h, gain, dh_in, name, out_scale):
    t, d = h.shape
    tt = _tile(t, 512, 16)
    row = pl.BlockSpec((tt, d), lambda i: (i, 0))
    vec = pl.BlockSpec((1, d), lambda i: (0, 0))

    def body(dn_ref, h_ref, g_ref, dhin_ref, dh_ref, dhb_ref, dg_ref):
        hv = h_ref[...]
        dnv = dn_ref[...].astype(F32)
        r = lax.rsqrt(jnp.mean(hv * hv, axis=-1, keepdims=True) + EPS)
        dng = dnv * g_ref[...]
        dh = dhin_ref[...] + r * dng - hv * (r * r * r) * jnp.mean(dng * hv, axis=-1, keepdims=True)
        dh_ref[...] = dh
        dhb_ref[...] = (out_scale * dh).astype(BF16)
        part = jnp.sum(dnv * hv * r, axis=0, keepdims=True)

        @pl.when(pl.program_id(0) == 0)
        def _():
            dg_ref[...] = part

        @pl.when(pl.program_id(0) > 0)
        def _():
            dg_ref[...] += part

    return _pcall(body, name=name,
                  out_shape=(jax.ShapeDtypeStruct((t, d), F32), jax.ShapeDtypeStruct((t, d), BF16),
                             jax.ShapeDtypeStruct((1, d), F32)),
                  grid=(t // tt,), in_specs=[row, row, vec, row], out_specs=(row, row, vec),
                  dims=("arbitrary",))(dn, h, gain, dh_in)


def _swiglu_fwd(g, u, name):
    t, f = g.shape
    tt = _tile(t, 256, 16)
    row = pl.BlockSpec((tt, f), lambda i: (i, 0))

    def body(g_ref, u_ref, a_ref):
        gv = g_ref[...]
        a_ref[...] = (gv * _sigmoid(gv) * u_ref[...]).astype(BF16)

    return _pcall(body, name=name, out_shape=jax.ShapeDtypeStruct((t, f), BF16), grid=(t // tt,),
                  in_specs=[row, row], out_specs=row, dims=("parallel",))(g, u)


def _swiglu_bwd(da, g, u, name):
    t, f = g.shape
    tt = _tile(t, 256, 16)
    row = pl.BlockSpec((tt, f), lambda i: (i, 0))

    def body(da_ref, g_ref, u_ref, dg_ref, du_ref):
        gv = g_ref[...]
        dav = da_ref[...].astype(F32)
        sg = _sigmoid(gv)
        dg_ref[...] = (dav * u_ref[...] * (sg * (1.0 + gv * (1.0 - sg)))).astype(BF16)
        du_ref[...] = (dav * (gv * sg)).astype(BF16)

    return _pcall(body, name=name, out_shape=(jax.ShapeDtypeStruct((t, f), BF16),) * 2, grid=(t // tt,),
                  in_specs=[row, row, row], out_specs=(row, row), dims=("parallel",))(da, g, u)


def _rope_tables(pos_col, inv2, sign2):
    t = pos_col.shape[0]

    def body(p_ref, inv_ref, sg_ref, c_ref, s_ref):
        ang = p_ref[...].astype(F32) * inv_ref[...]
        c_ref[...] = jnp.cos(ang)
        s_ref[...] = jnp.sin(ang) * sg_ref[...]

    full = lambda shape: pl.BlockSpec(shape, lambda i: (0, 0))
    return _pcall(body, name="rope_tables", out_shape=(jax.ShapeDtypeStruct((t, RET_DIM), F32),) * 2, grid=(1,),
                  in_specs=[full((t, 1)), full((1, RET_DIM)), full((1, RET_DIM))],
                  out_specs=(full((t, RET_DIM)),) * 2, dims=("arbitrary",))(pos_col, inv2, sign2)


def _rot(x, c, s):
    return x * c + pltpu.roll(x, RET_DIM // 2, 1) * s


def _rot_t(g, c, s):
    return g * c + pltpu.roll(g * s, RET_DIM // 2, 1)


def _ret_consts():
    hh = np.arange(RET_HEADS, dtype=np.float32)
    log_gamma = np.log1p(-np.exp2(-5.0 - hh)).astype(np.float32)
    idx = np.arange(CHUNK, dtype=np.float32)
    diff = idx[:, None] - idx[None, :]
    dmask = np.where(diff >= 0, np.exp(log_gamma[:, None, None] * np.maximum(diff, 0.0)), 0.0).astype(np.float32)
    kdec = np.exp(log_gamma[:, None] * (CHUNK - 1 - idx)).astype(np.float32)
    qdec = np.exp(log_gamma[:, None] * (idx + 1.0)).astype(np.float32)
    cdec = np.exp(log_gamma * CHUNK).astype(np.float32)
    bc = lambda v: np.ascontiguousarray(np.broadcast_to(v[:, :, None], (RET_HEADS, CHUNK, RET_DIM)))
    cd = np.ascontiguousarray(np.broadcast_to(cdec[:, None, None], (RET_HEADS, 8, RET_DIM)))
    return jnp.asarray(dmask), jnp.asarray(bc(qdec)), jnp.asarray(bc(kdec)), jnp.asarray(cd)


def _ret_fwd(pm, cos2, sin2, consts):
    t = pm.shape[0]
    n_chunks = t // CHUNK
    dmask, qdec, kdec, cd = consts
    scale = RET_DIM ** -0.5

    def col(c0):
        return pl.BlockSpec((CHUNK, RET_DIM), lambda h, n: (n, c0 + h))

    tab = pl.BlockSpec((CHUNK, RET_DIM), lambda h, n: (n, 0))
    head3 = lambda r: pl.BlockSpec((None, r, RET_DIM), lambda h, n: (h, 0, 0))

    def body(q_ref, k_ref, v_ref, g_ref, c_ref, s_ref, dm_ref, qd_ref, kd_ref, cd_ref, y_ref, raw_ref, st_ref, s_acc):
        @pl.when(pl.program_id(1) == 0)
        def _():
            s_acc[...] = jnp.zeros_like(s_acc)

        c, s = c_ref[...], s_ref[...]
        q = _rot(q_ref[...], c, s)
        k = _rot(k_ref[...], c, s) * scale
        vb = v_ref[...].astype(BF16)
        g = g_ref[...]
        s_in = s_acc[...]
        st_ref[...] = s_in
        a = _dot(q.astype(BF16), k.astype(BF16), 1, 1) * dm_ref[...]
        y = _dot(a.astype(BF16), vb, 1, 0) + _dot((q * qd_ref[...]).astype(BF16), s_in.astype(BF16), 1, 0)
        s_acc[...] = cd_ref[0:1, :] * s_in + _dot((k * kd_ref[...]).astype(BF16), vb, 0, 0)
        raw_ref[...] = y
        mu = jnp.mean(y, axis=-1, keepdims=True)
        yc = y - mu
        rs = lax.rsqrt(jnp.mean(yc * yc, axis=-1, keepdims=True) + EPS)
        y_ref[...] = (yc * rs * (g * _sigmoid(g))).astype(BF16)

    out_blk = pl.BlockSpec((CHUNK, RET_DIM), lambda h, n: (n, h))
    return _pcall(
        body, name="retention_fwd",
        out_shape=(jax.ShapeDtypeStruct((t, RET_WIDTH), BF16), jax.ShapeDtypeStruct((t, RET_WIDTH), F32),
                   jax.ShapeDtypeStruct((RET_HEADS, n_chunks, RET_DIM, RET_DIM), F32)),
        grid=(RET_HEADS, n_chunks),
        in_specs=[col(TILE_RQ), col(TILE_RK), col(TILE_RV), col(TILE_RG), tab, tab,
                  pl.BlockSpec((None, CHUNK, CHUNK), lambda h, n: (h, 0, 0)), head3(CHUNK), head3(CHUNK), head3(8)],
        out_specs=(out_blk, out_blk, pl.BlockSpec((None, None, RET_DIM, RET_DIM), lambda h, n: (h, n, 0, 0))),
        scratch_shapes=[pltpu.VMEM((RET_DIM, RET_DIM), F32)],
        dims=("parallel", "arbitrary"),
    )(pm, pm, pm, pm, cos2, sin2, dmask, qdec, kdec, cd)


def _ret_bwd(dy, pm, cos2, sin2, raw, states, consts):
    t = pm.shape[0]
    n_chunks = t // CHUNK
    dmask, qdec, kdec, cd = consts
    scale = RET_DIM ** -0.5
    rev = lambda n: n_chunks - 1 - n

    def col(c0):
        return pl.BlockSpec((CHUNK, RET_DIM), lambda h, n: (rev(n), c0 + h))

    tab = pl.BlockSpec((CHUNK, RET_DIM), lambda h, n: (rev(n), 0))
    blk = pl.BlockSpec((CHUNK, RET_DIM), lambda h, n: (rev(n), h))
    head3 = lambda r: pl.BlockSpec((None, r, RET_DIM), lambda h, n: (h, 0, 0))

    def body(dy_ref, q_ref, k_ref, v_ref, g_ref, c_ref, s_ref, raw_ref, st_ref, dm_ref, qd_ref, kd_ref, cd_ref,
             dq_ref, dk_ref, dv_ref, dg_ref, ds_acc):
        @pl.when(pl.program_id(1) == 0)
        def _():
            ds_acc[...] = jnp.zeros_like(ds_acc)

        c, s = c_ref[...], s_ref[...]
        q = _rot(q_ref[...], c, s)
        k = _rot(k_ref[...], c, s) * scale
        qb, kb, vb = q.astype(BF16), k.astype(BF16), v_ref[...].astype(BF16)
        g = g_ref[...]
        dm, qd, kd = dm_ref[...], qd_ref[...], kd_ref[...]
        y = raw_ref[...]
        mu = jnp.mean(y, axis=-1, keepdims=True)
        yc = y - mu
        rs = lax.rsqrt(jnp.mean(yc * yc, axis=-1, keepdims=True) + EPS)
        yn = yc * rs
        sg = _sigmoid(g)
        dyo = dy_ref[...]
        dg_ref[...] = (dyo * yn * (sg * (1.0 + g * (1.0 - sg)))).astype(BF16)
        dyn = dyo * (g * sg)
        dyr = rs * (dyn - jnp.mean(dyn, axis=-1, keepdims=True) - yn * jnp.mean(dyn * yn, axis=-1, keepdims=True))
        dyb = dyr.astype(BF16)
        s_in = st_ref[...].astype(BF16)
        ds_out = ds_acc[...]
        dsb = ds_out.astype(BF16)
        a = _dot(qb, kb, 1, 1) * dm
        da = (_dot(dyb, vb, 1, 1) * dm).astype(BF16)
        kdb = (k * kd).astype(BF16)
        qdb = (q * qd).astype(BF16)
        dv_ref[...] = (_dot(a.astype(BF16), dyb, 0, 0) + _dot(kdb, dsb, 1, 0)).astype(BF16)
        dqh = _dot(da, kb, 1, 0) + _dot(dyb, s_in, 1, 1) * qd
        dkh = _dot(da, qb, 0, 0) + _dot(vb, dsb, 1, 1) * kd
        ds_acc[...] = cd_ref[0:1, :] * ds_out + _dot(qdb, dyb, 0, 0)
        dq_ref[...] = _rot_t(dqh, c, s).astype(BF16)
        dk_ref[...] = (_rot_t(dkh, c, s) * scale).astype(BF16)

    return _pcall(
        body, name="retention_bwd",
        out_shape=(jax.ShapeDtypeStruct((t, RET_WIDTH), BF16),) * 4,
        grid=(RET_HEADS, n_chunks),
        in_specs=[blk, col(TILE_RQ), col(TILE_RK), col(TILE_RV), col(TILE_RG), tab, tab, blk,
                  pl.BlockSpec((None, None, RET_DIM, RET_DIM), lambda h, n: (h, rev(n), 0, 0)),
                  pl.BlockSpec((None, CHUNK, CHUNK), lambda h, n: (h, 0, 0)), head3(CHUNK), head3(CHUNK), head3(8)],
        out_specs=(blk,) * 4,
        scratch_shapes=[pltpu.VMEM((RET_DIM, RET_DIM), F32)],
        dims=("parallel", "arbitrary"),
    )(dy, pm, pm, pm, pm, cos2, sin2, raw, states, dmask, qdec, kdec, cd)


FOX_C_LANE = FOX_DIM
FOX_NEGC_LANE = FOX_DIM + 3
FOX_LSE_LANE = FOX_DIM + 6
FOX_L_LANE = FOX_C_LANE
FOX_ROWSUM_LANE = FOX_C_LANE
FOX_COLSUM_LANE = FOX_NEGC_LANE


def _split3(x):
    hi = x.astype(BF16)
    r1 = x - hi.astype(F32)
    mid = r1.astype(BF16)
    lo = (r1 - mid.astype(F32)).astype(BF16)
    return hi, mid, lo


def _tri_dot(tri, x):
    hi, mid, lo = _split3(x)
    return _dot(tri, lo, 1, 0) + _dot(tri, mid, 1, 0) + _dot(tri, hi, 1, 0)


def _log_sigmoid(z):
    return jnp.minimum(z, 0.0) - jnp.log1p(jnp.exp(-jnp.abs(z)))


def _fox_consts():
    place = np.zeros((2, 3, LANES, FOX_TILES), np.float32)
    ones = np.zeros((3, 1, FOX_TILES), np.float32)
    for h in range(FOX_HEADS):
        for part in range(3):
            place[0, part, h, LANES * h + FOX_C_LANE + part] = 1.0
            place[1, part, h, LANES * h + FOX_NEGC_LANE + part] = -1.0
            ones[0, 0, LANES * h + FOX_NEGC_LANE + part] = 1.0
            ones[1, 0, LANES * h + FOX_C_LANE + part] = 1.0
            ones[1, 0, LANES * h + FOX_LSE_LANE + part] = 1.0
            ones[2, 0, LANES * h + FOX_C_LANE + part] = 1.0
    return jnp.asarray(place, BF16), jnp.asarray(ones, F32)


def _fox_prep(pm, bpad):
    t = pm.shape[0]
    tt = _tile(t, 512, LANES)
    place, ones = _fox_consts()
    wide = lambda c0: pl.BlockSpec((tt, FOX_TILES), lambda i: (i, c0 // FOX_HEADS))
    const = lambda a: pl.BlockSpec(a.shape, lambda i: (0,) * a.ndim)

    def body(q_ref, k_ref, v_ref, ff_ref, b_ref, pl_ref, on_ref, qa_ref, ka_ref, va_ref, carry_s):
        @pl.when(pl.program_id(0) == 0)
        def _():
            carry_s[...] = jnp.zeros_like(carry_s)

        r = lax.broadcasted_iota(jnp.int32, (LANES, LANES), 0)
        cc = lax.broadcasted_iota(jnp.int32, (LANES, LANES), 1)
        tri = jnp.where(cc <= r, 1.0, 0.0).astype(BF16)
        bias = b_ref[...]
        for sub in range(tt // LANES):
            rows = pl.ds(sub * LANES, LANES)
            cs = _tri_dot(tri, _log_sigmoid(ff_ref[rows, :] + bias)) + carry_s[...]
            carry_s[...] = cs[LANES - 1:LANES, :]
            parts = _split3(cs)
            eq = sum(_dot(part, pl_ref[0, i], 1, 0) for i, part in enumerate(parts))
            ek = sum(_dot(part, pl_ref[1, i], 1, 0) for i, part in enumerate(parts))
            qa_ref[rows, :] = (q_ref[rows, :] * FOX_DIM ** -0.5 + eq + on_ref[0]).astype(BF16)
            ka_ref[rows, :] = (k_ref[rows, :] + ek + on_ref[1]).astype(BF16)
            va_ref[rows, :] = (v_ref[rows, :] + on_ref[2]).astype(BF16)

    out = pl.BlockSpec((tt, FOX_TILES), lambda i: (i, 0))
    return _pcall(body, name="fox_prep", out_shape=(jax.ShapeDtypeStruct((t, FOX_TILES), BF16),) * 3, grid=(t // tt,),
                  in_specs=[wide(TILE_FQ), wide(TILE_FK), wide(TILE_FV), pl.BlockSpec((tt, LANES), lambda i: (i, TILE_FF)),
                            pl.BlockSpec((1, LANES), lambda i: (0, 0)), const(place), const(ones)],
                  out_specs=(out,) * 3, scratch_shapes=[pltpu.VMEM((1, LANES), F32)],
                  dims=("arbitrary",))(pm, pm, pm, pm, bpad, place, ones)


def _fox_post(dc, pm, bpad):
    t = pm.shape[0]
    nb = t // LANES

    def body(dc_ref, ff_ref, b_ref, d_ref, db_ref):
        r = lax.broadcasted_iota(jnp.int32, (LANES, LANES), 0)
        cc = lax.broadcasted_iota(jnp.int32, (LANES, LANES), 1)
        tri = jnp.where(cc >= r, 1.0, 0.0).astype(BF16)
        bias = b_ref[...]

        def step(i, carry):
            tail, acc = carry
            rows = pl.ds(pl.multiple_of((nb - 1 - i) * LANES, LANES), LANES)
            cs = _tri_dot(tri, dc_ref[rows, :]) + tail
            dff = cs * _sigmoid(-(ff_ref[rows, :] + bias))
            d_ref[rows, :] = dff.astype(BF16)
            return cs[0:1, :], acc + jnp.sum(dff, axis=0, keepdims=True)

        zero = jnp.zeros((1, LANES), F32)
        _, acc = lax.fori_loop(0, nb, step, (zero, zero))
        db_ref[...] = acc

    return _pcall(body, name="fox_forget_bwd",
                  out_shape=(jax.ShapeDtypeStruct((t, LANES), BF16), jax.ShapeDtypeStruct((1, LANES), F32)), grid=(1,),
                  in_specs=[pl.BlockSpec((t, LANES), lambda i: (0, 0)), pl.BlockSpec((t, LANES), lambda i: (0, TILE_FF)),
                            pl.BlockSpec((1, LANES), lambda i: (0, 0))],
                  out_specs=(pl.BlockSpec((t, LANES), lambda i: (0, 0)), pl.BlockSpec((1, LANES), lambda i: (0, 0))),
                  dims=("arbitrary",))(dc, pm, bpad)


def _tri_tables(nb, q_major):
    pairs = [(i, j) for i in range(nb) for j in range(i + 1)] if q_major else \
            [(i, j) for j in range(nb) for i in range(j, nb)]
    return jnp.asarray([a for a, _ in pairs], jnp.int32), jnp.asarray([b for _, b in pairs], jnp.int32)


def _causal(s):
    n = s.shape[0]
    row = lax.broadcasted_iota(jnp.int32, (n, n), 0)
    col = lax.broadcasted_iota(jnp.int32, (n, n), 1)
    return jnp.where(col <= row, s, NEG)


def _lane_col(x, lane):
    sel = lax.broadcasted_iota(jnp.int32, x.shape, 1) == lane
    return jnp.sum(jnp.where(sel, x, 0.0), axis=1, keepdims=True)


def _fox_fwd(qa, ka, va, blk):
    t = qa.shape[0]
    nb = t // blk
    qi, kj = _tri_tables(nb, True)
    q_spec = pl.BlockSpec((blk, LANES), lambda h, s, qi_r, kj_r: (qi_r[s], h))
    k_spec = pl.BlockSpec((blk, LANES), lambda h, s, qi_r, kj_r: (kj_r[s], h))

    def body(qi_r, kj_r, q_ref, k_ref, v_ref, o_ref, ob_ref, qb_ref, m_s, acc_s):
        s_id = pl.program_id(1)
        i, j = qi_r[s_id], kj_r[s_id]

        @pl.when(j == 0)
        def _():
            m_s[...] = jnp.full_like(m_s, NEG)
            acc_s[...] = jnp.zeros_like(acc_s)

        def tile(diagonal):
            s = _dot(q_ref[...], k_ref[...], 1, 1)
            if diagonal:
                s = _causal(s)
            m_old = m_s[...]
            m_new = jnp.maximum(m_old, jnp.max(s, axis=1, keepdims=True))
            p = jnp.exp(s - jnp.tile(m_new, (1, blk // LANES)))
            acc_s[...] = jnp.exp(m_old - m_new) * acc_s[...] + _dot(p.astype(BF16), v_ref[...], 1, 0)
            m_s[...] = m_new

        @pl.when(j < i)
        def _():
            tile(False)

        @pl.when(j == i)
        def _():
            tile(True)
            acc = acc_s[...]
            l = _lane_col(acc, FOX_L_LANE)
            o = acc / l
            o_ref[...] = o
            ob_ref[...] = o.astype(BF16)
            hi, mid, lo = _split3(-(m_s[:, 0:1] + jnp.log(l)))
            lane = lax.broadcasted_iota(jnp.int32, acc.shape, 1)
            qb_ref[...] = jnp.where(lane == FOX_LSE_LANE, hi,
                                    jnp.where(lane == FOX_LSE_LANE + 1, mid,
                                              jnp.where(lane == FOX_LSE_LANE + 2, lo, q_ref[...])))

    wide = (t, FOX_TILES)
    return pl.pallas_call(
        body, name="fox_fwd",
        out_shape=(jax.ShapeDtypeStruct(wide, F32), jax.ShapeDtypeStruct(wide, BF16), jax.ShapeDtypeStruct(wide, BF16)),
        grid_spec=pltpu.PrefetchScalarGridSpec(
            num_scalar_prefetch=2, grid=(FOX_HEADS, qi.shape[0]), in_specs=[q_spec, k_spec, k_spec],
            out_specs=(q_spec,) * 3,
            scratch_shapes=[pltpu.VMEM((blk, LANES), F32), pltpu.VMEM((blk, LANES), F32)]),
        compiler_params=_params(("parallel", "arbitrary")),
    )(qi, kj, qa, ka, va)


def _fox_bwd(qa, ka, va, do, o, blk):
    t = qa.shape[0]
    nb = t // blk
    qi, kj = _tri_tables(nb, False)
    q_spec = pl.BlockSpec((blk, LANES), lambda h, s, qi_r, kj_r: (qi_r[s], h))
    k_spec = pl.BlockSpec((blk, LANES), lambda h, s, qi_r, kj_r: (kj_r[s], h))
    head_spec = pl.BlockSpec((t, LANES), lambda h, s, qi_r, kj_r: (0, h))
    head_col = pl.BlockSpec((None, t, 1), lambda h, s, qi_r, kj_r: (h, 0, 0))
    k_col = pl.BlockSpec((None, blk, 1), lambda h, s, qi_r, kj_r: (h, kj_r[s], 0))
    first_spec = pl.BlockSpec((blk, LANES), lambda h, s, qi_r, kj_r: (jnp.where(kj_r[s] == 0, qi_r[s], nb - 1), h))
    n_steps = int(qi.shape[0])

    def body(qi_r, kj_r, q_ref, k_ref, v_ref, do_ref, o_ref, dq_ref, dk_ref, dv_ref, rs_ref, cs_ref,
             doa_s, dq_s, dk_s, dv_s):
        s_id = pl.program_id(1)
        i, j = qi_r[s_id], kj_r[s_id]
        rows = pl.ds(pl.multiple_of(i * blk, blk), blk)

        @pl.when(j == 0)
        def _():
            dof = do_ref[...]
            hi, mid, lo = _split3(-jnp.sum(dof * o_ref[...], axis=1, keepdims=True))
            lane = lax.broadcasted_iota(jnp.int32, dof.shape, 1)
            doa = jnp.where(lane == FOX_C_LANE, hi.astype(F32),
                            jnp.where(lane == FOX_C_LANE + 1, mid.astype(F32),
                                      jnp.where(lane == FOX_C_LANE + 2, lo.astype(F32), dof)))
            doa_s[rows, :] = doa.astype(BF16)
            dq_s[rows, :] = jnp.zeros((blk, LANES), F32)

        @pl.when(i == j)
        def _():
            dk_s[...] = jnp.zeros_like(dk_s)
            dv_s[...] = jnp.zeros_like(dv_s)

        def tile(diagonal):
            q, k = q_ref[...], k_ref[...]
            s = _dot(q, k, 1, 1)
            if diagonal:
                s = _causal(s)
            p = jnp.exp(s)
            doa = doa_s[rows, :]
            ds = (p * _dot(doa, v_ref[...], 1, 1)).astype(BF16)
            dv_s[...] += _dot(p.astype(BF16), doa, 0, 0)
            dk_s[...] += _dot(ds, q, 0, 0)
            dq_s[rows, :] += _dot(ds, k, 1, 0)

        @pl.when(i > j)
        def _():
            tile(False)

        @pl.when(i == j)
        def _():
            tile(True)

        @pl.when(i == nb - 1)
        def _():
            dk = dk_s[...]
            dk_ref[...] = dk.astype(BF16)
            dv_ref[...] = dv_s[...].astype(BF16)
            cs_ref[...] = -_lane_col(dk, FOX_COLSUM_LANE)

        @pl.when(s_id == n_steps - 1)
        def _():
            dq = dq_s[...]
            dq_ref[...] = (dq * FOX_DIM ** -0.5).astype(BF16)
            rs_ref[...] = _lane_col(dq, FOX_ROWSUM_LANE)

    wide = jax.ShapeDtypeStruct((t, FOX_TILES), BF16)
    cols = jax.ShapeDtypeStruct((FOX_HEADS, t, 1), F32)
    return pl.pallas_call(
        body, name="fox_bwd", out_shape=(wide, wide, wide, cols, cols),
        grid_spec=pltpu.PrefetchScalarGridSpec(
            num_scalar_prefetch=2, grid=(FOX_HEADS, n_steps),
            in_specs=[q_spec, k_spec, k_spec, first_spec, first_spec],
            out_specs=(head_spec, k_spec, k_spec, head_col, k_col),
            scratch_shapes=[pltpu.VMEM((t, LANES), BF16), pltpu.VMEM((t, LANES), F32), pltpu.VMEM((blk, LANES), F32),
                            pltpu.VMEM((blk, LANES), F32)]),
        compiler_params=_params(("parallel", "arbitrary")),
    )(qi, kj, qa, ka, va, do, o)


def _merge_fwd(gm, bm, za, zb):
    t, d = za.shape
    tt = _tile(t, 256, 16)
    row = pl.BlockSpec((tt, d), lambda i: (i, 0))

    def body(gm_ref, b_ref, za_ref, zb_ref, o_ref):
        ga = _sigmoid(gm_ref[:, :d] + b_ref[:, :d])
        gb = _sigmoid(gm_ref[:, d:] + b_ref[:, d:])
        o_ref[...] = (ga * za_ref[...] + gb * zb_ref[...]).astype(BF16)

    return _pcall(body, name="merge_fwd", out_shape=jax.ShapeDtypeStruct((t, d), BF16), grid=(t // tt,),
                  in_specs=[pl.BlockSpec((tt, 2 * d), lambda i: (i, 0)), pl.BlockSpec((1, 2 * d), lambda i: (0, 0)), row, row],
                  out_specs=row, dims=("parallel",))(gm, bm, za, zb)


def _merge_bwd(dmix, gm, bm, za, zb):
    t, d = za.shape
    tt = _tile(t, 256, 16)
    row = pl.BlockSpec((tt, d), lambda i: (i, 0))
    wide = pl.BlockSpec((tt, 2 * d), lambda i: (i, 0))
    vec = pl.BlockSpec((1, 2 * d), lambda i: (0, 0))

    def body(dm_ref, gm_ref, b_ref, za_ref, zb_ref, dza_ref, dzb_ref, dgm_ref, db_ref):
        dm = dm_ref[...]
        ga = _sigmoid(gm_ref[:, :d] + b_ref[:, :d])
        gb = _sigmoid(gm_ref[:, d:] + b_ref[:, d:])
        dza_ref[...] = (dm * ga).astype(BF16)
        dzb_ref[...] = (dm * gb).astype(BF16)
        dla = dm * za_ref[...] * ga * (1.0 - ga)
        dlb = dm * zb_ref[...] * gb * (1.0 - gb)
        dgm_ref[:, :d] = dla.astype(BF16)
        dgm_ref[:, d:] = dlb.astype(BF16)
        pa = jnp.sum(dla, axis=0, keepdims=True)
        pb = jnp.sum(dlb, axis=0, keepdims=True)

        @pl.when(pl.program_id(0) == 0)
        def _():
            db_ref[:, :d] = pa
            db_ref[:, d:] = pb

        @pl.when(pl.program_id(0) > 0)
        def _():
            db_ref[:, :d] += pa
            db_ref[:, d:] += pb

    return _pcall(body, name="merge_bwd",
                  out_shape=(jax.ShapeDtypeStruct((t, d), BF16), jax.ShapeDtypeStruct((t, d), BF16),
                             jax.ShapeDtypeStruct((t, 2 * d), BF16), jax.ShapeDtypeStruct((1, 2 * d), F32)),
                  grid=(t // tt,), in_specs=[row, wide, vec, row, row], out_specs=(row, row, wide, vec),
                  dims=("arbitrary",))(dmix, gm, bm, za, zb)


def _ple_final(h3, pgl, pe, gain, target):
    t, d = h3.shape
    tt = _tile(t, 256, 16)
    row = pl.BlockSpec((tt, d), lambda i: (i, 0))
    vec = pl.BlockSpec((1, d), lambda i: (0, 0))
    lvec = pl.BlockSpec((1, LANES), lambda i: (0, 0))

    def body(h_ref, pgl_ref, pe_ref, g_ref, t_ref, dh_ref, dsg_ref, dpe_ref, loss_ref, dg_ref):
        pg = _sigmoid(pgl_ref[...])
        pe_v = pe_ref[...]
        h4 = h_ref[...] + pg * pe_v
        r = lax.rsqrt(jnp.mean(h4 * h4, axis=-1, keepdims=True) + EPS)
        gv = g_ref[...]
        err = h4 * r * gv - t_ref[...]
        part_loss = 0.5 * jnp.sum(jnp.mean(err * err, axis=-1, keepdims=True), axis=0, keepdims=True)
        dy = err * (1.0 / d)
        part_g = jnp.sum(dy * h4 * r, axis=0, keepdims=True)
        dyg = dy * gv
        dh = r * dyg - h4 * (r * r * r) * jnp.mean(dyg * h4, axis=-1, keepdims=True)
        dh_ref[...] = dh
        dsg_ref[...] = (dh * pe_v * pg * (1.0 - pg)).astype(BF16)
        dpe_ref[...] = (dh * pg).astype(BF16)

        @pl.when(pl.program_id(0) == 0)
        def _():
            loss_ref[...] = jnp.broadcast_to(part_loss, (1, LANES))
            dg_ref[...] = part_g

        @pl.when(pl.program_id(0) > 0)
        def _():
            loss_ref[...] += jnp.broadcast_to(part_loss, (1, LANES))
            dg_ref[...] += part_g

    return _pcall(body, name="ple_final",
                  out_shape=(jax.ShapeDtypeStruct((t, d), F32), jax.ShapeDtypeStruct((t, d), BF16),
                             jax.ShapeDtypeStruct((t, d), BF16), jax.ShapeDtypeStruct((1, LANES), F32),
                             jax.ShapeDtypeStruct((1, d), F32)),
                  grid=(t // tt,), in_specs=[row, row, row, vec, row], out_specs=(row, row, row, lvec, vec),
                  dims=("arbitrary",))(h3, pgl, pe, gain, target)


def _adamw_math(w, g, m, v):
    m = ADAM_B1 * m + (1.0 - ADAM_B1) * g
    v = ADAM_B2 * v + (1.0 - ADAM_B2) * (g * g)
    m_hat = m / (1.0 - ADAM_B1 ** ADAM_STEP)
    v_hat = v / (1.0 - ADAM_B2 ** ADAM_STEP)
    delta = -ADAM_LR * (m_hat / (jnp.sqrt(v_hat) + ADAM_EPS) + ADAM_WD * w)
    return delta, m, v


def _adamw(parts, w, m, v, name):
    n, r, c = parts.shape
    tr = _tile(r, 256, 16)
    row = pl.BlockSpec((tr, c), lambda i: (i, 0))

    def body(p_ref, w_ref, m_ref, v_ref, g_ref, d_ref, mo_ref, vo_ref):
        g = p_ref[0].astype(F32)
        for s in range(1, n):
            g = g + p_ref[s].astype(F32)
        g_ref[...] = g
        d_ref[...], mo_ref[...], vo_ref[...] = _adamw_math(w_ref[...], g, m_ref[...], v_ref[...])

    return _pcall(body, name=name, out_shape=(jax.ShapeDtypeStruct((r, c), F32),) * 4, grid=(r // tr,),
                  in_specs=[pl.BlockSpec((n, tr, c), lambda i: (0, i, 0)), row, row, row], out_specs=(row,) * 4,
                  dims=("parallel",))(parts, w, m, v)


ANY = pl.BlockSpec(memory_space=pl.ANY)


def _all_gather(shards):
    n = len(shards)

    def body(*refs):
        x_refs, out_refs = refs[:n], refs[n:2 * n]
        send_sems, recv_sems, local_sems = refs[2 * n:]
        x, y, cc = lax.axis_index("x"), lax.axis_index("y"), lax.axis_index("c")
        me, sibling = (x, y, cc), (x, y, 1 - cc)
        chips = [(1 - x, y), (x, 1 - y), (1 - x, 1 - y)]

        def slot(a, px, py, pc):
            return out_refs[a].at[4 * px + 2 * py + pc]

        def copy(a, k, block, to, src=None):
            return pltpu.make_async_remote_copy(
                src_ref=slot(a, *block) if src is None else src, dst_ref=slot(a, *block),
                send_sem=send_sems.at[7 * a + k], recv_sem=recv_sems.at[7 * a + k], device_id=to, device_id_type=MESH)

        local, sent = [], []
        for a in range(n):
            local.append(pltpu.make_async_copy(x_refs[a], slot(a, *me), local_sems.at[a]))
            sent.append(copy(a, 0, me, sibling, src=x_refs[a]))
            sent += [copy(a, 1 + j, me, (*chip, cc), src=x_refs[a]) for j, chip in enumerate(chips)]
        for cp in local + sent:
            cp.start()
        for j, chip in enumerate(chips):
            for a in range(n):
                copy(a, 1 + j, (*chip, cc), me).wait_recv()
                sent.append(copy(a, 4 + j, (*chip, cc), sibling))
                sent[-1].start()
        for a in range(n):
            copy(a, 0, sibling, me).wait_recv()
            for j, chip in enumerate(chips):
                copy(a, 4 + j, (*chip, 1 - cc), me).wait_recv()
        for cp in sent:
            cp.wait_send()
        for cp in local:
            cp.wait()

    return pl.pallas_call(
        body, name="weights_all_gather",
        out_shape=tuple(jax.ShapeDtypeStruct((N_DEV,) + s.shape, s.dtype) for s in shards),
        in_specs=[ANY] * n, out_specs=(ANY,) * n,
        scratch_shapes=[pltpu.SemaphoreType.DMA((7 * n,)), pltpu.SemaphoreType.DMA((7 * n,)),
                        pltpu.SemaphoreType.DMA((n,))],
    )(*shards)


def _reduce_scatter_exchange(blocks):
    n = len(blocks)

    def body(*refs):
        g_refs, recv_refs = refs[:n], refs[n:2 * n]
        send_sems, recv_sems, local_sems = refs[2 * n:]
        x, y, cc = lax.axis_index("x"), lax.axis_index("y"), lax.axis_index("c")
        me = 4 * x + 2 * y + cc
        local, sent, landing = [], [], []
        for a in range(n):
            local.append(pltpu.make_async_copy(g_refs[a].at[me], recv_refs[a].at[me], local_sems.at[a]))
        for k in range(1, N_DEV):
            px, py, pc = x ^ (k >> 2), y ^ ((k >> 1) & 1), cc ^ (k & 1)
            peer = 4 * px + 2 * py + pc
            for a in range(n):
                sems = dict(send_sem=send_sems.at[7 * a + k - 1], recv_sem=recv_sems.at[7 * a + k - 1],
                            device_id=(px, py, pc), device_id_type=MESH)
                sent.append(pltpu.make_async_remote_copy(src_ref=g_refs[a].at[peer], dst_ref=recv_refs[a].at[me], **sems))
                landing.append(pltpu.make_async_remote_copy(src_ref=g_refs[a].at[me], dst_ref=recv_refs[a].at[peer], **sems))
        for cp in local + sent:
            cp.start()
        for cp in landing:
            cp.wait_recv()
        for cp in sent:
            cp.wait_send()
        for cp in local:
            cp.wait()

    return pl.pallas_call(
        body, name="grads_reduce_scatter_exchange",
        out_shape=tuple(jax.ShapeDtypeStruct(b.shape, b.dtype) for b in blocks),
        in_specs=[ANY] * n, out_specs=(ANY,) * n,
        scratch_shapes=[pltpu.SemaphoreType.DMA((7 * n,)), pltpu.SemaphoreType.DMA((7 * n,)),
                        pltpu.SemaphoreType.DMA((n,))],
    )(*blocks)


HBM = pl.BlockSpec(memory_space=pltpu.HBM)
SEM = pl.BlockSpec(memory_space=pltpu.SEMAPHORE)
DATAFLOW = pltpu.SideEffectType.DATAFLOW_SIDE_EFFECTING


def _peers():
    x, y, cc = lax.axis_index("x"), lax.axis_index("y"), lax.axis_index("c")
    out = []
    for k in range(1, N_DEV):
        px, py, pc = x ^ (k >> 2), y ^ ((k >> 1) & 1), cc ^ (k & 1)
        out.append((k, (px, py, pc), 4 * px + 2 * py + pc))
    return 4 * x + 2 * y + cc, out


def _scatter_start(blocks, name):
    n = len(blocks)
    lands = [lax.empty(b.shape, b.dtype) for b in blocks]

    def body(*refs):
        g_refs, land_refs = refs[:n], refs[n:2 * n]
        send_sems, recv_sems, token = refs[2 * n], refs[2 * n + 1], refs[-1]
        me, peers = _peers()
        for k, peer, slot in peers:
            for a in range(n):
                pltpu.make_async_remote_copy(
                    src_ref=g_refs[a].at[slot], dst_ref=land_refs[a].at[me], send_sem=send_sems.at[7 * a + k - 1],
                    recv_sem=recv_sems.at[7 * a + k - 1], device_id=peer, device_id_type=MESH).start()
        token[...] = jnp.zeros_like(token)

    thru = [pltpu.HBM(b.shape, b.dtype) for b in blocks]
    return pl.pallas_call(
        body, name=name,
        out_shape=(pltpu.SemaphoreType.DMA((7 * n,)), pltpu.SemaphoreType.DMA((7 * n,)), *thru, *thru,
                   jax.ShapeDtypeStruct((8, LANES), F32)),
        in_specs=[HBM] * (2 * n), out_specs=(SEM, SEM, *[HBM] * (2 * n), pl.BlockSpec(memory_space=pltpu.VMEM)),
        input_output_aliases={i: 2 + i for i in range(2 * n)},
        compiler_params=pltpu.CompilerParams(has_side_effects=DATAFLOW),
    )(*[pltpu.with_memory_space_constraint(a, pltpu.HBM) for a in list(blocks) + lands])


def _scatter_wait(started, after, name):
    send_sems, recv_sems, *rest = started
    n = (len(rest) - 1) // 2
    thru = rest[:2 * n]

    def body(*refs):
        g_refs, land_refs = refs[:n], refs[n:2 * n]
        send_sems, recv_sems = refs[2 * n], refs[2 * n + 1]
        me, peers = _peers()
        for k, peer, slot in peers:
            for a in range(n):
                copy = pltpu.make_async_remote_copy(
                    src_ref=g_refs[a].at[slot], dst_ref=land_refs[a].at[slot], send_sem=send_sems.at[7 * a + k - 1],
                    recv_sem=recv_sems.at[7 * a + k - 1], device_id=peer, device_id_type=MESH)
                copy.wait_send()
                copy.wait_recv()

    out = pl.pallas_call(
        body, name=name, out_shape=tuple(pltpu.HBM(a.shape, a.dtype) for a in thru),
        in_specs=[HBM] * (2 * n) + [SEM, SEM, pl.BlockSpec(memory_space=pl.ANY)], out_specs=tuple([HBM] * (2 * n)),
        input_output_aliases={i: i for i in range(2 * n)},
        compiler_params=pltpu.CompilerParams(has_side_effects=DATAFLOW),
    )(*thru, send_sems, recv_sems, after)
    return out[:n], out[n:]


BIG = (("w_ffn1_gate", "colT"), ("w_ffn1_up", "colT"), ("w_ffn1_down", "row"), ("w_ffn2_gate", "colT"),
       ("w_ffn2_up", "colT"), ("w_ffn2_down", "row"), ("w_in", "colT"), ("w_merge", "col"), ("w_ret_out", "col"),
       ("w_fox_out", "col"), ("w_out", "row"), ("w_ple", "col"), ("w_ple_gate", "row"))


def _shard_view(a, kind):
    a = a.reshape(a.shape[-2:])
    return a.T if kind == "colT" else a


def _unview(a, kind, shape):
    return (a.T if kind == "colT" else a).reshape(shape)


def _full_from_slots(g, kind):
    n, r, c = g.shape
    return g.transpose(1, 0, 2).reshape(r, n * c) if kind == "col" else g.reshape(n * r, c)


def _slots_from_full(f, kind):
    r, c = f.shape
    return f.reshape(r, N_DEV, c // N_DEV).transpose(1, 0, 2) if kind == "col" else f.reshape(N_DEV, r // N_DEV, c)


EARLY_GROUPS = (("w_ple_gate", "w_ple", "w_ffn2_down", "w_ffn2_gate", "w_ffn2_up", "w_out", "w_ret_out", "w_fox_out"),
                ("w_in", "w_merge"))


def _scatter_group(gw, names, name):
    kind = dict(BIG)
    return names, _scatter_start([_slots_from_full(gw[n], kind[n]) for n in names], name)


def _pad_heads(w):
    d = w.shape[1]
    return jnp.pad(w.reshape(FOX_HEADS, FOX_DIM, d), ((0, 0), (0, LANES - FOX_DIM), (0, 0))).reshape(FOX_TILES, d)


def _unpad_heads(w):
    d = w.shape[1]
    return w.reshape(FOX_HEADS, LANES, d)[:, :FOX_DIM].reshape(FOX_WIDTH, d)


def _deinterleave_rows(w):
    d = w.shape[1]
    return w.reshape(RET_HEADS, RET_DIM // 2, 2, d).transpose(0, 2, 1, 3).reshape(RET_WIDTH, d)


def _interleave_rows(w):
    d = w.shape[1]
    return w.reshape(RET_HEADS, 2, RET_DIM // 2, d).transpose(0, 2, 1, 3).reshape(RET_WIDTH, d)


def _pad_w_in(wt):
    d = wt.shape[1]
    rw, fw = RET_WIDTH, FOX_WIDTH
    fo = 4 * rw
    return jnp.concatenate([
        _deinterleave_rows(wt[:rw]), _deinterleave_rows(wt[rw:2 * rw]), wt[2 * rw:4 * rw],
        _pad_heads(wt[fo:fo + fw]), _pad_heads(wt[fo + fw:fo + 2 * fw]), _pad_heads(wt[fo + 2 * fw:fo + 3 * fw]),
        wt[fo + 3 * fw:], jnp.zeros((2 * LANES - FOX_HEADS, d), wt.dtype)], axis=0)


def _unpad_w_in(g):
    rw = RET_WIDTH
    f0 = 4 * rw
    return jnp.concatenate([
        _interleave_rows(g[:rw]), _interleave_rows(g[rw:2 * rw]), g[2 * rw:4 * rw],
        _unpad_heads(g[f0:f0 + FOX_TILES]), _unpad_heads(g[f0 + FOX_TILES:f0 + 2 * FOX_TILES]),
        _unpad_heads(g[f0 + 2 * FOX_TILES:f0 + 3 * FOX_TILES]),
        g[f0 + 3 * FOX_TILES:f0 + 3 * FOX_TILES + FOX_HEADS]], axis=0)


SMALL = ("ln_ffn1", "ln_mix", "b_forget", "b_merge", "ln_ffn2", "ln_ple", "ln_final")


def _small_rows(n):
    rows = -(-n // LANES)
    return -(-rows // 8) * 8


def _pack_small(vals, with_loss=None):
    parts = []
    for name in SMALL:
        v = vals[name].reshape(-1).astype(F32)
        rows = _small_rows(v.shape[0])
        parts.append(jnp.pad(v, (0, rows * LANES - v.shape[0])).reshape(rows, LANES))
    if with_loss is not None:
        parts.append(jnp.pad(with_loss.reshape(1, LANES), ((0, 7), (0, 0))))
    else:
        parts.append(jnp.zeros((8, LANES), F32))
    return jnp.concatenate(parts, axis=0)


def _unpack_small(packed, shapes):
    out, at = {}, 0
    for name in SMALL:
        n = int(np.prod(shapes[name]))
        rows = _small_rows(n)
        out[name] = packed[at:at + rows].reshape(-1)[:n].reshape(shapes[name])
        at += rows
    return out, packed[at, 0]


def _local_step(x, p, positions, target, w, small):
    t, d = x.shape
    gain = lambda n: small[n].reshape(1, d)
    w_in_t = _pad_w_in(w["w_in"])
    w_fox_pad = _pad_heads(w["w_fox_out"])
    bpad = jnp.pad(small["b_forget"].reshape(1, FOX_HEADS), ((0, 0), (0, LANES - FOX_HEADS)))
    bm = small["b_merge"].reshape(1, 2 * d)
    fox_blk = _tile(t, 512, 128)

    def ffn_fwd(n, tag):
        g = _mm([(n, w[f"w_{tag}_gate"], "nt")], F32, f"{tag}_gate", tn=1408)
        u = _mm([(n, w[f"w_{tag}_up"], "nt")], F32, f"{tag}_up", tn=1408)
        a = _swiglu_fwd(g, u, f"{tag}_swiglu")
        return g, u, a, _mm([(a, w[f"w_{tag}_down"], "nn")], F32, f"{tag}_down")

    n1 = _rms_fwd(x, gain("ln_ffn1"), "rms_ffn1")
    g1, u1, a1, f1 = ffn_fwd(n1, "ffn1")
    h1, u = _rms_fwd(x, gain("ln_mix"), "rms_mix", f=f1)
    gm = _mm([(u, w["w_merge"], "nn")], F32, "mixer_gates")
    pm = _mm([(u, w_in_t, "nt")], F32, "mixer_in", tn=1792)

    half = jnp.arange(RET_DIM // 2, dtype=F32) / (RET_DIM // 2)
    inv = 1.0 / (ROPE_BASE ** half)
    inv2 = jnp.concatenate([inv, inv]).reshape(1, RET_DIM)
    sign2 = jnp.concatenate([-jnp.ones((RET_DIM // 2,), F32), jnp.ones((RET_DIM // 2,), F32)]).reshape(1, RET_DIM)
    cos2, sin2 = _rope_tables(positions.reshape(t, 1), inv2, sign2)
    consts = _ret_consts()
    y_ret, y_raw, states = _ret_fwd(pm, cos2, sin2, consts)
    za = _mm([(y_ret, w["w_ret_out"], "nn")], F32, "ret_out")

    qa, ka, va = _fox_prep(pm, bpad)
    o_fox, y_fox, qa_b = _fox_fwd(qa, ka, va, fox_blk)
    zb = _mm([(y_fox, w_fox_pad, "nn")], F32, "fox_out")

    mix = _merge_fwd(gm, bm, za, zb)
    mo = _mm([(mix, w["w_out"], "nn")], F32, "mix_out")
    h2, n2 = _rms_fwd(h1, gain("ln_ffn2"), "rms_ffn2", f=mo, scale=1.0)
    g2, u2, a2, f2 = ffn_fwd(n2, "ffn2")
    h3, n3 = _rms_fwd(h2, gain("ln_ple"), "rms_ple", f=f2)
    pgl = _mm([(n3, w["w_ple_gate"], "nn")], F32, "ple_gate")
    pb = p.astype(BF16)
    pe = _mm([(pb, w["w_ple"], "nn")], F32, "ple_embed")

    gw, gs = {}, {}
    dh4, dsg, dpe, loss, gs["ln_final"] = _ple_final(h3, pgl, pe, gain("ln_final"), target)
    gw["w_ple_gate"] = _mm([(n3, dsg, "tn")], BF16, "d_w_ple_gate", tn=256)
    gw["w_ple"] = _mm([(pb, dpe, "tn")], BF16, "d_w_ple", tn=256)
    dn3 = _mm([(dsg, w["w_ple_gate"], "nt")], F32, "d_n3")
    dh3, dh3_half, gs["ln_ple"] = _rms_bwd(dn3, h3, gain("ln_ple"), dh4, "rms_ple_bwd", 0.5)

    def ffn_bwd(dh_half, g, u_, a, n, tag):
        gw[f"w_{tag}_down"] = _mm([(a, dh_half, "tn")], BF16, f"d_w_{tag}_down", tm=1408, tn=256)
        da = _mm([(dh_half, w[f"w_{tag}_down"], "nt")], F32, f"d_a_{tag}", tn=1408)
        dg, du_ = _swiglu_bwd(da, g, u_, f"{tag}_swiglu_bwd")
        gw[f"w_{tag}_gate"] = _mm([(dg, n, "tn")], BF16, f"d_w_{tag}_gate", tm=1408, tn=256)
        gw[f"w_{tag}_up"] = _mm([(du_, n, "tn")], BF16, f"d_w_{tag}_up", tm=1408, tn=256)
        return _mm([(dg, w[f"w_{tag}_gate"], "nn"), (du_, w[f"w_{tag}_up"], "nn")], F32, f"d_n_{tag}", tm=512)

    dn2 = ffn_bwd(dh3_half, g2, u2, a2, n2, "ffn2")
    dh2, dh2_b, gs["ln_ffn2"] = _rms_bwd(dn2, h2, gain("ln_ffn2"), dh3, "rms_ffn2_bwd", 1.0)

    gw["w_out"] = _mm([(mix, dh2_b, "tn")], BF16, "d_w_out", tn=256)
    dmix = _mm([(dh2_b, w["w_out"], "nt")], F32, "d_mix")
    dza, dzb, dgm, gs["b_merge"] = _merge_bwd(dmix, gm, bm, za, zb)
    gw["w_ret_out"] = _mm([(y_ret, dza, "tn")], BF16, "d_w_ret_out", tn=256)
    gw["w_fox_out"] = _unpad_heads(_mm([(y_fox, dzb, "tn")], BF16, "d_w_fox_out", tn=256))
    dy_ret = _mm([(dza, w["w_ret_out"], "nt")], F32, "d_y_ret")
    do_fox = _mm([(dzb, w_fox_pad, "nt")], F32, "d_y_fox")

    pending = [_scatter_group(gw, EARLY_GROUPS[0], "grads_scatter_a_start")]
    token = pending[0][1][-1][0, 0]
    drq, drk, drv, drg = _ret_bwd(dy_ret, pm, cos2, sin2, y_raw, states, consts[:3] + (consts[3] + token,))

    dqa, dka, dva, ds_rows, ds_cols = _fox_bwd(qa_b, ka, va, do_fox, o_fox, fox_blk)
    dc = jnp.pad((ds_rows + ds_cols).reshape(FOX_HEADS, t).T, ((0, 0), (0, LANES - FOX_HEADS)))
    dff, db_forget = _fox_post(dc, pm, bpad)
    gs["b_forget"] = db_forget[:, :FOX_HEADS]

    dpm = jnp.concatenate([drq, drk, drv, drg, dqa, dka, dva, dff, jnp.zeros((t, LANES), BF16)], axis=1)
    gw["w_merge"] = _mm([(u, dgm, "tn")], BF16, "d_w_merge", tn=512)
    gw["w_in"] = _unpad_w_in(_mm([(dpm, u, "tn")], BF16, "d_w_in", tm=1792, tn=256))
    du = _mm([(dpm, w_in_t, "nn"), (dgm, w["w_merge"], "nt")], F32, "d_u", tm=512, tn=512)
    pending.append(_scatter_group(gw, EARLY_GROUPS[1], "grads_scatter_b_start"))
    token = pending[1][1][-1][0:1, 0:1]
    dh1, dh1_half, gs["ln_mix"] = _rms_bwd(du, h1, gain("ln_mix") + token, dh2, "rms_mix_bwd", 0.5)

    dn1 = ffn_bwd(dh1_half, g1, u1, a1, n1, "ffn1")
    dx, _, gs["ln_ffn1"] = _rms_bwd(dn1, x, gain("ln_ffn1"), dh1, "rms_ffn1_bwd", 1.0)
    return loss, dx, gw, gs, pending


WEIGHTS = ("ln_ffn1", "w_ffn1_gate", "w_ffn1_up", "w_ffn1_down", "ln_mix", "w_in", "b_forget", "w_merge", "b_merge",
           "w_ret_out", "w_fox_out", "w_out", "ln_ffn2", "w_ffn2_gate", "w_ffn2_up", "w_ffn2_down", "ln_ple", "w_ple",
           "w_ple_gate", "ln_final")


def kernel(x, p, positions, ln_ffn1, w_ffn1_gate, w_ffn1_up, w_ffn1_down, ln_mix, w_in, b_forget, w_merge, b_merge, w_ret_out, w_fox_out, w_out, ln_ffn2, w_ffn2_gate, w_ffn2_up, w_ffn2_down, ln_ple, w_ple, w_ple_gate, ln_final, loss_target, m_ln_ffn1, m_w_ffn1_gate, m_w_ffn1_up, m_w_ffn1_down, m_ln_mix, m_w_in, m_b_forget, m_w_merge, m_b_merge, m_w_ret_out, m_w_fox_out, m_w_out, m_ln_ffn2, m_w_ffn2_gate, m_w_ffn2_up, m_w_ffn2_down, m_ln_ple, m_w_ple, m_w_ple_gate, m_ln_final, v_ln_ffn1, v_w_ffn1_gate, v_w_ffn1_up, v_w_ffn1_down, v_ln_mix, v_w_in, v_b_forget, v_w_merge, v_b_merge, v_w_ret_out, v_w_fox_out, v_w_out, v_ln_ffn2, v_w_ffn2_gate, v_w_ffn2_up, v_w_ffn2_down, v_ln_ple, v_w_ple, v_w_ple_gate, v_ln_final):
    args = dict(ln_ffn1=ln_ffn1, w_ffn1_gate=w_ffn1_gate, w_ffn1_up=w_ffn1_up, w_ffn1_down=w_ffn1_down, ln_mix=ln_mix, w_in=w_in, b_forget=b_forget, w_merge=w_merge, b_merge=b_merge, w_ret_out=w_ret_out, w_fox_out=w_fox_out, w_out=w_out, ln_ffn2=ln_ffn2, w_ffn2_gate=w_ffn2_gate, w_ffn2_up=w_ffn2_up, w_ffn2_down=w_ffn2_down, ln_ple=ln_ple, w_ple=w_ple, w_ple_gate=w_ple_gate, ln_final=ln_final)
    moms = dict(ln_ffn1=m_ln_ffn1, w_ffn1_gate=m_w_ffn1_gate, w_ffn1_up=m_w_ffn1_up, w_ffn1_down=m_w_ffn1_down, ln_mix=m_ln_mix, w_in=m_w_in, b_forget=m_b_forget, w_merge=m_w_merge, b_merge=m_b_merge, w_ret_out=m_w_ret_out, w_fox_out=m_w_fox_out, w_out=m_w_out, ln_ffn2=m_ln_ffn2, w_ffn2_gate=m_w_ffn2_gate, w_ffn2_up=m_w_ffn2_up, w_ffn2_down=m_w_ffn2_down, ln_ple=m_ln_ple, w_ple=m_w_ple, w_ple_gate=m_w_ple_gate, ln_final=m_ln_final)
    vars_ = dict(ln_ffn1=v_ln_ffn1, w_ffn1_gate=v_w_ffn1_gate, w_ffn1_up=v_w_ffn1_up, w_ffn1_down=v_w_ffn1_down, ln_mix=v_ln_mix, w_in=v_w_in, b_forget=v_b_forget, w_merge=v_w_merge, b_merge=v_b_merge, w_ret_out=v_w_ret_out, w_fox_out=v_w_fox_out, w_out=v_w_out, ln_ffn2=v_ln_ffn2, w_ffn2_gate=v_w_ffn2_gate, w_ffn2_up=v_w_ffn2_up, w_ffn2_down=v_w_ffn2_down, ln_ple=v_ln_ple, w_ple=v_w_ple, w_ple_gate=v_w_ple_gate, ln_final=v_ln_final)
    kinds = ("grad", "delta", "new_m", "new_v")

    gathered = _all_gather([_shard_view(args[n], kind).astype(BF16) for n, kind in BIG])
    w_full = {n: _full_from_slots(g, kind) for (n, kind), g in zip(BIG, gathered)}

    small = {n: args[n] for n in SMALL}
    loss_part, dx, gw, gs, pending = _local_step(x[0], p[0, 0], positions[0], loss_target[0], w_full, small)

    me = 4 * lax.axis_index("x") + 2 * lax.axis_index("y") + lax.axis_index("c")
    parts_of = {}
    for tag, (names, started) in zip("ab", pending):
        sent, landed = _scatter_wait(started, dx, f"grads_scatter_{tag}_wait")
        for n, blk, land in zip(names, sent, landed):
            own = lax.dynamic_index_in_dim(blk, me, 0, keepdims=True)
            parts_of[n] = lax.dynamic_update_slice_in_dim(land, own, me, 0)
    late = [(n, kind) for n, kind in BIG if n not in parts_of]
    small_part = _pack_small(gs, with_loss=loss_part)
    blocks = [_slots_from_full(gw[n], kind) for n, kind in late]
    blocks.append(jnp.broadcast_to(small_part, (N_DEV,) + small_part.shape))
    recv = _reduce_scatter_exchange(blocks)
    parts_of.update({n: r for (n, _), r in zip(late, recv)})

    res = {}
    for n, kind in BIG:
        parts = parts_of[n]
        outs = _adamw(parts, _shard_view(args[n], kind), _shard_view(moms[n], kind), _shard_view(vars_[n], kind),
                      f"adamw_{n}")
        for what, o in zip(kinds, outs):
            res[(what, n)] = _unview(o, kind, args[n].shape)
    s_outs = _adamw(recv[-1], _pack_small(small), _pack_small({n: moms[n] for n in SMALL}),
                    _pack_small({n: vars_[n] for n in SMALL}), "adamw_small")
    for what, sm in zip(kinds, s_outs):
        svals, extra = _unpack_small(sm, {n: args[n].shape for n in SMALL})
        if what == "grad":
            loss = extra
        for n in SMALL:
            res[(what, n)] = svals[n]
    return (loss, dx[None], *[res[(what, n)] for what in kinds for n in WEIGHTS])
```

```python
import numpy as np
import jax
import jax.numpy as jnp
from jax import lax
from jax.experimental import pallas as pl
from jax.experimental.pallas import tpu as pltpu

F32 = jnp.float32
BF16 = jnp.bfloat16

N_DEV = 8
EPS = 1e-6
RET_HEADS = 4
RET_DIM = 128
RET_WIDTH = RET_HEADS * RET_DIM
FOX_HEADS = 8
FOX_DIM = 64
FOX_WIDTH = FOX_HEADS * FOX_DIM
CHUNK = 128
ROPE_BASE = 10000.0
LANES = 128
FOX_TILES = FOX_HEADS * LANES
IN_COLS = 4 * RET_WIDTH + 3 * FOX_WIDTH + FOX_HEADS
IN_PAD = 4 * RET_WIDTH + 3 * FOX_TILES + 2 * LANES
TILE_RQ, TILE_RK, TILE_RV, TILE_RG = 0, 4, 8, 12
TILE_FQ, TILE_FK, TILE_FV, TILE_FF = 16, 24, 32, 40
NEG = -1e30

ADAM_LR = 0.001
ADAM_B1 = 0.9
ADAM_B2 = 0.999
ADAM_EPS = 1e-08
ADAM_WD = 0.01
ADAM_STEP = 10

VMEM_LIMIT_BYTES = 56 * 1024 * 1024

MESH = pl.DeviceIdType.MESH


def _tile(dim, pref, mult):
    if dim <= pref:
        return dim
    t = (pref // mult) * mult
    while t >= mult:
        if dim % t == 0:
            return t
        t -= mult
    return dim


def _params(dims):
    return pltpu.CompilerParams(dimension_semantics=dims, vmem_limit_bytes=VMEM_LIMIT_BYTES)


def _pcall(body, *, name, out_shape, grid, in_specs, out_specs, scratch_shapes=(), dims=None):
    return pl.pallas_call(body, name=name, out_shape=out_shape, grid=grid, in_specs=in_specs, out_specs=out_specs,
                          scratch_shapes=list(scratch_shapes), compiler_params=_params(dims))


def _dot(a, b, ca, cb):
    return lax.dot_general(a, b, (((ca,), (cb,)), ((), ())), preferred_element_type=F32)


def _sigmoid(x):
    return 1.0 / (1.0 + jnp.exp(-x))


def _mm(pairs, out_dtype, name, tm=1024, tn=1024):
    dims = []
    for a, b, mode in pairs:
        m, k = (a.shape[1], a.shape[0]) if mode == "tn" else a.shape
        n, k2 = b.shape if mode == "nt" else (b.shape[1], b.shape[0])
        assert k == k2, (name, a.shape, b.shape, mode)
        dims.append((m, n))
    assert all(d == dims[0] for d in dims), (name, dims)
    m, n = dims[0]
    tm = _tile(m, tm, 128 if any(mode == "tn" for _, _, mode in pairs) else 16)
    tn = _tile(n, tn, 128)
    in_specs, contract, operands = [], [], []
    for a, b, mode in pairs:
        k = a.shape[0] if mode == "tn" else a.shape[1]
        in_specs.append(pl.BlockSpec((k, tm), lambda i, j: (0, i)) if mode == "tn" else
                        pl.BlockSpec((tm, k), lambda i, j: (i, 0)))
        in_specs.append(pl.BlockSpec((tn, k), lambda i, j: (j, 0)) if mode == "nt" else
                        pl.BlockSpec((k, tn), lambda i, j: (0, j)))
        contract.append((0 if mode == "tn" else 1, 1 if mode == "nt" else 0))
        operands += [a, b]

    def body(*refs):
        o_ref = refs[-1]
        acc = None
        for p, (ca, cb) in enumerate(contract):
            part = _dot(refs[2 * p][...], refs[2 * p + 1][...], ca, cb)
            acc = part if acc is None else acc + part
        o_ref[...] = acc.astype(out_dtype)

    return _pcall(body, name=name, out_shape=jax.ShapeDtypeStruct((m, n), out_dtype), grid=(m // tm, n // tn),
                  in_specs=in_specs, out_specs=pl.BlockSpec((tm, tn), lambda i, j: (i, j)),
                  dims=("parallel", "parallel"))(*operands)


def _rms_fwd(h, gain, name, f=None, scale=0.5):
    t, d = h.shape
    tt = _tile(t, 512, 16)
    row = pl.BlockSpec((tt, d), lambda i: (i, 0))
    vec = pl.BlockSpec((1, d), lambda i: (0, 0))

    def norm(hv, g_ref, n_ref):
        r = lax.rsqrt(jnp.mean(hv * hv, axis=-1, keepdims=True) + EPS)
        n_ref[...] = (hv * r * g_ref[...]).astype(BF16)

    if f is None:

        def body(h_ref, g_ref, n_ref):
            norm(h_ref[...], g_ref, n_ref)

        return _pcall(body, name=name, out_shape=jax.ShapeDtypeStruct((t, d), BF16), grid=(t // tt,),
                      in_specs=[row, vec], out_specs=row, dims=("parallel",))(h, gain)

    def body(h_ref, f_ref, g_ref, hn_ref, n_ref):
        hv = h_ref[...] + scale * f_ref[...]
        hn_ref[...] = hv
        norm(hv, g_ref, n_ref)

    return _pcall(body, name=name,
                  out_shape=(jax.ShapeDtypeStruct((t, d), F32), jax.ShapeDtypeStruct((t, d), BF16)),
                  grid=(t // tt,), in_specs=[row, row, vec], out_specs=(row, row), dims=("parallel",))(h, f, gain)


def _rms_bwd(dn, h, gain, dh_in, name, out_scale):
    t, d = h.shape
    tt = _tile(t, 512, 16)
    row = pl.BlockSpec((tt, d), lambda i: (i, 0))
    vec = pl.BlockSpec((1, d), lambda i: (0, 0))

    def body(dn_ref, h_ref, g_ref, dhin_ref, dh_ref, dhb_ref, dg_ref):
        hv = h_ref[...]
        dnv = dn_ref[...].astype(F32)
        r = lax.rsqrt(jnp.mean(hv * hv, axis=-1, keepdims=True) + EPS)
        dng = dnv * g_ref[...]
        dh = dhin_ref[...] + r * dng - hv * (r * r * r) * jnp.mean(dng * hv, axis=-1, keepdims=True)
        dh_ref[...] = dh
        dhb_ref[...] = (out_scale * dh).astype(BF16)
        part = jnp.sum(dnv * hv * r, axis=0, keepdims=True)

        @pl.when(pl.program_id(0) == 0)
        def _():
            dg_ref[...] = part

        @pl.when(pl.program_id(0) > 0)
        def _():
            dg_ref[...] += part

    return _pcall(body, name=name,
                  out_shape=(jax.ShapeDtypeStruct((t, d), F32), jax.ShapeDtypeStruct((t, d), BF16),
                             jax.ShapeDtypeStruct((1, d), F32)),
                  grid=(t // tt,), in_specs=[row, row, vec, row], out_specs=(row, row, vec),
                  dims=("arbitrary",))(dn, h, gain, dh_in)


def _swiglu_fwd(g, u, name):
    t, f = g.shape
    tt = _tile(t, 256, 16)
    row = pl.BlockSpec((tt, f), lambda i: (i, 0))

    def body(g_ref, u_ref, a_ref):
        gv = g_ref[...]
        a_ref[...] = (gv * _sigmoid(gv) * u_ref[...]).astype(BF16)

    return _pcall(body, name=name, out_shape=jax.ShapeDtypeStruct((t, f), BF16), grid=(t // tt,),
                  in_specs=[row, row], out_specs=row, dims=("parallel",))(g, u)


def _swiglu_bwd(da, g, u, name):
    t, f = g.shape
    tt = _tile(t, 256, 16)
    row = pl.BlockSpec((tt, f), lambda i: (i, 0))

    def body(da_ref, g_ref, u_ref, dg_ref, du_ref):
        gv = g_ref[...]
        dav = da_ref[...].astype(F32)
        sg = _sigmoid(gv)
        dg_ref[...] = (dav * u_ref[...] * (sg * (1.0 + gv * (1.0 - sg)))).astype(BF16)
        du_ref[...] = (dav * (gv * sg)).astype(BF16)

    return _pcall(body, name=name, out_shape=(jax.ShapeDtypeStruct((t, f), BF16),) * 2, grid=(t // tt,),
                  in_specs=[row, row, row], out_specs=(row, row), dims=("parallel",))(da, g, u)


def _rope_tables(pos_col, inv2, sign2):
    t = pos_col.shape[0]

    def body(p_ref, inv_ref, sg_ref, c_ref, s_ref):
        ang = p_ref[...].astype(F32) * inv_ref[...]
        c_ref[...] = jnp.cos(ang)
        s_ref[...] = jnp.sin(ang) * sg_ref[...]

    full = lambda shape: pl.BlockSpec(shape, lambda i: (0, 0))
    return _pcall(body, name="rope_tables", out_shape=(jax.ShapeDtypeStruct((t, RET_DIM), F32),) * 2, grid=(1,),
                  in_specs=[full((t, 1)), full((1, RET_DIM)), full((1, RET_DIM))],
                  out_specs=(full((t, RET_DIM)),) * 2, dims=("arbitrary",))(pos_col, inv2, sign2)


def _rot(x, c, s):
    return x * c + pltpu.roll(x, RET_DIM // 2, 1) * s


def _rot_t(g, c, s):
    return g * c + pltpu.roll(g * s, RET_DIM // 2, 1)


def _ret_consts():
    hh = np.arange(RET_HEADS, dtype=np.float32)
    log_gamma = np.log1p(-np.exp2(-5.0 - hh)).astype(np.float32)
    idx = np.arange(CHUNK, dtype=np.float32)
    diff = idx[:, None] - idx[None, :]
    dmask = np.where(diff >= 0, np.exp(log_gamma[:, None, None] * np.maximum(diff, 0.0)), 0.0).astype(np.float32)
    kdec = np.exp(log_gamma[:, None] * (CHUNK - 1 - idx)).astype(np.float32)
    qdec = np.exp(log_gamma[:, None] * (idx + 1.0)).astype(np.float32)
    cdec = np.exp(log_gamma * CHUNK).astype(np.float32)
    bc = lambda v: np.ascontiguousarray(np.broadcast_to(v[:, :, None], (RET_HEADS, CHUNK, RET_DIM)))
    cd = np.ascontiguousarray(np.broadcast_to(cdec[:, None, None], (RET_HEADS, 8, RET_DIM)))
    return jnp.asarray(dmask), jnp.asarray(bc(qdec)), jnp.asarray(bc(kdec)), jnp.asarray(cd)


def _ret_fwd(pm, cos2, sin2, consts):
    t = pm.shape[0]
    n_chunks = t // CHUNK
    dmask, qdec, kdec, cd = consts
    scale = RET_DIM ** -0.5

    def col(c0):
        return pl.BlockSpec((CHUNK, RET_DIM), lambda h, n: (n, c0 + h))

    tab = pl.BlockSpec((CHUNK, RET_DIM), lambda h, n: (n, 0))
    head3 = lambda r: pl.BlockSpec((None, r, RET_DIM), lambda h, n: (h, 0, 0))

    def body(q_ref, k_ref, v_ref, g_ref, c_ref, s_ref, dm_ref, qd_ref, kd_ref, cd_ref, y_ref, raw_ref, st_ref, s_acc):
        @pl.when(pl.program_id(1) == 0)
        def _():
            s_acc[...] = jnp.zeros_like(s_acc)

        c, s = c_ref[...], s_ref[...]
        q = _rot(q_ref[...], c, s)
        k = _rot(k_ref[...], c, s) * scale
        vb = v_ref[...].astype(BF16)
        g = g_ref[...]
        s_in = s_acc[...]
        st_ref[...] = s_in
        a = _dot(q.astype(BF16), k.astype(BF16), 1, 1) * dm_ref[...]
        y = _dot(a.astype(BF16), vb, 1, 0) + _dot((q * qd_ref[...]).astype(BF16), s_in.astype(BF16), 1, 0)
        s_acc[...] = cd_ref[0:1, :] * s_in + _dot((k * kd_ref[...]).astype(BF16), vb, 0, 0)
        raw_ref[...] = y
        mu = jnp.mean(y, axis=-1, keepdims=True)
        yc = y - mu
        rs = lax.rsqrt(jnp.mean(yc * yc, axis=-1, keepdims=True) + EPS)
        y_ref[...] = (yc * rs * (g * _sigmoid(g))).astype(BF16)

    out_blk = pl.BlockSpec((CHUNK, RET_DIM), lambda h, n: (n, h))
    return _pcall(
        body, name="retention_fwd",
        out_shape=(jax.ShapeDtypeStruct((t, RET_WIDTH), BF16), jax.ShapeDtypeStruct((t, RET_WIDTH), F32),
                   jax.ShapeDtypeStruct((RET_HEADS, n_chunks, RET_DIM, RET_DIM), F32)),
        grid=(RET_HEADS, n_chunks),
        in_specs=[col(TILE_RQ), col(TILE_RK), col(TILE_RV), col(TILE_RG), tab, tab,
                  pl.BlockSpec((None, CHUNK, CHUNK), lambda h, n: (h, 0, 0)), head3(CHUNK), head3(CHUNK), head3(8)],
        out_specs=(out_blk, out_blk, pl.BlockSpec((None, None, RET_DIM, RET_DIM), lambda h, n: (h, n, 0, 0))),
        scratch_shapes=[pltpu.VMEM((RET_DIM, RET_DIM), F32)],
        dims=("parallel", "arbitrary"),
    )(pm, pm, pm, pm, cos2, sin2, dmask, qdec, kdec, cd)


def _ret_bwd(dy, pm, cos2, sin2, raw, states, consts):
    t = pm.shape[0]
    n_chunks = t // CHUNK
    dmask, qdec, kdec, cd = consts
    scale = RET_DIM ** -0.5
    rev = lambda n: n_chunks - 1 - n

    def col(c0):
        return pl.BlockSpec((CHUNK, RET_DIM), lambda h, n: (rev(n), c0 + h))

    tab = pl.BlockSpec((CHUNK, RET_DIM), lambda h, n: (rev(n), 0))
    blk = pl.BlockSpec((CHUNK, RET_DIM), lambda h, n: (rev(n), h))
    head3 = lambda r: pl.BlockSpec((None, r, RET_DIM), lambda h, n: (h, 0, 0))

    def body(dy_ref, q_ref, k_ref, v_ref, g_ref, c_ref, s_ref, raw_ref, st_ref, dm_ref, qd_ref, kd_ref, cd_ref,
             dq_ref, dk_ref, dv_ref, dg_ref, ds_acc):
        @pl.when(pl.program_id(1) == 0)
        def _():
            ds_acc[...] = jnp.zeros_like(ds_acc)

        c, s = c_ref[...], s_ref[...]
        q = _rot(q_ref[...], c, s)
        k = _rot(k_ref[...], c, s) * scale
        qb, kb, vb = q.astype(BF16), k.astype(BF16), v_ref[...].astype(BF16)
        g = g_ref[...]
        dm, qd, kd = dm_ref[...], qd_ref[...], kd_ref[...]
        y = raw_ref[...]
        mu = jnp.mean(y, axis=-1, keepdims=True)
        yc = y - mu
        rs = lax.rsqrt(jnp.mean(yc * yc, axis=-1, keepdims=True) + EPS)
        yn = yc * rs
        sg = _sigmoid(g)
        dyo = dy_ref[...]
        dg_ref[...] = (dyo * yn * (sg * (1.0 + g * (1.0 - sg)))).astype(BF16)
        dyn = dyo * (g * sg)
        dyr = rs * (dyn - jnp.mean(dyn, axis=-1, keepdims=True) - yn * jnp.mean(dyn * yn, axis=-1, keepdims=True))
        dyb = dyr.astype(BF16)
        s_in = st_ref[...].astype(BF16)
        ds_out = ds_acc[...]
        dsb = ds_out.astype(BF16)
        a = _dot(qb, kb, 1, 1) * dm
        da = (_dot(dyb, vb, 1, 1) * dm).astype(BF16)
        kdb = (k * kd).astype(BF16)
        qdb = (q * qd).astype(BF16)
        dv_ref[...] = (_dot(a.astype(BF16), dyb, 0, 0) + _dot(kdb, dsb, 1, 0)).astype(BF16)
        dqh = _dot(da, kb, 1, 0) + _dot(dyb, s_in, 1, 1) * qd
        dkh = _dot(da, qb, 0, 0) + _dot(vb, dsb, 1, 1) * kd
        ds_acc[...] = cd_ref[0:1, :] * ds_out + _dot(qdb, dyb, 0, 0)
        dq_ref[...] = _rot_t(dqh, c, s).astype(BF16)
        dk_ref[...] = (_rot_t(dkh, c, s) * scale).astype(BF16)

    return _pcall(
        body, name="retention_bwd",
        out_shape=(jax.ShapeDtypeStruct((t, RET_WIDTH), BF16),) * 4,
        grid=(RET_HEADS, n_chunks),
        in_specs=[blk, col(TILE_RQ), col(TILE_RK), col(TILE_RV), col(TILE_RG), tab, tab, blk,
                  pl.BlockSpec((None, None, RET_DIM, RET_DIM), lambda h, n: (h, rev(n), 0, 0)),
                  pl.BlockSpec((None, CHUNK, CHUNK), lambda h, n: (h, 0, 0)), head3(CHUNK), head3(CHUNK), head3(8)],
        out_specs=(blk,) * 4,
        scratch_shapes=[pltpu.VMEM((RET_DIM, RET_DIM), F32)],
        dims=("parallel", "arbitrary"),
    )(dy, pm, pm, pm, pm, cos2, sin2, raw, states, dmask, qdec, kdec, cd)


FOX_C_LANE = FOX_DIM
FOX_NEGC_LANE = FOX_DIM + 3
FOX_LSE_LANE = FOX_DIM + 6
FOX_L_LANE = FOX_C_LANE
FOX_ROWSUM_LANE = FOX_C_LANE
FOX_COLSUM_LANE = FOX_NEGC_LANE


def _split3(x):
    hi = x.astype(BF16)
    r1 = x - hi.astype(F32)
    mid = r1.astype(BF16)
    lo = (r1 - mid.astype(F32)).astype(BF16)
    return hi, mid, lo


def _tri_dot(tri, x):
    hi, mid, lo = _split3(x)
    return _dot(tri, lo, 1, 0) + _dot(tri, mid, 1, 0) + _dot(tri, hi, 1, 0)


def _log_sigmoid(z):
    return jnp.minimum(z, 0.0) - jnp.log1p(jnp.exp(-jnp.abs(z)))


def _fox_consts():
    place = np.zeros((2, 3, LANES, FOX_TILES), np.float32)
    ones = np.zeros((3, 1, FOX_TILES), np.float32)
    for h in range(FOX_HEADS):
        for part in range(3):
            place[0, part, h, LANES * h + FOX_C_LANE + part] = 1.0
            place[1, part, h, LANES * h + FOX_NEGC_LANE + part] = -1.0
            ones[0, 0, LANES * h + FOX_NEGC_LANE + part] = 1.0
            ones[1, 0, LANES * h + FOX_C_LANE + part] = 1.0
            ones[1, 0, LANES * h + FOX_LSE_LANE + part] = 1.0
            ones[2, 0, LANES * h + FOX_C_LANE + part] = 1.0
    return jnp.asarray(place, BF16), jnp.asarray(ones, F32)


def _fox_prep(pm, bpad):
    t = pm.shape[0]
    tt = _tile(t, 512, LANES)
    place, ones = _fox_consts()
    wide = lambda c0: pl.BlockSpec((tt, FOX_TILES), lambda i: (i, c0 // FOX_HEADS))
    const = lambda a: pl.BlockSpec(a.shape, lambda i: (0,) * a.ndim)

    def body(q_ref, k_ref, v_ref, ff_ref, b_ref, pl_ref, on_ref, qa_ref, ka_ref, va_ref, carry_s):
        @pl.when(pl.program_id(0) == 0)
        def _():
            carry_s[...] = jnp.zeros_like(carry_s)

        r = lax.broadcasted_iota(jnp.int32, (LANES, LANES), 0)
        cc = lax.broadcasted_iota(jnp.int32, (LANES, LANES), 1)
        tri = jnp.where(cc <= r, 1.0, 0.0).astype(BF16)
        bias = b_ref[...]
        for sub in range(tt // LANES):
            rows = pl.ds(sub * LANES, LANES)
            cs = _tri_dot(tri, _log_sigmoid(ff_ref[rows, :] + bias)) + carry_s[...]
            carry_s[...] = cs[LANES - 1:LANES, :]
            parts = _split3(cs)
            eq = sum(_dot(part, pl_ref[0, i], 1, 0) for i, part in enumerate(parts))
            ek = sum(_dot(part, pl_ref[1, i], 1, 0) for i, part in enumerate(parts))
            qa_ref[rows, :] = (q_ref[rows, :] * FOX_DIM ** -0.5 + eq + on_ref[0]).astype(BF16)
            ka_ref[rows, :] = (k_ref[rows, :] + ek + on_ref[1]).astype(BF16)
            va_ref[rows, :] = (v_ref[rows, :] + on_ref[2]).astype(BF16)

    out = pl.BlockSpec((tt, FOX_TILES), lambda i: (i, 0))
    return _pcall(body, name="fox_prep", out_shape=(jax.ShapeDtypeStruct((t, FOX_TILES), BF16),) * 3, grid=(t // tt,),
                  in_specs=[wide(TILE_FQ), wide(TILE_FK), wide(TILE_FV), pl.BlockSpec((tt, LANES), lambda i: (i, TILE_FF)),
                            pl.BlockSpec((1, LANES), lambda i: (0, 0)), const(place), const(ones)],
                  out_specs=(out,) * 3, scratch_shapes=[pltpu.VMEM((1, LANES), F32)],
                  dims=("arbitrary",))(pm, pm, pm, pm, bpad, place, ones)


def _fox_post(dc, pm, bpad):
    t = pm.shape[0]
    nb = t // LANES

    def body(dc_ref, ff_ref, b_ref, d_ref, db_ref):
        r = lax.broadcasted_iota(jnp.int32, (LANES, LANES), 0)
        cc = lax.broadcasted_iota(jnp.int32, (LANES, LANES), 1)
        tri = jnp.where(cc >= r, 1.0, 0.0).astype(BF16)
        bias = b_ref[...]

        def step(i, carry):
            tail, acc = carry
            rows = pl.ds(pl.multiple_of((nb - 1 - i) * LANES, LANES), LANES)
            cs = _tri_dot(tri, dc_ref[rows, :]) + tail
            dff = cs * _sigmoid(-(ff_ref[rows, :] + bias))
            d_ref[rows, :] = dff.astype(BF16)
            return cs[0:1, :], acc + jnp.sum(dff, axis=0, keepdims=True)

        zero = jnp.zeros((1, LANES), F32)
        _, acc = lax.fori_loop(0, nb, step, (zero, zero))
        db_ref[...] = acc

    return _pcall(body, name="fox_forget_bwd",
                  out_shape=(jax.ShapeDtypeStruct((t, LANES), BF16), jax.ShapeDtypeStruct((1, LANES), F32)), grid=(1,),
                  in_specs=[pl.BlockSpec((t, LANES), lambda i: (0, 0)), pl.BlockSpec((t, LANES), lambda i: (0, TILE_FF)),
                            pl.BlockSpec((1, LANES), lambda i: (0, 0))],
                  out_specs=(pl.BlockSpec((t, LANES), lambda i: (0, 0)), pl.BlockSpec((1, LANES), lambda i: (0, 0))),
                  dims=("arbitrary",))(dc, pm, bpad)


def _tri_tables(nb, q_major):
    pairs = [(i, j) for i in range(nb) for j in range(i + 1)] if q_major else \
            [(i, j) for j in range(nb) for i in range(j, nb)]
    return jnp.asarray([a for a, _ in pairs], jnp.int32), jnp.asarray([b for _, b in pairs], jnp.int32)


def _causal(s):
    n = s.shape[0]
    row = lax.broadcasted_iota(jnp.int32, (n, n), 0)
    col = lax.broadcasted_iota(jnp.int32, (n, n), 1)
    return jnp.where(col <= row, s, NEG)


def _lane_col(x, lane):
    sel = lax.broadcasted_iota(jnp.int32, x.shape, 1) == lane
    return jnp.sum(jnp.where(sel, x, 0.0), axis=1, keepdims=True)


def _fox_fwd(qa, ka, va, blk):
    t = qa.shape[0]
    nb = t // blk
    qi, kj = _tri_tables(nb, True)
    q_spec = pl.BlockSpec((blk, LANES), lambda h, s, qi_r, kj_r: (qi_r[s], h))
    k_spec = pl.BlockSpec((blk, LANES), lambda h, s, qi_r, kj_r: (kj_r[s], h))

    def body(qi_r, kj_r, q_ref, k_ref, v_ref, o_ref, ob_ref, qb_ref, m_s, acc_s):
        s_id = pl.program_id(1)
        i, j = qi_r[s_id], kj_r[s_id]

        @pl.when(j == 0)
        def _():
            m_s[...] = jnp.full_like(m_s, NEG)
            acc_s[...] = jnp.zeros_like(acc_s)

        def tile(diagonal):
            s = _dot(q_ref[...], k_ref[...], 1, 1)
            if diagonal:
                s = _causal(s)
            m_old = m_s[...]
            m_new = jnp.maximum(m_old, jnp.max(s, axis=1, keepdims=True))
            p = jnp.exp(s - jnp.tile(m_new, (1, blk // LANES)))
            acc_s[...] = jnp.exp(m_old - m_new) * acc_s[...] + _dot(p.astype(BF16), v_ref[...], 1, 0)
            m_s[...] = m_new

        @pl.when(j < i)
        def _():
            tile(False)

        @pl.when(j == i)
        def _():
            tile(True)
            acc = acc_s[...]
            l = _lane_col(acc, FOX_L_LANE)
            o = acc / l
            o_ref[...] = o
            ob_ref[...] = o.astype(BF16)
            hi, mid, lo = _split3(-(m_s[:, 0:1] + jnp.log(l)))
            lane = lax.broadcasted_iota(jnp.int32, acc.shape, 1)
            qb_ref[...] = jnp.where(lane == FOX_LSE_LANE, hi,
                                    jnp.where(lane == FOX_LSE_LANE + 1, mid,
                                              jnp.where(lane == FOX_LSE_LANE + 2, lo, q_ref[...])))

    wide = (t, FOX_TILES)
    return pl.pallas_call(
        body, name="fox_fwd",
        out_shape=(jax.ShapeDtypeStruct(wide, F32), jax.ShapeDtypeStruct(wide, BF16), jax.ShapeDtypeStruct(wide, BF16)),
        grid_spec=pltpu.PrefetchScalarGridSpec(
            num_scalar_prefetch=2, grid=(FOX_HEADS, qi.shape[0]), in_specs=[q_spec, k_spec, k_spec],
            out_specs=(q_spec,) * 3,
            scratch_shapes=[pltpu.VMEM((blk, LANES), F32), pltpu.VMEM((blk, LANES), F32)]),
        compiler_params=_params(("parallel", "arbitrary")),
    )(qi, kj, qa, ka, va)


def _fox_bwd(qa, ka, va, do, o, blk):
    t = qa.shape[0]
    nb = t // blk
    qi, kj = _tri_tables(nb, False)
    q_spec = pl.BlockSpec((blk, LANES), lambda h, s, qi_r, kj_r: (qi_r[s], h))
    k_spec = pl.BlockSpec((blk, LANES), lambda h, s, qi_r, kj_r: (kj_r[s], h))
    head_spec = pl.BlockSpec((t, LANES), lambda h, s, qi_r, kj_r: (0, h))
    head_col = pl.BlockSpec((None, t, 1), lambda h, s, qi_r, kj_r: (h, 0, 0))
    k_col = pl.BlockSpec((None, blk, 1), lambda h, s, qi_r, kj_r: (h, kj_r[s], 0))
    first_spec = pl.BlockSpec((blk, LANES), lambda h, s, qi_r, kj_r: (jnp.where(kj_r[s] == 0, qi_r[s], nb - 1), h))
    n_steps = int(qi.shape[0])

    def body(qi_r, kj_r, q_ref, k_ref, v_ref, do_ref, o_ref, dq_ref, dk_ref, dv_ref, rs_ref, cs_ref,
             doa_s, dq_s, dk_s, dv_s):
        s_id = pl.program_id(1)
        i, j = qi_r[s_id], kj_r[s_id]
        rows = pl.ds(pl.multiple_of(i * blk, blk), blk)

        @pl.when(j == 0)
        def _():
            dof = do_ref[...]
            hi, mid, lo = _split3(-jnp.sum(dof * o_ref[...], axis=1, keepdims=True))
            lane = lax.broadcasted_iota(jnp.int32, dof.shape, 1)
            doa = jnp.where(lane == FOX_C_LANE, hi.astype(F32),
                            jnp.where(lane == FOX_C_LANE + 1, mid.astype(F32),
                                      jnp.where(lane == FOX_C_LANE + 2, lo.astype(F32), dof)))
            doa_s[rows, :] = doa.astype(BF16)
            dq_s[rows, :] = jnp.zeros((blk, LANES), F32)

        @pl.when(i == j)
        def _():
            dk_s[...] = jnp.zeros_like(dk_s)
            dv_s[...] = jnp.zeros_like(dv_s)

        def tile(diagonal):
            q, k = q_ref[...], k_ref[...]
            s = _dot(q, k, 1, 1)
            if diagonal:
                s = _causal(s)
            p = jnp.exp(s)
            doa = doa_s[rows, :]
            ds = (p * _dot(doa, v_ref[...], 1, 1)).astype(BF16)
            dv_s[...] += _dot(p.astype(BF16), doa, 0, 0)
            dk_s[...] += _dot(ds, q, 0, 0)
            dq_s[rows, :] += _dot(ds, k, 1, 0)

        @pl.when(i > j)
        def _():
            tile(False)

        @pl.when(i == j)
        def _():
            tile(True)

        @pl.when(i == nb - 1)
        def _():
            dk = dk_s[...]
            dk_ref[...] = dk.astype(BF16)
            dv_ref[...] = dv_s[...].astype(BF16)
            cs_ref[...] = -_lane_col(dk, FOX_COLSUM_LANE)

        @pl.when(s_id == n_steps - 1)
        def _():
            dq = dq_s[...]
            dq_ref[...] = (dq * FOX_DIM ** -0.5).astype(BF16)
            rs_ref[...] = _lane_col(dq, FOX_ROWSUM_LANE)

    wide = jax.ShapeDtypeStruct((t, FOX_TILES), BF16)
    cols = jax.ShapeDtypeStruct((FOX_HEADS, t, 1), F32)
    return pl.pallas_call(
        body, name="fox_bwd", out_shape=(wide, wide, wide, cols, cols),
        grid_spec=pltpu.PrefetchScalarGridSpec(
            num_scalar_prefetch=2, grid=(FOX_HEADS, n_steps),
            in_specs=[q_spec, k_spec, k_spec, first_spec, first_spec],
            out_specs=(head_spec, k_spec, k_spec, head_col, k_col),
            scratch_shapes=[pltpu.VMEM((t, LANES), BF16), pltpu.VMEM((t, LANES), F32), pltpu.VMEM((blk, LANES), F32),
                            pltpu.VMEM((blk, LANES), F32)]),
        compiler_params=_params(("parallel", "arbitrary")),
    )(qi, kj, qa, ka, va, do, o)


def _merge_fwd(gm, bm, za, zb):
    t, d = za.shape
    tt = _tile(t, 256, 16)
    row = pl.BlockSpec((tt, d), lambda i: (i, 0))

    def body(gm_ref, b_ref, za_ref, zb_ref, o_ref):
        ga = _sigmoid(gm_ref[:, :d] + b_ref[:, :d])
        gb = _sigmoid(gm_ref[:, d:] + b_ref[:, d:])
        o_ref[...] = (ga * za_ref[...] + gb * zb_ref[...]).astype(BF16)

    return _pcall(body, name="merge_fwd", out_shape=jax.ShapeDtypeStruct((t, d), BF16), grid=(t // tt,),
                  in_specs=[pl.BlockSpec((tt, 2 * d), lambda i: (i, 0)), pl.BlockSpec((1, 2 * d), lambda i: (0, 0)), row, row],
                  out_specs=row, dims=("parallel",))(gm, bm, za, zb)


def _merge_bwd(dmix, gm, bm, za, zb):
    t, d = za.shape
    tt = _tile(t, 256, 16)
    row = pl.BlockSpec((tt, d), lambda i: (i, 0))
    wide = pl.BlockSpec((tt, 2 * d), lambda i: (i, 0))
    vec = pl.BlockSpec((1, 2 * d), lambda i: (0, 0))

    def body(dm_ref, gm_ref, b_ref, za_ref, zb_ref, dza_ref, dzb_ref, dgm_ref, db_ref):
        dm = dm_ref[...]
        ga = _sigmoid(gm_ref[:, :d] + b_ref[:, :d])
        gb = _sigmoid(gm_ref[:, d:] + b_ref[:, d:])
        dza_ref[...] = (dm * ga).astype(BF16)
        dzb_ref[...] = (dm * gb).astype(BF16)
        dla = dm * za_ref[...] * ga * (1.0 - ga)
        dlb = dm * zb_ref[...] * gb * (1.0 - gb)
        dgm_ref[:, :d] = dla.astype(BF16)
        dgm_ref[:, d:] = dlb.astype(BF16)
        pa = jnp.sum(dla, axis=0, keepdims=True)
        pb = jnp.sum(dlb, axis=0, keepdims=True)

        @pl.when(pl.program_id(0) == 0)
        def _():
            db_ref[:, :d] = pa
            db_ref[:, d:] = pb

        @pl.when(pl.program_id(0) > 0)
        def _():
            db_ref[:, :d] += pa
            db_ref[:, d:] += pb

    return _pcall(body, name="merge_bwd",
                  out_shape=(jax.ShapeDtypeStruct((t, d), BF16), jax.ShapeDtypeStruct((t, d), BF16),
                             jax.ShapeDtypeStruct((t, 2 * d), BF16), jax.ShapeDtypeStruct((1, 2 * d), F32)),
                  grid=(t // tt,), in_specs=[row, wide, vec, row, row], out_specs=(row, row, wide, vec),
                  dims=("arbitrary",))(dmix, gm, bm, za, zb)


def _ple_final(h3, pgl, pe, gain, target):
    t, d = h3.shape
    tt = _tile(t, 256, 16)
    row = pl.BlockSpec((tt, d), lambda i: (i, 0))
    vec = pl.BlockSpec((1, d), lambda i: (0, 0))
    lvec = pl.BlockSpec((1, LANES), lambda i: (0, 0))

    def body(h_ref, pgl_ref, pe_ref, g_ref, t_ref, dh_ref, dsg_ref, dpe_ref, loss_ref, dg_ref):
        pg = _sigmoid(pgl_ref[...])
        pe_v = pe_ref[...]
        h4 = h_ref[...] + pg * pe_v
        r = lax.rsqrt(jnp.mean(h4 * h4, axis=-1, keepdims=True) + EPS)
        gv = g_ref[...]
        err = h4 * r * gv - t_ref[...]
        part_loss = 0.5 * jnp.sum(jnp.mean(err * err, axis=-1, keepdims=True), axis=0, keepdims=True)
        dy = err * (1.0 / d)
        part_g = jnp.sum(dy * h4 * r, axis=0, keepdims=True)
        dyg = dy * gv
        dh = r * dyg - h4 * (r * r * r) * jnp.mean(dyg * h4, axis=-1, keepdims=True)
        dh_ref[...] = dh
        dsg_ref[...] = (dh * pe_v * pg * (1.0 - pg)).astype(BF16)
        dpe_ref[...] = (dh * pg).astype(BF16)

        @pl.when(pl.program_id(0) == 0)
        def _():
            loss_ref[...] = jnp.broadcast_to(part_loss, (1, LANES))
            dg_ref[...] = part_g

        @pl.when(pl.program_id(0) > 0)
        def _():
            loss_ref[...] += jnp.broadcast_to(part_loss, (1, LANES))
            dg_ref[...] += part_g

    return _pcall(body, name="ple_final",
                  out_shape=(jax.ShapeDtypeStruct((t, d), F32), jax.ShapeDtypeStruct((t, d), BF16),
                             jax.ShapeDtypeStruct((t, d), BF16), jax.ShapeDtypeStruct((1, LANES), F32),
                             jax.ShapeDtypeStruct((1, d), F32)),
                  grid=(t // tt,), in_specs=[row, row, row, vec, row], out_specs=(row, row, row, lvec, vec),
                  dims=("arbitrary",))(h3, pgl, pe, gain, target)


def _adamw_math(w, g, m, v):
    m = ADAM_B1 * m + (1.0 - ADAM_B1) * g
    v = ADAM_B2 * v + (1.0 - ADAM_B2) * (g * g)
    m_hat = m / (1.0 - ADAM_B1 ** ADAM_STEP)
    v_hat = v / (1.0 - ADAM_B2 ** ADAM_STEP)
    delta = -ADAM_LR * (m_hat / (jnp.sqrt(v_hat) + ADAM_EPS) + ADAM_WD * w)
    return delta, m, v


def _adamw(parts, w, m, v, name):
    n, r, c = parts.shape
    tr = _tile(r, 256, 16)
    row = pl.BlockSpec((tr, c), lambda i: (i, 0))

    def body(p_ref, w_ref, m_ref, v_ref, g_ref, d_ref, mo_ref, vo_ref):
        g = p_ref[0].astype(F32)
        for s in range(1, n):
            g = g + p_ref[s].astype(F32)
        g_ref[...] = g
        d_ref[...], mo_ref[...], vo_ref[...] = _adamw_math(w_ref[...], g, m_ref[...], v_ref[...])

    return _pcall(body, name=name, out_shape=(jax.ShapeDtypeStruct((r, c), F32),) * 4, grid=(r // tr,),
                  in_specs=[pl.BlockSpec((n, tr, c), lambda i: (0, i, 0)), row, row, row], out_specs=(row,) * 4,
                  dims=("parallel",))(parts, w, m, v)


ANY = pl.BlockSpec(memory_space=pl.ANY)


def _all_gather(shards):
    n = len(shards)

    def body(*refs):
        x_refs, out_refs = refs[:n], refs[n:2 * n]
        send_sems, recv_sems, local_sems = refs[2 * n:]
        x, y, cc = lax.axis_index("x"), lax.axis_index("y"), lax.axis_index("c")
        me, sibling = (x, y, cc), (x, y, 1 - cc)
        chips = [(1 - x, y), (x, 1 - y), (1 - x, 1 - y)]

        def slot(a, px, py, pc):
            return out_refs[a].at[4 * px + 2 * py + pc]

        def copy(a, k, block, to, src=None):
            return pltpu.make_async_remote_copy(
                src_ref=slot(a, *block) if src is None else src, dst_ref=slot(a, *block),
                send_sem=send_sems.at[7 * a + k], recv_sem=recv_sems.at[7 * a + k], device_id=to, device_id_type=MESH)

        local, sent = [], []
        for a in range(n):
            local.append(pltpu.make_async_copy(x_refs[a], slot(a, *me), local_sems.at[a]))
            sent.append(copy(a, 0, me, sibling, src=x_refs[a]))
            sent += [copy(a, 1 + j, me, (*chip, cc), src=x_refs[a]) for j, chip in enumerate(chips)]
        for cp in local + sent:
            cp.start()
        for j, chip in enumerate(chips):
            for a in range(n):
                copy(a, 1 + j, (*chip, cc), me).wait_recv()
                sent.append(copy(a, 4 + j, (*chip, cc), sibling))
                sent[-1].start()
        for a in range(n):
            copy(a, 0, sibling, me).wait_recv()
            for j, chip in enumerate(chips):
                copy(a, 4 + j, (*chip, 1 - cc), me).wait_recv()
        for cp in sent:
            cp.wait_send()
        for cp in local:
            cp.wait()

    return pl.pallas_call(
        body, name="weights_all_gather",
        out_shape=tuple(jax.ShapeDtypeStruct((N_DEV,) + s.shape, s.dtype) for s in shards),
        in_specs=[ANY] * n, out_specs=(ANY,) * n,
        scratch_shapes=[pltpu.SemaphoreType.DMA((7 * n,)), pltpu.SemaphoreType.DMA((7 * n,)),
                        pltpu.SemaphoreType.DMA((n,))],
    )(*shards)


def _reduce_scatter_exchange(blocks):
    n = len(blocks)

    def body(*refs):
        g_refs, recv_refs = refs[:n], refs[n:2 * n]
        send_sems, recv_sems, local_sems = refs[2 * n:]
        x, y, cc = lax.axis_index("x"), lax.axis_index("y"), lax.axis_index("c")
        me = 4 * x + 2 * y + cc
        local, sent, landing = [], [], []
        for a in range(n):
            local.append(pltpu.make_async_copy(g_refs[a].at[me], recv_refs[a].at[me], local_sems.at[a]))
        for k in range(1, N_DEV):
            px, py, pc = x ^ (k >> 2), y ^ ((k >> 1) & 1), cc ^ (k & 1)
            peer = 4 * px + 2 * py + pc
            for a in range(n):
                sems = dict(send_sem=send_sems.at[7 * a + k - 1], recv_sem=recv_sems.at[7 * a + k - 1],
                            device_id=(px, py, pc), device_id_type=MESH)
                sent.append(pltpu.make_async_remote_copy(src_ref=g_refs[a].at[peer], dst_ref=recv_refs[a].at[me], **sems))
                landing.append(pltpu.make_async_remote_copy(src_ref=g_refs[a].at[me], dst_ref=recv_refs[a].at[peer], **sems))
        for cp in local + sent:
            cp.start()
        for cp in landing:
            cp.wait_recv()
        for cp in sent:
            cp.wait_send()
        for cp in local:
            cp.wait()

    return pl.pallas_call(
        body, name="grads_reduce_scatter_exchange",
        out_shape=tuple(jax.ShapeDtypeStruct(b.shape, b.dtype) for b in blocks),
        in_specs=[ANY] * n, out_specs=(ANY,) * n,
        scratch_shapes=[pltpu.SemaphoreType.DMA((7 * n,)), pltpu.SemaphoreType.DMA((7 * n,)),
                        pltpu.SemaphoreType.DMA((n,))],
    )(*blocks)


HBM = pl.BlockSpec(memory_space=pltpu.HBM)
SEM = pl.BlockSpec(memory_space=pltpu.SEMAPHORE)
DATAFLOW = pltpu.SideEffectType.DATAFLOW_SIDE_EFFECTING


def _peers():
    x, y, cc = lax.axis_index("x"), lax.axis_index("y"), lax.axis_index("c")
    out = []
    for k in range(1, N_DEV):
        px, py, pc = x ^ (k >> 2), y ^ ((k >> 1) & 1), cc ^ (k & 1)
        out.append((k, (px, py, pc), 4 * px + 2 * py + pc))
    return 4 * x + 2 * y + cc, out


def _scatter_start(blocks, name, gather=False):
    n = len(blocks)
    lands = [lax.empty((N_DEV,) + b.shape if gather else b.shape, b.dtype) for b in blocks]

    def body(*refs):
        g_refs, land_refs = refs[:n], refs[n:2 * n]
        send_sems, recv_sems, token = refs[2 * n], refs[2 * n + 1], refs[-1]
        me, peers = _peers()
        for k, peer, slot in peers:
            for a in range(n):
                pltpu.make_async_remote_copy(
                    src_ref=g_refs[a] if gather else g_refs[a].at[slot], dst_ref=land_refs[a].at[me],
                    send_sem=send_sems.at[7 * a + k - 1],
                    recv_sem=recv_sems.at[7 * a + k - 1], device_id=peer, device_id_type=MESH).start()
        token[...] = jnp.zeros_like(token)

    thru = [pltpu.HBM(b.shape, b.dtype) for b in blocks]
    thru_lands = [pltpu.HBM(b.shape, b.dtype) for b in lands]
    return pl.pallas_call(
        body, name=name,
        out_shape=(pltpu.SemaphoreType.DMA((7 * n,)), pltpu.SemaphoreType.DMA((7 * n,)), *thru, *thru_lands,
                   jax.ShapeDtypeStruct((8, LANES), F32)),
        in_specs=[HBM] * (2 * n), out_specs=(SEM, SEM, *[HBM] * (2 * n), pl.BlockSpec(memory_space=pltpu.VMEM)),
        input_output_aliases={i: 2 + i for i in range(2 * n)},
        compiler_params=pltpu.CompilerParams(has_side_effects=DATAFLOW),
    )(*[pltpu.with_memory_space_constraint(a, pltpu.HBM) for a in list(blocks) + lands])


def _scatter_wait(started, after, name, gather=False):
    send_sems, recv_sems, *rest = started
    n = (len(rest) - 1) // 2
    thru = rest[:2 * n]

    def body(*refs):
        g_refs, land_refs = refs[:n], refs[n:2 * n]
        send_sems, recv_sems = refs[2 * n], refs[2 * n + 1]
        me, peers = _peers()
        for k, peer, slot in peers:
            for a in range(n):
                copy = pltpu.make_async_remote_copy(
                    src_ref=g_refs[a] if gather else g_refs[a].at[slot], dst_ref=land_refs[a].at[slot],
                    send_sem=send_sems.at[7 * a + k - 1],
                    recv_sem=recv_sems.at[7 * a + k - 1], device_id=peer, device_id_type=MESH)
                copy.wait_send()
                copy.wait_recv()

    out = pl.pallas_call(
        body, name=name, out_shape=tuple(pltpu.HBM(a.shape, a.dtype) for a in thru),
        in_specs=[HBM] * (2 * n) + [SEM, SEM, pl.BlockSpec(memory_space=pl.ANY)], out_specs=tuple([HBM] * (2 * n)),
        input_output_aliases={i: i for i in range(2 * n)},
        compiler_params=pltpu.CompilerParams(has_side_effects=DATAFLOW),
    )(*thru, send_sems, recv_sems, after)
    return out[:n], out[n:]


BIG = (("w_ffn1_gate", "colT"), ("w_ffn1_up", "colT"), ("w_ffn1_down", "row"), ("w_ffn2_gate", "colT"),
       ("w_ffn2_up", "colT"), ("w_ffn2_down", "row"), ("w_in", "colT"), ("w_merge", "col"), ("w_ret_out", "col"),
       ("w_fox_out", "col"), ("w_out", "row"), ("w_ple", "col"), ("w_ple_gate", "row"))


def _shard_view(a, kind):
    a = a.reshape(a.shape[-2:])
    return a.T if kind == "colT" else a


def _unview(a, kind, shape):
    return (a.T if kind == "colT" else a).reshape(shape)


def _full_from_slots(g, kind):
    n, r, c = g.shape
    return g.transpose(1, 0, 2).reshape(r, n * c) if kind == "col" else g.reshape(n * r, c)


def _slots_from_full(f, kind):
    r, c = f.shape
    return f.reshape(r, N_DEV, c // N_DEV).transpose(1, 0, 2) if kind == "col" else f.reshape(N_DEV, r // N_DEV, c)


EARLY_GROUPS = (("w_ple_gate", "w_ple", "w_ffn2_down", "w_ffn2_gate", "w_ffn2_up", "w_out", "w_ret_out", "w_fox_out"),
                ("w_in", "w_merge"))


def _scatter_group(gw, names, name):
    kind = dict(BIG)
    return names, _scatter_start([_slots_from_full(gw[n], kind[n]) for n in names], name)


def _pad_heads(w):
    d = w.shape[1]
    return jnp.pad(w.reshape(FOX_HEADS, FOX_DIM, d), ((0, 0), (0, LANES - FOX_DIM), (0, 0))).reshape(FOX_TILES, d)


def _unpad_heads(w):
    d = w.shape[1]
    return w.reshape(FOX_HEADS, LANES, d)[:, :FOX_DIM].reshape(FOX_WIDTH, d)


def _deinterleave_rows(w):
    d = w.shape[1]
    return w.reshape(RET_HEADS, RET_DIM // 2, 2, d).transpose(0, 2, 1, 3).reshape(RET_WIDTH, d)


def _interleave_rows(w):
    d = w.shape[1]
    return w.reshape(RET_HEADS, 2, RET_DIM // 2, d).transpose(0, 2, 1, 3).reshape(RET_WIDTH, d)


def _pad_w_in(wt):
    d = wt.shape[1]
    rw, fw = RET_WIDTH, FOX_WIDTH
    fo = 4 * rw
    return jnp.concatenate([
        _deinterleave_rows(wt[:rw]), _deinterleave_rows(wt[rw:2 * rw]), wt[2 * rw:4 * rw],
        _pad_heads(wt[fo:fo + fw]), _pad_heads(wt[fo + fw:fo + 2 * fw]), _pad_heads(wt[fo + 2 * fw:fo + 3 * fw]),
        wt[fo + 3 * fw:], jnp.zeros((2 * LANES - FOX_HEADS, d), wt.dtype)], axis=0)


def _unpad_w_in(g):
    rw = RET_WIDTH
    f0 = 4 * rw
    return jnp.concatenate([
        _interleave_rows(g[:rw]), _interleave_rows(g[rw:2 * rw]), g[2 * rw:4 * rw],
        _unpad_heads(g[f0:f0 + FOX_TILES]), _unpad_heads(g[f0 + FOX_TILES:f0 + 2 * FOX_TILES]),
        _unpad_heads(g[f0 + 2 * FOX_TILES:f0 + 3 * FOX_TILES]),
        g[f0 + 3 * FOX_TILES:f0 + 3 * FOX_TILES + FOX_HEADS]], axis=0)


SMALL = ("ln_ffn1", "ln_mix", "b_forget", "b_merge", "ln_ffn2", "ln_ple", "ln_final")


def _small_rows(n):
    rows = -(-n // LANES)
    return -(-rows // 8) * 8


def _pack_small(vals, with_loss=None):
    parts = []
    for name in SMALL:
        v = vals[name].reshape(-1).astype(F32)
        rows = _small_rows(v.shape[0])
        parts.append(jnp.pad(v, (0, rows * LANES - v.shape[0])).reshape(rows, LANES))
    if with_loss is not None:
        parts.append(jnp.pad(with_loss.reshape(1, LANES), ((0, 7), (0, 0))))
    else:
        parts.append(jnp.zeros((8, LANES), F32))
    return jnp.concatenate(parts, axis=0)


def _unpack_small(packed, shapes):
    out, at = {}, 0
    for name in SMALL:
        n = int(np.prod(shapes[name]))
        rows = _small_rows(n)
        out[name] = packed[at:at + rows].reshape(-1)[:n].reshape(shapes[name])
        at += rows
    return out, packed[at, 0]


def _gather_finish(group, after, me):
    names, started, wait_name = group
    kind = dict(BIG)
    sent, landed = _scatter_wait(started, after, wait_name, gather=True)
    return {n: _full_from_slots(lax.dynamic_update_slice_in_dim(land, shard[None], me, 0), kind[n])
            for n, shard, land in zip(names, sent, landed)}


def _local_step(x, p, positions, target, w, small, me, entry_token, gathers):
    t, d = x.shape
    gain = lambda n: small[n].reshape(1, d)
    w = dict(w)
    bpad = jnp.pad(small["b_forget"].reshape(1, FOX_HEADS), ((0, 0), (0, LANES - FOX_HEADS)))
    bm = small["b_merge"].reshape(1, 2 * d)
    fox_blk = _tile(t, 512, 128)

    def ffn_fwd(n, tag):
        g = _mm([(n, w[f"w_{tag}_gate"], "nt")], F32, f"{tag}_gate", tn=1408)
        u = _mm([(n, w[f"w_{tag}_up"], "nt")], F32, f"{tag}_up", tn=1408)
        a = _swiglu_fwd(g, u, f"{tag}_swiglu")
        return g, u, a, _mm([(a, w[f"w_{tag}_down"], "nn")], F32, f"{tag}_down")

    n1 = _rms_fwd(x, gain("ln_ffn1") + entry_token, "rms_ffn1")
    g1, u1, a1, f1 = ffn_fwd(n1, "ffn1")
    h1, u = _rms_fwd(x, gain("ln_mix"), "rms_mix", f=f1)
    w.update(_gather_finish(gathers[0], f1, me))
    w_in_t = _pad_w_in(w["w_in"])
    gm = _mm([(u, w["w_merge"], "nn")], F32, "mixer_gates")
    pm = _mm([(u, w_in_t, "nt")], F32, "mixer_in", tn=1792)

    half = jnp.arange(RET_DIM // 2, dtype=F32) / (RET_DIM // 2)
    inv = 1.0 / (ROPE_BASE ** half)
    inv2 = jnp.concatenate([inv, inv]).reshape(1, RET_DIM)
    sign2 = jnp.concatenate([-jnp.ones((RET_DIM // 2,), F32), jnp.ones((RET_DIM // 2,), F32)]).reshape(1, RET_DIM)
    cos2, sin2 = _rope_tables(positions.reshape(t, 1), inv2, sign2)
    consts = _ret_consts()
    y_ret, y_raw, states = _ret_fwd(pm, cos2, sin2, consts)
    w.update(_gather_finish(gathers[1], y_raw, me))
    w_fox_pad = _pad_heads(w["w_fox_out"])
    za = _mm([(y_ret, w["w_ret_out"], "nn")], F32, "ret_out")

    qa, ka, va = _fox_prep(pm, bpad)
    o_fox, y_fox, qa_b = _fox_fwd(qa, ka, va, fox_blk)
    zb = _mm([(y_fox, w_fox_pad, "nn")], F32, "fox_out")

    mix = _merge_fwd(gm, bm, za, zb)
    mo = _mm([(mix, w["w_out"], "nn")], F32, "mix_out")
    h2, n2 = _rms_fwd(h1, gain("ln_ffn2"), "rms_ffn2", f=mo, scale=1.0)
    g2, u2, a2, f2 = ffn_fwd(n2, "ffn2")
    h3, n3 = _rms_fwd(h2, gain("ln_ple"), "rms_ple", f=f2)
    pgl = _mm([(n3, w["w_ple_gate"], "nn")], F32, "ple_gate")
    pb = p.astype(BF16)
    pe = _mm([(pb, w["w_ple"], "nn")], F32, "ple_embed")

    gw, gs = {}, {}
    dh4, dsg, dpe, loss, gs["ln_final"] = _ple_final(h3, pgl, pe, gain("ln_final"), target)
    gw["w_ple_gate"] = _mm([(n3, dsg, "tn")], BF16, "d_w_ple_gate", tn=256)
    gw["w_ple"] = _mm([(pb, dpe, "tn")], BF16, "d_w_ple", tn=256)
    dn3 = _mm([(dsg, w["w_ple_gate"], "nt")], F32, "d_n3")
    dh3, dh3_half, gs["ln_ple"] = _rms_bwd(dn3, h3, gain("ln_ple"), dh4, "rms_ple_bwd", 0.5)

    def ffn_bwd(dh_half, g, u_, a, n, tag):
        gw[f"w_{tag}_down"] = _mm([(a, dh_half, "tn")], BF16, f"d_w_{tag}_down", tm=1408, tn=256)
        da = _mm([(dh_half, w[f"w_{tag}_down"], "nt")], F32, f"d_a_{tag}", tn=1408)
        dg, du_ = _swiglu_bwd(da, g, u_, f"{tag}_swiglu_bwd")
        gw[f"w_{tag}_gate"] = _mm([(dg, n, "tn")], BF16, f"d_w_{tag}_gate", tm=1408, tn=256)
        gw[f"w_{tag}_up"] = _mm([(du_, n, "tn")], BF16, f"d_w_{tag}_up", tm=1408, tn=256)
        return _mm([(dg, w[f"w_{tag}_gate"], "nn"), (du_, w[f"w_{tag}_up"], "nn")], F32, f"d_n_{tag}", tm=512)

    dn2 = ffn_bwd(dh3_half, g2, u2, a2, n2, "ffn2")
    dh2, dh2_b, gs["ln_ffn2"] = _rms_bwd(dn2, h2, gain("ln_ffn2"), dh3, "rms_ffn2_bwd", 1.0)

    gw["w_out"] = _mm([(mix, dh2_b, "tn")], BF16, "d_w_out", tn=256)
    dmix = _mm([(dh2_b, w["w_out"], "nt")], F32, "d_mix")
    dza, dzb, dgm, gs["b_merge"] = _merge_bwd(dmix, gm, bm, za, zb)
    gw["w_ret_out"] = _mm([(y_ret, dza, "tn")], BF16, "d_w_ret_out", tn=256)
    gw["w_fox_out"] = _unpad_heads(_mm([(y_fox, dzb, "tn")], BF16, "d_w_fox_out", tn=256))
    dy_ret = _mm([(dza, w["w_ret_out"], "nt")], F32, "d_y_ret")
    do_fox = _mm([(dzb, w_fox_pad, "nt")], F32, "d_y_fox")

    pending = [_scatter_group(gw, EARLY_GROUPS[0], "grads_scatter_a_start")]
    token = pending[0][1][-1][0, 0]
    drq, drk, drv, drg = _ret_bwd(dy_ret, pm, cos2, sin2, y_raw, states, consts[:3] + (consts[3] + token,))

    dqa, dka, dva, ds_rows, ds_cols = _fox_bwd(qa_b, ka, va, do_fox, o_fox, fox_blk)
    dc = jnp.pad((ds_rows + ds_cols).reshape(FOX_HEADS, t).T, ((0, 0), (0, LANES - FOX_HEADS)))
    dff, db_forget = _fox_post(dc, pm, bpad)
    gs["b_forget"] = db_forget[:, :FOX_HEADS]

    dpm = jnp.concatenate([drq, drk, drv, drg, dqa, dka, dva, dff, jnp.zeros((t, LANES), BF16)], axis=1)
    gw["w_merge"] = _mm([(u, dgm, "tn")], BF16, "d_w_merge", tn=512)
    gw["w_in"] = _unpad_w_in(_mm([(dpm, u, "tn")], BF16, "d_w_in", tm=1792, tn=256))
    du = _mm([(dpm, w_in_t, "nn"), (dgm, w["w_merge"], "nt")], F32, "d_u", tm=512, tn=512)
    pending.append(_scatter_group(gw, EARLY_GROUPS[1], "grads_scatter_b_start"))
    token = pending[1][1][-1][0:1, 0:1]
    dh1, dh1_half, gs["ln_mix"] = _rms_bwd(du, h1, gain("ln_mix") + token, dh2, "rms_mix_bwd", 0.5)

    dn1 = ffn_bwd(dh1_half, g1, u1, a1, n1, "ffn1")
    dx, _, gs["ln_ffn1"] = _rms_bwd(dn1, x, gain("ln_ffn1"), dh1, "rms_ffn1_bwd", 1.0)
    return loss, dx, gw, gs, pending


WEIGHTS = ("ln_ffn1", "w_ffn1_gate", "w_ffn1_up", "w_ffn1_down", "ln_mix", "w_in", "b_forget", "w_merge", "b_merge",
           "w_ret_out", "w_fox_out", "w_out", "ln_ffn2", "w_ffn2_gate", "w_ffn2_up", "w_ffn2_down", "ln_ple", "w_ple",
           "w_ple_gate", "ln_final")


def kernel(x, p, positions, ln_ffn1, w_ffn1_gate, w_ffn1_up, w_ffn1_down, ln_mix, w_in, b_forget, w_merge, b_merge, w_ret_out, w_fox_out, w_out, ln_ffn2, w_ffn2_gate, w_ffn2_up, w_ffn2_down, ln_ple, w_ple, w_ple_gate, ln_final, loss_target, m_ln_ffn1, m_w_ffn1_gate, m_w_ffn1_up, m_w_ffn1_down, m_ln_mix, m_w_in, m_b_forget, m_w_merge, m_b_merge, m_w_ret_out, m_w_fox_out, m_w_out, m_ln_ffn2, m_w_ffn2_gate, m_w_ffn2_up, m_w_ffn2_down, m_ln_ple, m_w_ple, m_w_ple_gate, m_ln_final, v_ln_ffn1, v_w_ffn1_gate, v_w_ffn1_up, v_w_ffn1_down, v_ln_mix, v_w_in, v_b_forget, v_w_merge, v_b_merge, v_w_ret_out, v_w_fox_out, v_w_out, v_ln_ffn2, v_w_ffn2_gate, v_w_ffn2_up, v_w_ffn2_down, v_ln_ple, v_w_ple, v_w_ple_gate, v_ln_final):
    args = dict(ln_ffn1=ln_ffn1, w_ffn1_gate=w_ffn1_gate, w_ffn1_up=w_ffn1_up, w_ffn1_down=w_ffn1_down, ln_mix=ln_mix, w_in=w_in, b_forget=b_forget, w_merge=w_merge, b_merge=b_merge, w_ret_out=w_ret_out, w_fox_out=w_fox_out, w_out=w_out, ln_ffn2=ln_ffn2, w_ffn2_gate=w_ffn2_gate, w_ffn2_up=w_ffn2_up, w_ffn2_down=w_ffn2_down, ln_ple=ln_ple, w_ple=w_ple, w_ple_gate=w_ple_gate, ln_final=ln_final)
    moms = dict(ln_ffn1=m_ln_ffn1, w_ffn1_gate=m_w_ffn1_gate, w_ffn1_up=m_w_ffn1_up, w_ffn1_down=m_w_ffn1_down, ln_mix=m_ln_mix, w_in=m_w_in, b_forget=m_b_forget, w_merge=m_w_merge, b_merge=m_b_merge, w_ret_out=m_w_ret_out, w_fox_out=m_w_fox_out, w_out=m_w_out, ln_ffn2=m_ln_ffn2, w_ffn2_gate=m_w_ffn2_gate, w_ffn2_up=m_w_ffn2_up, w_ffn2_down=m_w_ffn2_down, ln_ple=m_ln_ple, w_ple=m_w_ple, w_ple_gate=m_w_ple_gate, ln_final=m_ln_final)
    vars_ = dict(ln_ffn1=v_ln_ffn1, w_ffn1_gate=v_w_ffn1_gate, w_ffn1_up=v_w_ffn1_up, w_ffn1_down=v_w_ffn1_down, ln_mix=v_ln_mix, w_in=v_w_in, b_forget=v_b_forget, w_merge=v_w_merge, b_merge=v_b_merge, w_ret_out=v_w_ret_out, w_fox_out=v_w_fox_out, w_out=v_w_out, ln_ffn2=v_ln_ffn2, w_ffn2_gate=v_w_ffn2_gate, w_ffn2_up=v_w_ffn2_up, w_ffn2_down=v_w_ffn2_down, ln_ple=v_ln_ple, w_ple=v_w_ple, w_ple_gate=v_w_ple_gate, ln_final=v_ln_final)
    kinds = ("grad", "delta", "new_m", "new_v")

    me = 4 * lax.axis_index("x") + 2 * lax.axis_index("y") + lax.axis_index("c")
    kind_of = dict(BIG)
    shard = {n: _shard_view(args[n], kind).astype(BF16) for n, kind in BIG}
    first = ("w_ffn1_gate", "w_ffn1_up", "w_ffn1_down")
    second = ("w_in", "w_merge")
    third = tuple(n for n, _ in BIG if n not in first + second)
    gathered = _all_gather([shard[n] for n in first])
    w_full = {n: _full_from_slots(g, kind_of[n]) for n, g in zip(first, gathered)}
    _, second_shards = lax.optimization_barrier((gathered, [shard[n] for n in second]))
    started_2 = _scatter_start(second_shards, "weights_gather_a_start", gather=True)
    _, third_shards = lax.optimization_barrier((started_2[-1], [shard[n] for n in third]))
    started_3 = _scatter_start(third_shards, "weights_gather_b_start", gather=True)
    gathers = [(second, started_2, "weights_gather_a_wait"), (third, started_3, "weights_gather_b_wait")]

    small = {n: args[n] for n in SMALL}
    loss_part, dx, gw, gs, pending = _local_step(x[0], p[0, 0], positions[0], loss_target[0], w_full, small, me,
                                                 started_3[-1][0:1, 0:1], gathers)

    parts_of = {}
    for tag, (names, started) in zip("ab", pending):
        sent, landed = _scatter_wait(started, dx, f"grads_scatter_{tag}_wait")
        for n, blk, land in zip(names, sent, landed):
            own = lax.dynamic_index_in_dim(blk, me, 0, keepdims=True)
            parts_of[n] = lax.dynamic_update_slice_in_dim(land, own, me, 0)
    late = [(n, kind) for n, kind in BIG if n not in parts_of]
    small_part = _pack_small(gs, with_loss=loss_part)
    blocks = [_slots_from_full(gw[n], kind) for n, kind in late]
    blocks.append(jnp.broadcast_to(small_part, (N_DEV,) + small_part.shape))
    recv = _reduce_scatter_exchange(blocks)
    parts_of.update({n: r for (n, _), r in zip(late, recv)})

    res = {}
    for n, kind in BIG:
        parts = parts_of[n]
        outs = _adamw(parts, _shard_view(args[n], kind), _shard_view(moms[n], kind), _shard_view(vars_[n], kind),
                      f"adamw_{n}")
        for what, o in zip(kinds, outs):
            res[(what, n)] = _unview(o, kind, args[n].shape)
    s_outs = _adamw(recv[-1], _pack_small(small), _pack_small({n: moms[n] for n in SMALL}),
                    _pack_small({n: vars_[n] for n in SMALL}), "adamw_small")
    for what, sm in zip(kinds, s_outs):
        svals, extra = _unpack_small(sm, {n: args[n].shape for n in SMALL})
        if what == "grad":
            loss = extra
        for n in SMALL:
            res[(what, n)] = svals[n]
    return (loss, dx[None], *[res[(what, n)] for what in kinds for n in WEIGHTS])
```

```python
import numpy as np
import jax
import jax.numpy as jnp
from jax import lax
from jax.experimental import pallas as pl
from jax.experimental.pallas import tpu as pltpu

F32 = jnp.float32
BF16 = jnp.bfloat16

N_DEV = 8
EPS = 1e-6
RET_HEADS = 4
RET_DIM = 128
RET_WIDTH = RET_HEADS * RET_DIM
FOX_HEADS = 8
FOX_DIM = 64
FOX_WIDTH = FOX_HEADS * FOX_DIM
CHUNK = 128
ROPE_BASE = 10000.0
LANES = 128
FOX_TILES = FOX_HEADS * LANES
IN_COLS = 4 * RET_WIDTH + 3 * FOX_WIDTH + FOX_HEADS
IN_PAD = 4 * RET_WIDTH + 3 * FOX_TILES + 2 * LANES
TILE_RQ, TILE_RK, TILE_RV, TILE_RG = 0, 4, 8, 12
TILE_FQ, TILE_FK, TILE_FV, TILE_FF = 16, 24, 32, 40
NEG = -1e30

ADAM_LR = 0.001
ADAM_B1 = 0.9
ADAM_B2 = 0.999
ADAM_EPS = 1e-08
ADAM_WD = 0.01
ADAM_STEP = 10

VMEM_LIMIT_BYTES = 56 * 1024 * 1024

MESH = pl.DeviceIdType.MESH


def _tile(dim, pref, mult):
    if dim <= pref:
        return dim
    t = (pref // mult) * mult
    while t >= mult:
        if dim % t == 0:
            return t
        t -= mult
    return dim


def _params(dims):
    return pltpu.CompilerParams(dimension_semantics=dims, vmem_limit_bytes=VMEM_LIMIT_BYTES)


def _pcall(body, *, name, out_shape, grid, in_specs, out_specs, scratch_shapes=(), dims=None):
    return pl.pallas_call(body, name=name, out_shape=out_shape, grid=grid, in_specs=in_specs, out_specs=out_specs,
                          scratch_shapes=list(scratch_shapes), compiler_params=_params(dims))


def _dot(a, b, ca, cb):
    return lax.dot_general(a, b, (((ca,), (cb,)), ((), ())), preferred_element_type=F32)


def _sigmoid(x):
    return 1.0 / (1.0 + jnp.exp(-x))


def _mm(pairs, out_dtype, name, tm=1024, tn=1024):
    dims = []
    for a, b, mode in pairs:
        m, k = (a.shape[1], a.shape[0]) if mode == "tn" else a.shape
        n, k2 = b.shape if mode == "nt" else (b.shape[1], b.shape[0])
        assert k == k2, (name, a.shape, b.shape, mode)
        dims.append((m, n))
    assert all(d == dims[0] for d in dims), (name, dims)
    m, n = dims[0]
    tm = _tile(m, tm, 128 if any(mode == "tn" for _, _, mode in pairs) else 16)
    tn = _tile(n, tn, 128)
    in_specs, contract, operands = [], [], []
    for a, b, mode in pairs:
        k = a.shape[0] if mode == "tn" else a.shape[1]
        in_specs.append(pl.BlockSpec((k, tm), lambda i, j: (0, i)) if mode == "tn" else
                        pl.BlockSpec((tm, k), lambda i, j: (i, 0)))
        in_specs.append(pl.BlockSpec((tn, k), lambda i, j: (j, 0)) if mode == "nt" else
                        pl.BlockSpec((k, tn), lambda i, j: (0, j)))
        contract.append((0 if mode == "tn" else 1, 1 if mode == "nt" else 0))
        operands += [a, b]

    def body(*refs):
        o_ref = refs[-1]
        acc = None
        for p, (ca, cb) in enumerate(contract):
            part = _dot(refs[2 * p][...], refs[2 * p + 1][...], ca, cb)
            acc = part if acc is None else acc + part
        o_ref[...] = acc.astype(out_dtype)

    return _pcall(body, name=name, out_shape=jax.ShapeDtypeStruct((m, n), out_dtype), grid=(m // tm, n // tn),
                  in_specs=in_specs, out_specs=pl.BlockSpec((tm, tn), lambda i, j: (i, j)),
                  dims=("parallel", "parallel"))(*operands)


def _rms_fwd(h, gain, name, f=None, scale=0.5):
    t, d = h.shape
    tt = _tile(t, 512, 16)
    row = pl.BlockSpec((tt, d), lambda i: (i, 0))
    vec = pl.BlockSpec((1, d), lambda i: (0, 0))

    def norm(hv, g_ref, n_ref):
        r = lax.rsqrt(jnp.mean(hv * hv, axis=-1, keepdims=True) + EPS)
        n_ref[...] = (hv * r * g_ref[...]).astype(BF16)

    if f is None:

        def body(h_ref, g_ref, n_ref):
            norm(h_ref[...], g_ref, n_ref)

        return _pcall(body, name=name, out_shape=jax.ShapeDtypeStruct((t, d), BF16), grid=(t // tt,),
                      in_specs=[row, vec], out_specs=row, dims=("parallel",))(h, gain)

    def body(h_ref, f_ref, g_ref, hn_ref, n_ref):
        hv = h_ref[...] + scale * f_ref[...]
        hn_ref[...] = hv
        norm(hv, g_ref, n_ref)

    return _pcall(body, name=name,
                  out_shape=(jax.ShapeDtypeStruct((t, d), F32), jax.ShapeDtypeStruct((t, d), BF16)),
                  grid=(t // tt,), in_specs=[row, row, vec], out_specs=(row, row), dims=("parallel",))(h, f, gain)


def _rms_bwd(dn, h, gain, dh_in, name, out_scale):
    t, d = h.shape
    tt = _tile(t, 512, 16)
    row = pl.BlockSpec((tt, d), lambda i: (i, 0))
    vec = pl.BlockSpec((1, d), lambda i: (0, 0))

    def body(dn_ref, h_ref, g_ref, dhin_ref, dh_ref, dhb_ref, dg_ref):
        hv = h_ref[...]
        dnv = dn_ref[...].astype(F32)
        r = lax.rsqrt(jnp.mean(hv * hv, axis=-1, keepdims=True) + EPS)
        dng = dnv * g_ref[...]
        dh = dhin_ref[...] + r * dng - hv * (r * r * r) * jnp.mean(dng * hv, axis=-1, keepdims=True)
        dh_ref[...] = dh
        dhb_ref[...] = (out_scale * dh).astype(BF16)
        part = jnp.sum(dnv * hv * r, axis=0, keepdims=True)

        @pl.when(pl.program_id(0) == 0)
        def _():
            dg_ref[...] = part

        @pl.when(pl.program_id(0) > 0)
        def _():
            dg_ref[...] += part

    return _pcall(body, name=name,
                  out_shape=(jax.ShapeDtypeStruct((t, d), F32), jax.ShapeDtypeStruct((t, d), BF16),
                             jax.ShapeDtypeStruct((1, d), F32)),
                  grid=(t // tt,), in_specs=[row, row, vec, row], out_specs=(row, row, vec),
                  dims=("arbitrary",))(dn, h, gain, dh_in)


def _swiglu_fwd(g, u, name):
    t, f = g.shape
    tt = _tile(t, 256, 16)
    row = pl.BlockSpec((tt, f), lambda i: (i, 0))

    def body(g_ref, u_ref, a_ref):
        gv = g_ref[...]
        a_ref[...] = (gv * _sigmoid(gv) * u_ref[...]).astype(BF16)

    return _pcall(body, name=name, out_shape=jax.ShapeDtypeStruct((t, f), BF16), grid=(t // tt,),
                  in_specs=[row, row], out_specs=row, dims=("parallel",))(g, u)


def _swiglu_bwd(da, g, u, name):
    t, f = g.shape
    tt = _tile(t, 256, 16)
    row = pl.BlockSpec((tt, f), lambda i: (i, 0))

    def body(da_ref, g_ref, u_ref, dg_ref, du_ref):
        gv = g_ref[...]
        dav = da_ref[...].astype(F32)
        sg = _sigmoid(gv)
        dg_ref[...] = (dav * u_ref[...] * (sg * (1.0 + gv * (1.0 - sg)))).astype(BF16)
        du_ref[...] = (dav * (gv * sg)).astype(BF16)

    return _pcall(body, name=name, out_shape=(jax.ShapeDtypeStruct((t, f), BF16),) * 2, grid=(t // tt,),
                  in_specs=[row, row, row], out_specs=(row, row), dims=("parallel",))(da, g, u)


def _rope_tables(pos_col, inv2, sign2):
    t = pos_col.shape[0]

    def body(p_ref, inv_ref, sg_ref, c_ref, s_ref):
        ang = p_ref[...].astype(F32) * inv_ref[...]
        c_ref[...] = jnp.cos(ang)
        s_ref[...] = jnp.sin(ang) * sg_ref[...]

    full = lambda shape: pl.BlockSpec(shape, lambda i: (0, 0))
    return _pcall(body, name="rope_tables", out_shape=(jax.ShapeDtypeStruct((t, RET_DIM), F32),) * 2, grid=(1,),
                  in_specs=[full((t, 1)), full((1, RET_DIM)), full((1, RET_DIM))],
                  out_specs=(full((t, RET_DIM)),) * 2, dims=("arbitrary",))(pos_col, inv2, sign2)


def _rot(x, c, s):
    return x * c + pltpu.roll(x, RET_DIM // 2, 1) * s


def _rot_t(g, c, s):
    return g * c + pltpu.roll(g * s, RET_DIM // 2, 1)


def _ret_consts():
    hh = np.arange(RET_HEADS, dtype=np.float32)
    log_gamma = np.log1p(-np.exp2(-5.0 - hh)).astype(np.float32)
    idx = np.arange(CHUNK, dtype=np.float32)
    diff = idx[:, None] - idx[None, :]
    dmask = np.where(diff >= 0, np.exp(log_gamma[:, None, None] * np.maximum(diff, 0.0)), 0.0).astype(np.float32)
    kdec = np.exp(log_gamma[:, None] * (CHUNK - 1 - idx)).astype(np.float32)
    qdec = np.exp(log_gamma[:, None] * (idx + 1.0)).astype(np.float32)
    cdec = np.exp(log_gamma * CHUNK).astype(np.float32)
    bc = lambda v: np.ascontiguousarray(np.broadcast_to(v[:, :, None], (RET_HEADS, CHUNK, RET_DIM)))
    cd = np.ascontiguousarray(np.broadcast_to(cdec[:, None, None], (RET_HEADS, 8, RET_DIM)))
    return jnp.asarray(dmask), jnp.asarray(bc(qdec)), jnp.asarray(bc(kdec)), jnp.asarray(cd)


def _ret_fwd(pm, cos2, sin2, consts):
    t = pm.shape[0]
    n_chunks = t // CHUNK
    dmask, qdec, kdec, cd = consts
    scale = RET_DIM ** -0.5

    def col(c0):
        return pl.BlockSpec((CHUNK, RET_DIM), lambda h, n: (n, c0 + h))

    tab = pl.BlockSpec((CHUNK, RET_DIM), lambda h, n: (n, 0))
    head3 = lambda r: pl.BlockSpec((None, r, RET_DIM), lambda h, n: (h, 0, 0))

    def body(q_ref, k_ref, v_ref, g_ref, c_ref, s_ref, dm_ref, qd_ref, kd_ref, cd_ref, y_ref, raw_ref, st_ref, s_acc):
        @pl.when(pl.program_id(1) == 0)
        def _():
            s_acc[...] = jnp.zeros_like(s_acc)

        c, s = c_ref[...], s_ref[...]
        q = _rot(q_ref[...], c, s)
        k = _rot(k_ref[...], c, s) * scale
        vb = v_ref[...].astype(BF16)
        g = g_ref[...]
        s_in = s_acc[...]
        st_ref[...] = s_in
        a = _dot(q.astype(BF16), k.astype(BF16), 1, 1) * dm_ref[...]
        y = _dot(a.astype(BF16), vb, 1, 0) + _dot((q * qd_ref[...]).astype(BF16), s_in.astype(BF16), 1, 0)
        s_acc[...] = cd_ref[0:1, :] * s_in + _dot((k * kd_ref[...]).astype(BF16), vb, 0, 0)
        raw_ref[...] = y
        mu = jnp.mean(y, axis=-1, keepdims=True)
        yc = y - mu
        rs = lax.rsqrt(jnp.mean(yc * yc, axis=-1, keepdims=True) + EPS)
        y_ref[...] = (yc * rs * (g * _sigmoid(g))).astype(BF16)

    out_blk = pl.BlockSpec((CHUNK, RET_DIM), lambda h, n: (n, h))
    return _pcall(
        body, name="retention_fwd",
        out_shape=(jax.ShapeDtypeStruct((t, RET_WIDTH), BF16), jax.ShapeDtypeStruct((t, RET_WIDTH), F32),
                   jax.ShapeDtypeStruct((RET_HEADS, n_chunks, RET_DIM, RET_DIM), F32)),
        grid=(RET_HEADS, n_chunks),
        in_specs=[col(TILE_RQ), col(TILE_RK), col(TILE_RV), col(TILE_RG), tab, tab,
                  pl.BlockSpec((None, CHUNK, CHUNK), lambda h, n: (h, 0, 0)), head3(CHUNK), head3(CHUNK), head3(8)],
        out_specs=(out_blk, out_blk, pl.BlockSpec((None, None, RET_DIM, RET_DIM), lambda h, n: (h, n, 0, 0))),
        scratch_shapes=[pltpu.VMEM((RET_DIM, RET_DIM), F32)],
        dims=("parallel", "arbitrary"),
    )(pm, pm, pm, pm, cos2, sin2, dmask, qdec, kdec, cd)


def _ret_bwd(dy, pm, cos2, sin2, raw, states, consts):
    t = pm.shape[0]
    n_chunks = t // CHUNK
    dmask, qdec, kdec, cd = consts
    scale = RET_DIM ** -0.5
    rev = lambda n: n_chunks - 1 - n

    def col(c0):
        return pl.BlockSpec((CHUNK, RET_DIM), lambda h, n: (rev(n), c0 + h))

    tab = pl.BlockSpec((CHUNK, RET_DIM), lambda h, n: (rev(n), 0))
    blk = pl.BlockSpec((CHUNK, RET_DIM), lambda h, n: (rev(n), h))
    head3 = lambda r: pl.BlockSpec((None, r, RET_DIM), lambda h, n: (h, 0, 0))

    def body(dy_ref, q_ref, k_ref, v_ref, g_ref, c_ref, s_ref, raw_ref, st_ref, dm_ref, qd_ref, kd_ref, cd_ref,
             dq_ref, dk_ref, dv_ref, dg_ref, ds_acc):
        @pl.when(pl.program_id(1) == 0)
        def _():
            ds_acc[...] = jnp.zeros_like(ds_acc)

        c, s = c_ref[...], s_ref[...]
        q = _rot(q_ref[...], c, s)
        k = _rot(k_ref[...], c, s) * scale
        qb, kb, vb = q.astype(BF16), k.astype(BF16), v_ref[...].astype(BF16)
        g = g_ref[...]
        dm, qd, kd = dm_ref[...], qd_ref[...], kd_ref[...]
        y = raw_ref[...]
        mu = jnp.mean(y, axis=-1, keepdims=True)
        yc = y - mu
        rs = lax.rsqrt(jnp.mean(yc * yc, axis=-1, keepdims=True) + EPS)
        yn = yc * rs
        sg = _sigmoid(g)
        dyo = dy_ref[...]
        dg_ref[...] = (dyo * yn * (sg * (1.0 + g * (1.0 - sg)))).astype(BF16)
        dyn = dyo * (g * sg)
        dyr = rs * (dyn - jnp.mean(dyn, axis=-1, keepdims=True) - yn * jnp.mean(dyn * yn, axis=-1, keepdims=True))
        dyb = dyr.astype(BF16)
        s_in = st_ref[...].astype(BF16)
        ds_out = ds_acc[...]
        dsb = ds_out.astype(BF16)
        a = _dot(qb, kb, 1, 1) * dm
        da = (_dot(dyb, vb, 1, 1) * dm).astype(BF16)
        kdb = (k * kd).astype(BF16)
        qdb = (q * qd).astype(BF16)
        dv_ref[...] = (_dot(a.astype(BF16), dyb, 0, 0) + _dot(kdb, dsb, 1, 0)).astype(BF16)
        dqh = _dot(da, kb, 1, 0) + _dot(dyb, s_in, 1, 1) * qd
        dkh = _dot(da, qb, 0, 0) + _dot(vb, dsb, 1, 1) * kd
        ds_acc[...] = cd_ref[0:1, :] * ds_out + _dot(qdb, dyb, 0, 0)
        dq_ref[...] = _rot_t(dqh, c, s).astype(BF16)
        dk_ref[...] = (_rot_t(dkh, c, s) * scale).astype(BF16)

    return _pcall(
        body, name="retention_bwd",
        out_shape=(jax.ShapeDtypeStruct((t, RET_WIDTH), BF16),) * 4,
        grid=(RET_HEADS, n_chunks),
        in_specs=[blk, col(TILE_RQ), col(TILE_RK), col(TILE_RV), col(TILE_RG), tab, tab, blk,
                  pl.BlockSpec((None, None, RET_DIM, RET_DIM), lambda h, n: (h, rev(n), 0, 0)),
                  pl.BlockSpec((None, CHUNK, CHUNK), lambda h, n: (h, 0, 0)), head3(CHUNK), head3(CHUNK), head3(8)],
        out_specs=(blk,) * 4,
        scratch_shapes=[pltpu.VMEM((RET_DIM, RET_DIM), F32)],
        dims=("parallel", "arbitrary"),
    )(dy, pm, pm, pm, pm, cos2, sin2, raw, states, dmask, qdec, kdec, cd)


FOX_C_LANE = FOX_DIM
FOX_NEGC_LANE = FOX_DIM + 3
FOX_LSE_LANE = FOX_DIM + 6
FOX_L_LANE = FOX_C_LANE
FOX_ROWSUM_LANE = FOX_C_LANE
FOX_COLSUM_LANE = FOX_NEGC_LANE


def _split3(x):
    hi = x.astype(BF16)
    r1 = x - hi.astype(F32)
    mid = r1.astype(BF16)
    lo = (r1 - mid.astype(F32)).astype(BF16)
    return hi, mid, lo


def _tri_dot(tri, x):
    hi, mid, lo = _split3(x)
    return _dot(tri, lo, 1, 0) + _dot(tri, mid, 1, 0) + _dot(tri, hi, 1, 0)


def _log_sigmoid(z):
    return jnp.minimum(z, 0.0) - jnp.log1p(jnp.exp(-jnp.abs(z)))


def _fox_consts():
    place = np.zeros((2, 3, LANES, FOX_TILES), np.float32)
    ones = np.zeros((3, 1, FOX_TILES), np.float32)
    for h in range(FOX_HEADS):
        for part in range(3):
            place[0, part, h, LANES * h + FOX_C_LANE + part] = 1.0
            place[1, part, h, LANES * h + FOX_NEGC_LANE + part] = -1.0
            ones[0, 0, LANES * h + FOX_NEGC_LANE + part] = 1.0
            ones[1, 0, LANES * h + FOX_C_LANE + part] = 1.0
            ones[1, 0, LANES * h + FOX_LSE_LANE + part] = 1.0
            ones[2, 0, LANES * h + FOX_C_LANE + part] = 1.0
    return jnp.asarray(place, BF16), jnp.asarray(ones, F32)


def _fox_prep(pm, bpad):
    t = pm.shape[0]
    tt = _tile(t, 512, LANES)
    place, ones = _fox_consts()
    wide = lambda c0: pl.BlockSpec((tt, FOX_TILES), lambda i: (i, c0 // FOX_HEADS))
    const = lambda a: pl.BlockSpec(a.shape, lambda i: (0,) * a.ndim)

    def body(q_ref, k_ref, v_ref, ff_ref, b_ref, pl_ref, on_ref, qa_ref, ka_ref, va_ref, carry_s):
        @pl.when(pl.program_id(0) == 0)
        def _():
            carry_s[...] = jnp.zeros_like(carry_s)

        r = lax.broadcasted_iota(jnp.int32, (LANES, LANES), 0)
        cc = lax.broadcasted_iota(jnp.int32, (LANES, LANES), 1)
        tri = jnp.where(cc <= r, 1.0, 0.0).astype(BF16)
        bias = b_ref[...]
        for sub in range(tt // LANES):
            rows = pl.ds(sub * LANES, LANES)
            cs = _tri_dot(tri, _log_sigmoid(ff_ref[rows, :] + bias)) + carry_s[...]
            carry_s[...] = cs[LANES - 1:LANES, :]
            parts = _split3(cs)
            eq = sum(_dot(part, pl_ref[0, i], 1, 0) for i, part in enumerate(parts))
            ek = sum(_dot(part, pl_ref[1, i], 1, 0) for i, part in enumerate(parts))
            qa_ref[rows, :] = (q_ref[rows, :] * FOX_DIM ** -0.5 + eq + on_ref[0]).astype(BF16)
            ka_ref[rows, :] = (k_ref[rows, :] + ek + on_ref[1]).astype(BF16)
            va_ref[rows, :] = (v_ref[rows, :] + on_ref[2]).astype(BF16)

    out = pl.BlockSpec((tt, FOX_TILES), lambda i: (i, 0))
    return _pcall(body, name="fox_prep", out_shape=(jax.ShapeDtypeStruct((t, FOX_TILES), BF16),) * 3, grid=(t // tt,),
                  in_specs=[wide(TILE_FQ), wide(TILE_FK), wide(TILE_FV), pl.BlockSpec((tt, LANES), lambda i: (i, TILE_FF)),
                            pl.BlockSpec((1, LANES), lambda i: (0, 0)), const(place), const(ones)],
                  out_specs=(out,) * 3, scratch_shapes=[pltpu.VMEM((1, LANES), F32)],
                  dims=("arbitrary",))(pm, pm, pm, pm, bpad, place, ones)


def _fox_post(dc, pm, bpad):
    t = pm.shape[0]
    nb = t // LANES

    def body(dc_ref, ff_ref, b_ref, d_ref, db_ref):
        r = lax.broadcasted_iota(jnp.int32, (LANES, LANES), 0)
        cc = lax.broadcasted_iota(jnp.int32, (LANES, LANES), 1)
        tri = jnp.where(cc >= r, 1.0, 0.0).astype(BF16)
        bias = b_ref[...]

        def step(i, carry):
            tail, acc = carry
            rows = pl.ds(pl.multiple_of((nb - 1 - i) * LANES, LANES), LANES)
            cs = _tri_dot(tri, dc_ref[rows, :]) + tail
            dff = cs * _sigmoid(-(ff_ref[rows, :] + bias))
            d_ref[rows, :] = dff.astype(BF16)
            return cs[0:1, :], acc + jnp.sum(dff, axis=0, keepdims=True)

        zero = jnp.zeros((1, LANES), F32)
        _, acc = lax.fori_loop(0, nb, step, (zero, zero))
        db_ref[...] = acc

    return _pcall(body, name="fox_forget_bwd",
                  out_shape=(jax.ShapeDtypeStruct((t, LANES), BF16), jax.ShapeDtypeStruct((1, LANES), F32)), grid=(1,),
                  in_specs=[pl.BlockSpec((t, LANES), lambda i: (0, 0)), pl.BlockSpec((t, LANES), lambda i: (0, TILE_FF)),
                            pl.BlockSpec((1, LANES), lambda i: (0, 0))],
                  out_specs=(pl.BlockSpec((t, LANES), lambda i: (0, 0)), pl.BlockSpec((1, LANES), lambda i: (0, 0))),
                  dims=("arbitrary",))(dc, pm, bpad)


def _tri_tables(nb, q_major):
    pairs = [(i, j) for i in range(nb) for j in range(i + 1)] if q_major else \
            [(i, j) for j in range(nb) for i in range(j, nb)]
    return jnp.asarray([a for a, _ in pairs], jnp.int32), jnp.asarray([b for _, b in pairs], jnp.int32)


def _causal(s):
    n = s.shape[0]
    row = lax.broadcasted_iota(jnp.int32, (n, n), 0)
    col = lax.broadcasted_iota(jnp.int32, (n, n), 1)
    return jnp.where(col <= row, s, NEG)


def _lane_col(x, lane):
    sel = lax.broadcasted_iota(jnp.int32, x.shape, 1) == lane
    return jnp.sum(jnp.where(sel, x, 0.0), axis=1, keepdims=True)


def _fox_fwd(qa, ka, va, blk):
    t = qa.shape[0]
    nb = t // blk
    qi, kj = _tri_tables(nb, True)
    q_spec = pl.BlockSpec((blk, LANES), lambda h, s, qi_r, kj_r: (qi_r[s], h))
    k_spec = pl.BlockSpec((blk, LANES), lambda h, s, qi_r, kj_r: (kj_r[s], h))

    def body(qi_r, kj_r, q_ref, k_ref, v_ref, o_ref, ob_ref, qb_ref, m_s, acc_s):
        s_id = pl.program_id(1)
        i, j = qi_r[s_id], kj_r[s_id]

        @pl.when(j == 0)
        def _():
            m_s[...] = jnp.full_like(m_s, NEG)
            acc_s[...] = jnp.zeros_like(acc_s)

        def tile(diagonal):
            s = _dot(q_ref[...], k_ref[...], 1, 1)
            if diagonal:
                s = _causal(s)
            m_old = m_s[...]
            m_new = jnp.maximum(m_old, jnp.max(s, axis=1, keepdims=True))
            p = jnp.exp(s - jnp.tile(m_new, (1, blk // LANES)))
            acc_s[...] = jnp.exp(m_old - m_new) * acc_s[...] + _dot(p.astype(BF16), v_ref[...], 1, 0)
            m_s[...] = m_new

        @pl.when(j < i)
        def _():
            tile(False)

        @pl.when(j == i)
        def _():
            tile(True)
            acc = acc_s[...]
            l = _lane_col(acc, FOX_L_LANE)
            o = acc / l
            o_ref[...] = o
            ob_ref[...] = o.astype(BF16)
            hi, mid, lo = _split3(-(m_s[:, 0:1] + jnp.log(l)))
            lane = lax.broadcasted_iota(jnp.int32, acc.shape, 1)
            qb_ref[...] = jnp.where(lane == FOX_LSE_LANE, hi,
                                    jnp.where(lane == FOX_LSE_LANE + 1, mid,
                                              jnp.where(lane == FOX_LSE_LANE + 2, lo, q_ref[...])))

    wide = (t, FOX_TILES)
    return pl.pallas_call(
        body, name="fox_fwd",
        out_shape=(jax.ShapeDtypeStruct(wide, F32), jax.ShapeDtypeStruct(wide, BF16), jax.ShapeDtypeStruct(wide, BF16)),
        grid_spec=pltpu.PrefetchScalarGridSpec(
            num_scalar_prefetch=2, grid=(FOX_HEADS, qi.shape[0]), in_specs=[q_spec, k_spec, k_spec],
            out_specs=(q_spec,) * 3,
            scratch_shapes=[pltpu.VMEM((blk, LANES), F32), pltpu.VMEM((blk, LANES), F32)]),
        compiler_params=_params(("parallel", "arbitrary")),
    )(qi, kj, qa, ka, va)


def _fox_bwd(qa, ka, va, do, o, blk):
    t = qa.shape[0]
    nb = t // blk
    qi, kj = _tri_tables(nb, False)
    q_spec = pl.BlockSpec((blk, LANES), lambda h, s, qi_r, kj_r: (qi_r[s], h))
    k_spec = pl.BlockSpec((blk, LANES), lambda h, s, qi_r, kj_r: (kj_r[s], h))
    head_spec = pl.BlockSpec((t, LANES), lambda h, s, qi_r, kj_r: (0, h))
    head_col = pl.BlockSpec((None, t, 1), lambda h, s, qi_r, kj_r: (h, 0, 0))
    k_col = pl.BlockSpec((None, blk, 1), lambda h, s, qi_r, kj_r: (h, kj_r[s], 0))
    first_spec = pl.BlockSpec((blk, LANES), lambda h, s, qi_r, kj_r: (jnp.where(kj_r[s] == 0, qi_r[s], nb - 1), h))
    n_steps = int(qi.shape[0])

    def body(qi_r, kj_r, q_ref, k_ref, v_ref, do_ref, o_ref, dq_ref, dk_ref, dv_ref, rs_ref, cs_ref,
             doa_s, dq_s, dk_s, dv_s):
        s_id = pl.program_id(1)
        i, j = qi_r[s_id], kj_r[s_id]
        rows = pl.ds(pl.multiple_of(i * blk, blk), blk)

        @pl.when(j == 0)
        def _():
            dof = do_ref[...]
            hi, mid, lo = _split3(-jnp.sum(dof * o_ref[...], axis=1, keepdims=True))
            lane = lax.broadcasted_iota(jnp.int32, dof.shape, 1)
            doa = jnp.where(lane == FOX_C_LANE, hi.astype(F32),
                            jnp.where(lane == FOX_C_LANE + 1, mid.astype(F32),
                                      jnp.where(lane == FOX_C_LANE + 2, lo.astype(F32), dof)))
            doa_s[rows, :] = doa.astype(BF16)
            dq_s[rows, :] = jnp.zeros((blk, LANES), F32)

        @pl.when(i == j)
        def _():
            dk_s[...] = jnp.zeros_like(dk_s)
            dv_s[...] = jnp.zeros_like(dv_s)

        def tile(diagonal):
            q, k = q_ref[...], k_ref[...]
            s = _dot(q, k, 1, 1)
            if diagonal:
                s = _causal(s)
            p = jnp.exp(s)
            doa = doa_s[rows, :]
            ds = (p * _dot(doa, v_ref[...], 1, 1)).astype(BF16)
            dv_s[...] += _dot(p.astype(BF16), doa, 0, 0)
            dk_s[...] += _dot(ds, q, 0, 0)
            dq_s[rows, :] += _dot(ds, k, 1, 0)

        @pl.when(i > j)
        def _():
            tile(False)

        @pl.when(i == j)
        def _():
            tile(True)

        @pl.when(i == nb - 1)
        def _():
            dk = dk_s[...]
            dk_ref[...] = dk.astype(BF16)
            dv_ref[...] = dv_s[...].astype(BF16)
            cs_ref[...] = -_lane_col(dk, FOX_COLSUM_LANE)

        @pl.when(s_id == n_steps - 1)
        def _():
            dq = dq_s[...]
            dq_ref[...] = (dq * FOX_DIM ** -0.5).astype(BF16)
            rs_ref[...] = _lane_col(dq, FOX_ROWSUM_LANE)

    wide = jax.ShapeDtypeStruct((t, FOX_TILES), BF16)
    cols = jax.ShapeDtypeStruct((FOX_HEADS, t, 1), F32)
    return pl.pallas_call(
        body, name="fox_bwd", out_shape=(wide, wide, wide, cols, cols),
        grid_spec=pltpu.PrefetchScalarGridSpec(
            num_scalar_prefetch=2, grid=(FOX_HEADS, n_steps),
            in_specs=[q_spec, k_spec, k_spec, first_spec, first_spec],
            out_specs=(head_spec, k_spec, k_spec, head_col, k_col),
            scratch_shapes=[pltpu.VMEM((t, LANES), BF16), pltpu.VMEM((t, LANES), F32), pltpu.VMEM((blk, LANES), F32),
                            pltpu.VMEM((blk, LANES), F32)]),
        compiler_params=_params(("parallel", "arbitrary")),
    )(qi, kj, qa, ka, va, do, o)


def _merge_fwd(gm, bm, za, zb):
    t, d = za.shape
    tt = _tile(t, 256, 16)
    row = pl.BlockSpec((tt, d), lambda i: (i, 0))

    def body(gm_ref, b_ref, za_ref, zb_ref, o_ref):
        ga = _sigmoid(gm_ref[:, :d] + b_ref[:, :d])
        gb = _sigmoid(gm_ref[:, d:] + b_ref[:, d:])
        o_ref[...] = (ga * za_ref[...] + gb * zb_ref[...]).astype(BF16)

    return _pcall(body, name="merge_fwd", out_shape=jax.ShapeDtypeStruct((t, d), BF16), grid=(t // tt,),
                  in_specs=[pl.BlockSpec((tt, 2 * d), lambda i: (i, 0)), pl.BlockSpec((1, 2 * d), lambda i: (0, 0)), row, row],
                  out_specs=row, dims=("parallel",))(gm, bm, za, zb)


def _merge_bwd(dmix, gm, bm, za, zb):
    t, d = za.shape
    tt = _tile(t, 256, 16)
    row = pl.BlockSpec((tt, d), lambda i: (i, 0))
    wide = pl.BlockSpec((tt, 2 * d), lambda i: (i, 0))
    vec = pl.BlockSpec((1, 2 * d), lambda i: (0, 0))

    def body(dm_ref, gm_ref, b_ref, za_ref, zb_ref, dza_ref, dzb_ref, dgm_ref, db_ref):
        dm = dm_ref[...]
        ga = _sigmoid(gm_ref[:, :d] + b_ref[:, :d])
        gb = _sigmoid(gm_ref[:, d:] + b_ref[:, d:])
        dza_ref[...] = (dm * ga).astype(BF16)
        dzb_ref[...] = (dm * gb).astype(BF16)
        dla = dm * za_ref[...] * ga * (1.0 - ga)
        dlb = dm * zb_ref[...] * gb * (1.0 - gb)
        dgm_ref[:, :d] = dla.astype(BF16)
        dgm_ref[:, d:] = dlb.astype(BF16)
        pa = jnp.sum(dla, axis=0, keepdims=True)
        pb = jnp.sum(dlb, axis=0, keepdims=True)

        @pl.when(pl.program_id(0) == 0)
        def _():
            db_ref[:, :d] = pa
            db_ref[:, d:] = pb

        @pl.when(pl.program_id(0) > 0)
        def _():
            db_ref[:, :d] += pa
            db_ref[:, d:] += pb

    return _pcall(body, name="merge_bwd",
                  out_shape=(jax.ShapeDtypeStruct((t, d), BF16), jax.ShapeDtypeStruct((t, d), BF16),
                             jax.ShapeDtypeStruct((t, 2 * d), BF16), jax.ShapeDtypeStruct((1, 2 * d), F32)),
                  grid=(t // tt,), in_specs=[row, wide, vec, row, row], out_specs=(row, row, wide, vec),
                  dims=("arbitrary",))(dmix, gm, bm, za, zb)


def _ple_final(h3, pgl, pe, gain, target):
    t, d = h3.shape
    tt = _tile(t, 256, 16)
    row = pl.BlockSpec((tt, d), lambda i: (i, 0))
    vec = pl.BlockSpec((1, d), lambda i: (0, 0))
    lvec = pl.BlockSpec((1, LANES), lambda i: (0, 0))

    def body(h_ref, pgl_ref, pe_ref, g_ref, t_ref, dh_ref, dsg_ref, dpe_ref, loss_ref, dg_ref):
        pg = _sigmoid(pgl_ref[...])
        pe_v = pe_ref[...]
        h4 = h_ref[...] + pg * pe_v
        r = lax.rsqrt(jnp.mean(h4 * h4, axis=-1, keepdims=True) + EPS)
        gv = g_ref[...]
        err = h4 * r * gv - t_ref[...]
        part_loss = 0.5 * jnp.sum(jnp.mean(err * err, axis=-1, keepdims=True), axis=0, keepdims=True)
        dy = err * (1.0 / d)
        part_g = jnp.sum(dy * h4 * r, axis=0, keepdims=True)
        dyg = dy * gv
        dh = r * dyg - h4 * (r * r * r) * jnp.mean(dyg * h4, axis=-1, keepdims=True)
        dh_ref[...] = dh
        dsg_ref[...] = (dh * pe_v * pg * (1.0 - pg)).astype(BF16)
        dpe_ref[...] = (dh * pg).astype(BF16)

        @pl.when(pl.program_id(0) == 0)
        def _():
            loss_ref[...] = jnp.broadcast_to(part_loss, (1, LANES))
            dg_ref[...] = part_g

        @pl.when(pl.program_id(0) > 0)
        def _():
            loss_ref[...] += jnp.broadcast_to(part_loss, (1, LANES))
            dg_ref[...] += part_g

    return _pcall(body, name="ple_final",
                  out_shape=(jax.ShapeDtypeStruct((t, d), F32), jax.ShapeDtypeStruct((t, d), BF16),
                             jax.ShapeDtypeStruct((t, d), BF16), jax.ShapeDtypeStruct((1, LANES), F32),
                             jax.ShapeDtypeStruct((1, d), F32)),
                  grid=(t // tt,), in_specs=[row, row, row, vec, row], out_specs=(row, row, row, lvec, vec),
                  dims=("arbitrary",))(h3, pgl, pe, gain, target)


def _adamw_math(w, g, m, v):
    m = ADAM_B1 * m + (1.0 - ADAM_B1) * g
    v = ADAM_B2 * v + (1.0 - ADAM_B2) * (g * g)
    m_hat = m / (1.0 - ADAM_B1 ** ADAM_STEP)
    v_hat = v / (1.0 - ADAM_B2 ** ADAM_STEP)
    delta = -ADAM_LR * (m_hat / (jnp.sqrt(v_hat) + ADAM_EPS) + ADAM_WD * w)
    return delta, m, v


def _adamw(parts, w, m, v, name):
    n, r, c = parts.shape
    tr = _tile(r, 256, 16)
    row = pl.BlockSpec((tr, c), lambda i: (i, 0))

    def body(p_ref, w_ref, m_ref, v_ref, g_ref, d_ref, mo_ref, vo_ref):
        g = p_ref[0].astype(F32)
        for s in range(1, n):
            g = g + p_ref[s].astype(F32)
        g_ref[...] = g
        d_ref[...], mo_ref[...], vo_ref[...] = _adamw_math(w_ref[...], g, m_ref[...], v_ref[...])

    return _pcall(body, name=name, out_shape=(jax.ShapeDtypeStruct((r, c), F32),) * 4, grid=(r // tr,),
                  in_specs=[pl.BlockSpec((n, tr, c), lambda i: (0, i, 0)), row, row, row], out_specs=(row,) * 4,
                  dims=("parallel",))(parts, w, m, v)


ANY = pl.BlockSpec(memory_space=pl.ANY)


def _all_gather(shards):
    n = len(shards)

    def body(*refs):
        x_refs, out_refs = refs[:n], refs[n:2 * n]
        send_sems, recv_sems, local_sems = refs[2 * n:]
        x, y, cc = lax.axis_index("x"), lax.axis_index("y"), lax.axis_index("c")
        me, sibling = (x, y, cc), (x, y, 1 - cc)
        chips = [(1 - x, y), (x, 1 - y), (1 - x, 1 - y)]

        def slot(a, px, py, pc):
            return out_refs[a].at[4 * px + 2 * py + pc]

        def copy(a, k, block, to, src=None):
            return pltpu.make_async_remote_copy(
                src_ref=slot(a, *block) if src is None else src, dst_ref=slot(a, *block),
                send_sem=send_sems.at[7 * a + k], recv_sem=recv_sems.at[7 * a + k], device_id=to, device_id_type=MESH)

        local, sent = [], []
        for a in range(n):
            local.append(pltpu.make_async_copy(x_refs[a], slot(a, *me), local_sems.at[a]))
            sent.append(copy(a, 0, me, sibling, src=x_refs[a]))
            sent += [copy(a, 1 + j, me, (*chip, cc), src=x_refs[a]) for j, chip in enumerate(chips)]
        for cp in local + sent:
            cp.start()
        for j, chip in enumerate(chips):
            for a in range(n):
                copy(a, 1 + j, (*chip, cc), me).wait_recv()
                sent.append(copy(a, 4 + j, (*chip, cc), sibling))
                sent[-1].start()
        for a in range(n):
            copy(a, 0, sibling, me).wait_recv()
            for j, chip in enumerate(chips):
                copy(a, 4 + j, (*chip, 1 - cc), me).wait_recv()
        for cp in sent:
            cp.wait_send()
        for cp in local:
            cp.wait()

    return pl.pallas_call(
        body, name="weights_all_gather",
        out_shape=tuple(jax.ShapeDtypeStruct((N_DEV,) + s.shape, s.dtype) for s in shards),
        in_specs=[ANY] * n, out_specs=(ANY,) * n,
        scratch_shapes=[pltpu.SemaphoreType.DMA((7 * n,)), pltpu.SemaphoreType.DMA((7 * n,)),
                        pltpu.SemaphoreType.DMA((n,))],
    )(*shards)


def _reduce_scatter_exchange(blocks):
    n = len(blocks)

    def body(*refs):
        g_refs, recv_refs = refs[:n], refs[n:2 * n]
        send_sems, recv_sems, local_sems = refs[2 * n:]
        x, y, cc = lax.axis_index("x"), lax.axis_index("y"), lax.axis_index("c")
        me = 4 * x + 2 * y + cc
        local, sent, landing = [], [], []
        for a in range(n):
            local.append(pltpu.make_async_copy(g_refs[a].at[me], recv_refs[a].at[me], local_sems.at[a]))
        for k in range(1, N_DEV):
            px, py, pc = x ^ (k >> 2), y ^ ((k >> 1) & 1), cc ^ (k & 1)
            peer = 4 * px + 2 * py + pc
            for a in range(n):
                sems = dict(send_sem=send_sems.at[7 * a + k - 1], recv_sem=recv_sems.at[7 * a + k - 1],
                            device_id=(px, py, pc), device_id_type=MESH)
                sent.append(pltpu.make_async_remote_copy(src_ref=g_refs[a].at[peer], dst_ref=recv_refs[a].at[me], **sems))
                landing.append(pltpu.make_async_remote_copy(src_ref=g_refs[a].at[me], dst_ref=recv_refs[a].at[peer], **sems))
        for cp in local + sent:
            cp.start()
        for cp in landing:
            cp.wait_recv()
        for cp in sent:
            cp.wait_send()
        for cp in local:
            cp.wait()

    return pl.pallas_call(
        body, name="grads_reduce_scatter_exchange",
        out_shape=tuple(jax.ShapeDtypeStruct(b.shape, b.dtype) for b in blocks),
        in_specs=[ANY] * n, out_specs=(ANY,) * n,
        scratch_shapes=[pltpu.SemaphoreType.DMA((7 * n,)), pltpu.SemaphoreType.DMA((7 * n,)),
                        pltpu.SemaphoreType.DMA((n,))],
    )(*blocks)


HBM = pl.BlockSpec(memory_space=pltpu.HBM)
SEM = pl.BlockSpec(memory_space=pltpu.SEMAPHORE)
DATAFLOW = pltpu.SideEffectType.DATAFLOW_SIDE_EFFECTING


def _peers():
    x, y, cc = lax.axis_index("x"), lax.axis_index("y"), lax.axis_index("c")
    out = []
    for k in range(1, N_DEV):
        px, py, pc = x ^ (k >> 2), y ^ ((k >> 1) & 1), cc ^ (k & 1)
        out.append((k, (px, py, pc), 4 * px + 2 * py + pc))
    return 4 * x + 2 * y + cc, out


def _scatter_start(blocks, after, name, gather=False):
    n = len(blocks)
    lands = [lax.empty((N_DEV,) + b.shape if gather else b.shape, b.dtype) for b in blocks]

    def body(*refs):
        g_refs, land_refs = refs[:n], refs[n:2 * n]
        send_sems, recv_sems, token = refs[2 * n + 1], refs[2 * n + 2], refs[-1]
        me, peers = _peers()
        for k, peer, slot in peers:
            for a in range(n):
                pltpu.make_async_remote_copy(
                    src_ref=g_refs[a] if gather else g_refs[a].at[slot], dst_ref=land_refs[a].at[me],
                    send_sem=send_sems.at[7 * a + k - 1],
                    recv_sem=recv_sems.at[7 * a + k - 1], device_id=peer, device_id_type=MESH).start()
        token[...] = jnp.zeros_like(token)

    thru = [pltpu.HBM(b.shape, b.dtype) for b in blocks]
    thru_lands = [pltpu.HBM(b.shape, b.dtype) for b in lands]
    return pl.pallas_call(
        body, name=name,
        out_shape=(pltpu.SemaphoreType.DMA((7 * n,)), pltpu.SemaphoreType.DMA((7 * n,)), *thru, *thru_lands,
                   jax.ShapeDtypeStruct((8, LANES), F32)),
        in_specs=[HBM] * (2 * n) + [pl.BlockSpec(memory_space=pl.ANY)],
        out_specs=(SEM, SEM, *[HBM] * (2 * n), pl.BlockSpec(memory_space=pltpu.VMEM)),
        input_output_aliases={i: 2 + i for i in range(2 * n)},
        compiler_params=pltpu.CompilerParams(has_side_effects=DATAFLOW),
    )(*[pltpu.with_memory_space_constraint(a, pltpu.HBM) for a in list(blocks) + lands], after)


def _scatter_wait(started, after, name, gather=False):
    send_sems, recv_sems, *rest = started
    n = (len(rest) - 1) // 2
    thru = rest[:2 * n]

    def body(*refs):
        g_refs, land_refs = refs[:n], refs[n:2 * n]
        send_sems, recv_sems = refs[2 * n], refs[2 * n + 1]
        me, peers = _peers()
        for k, peer, slot in peers:
            for a in range(n):
                copy = pltpu.make_async_remote_copy(
                    src_ref=g_refs[a] if gather else g_refs[a].at[slot], dst_ref=land_refs[a].at[slot],
                    send_sem=send_sems.at[7 * a + k - 1],
                    recv_sem=recv_sems.at[7 * a + k - 1], device_id=peer, device_id_type=MESH)
                copy.wait_send()
                copy.wait_recv()

    out = pl.pallas_call(
        body, name=name, out_shape=tuple(pltpu.HBM(a.shape, a.dtype) for a in thru),
        in_specs=[HBM] * (2 * n) + [SEM, SEM, pl.BlockSpec(memory_space=pl.ANY)], out_specs=tuple([HBM] * (2 * n)),
        input_output_aliases={i: i for i in range(2 * n)},
        compiler_params=pltpu.CompilerParams(has_side_effects=DATAFLOW),
    )(*thru, send_sems, recv_sems, after)
    return out[:n], out[n:]


BIG = (("w_ffn1_gate", "colT"), ("w_ffn1_up", "colT"), ("w_ffn1_down", "row"), ("w_ffn2_gate", "colT"),
       ("w_ffn2_up", "colT"), ("w_ffn2_down", "row"), ("w_in", "colT"), ("w_merge", "col"), ("w_ret_out", "col"),
       ("w_fox_out", "col"), ("w_out", "row"), ("w_ple", "col"), ("w_ple_gate", "row"))


def _shard_view(a, kind):
    a = a.reshape(a.shape[-2:])
    return a.T if kind == "colT" else a


def _unview(a, kind, shape):
    return (a.T if kind == "colT" else a).reshape(shape)


def _full_from_slots(g, kind):
    n, r, c = g.shape
    return g.transpose(1, 0, 2).reshape(r, n * c) if kind == "col" else g.reshape(n * r, c)


def _slots_from_full(f, kind):
    r, c = f.shape
    return f.reshape(r, N_DEV, c // N_DEV).transpose(1, 0, 2) if kind == "col" else f.reshape(N_DEV, r // N_DEV, c)


EARLY_GROUPS = (("w_ple_gate", "w_ple", "w_ffn2_down", "w_ffn2_gate", "w_ffn2_up", "w_out", "w_ret_out", "w_fox_out"),
                ("w_in", "w_merge"))


def _scatter_group(gw, names, after, name):
    kind = dict(BIG)
    return names, _scatter_start([_slots_from_full(gw[n], kind[n]) for n in names], after, name)


def _pad_heads(w):
    d = w.shape[1]
    return jnp.pad(w.reshape(FOX_HEADS, FOX_DIM, d), ((0, 0), (0, LANES - FOX_DIM), (0, 0))).reshape(FOX_TILES, d)


def _unpad_heads(w):
    d = w.shape[1]
    return w.reshape(FOX_HEADS, LANES, d)[:, :FOX_DIM].reshape(FOX_WIDTH, d)


def _deinterleave_rows(w):
    d = w.shape[1]
    return w.reshape(RET_HEADS, RET_DIM // 2, 2, d).transpose(0, 2, 1, 3).reshape(RET_WIDTH, d)


def _interleave_rows(w):
    d = w.shape[1]
    return w.reshape(RET_HEADS, 2, RET_DIM // 2, d).transpose(0, 2, 1, 3).reshape(RET_WIDTH, d)


def _pad_w_in(wt):
    d = wt.shape[1]
    rw, fw = RET_WIDTH, FOX_WIDTH
    fo = 4 * rw
    return jnp.concatenate([
        _deinterleave_rows(wt[:rw]), _deinterleave_rows(wt[rw:2 * rw]), wt[2 * rw:4 * rw],
        _pad_heads(wt[fo:fo + fw]), _pad_heads(wt[fo + fw:fo + 2 * fw]), _pad_heads(wt[fo + 2 * fw:fo + 3 * fw]),
        wt[fo + 3 * fw:], jnp.zeros((2 * LANES - FOX_HEADS, d), wt.dtype)], axis=0)


def _unpad_w_in(g):
    rw = RET_WIDTH
    f0 = 4 * rw
    return jnp.concatenate([
        _interleave_rows(g[:rw]), _interleave_rows(g[rw:2 * rw]), g[2 * rw:4 * rw],
        _unpad_heads(g[f0:f0 + FOX_TILES]), _unpad_heads(g[f0 + FOX_TILES:f0 + 2 * FOX_TILES]),
        _unpad_heads(g[f0 + 2 * FOX_TILES:f0 + 3 * FOX_TILES]),
        g[f0 + 3 * FOX_TILES:f0 + 3 * FOX_TILES + FOX_HEADS]], axis=0)


SMALL = ("ln_ffn1", "ln_mix", "b_forget", "b_merge", "ln_ffn2", "ln_ple", "ln_final")


def _small_rows(n):
    rows = -(-n // LANES)
    return -(-rows // 8) * 8


def _pack_small(vals, with_loss=None):
    parts = []
    for name in SMALL:
        v = vals[name].reshape(-1).astype(F32)
        rows = _small_rows(v.shape[0])
        parts.append(jnp.pad(v, (0, rows * LANES - v.shape[0])).reshape(rows, LANES))
    if with_loss is not None:
        parts.append(jnp.pad(with_loss.reshape(1, LANES), ((0, 7), (0, 0))))
    else:
        parts.append(jnp.zeros((8, LANES), F32))
    return jnp.concatenate(parts, axis=0)


def _unpack_small(packed, shapes):
    out, at = {}, 0
    for name in SMALL:
        n = int(np.prod(shapes[name]))
        rows = _small_rows(n)
        out[name] = packed[at:at + rows].reshape(-1)[:n].reshape(shapes[name])
        at += rows
    return out, packed[at, 0]


def _gather_finish(group, after, me):
    names, started, wait_name = group
    kind = dict(BIG)
    sent, landed = _scatter_wait(started, after, wait_name, gather=True)
    return {n: _full_from_slots(lax.dynamic_update_slice_in_dim(land, shard[None], me, 0), kind[n])
            for n, shard, land in zip(names, sent, landed)}


def _local_step(x, p, positions, target, w, small, me, entry_token, gathers):
    t, d = x.shape
    gain = lambda n: small[n].reshape(1, d)
    w = dict(w)
    bpad = jnp.pad(small["b_forget"].reshape(1, FOX_HEADS), ((0, 0), (0, LANES - FOX_HEADS)))
    bm = small["b_merge"].reshape(1, 2 * d)
    fox_blk = _tile(t, 512, 128)

    def ffn_fwd(n, tag):
        g = _mm([(n, w[f"w_{tag}_gate"], "nt")], F32, f"{tag}_gate", tn=1408)
        u = _mm([(n, w[f"w_{tag}_up"], "nt")], F32, f"{tag}_up", tn=1408)
        a = _swiglu_fwd(g, u, f"{tag}_swiglu")
        return g, u, a, _mm([(a, w[f"w_{tag}_down"], "nn")], F32, f"{tag}_down")

    n1 = _rms_fwd(x, gain("ln_ffn1") + entry_token, "rms_ffn1")
    g1, u1, a1, f1 = ffn_fwd(n1, "ffn1")
    h1, u = _rms_fwd(x, gain("ln_mix"), "rms_mix", f=f1)
    w.update(_gather_finish(gathers[0], f1, me))
    w_in_t = _pad_w_in(w["w_in"])
    gm = _mm([(u, w["w_merge"], "nn")], F32, "mixer_gates")
    pm = _mm([(u, w_in_t, "nt")], F32, "mixer_in", tn=1792)

    half = jnp.arange(RET_DIM // 2, dtype=F32) / (RET_DIM // 2)
    inv = 1.0 / (ROPE_BASE ** half)
    inv2 = jnp.concatenate([inv, inv]).reshape(1, RET_DIM)
    sign2 = jnp.concatenate([-jnp.ones((RET_DIM // 2,), F32), jnp.ones((RET_DIM // 2,), F32)]).reshape(1, RET_DIM)
    cos2, sin2 = _rope_tables(positions.reshape(t, 1), inv2, sign2)
    consts = _ret_consts()
    y_ret, y_raw, states = _ret_fwd(pm, cos2, sin2, consts)
    w.update(_gather_finish(gathers[1], y_raw, me))
    w_fox_pad = _pad_heads(w["w_fox_out"])
    za = _mm([(y_ret, w["w_ret_out"], "nn")], F32, "ret_out")

    qa, ka, va = _fox_prep(pm, bpad)
    o_fox, y_fox, qa_b = _fox_fwd(qa, ka, va, fox_blk)
    zb = _mm([(y_fox, w_fox_pad, "nn")], F32, "fox_out")

    mix = _merge_fwd(gm, bm, za, zb)
    mo = _mm([(mix, w["w_out"], "nn")], F32, "mix_out")
    h2, n2 = _rms_fwd(h1, gain("ln_ffn2"), "rms_ffn2", f=mo, scale=1.0)
    g2, u2, a2, f2 = ffn_fwd(n2, "ffn2")
    h3, n3 = _rms_fwd(h2, gain("ln_ple"), "rms_ple", f=f2)
    pgl = _mm([(n3, w["w_ple_gate"], "nn")], F32, "ple_gate")
    pb = p.astype(BF16)
    pe = _mm([(pb, w["w_ple"], "nn")], F32, "ple_embed")

    gw, gs = {}, {}
    dh4, dsg, dpe, loss, gs["ln_final"] = _ple_final(h3, pgl, pe, gain("ln_final"), target)
    gw["w_ple_gate"] = _mm([(n3, dsg, "tn")], BF16, "d_w_ple_gate", tn=256)
    gw["w_ple"] = _mm([(pb, dpe, "tn")], BF16, "d_w_ple", tn=256)
    dn3 = _mm([(dsg, w["w_ple_gate"], "nt")], F32, "d_n3")
    dh3, dh3_half, gs["ln_ple"] = _rms_bwd(dn3, h3, gain("ln_ple"), dh4, "rms_ple_bwd", 0.5)

    def ffn_bwd(dh_half, g, u_, a, n, tag):
        gw[f"w_{tag}_down"] = _mm([(a, dh_half, "tn")], BF16, f"d_w_{tag}_down", tm=1408, tn=256)
        da = _mm([(dh_half, w[f"w_{tag}_down"], "nt")], F32, f"d_a_{tag}", tn=1408)
        dg, du_ = _swiglu_bwd(da, g, u_, f"{tag}_swiglu_bwd")
        gw[f"w_{tag}_gate"] = _mm([(dg, n, "tn")], BF16, f"d_w_{tag}_gate", tm=1408, tn=256)
        gw[f"w_{tag}_up"] = _mm([(du_, n, "tn")], BF16, f"d_w_{tag}_up", tm=1408, tn=256)
        return _mm([(dg, w[f"w_{tag}_gate"], "nn"), (du_, w[f"w_{tag}_up"], "nn")], F32, f"d_n_{tag}", tm=512)

    dn2 = ffn_bwd(dh3_half, g2, u2, a2, n2, "ffn2")
    dh2, dh2_b, gs["ln_ffn2"] = _rms_bwd(dn2, h2, gain("ln_ffn2"), dh3, "rms_ffn2_bwd", 1.0)

    gw["w_out"] = _mm([(mix, dh2_b, "tn")], BF16, "d_w_out", tn=256)
    dmix = _mm([(dh2_b, w["w_out"], "nt")], F32, "d_mix")
    dza, dzb, dgm, gs["b_merge"] = _merge_bwd(dmix, gm, bm, za, zb)
    gw["w_ret_out"] = _mm([(y_ret, dza, "tn")], BF16, "d_w_ret_out", tn=256)
    gw["w_fox_out"] = _unpad_heads(_mm([(y_fox, dzb, "tn")], BF16, "d_w_fox_out", tn=256))
    dy_ret = _mm([(dza, w["w_ret_out"], "nt")], F32, "d_y_ret")
    do_fox = _mm([(dzb, w_fox_pad, "nt")], F32, "d_y_fox")

    pending = [_scatter_group(gw, EARLY_GROUPS[0], dy_ret, "grads_scatter_a_start")]
    token = pending[0][1][-1][0, 0]
    drq, drk, drv, drg = _ret_bwd(dy_ret, pm, cos2, sin2, y_raw, states, consts[:3] + (consts[3] + token,))

    dqa, dka, dva, ds_rows, ds_cols = _fox_bwd(qa_b, ka, va, do_fox, o_fox, fox_blk)
    dc = jnp.pad((ds_rows + ds_cols).reshape(FOX_HEADS, t).T, ((0, 0), (0, LANES - FOX_HEADS)))
    dff, db_forget = _fox_post(dc, pm, bpad)
    gs["b_forget"] = db_forget[:, :FOX_HEADS]

    dpm = jnp.concatenate([drq, drk, drv, drg, dqa, dka, dva, dff, jnp.zeros((t, LANES), BF16)], axis=1)
    gw["w_merge"] = _mm([(u, dgm, "tn")], BF16, "d_w_merge", tn=512)
    gw["w_in"] = _unpad_w_in(_mm([(dpm, u, "tn")], BF16, "d_w_in", tm=1792, tn=256))
    du = _mm([(dpm, w_in_t, "nn"), (dgm, w["w_merge"], "nt")], F32, "d_u", tm=512, tn=512)
    pending.append(_scatter_group(gw, EARLY_GROUPS[1], du, "grads_scatter_b_start"))
    token = pending[1][1][-1][0:1, 0:1]
    dh1, dh1_half, gs["ln_mix"] = _rms_bwd(du, h1, gain("ln_mix") + token, dh2, "rms_mix_bwd", 0.5)

    dn1 = ffn_bwd(dh1_half, g1, u1, a1, n1, "ffn1")
    dx, _, gs["ln_ffn1"] = _rms_bwd(dn1, x, gain("ln_ffn1"), dh1, "rms_ffn1_bwd", 1.0)
    return loss, dx, gw, gs, pending


WEIGHTS = ("ln_ffn1", "w_ffn1_gate", "w_ffn1_up", "w_ffn1_down", "ln_mix", "w_in", "b_forget", "w_merge", "b_merge",
           "w_ret_out", "w_fox_out", "w_out", "ln_ffn2", "w_ffn2_gate", "w_ffn2_up", "w_ffn2_down", "ln_ple", "w_ple",
           "w_ple_gate", "ln_final")


def kernel(x, p, positions, ln_ffn1, w_ffn1_gate, w_ffn1_up, w_ffn1_down, ln_mix, w_in, b_forget, w_merge, b_merge, w_ret_out, w_fox_out, w_out, ln_ffn2, w_ffn2_gate, w_ffn2_up, w_ffn2_down, ln_ple, w_ple, w_ple_gate, ln_final, loss_target, m_ln_ffn1, m_w_ffn1_gate, m_w_ffn1_up, m_w_ffn1_down, m_ln_mix, m_w_in, m_b_forget, m_w_merge, m_b_merge, m_w_ret_out, m_w_fox_out, m_w_out, m_ln_ffn2, m_w_ffn2_gate, m_w_ffn2_up, m_w_ffn2_down, m_ln_ple, m_w_ple, m_w_ple_gate, m_ln_final, v_ln_ffn1, v_w_ffn1_gate, v_w_ffn1_up, v_w_ffn1_down, v_ln_mix, v_w_in, v_b_forget, v_w_merge, v_b_merge, v_w_ret_out, v_w_fox_out, v_w_out, v_ln_ffn2, v_w_ffn2_gate, v_w_ffn2_up, v_w_ffn2_down, v_ln_ple, v_w_ple, v_w_ple_gate, v_ln_final):
    args = dict(ln_ffn1=ln_ffn1, w_ffn1_gate=w_ffn1_gate, w_ffn1_up=w_ffn1_up, w_ffn1_down=w_ffn1_down, ln_mix=ln_mix, w_in=w_in, b_forget=b_forget, w_merge=w_merge, b_merge=b_merge, w_ret_out=w_ret_out, w_fox_out=w_fox_out, w_out=w_out, ln_ffn2=ln_ffn2, w_ffn2_gate=w_ffn2_gate, w_ffn2_up=w_ffn2_up, w_ffn2_down=w_ffn2_down, ln_ple=ln_ple, w_ple=w_ple, w_ple_gate=w_ple_gate, ln_final=ln_final)
    moms = dict(ln_ffn1=m_ln_ffn1, w_ffn1_gate=m_w_ffn1_gate, w_ffn1_up=m_w_ffn1_up, w_ffn1_down=m_w_ffn1_down, ln_mix=m_ln_mix, w_in=m_w_in, b_forget=m_b_forget, w_merge=m_w_merge, b_merge=m_b_merge, w_ret_out=m_w_ret_out, w_fox_out=m_w_fox_out, w_out=m_w_out, ln_ffn2=m_ln_ffn2, w_ffn2_gate=m_w_ffn2_gate, w_ffn2_up=m_w_ffn2_up, w_ffn2_down=m_w_ffn2_down, ln_ple=m_ln_ple, w_ple=m_w_ple, w_ple_gate=m_w_ple_gate, ln_final=m_ln_final)
    vars_ = dict(ln_ffn1=v_ln_ffn1, w_ffn1_gate=v_w_ffn1_gate, w_ffn1_up=v_w_ffn1_up, w_ffn1_down=v_w_ffn1_down, ln_mix=v_ln_mix, w_in=v_w_in, b_forget=v_b_forget, w_merge=v_w_merge, b_merge=v_b_merge, w_ret_out=v_w_ret_out, w_fox_out=v_w_fox_out, w_out=v_w_out, ln_ffn2=v_ln_ffn2, w_ffn2_gate=v_w_ffn2_gate, w_ffn2_up=v_w_ffn2_up, w_ffn2_down=v_w_ffn2_down, ln_ple=v_ln_ple, w_ple=v_w_ple, w_ple_gate=v_w_ple_gate, ln_final=v_ln_final)
    kinds = ("grad", "delta", "new_m", "new_v")

    me = 4 * lax.axis_index("x") + 2 * lax.axis_index("y") + lax.axis_index("c")
    kind_of = dict(BIG)
    shard = {n: _shard_view(args[n], kind).astype(BF16) for n, kind in BIG}
    first = ("w_ffn1_gate", "w_ffn1_up", "w_ffn1_down")
    second = ("w_in", "w_merge")
    third = tuple(n for n, _ in BIG if n not in first + second)
    gathered = _all_gather([shard[n] for n in first])
    w_full = {n: _full_from_slots(g, kind_of[n]) for n, g in zip(first, gathered)}
    started_2 = _scatter_start([shard[n] for n in second], gathered[0], "weights_gather_a_start", gather=True)
    started_3 = _scatter_start([shard[n] for n in third], started_2[-1], "weights_gather_b_start", gather=True)
    gathers = [(second, started_2, "weights_gather_a_wait"), (third, started_3, "weights_gather_b_wait")]

    small = {n: args[n] for n in SMALL}
    loss_part, dx, gw, gs, pending = _local_step(x[0], p[0, 0], positions[0], loss_target[0], w_full, small, me,
                                                 started_3[-1][0:1, 0:1], gathers)

    parts_of = {}
    for tag, (names, started) in zip("ab", pending):
        sent, landed = _scatter_wait(started, dx, f"grads_scatter_{tag}_wait")
        for n, blk, land in zip(names, sent, landed):
            own = lax.dynamic_index_in_dim(blk, me, 0, keepdims=True)
            parts_of[n] = lax.dynamic_update_slice_in_dim(land, own, me, 0)
    late = [(n, kind) for n, kind in BIG if n not in parts_of]
    small_part = _pack_small(gs, with_loss=loss_part)
    blocks = [_slots_from_full(gw[n], kind) for n, kind in late]
    blocks.append(jnp.broadcast_to(small_part, (N_DEV,) + small_part.shape))
    recv = _reduce_scatter_exchange(blocks)
    parts_of.update({n: r for (n, _), r in zip(late, recv)})

    res = {}
    for n, kind in BIG:
        parts = parts_of[n]
        outs = _adamw(parts, _shard_view(args[n], kind), _shard_view(moms[n], kind), _shard_view(vars_[n], kind),
                      f"adamw_{n}")
        for what, o in zip(kinds, outs):
            res[(what, n)] = _unview(o, kind, args[n].shape)
    s_outs = _adamw(recv[-1], _pack_small(small), _pack_small({n: moms[n] for n in SMALL}),
                    _pack_small({n: vars_[n] for n in SMALL}), "adamw_small")
    for what, sm in zip(kinds, s_outs):
        svals, extra = _unpack_small(sm, {n: args[n].shape for n in SMALL})
        if what == "grad":
            loss = extra
        for n in SMALL:
            res[(what, n)] = svals[n]
    return (loss, dx[None], *[res[(what, n)] for what in kinds for n in WEIGHTS])
```

```python
import numpy as np
import jax
import jax.numpy as jnp
from jax import lax
from jax.experimental import pallas as pl
from jax.experimental.pallas import tpu as pltpu

F32 = jnp.float32
BF16 = jnp.bfloat16

N_DEV = 8
EPS = 1e-6
RET_HEADS = 4
RET_DIM = 128
RET_WIDTH = RET_HEADS * RET_DIM
FOX_HEADS = 8
FOX_DIM = 64
FOX_WIDTH = FOX_HEADS * FOX_DIM
CHUNK = 128
ROPE_BASE = 10000.0
LANES = 128
FOX_TILES = FOX_HEADS * LANES
IN_COLS = 4 * RET_WIDTH + 3 * FOX_WIDTH + FOX_HEADS
IN_PAD = 4 * RET_WIDTH + 3 * FOX_TILES + 2 * LANES
TILE_RQ, TILE_RK, TILE_RV, TILE_RG = 0, 4, 8, 12
TILE_FQ, TILE_FK, TILE_FV, TILE_FF = 16, 24, 32, 40
NEG = -1e30

ADAM_LR = 0.001
ADAM_B1 = 0.9
ADAM_B2 = 0.999
ADAM_EPS = 1e-08
ADAM_WD = 0.01
ADAM_STEP = 10

VMEM_LIMIT_BYTES = 56 * 1024 * 1024

MESH = pl.DeviceIdType.MESH


def _tile(dim, pref, mult):
    if dim <= pref:
        return dim
    t = (pref // mult) * mult
    while t >= mult:
        if dim % t == 0:
            return t
        t -= mult
    return dim


def _params(dims):
    return pltpu.CompilerParams(dimension_semantics=dims, vmem_limit_bytes=VMEM_LIMIT_BYTES)


def _pcall(body, *, name, out_shape, grid, in_specs, out_specs, scratch_shapes=(), dims=None):
    return pl.pallas_call(body, name=name, out_shape=out_shape, grid=grid, in_specs=in_specs, out_specs=out_specs,
                          scratch_shapes=list(scratch_shapes), compiler_params=_params(dims))


def _dot(a, b, ca, cb):
    return lax.dot_general(a, b, (((ca,), (cb,)), ((), ())), preferred_element_type=F32)


def _sigmoid(x):
    return 1.0 / (1.0 + jnp.exp(-x))


def _mm(pairs, out_dtype, name, tm=1024, tn=1024):
    dims = []
    for a, b, mode in pairs:
        m, k = (a.shape[1], a.shape[0]) if mode == "tn" else a.shape
        n, k2 = b.shape if mode == "nt" else (b.shape[1], b.shape[0])
        assert k == k2, (name, a.shape, b.shape, mode)
        dims.append((m, n))
    assert all(d == dims[0] for d in dims), (name, dims)
    m, n = dims[0]
    tm = _tile(m, tm, 128 if any(mode == "tn" for _, _, mode in pairs) else 16)
    tn = _tile(n, tn, 128)
    in_specs, contract, operands = [], [], []
    for a, b, mode in pairs:
        k = a.shape[0] if mode == "tn" else a.shape[1]
        in_specs.append(pl.BlockSpec((k, tm), lambda i, j: (0, i)) if mode == "tn" else
                        pl.BlockSpec((tm, k), lambda i, j: (i, 0)))
        in_specs.append(pl.BlockSpec((tn, k), lambda i, j: (j, 0)) if mode == "nt" else
                        pl.BlockSpec((k, tn), lambda i, j: (0, j)))
        contract.append((0 if mode == "tn" else 1, 1 if mode == "nt" else 0))
        operands += [a, b]

    def body(*refs):
        o_ref = refs[-1]
        acc = None
        for p, (ca, cb) in enumerate(contract):
            part = _dot(refs[2 * p][...], refs[2 * p + 1][...], ca, cb)
            acc = part if acc is None else acc + part
        o_ref[...] = acc.astype(out_dtype)

    return _pcall(body, name=name, out_shape=jax.ShapeDtypeStruct((m, n), out_dtype), grid=(m // tm, n // tn),
                  in_specs=in_specs, out_specs=pl.BlockSpec((tm, tn), lambda i, j: (i, j)),
                  dims=("parallel", "parallel"))(*operands)


def _rms_fwd(h, gain, name, f=None, scale=0.5):
    t, d = h.shape
    tt = _tile(t, 512, 16)
    row = pl.BlockSpec((tt, d), lambda i: (i, 0))
    vec = pl.BlockSpec((1, d), lambda i: (0, 0))

    def norm(hv, g_ref, n_ref):
        r = lax.rsqrt(jnp.mean(hv * hv, axis=-1, keepdims=True) + EPS)
        n_ref[...] = (hv * r * g_ref[...]).astype(BF16)

    if f is None:

        def body(h_ref, g_ref, n_ref):
            norm(h_ref[...], g_ref, n_ref)

        return _pcall(body, name=name, out_shape=jax.ShapeDtypeStruct((t, d), BF16), grid=(t // tt,),
                      in_specs=[row, vec], out_specs=row, dims=("parallel",))(h, gain)

    def body(h_ref, f_ref, g_ref, hn_ref, n_ref):
        hv = h_ref[...] + scale * f_ref[...]
        hn_ref[...] = hv
        norm(hv, g_ref, n_ref)

    return _pcall(body, name=name,
                  out_shape=(jax.ShapeDtypeStruct((t, d), F32), jax.ShapeDtypeStruct((t, d), BF16)),
                  grid=(t // tt,), in_specs=[row, row, vec], out_specs=(row, row), dims=("parallel",))(h, f, gain)


def _rms_bwd(dn, h, gain, dh_in, name, out_scale):
    t, d = h.shape
    tt = _tile(t, 512, 16)
    row = pl.BlockSpec((tt, d), lambda i: (i, 0))
    vec = pl.BlockSpec((1, d), lambda i: (0, 0))

    def body(dn_ref, h_ref, g_ref, dhin_ref, dh_ref, dhb_ref, dg_ref):
        hv = h_ref[...]
        dnv = dn_ref[...].astype(F32)
        r = lax.rsqrt(jnp.mean(hv * hv, axis=-1, keepdims=True) + EPS)
        dng = dnv * g_ref[...]
        dh = dhin_ref[...] + r * dng - hv * (r * r * r) * jnp.mean(dng * hv, axis=-1, keepdims=True)
        dh_ref[...] = dh
        dhb_ref[...] = (out_scale * dh).astype(BF16)
        part = jnp.sum(dnv * hv * r, axis=0, keepdims=True)

        @pl.when(pl.program_id(0) == 0)
        def _():
            dg_ref[...] = part

        @pl.when(pl.program_id(0) > 0)
        def _():
            dg_ref[...] += part

    return _pcall(body, name=name,
                  out_shape=(jax.ShapeDtypeStruct((t, d), F32), jax.ShapeDtypeStruct((t, d), BF16),
                             jax.ShapeDtypeStruct((1, d), F32)),
                  grid=(t // tt,), in_specs=[row, row, vec, row], out_specs=(row, row, vec),
                  dims=("arbitrary",))(dn, h, gain, dh_in)


def _swiglu_fwd(g, u, name):
    t, f = g.shape
    tt = _tile(t, 256, 16)
    row = pl.BlockSpec((tt, f), lambda i: (i, 0))

    def body(g_ref, u_ref, a_ref):
        gv = g_ref[...]
        a_ref[...] = (gv * _sigmoid(gv) * u_ref[...]).astype(BF16)

    return _pcall(body, name=name, out_shape=jax.ShapeDtypeStruct((t, f), BF16), grid=(t // tt,),
                  in_specs=[row, row], out_specs=row, dims=("parallel",))(g, u)


def _swiglu_bwd(da, g, u, name):
    t, f = g.shape
    tt = _tile(t, 256, 16)
    row = pl.BlockSpec((tt, f), lambda i: (i, 0))

    def body(da_ref, g_ref, u_ref, dg_ref, du_ref):
        gv = g_ref[...]
        dav = da_ref[...].astype(F32)
        sg = _sigmoid(gv)
        dg_ref[...] = (dav * u_ref[...] * (sg * (1.0 + gv * (1.0 - sg)))).astype(BF16)
        du_ref[...] = (dav * (gv * sg)).astype(BF16)

    return _pcall(body, name=name, out_shape=(jax.ShapeDtypeStruct((t, f), BF16),) * 2, grid=(t // tt,),
                  in_specs=[row, row, row], out_specs=(row, row), dims=("parallel",))(da, g, u)


def _rope_tables(pos_col, inv2, sign2):
    t = pos_col.shape[0]

    def body(p_ref, inv_ref, sg_ref, c_ref, s_ref):
        ang = p_ref[...].astype(F32) * inv_ref[...]
        c_ref[...] = jnp.cos(ang)
        s_ref[...] = jnp.sin(ang) * sg_ref[...]

    full = lambda shape: pl.BlockSpec(shape, lambda i: (0, 0))
    return _pcall(body, name="rope_tables", out_shape=(jax.ShapeDtypeStruct((t, RET_DIM), F32),) * 2, grid=(1,),
                  in_specs=[full((t, 1)), full((1, RET_DIM)), full((1, RET_DIM))],
                  out_specs=(full((t, RET_DIM)),) * 2, dims=("arbitrary",))(pos_col, inv2, sign2)


def _rot(x, c, s):
    return x * c + pltpu.roll(x, RET_DIM // 2, 1) * s


def _rot_t(g, c, s):
    return g * c + pltpu.roll(g * s, RET_DIM // 2, 1)


def _ret_consts():
    hh = np.arange(RET_HEADS, dtype=np.float32)
    log_gamma = np.log1p(-np.exp2(-5.0 - hh)).astype(np.float32)
    idx = np.arange(CHUNK, dtype=np.float32)
    diff = idx[:, None] - idx[None, :]
    dmask = np.where(diff >= 0, np.exp(log_gamma[:, None, None] * np.maximum(diff, 0.0)), 0.0).astype(np.float32)
    kdec = np.exp(log_gamma[:, None] * (CHUNK - 1 - idx)).astype(np.float32)
    qdec = np.exp(log_gamma[:, None] * (idx + 1.0)).astype(np.float32)
    cdec = np.exp(log_gamma * CHUNK).astype(np.float32)
    bc = lambda v: np.ascontiguousarray(np.broadcast_to(v[:, :, None], (RET_HEADS, CHUNK, RET_DIM)))
    cd = np.ascontiguousarray(np.broadcast_to(cdec[:, None, None], (RET_HEADS, 8, RET_DIM)))
    return jnp.asarray(dmask), jnp.asarray(bc(qdec)), jnp.asarray(bc(kdec)), jnp.asarray(cd)


def _ret_fwd(pm, cos2, sin2, consts):
    t = pm.shape[0]
    n_chunks = t // CHUNK
    dmask, qdec, kdec, cd = consts
    scale = RET_DIM ** -0.5

    def col(c0):
        return pl.BlockSpec((CHUNK, RET_DIM), lambda h, n: (n, c0 + h))

    tab = pl.BlockSpec((CHUNK, RET_DIM), lambda h, n: (n, 0))
    head3 = lambda r: pl.BlockSpec((None, r, RET_DIM), lambda h, n: (h, 0, 0))

    def body(q_ref, k_ref, v_ref, g_ref, c_ref, s_ref, dm_ref, qd_ref, kd_ref, cd_ref, y_ref, raw_ref, st_ref, s_acc):
        @pl.when(pl.program_id(1) == 0)
        def _():
            s_acc[...] = jnp.zeros_like(s_acc)

        c, s = c_ref[...], s_ref[...]
        q = _rot(q_ref[...], c, s)
        k = _rot(k_ref[...], c, s) * scale
        vb = v_ref[...].astype(BF16)
        g = g_ref[...]
        s_in = s_acc[...]
        st_ref[...] = s_in
        a = _dot(q.astype(BF16), k.astype(BF16), 1, 1) * dm_ref[...]
        y = _dot(a.astype(BF16), vb, 1, 0) + _dot((q * qd_ref[...]).astype(BF16), s_in.astype(BF16), 1, 0)
        s_acc[...] = cd_ref[0:1, :] * s_in + _dot((k * kd_ref[...]).astype(BF16), vb, 0, 0)
        raw_ref[...] = y
        mu = jnp.mean(y, axis=-1, keepdims=True)
        yc = y - mu
        rs = lax.rsqrt(jnp.mean(yc * yc, axis=-1, keepdims=True) + EPS)
        y_ref[...] = (yc * rs * (g * _sigmoid(g))).astype(BF16)

    out_blk = pl.BlockSpec((CHUNK, RET_DIM), lambda h, n: (n, h))
    return _pcall(
        body, name="retention_fwd",
        out_shape=(jax.ShapeDtypeStruct((t, RET_WIDTH), BF16), jax.ShapeDtypeStruct((t, RET_WIDTH), F32),
                   jax.ShapeDtypeStruct((RET_HEADS, n_chunks, RET_DIM, RET_DIM), F32)),
        grid=(RET_HEADS, n_chunks),
        in_specs=[col(TILE_RQ), col(TILE_RK), col(TILE_RV), col(TILE_RG), tab, tab,
                  pl.BlockSpec((None, CHUNK, CHUNK), lambda h, n: (h, 0, 0)), head3(CHUNK), head3(CHUNK), head3(8)],
        out_specs=(out_blk, out_blk, pl.BlockSpec((None, None, RET_DIM, RET_DIM), lambda h, n: (h, n, 0, 0))),
        scratch_shapes=[pltpu.VMEM((RET_DIM, RET_DIM), F32)],
        dims=("parallel", "arbitrary"),
    )(pm, pm, pm, pm, cos2, sin2, dmask, qdec, kdec, cd)


def _ret_bwd(dy, pm, cos2, sin2, raw, states, consts):
    t = pm.shape[0]
    n_chunks = t // CHUNK
    dmask, qdec, kdec, cd = consts
    scale = RET_DIM ** -0.5
    rev = lambda n: n_chunks - 1 - n

    def col(c0):
        return pl.BlockSpec((CHUNK, RET_DIM), lambda h, n: (rev(n), c0 + h))

    tab = pl.BlockSpec((CHUNK, RET_DIM), lambda h, n: (rev(n), 0))
    blk = pl.BlockSpec((CHUNK, RET_DIM), lambda h, n: (rev(n), h))
    head3 = lambda r: pl.BlockSpec((None, r, RET_DIM), lambda h, n: (h, 0, 0))

    def body(dy_ref, q_ref, k_ref, v_ref, g_ref, c_ref, s_ref, raw_ref, st_ref, dm_ref, qd_ref, kd_ref, cd_ref,
             dq_ref, dk_ref, dv_ref, dg_ref, ds_acc):
        @pl.when(pl.program_id(1) == 0)
        def _():
            ds_acc[...] = jnp.zeros_like(ds_acc)

        c, s = c_ref[...], s_ref[...]
        q = _rot(q_ref[...], c, s)
        k = _rot(k_ref[...], c, s) * scale
        qb, kb, vb = q.astype(BF16), k.astype(BF16), v_ref[...].astype(BF16)
        g = g_ref[...]
        dm, qd, kd = dm_ref[...], qd_ref[...], kd_ref[...]
        y = raw_ref[...]
        mu = jnp.mean(y, axis=-1, keepdims=True)
        yc = y - mu
        rs = lax.rsqrt(jnp.mean(yc * yc, axis=-1, keepdims=True) + EPS)
        yn = yc * rs
        sg = _sigmoid(g)
        dyo = dy_ref[...]
        dg_ref[...] = (dyo * yn * (sg * (1.0 + g * (1.0 - sg)))).astype(BF16)
        dyn = dyo * (g * sg)
        dyr = rs * (dyn - jnp.mean(dyn, axis=-1, keepdims=True) - yn * jnp.mean(dyn * yn, axis=-1, keepdims=True))
        dyb = dyr.astype(BF16)
        s_in = st_ref[...].astype(BF16)
        ds_out = ds_acc[...]
        dsb = ds_out.astype(BF16)
        a = _dot(qb, kb, 1, 1) * dm
        da = (_dot(dyb, vb, 1, 1) * dm).astype(BF16)
        kdb = (k * kd).astype(BF16)
        qdb = (q * qd).astype(BF16)
        dv_ref[...] = (_dot(a.astype(BF16), dyb, 0, 0) + _dot(kdb, dsb, 1, 0)).astype(BF16)
        dqh = _dot(da, kb, 1, 0) + _dot(dyb, s_in, 1, 1) * qd
        dkh = _dot(da, qb, 0, 0) + _dot(vb, dsb, 1, 1) * kd
        ds_acc[...] = cd_ref[0:1, :] * ds_out + _dot(qdb, dyb, 0, 0)
        dq_ref[...] = _rot_t(dqh, c, s).astype(BF16)
        dk_ref[...] = (_rot_t(dkh, c, s) * scale).astype(BF16)

    return _pcall(
        body, name="retention_bwd",
        out_shape=(jax.ShapeDtypeStruct((t, RET_WIDTH), BF16),) * 4,
        grid=(RET_HEADS, n_chunks),
        in_specs=[blk, col(TILE_RQ), col(TILE_RK), col(TILE_RV), col(TILE_RG), tab, tab, blk,
                  pl.BlockSpec((None, None, RET_DIM, RET_DIM), lambda h, n: (h, rev(n), 0, 0)),
                  pl.BlockSpec((None, CHUNK, CHUNK), lambda h, n: (h, 0, 0)), head3(CHUNK), head3(CHUNK), head3(8)],
        out_specs=(blk,) * 4,
        scratch_shapes=[pltpu.VMEM((RET_DIM, RET_DIM), F32)],
        dims=("parallel", "arbitrary"),
    )(dy, pm, pm, pm, pm, cos2, sin2, raw, states, dmask, qdec, kdec, cd)


FOX_C_LANE = FOX_DIM
FOX_NEGC_LANE = FOX_DIM + 3
FOX_LSE_LANE = FOX_DIM + 6
FOX_L_LANE = FOX_C_LANE
FOX_ROWSUM_LANE = FOX_C_LANE
FOX_COLSUM_LANE = FOX_NEGC_LANE


def _split3(x):
    hi = x.astype(BF16)
    r1 = x - hi.astype(F32)
    mid = r1.astype(BF16)
    lo = (r1 - mid.astype(F32)).astype(BF16)
    return hi, mid, lo


def _tri_dot(tri, x):
    hi, mid, lo = _split3(x)
    return _dot(tri, lo, 1, 0) + _dot(tri, mid, 1, 0) + _dot(tri, hi, 1, 0)


def _log_sigmoid(z):
    return jnp.minimum(z, 0.0) - jnp.log1p(jnp.exp(-jnp.abs(z)))


def _fox_consts():
    place = np.zeros((2, 3, LANES, FOX_TILES), np.float32)
    ones = np.zeros((3, 1, FOX_TILES), np.float32)
    for h in range(FOX_HEADS):
        for part in range(3):
            place[0, part, h, LANES * h + FOX_C_LANE + part] = 1.0
            place[1, part, h, LANES * h + FOX_NEGC_LANE + part] = -1.0
            ones[0, 0, LANES * h + FOX_NEGC_LANE + part] = 1.0
            ones[1, 0, LANES * h + FOX_C_LANE + part] = 1.0
            ones[1, 0, LANES * h + FOX_LSE_LANE + part] = 1.0
            ones[2, 0, LANES * h + FOX_C_LANE + part] = 1.0
    return jnp.asarray(place, BF16), jnp.asarray(ones, F32)


def _fox_prep(pm, bpad):
    t = pm.shape[0]
    tt = _tile(t, 512, LANES)
    place, ones = _fox_consts()
    wide = lambda c0: pl.BlockSpec((tt, FOX_TILES), lambda i: (i, c0 // FOX_HEADS))
    const = lambda a: pl.BlockSpec(a.shape, lambda i: (0,) * a.ndim)

    def body(q_ref, k_ref, v_ref, ff_ref, b_ref, pl_ref, on_ref, qa_ref, ka_ref, va_ref, carry_s):
        @pl.when(pl.program_id(0) == 0)
        def _():
            carry_s[...] = jnp.zeros_like(carry_s)

        r = lax.broadcasted_iota(jnp.int32, (LANES, LANES), 0)
        cc = lax.broadcasted_iota(jnp.int32, (LANES, LANES), 1)
        tri = jnp.where(cc <= r, 1.0, 0.0).astype(BF16)
        bias = b_ref[...]
        for sub in range(tt // LANES):
            rows = pl.ds(sub * LANES, LANES)
            cs = _tri_dot(tri, _log_sigmoid(ff_ref[rows, :] + bias)) + carry_s[...]
            carry_s[...] = cs[LANES - 1:LANES, :]
            parts = _split3(cs)
            eq = sum(_dot(part, pl_ref[0, i], 1, 0) for i, part in enumerate(parts))
            ek = sum(_dot(part, pl_ref[1, i], 1, 0) for i, part in enumerate(parts))
            qa_ref[rows, :] = (q_ref[rows, :] * FOX_DIM ** -0.5 + eq + on_ref[0]).astype(BF16)
            ka_ref[rows, :] = (k_ref[rows, :] + ek + on_ref[1]).astype(BF16)
            va_ref[rows, :] = (v_ref[rows, :] + on_ref[2]).astype(BF16)

    out = pl.BlockSpec((tt, FOX_TILES), lambda i: (i, 0))
    return _pcall(body, name="fox_prep", out_shape=(jax.ShapeDtypeStruct((t, FOX_TILES), BF16),) * 3, grid=(t // tt,),
                  in_specs=[wide(TILE_FQ), wide(TILE_FK), wide(TILE_FV), pl.BlockSpec((tt, LANES), lambda i: (i, TILE_FF)),
                            pl.BlockSpec((1, LANES), lambda i: (0, 0)), const(place), const(ones)],
                  out_specs=(out,) * 3, scratch_shapes=[pltpu.VMEM((1, LANES), F32)],
                  dims=("arbitrary",))(pm, pm, pm, pm, bpad, place, ones)


def _fox_post(dc, pm, bpad):
    t = pm.shape[0]
    nb = t // LANES

    def body(dc_ref, ff_ref, b_ref, d_ref, db_ref):
        r = lax.broadcasted_iota(jnp.int32, (LANES, LANES), 0)
        cc = lax.broadcasted_iota(jnp.int32, (LANES, LANES), 1)
        tri = jnp.where(cc >= r, 1.0, 0.0).astype(BF16)
        bias = b_ref[...]

        def step(i, carry):
            tail, acc = carry
            rows = pl.ds(pl.multiple_of((nb - 1 - i) * LANES, LANES), LANES)
            cs = _tri_dot(tri, dc_ref[rows, :]) + tail
            dff = cs * _sigmoid(-(ff_ref[rows, :] + bias))
            d_ref[rows, :] = dff.astype(BF16)
            return cs[0:1, :], acc + jnp.sum(dff, axis=0, keepdims=True)

        zero = jnp.zeros((1, LANES), F32)
        _, acc = lax.fori_loop(0, nb, step, (zero, zero))
        db_ref[...] = acc

    return _pcall(body, name="fox_forget_bwd",
                  out_shape=(jax.ShapeDtypeStruct((t, LANES), BF16), jax.ShapeDtypeStruct((1, LANES), F32)), grid=(1,),
                  in_specs=[pl.BlockSpec((t, LANES), lambda i: (0, 0)), pl.BlockSpec((t, LANES), lambda i: (0, TILE_FF)),
                            pl.BlockSpec((1, LANES), lambda i: (0, 0))],
                  out_specs=(pl.BlockSpec((t, LANES), lambda i: (0, 0)), pl.BlockSpec((1, LANES), lambda i: (0, 0))),
                  dims=("arbitrary",))(dc, pm, bpad)


def _tri_tables(nb, q_major):
    pairs = [(i, j) for i in range(nb) for j in range(i + 1)] if q_major else \
            [(i, j) for j in range(nb) for i in range(j, nb)]
    return jnp.asarray([a for a, _ in pairs], jnp.int32), jnp.asarray([b for _, b in pairs], jnp.int32)


def _causal(s):
    n = s.shape[0]
    row = lax.broadcasted_iota(jnp.int32, (n, n), 0)
    col = lax.broadcasted_iota(jnp.int32, (n, n), 1)
    return jnp.where(col <= row, s, NEG)


def _lane_col(x, lane):
    sel = lax.broadcasted_iota(jnp.int32, x.shape, 1) == lane
    return jnp.sum(jnp.where(sel, x, 0.0), axis=1, keepdims=True)


def _fox_fwd(qa, ka, va, blk):
    t = qa.shape[0]
    nb = t // blk
    qi, kj = _tri_tables(nb, True)
    q_spec = pl.BlockSpec((blk, LANES), lambda h, s, qi_r, kj_r: (qi_r[s], h))
    k_spec = pl.BlockSpec((blk, LANES), lambda h, s, qi_r, kj_r: (kj_r[s], h))

    def body(qi_r, kj_r, q_ref, k_ref, v_ref, o_ref, ob_ref, qb_ref, m_s, acc_s):
        s_id = pl.program_id(1)
        i, j = qi_r[s_id], kj_r[s_id]

        @pl.when(j == 0)
        def _():
            m_s[...] = jnp.full_like(m_s, NEG)
            acc_s[...] = jnp.zeros_like(acc_s)

        def tile(diagonal):
            s = _dot(q_ref[...], k_ref[...], 1, 1)
            if diagonal:
                s = _causal(s)
            m_old = m_s[...]
            m_new = jnp.maximum(m_old, jnp.max(s, axis=1, keepdims=True))
            p = jnp.exp(s - jnp.tile(m_new, (1, blk // LANES)))
            acc_s[...] = jnp.exp(m_old - m_new) * acc_s[...] + _dot(p.astype(BF16), v_ref[...], 1, 0)
            m_s[...] = m_new

        @pl.when(j < i)
        def _():
            tile(False)

        @pl.when(j == i)
        def _():
            tile(True)
            acc = acc_s[...]
            l = _lane_col(acc, FOX_L_LANE)
            o = acc / l
            o_ref[...] = o
            ob_ref[...] = o.astype(BF16)
            hi, mid, lo = _split3(-(m_s[:, 0:1] + jnp.log(l)))
            lane = lax.broadcasted_iota(jnp.int32, acc.shape, 1)
            qb_ref[...] = jnp.where(lane == FOX_LSE_LANE, hi,
                                    jnp.where(lane == FOX_LSE_LANE + 1, mid,
                                              jnp.where(lane == FOX_LSE_LANE + 2, lo, q_ref[...])))

    wide = (t, FOX_TILES)
    return pl.pallas_call(
        body, name="fox_fwd",
        out_shape=(jax.ShapeDtypeStruct(wide, F32), jax.ShapeDtypeStruct(wide, BF16), jax.ShapeDtypeStruct(wide, BF16)),
        grid_spec=pltpu.PrefetchScalarGridSpec(
            num_scalar_prefetch=2, grid=(FOX_HEADS, qi.shape[0]), in_specs=[q_spec, k_spec, k_spec],
            out_specs=(q_spec,) * 3,
            scratch_shapes=[pltpu.VMEM((blk, LANES), F32), pltpu.VMEM((blk, LANES), F32)]),
        compiler_params=_params(("parallel", "arbitrary")),
    )(qi, kj, qa, ka, va)


def _fox_bwd(qa, ka, va, do, o, blk):
    t = qa.shape[0]
    nb = t // blk
    qi, kj = _tri_tables(nb, False)
    q_spec = pl.BlockSpec((blk, LANES), lambda h, s, qi_r, kj_r: (qi_r[s], h))
    k_spec = pl.BlockSpec((blk, LANES), lambda h, s, qi_r, kj_r: (kj_r[s], h))
    head_spec = pl.BlockSpec((t, LANES), lambda h, s, qi_r, kj_r: (0, h))
    head_col = pl.BlockSpec((None, t, 1), lambda h, s, qi_r, kj_r: (h, 0, 0))
    k_col = pl.BlockSpec((None, blk, 1), lambda h, s, qi_r, kj_r: (h, kj_r[s], 0))
    first_spec = pl.BlockSpec((blk, LANES), lambda h, s, qi_r, kj_r: (jnp.where(kj_r[s] == 0, qi_r[s], nb - 1), h))
    n_steps = int(qi.shape[0])

    def body(qi_r, kj_r, q_ref, k_ref, v_ref, do_ref, o_ref, dq_ref, dk_ref, dv_ref, rs_ref, cs_ref,
             doa_s, dq_s, dk_s, dv_s):
        s_id = pl.program_id(1)
        i, j = qi_r[s_id], kj_r[s_id]
        rows = pl.ds(pl.multiple_of(i * blk, blk), blk)

        @pl.when(j == 0)
        def _():
            dof = do_ref[...]
            hi, mid, lo = _split3(-jnp.sum(dof * o_ref[...], axis=1, keepdims=True))
            lane = lax.broadcasted_iota(jnp.int32, dof.shape, 1)
            doa = jnp.where(lane == FOX_C_LANE, hi.astype(F32),
                            jnp.where(lane == FOX_C_LANE + 1, mid.astype(F32),
                                      jnp.where(lane == FOX_C_LANE + 2, lo.astype(F32), dof)))
            doa_s[rows, :] = doa.astype(BF16)
            dq_s[rows, :] = jnp.zeros((blk, LANES), F32)

        @pl.when(i == j)
        def _():
            dk_s[...] = jnp.zeros_like(dk_s)
            dv_s[...] = jnp.zeros_like(dv_s)

        def tile(diagonal):
            q, k = q_ref[...], k_ref[...]
            s = _dot(q, k, 1, 1)
            if diagonal:
                s = _causal(s)
            p = jnp.exp(s)
            doa = doa_s[rows, :]
            ds = (p * _dot(doa, v_ref[...], 1, 1)).astype(BF16)
            dv_s[...] += _dot(p.astype(BF16), doa, 0, 0)
            dk_s[...] += _dot(ds, q, 0, 0)
            dq_s[rows, :] += _dot(ds, k, 1, 0)

        @pl.when(i > j)
        def _():
            tile(False)

        @pl.when(i == j)
        def _():
            tile(True)

        @pl.when(i == nb - 1)
        def _():
            dk = dk_s[...]
            dk_ref[...] = dk.astype(BF16)
            dv_ref[...] = dv_s[...].astype(BF16)
            cs_ref[...] = -_lane_col(dk, FOX_COLSUM_LANE)

        @pl.when(s_id == n_steps - 1)
        def _():
            dq = dq_s[...]
            dq_ref[...] = (dq * FOX_DIM ** -0.5).astype(BF16)
            rs_ref[...] = _lane_col(dq, FOX_ROWSUM_LANE)

    wide = jax.ShapeDtypeStruct((t, FOX_TILES), BF16)
    cols = jax.ShapeDtypeStruct((FOX_HEADS, t, 1), F32)
    return pl.pallas_call(
        body, name="fox_bwd", out_shape=(wide, wide, wide, cols, cols),
        grid_spec=pltpu.PrefetchScalarGridSpec(
            num_scalar_prefetch=2, grid=(FOX_HEADS, n_steps),
            in_specs=[q_spec, k_spec, k_spec, first_spec, first_spec],
            out_specs=(head_spec, k_spec, k_spec, head_col, k_col),
            scratch_shapes=[pltpu.VMEM((t, LANES), BF16), pltpu.VMEM((t, LANES), F32), pltpu.VMEM((blk, LANES), F32),
                            pltpu.VMEM((blk, LANES), F32)]),
        compiler_params=_params(("parallel", "arbitrary")),
    )(qi, kj, qa, ka, va, do, o)


def _merge_fwd(gm, bm, za, zb):
    t, d = za.shape
    tt = _tile(t, 256, 16)
    row = pl.BlockSpec((tt, d), lambda i: (i, 0))

    def body(gm_ref, b_ref, za_ref, zb_ref, o_ref):
        ga = _sigmoid(gm_ref[:, :d] + b_ref[:, :d])
        gb = _sigmoid(gm_ref[:, d:] + b_ref[:, d:])
        o_ref[...] = (ga * za_ref[...] + gb * zb_ref[...]).astype(BF16)

    return _pcall(body, name="merge_fwd", out_shape=jax.ShapeDtypeStruct((t, d), BF16), grid=(t // tt,),
                  in_specs=[pl.BlockSpec((tt, 2 * d), lambda i: (i, 0)), pl.BlockSpec((1, 2 * d), lambda i: (0, 0)), row, row],
                  out_specs=row, dims=("parallel",))(gm, bm, za, zb)


def _merge_bwd(dmix, gm, bm, za, zb):
    t, d = za.shape
    tt = _tile(t, 256, 16)
    row = pl.BlockSpec((tt, d), lambda i: (i, 0))
    wide = pl.BlockSpec((tt, 2 * d), lambda i: (i, 0))
    vec = pl.BlockSpec((1, 2 * d), lambda i: (0, 0))

    def body(dm_ref, gm_ref, b_ref, za_ref, zb_ref, dza_ref, dzb_ref, dgm_ref, db_ref):
        dm = dm_ref[...]
        ga = _sigmoid(gm_ref[:, :d] + b_ref[:, :d])
        gb = _sigmoid(gm_ref[:, d:] + b_ref[:, d:])
        dza_ref[...] = (dm * ga).astype(BF16)
        dzb_ref[...] = (dm * gb).astype(BF16)
        dla = dm * za_ref[...] * ga * (1.0 - ga)
        dlb = dm * zb_ref[...] * gb * (1.0 - gb)
        dgm_ref[:, :d] = dla.astype(BF16)
        dgm_ref[:, d:] = dlb.astype(BF16)
        pa = jnp.sum(dla, axis=0, keepdims=True)
        pb = jnp.sum(dlb, axis=0, keepdims=True)

        @pl.when(pl.program_id(0) == 0)
        def _():
            db_ref[:, :d] = pa
            db_ref[:, d:] = pb

        @pl.when(pl.program_id(0) > 0)
        def _():
            db_ref[:, :d] += pa
            db_ref[:, d:] += pb

    return _pcall(body, name="merge_bwd",
                  out_shape=(jax.ShapeDtypeStruct((t, d), BF16), jax.ShapeDtypeStruct((t, d), BF16),
                             jax.ShapeDtypeStruct((t, 2 * d), BF16), jax.ShapeDtypeStruct((1, 2 * d), F32)),
                  grid=(t // tt,), in_specs=[row, wide, vec, row, row], out_specs=(row, row, wide, vec),
                  dims=("arbitrary",))(dmix, gm, bm, za, zb)


def _ple_final(h3, pgl, pe, gain, target):
    t, d = h3.shape
    tt = _tile(t, 256, 16)
    row = pl.BlockSpec((tt, d), lambda i: (i, 0))
    vec = pl.BlockSpec((1, d), lambda i: (0, 0))
    lvec = pl.BlockSpec((1, LANES), lambda i: (0, 0))

    def body(h_ref, pgl_ref, pe_ref, g_ref, t_ref, dh_ref, dsg_ref, dpe_ref, loss_ref, dg_ref):
        pg = _sigmoid(pgl_ref[...])
        pe_v = pe_ref[...]
        h4 = h_ref[...] + pg * pe_v
        r = lax.rsqrt(jnp.mean(h4 * h4, axis=-1, keepdims=True) + EPS)
        gv = g_ref[...]
        err = h4 * r * gv - t_ref[...]
        part_loss = 0.5 * jnp.sum(jnp.mean(err * err, axis=-1, keepdims=True), axis=0, keepdims=True)
        dy = err * (1.0 / d)
        part_g = jnp.sum(dy * h4 * r, axis=0, keepdims=True)
        dyg = dy * gv
        dh = r * dyg - h4 * (r * r * r) * jnp.mean(dyg * h4, axis=-1, keepdims=True)
        dh_ref[...] = dh
        dsg_ref[...] = (dh * pe_v * pg * (1.0 - pg)).astype(BF16)
        dpe_ref[...] = (dh * pg).astype(BF16)

        @pl.when(pl.program_id(0) == 0)
        def _():
            loss_ref[...] = jnp.broadcast_to(part_loss, (1, LANES))
            dg_ref[...] = part_g

        @pl.when(pl.program_id(0) > 0)
        def _():
            loss_ref[...] += jnp.broadcast_to(part_loss, (1, LANES))
            dg_ref[...] += part_g

    return _pcall(body, name="ple_final",
                  out_shape=(jax.ShapeDtypeStruct((t, d), F32), jax.ShapeDtypeStruct((t, d), BF16),
                             jax.ShapeDtypeStruct((t, d), BF16), jax.ShapeDtypeStruct((1, LANES), F32),
                             jax.ShapeDtypeStruct((1, d), F32)),
                  grid=(t // tt,), in_specs=[row, row, row, vec, row], out_specs=(row, row, row, lvec, vec),
                  dims=("arbitrary",))(h3, pgl, pe, gain, target)


def _adamw_math(w, g, m, v):
    m = ADAM_B1 * m + (1.0 - ADAM_B1) * g
    v = ADAM_B2 * v + (1.0 - ADAM_B2) * (g * g)
    m_hat = m / (1.0 - ADAM_B1 ** ADAM_STEP)
    v_hat = v / (1.0 - ADAM_B2 ** ADAM_STEP)
    delta = -ADAM_LR * (m_hat / (jnp.sqrt(v_hat) + ADAM_EPS) + ADAM_WD * w)
    return delta, m, v


def _adamw(parts, w, m, v, name):
    n, r, c = parts.shape
    tr = _tile(r, 256, 16)
    row = pl.BlockSpec((tr, c), lambda i: (i, 0))

    def body(p_ref, w_ref, m_ref, v_ref, g_ref, d_ref, mo_ref, vo_ref):
        g = p_ref[0].astype(F32)
        for s in range(1, n):
            g = g + p_ref[s].astype(F32)
        g_ref[...] = g
        d_ref[...], mo_ref[...], vo_ref[...] = _adamw_math(w_ref[...], g, m_ref[...], v_ref[...])

    return _pcall(body, name=name, out_shape=(jax.ShapeDtypeStruct((r, c), F32),) * 4, grid=(r // tr,),
                  in_specs=[pl.BlockSpec((n, tr, c), lambda i: (0, i, 0)), row, row, row], out_specs=(row,) * 4,
                  dims=("parallel",))(parts, w, m, v)


ANY = pl.BlockSpec(memory_space=pl.ANY)


def _all_gather(shards):
    n = len(shards)

    def body(*refs):
        x_refs, out_refs = refs[:n], refs[n:2 * n]
        send_sems, recv_sems, local_sems = refs[2 * n:]
        x, y, cc = lax.axis_index("x"), lax.axis_index("y"), lax.axis_index("c")
        me, sibling = (x, y, cc), (x, y, 1 - cc)
        chips = [(1 - x, y), (x, 1 - y), (1 - x, 1 - y)]

        def slot(a, px, py, pc):
            return out_refs[a].at[4 * px + 2 * py + pc]

        def copy(a, k, block, to, src=None):
            return pltpu.make_async_remote_copy(
                src_ref=slot(a, *block) if src is None else src, dst_ref=slot(a, *block),
                send_sem=send_sems.at[7 * a + k], recv_sem=recv_sems.at[7 * a + k], device_id=to, device_id_type=MESH)

        local, sent = [], []
        for a in range(n):
            local.append(pltpu.make_async_copy(x_refs[a], slot(a, *me), local_sems.at[a]))
            sent.append(copy(a, 0, me, sibling, src=x_refs[a]))
            sent += [copy(a, 1 + j, me, (*chip, cc), src=x_refs[a]) for j, chip in enumerate(chips)]
        for cp in local + sent:
            cp.start()
        for j, chip in enumerate(chips):
            for a in range(n):
                copy(a, 1 + j, (*chip, cc), me).wait_recv()
                sent.append(copy(a, 4 + j, (*chip, cc), sibling))
                sent[-1].start()
        for a in range(n):
            copy(a, 0, sibling, me).wait_recv()
            for j, chip in enumerate(chips):
                copy(a, 4 + j, (*chip, 1 - cc), me).wait_recv()
        for cp in sent:
            cp.wait_send()
        for cp in local:
            cp.wait()

    return pl.pallas_call(
        body, name="weights_all_gather",
        out_shape=tuple(jax.ShapeDtypeStruct((N_DEV,) + s.shape, s.dtype) for s in shards),
        in_specs=[ANY] * n, out_specs=(ANY,) * n,
        scratch_shapes=[pltpu.SemaphoreType.DMA((7 * n,)), pltpu.SemaphoreType.DMA((7 * n,)),
                        pltpu.SemaphoreType.DMA((n,))],
    )(*shards)


def _reduce_scatter_exchange(blocks):
    n = len(blocks)

    def body(*refs):
        g_refs, recv_refs = refs[:n], refs[n:2 * n]
        send_sems, recv_sems, local_sems = refs[2 * n:]
        x, y, cc = lax.axis_index("x"), lax.axis_index("y"), lax.axis_index("c")
        me = 4 * x + 2 * y + cc
        local, sent, landing = [], [], []
        for a in range(n):
            local.append(pltpu.make_async_copy(g_refs[a].at[me], recv_refs[a].at[me], local_sems.at[a]))
        for k in range(1, N_DEV):
            px, py, pc = x ^ (k >> 2), y ^ ((k >> 1) & 1), cc ^ (k & 1)
            peer = 4 * px + 2 * py + pc
            for a in range(n):
                sems = dict(send_sem=send_sems.at[7 * a + k - 1], recv_sem=recv_sems.at[7 * a + k - 1],
                            device_id=(px, py, pc), device_id_type=MESH)
                sent.append(pltpu.make_async_remote_copy(src_ref=g_refs[a].at[peer], dst_ref=recv_refs[a].at[me], **sems))
                landing.append(pltpu.make_async_remote_copy(src_ref=g_refs[a].at[me], dst_ref=recv_refs[a].at[peer], **sems))
        for cp in local + sent:
            cp.start()
        for cp in landing:
            cp.wait_recv()
        for cp in sent:
            cp.wait_send()
        for cp in local:
            cp.wait()

    return pl.pallas_call(
        body, name="grads_reduce_scatter_exchange",
        out_shape=tuple(jax.ShapeDtypeStruct(b.shape, b.dtype) for b in blocks),
        in_specs=[ANY] * n, out_specs=(ANY,) * n,
        scratch_shapes=[pltpu.SemaphoreType.DMA((7 * n,)), pltpu.SemaphoreType.DMA((7 * n,)),
                        pltpu.SemaphoreType.DMA((n,))],
    )(*blocks)


HBM = pl.BlockSpec(memory_space=pltpu.HBM)
SEM = pl.BlockSpec(memory_space=pltpu.SEMAPHORE)
DATAFLOW = pltpu.SideEffectType.DATAFLOW_SIDE_EFFECTING


def _peers():
    x, y, cc = lax.axis_index("x"), lax.axis_index("y"), lax.axis_index("c")
    out = []
    for k in range(1, N_DEV):
        px, py, pc = x ^ (k >> 2), y ^ ((k >> 1) & 1), cc ^ (k & 1)
        out.append((k, (px, py, pc), 4 * px + 2 * py + pc))
    return 4 * x + 2 * y + cc, out


def _scatter_start(blocks, after, name, gather=False):
    n = len(blocks)
    lands = [lax.empty((N_DEV,) + b.shape if gather else b.shape, b.dtype) for b in blocks]

    def body(*refs):
        g_refs, land_refs = refs[:n], refs[n:2 * n]
        send_sems, recv_sems, token = refs[2 * n + 1], refs[2 * n + 2], refs[-1]
        me, peers = _peers()
        for k, peer, slot in peers:
            for a in range(n):
                pltpu.make_async_remote_copy(
                    src_ref=g_refs[a] if gather else g_refs[a].at[slot], dst_ref=land_refs[a].at[me],
                    send_sem=send_sems.at[7 * a + k - 1],
                    recv_sem=recv_sems.at[7 * a + k - 1], device_id=peer, device_id_type=MESH).start()
        token[...] = jnp.zeros_like(token)

    thru = [pltpu.HBM(b.shape, b.dtype) for b in blocks]
    thru_lands = [pltpu.HBM(b.shape, b.dtype) for b in lands]
    return pl.pallas_call(
        body, name=name,
        out_shape=(pltpu.SemaphoreType.DMA((7 * n,)), pltpu.SemaphoreType.DMA((7 * n,)), *thru, *thru_lands,
                   jax.ShapeDtypeStruct((8, LANES), F32)),
        in_specs=[HBM] * (2 * n) + [pl.BlockSpec(memory_space=pl.ANY)],
        out_specs=(SEM, SEM, *[HBM] * (2 * n), pl.BlockSpec(memory_space=pltpu.VMEM)),
        input_output_aliases={i: 2 + i for i in range(2 * n)},
        compiler_params=pltpu.CompilerParams(has_side_effects=DATAFLOW),
    )(*[pltpu.with_memory_space_constraint(a, pltpu.HBM) for a in list(blocks) + lands], after)


def _scatter_wait(started, after, name, gather=False):
    send_sems, recv_sems, *rest = started
    n = (len(rest) - 1) // 2
    thru = rest[:2 * n]

    def body(*refs):
        g_refs, land_refs = refs[:n], refs[n:2 * n]
        send_sems, recv_sems = refs[2 * n], refs[2 * n + 1]
        me, peers = _peers()
        for k, peer, slot in peers:
            for a in range(n):
                copy = pltpu.make_async_remote_copy(
                    src_ref=g_refs[a] if gather else g_refs[a].at[slot], dst_ref=land_refs[a].at[slot],
                    send_sem=send_sems.at[7 * a + k - 1],
                    recv_sem=recv_sems.at[7 * a + k - 1], device_id=peer, device_id_type=MESH)
                copy.wait_send()
                copy.wait_recv()

    out = pl.pallas_call(
        body, name=name, out_shape=tuple(pltpu.HBM(a.shape, a.dtype) for a in thru),
        in_specs=[HBM] * (2 * n) + [SEM, SEM, pl.BlockSpec(memory_space=pl.ANY)], out_specs=tuple([HBM] * (2 * n)),
        input_output_aliases={i: i for i in range(2 * n)},
        compiler_params=pltpu.CompilerParams(has_side_effects=DATAFLOW),
    )(*thru, send_sems, recv_sems, after)
    return out[:n], out[n:]


BIG = (("w_ffn1_gate", "colT"), ("w_ffn1_up", "colT"), ("w_ffn1_down", "row"), ("w_ffn2_gate", "colT"),
       ("w_ffn2_up", "colT"), ("w_ffn2_down", "row"), ("w_in", "colT"), ("w_merge", "col"), ("w_ret_out", "col"),
       ("w_fox_out", "col"), ("w_out", "row"), ("w_ple", "col"), ("w_ple_gate", "row"))


def _shard_view(a, kind):
    a = a.reshape(a.shape[-2:])
    return a.T if kind == "colT" else a


def _unview(a, kind, shape):
    return (a.T if kind == "colT" else a).reshape(shape)


def _full_from_slots(g, kind):
    n, r, c = g.shape
    return g.transpose(1, 0, 2).reshape(r, n * c) if kind == "col" else g.reshape(n * r, c)


def _slots_from_full(f, kind):
    r, c = f.shape
    return f.reshape(r, N_DEV, c // N_DEV).transpose(1, 0, 2) if kind == "col" else f.reshape(N_DEV, r // N_DEV, c)


EARLY_GROUPS = (("w_ple_gate", "w_ple", "w_ffn2_down", "w_ffn2_gate", "w_ffn2_up", "w_out", "w_ret_out", "w_fox_out"),
                ("w_in", "w_merge"))


def _scatter_group(gw, names, after, name):
    kind = dict(BIG)
    return names, _scatter_start([_slots_from_full(gw[n], kind[n]) for n in names], after, name)


def _pad_heads(w):
    d = w.shape[1]
    return jnp.pad(w.reshape(FOX_HEADS, FOX_DIM, d), ((0, 0), (0, LANES - FOX_DIM), (0, 0))).reshape(FOX_TILES, d)


def _unpad_heads(w):
    d = w.shape[1]
    return w.reshape(FOX_HEADS, LANES, d)[:, :FOX_DIM].reshape(FOX_WIDTH, d)


def _deinterleave_rows(w):
    d = w.shape[1]
    return w.reshape(RET_HEADS, RET_DIM // 2, 2, d).transpose(0, 2, 1, 3).reshape(RET_WIDTH, d)


def _interleave_rows(w):
    d = w.shape[1]
    return w.reshape(RET_HEADS, 2, RET_DIM // 2, d).transpose(0, 2, 1, 3).reshape(RET_WIDTH, d)


def _pad_w_in(wt):
    d = wt.shape[1]
    rw, fw = RET_WIDTH, FOX_WIDTH
    fo = 4 * rw
    return jnp.concatenate([
        _deinterleave_rows(wt[:rw]), _deinterleave_rows(wt[rw:2 * rw]), wt[2 * rw:4 * rw],
        _pad_heads(wt[fo:fo + fw]), _pad_heads(wt[fo + fw:fo + 2 * fw]), _pad_heads(wt[fo + 2 * fw:fo + 3 * fw]),
        wt[fo + 3 * fw:], jnp.zeros((2 * LANES - FOX_HEADS, d), wt.dtype)], axis=0)


def _unpad_w_in(g):
    rw = RET_WIDTH
    f0 = 4 * rw
    return jnp.concatenate([
        _interleave_rows(g[:rw]), _interleave_rows(g[rw:2 * rw]), g[2 * rw:4 * rw],
        _unpad_heads(g[f0:f0 + FOX_TILES]), _unpad_heads(g[f0 + FOX_TILES:f0 + 2 * FOX_TILES]),
        _unpad_heads(g[f0 + 2 * FOX_TILES:f0 + 3 * FOX_TILES]),
        g[f0 + 3 * FOX_TILES:f0 + 3 * FOX_TILES + FOX_HEADS]], axis=0)


SMALL = ("ln_ffn1", "ln_mix", "b_forget", "b_merge", "ln_ffn2", "ln_ple", "ln_final")


def _small_rows(n):
    rows = -(-n // LANES)
    return -(-rows // 8) * 8


def _pack_small(vals, with_loss=None):
    parts = []
    for name in SMALL:
        v = vals[name].reshape(-1).astype(F32)
        rows = _small_rows(v.shape[0])
        parts.append(jnp.pad(v, (0, rows * LANES - v.shape[0])).reshape(rows, LANES))
    if with_loss is not None:
        parts.append(jnp.pad(with_loss.reshape(1, LANES), ((0, 7), (0, 0))))
    else:
        parts.append(jnp.zeros((8, LANES), F32))
    return jnp.concatenate(parts, axis=0)


def _unpack_small(packed, shapes):
    out, at = {}, 0
    for name in SMALL:
        n = int(np.prod(shapes[name]))
        rows = _small_rows(n)
        out[name] = packed[at:at + rows].reshape(-1)[:n].reshape(shapes[name])
        at += rows
    return out, packed[at, 0]


def _gather_finish(group, after, me):
    names, started, wait_name = group
    kind = dict(BIG)
    sent, landed = _scatter_wait(started, after, wait_name, gather=True)
    return {n: _full_from_slots(lax.dynamic_update_slice_in_dim(land, shard[None], me, 0), kind[n])
            for n, shard, land in zip(names, sent, landed)}


def _local_step(x, p, positions, target, w, small, me, entry_token, gathers):
    t, d = x.shape
    gain = lambda n: small[n].reshape(1, d)
    w = dict(w)
    bpad = jnp.pad(small["b_forget"].reshape(1, FOX_HEADS), ((0, 0), (0, LANES - FOX_HEADS)))
    bm = small["b_merge"].reshape(1, 2 * d)
    fox_blk = _tile(t, 512, 128)

    def ffn_fwd(n, tag):
        g = _mm([(n, w[f"w_{tag}_gate"], "nt")], F32, f"{tag}_gate", tn=1408)
        u = _mm([(n, w[f"w_{tag}_up"], "nt")], F32, f"{tag}_up", tn=1408)
        a = _swiglu_fwd(g, u, f"{tag}_swiglu")
        return g, u, a, _mm([(a, w[f"w_{tag}_down"], "nn")], F32, f"{tag}_down")

    n1 = _rms_fwd(x, gain("ln_ffn1") + entry_token, "rms_ffn1")
    g1, u1, a1, f1 = ffn_fwd(n1, "ffn1")
    h1, u = _rms_fwd(x, gain("ln_mix"), "rms_mix", f=f1)
    w.update(_gather_finish(gathers[0], f1, me))
    w_in_t = _pad_w_in(w["w_in"])
    gm = _mm([(u, w["w_merge"], "nn")], F32, "mixer_gates")
    pm = _mm([(u, w_in_t, "nt")], F32, "mixer_in", tn=1792)

    half = jnp.arange(RET_DIM // 2, dtype=F32) / (RET_DIM // 2)
    inv = 1.0 / (ROPE_BASE ** half)
    inv2 = jnp.concatenate([inv, inv]).reshape(1, RET_DIM)
    sign2 = jnp.concatenate([-jnp.ones((RET_DIM // 2,), F32), jnp.ones((RET_DIM // 2,), F32)]).reshape(1, RET_DIM)
    cos2, sin2 = _rope_tables(positions.reshape(t, 1), inv2, sign2)
    consts = _ret_consts()
    y_ret, y_raw, states = _ret_fwd(pm, cos2, sin2, consts)
    w.update(_gather_finish(gathers[1], y_raw, me))
    w_fox_pad = _pad_heads(w["w_fox_out"])
    za = _mm([(y_ret, w["w_ret_out"], "nn")], F32, "ret_out")

    qa, ka, va = _fox_prep(pm, bpad)
    o_fox, y_fox, qa_b = _fox_fwd(qa, ka, va, fox_blk)
    zb = _mm([(y_fox, w_fox_pad, "nn")], F32, "fox_out")

    mix = _merge_fwd(gm, bm, za, zb)
    mo = _mm([(mix, w["w_out"], "nn")], F32, "mix_out")
    h2, n2 = _rms_fwd(h1, gain("ln_ffn2"), "rms_ffn2", f=mo, scale=1.0)
    g2, u2, a2, f2 = ffn_fwd(n2, "ffn2")
    h3, n3 = _rms_fwd(h2, gain("ln_ple"), "rms_ple", f=f2)
    pgl = _mm([(n3, w["w_ple_gate"], "nn")], F32, "ple_gate")
    pb = p.astype(BF16)
    pe = _mm([(pb, w["w_ple"], "nn")], F32, "ple_embed")

    gw, gs = {}, {}
    dh4, dsg, dpe, loss, gs["ln_final"] = _ple_final(h3, pgl, pe, gain("ln_final"), target)
    gw["w_ple_gate"] = _mm([(n3, dsg, "tn")], BF16, "d_w_ple_gate", tn=256)
    gw["w_ple"] = _mm([(pb, dpe, "tn")], BF16, "d_w_ple", tn=256)
    dn3 = _mm([(dsg, w["w_ple_gate"], "nt")], F32, "d_n3")
    dh3, dh3_half, gs["ln_ple"] = _rms_bwd(dn3, h3, gain("ln_ple"), dh4, "rms_ple_bwd", 0.5)

    def ffn_bwd(dh_half, g, u_, a, n, tag, scatter_now=None):
        gw[f"w_{tag}_down"] = _mm([(a, dh_half, "tn")], BF16, f"d_w_{tag}_down", tm=1408, tn=256)
        if scatter_now is not None:
            scatter_now((f"w_{tag}_down",), dh_half, "c")
        da = _mm([(dh_half, w[f"w_{tag}_down"], "nt")], F32, f"d_a_{tag}", tn=1408)
        dg, du_ = _swiglu_bwd(da, g, u_, f"{tag}_swiglu_bwd")
        gw[f"w_{tag}_gate"] = _mm([(dg, n, "tn")], BF16, f"d_w_{tag}_gate", tm=1408, tn=256)
        gw[f"w_{tag}_up"] = _mm([(du_, n, "tn")], BF16, f"d_w_{tag}_up", tm=1408, tn=256)
        if scatter_now is not None:
            scatter_now((f"w_{tag}_gate", f"w_{tag}_up"), da, "d")
        return _mm([(dg, w[f"w_{tag}_gate"], "nn"), (du_, w[f"w_{tag}_up"], "nn")], F32, f"d_n_{tag}", tm=512)

    dn2 = ffn_bwd(dh3_half, g2, u2, a2, n2, "ffn2")
    dh2, dh2_b, gs["ln_ffn2"] = _rms_bwd(dn2, h2, gain("ln_ffn2"), dh3, "rms_ffn2_bwd", 1.0)

    gw["w_out"] = _mm([(mix, dh2_b, "tn")], BF16, "d_w_out", tn=256)
    dmix = _mm([(dh2_b, w["w_out"], "nt")], F32, "d_mix")
    dza, dzb, dgm, gs["b_merge"] = _merge_bwd(dmix, gm, bm, za, zb)
    gw["w_ret_out"] = _mm([(y_ret, dza, "tn")], BF16, "d_w_ret_out", tn=256)
    gw["w_fox_out"] = _unpad_heads(_mm([(y_fox, dzb, "tn")], BF16, "d_w_fox_out", tn=256))
    dy_ret = _mm([(dza, w["w_ret_out"], "nt")], F32, "d_y_ret")
    do_fox = _mm([(dzb, w_fox_pad, "nt")], F32, "d_y_fox")

    pending = [_scatter_group(gw, EARLY_GROUPS[0], dy_ret, "grads_scatter_a_start")]
    token = pending[0][1][-1][0, 0]
    drq, drk, drv, drg = _ret_bwd(dy_ret, pm, cos2, sin2, y_raw, states, consts[:3] + (consts[3] + token,))

    dqa, dka, dva, ds_rows, ds_cols = _fox_bwd(qa_b, ka, va, do_fox, o_fox, fox_blk)
    dc = jnp.pad((ds_rows + ds_cols).reshape(FOX_HEADS, t).T, ((0, 0), (0, LANES - FOX_HEADS)))
    dff, db_forget = _fox_post(dc, pm, bpad)
    gs["b_forget"] = db_forget[:, :FOX_HEADS]

    dpm = jnp.concatenate([drq, drk, drv, drg, dqa, dka, dva, dff, jnp.zeros((t, LANES), BF16)], axis=1)
    gw["w_merge"] = _mm([(u, dgm, "tn")], BF16, "d_w_merge", tn=512)
    gw["w_in"] = _unpad_w_in(_mm([(dpm, u, "tn")], BF16, "d_w_in", tm=1792, tn=256))
    du = _mm([(dpm, w_in_t, "nn"), (dgm, w["w_merge"], "nt")], F32, "d_u", tm=1024, tn=512)
    pending.append(_scatter_group(gw, EARLY_GROUPS[1], du, "grads_scatter_b_start"))
    token = pending[1][1][-1][0:1, 0:1]
    dh1, dh1_half, gs["ln_mix"] = _rms_bwd(du, h1, gain("ln_mix") + token, dh2, "rms_mix_bwd", 0.5)

    dn1 = ffn_bwd(dh1_half, g1, u1, a1, n1, "ffn1", scatter_now=lambda names, after, tag: pending.append(
        _scatter_group(gw, names, after, f"grads_scatter_{tag}_start")))
    dx, _, gs["ln_ffn1"] = _rms_bwd(dn1, x, gain("ln_ffn1"), dh1, "rms_ffn1_bwd", 1.0)
    return loss, dx, gw, gs, pending


WEIGHTS = ("ln_ffn1", "w_ffn1_gate", "w_ffn1_up", "w_ffn1_down", "ln_mix", "w_in", "b_forget", "w_merge", "b_merge",
           "w_ret_out", "w_fox_out", "w_out", "ln_ffn2", "w_ffn2_gate", "w_ffn2_up", "w_ffn2_down", "ln_ple", "w_ple",
           "w_ple_gate", "ln_final")


def kernel(x, p, positions, ln_ffn1, w_ffn1_gate, w_ffn1_up, w_ffn1_down, ln_mix, w_in, b_forget, w_merge, b_merge, w_ret_out, w_fox_out, w_out, ln_ffn2, w_ffn2_gate, w_ffn2_up, w_ffn2_down, ln_ple, w_ple, w_ple_gate, ln_final, loss_target, m_ln_ffn1, m_w_ffn1_gate, m_w_ffn1_up, m_w_ffn1_down, m_ln_mix, m_w_in, m_b_forget, m_w_merge, m_b_merge, m_w_ret_out, m_w_fox_out, m_w_out, m_ln_ffn2, m_w_ffn2_gate, m_w_ffn2_up, m_w_ffn2_down, m_ln_ple, m_w_ple, m_w_ple_gate, m_ln_final, v_ln_ffn1, v_w_ffn1_gate, v_w_ffn1_up, v_w_ffn1_down, v_ln_mix, v_w_in, v_b_forget, v_w_merge, v_b_merge, v_w_ret_out, v_w_fox_out, v_w_out, v_ln_ffn2, v_w_ffn2_gate, v_w_ffn2_up, v_w_ffn2_down, v_ln_ple, v_w_ple, v_w_ple_gate, v_ln_final):
    args = dict(ln_ffn1=ln_ffn1, w_ffn1_gate=w_ffn1_gate, w_ffn1_up=w_ffn1_up, w_ffn1_down=w_ffn1_down, ln_mix=ln_mix, w_in=w_in, b_forget=b_forget, w_merge=w_merge, b_merge=b_merge, w_ret_out=w_ret_out, w_fox_out=w_fox_out, w_out=w_out, ln_ffn2=ln_ffn2, w_ffn2_gate=w_ffn2_gate, w_ffn2_up=w_ffn2_up, w_ffn2_down=w_ffn2_down, ln_ple=ln_ple, w_ple=w_ple, w_ple_gate=w_ple_gate, ln_final=ln_final)
    moms = dict(ln_ffn1=m_ln_ffn1, w_ffn1_gate=m_w_ffn1_gate, w_ffn1_up=m_w_ffn1_up, w_ffn1_down=m_w_ffn1_down, ln_mix=m_ln_mix, w_in=m_w_in, b_forget=m_b_forget, w_merge=m_w_merge, b_merge=m_b_merge, w_ret_out=m_w_ret_out, w_fox_out=m_w_fox_out, w_out=m_w_out, ln_ffn2=m_ln_ffn2, w_ffn2_gate=m_w_ffn2_gate, w_ffn2_up=m_w_ffn2_up, w_ffn2_down=m_w_ffn2_down, ln_ple=m_ln_ple, w_ple=m_w_ple, w_ple_gate=m_w_ple_gate, ln_final=m_ln_final)
    vars_ = dict(ln_ffn1=v_ln_ffn1, w_ffn1_gate=v_w_ffn1_gate, w_ffn1_up=v_w_ffn1_up, w_ffn1_down=v_w_ffn1_down, ln_mix=v_ln_mix, w_in=v_w_in, b_forget=v_b_forget, w_merge=v_w_merge, b_merge=v_b_merge, w_ret_out=v_w_ret_out, w_fox_out=v_w_fox_out, w_out=v_w_out, ln_ffn2=v_ln_ffn2, w_ffn2_gate=v_w_ffn2_gate, w_ffn2_up=v_w_ffn2_up, w_ffn2_down=v_w_ffn2_down, ln_ple=v_ln_ple, w_ple=v_w_ple, w_ple_gate=v_w_ple_gate, ln_final=v_ln_final)
    kinds = ("grad", "delta", "new_m", "new_v")

    me = 4 * lax.axis_index("x") + 2 * lax.axis_index("y") + lax.axis_index("c")
    kind_of = dict(BIG)
    shard = {n: _shard_view(args[n], kind).astype(BF16) for n, kind in BIG}
    first = ("w_ffn1_gate", "w_ffn1_up", "w_ffn1_down")
    second = ("w_in", "w_merge")
    third = tuple(n for n, _ in BIG if n not in first + second)
    gathered = _all_gather([shard[n] for n in first])
    w_full = {n: _full_from_slots(g, kind_of[n]) for n, g in zip(first, gathered)}
    started_2 = _scatter_start([shard[n] for n in second], gathered[0], "weights_gather_a_start", gather=True)
    started_3 = _scatter_start([shard[n] for n in third], started_2[-1], "weights_gather_b_start", gather=True)
    gathers = [(second, started_2, "weights_gather_a_wait"), (third, started_3, "weights_gather_b_wait")]

    small = {n: args[n] for n in SMALL}
    loss_part, dx, gw, gs, pending = _local_step(x[0], p[0, 0], positions[0], loss_target[0], w_full, small, me,
                                                 started_3[-1][0:1, 0:1], gathers)

    parts_of = {}
    for tag, (names, started) in zip("abcd", pending):
        sent, landed = _scatter_wait(started, dx, f"grads_scatter_{tag}_wait")
        for n, blk, land in zip(names, sent, landed):
            own = lax.dynamic_index_in_dim(blk, me, 0, keepdims=True)
            parts_of[n] = lax.dynamic_update_slice_in_dim(land, own, me, 0)
    late = [(n, kind) for n, kind in BIG if n not in parts_of]
    small_part = _pack_small(gs, with_loss=loss_part)
    blocks = [_slots_from_full(gw[n], kind) for n, kind in late]
    blocks.append(jnp.broadcast_to(small_part, (N_DEV,) + small_part.shape))
    recv = _reduce_scatter_exchange(blocks)
    parts_of.update({n: r for (n, _), r in zip(late, recv)})

    res = {}
    for n, kind in BIG:
        parts = parts_of[n]
        outs = _adamw(parts, _shard_view(args[n], kind), _shard_view(moms[n], kind), _shard_view(vars_[n], kind),
                      f"adamw_{n}")
        for what, o in zip(kinds, outs):
            res[(what, n)] = _unview(o, kind, args[n].shape)
    s_outs = _adamw(recv[-1], _pack_small(small), _pack_small({n: moms[n] for n in SMALL}),
                    _pack_small({n: vars_[n] for n in SMALL}), "adamw_small")
    for what, sm in zip(kinds, s_outs):
        svals, extra = _unpack_small(sm, {n: args[n].shape for n in SMALL})
        if what == "grad":
            loss = extra
        for n in SMALL:
            res[(what, n)] = svals[n]
    return (loss, dx[None], *[res[(what, n)] for what in kinds for n in WEIGHTS])
```

```python
import numpy as np
import jax
import jax.numpy as jnp
from jax import lax
from jax.experimental import pallas as pl
from jax.experimental.pallas import tpu as pltpu

F32 = jnp.float32
BF16 = jnp.bfloat16

N_DEV = 8
EPS = 1e-6
RET_HEADS = 4
RET_DIM = 128
RET_WIDTH = RET_HEADS * RET_DIM
FOX_HEADS = 8
FOX_DIM = 64
FOX_WIDTH = FOX_HEADS * FOX_DIM
CHUNK = 128
ROPE_BASE = 10000.0
LANES = 128
FOX_TILES = FOX_HEADS * LANES
IN_COLS = 4 * RET_WIDTH + 3 * FOX_WIDTH + FOX_HEADS
IN_PAD = 4 * RET_WIDTH + 3 * FOX_TILES + 2 * LANES
TILE_RQ, TILE_RK, TILE_RV, TILE_RG = 0, 4, 8, 12
TILE_FQ, TILE_FK, TILE_FV, TILE_FF = 16, 24, 32, 40
NEG = -1e30

ADAM_LR = 0.001
ADAM_B1 = 0.9
ADAM_B2 = 0.999
ADAM_EPS = 1e-08
ADAM_WD = 0.01
ADAM_STEP = 10

VMEM_LIMIT_BYTES = 56 * 1024 * 1024

MESH = pl.DeviceIdType.MESH


def _tile(dim, pref, mult):
    if dim <= pref:
        return dim
    t = (pref // mult) * mult
    while t >= mult:
        if dim % t == 0:
            return t
        t -= mult
    return dim


def _params(dims):
    return pltpu.CompilerParams(dimension_semantics=dims, vmem_limit_bytes=VMEM_LIMIT_BYTES)


def _pcall(body, *, name, out_shape, grid, in_specs, out_specs, scratch_shapes=(), dims=None):
    return pl.pallas_call(body, name=name, out_shape=out_shape, grid=grid, in_specs=in_specs, out_specs=out_specs,
                          scratch_shapes=list(scratch_shapes), compiler_params=_params(dims))


def _dot(a, b, ca, cb):
    return lax.dot_general(a, b, (((ca,), (cb,)), ((), ())), preferred_element_type=F32)


def _sigmoid(x):
    return 1.0 / (1.0 + jnp.exp(-x))


def _mm(pairs, out_dtype, name, tm=1024, tn=1024, after=None):
    dims = []
    for a, b, mode in pairs:
        m, k = (a.shape[1], a.shape[0]) if mode == "tn" else a.shape
        n, k2 = b.shape if mode == "nt" else (b.shape[1], b.shape[0])
        assert k == k2, (name, a.shape, b.shape, mode)
        dims.append((m, n))
    assert all(d == dims[0] for d in dims), (name, dims)
    m, n = dims[0]
    tm = _tile(m, tm, 128 if any(mode == "tn" for _, _, mode in pairs) else 16)
    tn = _tile(n, tn, 128)
    in_specs, contract, operands = [], [], []
    for a, b, mode in pairs:
        k = a.shape[0] if mode == "tn" else a.shape[1]
        in_specs.append(pl.BlockSpec((k, tm), lambda i, j: (0, i)) if mode == "tn" else
                        pl.BlockSpec((tm, k), lambda i, j: (i, 0)))
        in_specs.append(pl.BlockSpec((tn, k), lambda i, j: (j, 0)) if mode == "nt" else
                        pl.BlockSpec((k, tn), lambda i, j: (0, j)))
        contract.append((0 if mode == "tn" else 1, 1 if mode == "nt" else 0))
        operands += [a, b]
    if after is not None:
        in_specs.append(pl.BlockSpec(memory_space=pl.ANY))
        operands.append(after)

    def body(*refs):
        o_ref = refs[-1]
        acc = None
        for p, (ca, cb) in enumerate(contract):
            part = _dot(refs[2 * p][...], refs[2 * p + 1][...], ca, cb)
            acc = part if acc is None else acc + part
        o_ref[...] = acc.astype(out_dtype)

    return _pcall(body, name=name, out_shape=jax.ShapeDtypeStruct((m, n), out_dtype), grid=(m // tm, n // tn),
                  in_specs=in_specs, out_specs=pl.BlockSpec((tm, tn), lambda i, j: (i, j)),
                  dims=("parallel", "parallel"))(*operands)


def _rms_fwd(h, gain, name, f=None, scale=0.5):
    t, d = h.shape
    tt = _tile(t, 512, 16)
    row = pl.BlockSpec((tt, d), lambda i: (i, 0))
    vec = pl.BlockSpec((1, d), lambda i: (0, 0))

    def norm(hv, g_ref, n_ref):
        r = lax.rsqrt(jnp.mean(hv * hv, axis=-1, keepdims=True) + EPS)
        n_ref[...] = (hv * r * g_ref[...]).astype(BF16)

    if f is None:

        def body(h_ref, g_ref, n_ref):
            norm(h_ref[...], g_ref, n_ref)

        return _pcall(body, name=name, out_shape=jax.ShapeDtypeStruct((t, d), BF16), grid=(t // tt,),
                      in_specs=[row, vec], out_specs=row, dims=("parallel",))(h, gain)

    def body(h_ref, f_ref, g_ref, hn_ref, n_ref):
        hv = h_ref[...] + scale * f_ref[...]
        hn_ref[...] = hv
        norm(hv, g_ref, n_ref)

    return _pcall(body, name=name,
                  out_shape=(jax.ShapeDtypeStruct((t, d), F32), jax.ShapeDtypeStruct((t, d), BF16)),
                  grid=(t // tt,), in_specs=[row, row, vec], out_specs=(row, row), dims=("parallel",))(h, f, gain)


def _rms_bwd(dn, h, gain, dh_in, name, out_scale):
    t, d = h.shape
    tt = _tile(t, 512, 16)
    row = pl.BlockSpec((tt, d), lambda i: (i, 0))
    vec = pl.BlockSpec((1, d), lambda i: (0, 0))

    def body(dn_ref, h_ref, g_ref, dhin_ref, dh_ref, dhb_ref, dg_ref):
        hv = h_ref[...]
        dnv = dn_ref[...].astype(F32)
        r = lax.rsqrt(jnp.mean(hv * hv, axis=-1, keepdims=True) + EPS)
        dng = dnv * g_ref[...]
        dh = dhin_ref[...] + r * dng - hv * (r * r * r) * jnp.mean(dng * hv, axis=-1, keepdims=True)
        dh_ref[...] = dh
        dhb_ref[...] = (out_scale * dh).astype(BF16)
        part = jnp.sum(dnv * hv * r, axis=0, keepdims=True)

        @pl.when(pl.program_id(0) == 0)
        def _():
            dg_ref[...] = part

        @pl.when(pl.program_id(0) > 0)
        def _():
            dg_ref[...] += part

    return _pcall(body, name=name,
                  out_shape=(jax.ShapeDtypeStruct((t, d), F32), jax.ShapeDtypeStruct((t, d), BF16),
                             jax.ShapeDtypeStruct((1, d), F32)),
                  grid=(t // tt,), in_specs=[row, row, vec, row], out_specs=(row, row, vec),
                  dims=("arbitrary",))(dn, h, gain, dh_in)


def _swiglu_fwd(g, u, name):
    t, f = g.shape
    tt = _tile(t, 256, 16)
    row = pl.BlockSpec((tt, f), lambda i: (i, 0))

    def body(g_ref, u_ref, a_ref):
        gv = g_ref[...]
        a_ref[...] = (gv * _sigmoid(gv) * u_ref[...]).astype(BF16)

    return _pcall(body, name=name, out_shape=jax.ShapeDtypeStruct((t, f), BF16), grid=(t // tt,),
                  in_specs=[row, row], out_specs=row, dims=("parallel",))(g, u)


def _swiglu_bwd(da, g, u, name):
    t, f = g.shape
    tt = _tile(t, 256, 16)
    row = pl.BlockSpec((tt, f), lambda i: (i, 0))

    def body(da_ref, g_ref, u_ref, dg_ref, du_ref):
        gv = g_ref[...]
        dav = da_ref[...].astype(F32)
        sg = _sigmoid(gv)
        dg_ref[...] = (dav * u_ref[...] * (sg * (1.0 + gv * (1.0 - sg)))).astype(BF16)
        du_ref[...] = (dav * (gv * sg)).astype(BF16)

    return _pcall(body, name=name, out_shape=(jax.ShapeDtypeStruct((t, f), BF16),) * 2, grid=(t // tt,),
                  in_specs=[row, row, row], out_specs=(row, row), dims=("parallel",))(da, g, u)


def _rope_tables(pos_col, inv2, sign2):
    t = pos_col.shape[0]

    def body(p_ref, inv_ref, sg_ref, c_ref, s_ref):
        ang = p_ref[...].astype(F32) * inv_ref[...]
        c_ref[...] = jnp.cos(ang)
        s_ref[...] = jnp.sin(ang) * sg_ref[...]

    full = lambda shape: pl.BlockSpec(shape, lambda i: (0, 0))
    return _pcall(body, name="rope_tables", out_shape=(jax.ShapeDtypeStruct((t, RET_DIM), F32),) * 2, grid=(1,),
                  in_specs=[full((t, 1)), full((1, RET_DIM)), full((1, RET_DIM))],
                  out_specs=(full((t, RET_DIM)),) * 2, dims=("arbitrary",))(pos_col, inv2, sign2)


def _rot(x, c, s):
    return x * c + pltpu.roll(x, RET_DIM // 2, 1) * s


def _rot_t(g, c, s):
    return g * c + pltpu.roll(g * s, RET_DIM // 2, 1)


def _ret_consts():
    hh = np.arange(RET_HEADS, dtype=np.float32)
    log_gamma = np.log1p(-np.exp2(-5.0 - hh)).astype(np.float32)
    idx = np.arange(CHUNK, dtype=np.float32)
    diff = idx[:, None] - idx[None, :]
    dmask = np.where(diff >= 0, np.exp(log_gamma[:, None, None] * np.maximum(diff, 0.0)), 0.0).astype(np.float32)
    kdec = np.exp(log_gamma[:, None] * (CHUNK - 1 - idx)).astype(np.float32)
    qdec = np.exp(log_gamma[:, None] * (idx + 1.0)).astype(np.float32)
    cdec = np.exp(log_gamma * CHUNK).astype(np.float32)
    bc = lambda v: np.ascontiguousarray(np.broadcast_to(v[:, :, None], (RET_HEADS, CHUNK, RET_DIM)))
    cd = np.ascontiguousarray(np.broadcast_to(cdec[:, None, None], (RET_HEADS, 8, RET_DIM)))
    return jnp.asarray(dmask), jnp.asarray(bc(qdec)), jnp.asarray(bc(kdec)), jnp.asarray(cd)


def _ret_fwd(pm, cos2, sin2, consts):
    t = pm.shape[0]
    n_chunks = t // CHUNK
    dmask, qdec, kdec, cd = consts
    scale = RET_DIM ** -0.5

    def col(c0):
        return pl.BlockSpec((CHUNK, RET_DIM), lambda h, n: (n, c0 + h))

    tab = pl.BlockSpec((CHUNK, RET_DIM), lambda h, n: (n, 0))
    head3 = lambda r: pl.BlockSpec((None, r, RET_DIM), lambda h, n: (h, 0, 0))

    def body(q_ref, k_ref, v_ref, g_ref, c_ref, s_ref, dm_ref, qd_ref, kd_ref, cd_ref, y_ref, raw_ref, st_ref, s_acc):
        @pl.when(pl.program_id(1) == 0)
        def _():
            s_acc[...] = jnp.zeros_like(s_acc)

        c, s = c_ref[...], s_ref[...]
        q = _rot(q_ref[...], c, s)
        k = _rot(k_ref[...], c, s) * scale
        vb = v_ref[...].astype(BF16)
        g = g_ref[...]
        s_in = s_acc[...]
        st_ref[...] = s_in
        a = _dot(q.astype(BF16), k.astype(BF16), 1, 1) * dm_ref[...]
        y = _dot(a.astype(BF16), vb, 1, 0) + _dot((q * qd_ref[...]).astype(BF16), s_in.astype(BF16), 1, 0)
        s_acc[...] = cd_ref[0:1, :] * s_in + _dot((k * kd_ref[...]).astype(BF16), vb, 0, 0)
        raw_ref[...] = y
        mu = jnp.mean(y, axis=-1, keepdims=True)
        yc = y - mu
        rs = lax.rsqrt(jnp.mean(yc * yc, axis=-1, keepdims=True) + EPS)
        y_ref[...] = (yc * rs * (g * _sigmoid(g))).astype(BF16)

    out_blk = pl.BlockSpec((CHUNK, RET_DIM), lambda h, n: (n, h))
    return _pcall(
        body, name="retention_fwd",
        out_shape=(jax.ShapeDtypeStruct((t, RET_WIDTH), BF16), jax.ShapeDtypeStruct((t, RET_WIDTH), F32),
                   jax.ShapeDtypeStruct((RET_HEADS, n_chunks, RET_DIM, RET_DIM), F32)),
        grid=(RET_HEADS, n_chunks),
        in_specs=[col(TILE_RQ), col(TILE_RK), col(TILE_RV), col(TILE_RG), tab, tab,
                  pl.BlockSpec((None, CHUNK, CHUNK), lambda h, n: (h, 0, 0)), head3(CHUNK), head3(CHUNK), head3(8)],
        out_specs=(out_blk, out_blk, pl.BlockSpec((None, None, RET_DIM, RET_DIM), lambda h, n: (h, n, 0, 0))),
        scratch_shapes=[pltpu.VMEM((RET_DIM, RET_DIM), F32)],
        dims=("parallel", "arbitrary"),
    )(pm, pm, pm, pm, cos2, sin2, dmask, qdec, kdec, cd)


def _ret_bwd(dy, pm, cos2, sin2, raw, states, consts):
    t = pm.shape[0]
    n_chunks = t // CHUNK
    dmask, qdec, kdec, cd = consts
    scale = RET_DIM ** -0.5
    rev = lambda n: n_chunks - 1 - n

    def col(c0):
        return pl.BlockSpec((CHUNK, RET_DIM), lambda h, n: (rev(n), c0 + h))

    tab = pl.BlockSpec((CHUNK, RET_DIM), lambda h, n: (rev(n), 0))
    blk = pl.BlockSpec((CHUNK, RET_DIM), lambda h, n: (rev(n), h))
    head3 = lambda r: pl.BlockSpec((None, r, RET_DIM), lambda h, n: (h, 0, 0))

    def body(dy_ref, q_ref, k_ref, v_ref, g_ref, c_ref, s_ref, raw_ref, st_ref, dm_ref, qd_ref, kd_ref, cd_ref,
             dq_ref, dk_ref, dv_ref, dg_ref, ds_acc):
        @pl.when(pl.program_id(1) == 0)
        def _():
            ds_acc[...] = jnp.zeros_like(ds_acc)

        c, s = c_ref[...], s_ref[...]
        q = _rot(q_ref[...], c, s)
        k = _rot(k_ref[...], c, s) * scale
        qb, kb, vb = q.astype(BF16), k.astype(BF16), v_ref[...].astype(BF16)
        g = g_ref[...]
        dm, qd, kd = dm_ref[...], qd_ref[...], kd_ref[...]
        y = raw_ref[...]
        mu = jnp.mean(y, axis=-1, keepdims=True)
        yc = y - mu
        rs = lax.rsqrt(jnp.mean(yc * yc, axis=-1, keepdims=True) + EPS)
        yn = yc * rs
        sg = _sigmoid(g)
        dyo = dy_ref[...]
        dg_ref[...] = (dyo * yn * (sg * (1.0 + g * (1.0 - sg)))).astype(BF16)
        dyn = dyo * (g * sg)
        dyr = rs * (dyn - jnp.mean(dyn, axis=-1, keepdims=True) - yn * jnp.mean(dyn * yn, axis=-1, keepdims=True))
        dyb = dyr.astype(BF16)
        s_in = st_ref[...].astype(BF16)
        ds_out = ds_acc[...]
        dsb = ds_out.astype(BF16)
        a = _dot(qb, kb, 1, 1) * dm
        da = (_dot(dyb, vb, 1, 1) * dm).astype(BF16)
        kdb = (k * kd).astype(BF16)
        qdb = (q * qd).astype(BF16)
        dv_ref[...] = (_dot(a.astype(BF16), dyb, 0, 0) + _dot(kdb, dsb, 1, 0)).astype(BF16)
        dqh = _dot(da, kb, 1, 0) + _dot(dyb, s_in, 1, 1) * qd
        dkh = _dot(da, qb, 0, 0) + _dot(vb, dsb, 1, 1) * kd
        ds_acc[...] = cd_ref[0:1, :] * ds_out + _dot(qdb, dyb, 0, 0)
        dq_ref[...] = _rot_t(dqh, c, s).astype(BF16)
        dk_ref[...] = (_rot_t(dkh, c, s) * scale).astype(BF16)

    return _pcall(
        body, name="retention_bwd",
        out_shape=(jax.ShapeDtypeStruct((t, RET_WIDTH), BF16),) * 4,
        grid=(RET_HEADS, n_chunks),
        in_specs=[blk, col(TILE_RQ), col(TILE_RK), col(TILE_RV), col(TILE_RG), tab, tab, blk,
                  pl.BlockSpec((None, None, RET_DIM, RET_DIM), lambda h, n: (h, rev(n), 0, 0)),
                  pl.BlockSpec((None, CHUNK, CHUNK), lambda h, n: (h, 0, 0)), head3(CHUNK), head3(CHUNK), head3(8)],
        out_specs=(blk,) * 4,
        scratch_shapes=[pltpu.VMEM((RET_DIM, RET_DIM), F32)],
        dims=("parallel", "arbitrary"),
    )(dy, pm, pm, pm, pm, cos2, sin2, raw, states, dmask, qdec, kdec, cd)


FOX_C_LANE = FOX_DIM
FOX_NEGC_LANE = FOX_DIM + 3
FOX_LSE_LANE = FOX_DIM + 6
FOX_L_LANE = FOX_C_LANE
FOX_ROWSUM_LANE = FOX_C_LANE
FOX_COLSUM_LANE = FOX_NEGC_LANE


def _split3(x):
    hi = x.astype(BF16)
    r1 = x - hi.astype(F32)
    mid = r1.astype(BF16)
    lo = (r1 - mid.astype(F32)).astype(BF16)
    return hi, mid, lo


def _tri_dot(tri, x):
    hi, mid, lo = _split3(x)
    return _dot(tri, lo, 1, 0) + _dot(tri, mid, 1, 0) + _dot(tri, hi, 1, 0)


def _log_sigmoid(z):
    return jnp.minimum(z, 0.0) - jnp.log1p(jnp.exp(-jnp.abs(z)))


def _fox_consts():
    place = np.zeros((2, 3, LANES, FOX_TILES), np.float32)
    ones = np.zeros((3, 1, FOX_TILES), np.float32)
    for h in range(FOX_HEADS):
        for part in range(3):
            place[0, part, h, LANES * h + FOX_C_LANE + part] = 1.0
            place[1, part, h, LANES * h + FOX_NEGC_LANE + part] = -1.0
            ones[0, 0, LANES * h + FOX_NEGC_LANE + part] = 1.0
            ones[1, 0, LANES * h + FOX_C_LANE + part] = 1.0
            ones[1, 0, LANES * h + FOX_LSE_LANE + part] = 1.0
            ones[2, 0, LANES * h + FOX_C_LANE + part] = 1.0
    return jnp.asarray(place, BF16), jnp.asarray(ones, F32)


def _fox_prep(pm, bpad):
    t = pm.shape[0]
    tt = _tile(t, 512, LANES)
    place, ones = _fox_consts()
    wide = lambda c0: pl.BlockSpec((tt, FOX_TILES), lambda i: (i, c0 // FOX_HEADS))
    const = lambda a: pl.BlockSpec(a.shape, lambda i: (0,) * a.ndim)

    def body(q_ref, k_ref, v_ref, ff_ref, b_ref, pl_ref, on_ref, qa_ref, ka_ref, va_ref, carry_s):
        @pl.when(pl.program_id(0) == 0)
        def _():
            carry_s[...] = jnp.zeros_like(carry_s)

        r = lax.broadcasted_iota(jnp.int32, (LANES, LANES), 0)
        cc = lax.broadcasted_iota(jnp.int32, (LANES, LANES), 1)
        tri = jnp.where(cc <= r, 1.0, 0.0).astype(BF16)
        bias = b_ref[...]
        for sub in range(tt // LANES):
            rows = pl.ds(sub * LANES, LANES)
            cs = _tri_dot(tri, _log_sigmoid(ff_ref[rows, :] + bias)) + carry_s[...]
            carry_s[...] = cs[LANES - 1:LANES, :]
            parts = _split3(cs)
            eq = sum(_dot(part, pl_ref[0, i], 1, 0) for i, part in enumerate(parts))
            ek = sum(_dot(part, pl_ref[1, i], 1, 0) for i, part in enumerate(parts))
            qa_ref[rows, :] = (q_ref[rows, :] * FOX_DIM ** -0.5 + eq + on_ref[0]).astype(BF16)
            ka_ref[rows, :] = (k_ref[rows, :] + ek + on_ref[1]).astype(BF16)
            va_ref[rows, :] = (v_ref[rows, :] + on_ref[2]).astype(BF16)

    out = pl.BlockSpec((tt, FOX_TILES), lambda i: (i, 0))
    return _pcall(body, name="fox_prep", out_shape=(jax.ShapeDtypeStruct((t, FOX_TILES), BF16),) * 3, grid=(t // tt,),
                  in_specs=[wide(TILE_FQ), wide(TILE_FK), wide(TILE_FV), pl.BlockSpec((tt, LANES), lambda i: (i, TILE_FF)),
                            pl.BlockSpec((1, LANES), lambda i: (0, 0)), const(place), const(ones)],
                  out_specs=(out,) * 3, scratch_shapes=[pltpu.VMEM((1, LANES), F32)],
                  dims=("arbitrary",))(pm, pm, pm, pm, bpad, place, ones)


def _fox_post(dc, pm, bpad):
    t = pm.shape[0]
    nb = t // LANES

    def body(dc_ref, ff_ref, b_ref, d_ref, db_ref):
        r = lax.broadcasted_iota(jnp.int32, (LANES, LANES), 0)
        cc = lax.broadcasted_iota(jnp.int32, (LANES, LANES), 1)
        tri = jnp.where(cc >= r, 1.0, 0.0).astype(BF16)
        bias = b_ref[...]

        def step(i, carry):
            tail, acc = carry
            rows = pl.ds(pl.multiple_of((nb - 1 - i) * LANES, LANES), LANES)
            cs = _tri_dot(tri, dc_ref[rows, :]) + tail
            dff = cs * _sigmoid(-(ff_ref[rows, :] + bias))
            d_ref[rows, :] = dff.astype(BF16)
            return cs[0:1, :], acc + jnp.sum(dff, axis=0, keepdims=True)

        zero = jnp.zeros((1, LANES), F32)
        _, acc = lax.fori_loop(0, nb, step, (zero, zero))
        db_ref[...] = acc

    return _pcall(body, name="fox_forget_bwd",
                  out_shape=(jax.ShapeDtypeStruct((t, LANES), BF16), jax.ShapeDtypeStruct((1, LANES), F32)), grid=(1,),
                  in_specs=[pl.BlockSpec((t, LANES), lambda i: (0, 0)), pl.BlockSpec((t, LANES), lambda i: (0, TILE_FF)),
                            pl.BlockSpec((1, LANES), lambda i: (0, 0))],
                  out_specs=(pl.BlockSpec((t, LANES), lambda i: (0, 0)), pl.BlockSpec((1, LANES), lambda i: (0, 0))),
                  dims=("arbitrary",))(dc, pm, bpad)


def _tri_tables(nb, q_major):
    pairs = [(i, j) for i in range(nb) for j in range(i + 1)] if q_major else \
            [(i, j) for j in range(nb) for i in range(j, nb)]
    return jnp.asarray([a for a, _ in pairs], jnp.int32), jnp.asarray([b for _, b in pairs], jnp.int32)


def _causal(s):
    n = s.shape[0]
    row = lax.broadcasted_iota(jnp.int32, (n, n), 0)
    col = lax.broadcasted_iota(jnp.int32, (n, n), 1)
    return jnp.where(col <= row, s, NEG)


def _lane_col(x, lane):
    sel = lax.broadcasted_iota(jnp.int32, x.shape, 1) == lane
    return jnp.sum(jnp.where(sel, x, 0.0), axis=1, keepdims=True)


def _fox_fwd(qa, ka, va, blk):
    t = qa.shape[0]
    nb = t // blk
    qi, kj = _tri_tables(nb, True)
    q_spec = pl.BlockSpec((blk, LANES), lambda h, s, qi_r, kj_r: (qi_r[s], h))
    k_spec = pl.BlockSpec((blk, LANES), lambda h, s, qi_r, kj_r: (kj_r[s], h))

    def body(qi_r, kj_r, q_ref, k_ref, v_ref, o_ref, ob_ref, qb_ref, m_s, acc_s):
        s_id = pl.program_id(1)
        i, j = qi_r[s_id], kj_r[s_id]

        @pl.when(j == 0)
        def _():
            m_s[...] = jnp.full_like(m_s, NEG)
            acc_s[...] = jnp.zeros_like(acc_s)

        def tile(diagonal):
            s = _dot(q_ref[...], k_ref[...], 1, 1)
            if diagonal:
                s = _causal(s)
            m_old = m_s[...]
            m_new = jnp.maximum(m_old, jnp.max(s, axis=1, keepdims=True))
            p = jnp.exp(s - jnp.tile(m_new, (1, blk // LANES)))
            acc_s[...] = jnp.exp(m_old - m_new) * acc_s[...] + _dot(p.astype(BF16), v_ref[...], 1, 0)
            m_s[...] = m_new

        @pl.when(j < i)
        def _():
            tile(False)

        @pl.when(j == i)
        def _():
            tile(True)
            acc = acc_s[...]
            l = _lane_col(acc, FOX_L_LANE)
            o = acc / l
            o_ref[...] = o
            ob_ref[...] = o.astype(BF16)
            hi, mid, lo = _split3(-(m_s[:, 0:1] + jnp.log(l)))
            lane = lax.broadcasted_iota(jnp.int32, acc.shape, 1)
            qb_ref[...] = jnp.where(lane == FOX_LSE_LANE, hi,
                                    jnp.where(lane == FOX_LSE_LANE + 1, mid,
                                              jnp.where(lane == FOX_LSE_LANE + 2, lo, q_ref[...])))

    wide = (t, FOX_TILES)
    return pl.pallas_call(
        body, name="fox_fwd",
        out_shape=(jax.ShapeDtypeStruct(wide, F32), jax.ShapeDtypeStruct(wide, BF16), jax.ShapeDtypeStruct(wide, BF16)),
        grid_spec=pltpu.PrefetchScalarGridSpec(
            num_scalar_prefetch=2, grid=(FOX_HEADS, qi.shape[0]), in_specs=[q_spec, k_spec, k_spec],
            out_specs=(q_spec,) * 3,
            scratch_shapes=[pltpu.VMEM((blk, LANES), F32), pltpu.VMEM((blk, LANES), F32)]),
        compiler_params=_params(("parallel", "arbitrary")),
    )(qi, kj, qa, ka, va)


def _fox_bwd(qa, ka, va, do, o, blk):
    t = qa.shape[0]
    nb = t // blk
    qi, kj = _tri_tables(nb, False)
    q_spec = pl.BlockSpec((blk, LANES), lambda h, s, qi_r, kj_r: (qi_r[s], h))
    k_spec = pl.BlockSpec((blk, LANES), lambda h, s, qi_r, kj_r: (kj_r[s], h))
    head_spec = pl.BlockSpec((t, LANES), lambda h, s, qi_r, kj_r: (0, h))
    head_col = pl.BlockSpec((None, t, 1), lambda h, s, qi_r, kj_r: (h, 0, 0))
    k_col = pl.BlockSpec((None, blk, 1), lambda h, s, qi_r, kj_r: (h, kj_r[s], 0))
    first_spec = pl.BlockSpec((blk, LANES), lambda h, s, qi_r, kj_r: (jnp.where(kj_r[s] == 0, qi_r[s], nb - 1), h))
    n_steps = int(qi.shape[0])

    def body(qi_r, kj_r, q_ref, k_ref, v_ref, do_ref, o_ref, dq_ref, dk_ref, dv_ref, rs_ref, cs_ref,
             doa_s, dq_s, dk_s, dv_s):
        s_id = pl.program_id(1)
        i, j = qi_r[s_id], kj_r[s_id]
        rows = pl.ds(pl.multiple_of(i * blk, blk), blk)

        @pl.when(j == 0)
        def _():
            dof = do_ref[...]
            hi, mid, lo = _split3(-jnp.sum(dof * o_ref[...], axis=1, keepdims=True))
            lane = lax.broadcasted_iota(jnp.int32, dof.shape, 1)
            doa = jnp.where(lane == FOX_C_LANE, hi.astype(F32),
                            jnp.where(lane == FOX_C_LANE + 1, mid.astype(F32),
                                      jnp.where(lane == FOX_C_LANE + 2, lo.astype(F32), dof)))
            doa_s[rows, :] = doa.astype(BF16)
            dq_s[rows, :] = jnp.zeros((blk, LANES), F32)

        @pl.when(i == j)
        def _():
            dk_s[...] = jnp.zeros_like(dk_s)
            dv_s[...] = jnp.zeros_like(dv_s)

        def tile(diagonal):
            q, k = q_ref[...], k_ref[...]
            s = _dot(q, k, 1, 1)
            if diagonal:
                s = _causal(s)
            p = jnp.exp(s)
            doa = doa_s[rows, :]
            ds = (p * _dot(doa, v_ref[...], 1, 1)).astype(BF16)
            dv_s[...] += _dot(p.astype(BF16), doa, 0, 0)
            dk_s[...] += _dot(ds, q, 0, 0)
            dq_s[rows, :] += _dot(ds, k, 1, 0)

        @pl.when(i > j)
        def _():
            tile(False)

        @pl.when(i == j)
        def _():
            tile(True)

        @pl.when(i == nb - 1)
        def _():
            dk = dk_s[...]
            dk_ref[...] = dk.astype(BF16)
            dv_ref[...] = dv_s[...].astype(BF16)
            cs_ref[...] = -_lane_col(dk, FOX_COLSUM_LANE)

        @pl.when(s_id == n_steps - 1)
        def _():
            dq = dq_s[...]
            dq_ref[...] = (dq * FOX_DIM ** -0.5).astype(BF16)
            rs_ref[...] = _lane_col(dq, FOX_ROWSUM_LANE)

    wide = jax.ShapeDtypeStruct((t, FOX_TILES), BF16)
    cols = jax.ShapeDtypeStruct((FOX_HEADS, t, 1), F32)
    return pl.pallas_call(
        body, name="fox_bwd", out_shape=(wide, wide, wide, cols, cols),
        grid_spec=pltpu.PrefetchScalarGridSpec(
            num_scalar_prefetch=2, grid=(FOX_HEADS, n_steps),
            in_specs=[q_spec, k_spec, k_spec, first_spec, first_spec],
            out_specs=(head_spec, k_spec, k_spec, head_col, k_col),
            scratch_shapes=[pltpu.VMEM((t, LANES), BF16), pltpu.VMEM((t, LANES), F32), pltpu.VMEM((blk, LANES), F32),
                            pltpu.VMEM((blk, LANES), F32)]),
        compiler_params=_params(("parallel", "arbitrary")),
    )(qi, kj, qa, ka, va, do, o)


def _merge_fwd(gm, bm, za, zb):
    t, d = za.shape
    tt = _tile(t, 256, 16)
    row = pl.BlockSpec((tt, d), lambda i: (i, 0))

    def body(gm_ref, b_ref, za_ref, zb_ref, o_ref):
        ga = _sigmoid(gm_ref[:, :d] + b_ref[:, :d])
        gb = _sigmoid(gm_ref[:, d:] + b_ref[:, d:])
        o_ref[...] = (ga * za_ref[...] + gb * zb_ref[...]).astype(BF16)

    return _pcall(body, name="merge_fwd", out_shape=jax.ShapeDtypeStruct((t, d), BF16), grid=(t // tt,),
                  in_specs=[pl.BlockSpec((tt, 2 * d), lambda i: (i, 0)), pl.BlockSpec((1, 2 * d), lambda i: (0, 0)), row, row],
                  out_specs=row, dims=("parallel",))(gm, bm, za, zb)


def _merge_bwd(dmix, gm, bm, za, zb):
    t, d = za.shape
    tt = _tile(t, 256, 16)
    row = pl.BlockSpec((tt, d), lambda i: (i, 0))
    wide = pl.BlockSpec((tt, 2 * d), lambda i: (i, 0))
    vec = pl.BlockSpec((1, 2 * d), lambda i: (0, 0))

    def body(dm_ref, gm_ref, b_ref, za_ref, zb_ref, dza_ref, dzb_ref, dgm_ref, db_ref):
        dm = dm_ref[...]
        ga = _sigmoid(gm_ref[:, :d] + b_ref[:, :d])
        gb = _sigmoid(gm_ref[:, d:] + b_ref[:, d:])
        dza_ref[...] = (dm * ga).astype(BF16)
        dzb_ref[...] = (dm * gb).astype(BF16)
        dla = dm * za_ref[...] * ga * (1.0 - ga)
        dlb = dm * zb_ref[...] * gb * (1.0 - gb)
        dgm_ref[:, :d] = dla.astype(BF16)
        dgm_ref[:, d:] = dlb.astype(BF16)
        pa = jnp.sum(dla, axis=0, keepdims=True)
        pb = jnp.sum(dlb, axis=0, keepdims=True)

        @pl.when(pl.program_id(0) == 0)
        def _():
            db_ref[:, :d] = pa
            db_ref[:, d:] = pb

        @pl.when(pl.program_id(0) > 0)
        def _():
            db_ref[:, :d] += pa
            db_ref[:, d:] += pb

    return _pcall(body, name="merge_bwd",
                  out_shape=(jax.ShapeDtypeStruct((t, d), BF16), jax.ShapeDtypeStruct((t, d), BF16),
                             jax.ShapeDtypeStruct((t, 2 * d), BF16), jax.ShapeDtypeStruct((1, 2 * d), F32)),
                  grid=(t // tt,), in_specs=[row, wide, vec, row, row], out_specs=(row, row, wide, vec),
                  dims=("arbitrary",))(dmix, gm, bm, za, zb)


def _ple_final(h3, pgl, pe, gain, target):
    t, d = h3.shape
    tt = _tile(t, 256, 16)
    row = pl.BlockSpec((tt, d), lambda i: (i, 0))
    vec = pl.BlockSpec((1, d), lambda i: (0, 0))
    lvec = pl.BlockSpec((1, LANES), lambda i: (0, 0))

    def body(h_ref, pgl_ref, pe_ref, g_ref, t_ref, dh_ref, dsg_ref, dpe_ref, loss_ref, dg_ref):
        pg = _sigmoid(pgl_ref[...])
        pe_v = pe_ref[...]
        h4 = h_ref[...] + pg * pe_v
        r = lax.rsqrt(jnp.mean(h4 * h4, axis=-1, keepdims=True) + EPS)
        gv = g_ref[...]
        err = h4 * r * gv - t_ref[...]
        part_loss = 0.5 * jnp.sum(jnp.mean(err * err, axis=-1, keepdims=True), axis=0, keepdims=True)
        dy = err * (1.0 / d)
        part_g = jnp.sum(dy * h4 * r, axis=0, keepdims=True)
        dyg = dy * gv
        dh = r * dyg - h4 * (r * r * r) * jnp.mean(dyg * h4, axis=-1, keepdims=True)
        dh_ref[...] = dh
        dsg_ref[...] = (dh * pe_v * pg * (1.0 - pg)).astype(BF16)
        dpe_ref[...] = (dh * pg).astype(BF16)

        @pl.when(pl.program_id(0) == 0)
        def _():
            loss_ref[...] = jnp.broadcast_to(part_loss, (1, LANES))
            dg_ref[...] = part_g

        @pl.when(pl.program_id(0) > 0)
        def _():
            loss_ref[...] += jnp.broadcast_to(part_loss, (1, LANES))
            dg_ref[...] += part_g

    return _pcall(body, name="ple_final",
                  out_shape=(jax.ShapeDtypeStruct((t, d), F32), jax.ShapeDtypeStruct((t, d), BF16),
                             jax.ShapeDtypeStruct((t, d), BF16), jax.ShapeDtypeStruct((1, LANES), F32),
                             jax.ShapeDtypeStruct((1, d), F32)),
                  grid=(t // tt,), in_specs=[row, row, row, vec, row], out_specs=(row, row, row, lvec, vec),
                  dims=("arbitrary",))(h3, pgl, pe, gain, target)


def _adamw_math(w, g, m, v):
    m = ADAM_B1 * m + (1.0 - ADAM_B1) * g
    v = ADAM_B2 * v + (1.0 - ADAM_B2) * (g * g)
    m_hat = m / (1.0 - ADAM_B1 ** ADAM_STEP)
    v_hat = v / (1.0 - ADAM_B2 ** ADAM_STEP)
    delta = -ADAM_LR * (m_hat / (jnp.sqrt(v_hat) + ADAM_EPS) + ADAM_WD * w)
    return delta, m, v


def _adamw(parts, w, m, v, name):
    n, r, c = parts.shape
    tr = _tile(r, 256, 16)
    row = pl.BlockSpec((tr, c), lambda i: (i, 0))

    def body(p_ref, w_ref, m_ref, v_ref, g_ref, d_ref, mo_ref, vo_ref):
        g = p_ref[0].astype(F32)
        for s in range(1, n):
            g = g + p_ref[s].astype(F32)
        g_ref[...] = g
        d_ref[...], mo_ref[...], vo_ref[...] = _adamw_math(w_ref[...], g, m_ref[...], v_ref[...])

    return _pcall(body, name=name, out_shape=(jax.ShapeDtypeStruct((r, c), F32),) * 4, grid=(r // tr,),
                  in_specs=[pl.BlockSpec((n, tr, c), lambda i: (0, i, 0)), row, row, row], out_specs=(row,) * 4,
                  dims=("parallel",))(parts, w, m, v)


ANY = pl.BlockSpec(memory_space=pl.ANY)


def _all_gather(shards):
    n = len(shards)

    def body(*refs):
        x_refs, out_refs = refs[:n], refs[n:2 * n]
        send_sems, recv_sems, local_sems = refs[2 * n:]
        x, y, cc = lax.axis_index("x"), lax.axis_index("y"), lax.axis_index("c")
        me, sibling = (x, y, cc), (x, y, 1 - cc)
        chips = [(1 - x, y), (x, 1 - y), (1 - x, 1 - y)]

        def slot(a, px, py, pc):
            return out_refs[a].at[4 * px + 2 * py + pc]

        def copy(a, k, block, to, src=None):
            return pltpu.make_async_remote_copy(
                src_ref=slot(a, *block) if src is None else src, dst_ref=slot(a, *block),
                send_sem=send_sems.at[7 * a + k], recv_sem=recv_sems.at[7 * a + k], device_id=to, device_id_type=MESH)

        local, sent = [], []
        for a in range(n):
            local.append(pltpu.make_async_copy(x_refs[a], slot(a, *me), local_sems.at[a]))
            sent.append(copy(a, 0, me, sibling, src=x_refs[a]))
            sent += [copy(a, 1 + j, me, (*chip, cc), src=x_refs[a]) for j, chip in enumerate(chips)]
        for cp in local + sent:
            cp.start()
        for j, chip in enumerate(chips):
            for a in range(n):
                copy(a, 1 + j, (*chip, cc), me).wait_recv()
                sent.append(copy(a, 4 + j, (*chip, cc), sibling))
                sent[-1].start()
        for a in range(n):
            copy(a, 0, sibling, me).wait_recv()
            for j, chip in enumerate(chips):
                copy(a, 4 + j, (*chip, 1 - cc), me).wait_recv()
        for cp in sent:
            cp.wait_send()
        for cp in local:
            cp.wait()

    return pl.pallas_call(
        body, name="weights_all_gather",
        out_shape=tuple(jax.ShapeDtypeStruct((N_DEV,) + s.shape, s.dtype) for s in shards),
        in_specs=[ANY] * n, out_specs=(ANY,) * n,
        scratch_shapes=[pltpu.SemaphoreType.DMA((7 * n,)), pltpu.SemaphoreType.DMA((7 * n,)),
                        pltpu.SemaphoreType.DMA((n,))],
    )(*shards)


def _reduce_scatter_exchange(blocks):
    n = len(blocks)

    def body(*refs):
        g_refs, recv_refs = refs[:n], refs[n:2 * n]
        send_sems, recv_sems, local_sems = refs[2 * n:]
        x, y, cc = lax.axis_index("x"), lax.axis_index("y"), lax.axis_index("c")
        me = 4 * x + 2 * y + cc
        local, sent, landing = [], [], []
        for a in range(n):
            local.append(pltpu.make_async_copy(g_refs[a].at[me], recv_refs[a].at[me], local_sems.at[a]))
        for k in range(1, N_DEV):
            px, py, pc = x ^ (k >> 2), y ^ ((k >> 1) & 1), cc ^ (k & 1)
            peer = 4 * px + 2 * py + pc
            for a in range(n):
                sems = dict(send_sem=send_sems.at[7 * a + k - 1], recv_sem=recv_sems.at[7 * a + k - 1],
                            device_id=(px, py, pc), device_id_type=MESH)
                sent.append(pltpu.make_async_remote_copy(src_ref=g_refs[a].at[peer], dst_ref=recv_refs[a].at[me], **sems))
                landing.append(pltpu.make_async_remote_copy(src_ref=g_refs[a].at[me], dst_ref=recv_refs[a].at[peer], **sems))
        for cp in local + sent:
            cp.start()
        for cp in landing:
            cp.wait_recv()
        for cp in sent:
            cp.wait_send()
        for cp in local:
            cp.wait()

    return pl.pallas_call(
        body, name="grads_reduce_scatter_exchange",
        out_shape=tuple(jax.ShapeDtypeStruct(b.shape, b.dtype) for b in blocks),
        in_specs=[ANY] * n, out_specs=(ANY,) * n,
        scratch_shapes=[pltpu.SemaphoreType.DMA((7 * n,)), pltpu.SemaphoreType.DMA((7 * n,)),
                        pltpu.SemaphoreType.DMA((n,))],
    )(*blocks)


HBM = pl.BlockSpec(memory_space=pltpu.HBM)
SEM = pl.BlockSpec(memory_space=pltpu.SEMAPHORE)
DATAFLOW = pltpu.SideEffectType.DATAFLOW_SIDE_EFFECTING


def _peers():
    x, y, cc = lax.axis_index("x"), lax.axis_index("y"), lax.axis_index("c")
    out = []
    for k in range(1, N_DEV):
        px, py, pc = x ^ (k >> 2), y ^ ((k >> 1) & 1), cc ^ (k & 1)
        out.append((k, (px, py, pc), 4 * px + 2 * py + pc))
    return 4 * x + 2 * y + cc, out


def _scatter_start(blocks, after, name, gather=False):
    n = len(blocks)
    lands = [lax.empty((N_DEV,) + b.shape if gather else b.shape, b.dtype) for b in blocks]

    def body(*refs):
        g_refs, land_refs = refs[:n], refs[n:2 * n]
        send_sems, recv_sems, token = refs[2 * n + 1], refs[2 * n + 2], refs[-1]
        me, peers = _peers()
        for k, peer, slot in peers:
            for a in range(n):
                pltpu.make_async_remote_copy(
                    src_ref=g_refs[a] if gather else g_refs[a].at[slot], dst_ref=land_refs[a].at[me],
                    send_sem=send_sems.at[7 * a + k - 1],
                    recv_sem=recv_sems.at[7 * a + k - 1], device_id=peer, device_id_type=MESH).start()
        token[...] = jnp.zeros_like(token)

    thru = [pltpu.HBM(b.shape, b.dtype) for b in blocks]
    thru_lands = [pltpu.HBM(b.shape, b.dtype) for b in lands]
    return pl.pallas_call(
        body, name=name,
        out_shape=(pltpu.SemaphoreType.DMA((7 * n,)), pltpu.SemaphoreType.DMA((7 * n,)), *thru, *thru_lands,
                   jax.ShapeDtypeStruct((8, LANES), F32)),
        in_specs=[HBM] * (2 * n) + [pl.BlockSpec(memory_space=pl.ANY)],
        out_specs=(SEM, SEM, *[HBM] * (2 * n), pl.BlockSpec(memory_space=pltpu.VMEM)),
        input_output_aliases={i: 2 + i for i in range(2 * n)},
        compiler_params=pltpu.CompilerParams(has_side_effects=DATAFLOW),
    )(*[pltpu.with_memory_space_constraint(a, pltpu.HBM) for a in list(blocks) + lands], after)


def _scatter_wait(started, after, name, gather=False):
    send_sems, recv_sems, *rest = started
    n = (len(rest) - 1) // 2
    thru = rest[:2 * n]

    def body(*refs):
        g_refs, land_refs = refs[:n], refs[n:2 * n]
        send_sems, recv_sems = refs[2 * n], refs[2 * n + 1]
        me, peers = _peers()
        for k, peer, slot in peers:
            for a in range(n):
                copy = pltpu.make_async_remote_copy(
                    src_ref=g_refs[a] if gather else g_refs[a].at[slot], dst_ref=land_refs[a].at[slot],
                    send_sem=send_sems.at[7 * a + k - 1],
                    recv_sem=recv_sems.at[7 * a + k - 1], device_id=peer, device_id_type=MESH)
                copy.wait_send()
                copy.wait_recv()

    out = pl.pallas_call(
        body, name=name, out_shape=tuple(pltpu.HBM(a.shape, a.dtype) for a in thru),
        in_specs=[HBM] * (2 * n) + [SEM, SEM, pl.BlockSpec(memory_space=pl.ANY)], out_specs=tuple([HBM] * (2 * n)),
        input_output_aliases={i: i for i in range(2 * n)},
        compiler_params=pltpu.CompilerParams(has_side_effects=DATAFLOW),
    )(*thru, send_sems, recv_sems, after)
    return out[:n], out[n:]


BIG = (("w_ffn1_gate", "colT"), ("w_ffn1_up", "colT"), ("w_ffn1_down", "row"), ("w_ffn2_gate", "colT"),
       ("w_ffn2_up", "colT"), ("w_ffn2_down", "row"), ("w_in", "colT"), ("w_merge", "col"), ("w_ret_out", "col"),
       ("w_fox_out", "col"), ("w_out", "row"), ("w_ple", "col"), ("w_ple_gate", "row"))


def _shard_view(a, kind):
    a = a.reshape(a.shape[-2:])
    return a.T if kind == "colT" else a


def _unview(a, kind, shape):
    return (a.T if kind == "colT" else a).reshape(shape)


def _full_from_slots(g, kind):
    n, r, c = g.shape
    return g.transpose(1, 0, 2).reshape(r, n * c) if kind == "col" else g.reshape(n * r, c)


def _slots_from_full(f, kind):
    r, c = f.shape
    return f.reshape(r, N_DEV, c // N_DEV).transpose(1, 0, 2) if kind == "col" else f.reshape(N_DEV, r // N_DEV, c)


EARLY_GROUPS = (("w_ple_gate", "w_ple", "w_ffn2_down", "w_ffn2_gate", "w_ffn2_up", "w_out", "w_ret_out", "w_fox_out"),
                ("w_in", "w_merge"))


def _scatter_group(gw, names, after, name):
    kind = dict(BIG)
    return names, _scatter_start([_slots_from_full(gw[n], kind[n]) for n in names], after, name)


def _pad_heads(w):
    d = w.shape[1]
    return jnp.pad(w.reshape(FOX_HEADS, FOX_DIM, d), ((0, 0), (0, LANES - FOX_DIM), (0, 0))).reshape(FOX_TILES, d)


def _unpad_heads(w):
    d = w.shape[1]
    return w.reshape(FOX_HEADS, LANES, d)[:, :FOX_DIM].reshape(FOX_WIDTH, d)


def _deinterleave_rows(w):
    d = w.shape[1]
    return w.reshape(RET_HEADS, RET_DIM // 2, 2, d).transpose(0, 2, 1, 3).reshape(RET_WIDTH, d)


def _interleave_rows(w):
    d = w.shape[1]
    return w.reshape(RET_HEADS, 2, RET_DIM // 2, d).transpose(0, 2, 1, 3).reshape(RET_WIDTH, d)


def _pad_w_in(wt):
    d = wt.shape[1]
    rw, fw = RET_WIDTH, FOX_WIDTH
    fo = 4 * rw
    return jnp.concatenate([
        _deinterleave_rows(wt[:rw]), _deinterleave_rows(wt[rw:2 * rw]), wt[2 * rw:4 * rw],
        _pad_heads(wt[fo:fo + fw]), _pad_heads(wt[fo + fw:fo + 2 * fw]), _pad_heads(wt[fo + 2 * fw:fo + 3 * fw]),
        wt[fo + 3 * fw:], jnp.zeros((2 * LANES - FOX_HEADS, d), wt.dtype)], axis=0)


def _unpad_w_in(g):
    rw = RET_WIDTH
    f0 = 4 * rw
    return jnp.concatenate([
        _interleave_rows(g[:rw]), _interleave_rows(g[rw:2 * rw]), g[2 * rw:4 * rw],
        _unpad_heads(g[f0:f0 + FOX_TILES]), _unpad_heads(g[f0 + FOX_TILES:f0 + 2 * FOX_TILES]),
        _unpad_heads(g[f0 + 2 * FOX_TILES:f0 + 3 * FOX_TILES]),
        g[f0 + 3 * FOX_TILES:f0 + 3 * FOX_TILES + FOX_HEADS]], axis=0)


SMALL = ("ln_ffn1", "ln_mix", "b_forget", "b_merge", "ln_ffn2", "ln_ple", "ln_final")


def _small_rows(n):
    rows = -(-n // LANES)
    return -(-rows // 8) * 8


def _pack_small(vals, with_loss=None):
    parts = []
    for name in SMALL:
        v = vals[name].reshape(-1).astype(F32)
        rows = _small_rows(v.shape[0])
        parts.append(jnp.pad(v, (0, rows * LANES - v.shape[0])).reshape(rows, LANES))
    if with_loss is not None:
        parts.append(jnp.pad(with_loss.reshape(1, LANES), ((0, 7), (0, 0))))
    else:
        parts.append(jnp.zeros((8, LANES), F32))
    return jnp.concatenate(parts, axis=0)


def _unpack_small(packed, shapes):
    out, at = {}, 0
    for name in SMALL:
        n = int(np.prod(shapes[name]))
        rows = _small_rows(n)
        out[name] = packed[at:at + rows].reshape(-1)[:n].reshape(shapes[name])
        at += rows
    return out, packed[at, 0]


def _gather_finish(group, after, me):
    names, started, wait_name = group
    kind = dict(BIG)
    sent, landed = _scatter_wait(started, after, wait_name, gather=True)
    return {n: _full_from_slots(lax.dynamic_update_slice_in_dim(land, shard[None], me, 0), kind[n])
            for n, shard, land in zip(names, sent, landed)}


def _local_step(x, p, positions, target, w, small, me, entry_token, gathers):
    t, d = x.shape
    gain = lambda n: small[n].reshape(1, d)
    w = dict(w)
    bpad = jnp.pad(small["b_forget"].reshape(1, FOX_HEADS), ((0, 0), (0, LANES - FOX_HEADS)))
    bm = small["b_merge"].reshape(1, 2 * d)
    fox_blk = _tile(t, 512, 128)

    def ffn_fwd(n, tag):
        g = _mm([(n, w[f"w_{tag}_gate"], "nt")], F32, f"{tag}_gate", tn=1408)
        u = _mm([(n, w[f"w_{tag}_up"], "nt")], F32, f"{tag}_up", tn=1408)
        a = _swiglu_fwd(g, u, f"{tag}_swiglu")
        return g, u, a, _mm([(a, w[f"w_{tag}_down"], "nn")], F32, f"{tag}_down")

    n1 = _rms_fwd(x, gain("ln_ffn1") + entry_token, "rms_ffn1")
    g1, u1, a1, f1 = ffn_fwd(n1, "ffn1")
    h1, u = _rms_fwd(x, gain("ln_mix"), "rms_mix", f=f1)
    w.update(_gather_finish(gathers[0], f1, me))
    w_in_t = _pad_w_in(w["w_in"])
    gm = _mm([(u, w["w_merge"], "nn")], F32, "mixer_gates")
    pm = _mm([(u, w_in_t, "nt")], F32, "mixer_in", tn=1792)

    half = jnp.arange(RET_DIM // 2, dtype=F32) / (RET_DIM // 2)
    inv = 1.0 / (ROPE_BASE ** half)
    inv2 = jnp.concatenate([inv, inv]).reshape(1, RET_DIM)
    sign2 = jnp.concatenate([-jnp.ones((RET_DIM // 2,), F32), jnp.ones((RET_DIM // 2,), F32)]).reshape(1, RET_DIM)
    cos2, sin2 = _rope_tables(positions.reshape(t, 1), inv2, sign2)
    consts = _ret_consts()
    y_ret, y_raw, states = _ret_fwd(pm, cos2, sin2, consts)
    w.update(_gather_finish(gathers[1], y_raw, me))
    w_fox_pad = _pad_heads(w["w_fox_out"])
    za = _mm([(y_ret, w["w_ret_out"], "nn")], F32, "ret_out")

    qa, ka, va = _fox_prep(pm, bpad)
    o_fox, y_fox, qa_b = _fox_fwd(qa, ka, va, fox_blk)
    zb = _mm([(y_fox, w_fox_pad, "nn")], F32, "fox_out")

    mix = _merge_fwd(gm, bm, za, zb)
    mo = _mm([(mix, w["w_out"], "nn")], F32, "mix_out")
    h2, n2 = _rms_fwd(h1, gain("ln_ffn2"), "rms_ffn2", f=mo, scale=1.0)
    g2, u2, a2, f2 = ffn_fwd(n2, "ffn2")
    h3, n3 = _rms_fwd(h2, gain("ln_ple"), "rms_ple", f=f2)
    pgl = _mm([(n3, w["w_ple_gate"], "nn")], F32, "ple_gate")
    pb = p.astype(BF16)
    pe = _mm([(pb, w["w_ple"], "nn")], F32, "ple_embed")

    gw, gs = {}, {}
    dh4, dsg, dpe, loss, gs["ln_final"] = _ple_final(h3, pgl, pe, gain("ln_final"), target)
    gw["w_ple_gate"] = _mm([(n3, dsg, "tn")], BF16, "d_w_ple_gate", tn=256)
    gw["w_ple"] = _mm([(pb, dpe, "tn")], BF16, "d_w_ple", tn=256)
    dn3 = _mm([(dsg, w["w_ple_gate"], "nt")], F32, "d_n3")
    dh3, dh3_half, gs["ln_ple"] = _rms_bwd(dn3, h3, gain("ln_ple"), dh4, "rms_ple_bwd", 0.5)

    def ffn_bwd(dh_half, g, u_, a, n, tag, scatter_now=None):
        gw[f"w_{tag}_down"] = _mm([(a, dh_half, "tn")], BF16, f"d_w_{tag}_down", tm=1408, tn=256)
        start = scatter_now if scatter_now is not None else (lambda *_: None)
        token = start((f"w_{tag}_down",), dh_half, "c")
        da = _mm([(dh_half, w[f"w_{tag}_down"], "nt")], F32, f"d_a_{tag}", tn=1408, after=token)
        dg, du_ = _swiglu_bwd(da, g, u_, f"{tag}_swiglu_bwd")
        gw[f"w_{tag}_gate"] = _mm([(dg, n, "tn")], BF16, f"d_w_{tag}_gate", tm=1408, tn=256)
        token = start((f"w_{tag}_gate",), da, "d")
        gw[f"w_{tag}_up"] = _mm([(du_, n, "tn")], BF16, f"d_w_{tag}_up", tm=1408, tn=256, after=token)
        token = start((f"w_{tag}_up",), da, "e")
        return _mm([(dg, w[f"w_{tag}_gate"], "nn"), (du_, w[f"w_{tag}_up"], "nn")], F32, f"d_n_{tag}", tm=512,
                   after=token)

    dn2 = ffn_bwd(dh3_half, g2, u2, a2, n2, "ffn2")
    dh2, dh2_b, gs["ln_ffn2"] = _rms_bwd(dn2, h2, gain("ln_ffn2"), dh3, "rms_ffn2_bwd", 1.0)

    gw["w_out"] = _mm([(mix, dh2_b, "tn")], BF16, "d_w_out", tn=256)
    dmix = _mm([(dh2_b, w["w_out"], "nt")], F32, "d_mix")
    dza, dzb, dgm, gs["b_merge"] = _merge_bwd(dmix, gm, bm, za, zb)
    gw["w_ret_out"] = _mm([(y_ret, dza, "tn")], BF16, "d_w_ret_out", tn=256)
    gw["w_fox_out"] = _unpad_heads(_mm([(y_fox, dzb, "tn")], BF16, "d_w_fox_out", tn=256))
    dy_ret = _mm([(dza, w["w_ret_out"], "nt")], F32, "d_y_ret")
    do_fox = _mm([(dzb, w_fox_pad, "nt")], F32, "d_y_fox")

    pending = [_scatter_group(gw, EARLY_GROUPS[0], dy_ret, "grads_scatter_a_start")]
    token = pending[0][1][-1][0, 0]
    drq, drk, drv, drg = _ret_bwd(dy_ret, pm, cos2, sin2, y_raw, states, consts[:3] + (consts[3] + token,))

    dqa, dka, dva, ds_rows, ds_cols = _fox_bwd(qa_b, ka, va, do_fox, o_fox, fox_blk)
    dc = jnp.pad((ds_rows + ds_cols).reshape(FOX_HEADS, t).T, ((0, 0), (0, LANES - FOX_HEADS)))
    dff, db_forget = _fox_post(dc, pm, bpad)
    gs["b_forget"] = db_forget[:, :FOX_HEADS]

    dpm = jnp.concatenate([drq, drk, drv, drg, dqa, dka, dva, dff, jnp.zeros((t, LANES), BF16)], axis=1)
    gw["w_merge"] = _mm([(u, dgm, "tn")], BF16, "d_w_merge", tn=512)
    gw["w_in"] = _unpad_w_in(_mm([(dpm, u, "tn")], BF16, "d_w_in", tm=1792, tn=256))
    du = _mm([(dpm, w_in_t, "nn"), (dgm, w["w_merge"], "nt")], F32, "d_u", tm=1024, tn=512)
    pending.append(_scatter_group(gw, EARLY_GROUPS[1], du, "grads_scatter_b_start"))
    token = pending[1][1][-1][0:1, 0:1]
    dh1, dh1_half, gs["ln_mix"] = _rms_bwd(du, h1, gain("ln_mix") + token, dh2, "rms_mix_bwd", 0.5)

    def scatter_now(names, after, tag):
        pending.append(_scatter_group(gw, names, after, f"grads_scatter_{tag}_start"))
        return pending[-1][1][-1]

    dn1 = ffn_bwd(dh1_half, g1, u1, a1, n1, "ffn1", scatter_now=scatter_now)
    dx, _, gs["ln_ffn1"] = _rms_bwd(dn1, x, gain("ln_ffn1"), dh1, "rms_ffn1_bwd", 1.0)
    return loss, dx, gw, gs, pending


WEIGHTS = ("ln_ffn1", "w_ffn1_gate", "w_ffn1_up", "w_ffn1_down", "ln_mix", "w_in", "b_forget", "w_merge", "b_merge",
           "w_ret_out", "w_fox_out", "w_out", "ln_ffn2", "w_ffn2_gate", "w_ffn2_up", "w_ffn2_down", "ln_ple", "w_ple",
           "w_ple_gate", "ln_final")


def kernel(x, p, positions, ln_ffn1, w_ffn1_gate, w_ffn1_up, w_ffn1_down, ln_mix, w_in, b_forget, w_merge, b_merge, w_ret_out, w_fox_out, w_out, ln_ffn2, w_ffn2_gate, w_ffn2_up, w_ffn2_down, ln_ple, w_ple, w_ple_gate, ln_final, loss_target, m_ln_ffn1, m_w_ffn1_gate, m_w_ffn1_up, m_w_ffn1_down, m_ln_mix, m_w_in, m_b_forget, m_w_merge, m_b_merge, m_w_ret_out, m_w_fox_out, m_w_out, m_ln_ffn2, m_w_ffn2_gate, m_w_ffn2_up, m_w_ffn2_down, m_ln_ple, m_w_ple, m_w_ple_gate, m_ln_final, v_ln_ffn1, v_w_ffn1_gate, v_w_ffn1_up, v_w_ffn1_down, v_ln_mix, v_w_in, v_b_forget, v_w_merge, v_b_merge, v_w_ret_out, v_w_fox_out, v_w_out, v_ln_ffn2, v_w_ffn2_gate, v_w_ffn2_up, v_w_ffn2_down, v_ln_ple, v_w_ple, v_w_ple_gate, v_ln_final):
    args = dict(ln_ffn1=ln_ffn1, w_ffn1_gate=w_ffn1_gate, w_ffn1_up=w_ffn1_up, w_ffn1_down=w_ffn1_down, ln_mix=ln_mix, w_in=w_in, b_forget=b_forget, w_merge=w_merge, b_merge=b_merge, w_ret_out=w_ret_out, w_fox_out=w_fox_out, w_out=w_out, ln_ffn2=ln_ffn2, w_ffn2_gate=w_ffn2_gate, w_ffn2_up=w_ffn2_up, w_ffn2_down=w_ffn2_down, ln_ple=ln_ple, w_ple=w_ple, w_ple_gate=w_ple_gate, ln_final=ln_final)
    moms = dict(ln_ffn1=m_ln_ffn1, w_ffn1_gate=m_w_ffn1_gate, w_ffn1_up=m_w_ffn1_up, w_ffn1_down=m_w_ffn1_down, ln_mix=m_ln_mix, w_in=m_w_in, b_forget=m_b_forget, w_merge=m_w_merge, b_merge=m_b_merge, w_ret_out=m_w_ret_out, w_fox_out=m_w_fox_out, w_out=m_w_out, ln_ffn2=m_ln_ffn2, w_ffn2_gate=m_w_ffn2_gate, w_ffn2_up=m_w_ffn2_up, w_ffn2_down=m_w_ffn2_down, ln_ple=m_ln_ple, w_ple=m_w_ple, w_ple_gate=m_w_ple_gate, ln_final=m_ln_final)
    vars_ = dict(ln_ffn1=v_ln_ffn1, w_ffn1_gate=v_w_ffn1_gate, w_ffn1_up=v_w_ffn1_up, w_ffn1_down=v_w_ffn1_down, ln_mix=v_ln_mix, w_in=v_w_in, b_forget=v_b_forget, w_merge=v_w_merge, b_merge=v_b_merge, w_ret_out=v_w_ret_out, w_fox_out=v_w_fox_out, w_out=v_w_out, ln_ffn2=v_ln_ffn2, w_ffn2_gate=v_w_ffn2_gate, w_ffn2_up=v_w_ffn2_up, w_ffn2_down=v_w_ffn2_down, ln_ple=v_ln_ple, w_ple=v_w_ple, w_ple_gate=v_w_ple_gate, ln_final=v_ln_final)
    kinds = ("grad", "delta", "new_m", "new_v")

    me = 4 * lax.axis_index("x") + 2 * lax.axis_index("y") + lax.axis_index("c")
    kind_of = dict(BIG)
    shard = {n: _shard_view(args[n], kind).astype(BF16) for n, kind in BIG}
    first = ("w_ffn1_gate", "w_ffn1_up", "w_ffn1_down")
    second = ("w_in", "w_merge")
    third = tuple(n for n, _ in BIG if n not in first + second)
    gathered = _all_gather([shard[n] for n in first])
    w_full = {n: _full_from_slots(g, kind_of[n]) for n, g in zip(first, gathered)}
    started_2 = _scatter_start([shard[n] for n in second], gathered[0], "weights_gather_a_start", gather=True)
    started_3 = _scatter_start([shard[n] for n in third], started_2[-1], "weights_gather_b_start", gather=True)
    gathers = [(second, started_2, "weights_gather_a_wait"), (third, started_3, "weights_gather_b_wait")]

    small = {n: args[n] for n in SMALL}
    loss_part, dx, gw, gs, pending = _local_step(x[0], p[0, 0], positions[0], loss_target[0], w_full, small, me,
                                                 started_3[-1][0:1, 0:1], gathers)

    parts_of = {}
    for tag, (names, started) in zip("abcde", pending):
        sent, landed = _scatter_wait(started, dx, f"grads_scatter_{tag}_wait")
        for n, blk, land in zip(names, sent, landed):
            own = lax.dynamic_index_in_dim(blk, me, 0, keepdims=True)
            parts_of[n] = lax.dynamic_update_slice_in_dim(land, own, me, 0)
    late = [(n, kind) for n, kind in BIG if n not in parts_of]
    small_part = _pack_small(gs, with_loss=loss_part)
    blocks = [_slots_from_full(gw[n], kind) for n, kind in late]
    blocks.append(jnp.broadcast_to(small_part, (N_DEV,) + small_part.shape))
    recv = _reduce_scatter_exchange(blocks)
    parts_of.update({n: r for (n, _), r in zip(late, recv)})

    res = {}
    for n, kind in BIG:
        parts = parts_of[n]
        outs = _adamw(parts, _shard_view(args[n], kind), _shard_view(moms[n], kind), _shard_view(vars_[n], kind),
                      f"adamw_{n}")
        for what, o in zip(kinds, outs):
            res[(what, n)] = _unview(o, kind, args[n].shape)
    s_outs = _adamw(recv[-1], _pack_small(small), _pack_small({n: moms[n] for n in SMALL}),
                    _pack_small({n: vars_[n] for n in SMALL}), "adamw_small")
    for what, sm in zip(kinds, s_outs):
        svals, extra = _unpack_small(sm, {n: args[n].shape for n in SMALL})
        if what == "grad":
            loss = extra
        for n in SMALL:
            res[(what, n)] = svals[n]
    return (loss, dx[None], *[res[(what, n)] for what in kinds for n in WEIGHTS])
```

```python
import numpy as np
import jax
import jax.numpy as jnp
from jax import lax
from jax.experimental import pallas as pl
from jax.experimental.pallas import tpu as pltpu

F32 = jnp.float32
BF16 = jnp.bfloat16

N_DEV = 8
EPS = 1e-6
RET_HEADS = 4
RET_DIM = 128
RET_WIDTH = RET_HEADS * RET_DIM
FOX_HEADS = 8
FOX_DIM = 64
FOX_WIDTH = FOX_HEADS * FOX_DIM
CHUNK = 128
ROPE_BASE = 10000.0
LANES = 128
FOX_TILES = FOX_HEADS * LANES
IN_COLS = 4 * RET_WIDTH + 3 * FOX_WIDTH + FOX_HEADS
IN_PAD = 4 * RET_WIDTH + 3 * FOX_TILES + 2 * LANES
TILE_RQ, TILE_RK, TILE_RV, TILE_RG = 0, 4, 8, 12
TILE_FQ, TILE_FK, TILE_FV, TILE_FF = 16, 24, 32, 40
NEG = -1e30

ADAM_LR = 0.001
ADAM_B1 = 0.9
ADAM_B2 = 0.999
ADAM_EPS = 1e-08
ADAM_WD = 0.01
ADAM_STEP = 10

VMEM_LIMIT_BYTES = 56 * 1024 * 1024

MESH = pl.DeviceIdType.MESH


def _tile(dim, pref, mult):
    if dim <= pref:
        return dim
    t = (pref // mult) * mult
    while t >= mult:
        if dim % t == 0:
            return t
        t -= mult
    return dim


def _params(dims):
    return pltpu.CompilerParams(dimension_semantics=dims, vmem_limit_bytes=VMEM_LIMIT_BYTES)


def _pcall(body, *, name, out_shape, grid, in_specs, out_specs, scratch_shapes=(), dims=None):
    return pl.pallas_call(body, name=name, out_shape=out_shape, grid=grid, in_specs=in_specs, out_specs=out_specs,
                          scratch_shapes=list(scratch_shapes), compiler_params=_params(dims))


def _dot(a, b, ca, cb):
    return lax.dot_general(a, b, (((ca,), (cb,)), ((), ())), preferred_element_type=F32)


def _sigmoid(x):
    return 1.0 / (1.0 + jnp.exp(-x))


def _mm(pairs, out_dtype, name, tm=1024, tn=1024, after=None):
    dims = []
    for a, b, mode in pairs:
        m, k = (a.shape[1], a.shape[0]) if mode == "tn" else a.shape
        n, k2 = b.shape if mode == "nt" else (b.shape[1], b.shape[0])
        assert k == k2, (name, a.shape, b.shape, mode)
        dims.append((m, n))
    assert all(d == dims[0] for d in dims), (name, dims)
    m, n = dims[0]
    tm = _tile(m, tm, 128 if any(mode == "tn" for _, _, mode in pairs) else 16)
    tn = _tile(n, tn, 128)
    in_specs, contract, operands = [], [], []
    for a, b, mode in pairs:
        k = a.shape[0] if mode == "tn" else a.shape[1]
        in_specs.append(pl.BlockSpec((k, tm), lambda i, j: (0, i)) if mode == "tn" else
                        pl.BlockSpec((tm, k), lambda i, j: (i, 0)))
        in_specs.append(pl.BlockSpec((tn, k), lambda i, j: (j, 0)) if mode == "nt" else
                        pl.BlockSpec((k, tn), lambda i, j: (0, j)))
        contract.append((0 if mode == "tn" else 1, 1 if mode == "nt" else 0))
        operands += [a, b]
    if after is not None:
        in_specs.append(pl.BlockSpec(memory_space=pl.ANY))
        operands.append(after)

    def body(*refs):
        o_ref = refs[-1]
        acc = None
        for p, (ca, cb) in enumerate(contract):
            part = _dot(refs[2 * p][...], refs[2 * p + 1][...], ca, cb)
            acc = part if acc is None else acc + part
        o_ref[...] = acc.astype(out_dtype)

    return _pcall(body, name=name, out_shape=jax.ShapeDtypeStruct((m, n), out_dtype), grid=(m // tm, n // tn),
                  in_specs=in_specs, out_specs=pl.BlockSpec((tm, tn), lambda i, j: (i, j)),
                  dims=("parallel", "parallel"))(*operands)


def _rms_fwd(h, gain, name, f=None, scale=0.5):
    t, d = h.shape
    tt = _tile(t, 512, 16)
    row = pl.BlockSpec((tt, d), lambda i: (i, 0))
    vec = pl.BlockSpec((1, d), lambda i: (0, 0))

    def norm(hv, g_ref, n_ref):
        r = lax.rsqrt(jnp.mean(hv * hv, axis=-1, keepdims=True) + EPS)
        n_ref[...] = (hv * r * g_ref[...]).astype(BF16)

    if f is None:

        def body(h_ref, g_ref, n_ref):
            norm(h_ref[...], g_ref, n_ref)

        return _pcall(body, name=name, out_shape=jax.ShapeDtypeStruct((t, d), BF16), grid=(t // tt,),
                      in_specs=[row, vec], out_specs=row, dims=("parallel",))(h, gain)

    def body(h_ref, f_ref, g_ref, hn_ref, n_ref):
        hv = h_ref[...] + scale * f_ref[...]
        hn_ref[...] = hv
        norm(hv, g_ref, n_ref)

    return _pcall(body, name=name,
                  out_shape=(jax.ShapeDtypeStruct((t, d), F32), jax.ShapeDtypeStruct((t, d), BF16)),
                  grid=(t // tt,), in_specs=[row, row, vec], out_specs=(row, row), dims=("parallel",))(h, f, gain)


def _rms_bwd(dn, h, gain, dh_in, name, out_scale):
    t, d = h.shape
    tt = _tile(t, 512, 16)
    row = pl.BlockSpec((tt, d), lambda i: (i, 0))
    vec = pl.BlockSpec((1, d), lambda i: (0, 0))

    def body(dn_ref, h_ref, g_ref, dhin_ref, dh_ref, dhb_ref, dg_ref):
        hv = h_ref[...]
        dnv = dn_ref[...].astype(F32)
        r = lax.rsqrt(jnp.mean(hv * hv, axis=-1, keepdims=True) + EPS)
        dng = dnv * g_ref[...]
        dh = dhin_ref[...] + r * dng - hv * (r * r * r) * jnp.mean(dng * hv, axis=-1, keepdims=True)
        dh_ref[...] = dh
        dhb_ref[...] = (out_scale * dh).astype(BF16)
        part = jnp.sum(dnv * hv * r, axis=0, keepdims=True)

        @pl.when(pl.program_id(0) == 0)
        def _():
            dg_ref[...] = part

        @pl.when(pl.program_id(0) > 0)
        def _():
            dg_ref[...] += part

    return _pcall(body, name=name,
                  out_shape=(jax.ShapeDtypeStruct((t, d), F32), jax.ShapeDtypeStruct((t, d), BF16),
                             jax.ShapeDtypeStruct((1, d), F32)),
                  grid=(t // tt,), in_specs=[row, row, vec, row], out_specs=(row, row, vec),
                  dims=("arbitrary",))(dn, h, gain, dh_in)


def _swiglu_fwd(g, u, name):
    t, f = g.shape
    tt = _tile(t, 256, 16)
    row = pl.BlockSpec((tt, f), lambda i: (i, 0))

    def body(g_ref, u_ref, a_ref):
        gv = g_ref[...].astype(F32)
        a_ref[...] = (gv * _sigmoid(gv) * u_ref[...].astype(F32)).astype(BF16)

    return _pcall(body, name=name, out_shape=jax.ShapeDtypeStruct((t, f), BF16), grid=(t // tt,),
                  in_specs=[row, row], out_specs=row, dims=("parallel",))(g, u)


def _swiglu_bwd(da, g, u, name):
    t, f = g.shape
    tt = _tile(t, 256, 16)
    row = pl.BlockSpec((tt, f), lambda i: (i, 0))

    def body(da_ref, g_ref, u_ref, dg_ref, du_ref):
        gv = g_ref[...].astype(F32)
        dav = da_ref[...].astype(F32)
        sg = _sigmoid(gv)
        dg_ref[...] = (dav * u_ref[...].astype(F32) * (sg * (1.0 + gv * (1.0 - sg)))).astype(BF16)
        du_ref[...] = (dav * (gv * sg)).astype(BF16)

    return _pcall(body, name=name, out_shape=(jax.ShapeDtypeStruct((t, f), BF16),) * 2, grid=(t // tt,),
                  in_specs=[row, row, row], out_specs=(row, row), dims=("parallel",))(da, g, u)


def _rope_tables(pos_col, inv2, sign2):
    t = pos_col.shape[0]

    def body(p_ref, inv_ref, sg_ref, c_ref, s_ref):
        ang = p_ref[...].astype(F32) * inv_ref[...]
        c_ref[...] = jnp.cos(ang)
        s_ref[...] = jnp.sin(ang) * sg_ref[...]

    full = lambda shape: pl.BlockSpec(shape, lambda i: (0, 0))
    return _pcall(body, name="rope_tables", out_shape=(jax.ShapeDtypeStruct((t, RET_DIM), F32),) * 2, grid=(1,),
                  in_specs=[full((t, 1)), full((1, RET_DIM)), full((1, RET_DIM))],
                  out_specs=(full((t, RET_DIM)),) * 2, dims=("arbitrary",))(pos_col, inv2, sign2)


def _rot(x, c, s):
    return x * c + pltpu.roll(x, RET_DIM // 2, 1) * s


def _rot_t(g, c, s):
    return g * c + pltpu.roll(g * s, RET_DIM // 2, 1)


def _ret_consts():
    hh = np.arange(RET_HEADS, dtype=np.float32)
    log_gamma = np.log1p(-np.exp2(-5.0 - hh)).astype(np.float32)
    idx = np.arange(CHUNK, dtype=np.float32)
    diff = idx[:, None] - idx[None, :]
    dmask = np.where(diff >= 0, np.exp(log_gamma[:, None, None] * np.maximum(diff, 0.0)), 0.0).astype(np.float32)
    kdec = np.exp(log_gamma[:, None] * (CHUNK - 1 - idx)).astype(np.float32)
    qdec = np.exp(log_gamma[:, None] * (idx + 1.0)).astype(np.float32)
    cdec = np.exp(log_gamma * CHUNK).astype(np.float32)
    bc = lambda v: np.ascontiguousarray(np.broadcast_to(v[:, :, None], (RET_HEADS, CHUNK, RET_DIM)))
    cd = np.ascontiguousarray(np.broadcast_to(cdec[:, None, None], (RET_HEADS, 8, RET_DIM)))
    return jnp.asarray(dmask), jnp.asarray(bc(qdec)), jnp.asarray(bc(kdec)), jnp.asarray(cd)


def _ret_fwd(pm, cos2, sin2, consts):
    t = pm.shape[0]
    n_chunks = t // CHUNK
    dmask, qdec, kdec, cd = consts
    scale = RET_DIM ** -0.5

    def col(c0):
        return pl.BlockSpec((CHUNK, RET_DIM), lambda h, n: (n, c0 + h))

    tab = pl.BlockSpec((CHUNK, RET_DIM), lambda h, n: (n, 0))
    head3 = lambda r: pl.BlockSpec((None, r, RET_DIM), lambda h, n: (h, 0, 0))

    def body(q_ref, k_ref, v_ref, g_ref, c_ref, s_ref, dm_ref, qd_ref, kd_ref, cd_ref, y_ref, raw_ref, st_ref, s_acc):
        @pl.when(pl.program_id(1) == 0)
        def _():
            s_acc[...] = jnp.zeros_like(s_acc)

        c, s = c_ref[...], s_ref[...]
        q = _rot(q_ref[...], c, s)
        k = _rot(k_ref[...], c, s) * scale
        vb = v_ref[...].astype(BF16)
        g = g_ref[...]
        s_in = s_acc[...]
        st_ref[...] = s_in
        a = _dot(q.astype(BF16), k.astype(BF16), 1, 1) * dm_ref[...]
        y = _dot(a.astype(BF16), vb, 1, 0) + _dot((q * qd_ref[...]).astype(BF16), s_in.astype(BF16), 1, 0)
        s_acc[...] = cd_ref[0:1, :] * s_in + _dot((k * kd_ref[...]).astype(BF16), vb, 0, 0)
        raw_ref[...] = y
        mu = jnp.mean(y, axis=-1, keepdims=True)
        yc = y - mu
        rs = lax.rsqrt(jnp.mean(yc * yc, axis=-1, keepdims=True) + EPS)
        y_ref[...] = (yc * rs * (g * _sigmoid(g))).astype(BF16)

    out_blk = pl.BlockSpec((CHUNK, RET_DIM), lambda h, n: (n, h))
    return _pcall(
        body, name="retention_fwd",
        out_shape=(jax.ShapeDtypeStruct((t, RET_WIDTH), BF16), jax.ShapeDtypeStruct((t, RET_WIDTH), F32),
                   jax.ShapeDtypeStruct((RET_HEADS, n_chunks, RET_DIM, RET_DIM), F32)),
        grid=(RET_HEADS, n_chunks),
        in_specs=[col(TILE_RQ), col(TILE_RK), col(TILE_RV), col(TILE_RG), tab, tab,
                  pl.BlockSpec((None, CHUNK, CHUNK), lambda h, n: (h, 0, 0)), head3(CHUNK), head3(CHUNK), head3(8)],
        out_specs=(out_blk, out_blk, pl.BlockSpec((None, None, RET_DIM, RET_DIM), lambda h, n: (h, n, 0, 0))),
        scratch_shapes=[pltpu.VMEM((RET_DIM, RET_DIM), F32)],
        dims=("parallel", "arbitrary"),
    )(pm, pm, pm, pm, cos2, sin2, dmask, qdec, kdec, cd)


def _ret_bwd(dy, pm, cos2, sin2, raw, states, consts):
    t = pm.shape[0]
    n_chunks = t // CHUNK
    dmask, qdec, kdec, cd = consts
    scale = RET_DIM ** -0.5
    rev = lambda n: n_chunks - 1 - n

    def col(c0):
        return pl.BlockSpec((CHUNK, RET_DIM), lambda h, n: (rev(n), c0 + h))

    tab = pl.BlockSpec((CHUNK, RET_DIM), lambda h, n: (rev(n), 0))
    blk = pl.BlockSpec((CHUNK, RET_DIM), lambda h, n: (rev(n), h))
    head3 = lambda r: pl.BlockSpec((None, r, RET_DIM), lambda h, n: (h, 0, 0))

    def body(dy_ref, q_ref, k_ref, v_ref, g_ref, c_ref, s_ref, raw_ref, st_ref, dm_ref, qd_ref, kd_ref, cd_ref,
             dq_ref, dk_ref, dv_ref, dg_ref, ds_acc):
        @pl.when(pl.program_id(1) == 0)
        def _():
            ds_acc[...] = jnp.zeros_like(ds_acc)

        c, s = c_ref[...], s_ref[...]
        q = _rot(q_ref[...], c, s)
        k = _rot(k_ref[...], c, s) * scale
        qb, kb, vb = q.astype(BF16), k.astype(BF16), v_ref[...].astype(BF16)
        g = g_ref[...]
        dm, qd, kd = dm_ref[...], qd_ref[...], kd_ref[...]
        y = raw_ref[...]
        mu = jnp.mean(y, axis=-1, keepdims=True)
        yc = y - mu
        rs = lax.rsqrt(jnp.mean(yc * yc, axis=-1, keepdims=True) + EPS)
        yn = yc * rs
        sg = _sigmoid(g)
        dyo = dy_ref[...]
        dg_ref[...] = (dyo * yn * (sg * (1.0 + g * (1.0 - sg)))).astype(BF16)
        dyn = dyo * (g * sg)
        dyr = rs * (dyn - jnp.mean(dyn, axis=-1, keepdims=True) - yn * jnp.mean(dyn * yn, axis=-1, keepdims=True))
        dyb = dyr.astype(BF16)
        s_in = st_ref[...].astype(BF16)
        ds_out = ds_acc[...]
        dsb = ds_out.astype(BF16)
        a = _dot(qb, kb, 1, 1) * dm
        da = (_dot(dyb, vb, 1, 1) * dm).astype(BF16)
        kdb = (k * kd).astype(BF16)
        qdb = (q * qd).astype(BF16)
        dv_ref[...] = (_dot(a.astype(BF16), dyb, 0, 0) + _dot(kdb, dsb, 1, 0)).astype(BF16)
        dqh = _dot(da, kb, 1, 0) + _dot(dyb, s_in, 1, 1) * qd
        dkh = _dot(da, qb, 0, 0) + _dot(vb, dsb, 1, 1) * kd
        ds_acc[...] = cd_ref[0:1, :] * ds_out + _dot(qdb, dyb, 0, 0)
        dq_ref[...] = _rot_t(dqh, c, s).astype(BF16)
        dk_ref[...] = (_rot_t(dkh, c, s) * scale).astype(BF16)

    return _pcall(
        body, name="retention_bwd",
        out_shape=(jax.ShapeDtypeStruct((t, RET_WIDTH), BF16),) * 4,
        grid=(RET_HEADS, n_chunks),
        in_specs=[blk, col(TILE_RQ), col(TILE_RK), col(TILE_RV), col(TILE_RG), tab, tab, blk,
                  pl.BlockSpec((None, None, RET_DIM, RET_DIM), lambda h, n: (h, rev(n), 0, 0)),
                  pl.BlockSpec((None, CHUNK, CHUNK), lambda h, n: (h, 0, 0)), head3(CHUNK), head3(CHUNK), head3(8)],
        out_specs=(blk,) * 4,
        scratch_shapes=[pltpu.VMEM((RET_DIM, RET_DIM), F32)],
        dims=("parallel", "arbitrary"),
    )(dy, pm, pm, pm, pm, cos2, sin2, raw, states, dmask, qdec, kdec, cd)


FOX_C_LANE = FOX_DIM
FOX_NEGC_LANE = FOX_DIM + 3
FOX_LSE_LANE = FOX_DIM + 6
FOX_L_LANE = FOX_C_LANE
FOX_ROWSUM_LANE = FOX_C_LANE
FOX_COLSUM_LANE = FOX_NEGC_LANE


def _split3(x):
    hi = x.astype(BF16)
    r1 = x - hi.astype(F32)
    mid = r1.astype(BF16)
    lo = (r1 - mid.astype(F32)).astype(BF16)
    return hi, mid, lo


def _tri_dot(tri, x):
    hi, mid, lo = _split3(x)
    return _dot(tri, lo, 1, 0) + _dot(tri, mid, 1, 0) + _dot(tri, hi, 1, 0)


def _log_sigmoid(z):
    return jnp.minimum(z, 0.0) - jnp.log1p(jnp.exp(-jnp.abs(z)))


def _fox_consts():
    place = np.zeros((2, 3, LANES, FOX_TILES), np.float32)
    ones = np.zeros((3, 1, FOX_TILES), np.float32)
    for h in range(FOX_HEADS):
        for part in range(3):
            place[0, part, h, LANES * h + FOX_C_LANE + part] = 1.0
            place[1, part, h, LANES * h + FOX_NEGC_LANE + part] = -1.0
            ones[0, 0, LANES * h + FOX_NEGC_LANE + part] = 1.0
            ones[1, 0, LANES * h + FOX_C_LANE + part] = 1.0
            ones[1, 0, LANES * h + FOX_LSE_LANE + part] = 1.0
            ones[2, 0, LANES * h + FOX_C_LANE + part] = 1.0
    return jnp.asarray(place, BF16), jnp.asarray(ones, F32)


def _fox_prep(pm, bpad):
    t = pm.shape[0]
    tt = _tile(t, 512, LANES)
    place, ones = _fox_consts()
    wide = lambda c0: pl.BlockSpec((tt, FOX_TILES), lambda i: (i, c0 // FOX_HEADS))
    const = lambda a: pl.BlockSpec(a.shape, lambda i: (0,) * a.ndim)

    def body(q_ref, k_ref, v_ref, ff_ref, b_ref, pl_ref, on_ref, qa_ref, ka_ref, va_ref, carry_s):
        @pl.when(pl.program_id(0) == 0)
        def _():
            carry_s[...] = jnp.zeros_like(carry_s)

        r = lax.broadcasted_iota(jnp.int32, (LANES, LANES), 0)
        cc = lax.broadcasted_iota(jnp.int32, (LANES, LANES), 1)
        tri = jnp.where(cc <= r, 1.0, 0.0).astype(BF16)
        bias = b_ref[...]
        for sub in range(tt // LANES):
            rows = pl.ds(sub * LANES, LANES)
            cs = _tri_dot(tri, _log_sigmoid(ff_ref[rows, :] + bias)) + carry_s[...]
            carry_s[...] = cs[LANES - 1:LANES, :]
            parts = _split3(cs)
            eq = sum(_dot(part, pl_ref[0, i], 1, 0) for i, part in enumerate(parts))
            ek = sum(_dot(part, pl_ref[1, i], 1, 0) for i, part in enumerate(parts))
            qa_ref[rows, :] = (q_ref[rows, :] * FOX_DIM ** -0.5 + eq + on_ref[0]).astype(BF16)
            ka_ref[rows, :] = (k_ref[rows, :] + ek + on_ref[1]).astype(BF16)
            va_ref[rows, :] = (v_ref[rows, :] + on_ref[2]).astype(BF16)

    out = pl.BlockSpec((tt, FOX_TILES), lambda i: (i, 0))
    return _pcall(body, name="fox_prep", out_shape=(jax.ShapeDtypeStruct((t, FOX_TILES), BF16),) * 3, grid=(t // tt,),
                  in_specs=[wide(TILE_FQ), wide(TILE_FK), wide(TILE_FV), pl.BlockSpec((tt, LANES), lambda i: (i, TILE_FF)),
                            pl.BlockSpec((1, LANES), lambda i: (0, 0)), const(place), const(ones)],
                  out_specs=(out,) * 3, scratch_shapes=[pltpu.VMEM((1, LANES), F32)],
                  dims=("arbitrary",))(pm, pm, pm, pm, bpad, place, ones)


def _fox_post(dc, pm, bpad):
    t = pm.shape[0]
    nb = t // LANES

    def body(dc_ref, ff_ref, b_ref, d_ref, db_ref):
        r = lax.broadcasted_iota(jnp.int32, (LANES, LANES), 0)
        cc = lax.broadcasted_iota(jnp.int32, (LANES, LANES), 1)
        tri = jnp.where(cc >= r, 1.0, 0.0).astype(BF16)
        bias = b_ref[...]

        def step(i, carry):
            tail, acc = carry
            rows = pl.ds(pl.multiple_of((nb - 1 - i) * LANES, LANES), LANES)
            cs = _tri_dot(tri, dc_ref[rows, :]) + tail
            dff = cs * _sigmoid(-(ff_ref[rows, :] + bias))
            d_ref[rows, :] = dff.astype(BF16)
            return cs[0:1, :], acc + jnp.sum(dff, axis=0, keepdims=True)

        zero = jnp.zeros((1, LANES), F32)
        _, acc = lax.fori_loop(0, nb, step, (zero, zero))
        db_ref[...] = acc

    return _pcall(body, name="fox_forget_bwd",
                  out_shape=(jax.ShapeDtypeStruct((t, LANES), BF16), jax.ShapeDtypeStruct((1, LANES), F32)), grid=(1,),
                  in_specs=[pl.BlockSpec((t, LANES), lambda i: (0, 0)), pl.BlockSpec((t, LANES), lambda i: (0, TILE_FF)),
                            pl.BlockSpec((1, LANES), lambda i: (0, 0))],
                  out_specs=(pl.BlockSpec((t, LANES), lambda i: (0, 0)), pl.BlockSpec((1, LANES), lambda i: (0, 0))),
                  dims=("arbitrary",))(dc, pm, bpad)


def _tri_tables(nb, q_major):
    pairs = [(i, j) for i in range(nb) for j in range(i + 1)] if q_major else \
            [(i, j) for j in range(nb) for i in range(j, nb)]
    return jnp.asarray([a for a, _ in pairs], jnp.int32), jnp.asarray([b for _, b in pairs], jnp.int32)


def _causal(s):
    n = s.shape[0]
    row = lax.broadcasted_iota(jnp.int32, (n, n), 0)
    col = lax.broadcasted_iota(jnp.int32, (n, n), 1)
    return jnp.where(col <= row, s, NEG)


def _lane_col(x, lane):
    sel = lax.broadcasted_iota(jnp.int32, x.shape, 1) == lane
    return jnp.sum(jnp.where(sel, x, 0.0), axis=1, keepdims=True)


def _fox_fwd(qa, ka, va, blk):
    t = qa.shape[0]
    nb = t // blk
    qi, kj = _tri_tables(nb, True)
    q_spec = pl.BlockSpec((blk, LANES), lambda h, s, qi_r, kj_r: (qi_r[s], h))
    k_spec = pl.BlockSpec((blk, LANES), lambda h, s, qi_r, kj_r: (kj_r[s], h))

    def body(qi_r, kj_r, q_ref, k_ref, v_ref, o_ref, ob_ref, qb_ref, m_s, acc_s):
        s_id = pl.program_id(1)
        i, j = qi_r[s_id], kj_r[s_id]

        @pl.when(j == 0)
        def _():
            m_s[...] = jnp.full_like(m_s, NEG)
            acc_s[...] = jnp.zeros_like(acc_s)

        def tile(diagonal):
            s = _dot(q_ref[...], k_ref[...], 1, 1)
            if diagonal:
                s = _causal(s)
            m_old = m_s[...]
            m_new = jnp.maximum(m_old, jnp.max(s, axis=1, keepdims=True))
            p = jnp.exp(s - jnp.tile(m_new, (1, blk // LANES)))
            acc_s[...] = jnp.exp(m_old - m_new) * acc_s[...] + _dot(p.astype(BF16), v_ref[...], 1, 0)
            m_s[...] = m_new

        @pl.when(j < i)
        def _():
            tile(False)

        @pl.when(j == i)
        def _():
            tile(True)
            acc = acc_s[...]
            l = _lane_col(acc, FOX_L_LANE)
            o = acc / l
            o_ref[...] = o
            ob_ref[...] = o.astype(BF16)
            hi, mid, lo = _split3(-(m_s[:, 0:1] + jnp.log(l)))
            lane = lax.broadcasted_iota(jnp.int32, acc.shape, 1)
            qb_ref[...] = jnp.where(lane == FOX_LSE_LANE, hi,
                                    jnp.where(lane == FOX_LSE_LANE + 1, mid,
                                              jnp.where(lane == FOX_LSE_LANE + 2, lo, q_ref[...])))

    wide = (t, FOX_TILES)
    return pl.pallas_call(
        body, name="fox_fwd",
        out_shape=(jax.ShapeDtypeStruct(wide, F32), jax.ShapeDtypeStruct(wide, BF16), jax.ShapeDtypeStruct(wide, BF16)),
        grid_spec=pltpu.PrefetchScalarGridSpec(
            num_scalar_prefetch=2, grid=(FOX_HEADS, qi.shape[0]), in_specs=[q_spec, k_spec, k_spec],
            out_specs=(q_spec,) * 3,
            scratch_shapes=[pltpu.VMEM((blk, LANES), F32), pltpu.VMEM((blk, LANES), F32)]),
        compiler_params=_params(("parallel", "arbitrary")),
    )(qi, kj, qa, ka, va)


def _fox_bwd(qa, ka, va, do, o, blk):
    t = qa.shape[0]
    nb = t // blk
    qi, kj = _tri_tables(nb, False)
    q_spec = pl.BlockSpec((blk, LANES), lambda h, s, qi_r, kj_r: (qi_r[s], h))
    k_spec = pl.BlockSpec((blk, LANES), lambda h, s, qi_r, kj_r: (kj_r[s], h))
    head_spec = pl.BlockSpec((t, LANES), lambda h, s, qi_r, kj_r: (0, h))
    head_col = pl.BlockSpec((None, t, 1), lambda h, s, qi_r, kj_r: (h, 0, 0))
    k_col = pl.BlockSpec((None, blk, 1), lambda h, s, qi_r, kj_r: (h, kj_r[s], 0))
    first_spec = pl.BlockSpec((blk, LANES), lambda h, s, qi_r, kj_r: (jnp.where(kj_r[s] == 0, qi_r[s], nb - 1), h))
    n_steps = int(qi.shape[0])

    def body(qi_r, kj_r, q_ref, k_ref, v_ref, do_ref, o_ref, dq_ref, dk_ref, dv_ref, rs_ref, cs_ref,
             doa_s, dq_s, dk_s, dv_s):
        s_id = pl.program_id(1)
        i, j = qi_r[s_id], kj_r[s_id]
        rows = pl.ds(pl.multiple_of(i * blk, blk), blk)

        @pl.when(j == 0)
        def _():
            dof = do_ref[...]
            hi, mid, lo = _split3(-jnp.sum(dof * o_ref[...], axis=1, keepdims=True))
            lane = lax.broadcasted_iota(jnp.int32, dof.shape, 1)
            doa = jnp.where(lane == FOX_C_LANE, hi.astype(F32),
                            jnp.where(lane == FOX_C_LANE + 1, mid.astype(F32),
                                      jnp.where(lane == FOX_C_LANE + 2, lo.astype(F32), dof)))
            doa_s[rows, :] = doa.astype(BF16)
            dq_s[rows, :] = jnp.zeros((blk, LANES), F32)

        @pl.when(i == j)
        def _():
            dk_s[...] = jnp.zeros_like(dk_s)
            dv_s[...] = jnp.zeros_like(dv_s)

        def tile(diagonal):
            q, k = q_ref[...], k_ref[...]
            s = _dot(q, k, 1, 1)
            if diagonal:
                s = _causal(s)
            p = jnp.exp(s)
            doa = doa_s[rows, :]
            ds = (p * _dot(doa, v_ref[...], 1, 1)).astype(BF16)
            dv_s[...] += _dot(p.astype(BF16), doa, 0, 0)
            dk_s[...] += _dot(ds, q, 0, 0)
            dq_s[rows, :] += _dot(ds, k, 1, 0)

        @pl.when(i > j)
        def _():
            tile(False)

        @pl.when(i == j)
        def _():
            tile(True)

        @pl.when(i == nb - 1)
        def _():
            dk = dk_s[...]
            dk_ref[...] = dk.astype(BF16)
            dv_ref[...] = dv_s[...].astype(BF16)
            cs_ref[...] = -_lane_col(dk, FOX_COLSUM_LANE)

        @pl.when(s_id == n_steps - 1)
        def _():
            dq = dq_s[...]
            dq_ref[...] = (dq * FOX_DIM ** -0.5).astype(BF16)
            rs_ref[...] = _lane_col(dq, FOX_ROWSUM_LANE)

    wide = jax.ShapeDtypeStruct((t, FOX_TILES), BF16)
    cols = jax.ShapeDtypeStruct((FOX_HEADS, t, 1), F32)
    return pl.pallas_call(
        body, name="fox_bwd", out_shape=(wide, wide, wide, cols, cols),
        grid_spec=pltpu.PrefetchScalarGridSpec(
            num_scalar_prefetch=2, grid=(FOX_HEADS, n_steps),
            in_specs=[q_spec, k_spec, k_spec, first_spec, first_spec],
            out_specs=(head_spec, k_spec, k_spec, head_col, k_col),
            scratch_shapes=[pltpu.VMEM((t, LANES), BF16), pltpu.VMEM((t, LANES), F32), pltpu.VMEM((blk, LANES), F32),
                            pltpu.VMEM((blk, LANES), F32)]),
        compiler_params=_params(("parallel", "arbitrary")),
    )(qi, kj, qa, ka, va, do, o)


def _merge_fwd(gm, bm, za, zb):
    t, d = za.shape
    tt = _tile(t, 256, 16)
    row = pl.BlockSpec((tt, d), lambda i: (i, 0))

    def body(gm_ref, b_ref, za_ref, zb_ref, o_ref):
        ga = _sigmoid(gm_ref[:, :d] + b_ref[:, :d])
        gb = _sigmoid(gm_ref[:, d:] + b_ref[:, d:])
        o_ref[...] = (ga * za_ref[...] + gb * zb_ref[...]).astype(BF16)

    return _pcall(body, name="merge_fwd", out_shape=jax.ShapeDtypeStruct((t, d), BF16), grid=(t // tt,),
                  in_specs=[pl.BlockSpec((tt, 2 * d), lambda i: (i, 0)), pl.BlockSpec((1, 2 * d), lambda i: (0, 0)), row, row],
                  out_specs=row, dims=("parallel",))(gm, bm, za, zb)


def _merge_bwd(dmix, gm, bm, za, zb):
    t, d = za.shape
    tt = _tile(t, 256, 16)
    row = pl.BlockSpec((tt, d), lambda i: (i, 0))
    wide = pl.BlockSpec((tt, 2 * d), lambda i: (i, 0))
    vec = pl.BlockSpec((1, 2 * d), lambda i: (0, 0))

    def body(dm_ref, gm_ref, b_ref, za_ref, zb_ref, dza_ref, dzb_ref, dgm_ref, db_ref):
        dm = dm_ref[...]
        ga = _sigmoid(gm_ref[:, :d] + b_ref[:, :d])
        gb = _sigmoid(gm_ref[:, d:] + b_ref[:, d:])
        dza_ref[...] = (dm * ga).astype(BF16)
        dzb_ref[...] = (dm * gb).astype(BF16)
        dla = dm * za_ref[...] * ga * (1.0 - ga)
        dlb = dm * zb_ref[...] * gb * (1.0 - gb)
        dgm_ref[:, :d] = dla.astype(BF16)
        dgm_ref[:, d:] = dlb.astype(BF16)
        pa = jnp.sum(dla, axis=0, keepdims=True)
        pb = jnp.sum(dlb, axis=0, keepdims=True)

        @pl.when(pl.program_id(0) == 0)
        def _():
            db_ref[:, :d] = pa
            db_ref[:, d:] = pb

        @pl.when(pl.program_id(0) > 0)
        def _():
            db_ref[:, :d] += pa
            db_ref[:, d:] += pb

    return _pcall(body, name="merge_bwd",
                  out_shape=(jax.ShapeDtypeStruct((t, d), BF16), jax.ShapeDtypeStruct((t, d), BF16),
                             jax.ShapeDtypeStruct((t, 2 * d), BF16), jax.ShapeDtypeStruct((1, 2 * d), F32)),
                  grid=(t // tt,), in_specs=[row, wide, vec, row, row], out_specs=(row, row, wide, vec),
                  dims=("arbitrary",))(dmix, gm, bm, za, zb)


def _ple_final(h3, pgl, pe, gain, target):
    t, d = h3.shape
    tt = _tile(t, 256, 16)
    row = pl.BlockSpec((tt, d), lambda i: (i, 0))
    vec = pl.BlockSpec((1, d), lambda i: (0, 0))
    lvec = pl.BlockSpec((1, LANES), lambda i: (0, 0))

    def body(h_ref, pgl_ref, pe_ref, g_ref, t_ref, dh_ref, dsg_ref, dpe_ref, loss_ref, dg_ref):
        pg = _sigmoid(pgl_ref[...])
        pe_v = pe_ref[...]
        h4 = h_ref[...] + pg * pe_v
        r = lax.rsqrt(jnp.mean(h4 * h4, axis=-1, keepdims=True) + EPS)
        gv = g_ref[...]
        err = h4 * r * gv - t_ref[...]
        part_loss = 0.5 * jnp.sum(jnp.mean(err * err, axis=-1, keepdims=True), axis=0, keepdims=True)
        dy = err * (1.0 / d)
        part_g = jnp.sum(dy * h4 * r, axis=0, keepdims=True)
        dyg = dy * gv
        dh = r * dyg - h4 * (r * r * r) * jnp.mean(dyg * h4, axis=-1, keepdims=True)
        dh_ref[...] = dh
        dsg_ref[...] = (dh * pe_v * pg * (1.0 - pg)).astype(BF16)
        dpe_ref[...] = (dh * pg).astype(BF16)

        @pl.when(pl.program_id(0) == 0)
        def _():
            loss_ref[...] = jnp.broadcast_to(part_loss, (1, LANES))
            dg_ref[...] = part_g

        @pl.when(pl.program_id(0) > 0)
        def _():
            loss_ref[...] += jnp.broadcast_to(part_loss, (1, LANES))
            dg_ref[...] += part_g

    return _pcall(body, name="ple_final",
                  out_shape=(jax.ShapeDtypeStruct((t, d), F32), jax.ShapeDtypeStruct((t, d), BF16),
                             jax.ShapeDtypeStruct((t, d), BF16), jax.ShapeDtypeStruct((1, LANES), F32),
                             jax.ShapeDtypeStruct((1, d), F32)),
                  grid=(t // tt,), in_specs=[row, row, row, vec, row], out_specs=(row, row, row, lvec, vec),
                  dims=("arbitrary",))(h3, pgl, pe, gain, target)


def _adamw_math(w, g, m, v):
    m = ADAM_B1 * m + (1.0 - ADAM_B1) * g
    v = ADAM_B2 * v + (1.0 - ADAM_B2) * (g * g)
    m_hat = m / (1.0 - ADAM_B1 ** ADAM_STEP)
    v_hat = v / (1.0 - ADAM_B2 ** ADAM_STEP)
    delta = -ADAM_LR * (m_hat / (jnp.sqrt(v_hat) + ADAM_EPS) + ADAM_WD * w)
    return delta, m, v


def _adamw(parts, w, m, v, name):
    n, r, c = parts.shape
    tr = _tile(r, 256, 16)
    row = pl.BlockSpec((tr, c), lambda i: (i, 0))

    def body(p_ref, w_ref, m_ref, v_ref, g_ref, d_ref, mo_ref, vo_ref):
        g = p_ref[0].astype(F32)
        for s in range(1, n):
            g = g + p_ref[s].astype(F32)
        g_ref[...] = g
        d_ref[...], mo_ref[...], vo_ref[...] = _adamw_math(w_ref[...], g, m_ref[...], v_ref[...])

    return _pcall(body, name=name, out_shape=(jax.ShapeDtypeStruct((r, c), F32),) * 4, grid=(r // tr,),
                  in_specs=[pl.BlockSpec((n, tr, c), lambda i: (0, i, 0)), row, row, row], out_specs=(row,) * 4,
                  dims=("parallel",))(parts, w, m, v)


ANY = pl.BlockSpec(memory_space=pl.ANY)


def _all_gather(shards):
    n = len(shards)

    def body(*refs):
        x_refs, out_refs = refs[:n], refs[n:2 * n]
        send_sems, recv_sems, local_sems = refs[2 * n:]
        x, y, cc = lax.axis_index("x"), lax.axis_index("y"), lax.axis_index("c")
        me, sibling = (x, y, cc), (x, y, 1 - cc)
        chips = [(1 - x, y), (x, 1 - y), (1 - x, 1 - y)]

        def slot(a, px, py, pc):
            return out_refs[a].at[4 * px + 2 * py + pc]

        def copy(a, k, block, to, src=None):
            return pltpu.make_async_remote_copy(
                src_ref=slot(a, *block) if src is None else src, dst_ref=slot(a, *block),
                send_sem=send_sems.at[7 * a + k], recv_sem=recv_sems.at[7 * a + k], device_id=to, device_id_type=MESH)

        local, sent = [], []
        for a in range(n):
            local.append(pltpu.make_async_copy(x_refs[a], slot(a, *me), local_sems.at[a]))
            sent.append(copy(a, 0, me, sibling, src=x_refs[a]))
            sent += [copy(a, 1 + j, me, (*chip, cc), src=x_refs[a]) for j, chip in enumerate(chips)]
        for cp in local + sent:
            cp.start()
        for j, chip in enumerate(chips):
            for a in range(n):
                copy(a, 1 + j, (*chip, cc), me).wait_recv()
                sent.append(copy(a, 4 + j, (*chip, cc), sibling))
                sent[-1].start()
        for a in range(n):
            copy(a, 0, sibling, me).wait_recv()
            for j, chip in enumerate(chips):
                copy(a, 4 + j, (*chip, 1 - cc), me).wait_recv()
        for cp in sent:
            cp.wait_send()
        for cp in local:
            cp.wait()

    return pl.pallas_call(
        body, name="weights_all_gather",
        out_shape=tuple(jax.ShapeDtypeStruct((N_DEV,) + s.shape, s.dtype) for s in shards),
        in_specs=[ANY] * n, out_specs=(ANY,) * n,
        scratch_shapes=[pltpu.SemaphoreType.DMA((7 * n,)), pltpu.SemaphoreType.DMA((7 * n,)),
                        pltpu.SemaphoreType.DMA((n,))],
    )(*shards)


def _reduce_scatter_exchange(blocks):
    n = len(blocks)

    def body(*refs):
        g_refs, recv_refs = refs[:n], refs[n:2 * n]
        send_sems, recv_sems, local_sems = refs[2 * n:]
        x, y, cc = lax.axis_index("x"), lax.axis_index("y"), lax.axis_index("c")
        me = 4 * x + 2 * y + cc
        local, sent, landing = [], [], []
        for a in range(n):
            local.append(pltpu.make_async_copy(g_refs[a].at[me], recv_refs[a].at[me], local_sems.at[a]))
        for k in range(1, N_DEV):
            px, py, pc = x ^ (k >> 2), y ^ ((k >> 1) & 1), cc ^ (k & 1)
            peer = 4 * px + 2 * py + pc
            for a in range(n):
                sems = dict(send_sem=send_sems.at[7 * a + k - 1], recv_sem=recv_sems.at[7 * a + k - 1],
                            device_id=(px, py, pc), device_id_type=MESH)
                sent.append(pltpu.make_async_remote_copy(src_ref=g_refs[a].at[peer], dst_ref=recv_refs[a].at[me], **sems))
                landing.append(pltpu.make_async_remote_copy(src_ref=g_refs[a].at[me], dst_ref=recv_refs[a].at[peer], **sems))
        for cp in local + sent:
            cp.start()
        for cp in landing:
            cp.wait_recv()
        for cp in sent:
            cp.wait_send()
        for cp in local:
            cp.wait()

    return pl.pallas_call(
        body, name="grads_reduce_scatter_exchange",
        out_shape=tuple(jax.ShapeDtypeStruct(b.shape, b.dtype) for b in blocks),
        in_specs=[ANY] * n, out_specs=(ANY,) * n,
        scratch_shapes=[pltpu.SemaphoreType.DMA((7 * n,)), pltpu.SemaphoreType.DMA((7 * n,)),
                        pltpu.SemaphoreType.DMA((n,))],
    )(*blocks)


HBM = pl.BlockSpec(memory_space=pltpu.HBM)
SEM = pl.BlockSpec(memory_space=pltpu.SEMAPHORE)
DATAFLOW = pltpu.SideEffectType.DATAFLOW_SIDE_EFFECTING


def _peers():
    x, y, cc = lax.axis_index("x"), lax.axis_index("y"), lax.axis_index("c")
    out = []
    for k in range(1, N_DEV):
        px, py, pc = x ^ (k >> 2), y ^ ((k >> 1) & 1), cc ^ (k & 1)
        out.append((k, (px, py, pc), 4 * px + 2 * py + pc))
    return 4 * x + 2 * y + cc, out


def _scatter_start(blocks, after, name, gather=False):
    n = len(blocks)
    lands = [lax.empty((N_DEV,) + b.shape if gather else b.shape, b.dtype) for b in blocks]

    def body(*refs):
        g_refs, land_refs = refs[:n], refs[n:2 * n]
        send_sems, recv_sems, token = refs[2 * n + 1], refs[2 * n + 2], refs[-1]
        me, peers = _peers()
        for k, peer, slot in peers:
            for a in range(n):
                pltpu.make_async_remote_copy(
                    src_ref=g_refs[a] if gather else g_refs[a].at[slot], dst_ref=land_refs[a].at[me],
                    send_sem=send_sems.at[7 * a + k - 1],
                    recv_sem=recv_sems.at[7 * a + k - 1], device_id=peer, device_id_type=MESH).start()
        token[...] = jnp.zeros_like(token)

    thru = [pltpu.HBM(b.shape, b.dtype) for b in blocks]
    thru_lands = [pltpu.HBM(b.shape, b.dtype) for b in lands]
    return pl.pallas_call(
        body, name=name,
        out_shape=(pltpu.SemaphoreType.DMA((7 * n,)), pltpu.SemaphoreType.DMA((7 * n,)), *thru, *thru_lands,
                   jax.ShapeDtypeStruct((8, LANES), F32)),
        in_specs=[HBM] * (2 * n) + [pl.BlockSpec(memory_space=pl.ANY)],
        out_specs=(SEM, SEM, *[HBM] * (2 * n), pl.BlockSpec(memory_space=pltpu.VMEM)),
        input_output_aliases={i: 2 + i for i in range(2 * n)},
        compiler_params=pltpu.CompilerParams(has_side_effects=DATAFLOW),
    )(*[pltpu.with_memory_space_constraint(a, pltpu.HBM) for a in list(blocks) + lands], after)


def _scatter_wait(started, after, name, gather=False):
    send_sems, recv_sems, *rest = started
    n = (len(rest) - 1) // 2
    thru = rest[:2 * n]

    def body(*refs):
        g_refs, land_refs = refs[:n], refs[n:2 * n]
        send_sems, recv_sems = refs[2 * n], refs[2 * n + 1]
        me, peers = _peers()
        for k, peer, slot in peers:
            for a in range(n):
                copy = pltpu.make_async_remote_copy(
                    src_ref=g_refs[a] if gather else g_refs[a].at[slot], dst_ref=land_refs[a].at[slot],
                    send_sem=send_sems.at[7 * a + k - 1],
                    recv_sem=recv_sems.at[7 * a + k - 1], device_id=peer, device_id_type=MESH)
                copy.wait_send()
                copy.wait_recv()

    out = pl.pallas_call(
        body, name=name, out_shape=tuple(pltpu.HBM(a.shape, a.dtype) for a in thru),
        in_specs=[HBM] * (2 * n) + [SEM, SEM, pl.BlockSpec(memory_space=pl.ANY)], out_specs=tuple([HBM] * (2 * n)),
        input_output_aliases={i: i for i in range(2 * n)},
        compiler_params=pltpu.CompilerParams(has_side_effects=DATAFLOW),
    )(*thru, send_sems, recv_sems, after)
    return out[:n], out[n:]


BIG = (("w_ffn1_gate", "colT"), ("w_ffn1_up", "colT"), ("w_ffn1_down", "row"), ("w_ffn2_gate", "colT"),
       ("w_ffn2_up", "colT"), ("w_ffn2_down", "row"), ("w_in", "colT"), ("w_merge", "col"), ("w_ret_out", "col"),
       ("w_fox_out", "col"), ("w_out", "row"), ("w_ple", "col"), ("w_ple_gate", "row"))


def _shard_view(a, kind):
    a = a.reshape(a.shape[-2:])
    return a.T if kind == "colT" else a


def _unview(a, kind, shape):
    return (a.T if kind == "colT" else a).reshape(shape)


def _full_from_slots(g, kind):
    n, r, c = g.shape
    return g.transpose(1, 0, 2).reshape(r, n * c) if kind == "col" else g.reshape(n * r, c)


def _slots_from_full(f, kind):
    r, c = f.shape
    return f.reshape(r, N_DEV, c // N_DEV).transpose(1, 0, 2) if kind == "col" else f.reshape(N_DEV, r // N_DEV, c)


EARLY_GROUPS = (("w_ple_gate", "w_ple", "w_ffn2_down", "w_ffn2_gate", "w_ffn2_up", "w_out", "w_ret_out", "w_fox_out"),
                ("w_in", "w_merge"))


def _scatter_group(gw, names, after, name):
    kind = dict(BIG)
    return names, _scatter_start([_slots_from_full(gw[n], kind[n]) for n in names], after, name)


def _pad_heads(w):
    d = w.shape[1]
    return jnp.pad(w.reshape(FOX_HEADS, FOX_DIM, d), ((0, 0), (0, LANES - FOX_DIM), (0, 0))).reshape(FOX_TILES, d)


def _unpad_heads(w):
    d = w.shape[1]
    return w.reshape(FOX_HEADS, LANES, d)[:, :FOX_DIM].reshape(FOX_WIDTH, d)


def _deinterleave_rows(w):
    d = w.shape[1]
    return w.reshape(RET_HEADS, RET_DIM // 2, 2, d).transpose(0, 2, 1, 3).reshape(RET_WIDTH, d)


def _interleave_rows(w):
    d = w.shape[1]
    return w.reshape(RET_HEADS, 2, RET_DIM // 2, d).transpose(0, 2, 1, 3).reshape(RET_WIDTH, d)


def _pad_w_in(wt):
    d = wt.shape[1]
    rw, fw = RET_WIDTH, FOX_WIDTH
    fo = 4 * rw
    return jnp.concatenate([
        _deinterleave_rows(wt[:rw]), _deinterleave_rows(wt[rw:2 * rw]), wt[2 * rw:4 * rw],
        _pad_heads(wt[fo:fo + fw]), _pad_heads(wt[fo + fw:fo + 2 * fw]), _pad_heads(wt[fo + 2 * fw:fo + 3 * fw]),
        wt[fo + 3 * fw:], jnp.zeros((2 * LANES - FOX_HEADS, d), wt.dtype)], axis=0)


def _unpad_w_in(g):
    rw = RET_WIDTH
    f0 = 4 * rw
    return jnp.concatenate([
        _interleave_rows(g[:rw]), _interleave_rows(g[rw:2 * rw]), g[2 * rw:4 * rw],
        _unpad_heads(g[f0:f0 + FOX_TILES]), _unpad_heads(g[f0 + FOX_TILES:f0 + 2 * FOX_TILES]),
        _unpad_heads(g[f0 + 2 * FOX_TILES:f0 + 3 * FOX_TILES]),
        g[f0 + 3 * FOX_TILES:f0 + 3 * FOX_TILES + FOX_HEADS]], axis=0)


SMALL = ("ln_ffn1", "ln_mix", "b_forget", "b_merge", "ln_ffn2", "ln_ple", "ln_final")


def _small_rows(n):
    rows = -(-n // LANES)
    return -(-rows // 8) * 8


def _pack_small(vals, with_loss=None):
    parts = []
    for name in SMALL:
        v = vals[name].reshape(-1).astype(F32)
        rows = _small_rows(v.shape[0])
        parts.append(jnp.pad(v, (0, rows * LANES - v.shape[0])).reshape(rows, LANES))
    if with_loss is not None:
        parts.append(jnp.pad(with_loss.reshape(1, LANES), ((0, 7), (0, 0))))
    else:
        parts.append(jnp.zeros((8, LANES), F32))
    return jnp.concatenate(parts, axis=0)


def _unpack_small(packed, shapes):
    out, at = {}, 0
    for name in SMALL:
        n = int(np.prod(shapes[name]))
        rows = _small_rows(n)
        out[name] = packed[at:at + rows].reshape(-1)[:n].reshape(shapes[name])
        at += rows
    return out, packed[at, 0]


def _gather_finish(group, after, me):
    names, started, wait_name = group
    kind = dict(BIG)
    sent, landed = _scatter_wait(started, after, wait_name, gather=True)
    return {n: _full_from_slots(lax.dynamic_update_slice_in_dim(land, shard[None], me, 0), kind[n])
            for n, shard, land in zip(names, sent, landed)}


def _local_step(x, p, positions, target, w, small, me, entry_token, gathers):
    t, d = x.shape
    gain = lambda n: small[n].reshape(1, d)
    w = dict(w)
    bpad = jnp.pad(small["b_forget"].reshape(1, FOX_HEADS), ((0, 0), (0, LANES - FOX_HEADS)))
    bm = small["b_merge"].reshape(1, 2 * d)
    fox_blk = _tile(t, 512, 128)

    def ffn_fwd(n, tag, down_gather=None):
        g = _mm([(n, w[f"w_{tag}_gate"], "nt")], BF16, f"{tag}_gate", tn=1408)
        u = _mm([(n, w[f"w_{tag}_up"], "nt")], BF16, f"{tag}_up", tn=1408)
        a = _swiglu_fwd(g, u, f"{tag}_swiglu")
        if down_gather is not None:
            w.update(_gather_finish(down_gather, a, me))
        return g, u, a, _mm([(a, w[f"w_{tag}_down"], "nn")], F32, f"{tag}_down")

    n1 = _rms_fwd(x, gain("ln_ffn1") + entry_token, "rms_ffn1")
    g1, u1, a1, f1 = ffn_fwd(n1, "ffn1", down_gather=gathers[0])
    h1, u = _rms_fwd(x, gain("ln_mix"), "rms_mix", f=f1)
    w.update(_gather_finish(gathers[1], f1, me))
    w_in_t = _pad_w_in(w["w_in"])
    gm = _mm([(u, w["w_merge"], "nn")], F32, "mixer_gates")
    pm = _mm([(u, w_in_t, "nt")], F32, "mixer_in", tn=1792)

    half = jnp.arange(RET_DIM // 2, dtype=F32) / (RET_DIM // 2)
    inv = 1.0 / (ROPE_BASE ** half)
    inv2 = jnp.concatenate([inv, inv]).reshape(1, RET_DIM)
    sign2 = jnp.concatenate([-jnp.ones((RET_DIM // 2,), F32), jnp.ones((RET_DIM // 2,), F32)]).reshape(1, RET_DIM)
    cos2, sin2 = _rope_tables(positions.reshape(t, 1), inv2, sign2)
    consts = _ret_consts()
    y_ret, y_raw, states = _ret_fwd(pm, cos2, sin2, consts)
    w.update(_gather_finish(gathers[2], y_raw, me))
    w_fox_pad = _pad_heads(w["w_fox_out"])
    za = _mm([(y_ret, w["w_ret_out"], "nn")], F32, "ret_out")

    qa, ka, va = _fox_prep(pm, bpad)
    o_fox, y_fox, qa_b = _fox_fwd(qa, ka, va, fox_blk)
    zb = _mm([(y_fox, w_fox_pad, "nn")], F32, "fox_out")

    mix = _merge_fwd(gm, bm, za, zb)
    mo = _mm([(mix, w["w_out"], "nn")], F32, "mix_out")
    h2, n2 = _rms_fwd(h1, gain("ln_ffn2"), "rms_ffn2", f=mo, scale=1.0)
    g2, u2, a2, f2 = ffn_fwd(n2, "ffn2")
    h3, n3 = _rms_fwd(h2, gain("ln_ple"), "rms_ple", f=f2)
    pgl = _mm([(n3, w["w_ple_gate"], "nn")], F32, "ple_gate")
    pb = p.astype(BF16)
    pe = _mm([(pb, w["w_ple"], "nn")], F32, "ple_embed")

    gw, gs = {}, {}
    dh4, dsg, dpe, loss, gs["ln_final"] = _ple_final(h3, pgl, pe, gain("ln_final"), target)
    gw["w_ple_gate"] = _mm([(n3, dsg, "tn")], BF16, "d_w_ple_gate", tn=256)
    gw["w_ple"] = _mm([(pb, dpe, "tn")], BF16, "d_w_ple", tn=256)
    dn3 = _mm([(dsg, w["w_ple_gate"], "nt")], F32, "d_n3")
    dh3, dh3_half, gs["ln_ple"] = _rms_bwd(dn3, h3, gain("ln_ple"), dh4, "rms_ple_bwd", 0.5)

    def ffn_bwd(dh_half, g, u_, a, n, tag, scatter_now=None):
        gw[f"w_{tag}_down"] = _mm([(a, dh_half, "tn")], BF16, f"d_w_{tag}_down", tm=1408, tn=256)
        start = scatter_now if scatter_now is not None else (lambda *_: None)
        token = start((f"w_{tag}_down",), dh_half, "c")
        da = _mm([(dh_half, w[f"w_{tag}_down"], "nt")], BF16, f"d_a_{tag}", tn=1408, after=token)
        dg, du_ = _swiglu_bwd(da, g, u_, f"{tag}_swiglu_bwd")
        gw[f"w_{tag}_gate"] = _mm([(dg, n, "tn")], BF16, f"d_w_{tag}_gate", tm=1408, tn=256)
        token = start((f"w_{tag}_gate",), da, "d")
        gw[f"w_{tag}_up"] = _mm([(du_, n, "tn")], BF16, f"d_w_{tag}_up", tm=1408, tn=256, after=token)
        token = start((f"w_{tag}_up",), da, "e")
        return _mm([(dg, w[f"w_{tag}_gate"], "nn"), (du_, w[f"w_{tag}_up"], "nn")], F32, f"d_n_{tag}", tm=512,
                   after=token)

    dn2 = ffn_bwd(dh3_half, g2, u2, a2, n2, "ffn2")
    dh2, dh2_b, gs["ln_ffn2"] = _rms_bwd(dn2, h2, gain("ln_ffn2"), dh3, "rms_ffn2_bwd", 1.0)

    gw["w_out"] = _mm([(mix, dh2_b, "tn")], BF16, "d_w_out", tn=256)
    dmix = _mm([(dh2_b, w["w_out"], "nt")], F32, "d_mix")
    dza, dzb, dgm, gs["b_merge"] = _merge_bwd(dmix, gm, bm, za, zb)
    gw["w_ret_out"] = _mm([(y_ret, dza, "tn")], BF16, "d_w_ret_out", tn=256)
    gw["w_fox_out"] = _unpad_heads(_mm([(y_fox, dzb, "tn")], BF16, "d_w_fox_out", tn=256))
    dy_ret = _mm([(dza, w["w_ret_out"], "nt")], F32, "d_y_ret")
    do_fox = _mm([(dzb, w_fox_pad, "nt")], F32, "d_y_fox")

    pending = [_scatter_group(gw, EARLY_GROUPS[0], dy_ret, "grads_scatter_a_start")]
    token = pending[0][1][-1][0, 0]
    drq, drk, drv, drg = _ret_bwd(dy_ret, pm, cos2, sin2, y_raw, states, consts[:3] + (consts[3] + token,))

    dqa, dka, dva, ds_rows, ds_cols = _fox_bwd(qa_b, ka, va, do_fox, o_fox, fox_blk)
    dc = jnp.pad((ds_rows + ds_cols).reshape(FOX_HEADS, t).T, ((0, 0), (0, LANES - FOX_HEADS)))
    dff, db_forget = _fox_post(dc, pm, bpad)
    gs["b_forget"] = db_forget[:, :FOX_HEADS]

    dpm = jnp.concatenate([drq, drk, drv, drg, dqa, dka, dva, dff, jnp.zeros((t, LANES), BF16)], axis=1)
    gw["w_merge"] = _mm([(u, dgm, "tn")], BF16, "d_w_merge", tn=512)
    gw["w_in"] = _unpad_w_in(_mm([(dpm, u, "tn")], BF16, "d_w_in", tm=1792, tn=256))
    du = _mm([(dpm, w_in_t, "nn"), (dgm, w["w_merge"], "nt")], F32, "d_u", tm=1024, tn=512)
    pending.append(_scatter_group(gw, EARLY_GROUPS[1], du, "grads_scatter_b_start"))
    token = pending[1][1][-1][0:1, 0:1]
    dh1, dh1_half, gs["ln_mix"] = _rms_bwd(du, h1, gain("ln_mix") + token, dh2, "rms_mix_bwd", 0.5)

    def scatter_now(names, after, tag):
        pending.append(_scatter_group(gw, names, after, f"grads_scatter_{tag}_start"))
        return pending[-1][1][-1]

    dn1 = ffn_bwd(dh1_half, g1, u1, a1, n1, "ffn1", scatter_now=scatter_now)
    dx, _, gs["ln_ffn1"] = _rms_bwd(dn1, x, gain("ln_ffn1"), dh1, "rms_ffn1_bwd", 1.0)
    return loss, dx, gw, gs, pending


WEIGHTS = ("ln_ffn1", "w_ffn1_gate", "w_ffn1_up", "w_ffn1_down", "ln_mix", "w_in", "b_forget", "w_merge", "b_merge",
           "w_ret_out", "w_fox_out", "w_out", "ln_ffn2", "w_ffn2_gate", "w_ffn2_up", "w_ffn2_down", "ln_ple", "w_ple",
           "w_ple_gate", "ln_final")


def kernel(x, p, positions, ln_ffn1, w_ffn1_gate, w_ffn1_up, w_ffn1_down, ln_mix, w_in, b_forget, w_merge, b_merge, w_ret_out, w_fox_out, w_out, ln_ffn2, w_ffn2_gate, w_ffn2_up, w_ffn2_down, ln_ple, w_ple, w_ple_gate, ln_final, loss_target, m_ln_ffn1, m_w_ffn1_gate, m_w_ffn1_up, m_w_ffn1_down, m_ln_mix, m_w_in, m_b_forget, m_w_merge, m_b_merge, m_w_ret_out, m_w_fox_out, m_w_out, m_ln_ffn2, m_w_ffn2_gate, m_w_ffn2_up, m_w_ffn2_down, m_ln_ple, m_w_ple, m_w_ple_gate, m_ln_final, v_ln_ffn1, v_w_ffn1_gate, v_w_ffn1_up, v_w_ffn1_down, v_ln_mix, v_w_in, v_b_forget, v_w_merge, v_b_merge, v_w_ret_out, v_w_fox_out, v_w_out, v_ln_ffn2, v_w_ffn2_gate, v_w_ffn2_up, v_w_ffn2_down, v_ln_ple, v_w_ple, v_w_ple_gate, v_ln_final):
    args = dict(ln_ffn1=ln_ffn1, w_ffn1_gate=w_ffn1_gate, w_ffn1_up=w_ffn1_up, w_ffn1_down=w_ffn1_down, ln_mix=ln_mix, w_in=w_in, b_forget=b_forget, w_merge=w_merge, b_merge=b_merge, w_ret_out=w_ret_out, w_fox_out=w_fox_out, w_out=w_out, ln_ffn2=ln_ffn2, w_ffn2_gate=w_ffn2_gate, w_ffn2_up=w_ffn2_up, w_ffn2_down=w_ffn2_down, ln_ple=ln_ple, w_ple=w_ple, w_ple_gate=w_ple_gate, ln_final=ln_final)
    moms = dict(ln_ffn1=m_ln_ffn1, w_ffn1_gate=m_w_ffn1_gate, w_ffn1_up=m_w_ffn1_up, w_ffn1_down=m_w_ffn1_down, ln_mix=m_ln_mix, w_in=m_w_in, b_forget=m_b_forget, w_merge=m_w_merge, b_merge=m_b_merge, w_ret_out=m_w_ret_out, w_fox_out=m_w_fox_out, w_out=m_w_out, ln_ffn2=m_ln_ffn2, w_ffn2_gate=m_w_ffn2_gate, w_ffn2_up=m_w_ffn2_up, w_ffn2_down=m_w_ffn2_down, ln_ple=m_ln_ple, w_ple=m_w_ple, w_ple_gate=m_w_ple_gate, ln_final=m_ln_final)
    vars_ = dict(ln_ffn1=v_ln_ffn1, w_ffn1_gate=v_w_ffn1_gate, w_ffn1_up=v_w_ffn1_up, w_ffn1_down=v_w_ffn1_down, ln_mix=v_ln_mix, w_in=v_w_in, b_forget=v_b_forget, w_merge=v_w_merge, b_merge=v_b_merge, w_ret_out=v_w_ret_out, w_fox_out=v_w_fox_out, w_out=v_w_out, ln_ffn2=v_ln_ffn2, w_ffn2_gate=v_w_ffn2_gate, w_ffn2_up=v_w_ffn2_up, w_ffn2_down=v_w_ffn2_down, ln_ple=v_ln_ple, w_ple=v_w_ple, w_ple_gate=v_w_ple_gate, ln_final=v_ln_final)
    kinds = ("grad", "delta", "new_m", "new_v")

    me = 4 * lax.axis_index("x") + 2 * lax.axis_index("y") + lax.axis_index("c")
    kind_of = dict(BIG)
    shard = {n: _shard_view(args[n], kind).astype(BF16) for n, kind in BIG}
    first = ("w_ffn1_gate", "w_ffn1_up")
    groups = (("w_ffn1_down",), ("w_in", "w_merge"))
    groups += (tuple(n for n, _ in BIG if n not in first + groups[0] + groups[1]),)
    gathered = _all_gather([shard[n] for n in first])
    w_full = {n: _full_from_slots(g, kind_of[n]) for n, g in zip(first, gathered)}
    gathers, after = [], gathered[0]
    for tag, names in zip("zab", groups):
        started = _scatter_start([shard[n] for n in names], after, f"weights_gather_{tag}_start", gather=True)
        gathers.append((names, started, f"weights_gather_{tag}_wait"))
        after = started[-1]

    small = {n: args[n] for n in SMALL}
    loss_part, dx, gw, gs, pending = _local_step(x[0], p[0, 0], positions[0], loss_target[0], w_full, small, me,
                                                 after[0:1, 0:1], gathers)

    parts_of = {}
    for tag, (names, started) in zip("abcde", pending):
        sent, landed = _scatter_wait(started, dx, f"grads_scatter_{tag}_wait")
        for n, blk, land in zip(names, sent, landed):
            own = lax.dynamic_index_in_dim(blk, me, 0, keepdims=True)
            parts_of[n] = lax.dynamic_update_slice_in_dim(land, own, me, 0)
    late = [(n, kind) for n, kind in BIG if n not in parts_of]
    small_part = _pack_small(gs, with_loss=loss_part)
    blocks = [_slots_from_full(gw[n], kind) for n, kind in late]
    blocks.append(jnp.broadcast_to(small_part, (N_DEV,) + small_part.shape))
    recv = _reduce_scatter_exchange(blocks)
    parts_of.update({n: r for (n, _), r in zip(late, recv)})

    res = {}
    for n, kind in BIG:
        parts = parts_of[n]
        outs = _adamw(parts, _shard_view(args[n], kind), _shard_view(moms[n], kind), _shard_view(vars_[n], kind),
                      f"adamw_{n}")
        for what, o in zip(kinds, outs):
            res[(what, n)] = _unview(o, kind, args[n].shape)
    s_outs = _adamw(recv[-1], _pack_small(small), _pack_small({n: moms[n] for n in SMALL}),
                    _pack_small({n: vars_[n] for n in SMALL}), "adamw_small")
    for what, sm in zip(kinds, s_outs):
        svals, extra = _unpack_small(sm, {n: args[n].shape for n in SMALL})
        if what == "grad":
            loss = extra
        for n in SMALL:
            res[(what, n)] = svals[n]
    return (loss, dx[None], *[res[(what, n)] for what in kinds for n in WEIGHTS])
```

```python
import numpy as np
import jax
import jax.numpy as jnp
from jax import lax
from jax.experimental import pallas as pl
from jax.experimental.pallas import tpu as pltpu

F32 = jnp.float32
BF16 = jnp.bfloat16

N_DEV = 8
EPS = 1e-6
RET_HEADS = 4
RET_DIM = 128
RET_WIDTH = RET_HEADS * RET_DIM
FOX_HEADS = 8
FOX_DIM = 64
FOX_WIDTH = FOX_HEADS * FOX_DIM
CHUNK = 128
ROPE_BASE = 10000.0
LANES = 128
FOX_TILES = FOX_HEADS * LANES
IN_COLS = 4 * RET_WIDTH + 3 * FOX_WIDTH + FOX_HEADS
IN_PAD = 4 * RET_WIDTH + 3 * FOX_TILES + 2 * LANES
TILE_RQ, TILE_RK, TILE_RV, TILE_RG = 0, 4, 8, 12
TILE_FQ, TILE_FK, TILE_FV, TILE_FF = 16, 24, 32, 40
NEG = -1e30

ADAM_LR = 0.001
ADAM_B1 = 0.9
ADAM_B2 = 0.999
ADAM_EPS = 1e-08
ADAM_WD = 0.01
ADAM_STEP = 10

VMEM_LIMIT_BYTES = 56 * 1024 * 1024

MESH = pl.DeviceIdType.MESH


def _tile(dim, pref, mult):
    if dim <= pref:
        return dim
    t = (pref // mult) * mult
    while t >= mult:
        if dim % t == 0:
            return t
        t -= mult
    return dim


def _params(dims):
    return pltpu.CompilerParams(dimension_semantics=dims, vmem_limit_bytes=VMEM_LIMIT_BYTES)


def _pcall(body, *, name, out_shape, grid, in_specs, out_specs, scratch_shapes=(), dims=None):
    return pl.pallas_call(body, name=name, out_shape=out_shape, grid=grid, in_specs=in_specs, out_specs=out_specs,
                          scratch_shapes=list(scratch_shapes), compiler_params=_params(dims))


def _dot(a, b, ca, cb):
    return lax.dot_general(a, b, (((ca,), (cb,)), ((), ())), preferred_element_type=F32)


def _sigmoid(x):
    return 1.0 / (1.0 + jnp.exp(-x))


def _mm(pairs, out_dtype, name, tm=1024, tn=1024, after=None):
    dims = []
    for a, b, mode in pairs:
        m, k = (a.shape[1], a.shape[0]) if mode == "tn" else a.shape
        n, k2 = b.shape if mode == "nt" else (b.shape[1], b.shape[0])
        assert k == k2, (name, a.shape, b.shape, mode)
        dims.append((m, n))
    assert all(d == dims[0] for d in dims), (name, dims)
    m, n = dims[0]
    tm = _tile(m, tm, 128 if any(mode == "tn" for _, _, mode in pairs) else 16)
    tn = _tile(n, tn, 128)
    in_specs, contract, operands = [], [], []
    for a, b, mode in pairs:
        k = a.shape[0] if mode == "tn" else a.shape[1]
        in_specs.append(pl.BlockSpec((k, tm), lambda i, j: (0, i)) if mode == "tn" else
                        pl.BlockSpec((tm, k), lambda i, j: (i, 0)))
        in_specs.append(pl.BlockSpec((tn, k), lambda i, j: (j, 0)) if mode == "nt" else
                        pl.BlockSpec((k, tn), lambda i, j: (0, j)))
        contract.append((0 if mode == "tn" else 1, 1 if mode == "nt" else 0))
        operands += [a, b]
    if after is not None:
        in_specs.append(pl.BlockSpec(memory_space=pl.ANY))
        operands.append(after)

    def body(*refs):
        o_ref = refs[-1]
        acc = None
        for p, (ca, cb) in enumerate(contract):
            part = _dot(refs[2 * p][...], refs[2 * p + 1][...], ca, cb)
            acc = part if acc is None else acc + part
        o_ref[...] = acc.astype(out_dtype)

    return _pcall(body, name=name, out_shape=jax.ShapeDtypeStruct((m, n), out_dtype), grid=(m // tm, n // tn),
                  in_specs=in_specs, out_specs=pl.BlockSpec((tm, tn), lambda i, j: (i, j)),
                  dims=("parallel", "parallel"))(*operands)


def _rms_fwd(h, gain, name, f=None, scale=0.5):
    t, d = h.shape
    tt = _tile(t, 512, 16)
    row = pl.BlockSpec((tt, d), lambda i: (i, 0))
    vec = pl.BlockSpec((1, d), lambda i: (0, 0))

    def norm(hv, g_ref, n_ref):
        r = lax.rsqrt(jnp.mean(hv * hv, axis=-1, keepdims=True) + EPS)
        n_ref[...] = (hv * r * g_ref[...]).astype(BF16)

    if f is None:

        def body(h_ref, g_ref, n_ref):
            norm(h_ref[...], g_ref, n_ref)

        return _pcall(body, name=name, out_shape=jax.ShapeDtypeStruct((t, d), BF16), grid=(t // tt,),
                      in_specs=[row, vec], out_specs=row, dims=("parallel",))(h, gain)

    def body(h_ref, f_ref, g_ref, hn_ref, n_ref):
        hv = h_ref[...] + scale * f_ref[...]
        hn_ref[...] = hv
        norm(hv, g_ref, n_ref)

    return _pcall(body, name=name,
                  out_shape=(jax.ShapeDtypeStruct((t, d), F32), jax.ShapeDtypeStruct((t, d), BF16)),
                  grid=(t // tt,), in_specs=[row, row, vec], out_specs=(row, row), dims=("parallel",))(h, f, gain)


def _rms_bwd(dn, h, gain, dh_in, name, out_scale):
    t, d = h.shape
    tt = _tile(t, 512, 16)
    row = pl.BlockSpec((tt, d), lambda i: (i, 0))
    vec = pl.BlockSpec((1, d), lambda i: (0, 0))

    def body(dn_ref, h_ref, g_ref, dhin_ref, dh_ref, dhb_ref, dg_ref):
        hv = h_ref[...]
        dnv = dn_ref[...].astype(F32)
        r = lax.rsqrt(jnp.mean(hv * hv, axis=-1, keepdims=True) + EPS)
        dng = dnv * g_ref[...]
        dh = dhin_ref[...] + r * dng - hv * (r * r * r) * jnp.mean(dng * hv, axis=-1, keepdims=True)
        dh_ref[...] = dh
        dhb_ref[...] = (out_scale * dh).astype(BF16)
        part = jnp.sum(dnv * hv * r, axis=0, keepdims=True)

        @pl.when(pl.program_id(0) == 0)
        def _():
            dg_ref[...] = part

        @pl.when(pl.program_id(0) > 0)
        def _():
            dg_ref[...] += part

    return _pcall(body, name=name,
                  out_shape=(jax.ShapeDtypeStruct((t, d), F32), jax.ShapeDtypeStruct((t, d), BF16),
                             jax.ShapeDtypeStruct((1, d), F32)),
                  grid=(t // tt,), in_specs=[row, row, vec, row], out_specs=(row, row, vec),
                  dims=("arbitrary",))(dn, h, gain, dh_in)


def _swiglu_fwd(g, u, name):
    t, f = g.shape
    tt = _tile(t, 256, 16)
    row = pl.BlockSpec((tt, f), lambda i: (i, 0))

    def body(g_ref, u_ref, a_ref):
        gv = g_ref[...].astype(F32)
        a_ref[...] = (gv * _sigmoid(gv) * u_ref[...].astype(F32)).astype(BF16)

    return _pcall(body, name=name, out_shape=jax.ShapeDtypeStruct((t, f), BF16), grid=(t // tt,),
                  in_specs=[row, row], out_specs=row, dims=("parallel",))(g, u)


def _swiglu_bwd(da, g, u, name):
    t, f = g.shape
    tt = _tile(t, 256, 16)
    row = pl.BlockSpec((tt, f), lambda i: (i, 0))

    def body(da_ref, g_ref, u_ref, dg_ref, du_ref):
        gv = g_ref[...].astype(F32)
        dav = da_ref[...].astype(F32)
        sg = _sigmoid(gv)
        dg_ref[...] = (dav * u_ref[...].astype(F32) * (sg * (1.0 + gv * (1.0 - sg)))).astype(BF16)
        du_ref[...] = (dav * (gv * sg)).astype(BF16)

    return _pcall(body, name=name, out_shape=(jax.ShapeDtypeStruct((t, f), BF16),) * 2, grid=(t // tt,),
                  in_specs=[row, row, row], out_specs=(row, row), dims=("parallel",))(da, g, u)


def _rope_tables(pos_col, inv2, sign2):
    t = pos_col.shape[0]

    def body(p_ref, inv_ref, sg_ref, c_ref, s_ref):
        ang = p_ref[...].astype(F32) * inv_ref[...]
        c_ref[...] = jnp.cos(ang)
        s_ref[...] = jnp.sin(ang) * sg_ref[...]

    full = lambda shape: pl.BlockSpec(shape, lambda i: (0, 0))
    return _pcall(body, name="rope_tables", out_shape=(jax.ShapeDtypeStruct((t, RET_DIM), F32),) * 2, grid=(1,),
                  in_specs=[full((t, 1)), full((1, RET_DIM)), full((1, RET_DIM))],
                  out_specs=(full((t, RET_DIM)),) * 2, dims=("arbitrary",))(pos_col, inv2, sign2)


def _rot(x, c, s):
    return x * c + pltpu.roll(x, RET_DIM // 2, 1) * s


def _rot_t(g, c, s):
    return g * c + pltpu.roll(g * s, RET_DIM // 2, 1)


def _ret_consts():
    hh = np.arange(RET_HEADS, dtype=np.float32)
    log_gamma = np.log1p(-np.exp2(-5.0 - hh)).astype(np.float32)
    idx = np.arange(CHUNK, dtype=np.float32)
    diff = idx[:, None] - idx[None, :]
    dmask = np.where(diff >= 0, np.exp(log_gamma[:, None, None] * np.maximum(diff, 0.0)), 0.0).astype(np.float32)
    kdec = np.exp(log_gamma[:, None] * (CHUNK - 1 - idx)).astype(np.float32)
    qdec = np.exp(log_gamma[:, None] * (idx + 1.0)).astype(np.float32)
    cdec = np.exp(log_gamma * CHUNK).astype(np.float32)
    bc = lambda v: np.ascontiguousarray(np.broadcast_to(v[:, :, None], (RET_HEADS, CHUNK, RET_DIM)))
    cd = np.ascontiguousarray(np.broadcast_to(cdec[:, None, None], (RET_HEADS, 8, RET_DIM)))
    return jnp.asarray(dmask), jnp.asarray(bc(qdec)), jnp.asarray(bc(kdec)), jnp.asarray(cd)


def _ret_fwd(pm, cos2, sin2, consts):
    t = pm.shape[0]
    n_chunks = t // CHUNK
    dmask, qdec, kdec, cd = consts
    scale = RET_DIM ** -0.5

    def col(c0):
        return pl.BlockSpec((CHUNK, RET_DIM), lambda h, n: (n, c0 + h))

    tab = pl.BlockSpec((CHUNK, RET_DIM), lambda h, n: (n, 0))
    head3 = lambda r: pl.BlockSpec((None, r, RET_DIM), lambda h, n: (h, 0, 0))

    def body(q_ref, k_ref, v_ref, g_ref, c_ref, s_ref, dm_ref, qd_ref, kd_ref, cd_ref, y_ref, raw_ref, st_ref, s_acc):
        @pl.when(pl.program_id(1) == 0)
        def _():
            s_acc[...] = jnp.zeros_like(s_acc)

        c, s = c_ref[...], s_ref[...]
        q = _rot(q_ref[...], c, s)
        k = _rot(k_ref[...], c, s) * scale
        vb = v_ref[...].astype(BF16)
        g = g_ref[...]
        s_in = s_acc[...]
        st_ref[...] = s_in
        a = _dot(q.astype(BF16), k.astype(BF16), 1, 1) * dm_ref[...]
        y = _dot(a.astype(BF16), vb, 1, 0) + _dot((q * qd_ref[...]).astype(BF16), s_in.astype(BF16), 1, 0)
        s_acc[...] = cd_ref[0:1, :] * s_in + _dot((k * kd_ref[...]).astype(BF16), vb, 0, 0)
        raw_ref[...] = y
        mu = jnp.mean(y, axis=-1, keepdims=True)
        yc = y - mu
        rs = lax.rsqrt(jnp.mean(yc * yc, axis=-1, keepdims=True) + EPS)
        y_ref[...] = (yc * rs * (g * _sigmoid(g))).astype(BF16)

    out_blk = pl.BlockSpec((CHUNK, RET_DIM), lambda h, n: (n, h))
    return _pcall(
        body, name="retention_fwd",
        out_shape=(jax.ShapeDtypeStruct((t, RET_WIDTH), BF16), jax.ShapeDtypeStruct((t, RET_WIDTH), F32),
                   jax.ShapeDtypeStruct((RET_HEADS, n_chunks, RET_DIM, RET_DIM), F32)),
        grid=(RET_HEADS, n_chunks),
        in_specs=[col(TILE_RQ), col(TILE_RK), col(TILE_RV), col(TILE_RG), tab, tab,
                  pl.BlockSpec((None, CHUNK, CHUNK), lambda h, n: (h, 0, 0)), head3(CHUNK), head3(CHUNK), head3(8)],
        out_specs=(out_blk, out_blk, pl.BlockSpec((None, None, RET_DIM, RET_DIM), lambda h, n: (h, n, 0, 0))),
        scratch_shapes=[pltpu.VMEM((RET_DIM, RET_DIM), F32)],
        dims=("parallel", "arbitrary"),
    )(pm, pm, pm, pm, cos2, sin2, dmask, qdec, kdec, cd)


def _ret_bwd(dy, pm, cos2, sin2, raw, states, consts):
    t = pm.shape[0]
    n_chunks = t // CHUNK
    dmask, qdec, kdec, cd = consts
    scale = RET_DIM ** -0.5
    rev = lambda n: n_chunks - 1 - n

    def col(c0):
        return pl.BlockSpec((CHUNK, RET_DIM), lambda h, n: (rev(n), c0 + h))

    tab = pl.BlockSpec((CHUNK, RET_DIM), lambda h, n: (rev(n), 0))
    blk = pl.BlockSpec((CHUNK, RET_DIM), lambda h, n: (rev(n), h))
    head3 = lambda r: pl.BlockSpec((None, r, RET_DIM), lambda h, n: (h, 0, 0))

    def body(dy_ref, q_ref, k_ref, v_ref, g_ref, c_ref, s_ref, raw_ref, st_ref, dm_ref, qd_ref, kd_ref, cd_ref,
             dq_ref, dk_ref, dv_ref, dg_ref, ds_acc):
        @pl.when(pl.program_id(1) == 0)
        def _():
            ds_acc[...] = jnp.zeros_like(ds_acc)

        c, s = c_ref[...], s_ref[...]
        q = _rot(q_ref[...], c, s)
        k = _rot(k_ref[...], c, s) * scale
        qb, kb, vb = q.astype(BF16), k.astype(BF16), v_ref[...].astype(BF16)
        g = g_ref[...]
        dm, qd, kd = dm_ref[...], qd_ref[...], kd_ref[...]
        y = raw_ref[...]
        mu = jnp.mean(y, axis=-1, keepdims=True)
        yc = y - mu
        rs = lax.rsqrt(jnp.mean(yc * yc, axis=-1, keepdims=True) + EPS)
        yn = yc * rs
        sg = _sigmoid(g)
        dyo = dy_ref[...]
        dg_ref[...] = (dyo * yn * (sg * (1.0 + g * (1.0 - sg)))).astype(BF16)
        dyn = dyo * (g * sg)
        dyr = rs * (dyn - jnp.mean(dyn, axis=-1, keepdims=True) - yn * jnp.mean(dyn * yn, axis=-1, keepdims=True))
        dyb = dyr.astype(BF16)
        s_in = st_ref[...].astype(BF16)
        ds_out = ds_acc[...]
        dsb = ds_out.astype(BF16)
        a = _dot(qb, kb, 1, 1) * dm
        da = (_dot(dyb, vb, 1, 1) * dm).astype(BF16)
        kdb = (k * kd).astype(BF16)
        qdb = (q * qd).astype(BF16)
        dv_ref[...] = (_dot(a.astype(BF16), dyb, 0, 0) + _dot(kdb, dsb, 1, 0)).astype(BF16)
        dqh = _dot(da, kb, 1, 0) + _dot(dyb, s_in, 1, 1) * qd
        dkh = _dot(da, qb, 0, 0) + _dot(vb, dsb, 1, 1) * kd
        ds_acc[...] = cd_ref[0:1, :] * ds_out + _dot(qdb, dyb, 0, 0)
        dq_ref[...] = _rot_t(dqh, c, s).astype(BF16)
        dk_ref[...] = (_rot_t(dkh, c, s) * scale).astype(BF16)

    return _pcall(
        body, name="retention_bwd",
        out_shape=(jax.ShapeDtypeStruct((t, RET_WIDTH), BF16),) * 4,
        grid=(RET_HEADS, n_chunks),
        in_specs=[blk, col(TILE_RQ), col(TILE_RK), col(TILE_RV), col(TILE_RG), tab, tab, blk,
                  pl.BlockSpec((None, None, RET_DIM, RET_DIM), lambda h, n: (h, rev(n), 0, 0)),
                  pl.BlockSpec((None, CHUNK, CHUNK), lambda h, n: (h, 0, 0)), head3(CHUNK), head3(CHUNK), head3(8)],
        out_specs=(blk,) * 4,
        scratch_shapes=[pltpu.VMEM((RET_DIM, RET_DIM), F32)],
        dims=("parallel", "arbitrary"),
    )(dy, pm, pm, pm, pm, cos2, sin2, raw, states, dmask, qdec, kdec, cd)


FOX_C_LANE = FOX_DIM
FOX_NEGC_LANE = FOX_DIM + 3
FOX_LSE_LANE = FOX_DIM + 6
FOX_L_LANE = FOX_C_LANE
FOX_ROWSUM_LANE = FOX_C_LANE
FOX_COLSUM_LANE = FOX_NEGC_LANE


def _split3(x):
    hi = x.astype(BF16)
    r1 = x - hi.astype(F32)
    mid = r1.astype(BF16)
    lo = (r1 - mid.astype(F32)).astype(BF16)
    return hi, mid, lo


def _tri_dot(tri, x):
    hi, mid, lo = _split3(x)
    return _dot(tri, lo, 1, 0) + _dot(tri, mid, 1, 0) + _dot(tri, hi, 1, 0)


def _log_sigmoid(z):
    return jnp.minimum(z, 0.0) - jnp.log1p(jnp.exp(-jnp.abs(z)))


def _fox_consts():
    place = np.zeros((2, 3, LANES, FOX_TILES), np.float32)
    ones = np.zeros((3, 1, FOX_TILES), np.float32)
    for h in range(FOX_HEADS):
        for part in range(3):
            place[0, part, h, LANES * h + FOX_C_LANE + part] = 1.0
            place[1, part, h, LANES * h + FOX_NEGC_LANE + part] = -1.0
            ones[0, 0, LANES * h + FOX_NEGC_LANE + part] = 1.0
            ones[1, 0, LANES * h + FOX_C_LANE + part] = 1.0
            ones[1, 0, LANES * h + FOX_LSE_LANE + part] = 1.0
            ones[2, 0, LANES * h + FOX_C_LANE + part] = 1.0
    return jnp.asarray(place, BF16), jnp.asarray(ones, F32)


def _fox_prep(pm, bpad):
    t = pm.shape[0]
    tt = _tile(t, 512, LANES)
    place, ones = _fox_consts()
    wide = lambda c0: pl.BlockSpec((tt, FOX_TILES), lambda i: (i, c0 // FOX_HEADS))
    const = lambda a: pl.BlockSpec(a.shape, lambda i: (0,) * a.ndim)

    def body(q_ref, k_ref, v_ref, ff_ref, b_ref, pl_ref, on_ref, qa_ref, ka_ref, va_ref, carry_s):
        @pl.when(pl.program_id(0) == 0)
        def _():
            carry_s[...] = jnp.zeros_like(carry_s)

        r = lax.broadcasted_iota(jnp.int32, (LANES, LANES), 0)
        cc = lax.broadcasted_iota(jnp.int32, (LANES, LANES), 1)
        tri = jnp.where(cc <= r, 1.0, 0.0).astype(BF16)
        bias = b_ref[...]
        for sub in range(tt // LANES):
            rows = pl.ds(sub * LANES, LANES)
            cs = _tri_dot(tri, _log_sigmoid(ff_ref[rows, :] + bias)) + carry_s[...]
            carry_s[...] = cs[LANES - 1:LANES, :]
            parts = _split3(cs)
            eq = sum(_dot(part, pl_ref[0, i], 1, 0) for i, part in enumerate(parts))
            ek = sum(_dot(part, pl_ref[1, i], 1, 0) for i, part in enumerate(parts))
            qa_ref[rows, :] = (q_ref[rows, :] * FOX_DIM ** -0.5 + eq + on_ref[0]).astype(BF16)
            ka_ref[rows, :] = (k_ref[rows, :] + ek + on_ref[1]).astype(BF16)
            va_ref[rows, :] = (v_ref[rows, :] + on_ref[2]).astype(BF16)

    out = pl.BlockSpec((tt, FOX_TILES), lambda i: (i, 0))
    return _pcall(body, name="fox_prep", out_shape=(jax.ShapeDtypeStruct((t, FOX_TILES), BF16),) * 3, grid=(t // tt,),
                  in_specs=[wide(TILE_FQ), wide(TILE_FK), wide(TILE_FV), pl.BlockSpec((tt, LANES), lambda i: (i, TILE_FF)),
                            pl.BlockSpec((1, LANES), lambda i: (0, 0)), const(place), const(ones)],
                  out_specs=(out,) * 3, scratch_shapes=[pltpu.VMEM((1, LANES), F32)],
                  dims=("arbitrary",))(pm, pm, pm, pm, bpad, place, ones)


def _fox_post(dc, pm, bpad):
    t = pm.shape[0]
    nb = t // LANES

    def body(dc_ref, ff_ref, b_ref, d_ref, db_ref):
        r = lax.broadcasted_iota(jnp.int32, (LANES, LANES), 0)
        cc = lax.broadcasted_iota(jnp.int32, (LANES, LANES), 1)
        tri = jnp.where(cc >= r, 1.0, 0.0).astype(BF16)
        bias = b_ref[...]

        def step(i, carry):
            tail, acc = carry
            rows = pl.ds(pl.multiple_of((nb - 1 - i) * LANES, LANES), LANES)
            cs = _tri_dot(tri, dc_ref[rows, :]) + tail
            dff = cs * _sigmoid(-(ff_ref[rows, :] + bias))
            d_ref[rows, :] = dff.astype(BF16)
            return cs[0:1, :], acc + jnp.sum(dff, axis=0, keepdims=True)

        zero = jnp.zeros((1, LANES), F32)
        _, acc = lax.fori_loop(0, nb, step, (zero, zero))
        db_ref[...] = acc

    return _pcall(body, name="fox_forget_bwd",
                  out_shape=(jax.ShapeDtypeStruct((t, LANES), BF16), jax.ShapeDtypeStruct((1, LANES), F32)), grid=(1,),
                  in_specs=[pl.BlockSpec((t, LANES), lambda i: (0, 0)), pl.BlockSpec((t, LANES), lambda i: (0, TILE_FF)),
                            pl.BlockSpec((1, LANES), lambda i: (0, 0))],
                  out_specs=(pl.BlockSpec((t, LANES), lambda i: (0, 0)), pl.BlockSpec((1, LANES), lambda i: (0, 0))),
                  dims=("arbitrary",))(dc, pm, bpad)


def _tri_tables(nb, q_major):
    pairs = [(i, j) for i in range(nb) for j in range(i + 1)] if q_major else \
            [(i, j) for j in range(nb) for i in range(j, nb)]
    return jnp.asarray([a for a, _ in pairs], jnp.int32), jnp.asarray([b for _, b in pairs], jnp.int32)


def _causal(s):
    n = s.shape[0]
    row = lax.broadcasted_iota(jnp.int32, (n, n), 0)
    col = lax.broadcasted_iota(jnp.int32, (n, n), 1)
    return jnp.where(col <= row, s, NEG)


def _lane_col(x, lane):
    sel = lax.broadcasted_iota(jnp.int32, x.shape, 1) == lane
    return jnp.sum(jnp.where(sel, x, 0.0), axis=1, keepdims=True)


def _fox_fwd(qa, ka, va, blk):
    t = qa.shape[0]
    nb = t // blk
    qi, kj = _tri_tables(nb, True)
    q_spec = pl.BlockSpec((blk, LANES), lambda h, s, qi_r, kj_r: (qi_r[s], h))
    k_spec = pl.BlockSpec((blk, LANES), lambda h, s, qi_r, kj_r: (kj_r[s], h))

    def body(qi_r, kj_r, q_ref, k_ref, v_ref, o_ref, ob_ref, qb_ref, m_s, acc_s):
        s_id = pl.program_id(1)
        i, j = qi_r[s_id], kj_r[s_id]

        @pl.when(j == 0)
        def _():
            m_s[...] = jnp.full_like(m_s, NEG)
            acc_s[...] = jnp.zeros_like(acc_s)

        def tile(diagonal):
            s = _dot(q_ref[...], k_ref[...], 1, 1)
            if diagonal:
                s = _causal(s)
            m_old = m_s[...]
            m_new = jnp.maximum(m_old, jnp.max(s, axis=1, keepdims=True))
            p = jnp.exp(s - jnp.tile(m_new, (1, blk // LANES)))
            acc_s[...] = jnp.exp(m_old - m_new) * acc_s[...] + _dot(p.astype(BF16), v_ref[...], 1, 0)
            m_s[...] = m_new

        @pl.when(j < i)
        def _():
            tile(False)

        @pl.when(j == i)
        def _():
            tile(True)
            acc = acc_s[...]
            l = _lane_col(acc, FOX_L_LANE)
            o = acc / l
            o_ref[...] = o
            ob_ref[...] = o.astype(BF16)
            hi, mid, lo = _split3(-(m_s[:, 0:1] + jnp.log(l)))
            lane = lax.broadcasted_iota(jnp.int32, acc.shape, 1)
            qb_ref[...] = jnp.where(lane == FOX_LSE_LANE, hi,
                                    jnp.where(lane == FOX_LSE_LANE + 1, mid,
                                              jnp.where(lane == FOX_LSE_LANE + 2, lo, q_ref[...])))

    wide = (t, FOX_TILES)
    return pl.pallas_call(
        body, name="fox_fwd",
        out_shape=(jax.ShapeDtypeStruct(wide, F32), jax.ShapeDtypeStruct(wide, BF16), jax.ShapeDtypeStruct(wide, BF16)),
        grid_spec=pltpu.PrefetchScalarGridSpec(
            num_scalar_prefetch=2, grid=(FOX_HEADS, qi.shape[0]), in_specs=[q_spec, k_spec, k_spec],
            out_specs=(q_spec,) * 3,
            scratch_shapes=[pltpu.VMEM((blk, LANES), F32), pltpu.VMEM((blk, LANES), F32)]),
        compiler_params=_params(("parallel", "arbitrary")),
    )(qi, kj, qa, ka, va)


def _fox_bwd(qa, ka, va, do, o, blk):
    t = qa.shape[0]
    nb = t // blk
    qi, kj = _tri_tables(nb, False)
    q_spec = pl.BlockSpec((blk, LANES), lambda h, s, qi_r, kj_r: (qi_r[s], h))
    k_spec = pl.BlockSpec((blk, LANES), lambda h, s, qi_r, kj_r: (kj_r[s], h))
    head_spec = pl.BlockSpec((t, LANES), lambda h, s, qi_r, kj_r: (0, h))
    head_col = pl.BlockSpec((None, t, 1), lambda h, s, qi_r, kj_r: (h, 0, 0))
    k_col = pl.BlockSpec((None, blk, 1), lambda h, s, qi_r, kj_r: (h, kj_r[s], 0))
    first_spec = pl.BlockSpec((blk, LANES), lambda h, s, qi_r, kj_r: (jnp.where(kj_r[s] == 0, qi_r[s], nb - 1), h))
    n_steps = int(qi.shape[0])

    def body(qi_r, kj_r, q_ref, k_ref, v_ref, do_ref, o_ref, dq_ref, dk_ref, dv_ref, rs_ref, cs_ref,
             doa_s, dq_s, dk_s, dv_s):
        s_id = pl.program_id(1)
        i, j = qi_r[s_id], kj_r[s_id]
        rows = pl.ds(pl.multiple_of(i * blk, blk), blk)

        @pl.when(j == 0)
        def _():
            dof = do_ref[...]
            hi, mid, lo = _split3(-jnp.sum(dof * o_ref[...], axis=1, keepdims=True))
            lane = lax.broadcasted_iota(jnp.int32, dof.shape, 1)
            doa = jnp.where(lane == FOX_C_LANE, hi.astype(F32),
                            jnp.where(lane == FOX_C_LANE + 1, mid.astype(F32),
                                      jnp.where(lane == FOX_C_LANE + 2, lo.astype(F32), dof)))
            doa_s[rows, :] = doa.astype(BF16)
            dq_s[rows, :] = jnp.zeros((blk, LANES), F32)

        @pl.when(i == j)
        def _():
            dk_s[...] = jnp.zeros_like(dk_s)
            dv_s[...] = jnp.zeros_like(dv_s)

        def tile(diagonal):
            q, k = q_ref[...], k_ref[...]
            s = _dot(q, k, 1, 1)
            if diagonal:
                s = _causal(s)
            p = jnp.exp(s)
            doa = doa_s[rows, :]
            ds = (p * _dot(doa, v_ref[...], 1, 1)).astype(BF16)
            dv_s[...] += _dot(p.astype(BF16), doa, 0, 0)
            dk_s[...] += _dot(ds, q, 0, 0)
            dq_s[rows, :] += _dot(ds, k, 1, 0)

        @pl.when(i > j)
        def _():
            tile(False)

        @pl.when(i == j)
        def _():
            tile(True)

        @pl.when(i == nb - 1)
        def _():
            dk = dk_s[...]
            dk_ref[...] = dk.astype(BF16)
            dv_ref[...] = dv_s[...].astype(BF16)
            cs_ref[...] = -_lane_col(dk, FOX_COLSUM_LANE)

        @pl.when(s_id == n_steps - 1)
        def _():
            dq = dq_s[...]
            dq_ref[...] = (dq * FOX_DIM ** -0.5).astype(BF16)
            rs_ref[...] = _lane_col(dq, FOX_ROWSUM_LANE)

    wide = jax.ShapeDtypeStruct((t, FOX_TILES), BF16)
    cols = jax.ShapeDtypeStruct((FOX_HEADS, t, 1), F32)
    return pl.pallas_call(
        body, name="fox_bwd", out_shape=(wide, wide, wide, cols, cols),
        grid_spec=pltpu.PrefetchScalarGridSpec(
            num_scalar_prefetch=2, grid=(FOX_HEADS, n_steps),
            in_specs=[q_spec, k_spec, k_spec, first_spec, first_spec],
            out_specs=(head_spec, k_spec, k_spec, head_col, k_col),
            scratch_shapes=[pltpu.VMEM((t, LANES), BF16), pltpu.VMEM((t, LANES), F32), pltpu.VMEM((blk, LANES), F32),
                            pltpu.VMEM((blk, LANES), F32)]),
        compiler_params=_params(("parallel", "arbitrary")),
    )(qi, kj, qa, ka, va, do, o)


def _merge_fwd(gm, bm, za, zb):
    t, d = za.shape
    tt = _tile(t, 256, 16)
    row = pl.BlockSpec((tt, d), lambda i: (i, 0))

    def body(gm_ref, b_ref, za_ref, zb_ref, o_ref):
        ga = _sigmoid(gm_ref[:, :d] + b_ref[:, :d])
        gb = _sigmoid(gm_ref[:, d:] + b_ref[:, d:])
        o_ref[...] = (ga * za_ref[...] + gb * zb_ref[...]).astype(BF16)

    return _pcall(body, name="merge_fwd", out_shape=jax.ShapeDtypeStruct((t, d), BF16), grid=(t // tt,),
                  in_specs=[pl.BlockSpec((tt, 2 * d), lambda i: (i, 0)), pl.BlockSpec((1, 2 * d), lambda i: (0, 0)), row, row],
                  out_specs=row, dims=("parallel",))(gm, bm, za, zb)


def _merge_bwd(dmix, gm, bm, za, zb):
    t, d = za.shape
    tt = _tile(t, 256, 16)
    row = pl.BlockSpec((tt, d), lambda i: (i, 0))
    wide = pl.BlockSpec((tt, 2 * d), lambda i: (i, 0))
    vec = pl.BlockSpec((1, 2 * d), lambda i: (0, 0))

    def body(dm_ref, gm_ref, b_ref, za_ref, zb_ref, dza_ref, dzb_ref, dgm_ref, db_ref):
        dm = dm_ref[...]
        ga = _sigmoid(gm_ref[:, :d] + b_ref[:, :d])
        gb = _sigmoid(gm_ref[:, d:] + b_ref[:, d:])
        dza_ref[...] = (dm * ga).astype(BF16)
        dzb_ref[...] = (dm * gb).astype(BF16)
        dla = dm * za_ref[...] * ga * (1.0 - ga)
        dlb = dm * zb_ref[...] * gb * (1.0 - gb)
        dgm_ref[:, :d] = dla.astype(BF16)
        dgm_ref[:, d:] = dlb.astype(BF16)
        pa = jnp.sum(dla, axis=0, keepdims=True)
        pb = jnp.sum(dlb, axis=0, keepdims=True)

        @pl.when(pl.program_id(0) == 0)
        def _():
            db_ref[:, :d] = pa
            db_ref[:, d:] = pb

        @pl.when(pl.program_id(0) > 0)
        def _():
            db_ref[:, :d] += pa
            db_ref[:, d:] += pb

    return _pcall(body, name="merge_bwd",
                  out_shape=(jax.ShapeDtypeStruct((t, d), BF16), jax.ShapeDtypeStruct((t, d), BF16),
                             jax.ShapeDtypeStruct((t, 2 * d), BF16), jax.ShapeDtypeStruct((1, 2 * d), F32)),
                  grid=(t // tt,), in_specs=[row, wide, vec, row, row], out_specs=(row, row, wide, vec),
                  dims=("arbitrary",))(dmix, gm, bm, za, zb)


def _ple_final(h3, pgl, pe, gain, target):
    t, d = h3.shape
    tt = _tile(t, 256, 16)
    row = pl.BlockSpec((tt, d), lambda i: (i, 0))
    vec = pl.BlockSpec((1, d), lambda i: (0, 0))
    lvec = pl.BlockSpec((1, LANES), lambda i: (0, 0))

    def body(h_ref, pgl_ref, pe_ref, g_ref, t_ref, dh_ref, dsg_ref, dpe_ref, loss_ref, dg_ref):
        pg = _sigmoid(pgl_ref[...])
        pe_v = pe_ref[...]
        h4 = h_ref[...] + pg * pe_v
        r = lax.rsqrt(jnp.mean(h4 * h4, axis=-1, keepdims=True) + EPS)
        gv = g_ref[...]
        err = h4 * r * gv - t_ref[...]
        part_loss = 0.5 * jnp.sum(jnp.mean(err * err, axis=-1, keepdims=True), axis=0, keepdims=True)
        dy = err * (1.0 / d)
        part_g = jnp.sum(dy * h4 * r, axis=0, keepdims=True)
        dyg = dy * gv
        dh = r * dyg - h4 * (r * r * r) * jnp.mean(dyg * h4, axis=-1, keepdims=True)
        dh_ref[...] = dh
        dsg_ref[...] = (dh * pe_v * pg * (1.0 - pg)).astype(BF16)
        dpe_ref[...] = (dh * pg).astype(BF16)

        @pl.when(pl.program_id(0) == 0)
        def _():
            loss_ref[...] = jnp.broadcast_to(part_loss, (1, LANES))
            dg_ref[...] = part_g

        @pl.when(pl.program_id(0) > 0)
        def _():
            loss_ref[...] += jnp.broadcast_to(part_loss, (1, LANES))
            dg_ref[...] += part_g

    return _pcall(body, name="ple_final",
                  out_shape=(jax.ShapeDtypeStruct((t, d), F32), jax.ShapeDtypeStruct((t, d), BF16),
                             jax.ShapeDtypeStruct((t, d), BF16), jax.ShapeDtypeStruct((1, LANES), F32),
                             jax.ShapeDtypeStruct((1, d), F32)),
                  grid=(t // tt,), in_specs=[row, row, row, vec, row], out_specs=(row, row, row, lvec, vec),
                  dims=("arbitrary",))(h3, pgl, pe, gain, target)


def _adamw_math(w, g, m, v):
    m = ADAM_B1 * m + (1.0 - ADAM_B1) * g
    v = ADAM_B2 * v + (1.0 - ADAM_B2) * (g * g)
    m_hat = m / (1.0 - ADAM_B1 ** ADAM_STEP)
    v_hat = v / (1.0 - ADAM_B2 ** ADAM_STEP)
    delta = -ADAM_LR * (m_hat / (jnp.sqrt(v_hat) + ADAM_EPS) + ADAM_WD * w)
    return delta, m, v


def _adamw(parts, w, m, v, name):
    n, r, c = parts.shape
    tr = _tile(r, 256, 16)
    row = pl.BlockSpec((tr, c), lambda i: (i, 0))

    def body(p_ref, w_ref, m_ref, v_ref, g_ref, d_ref, mo_ref, vo_ref):
        g = p_ref[0].astype(F32)
        for s in range(1, n):
            g = g + p_ref[s].astype(F32)
        g_ref[...] = g
        d_ref[...], mo_ref[...], vo_ref[...] = _adamw_math(w_ref[...], g, m_ref[...], v_ref[...])

    return _pcall(body, name=name, out_shape=(jax.ShapeDtypeStruct((r, c), F32),) * 4, grid=(r // tr,),
                  in_specs=[pl.BlockSpec((n, tr, c), lambda i: (0, i, 0)), row, row, row], out_specs=(row,) * 4,
                  dims=("parallel",))(parts, w, m, v)


ANY = pl.BlockSpec(memory_space=pl.ANY)


def _all_gather(shards):
    n = len(shards)

    def body(*refs):
        x_refs, out_refs = refs[:n], refs[n:2 * n]
        send_sems, recv_sems, local_sems = refs[2 * n:]
        x, y, cc = lax.axis_index("x"), lax.axis_index("y"), lax.axis_index("c")
        me, sibling = (x, y, cc), (x, y, 1 - cc)
        chips = [(1 - x, y), (x, 1 - y), (1 - x, 1 - y)]

        def slot(a, px, py, pc):
            return out_refs[a].at[4 * px + 2 * py + pc]

        def copy(a, k, block, to, src=None):
            return pltpu.make_async_remote_copy(
                src_ref=slot(a, *block) if src is None else src, dst_ref=slot(a, *block),
                send_sem=send_sems.at[7 * a + k], recv_sem=recv_sems.at[7 * a + k], device_id=to, device_id_type=MESH)

        local, sent = [], []
        for a in range(n):
            local.append(pltpu.make_async_copy(x_refs[a], slot(a, *me), local_sems.at[a]))
            sent.append(copy(a, 0, me, sibling, src=x_refs[a]))
            sent += [copy(a, 1 + j, me, (*chip, cc), src=x_refs[a]) for j, chip in enumerate(chips)]
        for cp in local + sent:
            cp.start()
        for j, chip in enumerate(chips):
            for a in range(n):
                copy(a, 1 + j, (*chip, cc), me).wait_recv()
                sent.append(copy(a, 4 + j, (*chip, cc), sibling))
                sent[-1].start()
        for a in range(n):
            copy(a, 0, sibling, me).wait_recv()
            for j, chip in enumerate(chips):
                copy(a, 4 + j, (*chip, 1 - cc), me).wait_recv()
        for cp in sent:
            cp.wait_send()
        for cp in local:
            cp.wait()

    return pl.pallas_call(
        body, name="weights_all_gather",
        out_shape=tuple(jax.ShapeDtypeStruct((N_DEV,) + s.shape, s.dtype) for s in shards),
        in_specs=[ANY] * n, out_specs=(ANY,) * n,
        scratch_shapes=[pltpu.SemaphoreType.DMA((7 * n,)), pltpu.SemaphoreType.DMA((7 * n,)),
                        pltpu.SemaphoreType.DMA((n,))],
    )(*shards)


def _reduce_scatter_exchange(blocks):
    n = len(blocks)

    def body(*refs):
        g_refs, recv_refs = refs[:n], refs[n:2 * n]
        send_sems, recv_sems, local_sems = refs[2 * n:]
        x, y, cc = lax.axis_index("x"), lax.axis_index("y"), lax.axis_index("c")
        me = 4 * x + 2 * y + cc
        local, sent, landing = [], [], []
        for a in range(n):
            local.append(pltpu.make_async_copy(g_refs[a].at[me], recv_refs[a].at[me], local_sems.at[a]))
        for k in range(1, N_DEV):
            px, py, pc = x ^ (k >> 2), y ^ ((k >> 1) & 1), cc ^ (k & 1)
            peer = 4 * px + 2 * py + pc
            for a in range(n):
                sems = dict(send_sem=send_sems.at[7 * a + k - 1], recv_sem=recv_sems.at[7 * a + k - 1],
                            device_id=(px, py, pc), device_id_type=MESH)
                sent.append(pltpu.make_async_remote_copy(src_ref=g_refs[a].at[peer], dst_ref=recv_refs[a].at[me], **sems))
                landing.append(pltpu.make_async_remote_copy(src_ref=g_refs[a].at[me], dst_ref=recv_refs[a].at[peer], **sems))
        for cp in local + sent:
            cp.start()
        for cp in landing:
            cp.wait_recv()
        for cp in sent:
            cp.wait_send()
        for cp in local:
            cp.wait()

    return pl.pallas_call(
        body, name="grads_reduce_scatter_exchange",
        out_shape=tuple(jax.ShapeDtypeStruct(b.shape, b.dtype) for b in blocks),
        in_specs=[ANY] * n, out_specs=(ANY,) * n,
        scratch_shapes=[pltpu.SemaphoreType.DMA((7 * n,)), pltpu.SemaphoreType.DMA((7 * n,)),
                        pltpu.SemaphoreType.DMA((n,))],
    )(*blocks)


HBM = pl.BlockSpec(memory_space=pltpu.HBM)
SEM = pl.BlockSpec(memory_space=pltpu.SEMAPHORE)
DATAFLOW = pltpu.SideEffectType.DATAFLOW_SIDE_EFFECTING


def _peers():
    x, y, cc = lax.axis_index("x"), lax.axis_index("y"), lax.axis_index("c")
    out = []
    for k in range(1, N_DEV):
        px, py, pc = x ^ (k >> 2), y ^ ((k >> 1) & 1), cc ^ (k & 1)
        out.append((k, (px, py, pc), 4 * px + 2 * py + pc))
    return 4 * x + 2 * y + cc, out


def _scatter_start(blocks, after, name, gather=False):
    n = len(blocks)
    lands = [lax.empty((N_DEV,) + b.shape if gather else b.shape, b.dtype) for b in blocks]

    def body(*refs):
        g_refs, land_refs = refs[:n], refs[n:2 * n]
        send_sems, recv_sems, token = refs[2 * n + 1], refs[2 * n + 2], refs[-1]
        me, peers = _peers()
        for k, peer, slot in peers:
            for a in range(n):
                pltpu.make_async_remote_copy(
                    src_ref=g_refs[a] if gather else g_refs[a].at[slot], dst_ref=land_refs[a].at[me],
                    send_sem=send_sems.at[7 * a + k - 1],
                    recv_sem=recv_sems.at[7 * a + k - 1], device_id=peer, device_id_type=MESH).start()
        token[...] = jnp.zeros_like(token)

    thru = [pltpu.HBM(b.shape, b.dtype) for b in blocks]
    thru_lands = [pltpu.HBM(b.shape, b.dtype) for b in lands]
    return pl.pallas_call(
        body, name=name,
        out_shape=(pltpu.SemaphoreType.DMA((7 * n,)), pltpu.SemaphoreType.DMA((7 * n,)), *thru, *thru_lands,
                   jax.ShapeDtypeStruct((8, LANES), F32)),
        in_specs=[HBM] * (2 * n) + [pl.BlockSpec(memory_space=pl.ANY)],
        out_specs=(SEM, SEM, *[HBM] * (2 * n), pl.BlockSpec(memory_space=pltpu.VMEM)),
        input_output_aliases={i: 2 + i for i in range(2 * n)},
        compiler_params=pltpu.CompilerParams(has_side_effects=DATAFLOW),
    )(*[pltpu.with_memory_space_constraint(a, pltpu.HBM) for a in list(blocks) + lands], after)


def _scatter_wait(started, after, name, gather=False):
    send_sems, recv_sems, *rest = started
    n = (len(rest) - 1) // 2
    thru = rest[:2 * n]

    def body(*refs):
        g_refs, land_refs = refs[:n], refs[n:2 * n]
        send_sems, recv_sems = refs[2 * n], refs[2 * n + 1]
        me, peers = _peers()
        for k, peer, slot in peers:
            for a in range(n):
                copy = pltpu.make_async_remote_copy(
                    src_ref=g_refs[a] if gather else g_refs[a].at[slot], dst_ref=land_refs[a].at[slot],
                    send_sem=send_sems.at[7 * a + k - 1],
                    recv_sem=recv_sems.at[7 * a + k - 1], device_id=peer, device_id_type=MESH)
                copy.wait_send()
                copy.wait_recv()

    out = pl.pallas_call(
        body, name=name, out_shape=tuple(pltpu.HBM(a.shape, a.dtype) for a in thru),
        in_specs=[HBM] * (2 * n) + [SEM, SEM, pl.BlockSpec(memory_space=pl.ANY)], out_specs=tuple([HBM] * (2 * n)),
        input_output_aliases={i: i for i in range(2 * n)},
        compiler_params=pltpu.CompilerParams(has_side_effects=DATAFLOW),
    )(*thru, send_sems, recv_sems, after)
    return out[:n], out[n:]


BIG = (("w_ffn1_gate", "colT"), ("w_ffn1_up", "colT"), ("w_ffn1_down", "row"), ("w_ffn2_gate", "colT"),
       ("w_ffn2_up", "colT"), ("w_ffn2_down", "row"), ("w_in", "colT"), ("w_merge", "col"), ("w_ret_out", "col"),
       ("w_fox_out", "col"), ("w_out", "row"), ("w_ple", "col"), ("w_ple_gate", "row"))


def _shard_view(a, kind):
    a = a.reshape(a.shape[-2:])
    return a.T if kind == "colT" else a


def _unview(a, kind, shape):
    return (a.T if kind == "colT" else a).reshape(shape)


def _full_from_slots(g, kind):
    n, r, c = g.shape
    return g.transpose(1, 0, 2).reshape(r, n * c) if kind == "col" else g.reshape(n * r, c)


def _slots_from_full(f, kind):
    r, c = f.shape
    return f.reshape(r, N_DEV, c // N_DEV).transpose(1, 0, 2) if kind == "col" else f.reshape(N_DEV, r // N_DEV, c)


EARLY_GROUPS = (("w_ple_gate", "w_ple", "w_ffn2_down", "w_ffn2_gate", "w_ffn2_up", "w_out", "w_ret_out", "w_fox_out"),
                ("w_in", "w_merge"))


def _scatter_group(gw, names, after, name):
    kind = dict(BIG)
    return names, _scatter_start([_slots_from_full(gw[n], kind[n]) for n in names], after, name)


def _pad_heads(w):
    d = w.shape[1]
    return jnp.pad(w.reshape(FOX_HEADS, FOX_DIM, d), ((0, 0), (0, LANES - FOX_DIM), (0, 0))).reshape(FOX_TILES, d)


def _unpad_heads(w):
    d = w.shape[1]
    return w.reshape(FOX_HEADS, LANES, d)[:, :FOX_DIM].reshape(FOX_WIDTH, d)


def _deinterleave_rows(w):
    d = w.shape[1]
    return w.reshape(RET_HEADS, RET_DIM // 2, 2, d).transpose(0, 2, 1, 3).reshape(RET_WIDTH, d)


def _interleave_rows(w):
    d = w.shape[1]
    return w.reshape(RET_HEADS, 2, RET_DIM // 2, d).transpose(0, 2, 1, 3).reshape(RET_WIDTH, d)


def _pad_w_in(wt):
    d = wt.shape[1]
    rw, fw = RET_WIDTH, FOX_WIDTH
    fo = 4 * rw
    return jnp.concatenate([
        _deinterleave_rows(wt[:rw]), _deinterleave_rows(wt[rw:2 * rw]), wt[2 * rw:4 * rw],
        _pad_heads(wt[fo:fo + fw]), _pad_heads(wt[fo + fw:fo + 2 * fw]), _pad_heads(wt[fo + 2 * fw:fo + 3 * fw]),
        wt[fo + 3 * fw:], jnp.zeros((2 * LANES - FOX_HEADS, d), wt.dtype)], axis=0)


def _unpad_w_in(g):
    rw = RET_WIDTH
    f0 = 4 * rw
    return jnp.concatenate([
        _interleave_rows(g[:rw]), _interleave_rows(g[rw:2 * rw]), g[2 * rw:4 * rw],
        _unpad_heads(g[f0:f0 + FOX_TILES]), _unpad_heads(g[f0 + FOX_TILES:f0 + 2 * FOX_TILES]),
        _unpad_heads(g[f0 + 2 * FOX_TILES:f0 + 3 * FOX_TILES]),
        g[f0 + 3 * FOX_TILES:f0 + 3 * FOX_TILES + FOX_HEADS]], axis=0)


SMALL = ("ln_ffn1", "ln_mix", "b_forget", "b_merge", "ln_ffn2", "ln_ple", "ln_final")


def _small_rows(n):
    rows = -(-n // LANES)
    return -(-rows // 8) * 8


def _pack_small(vals, with_loss=None):
    parts = []
    for name in SMALL:
        v = vals[name].reshape(-1).astype(F32)
        rows = _small_rows(v.shape[0])
        parts.append(jnp.pad(v, (0, rows * LANES - v.shape[0])).reshape(rows, LANES))
    if with_loss is not None:
        parts.append(jnp.pad(with_loss.reshape(1, LANES), ((0, 7), (0, 0))))
    else:
        parts.append(jnp.zeros((8, LANES), F32))
    return jnp.concatenate(parts, axis=0)


def _unpack_small(packed, shapes):
    out, at = {}, 0
    for name in SMALL:
        n = int(np.prod(shapes[name]))
        rows = _small_rows(n)
        out[name] = packed[at:at + rows].reshape(-1)[:n].reshape(shapes[name])
        at += rows
    return out, packed[at, 0]


def _gather_finish(group, after, me):
    names, started, wait_name = group
    kind = dict(BIG)
    sent, landed = _scatter_wait(started, after, wait_name, gather=True)
    return {n: _full_from_slots(lax.dynamic_update_slice_in_dim(land, shard[None], me, 0), kind[n])
            for n, shard, land in zip(names, sent, landed)}


def _local_step(x, p, positions, target, w, small, me, entry_token, gathers):
    t, d = x.shape
    gain = lambda n: small[n].reshape(1, d)
    w = dict(w)
    bpad = jnp.pad(small["b_forget"].reshape(1, FOX_HEADS), ((0, 0), (0, LANES - FOX_HEADS)))
    bm = small["b_merge"].reshape(1, 2 * d)
    fox_blk = _tile(t, 1024, 128)

    def ffn_fwd(n, tag, down_gather=None):
        g = _mm([(n, w[f"w_{tag}_gate"], "nt")], BF16, f"{tag}_gate", tn=1408)
        u = _mm([(n, w[f"w_{tag}_up"], "nt")], BF16, f"{tag}_up", tn=1408)
        a = _swiglu_fwd(g, u, f"{tag}_swiglu")
        if down_gather is not None:
            w.update(_gather_finish(down_gather, a, me))
        return g, u, a, _mm([(a, w[f"w_{tag}_down"], "nn")], F32, f"{tag}_down")

    n1 = _rms_fwd(x, gain("ln_ffn1") + entry_token, "rms_ffn1")
    g1, u1, a1, f1 = ffn_fwd(n1, "ffn1", down_gather=gathers[0])
    h1, u = _rms_fwd(x, gain("ln_mix"), "rms_mix", f=f1)
    w.update(_gather_finish(gathers[1], f1, me))
    w_in_t = _pad_w_in(w["w_in"])
    gm = _mm([(u, w["w_merge"], "nn")], F32, "mixer_gates")
    pm = _mm([(u, w_in_t, "nt")], F32, "mixer_in", tn=1792)

    half = jnp.arange(RET_DIM // 2, dtype=F32) / (RET_DIM // 2)
    inv = 1.0 / (ROPE_BASE ** half)
    inv2 = jnp.concatenate([inv, inv]).reshape(1, RET_DIM)
    sign2 = jnp.concatenate([-jnp.ones((RET_DIM // 2,), F32), jnp.ones((RET_DIM // 2,), F32)]).reshape(1, RET_DIM)
    cos2, sin2 = _rope_tables(positions.reshape(t, 1), inv2, sign2)
    consts = _ret_consts()
    y_ret, y_raw, states = _ret_fwd(pm, cos2, sin2, consts)
    w.update(_gather_finish(gathers[2], y_raw, me))
    w_fox_pad = _pad_heads(w["w_fox_out"])
    za = _mm([(y_ret, w["w_ret_out"], "nn")], F32, "ret_out")

    qa, ka, va = _fox_prep(pm, bpad)
    o_fox, y_fox, qa_b = _fox_fwd(qa, ka, va, fox_blk)
    zb = _mm([(y_fox, w_fox_pad, "nn")], F32, "fox_out")

    mix = _merge_fwd(gm, bm, za, zb)
    mo = _mm([(mix, w["w_out"], "nn")], F32, "mix_out")
    h2, n2 = _rms_fwd(h1, gain("ln_ffn2"), "rms_ffn2", f=mo, scale=1.0)
    g2, u2, a2, f2 = ffn_fwd(n2, "ffn2")
    h3, n3 = _rms_fwd(h2, gain("ln_ple"), "rms_ple", f=f2)
    pgl = _mm([(n3, w["w_ple_gate"], "nn")], F32, "ple_gate")
    pb = p.astype(BF16)
    pe = _mm([(pb, w["w_ple"], "nn")], F32, "ple_embed")

    gw, gs = {}, {}
    dh4, dsg, dpe, loss, gs["ln_final"] = _ple_final(h3, pgl, pe, gain("ln_final"), target)
    gw["w_ple_gate"] = _mm([(n3, dsg, "tn")], BF16, "d_w_ple_gate", tn=256)
    gw["w_ple"] = _mm([(pb, dpe, "tn")], BF16, "d_w_ple", tn=256)
    dn3 = _mm([(dsg, w["w_ple_gate"], "nt")], F32, "d_n3")
    dh3, dh3_half, gs["ln_ple"] = _rms_bwd(dn3, h3, gain("ln_ple"), dh4, "rms_ple_bwd", 0.5)

    def ffn_bwd(dh_half, g, u_, a, n, tag, scatter_now=None):
        gw[f"w_{tag}_down"] = _mm([(a, dh_half, "tn")], BF16, f"d_w_{tag}_down", tm=1408, tn=256)
        start = scatter_now if scatter_now is not None else (lambda *_: None)
        token = start((f"w_{tag}_down",), dh_half, "c")
        da = _mm([(dh_half, w[f"w_{tag}_down"], "nt")], BF16, f"d_a_{tag}", tn=1408, after=token)
        dg, du_ = _swiglu_bwd(da, g, u_, f"{tag}_swiglu_bwd")
        gw[f"w_{tag}_gate"] = _mm([(dg, n, "tn")], BF16, f"d_w_{tag}_gate", tm=1408, tn=256)
        token = start((f"w_{tag}_gate",), da, "d")
        gw[f"w_{tag}_up"] = _mm([(du_, n, "tn")], BF16, f"d_w_{tag}_up", tm=1408, tn=256, after=token)
        token = start((f"w_{tag}_up",), da, "e")
        return _mm([(dg, w[f"w_{tag}_gate"], "nn"), (du_, w[f"w_{tag}_up"], "nn")], F32, f"d_n_{tag}", tm=512,
                   after=token)

    dn2 = ffn_bwd(dh3_half, g2, u2, a2, n2, "ffn2")
    dh2, dh2_b, gs["ln_ffn2"] = _rms_bwd(dn2, h2, gain("ln_ffn2"), dh3, "rms_ffn2_bwd", 1.0)

    gw["w_out"] = _mm([(mix, dh2_b, "tn")], BF16, "d_w_out", tn=256)
    dmix = _mm([(dh2_b, w["w_out"], "nt")], F32, "d_mix")
    dza, dzb, dgm, gs["b_merge"] = _merge_bwd(dmix, gm, bm, za, zb)
    gw["w_ret_out"] = _mm([(y_ret, dza, "tn")], BF16, "d_w_ret_out", tn=256)
    gw["w_fox_out"] = _unpad_heads(_mm([(y_fox, dzb, "tn")], BF16, "d_w_fox_out", tn=256))
    dy_ret = _mm([(dza, w["w_ret_out"], "nt")], F32, "d_y_ret")
    do_fox = _mm([(dzb, w_fox_pad, "nt")], F32, "d_y_fox")

    pending = [_scatter_group(gw, EARLY_GROUPS[0], dy_ret, "grads_scatter_a_start")]
    token = pending[0][1][-1][0, 0]
    drq, drk, drv, drg = _ret_bwd(dy_ret, pm, cos2, sin2, y_raw, states, consts[:3] + (consts[3] + token,))

    dqa, dka, dva, ds_rows, ds_cols = _fox_bwd(qa_b, ka, va, do_fox, o_fox, fox_blk)
    dc = jnp.pad((ds_rows + ds_cols).reshape(FOX_HEADS, t).T, ((0, 0), (0, LANES - FOX_HEADS)))
    dff, db_forget = _fox_post(dc, pm, bpad)
    gs["b_forget"] = db_forget[:, :FOX_HEADS]

    dpm = jnp.concatenate([drq, drk, drv, drg, dqa, dka, dva, dff, jnp.zeros((t, LANES), BF16)], axis=1)
    gw["w_merge"] = _mm([(u, dgm, "tn")], BF16, "d_w_merge", tn=512)
    gw["w_in"] = _unpad_w_in(_mm([(dpm, u, "tn")], BF16, "d_w_in", tm=1792, tn=256))
    du = _mm([(dpm, w_in_t, "nn"), (dgm, w["w_merge"], "nt")], F32, "d_u", tm=1024, tn=512)
    pending.append(_scatter_group(gw, EARLY_GROUPS[1], du, "grads_scatter_b_start"))
    token = pending[1][1][-1][0:1, 0:1]
    dh1, dh1_half, gs["ln_mix"] = _rms_bwd(du, h1, gain("ln_mix") + token, dh2, "rms_mix_bwd", 0.5)

    def scatter_now(names, after, tag):
        pending.append(_scatter_group(gw, names, after, f"grads_scatter_{tag}_start"))
        return pending[-1][1][-1]

    dn1 = ffn_bwd(dh1_half, g1, u1, a1, n1, "ffn1", scatter_now=scatter_now)
    dx, _, gs["ln_ffn1"] = _rms_bwd(dn1, x, gain("ln_ffn1"), dh1, "rms_ffn1_bwd", 1.0)
    return loss, dx, gw, gs, pending


WEIGHTS = ("ln_ffn1", "w_ffn1_gate", "w_ffn1_up", "w_ffn1_down", "ln_mix", "w_in", "b_forget", "w_merge", "b_merge",
           "w_ret_out", "w_fox_out", "w_out", "ln_ffn2", "w_ffn2_gate", "w_ffn2_up", "w_ffn2_down", "ln_ple", "w_ple",
           "w_ple_gate", "ln_final")


def kernel(x, p, positions, ln_ffn1, w_ffn1_gate, w_ffn1_up, w_ffn1_down, ln_mix, w_in, b_forget, w_merge, b_merge, w_ret_out, w_fox_out, w_out, ln_ffn2, w_ffn2_gate, w_ffn2_up, w_ffn2_down, ln_ple, w_ple, w_ple_gate, ln_final, loss_target, m_ln_ffn1, m_w_ffn1_gate, m_w_ffn1_up, m_w_ffn1_down, m_ln_mix, m_w_in, m_b_forget, m_w_merge, m_b_merge, m_w_ret_out, m_w_fox_out, m_w_out, m_ln_ffn2, m_w_ffn2_gate, m_w_ffn2_up, m_w_ffn2_down, m_ln_ple, m_w_ple, m_w_ple_gate, m_ln_final, v_ln_ffn1, v_w_ffn1_gate, v_w_ffn1_up, v_w_ffn1_down, v_ln_mix, v_w_in, v_b_forget, v_w_merge, v_b_merge, v_w_ret_out, v_w_fox_out, v_w_out, v_ln_ffn2, v_w_ffn2_gate, v_w_ffn2_up, v_w_ffn2_down, v_ln_ple, v_w_ple, v_w_ple_gate, v_ln_final):
    args = dict(ln_ffn1=ln_ffn1, w_ffn1_gate=w_ffn1_gate, w_ffn1_up=w_ffn1_up, w_ffn1_down=w_ffn1_down, ln_mix=ln_mix, w_in=w_in, b_forget=b_forget, w_merge=w_merge, b_merge=b_merge, w_ret_out=w_ret_out, w_fox_out=w_fox_out, w_out=w_out, ln_ffn2=ln_ffn2, w_ffn2_gate=w_ffn2_gate, w_ffn2_up=w_ffn2_up, w_ffn2_down=w_ffn2_down, ln_ple=ln_ple, w_ple=w_ple, w_ple_gate=w_ple_gate, ln_final=ln_final)
    moms = dict(ln_ffn1=m_ln_ffn1, w_ffn1_gate=m_w_ffn1_gate, w_ffn1_up=m_w_ffn1_up, w_ffn1_down=m_w_ffn1_down, ln_mix=m_ln_mix, w_in=m_w_in, b_forget=m_b_forget, w_merge=m_w_merge, b_merge=m_b_merge, w_ret_out=m_w_ret_out, w_fox_out=m_w_fox_out, w_out=m_w_out, ln_ffn2=m_ln_ffn2, w_ffn2_gate=m_w_ffn2_gate, w_ffn2_up=m_w_ffn2_up, w_ffn2_down=m_w_ffn2_down, ln_ple=m_ln_ple, w_ple=m_w_ple, w_ple_gate=m_w_ple_gate, ln_final=m_ln_final)
    vars_ = dict(ln_ffn1=v_ln_ffn1, w_ffn1_gate=v_w_ffn1_gate, w_ffn1_up=v_w_ffn1_up, w_ffn1_down=v_w_ffn1_down, ln_mix=v_ln_mix, w_in=v_w_in, b_forget=v_b_forget, w_merge=v_w_merge, b_merge=v_b_merge, w_ret_out=v_w_ret_out, w_fox_out=v_w_fox_out, w_out=v_w_out, ln_ffn2=v_ln_ffn2, w_ffn2_gate=v_w_ffn2_gate, w_ffn2_up=v_w_ffn2_up, w_ffn2_down=v_w_ffn2_down, ln_ple=v_ln_ple, w_ple=v_w_ple, w_ple_gate=v_w_ple_gate, ln_final=v_ln_final)
    kinds = ("grad", "delta", "new_m", "new_v")

    me = 4 * lax.axis_index("x") + 2 * lax.axis_index("y") + lax.axis_index("c")
    kind_of = dict(BIG)
    shard = {n: _shard_view(args[n], kind).astype(BF16) for n, kind in BIG}
    first = ("w_ffn1_gate", "w_ffn1_up")
    groups = (("w_ffn1_down",), ("w_in", "w_merge"))
    groups += (tuple(n for n, _ in BIG if n not in first + groups[0] + groups[1]),)
    gathered = _all_gather([shard[n] for n in first])
    w_full = {n: _full_from_slots(g, kind_of[n]) for n, g in zip(first, gathered)}
    gathers, after = [], gathered[0]
    for tag, names in zip("zab", groups):
        started = _scatter_start([shard[n] for n in names], after, f"weights_gather_{tag}_start", gather=True)
        gathers.append((names, started, f"weights_gather_{tag}_wait"))
        after = started[-1]

    small = {n: args[n] for n in SMALL}
    loss_part, dx, gw, gs, pending = _local_step(x[0], p[0, 0], positions[0], loss_target[0], w_full, small, me,
                                                 after[0:1, 0:1], gathers)

    parts_of = {}
    for tag, (names, started) in zip("abcde", pending):
        sent, landed = _scatter_wait(started, dx, f"grads_scatter_{tag}_wait")
        for n, blk, land in zip(names, sent, landed):
            own = lax.dynamic_index_in_dim(blk, me, 0, keepdims=True)
            parts_of[n] = lax.dynamic_update_slice_in_dim(land, own, me, 0)
    late = [(n, kind) for n, kind in BIG if n not in parts_of]
    small_part = _pack_small(gs, with_loss=loss_part)
    blocks = [_slots_from_full(gw[n], kind) for n, kind in late]
    blocks.append(jnp.broadcast_to(small_part, (N_DEV,) + small_part.shape))
    recv = _reduce_scatter_exchange(blocks)
    parts_of.update({n: r for (n, _), r in zip(late, recv)})

    res = {}
    for n, kind in BIG:
        parts = parts_of[n]
        outs = _adamw(parts, _shard_view(args[n], kind), _shard_view(moms[n], kind), _shard_view(vars_[n], kind),
                      f"adamw_{n}")
        for what, o in zip(kinds, outs):
            res[(what, n)] = _unview(o, kind, args[n].shape)
    s_outs = _adamw(recv[-1], _pack_small(small), _pack_small({n: moms[n] for n in SMALL}),
                    _pack_small({n: vars_[n] for n in SMALL}), "adamw_small")
    for what, sm in zip(kinds, s_outs):
        svals, extra = _unpack_small(sm, {n: args[n].shape for n in SMALL})
        if what == "grad":
            loss = extra
        for n in SMALL:
            res[(what, n)] = svals[n]
    return (loss, dx[None], *[res[(what, n)] for what in kinds for n in WEIGHTS])
```

```python
import numpy as np
import jax
import jax.numpy as jnp
from jax import lax
from jax.experimental import pallas as pl
from jax.experimental.pallas import tpu as pltpu

F32 = jnp.float32
BF16 = jnp.bfloat16

N_DEV = 8
EPS = 1e-6
RET_HEADS = 4
RET_DIM = 128
RET_WIDTH = RET_HEADS * RET_DIM
FOX_HEADS = 8
FOX_DIM = 64
FOX_WIDTH = FOX_HEADS * FOX_DIM
CHUNK = 128
ROPE_BASE = 10000.0
LANES = 128
FOX_TILES = FOX_HEADS * LANES
IN_COLS = 4 * RET_WIDTH + 3 * FOX_WIDTH + FOX_HEADS
IN_PAD = 4 * RET_WIDTH + 3 * FOX_TILES + 2 * LANES
TILE_RQ, TILE_RK, TILE_RV, TILE_RG = 0, 4, 8, 12
TILE_FQ, TILE_FK, TILE_FV, TILE_FF = 16, 24, 32, 40
NEG = -1e30

ADAM_LR = 0.001
ADAM_B1 = 0.9
ADAM_B2 = 0.999
ADAM_EPS = 1e-08
ADAM_WD = 0.01
ADAM_STEP = 10

VMEM_LIMIT_BYTES = 56 * 1024 * 1024

MESH = pl.DeviceIdType.MESH


def _tile(dim, pref, mult):
    if dim <= pref:
        return dim
    t = (pref // mult) * mult
    while t >= mult:
        if dim % t == 0:
            return t
        t -= mult
    return dim


def _params(dims):
    return pltpu.CompilerParams(dimension_semantics=dims, vmem_limit_bytes=VMEM_LIMIT_BYTES)


def _pcall(body, *, name, out_shape, grid, in_specs, out_specs, scratch_shapes=(), dims=None):
    return pl.pallas_call(body, name=name, out_shape=out_shape, grid=grid, in_specs=in_specs, out_specs=out_specs,
                          scratch_shapes=list(scratch_shapes), compiler_params=_params(dims))


def _dot(a, b, ca, cb):
    return lax.dot_general(a, b, (((ca,), (cb,)), ((), ())), preferred_element_type=F32)


def _sigmoid(x):
    return 1.0 / (1.0 + jnp.exp(-x))


def _mm(pairs, out_dtype, name, tm=1024, tn=1024, after=None):
    dims = []
    for a, b, mode in pairs:
        m, k = (a.shape[1], a.shape[0]) if mode == "tn" else a.shape
        n, k2 = b.shape if mode == "nt" else (b.shape[1], b.shape[0])
        assert k == k2, (name, a.shape, b.shape, mode)
        dims.append((m, n))
    assert all(d == dims[0] for d in dims), (name, dims)
    m, n = dims[0]
    tm = _tile(m, tm, 128 if any(mode == "tn" for _, _, mode in pairs) else 16)
    tn = _tile(n, tn, 128)
    in_specs, contract, operands = [], [], []
    for a, b, mode in pairs:
        k = a.shape[0] if mode == "tn" else a.shape[1]
        in_specs.append(pl.BlockSpec((k, tm), lambda i, j: (0, i)) if mode == "tn" else
                        pl.BlockSpec((tm, k), lambda i, j: (i, 0)))
        in_specs.append(pl.BlockSpec((tn, k), lambda i, j: (j, 0)) if mode == "nt" else
                        pl.BlockSpec((k, tn), lambda i, j: (0, j)))
        contract.append((0 if mode == "tn" else 1, 1 if mode == "nt" else 0))
        operands += [a, b]
    if after is not None:
        in_specs.append(pl.BlockSpec(memory_space=pl.ANY))
        operands.append(after)

    def body(*refs):
        o_ref = refs[-1]
        acc = None
        for p, (ca, cb) in enumerate(contract):
            part = _dot(refs[2 * p][...], refs[2 * p + 1][...], ca, cb)
            acc = part if acc is None else acc + part
        o_ref[...] = acc.astype(out_dtype)

    return _pcall(body, name=name, out_shape=jax.ShapeDtypeStruct((m, n), out_dtype), grid=(m // tm, n // tn),
                  in_specs=in_specs, out_specs=pl.BlockSpec((tm, tn), lambda i, j: (i, j)),
                  dims=("parallel", "parallel"))(*operands)


def _rms_fwd(h, gain, name, f=None, scale=0.5):
    t, d = h.shape
    tt = _tile(t, 512, 16)
    row = pl.BlockSpec((tt, d), lambda i: (i, 0))
    vec = pl.BlockSpec((1, d), lambda i: (0, 0))

    def norm(hv, g_ref, n_ref):
        r = lax.rsqrt(jnp.mean(hv * hv, axis=-1, keepdims=True) + EPS)
        n_ref[...] = (hv * r * g_ref[...]).astype(BF16)

    if f is None:

        def body(h_ref, g_ref, n_ref):
            norm(h_ref[...], g_ref, n_ref)

        return _pcall(body, name=name, out_shape=jax.ShapeDtypeStruct((t, d), BF16), grid=(t // tt,),
                      in_specs=[row, vec], out_specs=row, dims=("parallel",))(h, gain)

    def body(h_ref, f_ref, g_ref, hn_ref, n_ref):
        hv = h_ref[...] + scale * f_ref[...]
        hn_ref[...] = hv
        norm(hv, g_ref, n_ref)

    return _pcall(body, name=name,
                  out_shape=(jax.ShapeDtypeStruct((t, d), F32), jax.ShapeDtypeStruct((t, d), BF16)),
                  grid=(t // tt,), in_specs=[row, row, vec], out_specs=(row, row), dims=("parallel",))(h, f, gain)


def _rms_bwd(dn, h, gain, dh_in, name, out_scale):
    t, d = h.shape
    tt = _tile(t, 512, 16)
    row = pl.BlockSpec((tt, d), lambda i: (i, 0))
    vec = pl.BlockSpec((1, d), lambda i: (0, 0))

    def body(dn_ref, h_ref, g_ref, dhin_ref, dh_ref, dhb_ref, dg_ref):
        hv = h_ref[...]
        dnv = dn_ref[...].astype(F32)
        r = lax.rsqrt(jnp.mean(hv * hv, axis=-1, keepdims=True) + EPS)
        dng = dnv * g_ref[...]
        dh = dhin_ref[...] + r * dng - hv * (r * r * r) * jnp.mean(dng * hv, axis=-1, keepdims=True)
        dh_ref[...] = dh
        dhb_ref[...] = (out_scale * dh).astype(BF16)
        part = jnp.sum(dnv * hv * r, axis=0, keepdims=True)

        @pl.when(pl.program_id(0) == 0)
        def _():
            dg_ref[...] = part

        @pl.when(pl.program_id(0) > 0)
        def _():
            dg_ref[...] += part

    return _pcall(body, name=name,
                  out_shape=(jax.ShapeDtypeStruct((t, d), F32), jax.ShapeDtypeStruct((t, d), BF16),
                             jax.ShapeDtypeStruct((1, d), F32)),
                  grid=(t // tt,), in_specs=[row, row, vec, row], out_specs=(row, row, vec),
                  dims=("arbitrary",))(dn, h, gain, dh_in)


def _swiglu_fwd(g, u, name):
    t, f = g.shape
    tt = _tile(t, 256, 16)
    row = pl.BlockSpec((tt, f), lambda i: (i, 0))

    def body(g_ref, u_ref, a_ref):
        gv = g_ref[...].astype(F32)
        a_ref[...] = (gv * _sigmoid(gv) * u_ref[...].astype(F32)).astype(BF16)

    return _pcall(body, name=name, out_shape=jax.ShapeDtypeStruct((t, f), BF16), grid=(t // tt,),
                  in_specs=[row, row], out_specs=row, dims=("parallel",))(g, u)


def _swiglu_bwd(da, g, u, name):
    t, f = g.shape
    tt = _tile(t, 256, 16)
    row = pl.BlockSpec((tt, f), lambda i: (i, 0))

    def body(da_ref, g_ref, u_ref, dg_ref, du_ref):
        gv = g_ref[...].astype(F32)
        dav = da_ref[...].astype(F32)
        sg = _sigmoid(gv)
        dg_ref[...] = (dav * u_ref[...].astype(F32) * (sg * (1.0 + gv * (1.0 - sg)))).astype(BF16)
        du_ref[...] = (dav * (gv * sg)).astype(BF16)

    return _pcall(body, name=name, out_shape=(jax.ShapeDtypeStruct((t, f), BF16),) * 2, grid=(t // tt,),
                  in_specs=[row, row, row], out_specs=(row, row), dims=("parallel",))(da, g, u)


def _rope_tables(pos_col, inv2, sign2):
    t = pos_col.shape[0]

    def body(p_ref, inv_ref, sg_ref, c_ref, s_ref):
        ang = p_ref[...].astype(F32) * inv_ref[...]
        c_ref[...] = jnp.cos(ang)
        s_ref[...] = jnp.sin(ang) * sg_ref[...]

    full = lambda shape: pl.BlockSpec(shape, lambda i: (0, 0))
    return _pcall(body, name="rope_tables", out_shape=(jax.ShapeDtypeStruct((t, RET_DIM), F32),) * 2, grid=(1,),
                  in_specs=[full((t, 1)), full((1, RET_DIM)), full((1, RET_DIM))],
                  out_specs=(full((t, RET_DIM)),) * 2, dims=("arbitrary",))(pos_col, inv2, sign2)


def _rot(x, c, s):
    return x * c + pltpu.roll(x, RET_DIM // 2, 1) * s


def _rot_t(g, c, s):
    return g * c + pltpu.roll(g * s, RET_DIM // 2, 1)


def _ret_consts():
    hh = np.arange(RET_HEADS, dtype=np.float32)
    log_gamma = np.log1p(-np.exp2(-5.0 - hh)).astype(np.float32)
    idx = np.arange(CHUNK, dtype=np.float32)
    diff = idx[:, None] - idx[None, :]
    dmask = np.where(diff >= 0, np.exp(log_gamma[:, None, None] * np.maximum(diff, 0.0)), 0.0).astype(np.float32)
    kdec = np.exp(log_gamma[:, None] * (CHUNK - 1 - idx)).astype(np.float32)
    qdec = np.exp(log_gamma[:, None] * (idx + 1.0)).astype(np.float32)
    cdec = np.exp(log_gamma * CHUNK).astype(np.float32)
    bc = lambda v: np.ascontiguousarray(np.broadcast_to(v[:, :, None], (RET_HEADS, CHUNK, RET_DIM)))
    cd = np.ascontiguousarray(np.broadcast_to(cdec[:, None, None], (RET_HEADS, 8, RET_DIM)))
    return jnp.asarray(dmask), jnp.asarray(bc(qdec)), jnp.asarray(bc(kdec)), jnp.asarray(cd)


def _ret_fwd(pm, cos2, sin2, consts):
    t = pm.shape[0]
    n_chunks = t // CHUNK
    dmask, qdec, kdec, cd = consts
    scale = RET_DIM ** -0.5
    wide = lambda c0: pl.BlockSpec((CHUNK, RET_WIDTH), lambda n: (n, c0 // RET_HEADS))
    tab = pl.BlockSpec((CHUNK, RET_DIM), lambda n: (n, 0))
    const = lambda a: pl.BlockSpec(a.shape, lambda n: (0,) * a.ndim)

    def body(q_ref, k_ref, v_ref, g_ref, c_ref, s_ref, dm_ref, qd_ref, kd_ref, cd_ref, y_ref, raw_ref, st_ref, s_acc):
        @pl.when(pl.program_id(0) == 0)
        def _():
            s_acc[...] = jnp.zeros_like(s_acc)

        c, s = c_ref[...], s_ref[...]
        for h in range(RET_HEADS):
            hs = slice(h * RET_DIM, (h + 1) * RET_DIM)
            q = _rot(q_ref[:, hs], c, s)
            k = _rot(k_ref[:, hs], c, s) * scale
            vb = v_ref[:, hs].astype(BF16)
            g = g_ref[:, hs]
            s_in = s_acc[h]
            st_ref[h] = s_in
            a = _dot(q.astype(BF16), k.astype(BF16), 1, 1) * dm_ref[h]
            y = _dot(a.astype(BF16), vb, 1, 0) + _dot((q * qd_ref[h]).astype(BF16), s_in.astype(BF16), 1, 0)
            s_acc[h] = cd_ref[h, 0:1, :] * s_in + _dot((k * kd_ref[h]).astype(BF16), vb, 0, 0)
            raw_ref[:, hs] = y
            mu = jnp.mean(y, axis=-1, keepdims=True)
            yc = y - mu
            rs = lax.rsqrt(jnp.mean(yc * yc, axis=-1, keepdims=True) + EPS)
            y_ref[:, hs] = (yc * rs * (g * _sigmoid(g))).astype(BF16)

    out_blk = pl.BlockSpec((CHUNK, RET_WIDTH), lambda n: (n, 0))
    return _pcall(
        body, name="retention_fwd",
        out_shape=(jax.ShapeDtypeStruct((t, RET_WIDTH), BF16), jax.ShapeDtypeStruct((t, RET_WIDTH), F32),
                   jax.ShapeDtypeStruct((RET_HEADS, n_chunks, RET_DIM, RET_DIM), F32)),
        grid=(n_chunks,),
        in_specs=[wide(TILE_RQ), wide(TILE_RK), wide(TILE_RV), wide(TILE_RG), tab, tab,
                  const(dmask), const(qdec), const(kdec), const(cd)],
        out_specs=(out_blk, out_blk, pl.BlockSpec((RET_HEADS, None, RET_DIM, RET_DIM), lambda n: (0, n, 0, 0))),
        scratch_shapes=[pltpu.VMEM((RET_HEADS, RET_DIM, RET_DIM), F32)],
        dims=("arbitrary",),
    )(pm, pm, pm, pm, cos2, sin2, dmask, qdec, kdec, cd)


def _ret_bwd(dy, pm, cos2, sin2, raw, states, consts):
    t = pm.shape[0]
    n_chunks = t // CHUNK
    dmask, qdec, kdec, cd = consts
    scale = RET_DIM ** -0.5
    rev = lambda n: n_chunks - 1 - n
    wide = lambda c0: pl.BlockSpec((CHUNK, RET_WIDTH), lambda n: (rev(n), c0 // RET_HEADS))
    tab = pl.BlockSpec((CHUNK, RET_DIM), lambda n: (rev(n), 0))
    blk = pl.BlockSpec((CHUNK, RET_WIDTH), lambda n: (rev(n), 0))
    const = lambda a: pl.BlockSpec(a.shape, lambda n: (0,) * a.ndim)

    def body(dy_ref, q_ref, k_ref, v_ref, g_ref, c_ref, s_ref, raw_ref, st_ref, dm_ref, qd_ref, kd_ref, cd_ref,
             dq_ref, dk_ref, dv_ref, dg_ref, ds_acc):
        @pl.when(pl.program_id(0) == 0)
        def _():
            ds_acc[...] = jnp.zeros_like(ds_acc)

        c, s = c_ref[...], s_ref[...]
        for h in range(RET_HEADS):
            hs = slice(h * RET_DIM, (h + 1) * RET_DIM)
            q = _rot(q_ref[:, hs], c, s)
            k = _rot(k_ref[:, hs], c, s) * scale
            qb, kb, vb = q.astype(BF16), k.astype(BF16), v_ref[:, hs].astype(BF16)
            g = g_ref[:, hs]
            dm, qd, kd = dm_ref[h], qd_ref[h], kd_ref[h]
            y = raw_ref[:, hs]
            mu = jnp.mean(y, axis=-1, keepdims=True)
            yc = y - mu
            rs = lax.rsqrt(jnp.mean(yc * yc, axis=-1, keepdims=True) + EPS)
            yn = yc * rs
            sg = _sigmoid(g)
            dyo = dy_ref[:, hs]
            dg_ref[:, hs] = (dyo * yn * (sg * (1.0 + g * (1.0 - sg)))).astype(BF16)
            dyn = dyo * (g * sg)
            dyr = rs * (dyn - jnp.mean(dyn, axis=-1, keepdims=True) - yn * jnp.mean(dyn * yn, axis=-1, keepdims=True))
            dyb = dyr.astype(BF16)
            s_in = st_ref[h].astype(BF16)
            ds_out = ds_acc[h]
            dsb = ds_out.astype(BF16)
            a = _dot(qb, kb, 1, 1) * dm
            da = (_dot(dyb, vb, 1, 1) * dm).astype(BF16)
            kdb = (k * kd).astype(BF16)
            qdb = (q * qd).astype(BF16)
            dv_ref[:, hs] = (_dot(a.astype(BF16), dyb, 0, 0) + _dot(kdb, dsb, 1, 0)).astype(BF16)
            dqh = _dot(da, kb, 1, 0) + _dot(dyb, s_in, 1, 1) * qd
            dkh = _dot(da, qb, 0, 0) + _dot(vb, dsb, 1, 1) * kd
            ds_acc[h] = cd_ref[h, 0:1, :] * ds_out + _dot(qdb, dyb, 0, 0)
            dq_ref[:, hs] = _rot_t(dqh, c, s).astype(BF16)
            dk_ref[:, hs] = (_rot_t(dkh, c, s) * scale).astype(BF16)

    return _pcall(
        body, name="retention_bwd",
        out_shape=(jax.ShapeDtypeStruct((t, RET_WIDTH), BF16),) * 4,
        grid=(n_chunks,),
        in_specs=[blk, wide(TILE_RQ), wide(TILE_RK), wide(TILE_RV), wide(TILE_RG), tab, tab, blk,
                  pl.BlockSpec((RET_HEADS, None, RET_DIM, RET_DIM), lambda n: (0, rev(n), 0, 0)),
                  const(dmask), const(qdec), const(kdec), const(cd)],
        out_specs=(blk,) * 4,
        scratch_shapes=[pltpu.VMEM((RET_HEADS, RET_DIM, RET_DIM), F32)],
        dims=("arbitrary",),
    )(dy, pm, pm, pm, pm, cos2, sin2, raw, states, dmask, qdec, kdec, cd)


FOX_C_LANE = FOX_DIM
FOX_NEGC_LANE = FOX_DIM + 3
FOX_LSE_LANE = FOX_DIM + 6
FOX_L_LANE = FOX_C_LANE
FOX_ROWSUM_LANE = FOX_C_LANE
FOX_COLSUM_LANE = FOX_NEGC_LANE


def _split3(x):
    hi = x.astype(BF16)
    r1 = x - hi.astype(F32)
    mid = r1.astype(BF16)
    lo = (r1 - mid.astype(F32)).astype(BF16)
    return hi, mid, lo


def _tri_dot(tri, x):
    hi, mid, lo = _split3(x)
    return _dot(tri, lo, 1, 0) + _dot(tri, mid, 1, 0) + _dot(tri, hi, 1, 0)


def _log_sigmoid(z):
    return jnp.minimum(z, 0.0) - jnp.log1p(jnp.exp(-jnp.abs(z)))


def _fox_consts():
    place = np.zeros((2, 3, LANES, FOX_TILES), np.float32)
    ones = np.zeros((3, 1, FOX_TILES), np.float32)
    for h in range(FOX_HEADS):
        for part in range(3):
            place[0, part, h, LANES * h + FOX_C_LANE + part] = 1.0
            place[1, part, h, LANES * h + FOX_NEGC_LANE + part] = -1.0
            ones[0, 0, LANES * h + FOX_NEGC_LANE + part] = 1.0
            ones[1, 0, LANES * h + FOX_C_LANE + part] = 1.0
            ones[1, 0, LANES * h + FOX_LSE_LANE + part] = 1.0
            ones[2, 0, LANES * h + FOX_C_LANE + part] = 1.0
    return jnp.asarray(place, BF16), jnp.asarray(ones, F32)


def _fox_prep(pm, bpad):
    t = pm.shape[0]
    tt = _tile(t, 512, LANES)
    place, ones = _fox_consts()
    wide = lambda c0: pl.BlockSpec((tt, FOX_TILES), lambda i: (i, c0 // FOX_HEADS))
    const = lambda a: pl.BlockSpec(a.shape, lambda i: (0,) * a.ndim)

    def body(q_ref, k_ref, v_ref, ff_ref, b_ref, pl_ref, on_ref, qa_ref, ka_ref, va_ref, carry_s):
        @pl.when(pl.program_id(0) == 0)
        def _():
            carry_s[...] = jnp.zeros_like(carry_s)

        r = lax.broadcasted_iota(jnp.int32, (LANES, LANES), 0)
        cc = lax.broadcasted_iota(jnp.int32, (LANES, LANES), 1)
        tri = jnp.where(cc <= r, 1.0, 0.0).astype(BF16)
        bias = b_ref[...]
        for sub in range(tt // LANES):
            rows = pl.ds(sub * LANES, LANES)
            cs = _tri_dot(tri, _log_sigmoid(ff_ref[rows, :] + bias)) + carry_s[...]
            carry_s[...] = cs[LANES - 1:LANES, :]
            parts = _split3(cs)
            eq = sum(_dot(part, pl_ref[0, i], 1, 0) for i, part in enumerate(parts))
            ek = sum(_dot(part, pl_ref[1, i], 1, 0) for i, part in enumerate(parts))
            qa_ref[rows, :] = (q_ref[rows, :] * FOX_DIM ** -0.5 + eq + on_ref[0]).astype(BF16)
            ka_ref[rows, :] = (k_ref[rows, :] + ek + on_ref[1]).astype(BF16)
            va_ref[rows, :] = (v_ref[rows, :] + on_ref[2]).astype(BF16)

    out = pl.BlockSpec((tt, FOX_TILES), lambda i: (i, 0))
    return _pcall(body, name="fox_prep", out_shape=(jax.ShapeDtypeStruct((t, FOX_TILES), BF16),) * 3, grid=(t // tt,),
                  in_specs=[wide(TILE_FQ), wide(TILE_FK), wide(TILE_FV), pl.BlockSpec((tt, LANES), lambda i: (i, TILE_FF)),
                            pl.BlockSpec((1, LANES), lambda i: (0, 0)), const(place), const(ones)],
                  out_specs=(out,) * 3, scratch_shapes=[pltpu.VMEM((1, LANES), F32)],
                  dims=("arbitrary",))(pm, pm, pm, pm, bpad, place, ones)


def _fox_post(dc, pm, bpad):
    t = pm.shape[0]
    nb = t // LANES

    def body(dc_ref, ff_ref, b_ref, d_ref, db_ref):
        r = lax.broadcasted_iota(jnp.int32, (LANES, LANES), 0)
        cc = lax.broadcasted_iota(jnp.int32, (LANES, LANES), 1)
        tri = jnp.where(cc >= r, 1.0, 0.0).astype(BF16)
        bias = b_ref[...]

        def step(i, carry):
            tail, acc = carry
            rows = pl.ds(pl.multiple_of((nb - 1 - i) * LANES, LANES), LANES)
            cs = _tri_dot(tri, dc_ref[rows, :]) + tail
            dff = cs * _sigmoid(-(ff_ref[rows, :] + bias))
            d_ref[rows, :] = dff.astype(BF16)
            return cs[0:1, :], acc + jnp.sum(dff, axis=0, keepdims=True)

        zero = jnp.zeros((1, LANES), F32)
        _, acc = lax.fori_loop(0, nb, step, (zero, zero))
        db_ref[...] = acc

    return _pcall(body, name="fox_forget_bwd",
                  out_shape=(jax.ShapeDtypeStruct((t, LANES), BF16), jax.ShapeDtypeStruct((1, LANES), F32)), grid=(1,),
                  in_specs=[pl.BlockSpec((t, LANES), lambda i: (0, 0)), pl.BlockSpec((t, LANES), lambda i: (0, TILE_FF)),
                            pl.BlockSpec((1, LANES), lambda i: (0, 0))],
                  out_specs=(pl.BlockSpec((t, LANES), lambda i: (0, 0)), pl.BlockSpec((1, LANES), lambda i: (0, 0))),
                  dims=("arbitrary",))(dc, pm, bpad)


def _tri_tables(nb, q_major):
    pairs = [(i, j) for i in range(nb) for j in range(i + 1)] if q_major else \
            [(i, j) for j in range(nb) for i in range(j, nb)]
    return jnp.asarray([a for a, _ in pairs], jnp.int32), jnp.asarray([b for _, b in pairs], jnp.int32)


def _causal(s):
    n = s.shape[0]
    row = lax.broadcasted_iota(jnp.int32, (n, n), 0)
    col = lax.broadcasted_iota(jnp.int32, (n, n), 1)
    return jnp.where(col <= row, s, NEG)


def _lane_col(x, lane):
    sel = lax.broadcasted_iota(jnp.int32, x.shape, 1) == lane
    return jnp.sum(jnp.where(sel, x, 0.0), axis=1, keepdims=True)


def _fox_fwd(qa, ka, va, blk):
    t = qa.shape[0]
    nb = t // blk
    qi, kj = _tri_tables(nb, True)
    q_spec = pl.BlockSpec((blk, LANES), lambda h, s, qi_r, kj_r: (qi_r[s], h))
    k_spec = pl.BlockSpec((blk, LANES), lambda h, s, qi_r, kj_r: (kj_r[s], h))

    def body(qi_r, kj_r, q_ref, k_ref, v_ref, o_ref, ob_ref, qb_ref, m_s, acc_s):
        s_id = pl.program_id(1)
        i, j = qi_r[s_id], kj_r[s_id]

        @pl.when(j == 0)
        def _():
            m_s[...] = jnp.full_like(m_s, NEG)
            acc_s[...] = jnp.zeros_like(acc_s)

        def tile(diagonal):
            s = _dot(q_ref[...], k_ref[...], 1, 1)
            if diagonal:
                s = _causal(s)
            m_old = m_s[...]
            m_new = jnp.maximum(m_old, jnp.max(s, axis=1, keepdims=True))
            p = jnp.exp(s - jnp.tile(m_new, (1, blk // LANES)))
            acc_s[...] = jnp.exp(m_old - m_new) * acc_s[...] + _dot(p.astype(BF16), v_ref[...], 1, 0)
            m_s[...] = m_new

        @pl.when(j < i)
        def _():
            tile(False)

        @pl.when(j == i)
        def _():
            tile(True)
            acc = acc_s[...]
            l = _lane_col(acc, FOX_L_LANE)
            o = acc / l
            o_ref[...] = o
            ob_ref[...] = o.astype(BF16)
            hi, mid, lo = _split3(-(m_s[:, 0:1] + jnp.log(l)))
            lane = lax.broadcasted_iota(jnp.int32, acc.shape, 1)
            qb_ref[...] = jnp.where(lane == FOX_LSE_LANE, hi,
                                    jnp.where(lane == FOX_LSE_LANE + 1, mid,
                                              jnp.where(lane == FOX_LSE_LANE + 2, lo, q_ref[...])))

    wide = (t, FOX_TILES)
    return pl.pallas_call(
        body, name="fox_fwd",
        out_shape=(jax.ShapeDtypeStruct(wide, F32), jax.ShapeDtypeStruct(wide, BF16), jax.ShapeDtypeStruct(wide, BF16)),
        grid_spec=pltpu.PrefetchScalarGridSpec(
            num_scalar_prefetch=2, grid=(FOX_HEADS, qi.shape[0]), in_specs=[q_spec, k_spec, k_spec],
            out_specs=(q_spec,) * 3,
            scratch_shapes=[pltpu.VMEM((blk, LANES), F32), pltpu.VMEM((blk, LANES), F32)]),
        compiler_params=_params(("parallel", "arbitrary")),
    )(qi, kj, qa, ka, va)


def _fox_bwd(qa, ka, va, do, o, blk):
    t = qa.shape[0]
    nb = t // blk
    qi, kj = _tri_tables(nb, False)
    q_spec = pl.BlockSpec((blk, LANES), lambda h, s, qi_r, kj_r: (qi_r[s], h))
    k_spec = pl.BlockSpec((blk, LANES), lambda h, s, qi_r, kj_r: (kj_r[s], h))
    head_spec = pl.BlockSpec((t, LANES), lambda h, s, qi_r, kj_r: (0, h))
    head_col = pl.BlockSpec((None, t, 1), lambda h, s, qi_r, kj_r: (h, 0, 0))
    k_col = pl.BlockSpec((None, blk, 1), lambda h, s, qi_r, kj_r: (h, kj_r[s], 0))
    first_spec = pl.BlockSpec((blk, LANES), lambda h, s, qi_r, kj_r: (jnp.where(kj_r[s] == 0, qi_r[s], nb - 1), h))
    n_steps = int(qi.shape[0])

    def body(qi_r, kj_r, q_ref, k_ref, v_ref, do_ref, o_ref, dq_ref, dk_ref, dv_ref, rs_ref, cs_ref,
             doa_s, dq_s, dk_s, dv_s):
        s_id = pl.program_id(1)
        i, j = qi_r[s_id], kj_r[s_id]
        rows = pl.ds(pl.multiple_of(i * blk, blk), blk)

        @pl.when(j == 0)
        def _():
            dof = do_ref[...]
            hi, mid, lo = _split3(-jnp.sum(dof * o_ref[...], axis=1, keepdims=True))
            lane = lax.broadcasted_iota(jnp.int32, dof.shape, 1)
            doa = jnp.where(lane == FOX_C_LANE, hi.astype(F32),
                            jnp.where(lane == FOX_C_LANE + 1, mid.astype(F32),
                                      jnp.where(lane == FOX_C_LANE + 2, lo.astype(F32), dof)))
            doa_s[rows, :] = doa.astype(BF16)
            dq_s[rows, :] = jnp.zeros((blk, LANES), F32)

        @pl.when(i == j)
        def _():
            dk_s[...] = jnp.zeros_like(dk_s)
            dv_s[...] = jnp.zeros_like(dv_s)

        def tile(diagonal):
            q, k = q_ref[...], k_ref[...]
            s = _dot(q, k, 1, 1)
            if diagonal:
                s = _causal(s)
            p = jnp.exp(s)
            doa = doa_s[rows, :]
            ds = (p * _dot(doa, v_ref[...], 1, 1)).astype(BF16)
            dv_s[...] += _dot(p.astype(BF16), doa, 0, 0)
            dk_s[...] += _dot(ds, q, 0, 0)
            dq_s[rows, :] += _dot(ds, k, 1, 0)

        @pl.when(i > j)
        def _():
            tile(False)

        @pl.when(i == j)
        def _():
            tile(True)

        @pl.when(i == nb - 1)
        def _():
            dk = dk_s[...]
            dk_ref[...] = dk.astype(BF16)
            dv_ref[...] = dv_s[...].astype(BF16)
            cs_ref[...] = -_lane_col(dk, FOX_COLSUM_LANE)

        @pl.when(s_id == n_steps - 1)
        def _():
            dq = dq_s[...]
            dq_ref[...] = (dq * FOX_DIM ** -0.5).astype(BF16)
            rs_ref[...] = _lane_col(dq, FOX_ROWSUM_LANE)

    wide = jax.ShapeDtypeStruct((t, FOX_TILES), BF16)
    cols = jax.ShapeDtypeStruct((FOX_HEADS, t, 1), F32)
    return pl.pallas_call(
        body, name="fox_bwd", out_shape=(wide, wide, wide, cols, cols),
        grid_spec=pltpu.PrefetchScalarGridSpec(
            num_scalar_prefetch=2, grid=(FOX_HEADS, n_steps),
            in_specs=[q_spec, k_spec, k_spec, first_spec, first_spec],
            out_specs=(head_spec, k_spec, k_spec, head_col, k_col),
            scratch_shapes=[pltpu.VMEM((t, LANES), BF16), pltpu.VMEM((t, LANES), F32), pltpu.VMEM((blk, LANES), F32),
                            pltpu.VMEM((blk, LANES), F32)]),
        compiler_params=_params(("parallel", "arbitrary")),
    )(qi, kj, qa, ka, va, do, o)


def _merge_fwd(gm, bm, za, zb):
    t, d = za.shape
    tt = _tile(t, 256, 16)
    row = pl.BlockSpec((tt, d), lambda i: (i, 0))

    def body(gm_ref, b_ref, za_ref, zb_ref, o_ref):
        ga = _sigmoid(gm_ref[:, :d] + b_ref[:, :d])
        gb = _sigmoid(gm_ref[:, d:] + b_ref[:, d:])
        o_ref[...] = (ga * za_ref[...] + gb * zb_ref[...]).astype(BF16)

    return _pcall(body, name="merge_fwd", out_shape=jax.ShapeDtypeStruct((t, d), BF16), grid=(t // tt,),
                  in_specs=[pl.BlockSpec((tt, 2 * d), lambda i: (i, 0)), pl.BlockSpec((1, 2 * d), lambda i: (0, 0)), row, row],
                  out_specs=row, dims=("parallel",))(gm, bm, za, zb)


def _merge_bwd(dmix, gm, bm, za, zb):
    t, d = za.shape
    tt = _tile(t, 256, 16)
    row = pl.BlockSpec((tt, d), lambda i: (i, 0))
    wide = pl.BlockSpec((tt, 2 * d), lambda i: (i, 0))
    vec = pl.BlockSpec((1, 2 * d), lambda i: (0, 0))

    def body(dm_ref, gm_ref, b_ref, za_ref, zb_ref, dza_ref, dzb_ref, dgm_ref, db_ref):
        dm = dm_ref[...]
        ga = _sigmoid(gm_ref[:, :d] + b_ref[:, :d])
        gb = _sigmoid(gm_ref[:, d:] + b_ref[:, d:])
        dza_ref[...] = (dm * ga).astype(BF16)
        dzb_ref[...] = (dm * gb).astype(BF16)
        dla = dm * za_ref[...] * ga * (1.0 - ga)
        dlb = dm * zb_ref[...] * gb * (1.0 - gb)
        dgm_ref[:, :d] = dla.astype(BF16)
        dgm_ref[:, d:] = dlb.astype(BF16)
        pa = jnp.sum(dla, axis=0, keepdims=True)
        pb = jnp.sum(dlb, axis=0, keepdims=True)

        @pl.when(pl.program_id(0) == 0)
        def _():
            db_ref[:, :d] = pa
            db_ref[:, d:] = pb

        @pl.when(pl.program_id(0) > 0)
        def _():
            db_ref[:, :d] += pa
            db_ref[:, d:] += pb

    return _pcall(body, name="merge_bwd",
                  out_shape=(jax.ShapeDtypeStruct((t, d), BF16), jax.ShapeDtypeStruct((t, d), BF16),
                             jax.ShapeDtypeStruct((t, 2 * d), BF16), jax.ShapeDtypeStruct((1, 2 * d), F32)),
                  grid=(t // tt,), in_specs=[row, wide, vec, row, row], out_specs=(row, row, wide, vec),
                  dims=("arbitrary",))(dmix, gm, bm, za, zb)


def _ple_final(h3, pgl, pe, gain, target):
    t, d = h3.shape
    tt = _tile(t, 256, 16)
    row = pl.BlockSpec((tt, d), lambda i: (i, 0))
    vec = pl.BlockSpec((1, d), lambda i: (0, 0))
    lvec = pl.BlockSpec((1, LANES), lambda i: (0, 0))

    def body(h_ref, pgl_ref, pe_ref, g_ref, t_ref, dh_ref, dsg_ref, dpe_ref, loss_ref, dg_ref):
        pg = _sigmoid(pgl_ref[...])
        pe_v = pe_ref[...]
        h4 = h_ref[...] + pg * pe_v
        r = lax.rsqrt(jnp.mean(h4 * h4, axis=-1, keepdims=True) + EPS)
        gv = g_ref[...]
        err = h4 * r * gv - t_ref[...]
        part_loss = 0.5 * jnp.sum(jnp.mean(err * err, axis=-1, keepdims=True), axis=0, keepdims=True)
        dy = err * (1.0 / d)
        part_g = jnp.sum(dy * h4 * r, axis=0, keepdims=True)
        dyg = dy * gv
        dh = r * dyg - h4 * (r * r * r) * jnp.mean(dyg * h4, axis=-1, keepdims=True)
        dh_ref[...] = dh
        dsg_ref[...] = (dh * pe_v * pg * (1.0 - pg)).astype(BF16)
        dpe_ref[...] = (dh * pg).astype(BF16)

        @pl.when(pl.program_id(0) == 0)
        def _():
            loss_ref[...] = jnp.broadcast_to(part_loss, (1, LANES))
            dg_ref[...] = part_g

        @pl.when(pl.program_id(0) > 0)
        def _():
            loss_ref[...] += jnp.broadcast_to(part_loss, (1, LANES))
            dg_ref[...] += part_g

    return _pcall(body, name="ple_final",
                  out_shape=(jax.ShapeDtypeStruct((t, d), F32), jax.ShapeDtypeStruct((t, d), BF16),
                             jax.ShapeDtypeStruct((t, d), BF16), jax.ShapeDtypeStruct((1, LANES), F32),
                             jax.ShapeDtypeStruct((1, d), F32)),
                  grid=(t // tt,), in_specs=[row, row, row, vec, row], out_specs=(row, row, row, lvec, vec),
                  dims=("arbitrary",))(h3, pgl, pe, gain, target)


def _adamw_math(w, g, m, v):
    m = ADAM_B1 * m + (1.0 - ADAM_B1) * g
    v = ADAM_B2 * v + (1.0 - ADAM_B2) * (g * g)
    m_hat = m / (1.0 - ADAM_B1 ** ADAM_STEP)
    v_hat = v / (1.0 - ADAM_B2 ** ADAM_STEP)
    delta = -ADAM_LR * (m_hat / (jnp.sqrt(v_hat) + ADAM_EPS) + ADAM_WD * w)
    return delta, m, v


def _adamw(parts, w, m, v, name):
    n, r, c = parts.shape
    tr = _tile(r, 256, 16)
    row = pl.BlockSpec((tr, c), lambda i: (i, 0))

    def body(p_ref, w_ref, m_ref, v_ref, g_ref, d_ref, mo_ref, vo_ref):
        g = p_ref[0].astype(F32)
        for s in range(1, n):
            g = g + p_ref[s].astype(F32)
        g_ref[...] = g
        d_ref[...], mo_ref[...], vo_ref[...] = _adamw_math(w_ref[...], g, m_ref[...], v_ref[...])

    return _pcall(body, name=name, out_shape=(jax.ShapeDtypeStruct((r, c), F32),) * 4, grid=(r // tr,),
                  in_specs=[pl.BlockSpec((n, tr, c), lambda i: (0, i, 0)), row, row, row], out_specs=(row,) * 4,
                  dims=("parallel",))(parts, w, m, v)


ANY = pl.BlockSpec(memory_space=pl.ANY)


def _all_gather(shards):
    n = len(shards)

    def body(*refs):
        x_refs, out_refs = refs[:n], refs[n:2 * n]
        send_sems, recv_sems, local_sems = refs[2 * n:]
        x, y, cc = lax.axis_index("x"), lax.axis_index("y"), lax.axis_index("c")
        me, sibling = (x, y, cc), (x, y, 1 - cc)
        chips = [(1 - x, y), (x, 1 - y), (1 - x, 1 - y)]

        def slot(a, px, py, pc):
            return out_refs[a].at[4 * px + 2 * py + pc]

        def copy(a, k, block, to, src=None):
            return pltpu.make_async_remote_copy(
                src_ref=slot(a, *block) if src is None else src, dst_ref=slot(a, *block),
                send_sem=send_sems.at[7 * a + k], recv_sem=recv_sems.at[7 * a + k], device_id=to, device_id_type=MESH)

        local, sent = [], []
        for a in range(n):
            local.append(pltpu.make_async_copy(x_refs[a], slot(a, *me), local_sems.at[a]))
            sent.append(copy(a, 0, me, sibling, src=x_refs[a]))
            sent += [copy(a, 1 + j, me, (*chip, cc), src=x_refs[a]) for j, chip in enumerate(chips)]
        for cp in local + sent:
            cp.start()
        for j, chip in enumerate(chips):
            for a in range(n):
                copy(a, 1 + j, (*chip, cc), me).wait_recv()
                sent.append(copy(a, 4 + j, (*chip, cc), sibling))
                sent[-1].start()
        for a in range(n):
            copy(a, 0, sibling, me).wait_recv()
            for j, chip in enumerate(chips):
                copy(a, 4 + j, (*chip, 1 - cc), me).wait_recv()
        for cp in sent:
            cp.wait_send()
        for cp in local:
            cp.wait()

    return pl.pallas_call(
        body, name="weights_all_gather",
        out_shape=tuple(jax.ShapeDtypeStruct((N_DEV,) + s.shape, s.dtype) for s in shards),
        in_specs=[ANY] * n, out_specs=(ANY,) * n,
        scratch_shapes=[pltpu.SemaphoreType.DMA((7 * n,)), pltpu.SemaphoreType.DMA((7 * n,)),
                        pltpu.SemaphoreType.DMA((n,))],
    )(*shards)


def _reduce_scatter_exchange(blocks):
    n = len(blocks)

    def body(*refs):
        g_refs, recv_refs = refs[:n], refs[n:2 * n]
        send_sems, recv_sems, local_sems = refs[2 * n:]
        x, y, cc = lax.axis_index("x"), lax.axis_index("y"), lax.axis_index("c")
        me = 4 * x + 2 * y + cc
        local, sent, landing = [], [], []
        for a in range(n):
            local.append(pltpu.make_async_copy(g_refs[a].at[me], recv_refs[a].at[me], local_sems.at[a]))
        for k in range(1, N_DEV):
            px, py, pc = x ^ (k >> 2), y ^ ((k >> 1) & 1), cc ^ (k & 1)
            peer = 4 * px + 2 * py + pc
            for a in range(n):
                sems = dict(send_sem=send_sems.at[7 * a + k - 1], recv_sem=recv_sems.at[7 * a + k - 1],
                            device_id=(px, py, pc), device_id_type=MESH)
                sent.append(pltpu.make_async_remote_copy(src_ref=g_refs[a].at[peer], dst_ref=recv_refs[a].at[me], **sems))
                landing.append(pltpu.make_async_remote_copy(src_ref=g_refs[a].at[me], dst_ref=recv_refs[a].at[peer], **sems))
        for cp in local + sent:
            cp.start()
        for cp in landing:
            cp.wait_recv()
        for cp in sent:
            cp.wait_send()
        for cp in local:
            cp.wait()

    return pl.pallas_call(
        body, name="grads_reduce_scatter_exchange",
        out_shape=tuple(jax.ShapeDtypeStruct(b.shape, b.dtype) for b in blocks),
        in_specs=[ANY] * n, out_specs=(ANY,) * n,
        scratch_shapes=[pltpu.SemaphoreType.DMA((7 * n,)), pltpu.SemaphoreType.DMA((7 * n,)),
                        pltpu.SemaphoreType.DMA((n,))],
    )(*blocks)


HBM = pl.BlockSpec(memory_space=pltpu.HBM)
SEM = pl.BlockSpec(memory_space=pltpu.SEMAPHORE)
DATAFLOW = pltpu.SideEffectType.DATAFLOW_SIDE_EFFECTING


def _peers():
    x, y, cc = lax.axis_index("x"), lax.axis_index("y"), lax.axis_index("c")
    out = []
    for k in range(1, N_DEV):
        px, py, pc = x ^ (k >> 2), y ^ ((k >> 1) & 1), cc ^ (k & 1)
        out.append((k, (px, py, pc), 4 * px + 2 * py + pc))
    return 4 * x + 2 * y + cc, out


def _scatter_start(blocks, after, name, gather=False):
    n = len(blocks)
    lands = [lax.empty((N_DEV,) + b.shape if gather else b.shape, b.dtype) for b in blocks]

    def body(*refs):
        g_refs, land_refs = refs[:n], refs[n:2 * n]
        send_sems, recv_sems, token = refs[2 * n + 1], refs[2 * n + 2], refs[-1]
        me, peers = _peers()
        for k, peer, slot in peers:
            for a in range(n):
                pltpu.make_async_remote_copy(
                    src_ref=g_refs[a] if gather else g_refs[a].at[slot], dst_ref=land_refs[a].at[me],
                    send_sem=send_sems.at[7 * a + k - 1],
                    recv_sem=recv_sems.at[7 * a + k - 1], device_id=peer, device_id_type=MESH).start()
        token[...] = jnp.zeros_like(token)

    thru = [pltpu.HBM(b.shape, b.dtype) for b in blocks]
    thru_lands = [pltpu.HBM(b.shape, b.dtype) for b in lands]
    return pl.pallas_call(
        body, name=name,
        out_shape=(pltpu.SemaphoreType.DMA((7 * n,)), pltpu.SemaphoreType.DMA((7 * n,)), *thru, *thru_lands,
                   jax.ShapeDtypeStruct((8, LANES), F32)),
        in_specs=[HBM] * (2 * n) + [pl.BlockSpec(memory_space=pl.ANY)],
        out_specs=(SEM, SEM, *[HBM] * (2 * n), pl.BlockSpec(memory_space=pltpu.VMEM)),
        input_output_aliases={i: 2 + i for i in range(2 * n)},
        compiler_params=pltpu.CompilerParams(has_side_effects=DATAFLOW),
    )(*[pltpu.with_memory_space_constraint(a, pltpu.HBM) for a in list(blocks) + lands], after)


def _scatter_wait(started, after, name, gather=False):
    send_sems, recv_sems, *rest = started
    n = (len(rest) - 1) // 2
    thru = rest[:2 * n]

    def body(*refs):
        g_refs, land_refs = refs[:n], refs[n:2 * n]
        send_sems, recv_sems = refs[2 * n], refs[2 * n + 1]
        me, peers = _peers()
        for k, peer, slot in peers:
            for a in range(n):
                copy = pltpu.make_async_remote_copy(
                    src_ref=g_refs[a] if gather else g_refs[a].at[slot], dst_ref=land_refs[a].at[slot],
                    send_sem=send_sems.at[7 * a + k - 1],
                    recv_sem=recv_sems.at[7 * a + k - 1], device_id=peer, device_id_type=MESH)
                copy.wait_send()
                copy.wait_recv()

    out = pl.pallas_call(
        body, name=name, out_shape=tuple(pltpu.HBM(a.shape, a.dtype) for a in thru),
        in_specs=[HBM] * (2 * n) + [SEM, SEM, pl.BlockSpec(memory_space=pl.ANY)], out_specs=tuple([HBM] * (2 * n)),
        input_output_aliases={i: i for i in range(2 * n)},
        compiler_params=pltpu.CompilerParams(has_side_effects=DATAFLOW),
    )(*thru, send_sems, recv_sems, after)
    return out[:n], out[n:]


BIG = (("w_ffn1_gate", "colT"), ("w_ffn1_up", "colT"), ("w_ffn1_down", "row"), ("w_ffn2_gate", "colT"),
       ("w_ffn2_up", "colT"), ("w_ffn2_down", "row"), ("w_in", "colT"), ("w_merge", "col"), ("w_ret_out", "col"),
       ("w_fox_out", "col"), ("w_out", "row"), ("w_ple", "col"), ("w_ple_gate", "row"))


def _shard_view(a, kind):
    a = a.reshape(a.shape[-2:])
    return a.T if kind == "colT" else a


def _unview(a, kind, shape):
    return (a.T if kind == "colT" else a).reshape(shape)


def _full_from_slots(g, kind):
    n, r, c = g.shape
    return g.transpose(1, 0, 2).reshape(r, n * c) if kind == "col" else g.reshape(n * r, c)


def _slots_from_full(f, kind):
    r, c = f.shape
    return f.reshape(r, N_DEV, c // N_DEV).transpose(1, 0, 2) if kind == "col" else f.reshape(N_DEV, r // N_DEV, c)


EARLY_GROUPS = (("w_ple_gate", "w_ple", "w_ffn2_down", "w_ffn2_gate", "w_ffn2_up", "w_out", "w_ret_out", "w_fox_out"),
                ("w_in", "w_merge"))


def _scatter_group(gw, names, after, name):
    kind = dict(BIG)
    return names, _scatter_start([_slots_from_full(gw[n], kind[n]) for n in names], after, name)


def _pad_heads(w):
    d = w.shape[1]
    return jnp.pad(w.reshape(FOX_HEADS, FOX_DIM, d), ((0, 0), (0, LANES - FOX_DIM), (0, 0))).reshape(FOX_TILES, d)


def _unpad_heads(w):
    d = w.shape[1]
    return w.reshape(FOX_HEADS, LANES, d)[:, :FOX_DIM].reshape(FOX_WIDTH, d)


def _deinterleave_rows(w):
    d = w.shape[1]
    return w.reshape(RET_HEADS, RET_DIM // 2, 2, d).transpose(0, 2, 1, 3).reshape(RET_WIDTH, d)


def _interleave_rows(w):
    d = w.shape[1]
    return w.reshape(RET_HEADS, 2, RET_DIM // 2, d).transpose(0, 2, 1, 3).reshape(RET_WIDTH, d)


def _pad_w_in(wt):
    d = wt.shape[1]
    rw, fw = RET_WIDTH, FOX_WIDTH
    fo = 4 * rw
    return jnp.concatenate([
        _deinterleave_rows(wt[:rw]), _deinterleave_rows(wt[rw:2 * rw]), wt[2 * rw:4 * rw],
        _pad_heads(wt[fo:fo + fw]), _pad_heads(wt[fo + fw:fo + 2 * fw]), _pad_heads(wt[fo + 2 * fw:fo + 3 * fw]),
        wt[fo + 3 * fw:], jnp.zeros((2 * LANES - FOX_HEADS, d), wt.dtype)], axis=0)


def _unpad_w_in(g):
    rw = RET_WIDTH
    f0 = 4 * rw
    return jnp.concatenate([
        _interleave_rows(g[:rw]), _interleave_rows(g[rw:2 * rw]), g[2 * rw:4 * rw],
        _unpad_heads(g[f0:f0 + FOX_TILES]), _unpad_heads(g[f0 + FOX_TILES:f0 + 2 * FOX_TILES]),
        _unpad_heads(g[f0 + 2 * FOX_TILES:f0 + 3 * FOX_TILES]),
        g[f0 + 3 * FOX_TILES:f0 + 3 * FOX_TILES + FOX_HEADS]], axis=0)


SMALL = ("ln_ffn1", "ln_mix", "b_forget", "b_merge", "ln_ffn2", "ln_ple", "ln_final")


def _small_rows(n):
    rows = -(-n // LANES)
    return -(-rows // 8) * 8


def _pack_small(vals, with_loss=None):
    parts = []
    for name in SMALL:
        v = vals[name].reshape(-1).astype(F32)
        rows = _small_rows(v.shape[0])
        parts.append(jnp.pad(v, (0, rows * LANES - v.shape[0])).reshape(rows, LANES))
    if with_loss is not None:
        parts.append(jnp.pad(with_loss.reshape(1, LANES), ((0, 7), (0, 0))))
    else:
        parts.append(jnp.zeros((8, LANES), F32))
    return jnp.concatenate(parts, axis=0)


def _unpack_small(packed, shapes):
    out, at = {}, 0
    for name in SMALL:
        n = int(np.prod(shapes[name]))
        rows = _small_rows(n)
        out[name] = packed[at:at + rows].reshape(-1)[:n].reshape(shapes[name])
        at += rows
    return out, packed[at, 0]


def _gather_finish(group, after, me):
    names, started, wait_name = group
    kind = dict(BIG)
    sent, landed = _scatter_wait(started, after, wait_name, gather=True)
    return {n: _full_from_slots(lax.dynamic_update_slice_in_dim(land, shard[None], me, 0), kind[n])
            for n, shard, land in zip(names, sent, landed)}


def _local_step(x, p, positions, target, w, small, me, entry_token, gathers):
    t, d = x.shape
    gain = lambda n: small[n].reshape(1, d)
    w = dict(w)
    bpad = jnp.pad(small["b_forget"].reshape(1, FOX_HEADS), ((0, 0), (0, LANES - FOX_HEADS)))
    bm = small["b_merge"].reshape(1, 2 * d)
    fox_blk = _tile(t, 1024, 128)

    def ffn_fwd(n, tag, down_gather=None):
        g = _mm([(n, w[f"w_{tag}_gate"], "nt")], BF16, f"{tag}_gate", tn=1408)
        u = _mm([(n, w[f"w_{tag}_up"], "nt")], BF16, f"{tag}_up", tn=1408)
        a = _swiglu_fwd(g, u, f"{tag}_swiglu")
        if down_gather is not None:
            w.update(_gather_finish(down_gather, a, me))
        return g, u, a, _mm([(a, w[f"w_{tag}_down"], "nn")], F32, f"{tag}_down")

    n1 = _rms_fwd(x, gain("ln_ffn1") + entry_token, "rms_ffn1")
    g1, u1, a1, f1 = ffn_fwd(n1, "ffn1", down_gather=gathers[0])
    h1, u = _rms_fwd(x, gain("ln_mix"), "rms_mix", f=f1)
    w.update(_gather_finish(gathers[1], f1, me))
    w_in_t = _pad_w_in(w["w_in"])
    gm = _mm([(u, w["w_merge"], "nn")], F32, "mixer_gates")
    pm = _mm([(u, w_in_t, "nt")], F32, "mixer_in", tn=1792)

    half = jnp.arange(RET_DIM // 2, dtype=F32) / (RET_DIM // 2)
    inv = 1.0 / (ROPE_BASE ** half)
    inv2 = jnp.concatenate([inv, inv]).reshape(1, RET_DIM)
    sign2 = jnp.concatenate([-jnp.ones((RET_DIM // 2,), F32), jnp.ones((RET_DIM // 2,), F32)]).reshape(1, RET_DIM)
    cos2, sin2 = _rope_tables(positions.reshape(t, 1), inv2, sign2)
    consts = _ret_consts()
    y_ret, y_raw, states = _ret_fwd(pm, cos2, sin2, consts)
    w.update(_gather_finish(gathers[2], y_raw, me))
    w_fox_pad = _pad_heads(w["w_fox_out"])
    za = _mm([(y_ret, w["w_ret_out"], "nn")], F32, "ret_out")

    qa, ka, va = _fox_prep(pm, bpad)
    o_fox, y_fox, qa_b = _fox_fwd(qa, ka, va, fox_blk)
    zb = _mm([(y_fox, w_fox_pad, "nn")], F32, "fox_out")

    mix = _merge_fwd(gm, bm, za, zb)
    mo = _mm([(mix, w["w_out"], "nn")], F32, "mix_out")
    h2, n2 = _rms_fwd(h1, gain("ln_ffn2"), "rms_ffn2", f=mo, scale=1.0)
    g2, u2, a2, f2 = ffn_fwd(n2, "ffn2")
    h3, n3 = _rms_fwd(h2, gain("ln_ple"), "rms_ple", f=f2)
    pgl = _mm([(n3, w["w_ple_gate"], "nn")], F32, "ple_gate")
    pb = p.astype(BF16)
    pe = _mm([(pb, w["w_ple"], "nn")], F32, "ple_embed")

    gw, gs = {}, {}
    dh4, dsg, dpe, loss, gs["ln_final"] = _ple_final(h3, pgl, pe, gain("ln_final"), target)
    gw["w_ple_gate"] = _mm([(n3, dsg, "tn")], BF16, "d_w_ple_gate", tn=256)
    gw["w_ple"] = _mm([(pb, dpe, "tn")], BF16, "d_w_ple", tn=256)
    dn3 = _mm([(dsg, w["w_ple_gate"], "nt")], F32, "d_n3")
    dh3, dh3_half, gs["ln_ple"] = _rms_bwd(dn3, h3, gain("ln_ple"), dh4, "rms_ple_bwd", 0.5)

    def ffn_bwd(dh_half, g, u_, a, n, tag, scatter_now=None):
        gw[f"w_{tag}_down"] = _mm([(a, dh_half, "tn")], BF16, f"d_w_{tag}_down", tm=1408, tn=256)
        start = scatter_now if scatter_now is not None else (lambda *_: None)
        token = start((f"w_{tag}_down",), dh_half, "c")
        da = _mm([(dh_half, w[f"w_{tag}_down"], "nt")], BF16, f"d_a_{tag}", tn=1408, after=token)
        dg, du_ = _swiglu_bwd(da, g, u_, f"{tag}_swiglu_bwd")
        gw[f"w_{tag}_gate"] = _mm([(dg, n, "tn")], BF16, f"d_w_{tag}_gate", tm=1408, tn=256)
        token = start((f"w_{tag}_gate",), da, "d")
        gw[f"w_{tag}_up"] = _mm([(du_, n, "tn")], BF16, f"d_w_{tag}_up", tm=1408, tn=256, after=token)
        token = start((f"w_{tag}_up",), da, "e")
        return _mm([(dg, w[f"w_{tag}_gate"], "nn"), (du_, w[f"w_{tag}_up"], "nn")], F32, f"d_n_{tag}", tm=512,
                   after=token)

    dn2 = ffn_bwd(dh3_half, g2, u2, a2, n2, "ffn2")
    dh2, dh2_b, gs["ln_ffn2"] = _rms_bwd(dn2, h2, gain("ln_ffn2"), dh3, "rms_ffn2_bwd", 1.0)

    gw["w_out"] = _mm([(mix, dh2_b, "tn")], BF16, "d_w_out", tn=256)
    dmix = _mm([(dh2_b, w["w_out"], "nt")], F32, "d_mix")
    dza, dzb, dgm, gs["b_merge"] = _merge_bwd(dmix, gm, bm, za, zb)
    gw["w_ret_out"] = _mm([(y_ret, dza, "tn")], BF16, "d_w_ret_out", tn=256)
    gw["w_fox_out"] = _unpad_heads(_mm([(y_fox, dzb, "tn")], BF16, "d_w_fox_out", tn=256))
    dy_ret = _mm([(dza, w["w_ret_out"], "nt")], F32, "d_y_ret")
    do_fox = _mm([(dzb, w_fox_pad, "nt")], F32, "d_y_fox")

    pending = [_scatter_group(gw, EARLY_GROUPS[0], dy_ret, "grads_scatter_a_start")]
    token = pending[0][1][-1][0, 0]
    drq, drk, drv, drg = _ret_bwd(dy_ret, pm, cos2, sin2, y_raw, states, consts[:3] + (consts[3] + token,))

    dqa, dka, dva, ds_rows, ds_cols = _fox_bwd(qa_b, ka, va, do_fox, o_fox, fox_blk)
    dc = jnp.pad((ds_rows + ds_cols).reshape(FOX_HEADS, t).T, ((0, 0), (0, LANES - FOX_HEADS)))
    dff, db_forget = _fox_post(dc, pm, bpad)
    gs["b_forget"] = db_forget[:, :FOX_HEADS]

    dpm = jnp.concatenate([drq, drk, drv, drg, dqa, dka, dva, dff, jnp.zeros((t, LANES), BF16)], axis=1)
    gw["w_merge"] = _mm([(u, dgm, "tn")], BF16, "d_w_merge", tn=512)
    gw["w_in"] = _unpad_w_in(_mm([(dpm, u, "tn")], BF16, "d_w_in", tm=1792, tn=256))
    du = _mm([(dpm, w_in_t, "nn"), (dgm, w["w_merge"], "nt")], F32, "d_u", tm=1024, tn=512)
    pending.append(_scatter_group(gw, EARLY_GROUPS[1], du, "grads_scatter_b_start"))
    token = pending[1][1][-1][0:1, 0:1]
    dh1, dh1_half, gs["ln_mix"] = _rms_bwd(du, h1, gain("ln_mix") + token, dh2, "rms_mix_bwd", 0.5)

    def scatter_now(names, after, tag):
        pending.append(_scatter_group(gw, names, after, f"grads_scatter_{tag}_start"))
        return pending[-1][1][-1]

    dn1 = ffn_bwd(dh1_half, g1, u1, a1, n1, "ffn1", scatter_now=scatter_now)
    dx, _, gs["ln_ffn1"] = _rms_bwd(dn1, x, gain("ln_ffn1"), dh1, "rms_ffn1_bwd", 1.0)
    return loss, dx, gw, gs, pending


WEIGHTS = ("ln_ffn1", "w_ffn1_gate", "w_ffn1_up", "w_ffn1_down", "ln_mix", "w_in", "b_forget", "w_merge", "b_merge",
           "w_ret_out", "w_fox_out", "w_out", "ln_ffn2", "w_ffn2_gate", "w_ffn2_up", "w_ffn2_down", "ln_ple", "w_ple",
           "w_ple_gate", "ln_final")


def kernel(x, p, positions, ln_ffn1, w_ffn1_gate, w_ffn1_up, w_ffn1_down, ln_mix, w_in, b_forget, w_merge, b_merge, w_ret_out, w_fox_out, w_out, ln_ffn2, w_ffn2_gate, w_ffn2_up, w_ffn2_down, ln_ple, w_ple, w_ple_gate, ln_final, loss_target, m_ln_ffn1, m_w_ffn1_gate, m_w_ffn1_up, m_w_ffn1_down, m_ln_mix, m_w_in, m_b_forget, m_w_merge, m_b_merge, m_w_ret_out, m_w_fox_out, m_w_out, m_ln_ffn2, m_w_ffn2_gate, m_w_ffn2_up, m_w_ffn2_down, m_ln_ple, m_w_ple, m_w_ple_gate, m_ln_final, v_ln_ffn1, v_w_ffn1_gate, v_w_ffn1_up, v_w_ffn1_down, v_ln_mix, v_w_in, v_b_forget, v_w_merge, v_b_merge, v_w_ret_out, v_w_fox_out, v_w_out, v_ln_ffn2, v_w_ffn2_gate, v_w_ffn2_up, v_w_ffn2_down, v_ln_ple, v_w_ple, v_w_ple_gate, v_ln_final):
    args = dict(ln_ffn1=ln_ffn1, w_ffn1_gate=w_ffn1_gate, w_ffn1_up=w_ffn1_up, w_ffn1_down=w_ffn1_down, ln_mix=ln_mix, w_in=w_in, b_forget=b_forget, w_merge=w_merge, b_merge=b_merge, w_ret_out=w_ret_out, w_fox_out=w_fox_out, w_out=w_out, ln_ffn2=ln_ffn2, w_ffn2_gate=w_ffn2_gate, w_ffn2_up=w_ffn2_up, w_ffn2_down=w_ffn2_down, ln_ple=ln_ple, w_ple=w_ple, w_ple_gate=w_ple_gate, ln_final=ln_final)
    moms = dict(ln_ffn1=m_ln_ffn1, w_ffn1_gate=m_w_ffn1_gate, w_ffn1_up=m_w_ffn1_up, w_ffn1_down=m_w_ffn1_down, ln_mix=m_ln_mix, w_in=m_w_in, b_forget=m_b_forget, w_merge=m_w_merge, b_merge=m_b_merge, w_ret_out=m_w_ret_out, w_fox_out=m_w_fox_out, w_out=m_w_out, ln_ffn2=m_ln_ffn2, w_ffn2_gate=m_w_ffn2_gate, w_ffn2_up=m_w_ffn2_up, w_ffn2_down=m_w_ffn2_down, ln_ple=m_ln_ple, w_ple=m_w_ple, w_ple_gate=m_w_ple_gate, ln_final=m_ln_final)
    vars_ = dict(ln_ffn1=v_ln_ffn1, w_ffn1_gate=v_w_ffn1_gate, w_ffn1_up=v_w_ffn1_up, w_ffn1_down=v_w_ffn1_down, ln_mix=v_ln_mix, w_in=v_w_in, b_forget=v_b_forget, w_merge=v_w_merge, b_merge=v_b_merge, w_ret_out=v_w_ret_out, w_fox_out=v_w_fox_out, w_out=v_w_out, ln_ffn2=v_ln_ffn2, w_ffn2_gate=v_w_ffn2_gate, w_ffn2_up=v_w_ffn2_up, w_ffn2_down=v_w_ffn2_down, ln_ple=v_ln_ple, w_ple=v_w_ple, w_ple_gate=v_w_ple_gate, ln_final=v_ln_final)
    kinds = ("grad", "delta", "new_m", "new_v")

    me = 4 * lax.axis_index("x") + 2 * lax.axis_index("y") + lax.axis_index("c")
    kind_of = dict(BIG)
    shard = {n: _shard_view(args[n], kind).astype(BF16) for n, kind in BIG}
    first = ("w_ffn1_gate", "w_ffn1_up")
    groups = (("w_ffn1_down",), ("w_in", "w_merge"))
    groups += (tuple(n for n, _ in BIG if n not in first + groups[0] + groups[1]),)
    gathered = _all_gather([shard[n] for n in first])
    w_full = {n: _full_from_slots(g, kind_of[n]) for n, g in zip(first, gathered)}
    gathers, after = [], gathered[0]
    for tag, names in zip("zab", groups):
        started = _scatter_start([shard[n] for n in names], after, f"weights_gather_{tag}_start", gather=True)
        gathers.append((names, started, f"weights_gather_{tag}_wait"))
        after = started[-1]

    small = {n: args[n] for n in SMALL}
    loss_part, dx, gw, gs, pending = _local_step(x[0], p[0, 0], positions[0], loss_target[0], w_full, small, me,
                                                 after[0:1, 0:1], gathers)

    parts_of = {}
    for tag, (names, started) in zip("abcde", pending):
        sent, landed = _scatter_wait(started, dx, f"grads_scatter_{tag}_wait")
        for n, blk, land in zip(names, sent, landed):
            own = lax.dynamic_index_in_dim(blk, me, 0, keepdims=True)
            parts_of[n] = lax.dynamic_update_slice_in_dim(land, own, me, 0)
    late = [(n, kind) for n, kind in BIG if n not in parts_of]
    small_part = _pack_small(gs, with_loss=loss_part)
    blocks = [_slots_from_full(gw[n], kind) for n, kind in late]
    blocks.append(jnp.broadcast_to(small_part, (N_DEV,) + small_part.shape))
    recv = _reduce_scatter_exchange(blocks)
    parts_of.update({n: r for (n, _), r in zip(late, recv)})

    res = {}
    for n, kind in BIG:
        parts = parts_of[n]
        outs = _adamw(parts, _shard_view(args[n], kind), _shard_view(moms[n], kind), _shard_view(vars_[n], kind),
                      f"adamw_{n}")
        for what, o in zip(kinds, outs):
            res[(what, n)] = _unview(o, kind, args[n].shape)
    s_outs = _adamw(recv[-1], _pack_small(small), _pack_small({n: moms[n] for n in SMALL}),
                    _pack_small({n: vars_[n] for n in SMALL}), "adamw_small")
    for what, sm in zip(kinds, s_outs):
        svals, extra = _unpack_small(sm, {n: args[n].shape for n in SMALL})
        if what == "grad":
            loss = extra
        for n in SMALL:
            res[(what, n)] = svals[n]
    return (loss, dx[None], *[res[(what, n)] for what in kinds for n in WEIGHTS])
```

```python
import numpy as np
import jax
import jax.numpy as jnp
from jax import lax
from jax.experimental import pallas as pl
from jax.experimental.pallas import tpu as pltpu

F32 = jnp.float32
BF16 = jnp.bfloat16

N_DEV = 8
EPS = 1e-6
RET_HEADS = 4
RET_DIM = 128
RET_WIDTH = RET_HEADS * RET_DIM
FOX_HEADS = 8
FOX_DIM = 64
FOX_WIDTH = FOX_HEADS * FOX_DIM
CHUNK = 128
ROPE_BASE = 10000.0
LANES = 128
FOX_TILES = FOX_HEADS * LANES
IN_COLS = 4 * RET_WIDTH + 3 * FOX_WIDTH + FOX_HEADS
IN_PAD = 4 * RET_WIDTH + 3 * FOX_TILES + 2 * LANES
TILE_RQ, TILE_RK, TILE_RV, TILE_RG = 0, 4, 8, 12
TILE_FQ, TILE_FK, TILE_FV, TILE_FF = 16, 24, 32, 40
NEG = -1e30

ADAM_LR = 0.001
ADAM_B1 = 0.9
ADAM_B2 = 0.999
ADAM_EPS = 1e-08
ADAM_WD = 0.01
ADAM_STEP = 10

VMEM_LIMIT_BYTES = 56 * 1024 * 1024

MESH = pl.DeviceIdType.MESH


def _tile(dim, pref, mult):
    if dim <= pref:
        return dim
    t = (pref // mult) * mult
    while t >= mult:
        if dim % t == 0:
            return t
        t -= mult
    return dim


def _params(dims):
    return pltpu.CompilerParams(dimension_semantics=dims, vmem_limit_bytes=VMEM_LIMIT_BYTES)


def _pcall(body, *, name, out_shape, grid, in_specs, out_specs, scratch_shapes=(), dims=None):
    return pl.pallas_call(body, name=name, out_shape=out_shape, grid=grid, in_specs=in_specs, out_specs=out_specs,
                          scratch_shapes=list(scratch_shapes), compiler_params=_params(dims))


def _dot(a, b, ca, cb):
    return lax.dot_general(a, b, (((ca,), (cb,)), ((), ())), preferred_element_type=F32)


def _sigmoid(x):
    return 1.0 / (1.0 + jnp.exp(-x))


def _mm(pairs, out_dtype, name, tm=1024, tn=1024, after=None):
    dims = []
    for a, b, mode in pairs:
        m, k = (a.shape[1], a.shape[0]) if mode == "tn" else a.shape
        n, k2 = b.shape if mode == "nt" else (b.shape[1], b.shape[0])
        assert k == k2, (name, a.shape, b.shape, mode)
        dims.append((m, n))
    assert all(d == dims[0] for d in dims), (name, dims)
    m, n = dims[0]
    tm = _tile(m, tm, 128 if any(mode == "tn" for _, _, mode in pairs) else 16)
    tn = _tile(n, tn, 128)
    in_specs, contract, operands = [], [], []
    for a, b, mode in pairs:
        k = a.shape[0] if mode == "tn" else a.shape[1]
        in_specs.append(pl.BlockSpec((k, tm), lambda i, j: (0, i)) if mode == "tn" else
                        pl.BlockSpec((tm, k), lambda i, j: (i, 0)))
        in_specs.append(pl.BlockSpec((tn, k), lambda i, j: (j, 0)) if mode == "nt" else
                        pl.BlockSpec((k, tn), lambda i, j: (0, j)))
        contract.append((0 if mode == "tn" else 1, 1 if mode == "nt" else 0))
        operands += [a, b]
    if after is not None:
        in_specs.append(pl.BlockSpec(memory_space=pl.ANY))
        operands.append(after)

    def body(*refs):
        o_ref = refs[-1]
        acc = None
        for p, (ca, cb) in enumerate(contract):
            part = _dot(refs[2 * p][...], refs[2 * p + 1][...], ca, cb)
            acc = part if acc is None else acc + part
        o_ref[...] = acc.astype(out_dtype)

    return _pcall(body, name=name, out_shape=jax.ShapeDtypeStruct((m, n), out_dtype), grid=(m // tm, n // tn),
                  in_specs=in_specs, out_specs=pl.BlockSpec((tm, tn), lambda i, j: (i, j)),
                  dims=("parallel", "parallel"))(*operands)


def _rms_fwd(h, gain, name, f=None, scale=0.5):
    t, d = h.shape
    tt = _tile(t, 512, 16)
    row = pl.BlockSpec((tt, d), lambda i: (i, 0))
    vec = pl.BlockSpec((1, d), lambda i: (0, 0))

    def norm(hv, g_ref, n_ref):
        r = lax.rsqrt(jnp.mean(hv * hv, axis=-1, keepdims=True) + EPS)
        n_ref[...] = (hv * r * g_ref[...]).astype(BF16)

    if f is None:

        def body(h_ref, g_ref, n_ref):
            norm(h_ref[...], g_ref, n_ref)

        return _pcall(body, name=name, out_shape=jax.ShapeDtypeStruct((t, d), BF16), grid=(t // tt,),
                      in_specs=[row, vec], out_specs=row, dims=("parallel",))(h, gain)

    def body(h_ref, f_ref, g_ref, hn_ref, n_ref):
        hv = h_ref[...] + scale * f_ref[...]
        hn_ref[...] = hv
        norm(hv, g_ref, n_ref)

    return _pcall(body, name=name,
                  out_shape=(jax.ShapeDtypeStruct((t, d), F32), jax.ShapeDtypeStruct((t, d), BF16)),
                  grid=(t // tt,), in_specs=[row, row, vec], out_specs=(row, row), dims=("parallel",))(h, f, gain)


def _rms_bwd(dn, h, gain, dh_in, name, out_scale):
    t, d = h.shape
    tt = _tile(t, 512, 16)
    row = pl.BlockSpec((tt, d), lambda i: (i, 0))
    vec = pl.BlockSpec((1, d), lambda i: (0, 0))

    def body(dn_ref, h_ref, g_ref, dhin_ref, dh_ref, dhb_ref, dg_ref):
        hv = h_ref[...]
        dnv = dn_ref[...].astype(F32)
        r = lax.rsqrt(jnp.mean(hv * hv, axis=-1, keepdims=True) + EPS)
        dng = dnv * g_ref[...]
        dh = dhin_ref[...] + r * dng - hv * (r * r * r) * jnp.mean(dng * hv, axis=-1, keepdims=True)
        dh_ref[...] = dh
        dhb_ref[...] = (out_scale * dh).astype(BF16)
        part = jnp.sum(dnv * hv * r, axis=0, keepdims=True)

        @pl.when(pl.program_id(0) == 0)
        def _():
            dg_ref[...] = part

        @pl.when(pl.program_id(0) > 0)
        def _():
            dg_ref[...] += part

    return _pcall(body, name=name,
                  out_shape=(jax.ShapeDtypeStruct((t, d), F32), jax.ShapeDtypeStruct((t, d), BF16),
                             jax.ShapeDtypeStruct((1, d), F32)),
                  grid=(t // tt,), in_specs=[row, row, vec, row], out_specs=(row, row, vec),
                  dims=("arbitrary",))(dn, h, gain, dh_in)


def _swiglu_fwd(g, u, name):
    t, f = g.shape
    tt = _tile(t, 256, 16)
    row = pl.BlockSpec((tt, f), lambda i: (i, 0))

    def body(g_ref, u_ref, a_ref):
        gv = g_ref[...].astype(F32)
        a_ref[...] = (gv * _sigmoid(gv) * u_ref[...].astype(F32)).astype(BF16)

    return _pcall(body, name=name, out_shape=jax.ShapeDtypeStruct((t, f), BF16), grid=(t // tt,),
                  in_specs=[row, row], out_specs=row, dims=("parallel",))(g, u)


def _swiglu_bwd(da, g, u, name):
    t, f = g.shape
    tt = _tile(t, 256, 16)
    row = pl.BlockSpec((tt, f), lambda i: (i, 0))

    def body(da_ref, g_ref, u_ref, dg_ref, du_ref):
        gv = g_ref[...].astype(F32)
        dav = da_ref[...].astype(F32)
        sg = _sigmoid(gv)
        dg_ref[...] = (dav * u_ref[...].astype(F32) * (sg * (1.0 + gv * (1.0 - sg)))).astype(BF16)
        du_ref[...] = (dav * (gv * sg)).astype(BF16)

    return _pcall(body, name=name, out_shape=(jax.ShapeDtypeStruct((t, f), BF16),) * 2, grid=(t // tt,),
                  in_specs=[row, row, row], out_specs=(row, row), dims=("parallel",))(da, g, u)


def _rope_tables(pos_col, inv2, sign2):
    t = pos_col.shape[0]

    def body(p_ref, inv_ref, sg_ref, c_ref, s_ref):
        ang = p_ref[...].astype(F32) * inv_ref[...]
        c_ref[...] = jnp.cos(ang)
        s_ref[...] = jnp.sin(ang) * sg_ref[...]

    full = lambda shape: pl.BlockSpec(shape, lambda i: (0, 0))
    return _pcall(body, name="rope_tables", out_shape=(jax.ShapeDtypeStruct((t, RET_DIM), F32),) * 2, grid=(1,),
                  in_specs=[full((t, 1)), full((1, RET_DIM)), full((1, RET_DIM))],
                  out_specs=(full((t, RET_DIM)),) * 2, dims=("arbitrary",))(pos_col, inv2, sign2)


def _rot(x, c, s):
    return x * c + pltpu.roll(x, RET_DIM // 2, 1) * s


def _rot_t(g, c, s):
    return g * c + pltpu.roll(g * s, RET_DIM // 2, 1)


def _ret_consts():
    hh = np.arange(RET_HEADS, dtype=np.float32)
    log_gamma = np.log1p(-np.exp2(-5.0 - hh)).astype(np.float32)
    idx = np.arange(CHUNK, dtype=np.float32)
    diff = idx[:, None] - idx[None, :]
    dmask = np.where(diff >= 0, np.exp(log_gamma[:, None, None] * np.maximum(diff, 0.0)), 0.0).astype(np.float32)
    kdec = np.exp(log_gamma[:, None] * (CHUNK - 1 - idx)).astype(np.float32)
    qdec = np.exp(log_gamma[:, None] * (idx + 1.0)).astype(np.float32)
    cdec = np.exp(log_gamma * CHUNK).astype(np.float32)
    bc = lambda v: np.ascontiguousarray(np.broadcast_to(v[:, :, None], (RET_HEADS, CHUNK, RET_DIM)))
    cd = np.ascontiguousarray(np.broadcast_to(cdec[:, None, None], (RET_HEADS, 8, RET_DIM)))
    return jnp.asarray(dmask), jnp.asarray(bc(qdec)), jnp.asarray(bc(kdec)), jnp.asarray(cd)


def _ret_fwd(pm, cos2, sin2, consts):
    t = pm.shape[0]
    n_chunks = t // CHUNK
    dmask, qdec, kdec, cd = consts
    scale = RET_DIM ** -0.5
    wide = lambda c0: pl.BlockSpec((CHUNK, RET_WIDTH), lambda n: (n, c0 // RET_HEADS))
    tab = pl.BlockSpec((CHUNK, RET_DIM), lambda n: (n, 0))
    const = lambda a: pl.BlockSpec(a.shape, lambda n: (0,) * a.ndim)

    def body(q_ref, k_ref, v_ref, g_ref, c_ref, s_ref, dm_ref, qd_ref, kd_ref, cd_ref, y_ref, raw_ref, st_ref, s_acc):
        @pl.when(pl.program_id(0) == 0)
        def _():
            s_acc[...] = jnp.zeros_like(s_acc)

        c, s = c_ref[...], s_ref[...]
        for h in range(RET_HEADS):
            hs = slice(h * RET_DIM, (h + 1) * RET_DIM)
            q = _rot(q_ref[:, hs], c, s)
            k = _rot(k_ref[:, hs], c, s) * scale
            vb = v_ref[:, hs].astype(BF16)
            g = g_ref[:, hs]
            s_in = s_acc[h]
            st_ref[h] = s_in
            a = _dot(q.astype(BF16), k.astype(BF16), 1, 1) * dm_ref[h]
            y = _dot(a.astype(BF16), vb, 1, 0) + _dot((q * qd_ref[h]).astype(BF16), s_in.astype(BF16), 1, 0)
            s_acc[h] = cd_ref[h, 0:1, :] * s_in + _dot((k * kd_ref[h]).astype(BF16), vb, 0, 0)
            raw_ref[:, hs] = y
            mu = jnp.mean(y, axis=-1, keepdims=True)
            yc = y - mu
            rs = lax.rsqrt(jnp.mean(yc * yc, axis=-1, keepdims=True) + EPS)
            y_ref[:, hs] = (yc * rs * (g * _sigmoid(g))).astype(BF16)

    out_blk = pl.BlockSpec((CHUNK, RET_WIDTH), lambda n: (n, 0))
    return _pcall(
        body, name="retention_fwd",
        out_shape=(jax.ShapeDtypeStruct((t, RET_WIDTH), BF16), jax.ShapeDtypeStruct((t, RET_WIDTH), F32),
                   jax.ShapeDtypeStruct((RET_HEADS, n_chunks, RET_DIM, RET_DIM), F32)),
        grid=(n_chunks,),
        in_specs=[wide(TILE_RQ), wide(TILE_RK), wide(TILE_RV), wide(TILE_RG), tab, tab,
                  const(dmask), const(qdec), const(kdec), const(cd)],
        out_specs=(out_blk, out_blk, pl.BlockSpec((RET_HEADS, None, RET_DIM, RET_DIM), lambda n: (0, n, 0, 0))),
        scratch_shapes=[pltpu.VMEM((RET_HEADS, RET_DIM, RET_DIM), F32)],
        dims=("arbitrary",),
    )(pm, pm, pm, pm, cos2, sin2, dmask, qdec, kdec, cd)


def _ret_bwd(dy, pm, cos2, sin2, raw, states, consts):
    t = pm.shape[0]
    n_chunks = t // CHUNK
    dmask, qdec, kdec, cd = consts
    scale = RET_DIM ** -0.5
    rev = lambda n: n_chunks - 1 - n
    wide = lambda c0: pl.BlockSpec((CHUNK, RET_WIDTH), lambda n: (rev(n), c0 // RET_HEADS))
    tab = pl.BlockSpec((CHUNK, RET_DIM), lambda n: (rev(n), 0))
    blk = pl.BlockSpec((CHUNK, RET_WIDTH), lambda n: (rev(n), 0))
    const = lambda a: pl.BlockSpec(a.shape, lambda n: (0,) * a.ndim)

    def body(dy_ref, q_ref, k_ref, v_ref, g_ref, c_ref, s_ref, raw_ref, st_ref, dm_ref, qd_ref, kd_ref, cd_ref,
             dq_ref, dk_ref, dv_ref, dg_ref, ds_acc):
        @pl.when(pl.program_id(0) == 0)
        def _():
            ds_acc[...] = jnp.zeros_like(ds_acc)

        c, s = c_ref[...], s_ref[...]
        for h in range(RET_HEADS):
            hs = slice(h * RET_DIM, (h + 1) * RET_DIM)
            q = _rot(q_ref[:, hs], c, s)
            k = _rot(k_ref[:, hs], c, s) * scale
            qb, kb, vb = q.astype(BF16), k.astype(BF16), v_ref[:, hs].astype(BF16)
            g = g_ref[:, hs]
            dm, qd, kd = dm_ref[h], qd_ref[h], kd_ref[h]
            y = raw_ref[:, hs]
            mu = jnp.mean(y, axis=-1, keepdims=True)
            yc = y - mu
            rs = lax.rsqrt(jnp.mean(yc * yc, axis=-1, keepdims=True) + EPS)
            yn = yc * rs
            sg = _sigmoid(g)
            dyo = dy_ref[:, hs]
            dg_ref[:, hs] = (dyo * yn * (sg * (1.0 + g * (1.0 - sg)))).astype(BF16)
            dyn = dyo * (g * sg)
            dyr = rs * (dyn - jnp.mean(dyn, axis=-1, keepdims=True) - yn * jnp.mean(dyn * yn, axis=-1, keepdims=True))
            dyb = dyr.astype(BF16)
            s_in = st_ref[h].astype(BF16)
            ds_out = ds_acc[h]
            dsb = ds_out.astype(BF16)
            a = _dot(qb, kb, 1, 1) * dm
            da = (_dot(dyb, vb, 1, 1) * dm).astype(BF16)
            kdb = (k * kd).astype(BF16)
            qdb = (q * qd).astype(BF16)
            dv_ref[:, hs] = (_dot(a.astype(BF16), dyb, 0, 0) + _dot(kdb, dsb, 1, 0)).astype(BF16)
            dqh = _dot(da, kb, 1, 0) + _dot(dyb, s_in, 1, 1) * qd
            dkh = _dot(da, qb, 0, 0) + _dot(vb, dsb, 1, 1) * kd
            ds_acc[h] = cd_ref[h, 0:1, :] * ds_out + _dot(qdb, dyb, 0, 0)
            dq_ref[:, hs] = _rot_t(dqh, c, s).astype(BF16)
            dk_ref[:, hs] = (_rot_t(dkh, c, s) * scale).astype(BF16)

    return _pcall(
        body, name="retention_bwd",
        out_shape=(jax.ShapeDtypeStruct((t, RET_WIDTH), BF16),) * 4,
        grid=(n_chunks,),
        in_specs=[blk, wide(TILE_RQ), wide(TILE_RK), wide(TILE_RV), wide(TILE_RG), tab, tab, blk,
                  pl.BlockSpec((RET_HEADS, None, RET_DIM, RET_DIM), lambda n: (0, rev(n), 0, 0)),
                  const(dmask), const(qdec), const(kdec), const(cd)],
        out_specs=(blk,) * 4,
        scratch_shapes=[pltpu.VMEM((RET_HEADS, RET_DIM, RET_DIM), F32)],
        dims=("arbitrary",),
    )(dy, pm, pm, pm, pm, cos2, sin2, raw, states, dmask, qdec, kdec, cd)


FOX_C_LANE = FOX_DIM
FOX_NEGC_LANE = FOX_DIM + 3
FOX_LSE_LANE = FOX_DIM + 6
FOX_L_LANE = FOX_C_LANE
FOX_ROWSUM_LANE = FOX_C_LANE
FOX_COLSUM_LANE = FOX_NEGC_LANE


def _split3(x):
    hi = x.astype(BF16)
    r1 = x - hi.astype(F32)
    mid = r1.astype(BF16)
    lo = (r1 - mid.astype(F32)).astype(BF16)
    return hi, mid, lo


def _tri_dot(tri, x):
    hi, mid, lo = _split3(x)
    return _dot(tri, lo, 1, 0) + _dot(tri, mid, 1, 0) + _dot(tri, hi, 1, 0)


def _log_sigmoid(z):
    return jnp.minimum(z, 0.0) - jnp.log1p(jnp.exp(-jnp.abs(z)))


def _fox_consts():
    place = np.zeros((2, 3, LANES, FOX_TILES), np.float32)
    ones = np.zeros((3, 1, FOX_TILES), np.float32)
    for h in range(FOX_HEADS):
        for part in range(3):
            place[0, part, h, LANES * h + FOX_C_LANE + part] = 1.0
            place[1, part, h, LANES * h + FOX_NEGC_LANE + part] = -1.0
            ones[0, 0, LANES * h + FOX_NEGC_LANE + part] = 1.0
            ones[1, 0, LANES * h + FOX_C_LANE + part] = 1.0
            ones[1, 0, LANES * h + FOX_LSE_LANE + part] = 1.0
            ones[2, 0, LANES * h + FOX_C_LANE + part] = 1.0
    return jnp.asarray(place, BF16), jnp.asarray(ones, F32)


def _fox_prep(pm, bpad):
    t = pm.shape[0]
    tt = _tile(t, 512, LANES)
    place, ones = _fox_consts()
    wide = lambda c0: pl.BlockSpec((tt, FOX_TILES), lambda i: (i, c0 // FOX_HEADS))
    const = lambda a: pl.BlockSpec(a.shape, lambda i: (0,) * a.ndim)

    def body(q_ref, k_ref, v_ref, ff_ref, b_ref, pl_ref, on_ref, qa_ref, ka_ref, va_ref, carry_s):
        @pl.when(pl.program_id(0) == 0)
        def _():
            carry_s[...] = jnp.zeros_like(carry_s)

        r = lax.broadcasted_iota(jnp.int32, (LANES, LANES), 0)
        cc = lax.broadcasted_iota(jnp.int32, (LANES, LANES), 1)
        tri = jnp.where(cc <= r, 1.0, 0.0).astype(BF16)
        bias = b_ref[...]
        for sub in range(tt // LANES):
            rows = pl.ds(sub * LANES, LANES)
            cs = _tri_dot(tri, _log_sigmoid(ff_ref[rows, :] + bias)) + carry_s[...]
            carry_s[...] = cs[LANES - 1:LANES, :]
            parts = _split3(cs)
            eq = sum(_dot(part, pl_ref[0, i], 1, 0) for i, part in enumerate(parts))
            ek = sum(_dot(part, pl_ref[1, i], 1, 0) for i, part in enumerate(parts))
            qa_ref[rows, :] = (q_ref[rows, :] * FOX_DIM ** -0.5 + eq + on_ref[0]).astype(BF16)
            ka_ref[rows, :] = (k_ref[rows, :] + ek + on_ref[1]).astype(BF16)
            va_ref[rows, :] = (v_ref[rows, :] + on_ref[2]).astype(BF16)

    out = pl.BlockSpec((tt, FOX_TILES), lambda i: (i, 0))
    return _pcall(body, name="fox_prep", out_shape=(jax.ShapeDtypeStruct((t, FOX_TILES), BF16),) * 3, grid=(t // tt,),
                  in_specs=[wide(TILE_FQ), wide(TILE_FK), wide(TILE_FV), pl.BlockSpec((tt, LANES), lambda i: (i, TILE_FF)),
                            pl.BlockSpec((1, LANES), lambda i: (0, 0)), const(place), const(ones)],
                  out_specs=(out,) * 3, scratch_shapes=[pltpu.VMEM((1, LANES), F32)],
                  dims=("arbitrary",))(pm, pm, pm, pm, bpad, place, ones)


def _fox_post(dc, pm, bpad):
    t = pm.shape[0]
    nb = t // LANES

    def body(dc_ref, ff_ref, b_ref, d_ref, db_ref):
        r = lax.broadcasted_iota(jnp.int32, (LANES, LANES), 0)
        cc = lax.broadcasted_iota(jnp.int32, (LANES, LANES), 1)
        tri = jnp.where(cc >= r, 1.0, 0.0).astype(BF16)
        bias = b_ref[...]

        def step(i, carry):
            tail, acc = carry
            rows = pl.ds(pl.multiple_of((nb - 1 - i) * LANES, LANES), LANES)
            cs = _tri_dot(tri, dc_ref[rows, :]) + tail
            dff = cs * _sigmoid(-(ff_ref[rows, :] + bias))
            d_ref[rows, :] = dff.astype(BF16)
            return cs[0:1, :], acc + jnp.sum(dff, axis=0, keepdims=True)

        zero = jnp.zeros((1, LANES), F32)
        _, acc = lax.fori_loop(0, nb, step, (zero, zero))
        db_ref[...] = acc

    return _pcall(body, name="fox_forget_bwd",
                  out_shape=(jax.ShapeDtypeStruct((t, LANES), BF16), jax.ShapeDtypeStruct((1, LANES), F32)), grid=(1,),
                  in_specs=[pl.BlockSpec((t, LANES), lambda i: (0, 0)), pl.BlockSpec((t, LANES), lambda i: (0, TILE_FF)),
                            pl.BlockSpec((1, LANES), lambda i: (0, 0))],
                  out_specs=(pl.BlockSpec((t, LANES), lambda i: (0, 0)), pl.BlockSpec((1, LANES), lambda i: (0, 0))),
                  dims=("arbitrary",))(dc, pm, bpad)


def _tri_tables(nb, q_major):
    pairs = [(i, j) for i in range(nb) for j in range(i + 1)] if q_major else \
            [(i, j) for j in range(nb) for i in range(j, nb)]
    return jnp.asarray([a for a, _ in pairs], jnp.int32), jnp.asarray([b for _, b in pairs], jnp.int32)


def _causal(s):
    n = s.shape[0]
    row = lax.broadcasted_iota(jnp.int32, (n, n), 0)
    col = lax.broadcasted_iota(jnp.int32, (n, n), 1)
    return jnp.where(col <= row, s, NEG)


def _lane_col(x, lane):
    sel = lax.broadcasted_iota(jnp.int32, x.shape, 1) == lane
    return jnp.sum(jnp.where(sel, x, 0.0), axis=1, keepdims=True)


def _fox_fwd(qa, ka, va, blk):
    t = qa.shape[0]
    nb = t // blk
    qi, kj = _tri_tables(nb, True)
    q_spec = pl.BlockSpec((blk, LANES), lambda h, s, qi_r, kj_r: (qi_r[s], h))
    k_spec = pl.BlockSpec((blk, LANES), lambda h, s, qi_r, kj_r: (kj_r[s], h))

    def body(qi_r, kj_r, q_ref, k_ref, v_ref, o_ref, ob_ref, qb_ref, m_s, acc_s):
        s_id = pl.program_id(1)
        i, j = qi_r[s_id], kj_r[s_id]

        @pl.when(j == 0)
        def _():
            m_s[...] = jnp.full_like(m_s, NEG)
            acc_s[...] = jnp.zeros_like(acc_s)

        def tile(diagonal):
            s = _dot(q_ref[...], k_ref[...], 1, 1)
            if diagonal:
                s = _causal(s)
            m_old = m_s[...]
            m_new = jnp.maximum(m_old, jnp.max(s, axis=1, keepdims=True))
            p = jnp.exp(s - jnp.tile(m_new, (1, blk // LANES)))
            acc_s[...] = jnp.exp(m_old - m_new) * acc_s[...] + _dot(p.astype(BF16), v_ref[...], 1, 0)
            m_s[...] = m_new

        @pl.when(j < i)
        def _():
            tile(False)

        @pl.when(j == i)
        def _():
            tile(True)
            acc = acc_s[...]
            l = _lane_col(acc, FOX_L_LANE)
            o = acc / l
            o_ref[...] = o
            ob_ref[...] = o.astype(BF16)
            hi, mid, lo = _split3(-(m_s[:, 0:1] + jnp.log(l)))
            lane = lax.broadcasted_iota(jnp.int32, acc.shape, 1)
            qb_ref[...] = jnp.where(lane == FOX_LSE_LANE, hi,
                                    jnp.where(lane == FOX_LSE_LANE + 1, mid,
                                              jnp.where(lane == FOX_LSE_LANE + 2, lo, q_ref[...])))

    wide = (t, FOX_TILES)
    return pl.pallas_call(
        body, name="fox_fwd",
        out_shape=(jax.ShapeDtypeStruct(wide, F32), jax.ShapeDtypeStruct(wide, BF16), jax.ShapeDtypeStruct(wide, BF16)),
        grid_spec=pltpu.PrefetchScalarGridSpec(
            num_scalar_prefetch=2, grid=(FOX_HEADS, qi.shape[0]), in_specs=[q_spec, k_spec, k_spec],
            out_specs=(q_spec,) * 3,
            scratch_shapes=[pltpu.VMEM((blk, LANES), F32), pltpu.VMEM((blk, LANES), F32)]),
        compiler_params=_params(("parallel", "arbitrary")),
    )(qi, kj, qa, ka, va)


def _fox_bwd(qa, ka, va, do, o, blk):
    t = qa.shape[0]
    nb = t // blk
    qi, kj = _tri_tables(nb, False)
    q_spec = pl.BlockSpec((blk, LANES), lambda h, s, qi_r, kj_r: (qi_r[s], h))
    k_spec = pl.BlockSpec((blk, LANES), lambda h, s, qi_r, kj_r: (kj_r[s], h))
    head_spec = pl.BlockSpec((t, LANES), lambda h, s, qi_r, kj_r: (0, h))
    head_col = pl.BlockSpec((None, t, 1), lambda h, s, qi_r, kj_r: (h, 0, 0))
    k_col = pl.BlockSpec((None, blk, 1), lambda h, s, qi_r, kj_r: (h, kj_r[s], 0))
    first_spec = pl.BlockSpec((blk, LANES), lambda h, s, qi_r, kj_r: (jnp.where(kj_r[s] == 0, qi_r[s], nb - 1), h))
    n_steps = int(qi.shape[0])

    def body(qi_r, kj_r, q_ref, k_ref, v_ref, do_ref, o_ref, dq_ref, dk_ref, dv_ref, rs_ref, cs_ref,
             doa_s, dq_s, dk_s, dv_s):
        s_id = pl.program_id(1)
        i, j = qi_r[s_id], kj_r[s_id]
        rows = pl.ds(pl.multiple_of(i * blk, blk), blk)

        @pl.when(j == 0)
        def _():
            dof = do_ref[...]
            hi, mid, lo = _split3(-jnp.sum(dof * o_ref[...], axis=1, keepdims=True))
            lane = lax.broadcasted_iota(jnp.int32, dof.shape, 1)
            doa = jnp.where(lane == FOX_C_LANE, hi.astype(F32),
                            jnp.where(lane == FOX_C_LANE + 1, mid.astype(F32),
                                      jnp.where(lane == FOX_C_LANE + 2, lo.astype(F32), dof)))
            doa_s[rows, :] = doa.astype(BF16)
            dq_s[rows, :] = jnp.zeros((blk, LANES), F32)

        @pl.when(i == j)
        def _():
            dk_s[...] = jnp.zeros_like(dk_s)
            dv_s[...] = jnp.zeros_like(dv_s)

        def tile(diagonal):
            q, k = q_ref[...], k_ref[...]
            s = _dot(q, k, 1, 1)
            if diagonal:
                s = _causal(s)
            p = jnp.exp(s)
            doa = doa_s[rows, :]
            ds = (p * _dot(doa, v_ref[...], 1, 1)).astype(BF16)
            dv_s[...] += _dot(p.astype(BF16), doa, 0, 0)
            dk_s[...] += _dot(ds, q, 0, 0)
            dq_s[rows, :] += _dot(ds, k, 1, 0)

        @pl.when(i > j)
        def _():
            tile(False)

        @pl.when(i == j)
        def _():
            tile(True)

        @pl.when(i == nb - 1)
        def _():
            dk = dk_s[...]
            dk_ref[...] = dk.astype(BF16)
            dv_ref[...] = dv_s[...].astype(BF16)
            cs_ref[...] = -_lane_col(dk, FOX_COLSUM_LANE)

        @pl.when(s_id == n_steps - 1)
        def _():
            dq = dq_s[...]
            dq_ref[...] = (dq * FOX_DIM ** -0.5).astype(BF16)
            rs_ref[...] = _lane_col(dq, FOX_ROWSUM_LANE)

    wide = jax.ShapeDtypeStruct((t, FOX_TILES), BF16)
    cols = jax.ShapeDtypeStruct((FOX_HEADS, t, 1), F32)
    return pl.pallas_call(
        body, name="fox_bwd", out_shape=(wide, wide, wide, cols, cols),
        grid_spec=pltpu.PrefetchScalarGridSpec(
            num_scalar_prefetch=2, grid=(FOX_HEADS, n_steps),
            in_specs=[q_spec, k_spec, k_spec, first_spec, first_spec],
            out_specs=(head_spec, k_spec, k_spec, head_col, k_col),
            scratch_shapes=[pltpu.VMEM((t, LANES), BF16), pltpu.VMEM((t, LANES), F32), pltpu.VMEM((blk, LANES), F32),
                            pltpu.VMEM((blk, LANES), F32)]),
        compiler_params=_params(("parallel", "arbitrary")),
    )(qi, kj, qa, ka, va, do, o)


def _merge_fwd(gm, bm, za, zb):
    t, d = za.shape
    tt = _tile(t, 256, 16)
    row = pl.BlockSpec((tt, d), lambda i: (i, 0))

    def body(gm_ref, b_ref, za_ref, zb_ref, o_ref):
        ga = _sigmoid(gm_ref[:, :d] + b_ref[:, :d])
        gb = _sigmoid(gm_ref[:, d:] + b_ref[:, d:])
        o_ref[...] = (ga * za_ref[...] + gb * zb_ref[...]).astype(BF16)

    return _pcall(body, name="merge_fwd", out_shape=jax.ShapeDtypeStruct((t, d), BF16), grid=(t // tt,),
                  in_specs=[pl.BlockSpec((tt, 2 * d), lambda i: (i, 0)), pl.BlockSpec((1, 2 * d), lambda i: (0, 0)), row, row],
                  out_specs=row, dims=("parallel",))(gm, bm, za, zb)


def _merge_bwd(dmix, gm, bm, za, zb):
    t, d = za.shape
    tt = _tile(t, 256, 16)
    row = pl.BlockSpec((tt, d), lambda i: (i, 0))
    wide = pl.BlockSpec((tt, 2 * d), lambda i: (i, 0))
    vec = pl.BlockSpec((1, 2 * d), lambda i: (0, 0))

    def body(dm_ref, gm_ref, b_ref, za_ref, zb_ref, dza_ref, dzb_ref, dgm_ref, db_ref):
        dm = dm_ref[...]
        ga = _sigmoid(gm_ref[:, :d] + b_ref[:, :d])
        gb = _sigmoid(gm_ref[:, d:] + b_ref[:, d:])
        dza_ref[...] = (dm * ga).astype(BF16)
        dzb_ref[...] = (dm * gb).astype(BF16)
        dla = dm * za_ref[...] * ga * (1.0 - ga)
        dlb = dm * zb_ref[...] * gb * (1.0 - gb)
        dgm_ref[:, :d] = dla.astype(BF16)
        dgm_ref[:, d:] = dlb.astype(BF16)
        pa = jnp.sum(dla, axis=0, keepdims=True)
        pb = jnp.sum(dlb, axis=0, keepdims=True)

        @pl.when(pl.program_id(0) == 0)
        def _():
            db_ref[:, :d] = pa
            db_ref[:, d:] = pb

        @pl.when(pl.program_id(0) > 0)
        def _():
            db_ref[:, :d] += pa
            db_ref[:, d:] += pb

    return _pcall(body, name="merge_bwd",
                  out_shape=(jax.ShapeDtypeStruct((t, d), BF16), jax.ShapeDtypeStruct((t, d), BF16),
                             jax.ShapeDtypeStruct((t, 2 * d), BF16), jax.ShapeDtypeStruct((1, 2 * d), F32)),
                  grid=(t // tt,), in_specs=[row, wide, vec, row, row], out_specs=(row, row, wide, vec),
                  dims=("arbitrary",))(dmix, gm, bm, za, zb)


def _ple_final(h3, pgl, pe, gain, target):
    t, d = h3.shape
    tt = _tile(t, 256, 16)
    row = pl.BlockSpec((tt, d), lambda i: (i, 0))
    vec = pl.BlockSpec((1, d), lambda i: (0, 0))
    lvec = pl.BlockSpec((1, LANES), lambda i: (0, 0))

    def body(h_ref, pgl_ref, pe_ref, g_ref, t_ref, dh_ref, dsg_ref, dpe_ref, loss_ref, dg_ref):
        pg = _sigmoid(pgl_ref[...])
        pe_v = pe_ref[...]
        h4 = h_ref[...] + pg * pe_v
        r = lax.rsqrt(jnp.mean(h4 * h4, axis=-1, keepdims=True) + EPS)
        gv = g_ref[...]
        err = h4 * r * gv - t_ref[...]
        part_loss = 0.5 * jnp.sum(jnp.mean(err * err, axis=-1, keepdims=True), axis=0, keepdims=True)
        dy = err * (1.0 / d)
        part_g = jnp.sum(dy * h4 * r, axis=0, keepdims=True)
        dyg = dy * gv
        dh = r * dyg - h4 * (r * r * r) * jnp.mean(dyg * h4, axis=-1, keepdims=True)
        dh_ref[...] = dh
        dsg_ref[...] = (dh * pe_v * pg * (1.0 - pg)).astype(BF16)
        dpe_ref[...] = (dh * pg).astype(BF16)

        @pl.when(pl.program_id(0) == 0)
        def _():
            loss_ref[...] = jnp.broadcast_to(part_loss, (1, LANES))
            dg_ref[...] = part_g

        @pl.when(pl.program_id(0) > 0)
        def _():
            loss_ref[...] += jnp.broadcast_to(part_loss, (1, LANES))
            dg_ref[...] += part_g

    return _pcall(body, name="ple_final",
                  out_shape=(jax.ShapeDtypeStruct((t, d), F32), jax.ShapeDtypeStruct((t, d), BF16),
                             jax.ShapeDtypeStruct((t, d), BF16), jax.ShapeDtypeStruct((1, LANES), F32),
                             jax.ShapeDtypeStruct((1, d), F32)),
                  grid=(t // tt,), in_specs=[row, row, row, vec, row], out_specs=(row, row, row, lvec, vec),
                  dims=("arbitrary",))(h3, pgl, pe, gain, target)


def _adamw_math(w, g, m, v):
    m = ADAM_B1 * m + (1.0 - ADAM_B1) * g
    v = ADAM_B2 * v + (1.0 - ADAM_B2) * (g * g)
    m_hat = m / (1.0 - ADAM_B1 ** ADAM_STEP)
    v_hat = v / (1.0 - ADAM_B2 ** ADAM_STEP)
    delta = -ADAM_LR * (m_hat / (jnp.sqrt(v_hat) + ADAM_EPS) + ADAM_WD * w)
    return delta, m, v


def _adamw(parts, w, m, v, name):
    n, r, c = parts.shape
    tr = _tile(r, 256, 16)
    row = pl.BlockSpec((tr, c), lambda i: (i, 0))

    def body(p_ref, w_ref, m_ref, v_ref, g_ref, d_ref, mo_ref, vo_ref):
        g = p_ref[0].astype(F32)
        for s in range(1, n):
            g = g + p_ref[s].astype(F32)
        g_ref[...] = g
        d_ref[...], mo_ref[...], vo_ref[...] = _adamw_math(w_ref[...], g, m_ref[...], v_ref[...])

    return _pcall(body, name=name, out_shape=(jax.ShapeDtypeStruct((r, c), F32),) * 4, grid=(r // tr,),
                  in_specs=[pl.BlockSpec((n, tr, c), lambda i: (0, i, 0)), row, row, row], out_specs=(row,) * 4,
                  dims=("parallel",))(parts, w, m, v)


ANY = pl.BlockSpec(memory_space=pl.ANY)


def _all_gather(shards):
    n = len(shards)

    def body(*refs):
        x_refs, out_refs = refs[:n], refs[n:2 * n]
        send_sems, recv_sems, local_sems = refs[2 * n:]
        x, y, cc = lax.axis_index("x"), lax.axis_index("y"), lax.axis_index("c")
        me, sibling = (x, y, cc), (x, y, 1 - cc)
        chips = [(1 - x, y), (x, 1 - y), (1 - x, 1 - y)]

        def slot(a, px, py, pc):
            return out_refs[a].at[4 * px + 2 * py + pc]

        def copy(a, k, block, to, src=None):
            return pltpu.make_async_remote_copy(
                src_ref=slot(a, *block) if src is None else src, dst_ref=slot(a, *block),
                send_sem=send_sems.at[7 * a + k], recv_sem=recv_sems.at[7 * a + k], device_id=to, device_id_type=MESH)

        local, sent = [], []
        for a in range(n):
            local.append(pltpu.make_async_copy(x_refs[a], slot(a, *me), local_sems.at[a]))
            sent.append(copy(a, 0, me, sibling, src=x_refs[a]))
            sent += [copy(a, 1 + j, me, (*chip, cc), src=x_refs[a]) for j, chip in enumerate(chips)]
        for cp in local + sent:
            cp.start()
        for j, chip in enumerate(chips):
            for a in range(n):
                copy(a, 1 + j, (*chip, cc), me).wait_recv()
                sent.append(copy(a, 4 + j, (*chip, cc), sibling))
                sent[-1].start()
        for a in range(n):
            copy(a, 0, sibling, me).wait_recv()
            for j, chip in enumerate(chips):
                copy(a, 4 + j, (*chip, 1 - cc), me).wait_recv()
        for cp in sent:
            cp.wait_send()
        for cp in local:
            cp.wait()

    return pl.pallas_call(
        body, name="weights_all_gather",
        out_shape=tuple(jax.ShapeDtypeStruct((N_DEV,) + s.shape, s.dtype) for s in shards),
        in_specs=[ANY] * n, out_specs=(ANY,) * n,
        scratch_shapes=[pltpu.SemaphoreType.DMA((7 * n,)), pltpu.SemaphoreType.DMA((7 * n,)),
                        pltpu.SemaphoreType.DMA((n,))],
    )(*shards)


def _reduce_scatter_exchange(blocks):
    n = len(blocks)

    def body(*refs):
        g_refs, recv_refs = refs[:n], refs[n:2 * n]
        send_sems, recv_sems, local_sems = refs[2 * n:]
        x, y, cc = lax.axis_index("x"), lax.axis_index("y"), lax.axis_index("c")
        me = 4 * x + 2 * y + cc
        local, sent, landing = [], [], []
        for a in range(n):
            local.append(pltpu.make_async_copy(g_refs[a].at[me], recv_refs[a].at[me], local_sems.at[a]))
        for k in range(1, N_DEV):
            px, py, pc = x ^ (k >> 2), y ^ ((k >> 1) & 1), cc ^ (k & 1)
            peer = 4 * px + 2 * py + pc
            for a in range(n):
                sems = dict(send_sem=send_sems.at[7 * a + k - 1], recv_sem=recv_sems.at[7 * a + k - 1],
                            device_id=(px, py, pc), device_id_type=MESH)
                sent.append(pltpu.make_async_remote_copy(src_ref=g_refs[a].at[peer], dst_ref=recv_refs[a].at[me], **sems))
                landing.append(pltpu.make_async_remote_copy(src_ref=g_refs[a].at[me], dst_ref=recv_refs[a].at[peer], **sems))
        for cp in local + sent:
            cp.start()
        for cp in landing:
            cp.wait_recv()
        for cp in sent:
            cp.wait_send()
        for cp in local:
            cp.wait()

    return pl.pallas_call(
        body, name="grads_reduce_scatter_exchange",
        out_shape=tuple(jax.ShapeDtypeStruct(b.shape, b.dtype) for b in blocks),
        in_specs=[ANY] * n, out_specs=(ANY,) * n,
        scratch_shapes=[pltpu.SemaphoreType.DMA((7 * n,)), pltpu.SemaphoreType.DMA((7 * n,)),
                        pltpu.SemaphoreType.DMA((n,))],
    )(*blocks)


HBM = pl.BlockSpec(memory_space=pltpu.HBM)
SEM = pl.BlockSpec(memory_space=pltpu.SEMAPHORE)
DATAFLOW = pltpu.SideEffectType.DATAFLOW_SIDE_EFFECTING


def _peers():
    x, y, cc = lax.axis_index("x"), lax.axis_index("y"), lax.axis_index("c")
    out = []
    for k in range(1, N_DEV):
        px, py, pc = x ^ (k >> 2), y ^ ((k >> 1) & 1), cc ^ (k & 1)
        out.append((k, (px, py, pc), 4 * px + 2 * py + pc))
    return 4 * x + 2 * y + cc, out


def _scatter_start(blocks, after, name, gather=False):
    n = len(blocks)
    lands = [lax.empty((N_DEV,) + b.shape if gather else b.shape, b.dtype) for b in blocks]

    def body(*refs):
        g_refs, land_refs = refs[:n], refs[n:2 * n]
        send_sems, recv_sems, token = refs[2 * n + 1], refs[2 * n + 2], refs[-1]
        me, peers = _peers()
        for k, peer, slot in peers:
            for a in range(n):
                pltpu.make_async_remote_copy(
                    src_ref=g_refs[a] if gather else g_refs[a].at[slot], dst_ref=land_refs[a].at[me],
                    send_sem=send_sems.at[7 * a + k - 1],
                    recv_sem=recv_sems.at[7 * a + k - 1], device_id=peer, device_id_type=MESH).start()
        token[...] = jnp.zeros_like(token)

    thru = [pltpu.HBM(b.shape, b.dtype) for b in blocks]
    thru_lands = [pltpu.HBM(b.shape, b.dtype) for b in lands]
    return pl.pallas_call(
        body, name=name,
        out_shape=(pltpu.SemaphoreType.DMA((7 * n,)), pltpu.SemaphoreType.DMA((7 * n,)), *thru, *thru_lands,
                   jax.ShapeDtypeStruct((8, LANES), F32)),
        in_specs=[HBM] * (2 * n) + [pl.BlockSpec(memory_space=pl.ANY)],
        out_specs=(SEM, SEM, *[HBM] * (2 * n), pl.BlockSpec(memory_space=pltpu.VMEM)),
        input_output_aliases={i: 2 + i for i in range(2 * n)},
        compiler_params=pltpu.CompilerParams(has_side_effects=DATAFLOW),
    )(*[pltpu.with_memory_space_constraint(a, pltpu.HBM) for a in list(blocks) + lands], after)


def _scatter_wait(started, after, name, gather=False):
    send_sems, recv_sems, *rest = started
    n = (len(rest) - 1) // 2
    thru = rest[:2 * n]

    def body(*refs):
        g_refs, land_refs = refs[:n], refs[n:2 * n]
        send_sems, recv_sems = refs[2 * n], refs[2 * n + 1]
        me, peers = _peers()
        for k, peer, slot in peers:
            for a in range(n):
                copy = pltpu.make_async_remote_copy(
                    src_ref=g_refs[a] if gather else g_refs[a].at[slot], dst_ref=land_refs[a].at[slot],
                    send_sem=send_sems.at[7 * a + k - 1],
                    recv_sem=recv_sems.at[7 * a + k - 1], device_id=peer, device_id_type=MESH)
                copy.wait_send()
                copy.wait_recv()

    out = pl.pallas_call(
        body, name=name, out_shape=tuple(pltpu.HBM(a.shape, a.dtype) for a in thru),
        in_specs=[HBM] * (2 * n) + [SEM, SEM, pl.BlockSpec(memory_space=pl.ANY)], out_specs=tuple([HBM] * (2 * n)),
        input_output_aliases={i: i for i in range(2 * n)},
        compiler_params=pltpu.CompilerParams(has_side_effects=DATAFLOW),
    )(*thru, send_sems, recv_sems, after)
    return out[:n], out[n:]


BIG = (("w_ffn1_gate", "colT"), ("w_ffn1_up", "colT"), ("w_ffn1_down", "row"), ("w_ffn2_gate", "colT"),
       ("w_ffn2_up", "colT"), ("w_ffn2_down", "row"), ("w_in", "colT"), ("w_merge", "col"), ("w_ret_out", "col"),
       ("w_fox_out", "col"), ("w_out", "row"), ("w_ple", "col"), ("w_ple_gate", "row"))


def _shard_view(a, kind):
    a = a.reshape(a.shape[-2:])
    return a.T if kind == "colT" else a


def _unview(a, kind, shape):
    return (a.T if kind == "colT" else a).reshape(shape)


def _full_from_slots(g, kind):
    n, r, c = g.shape
    return g.transpose(1, 0, 2).reshape(r, n * c) if kind == "col" else g.reshape(n * r, c)


def _slots_from_full(f, kind):
    r, c = f.shape
    return f.reshape(r, N_DEV, c // N_DEV).transpose(1, 0, 2) if kind == "col" else f.reshape(N_DEV, r // N_DEV, c)


EARLY_GROUPS = (("w_ple_gate", "w_ple", "w_ffn2_down", "w_ffn2_gate", "w_ffn2_up", "w_out", "w_ret_out", "w_fox_out"),
                ("w_in", "w_merge"))


def _scatter_group(gw, names, after, name):
    kind = dict(BIG)
    return names, _scatter_start([_slots_from_full(gw[n], kind[n]) for n in names], after, name)


def _pad_heads(w):
    d = w.shape[1]
    return jnp.pad(w.reshape(FOX_HEADS, FOX_DIM, d), ((0, 0), (0, LANES - FOX_DIM), (0, 0))).reshape(FOX_TILES, d)


def _unpad_heads(w):
    d = w.shape[1]
    return w.reshape(FOX_HEADS, LANES, d)[:, :FOX_DIM].reshape(FOX_WIDTH, d)


def _deinterleave_rows(w):
    d = w.shape[1]
    return w.reshape(RET_HEADS, RET_DIM // 2, 2, d).transpose(0, 2, 1, 3).reshape(RET_WIDTH, d)


def _interleave_rows(w):
    d = w.shape[1]
    return w.reshape(RET_HEADS, 2, RET_DIM // 2, d).transpose(0, 2, 1, 3).reshape(RET_WIDTH, d)


def _pad_w_in(wt):
    d = wt.shape[1]
    rw, fw = RET_WIDTH, FOX_WIDTH
    fo = 4 * rw
    return jnp.concatenate([
        _deinterleave_rows(wt[:rw]), _deinterleave_rows(wt[rw:2 * rw]), wt[2 * rw:4 * rw],
        _pad_heads(wt[fo:fo + fw]), _pad_heads(wt[fo + fw:fo + 2 * fw]), _pad_heads(wt[fo + 2 * fw:fo + 3 * fw]),
        wt[fo + 3 * fw:], jnp.zeros((2 * LANES - FOX_HEADS, d), wt.dtype)], axis=0)


def _unpad_w_in(g):
    rw = RET_WIDTH
    f0 = 4 * rw
    return jnp.concatenate([
        _interleave_rows(g[:rw]), _interleave_rows(g[rw:2 * rw]), g[2 * rw:4 * rw],
        _unpad_heads(g[f0:f0 + FOX_TILES]), _unpad_heads(g[f0 + FOX_TILES:f0 + 2 * FOX_TILES]),
        _unpad_heads(g[f0 + 2 * FOX_TILES:f0 + 3 * FOX_TILES]),
        g[f0 + 3 * FOX_TILES:f0 + 3 * FOX_TILES + FOX_HEADS]], axis=0)


SMALL = ("ln_ffn1", "ln_mix", "b_forget", "b_merge", "ln_ffn2", "ln_ple", "ln_final")


def _small_rows(n):
    rows = -(-n // LANES)
    return -(-rows // 8) * 8


def _pack_small(vals, with_loss=None):
    parts = []
    for name in SMALL:
        v = vals[name].reshape(-1).astype(F32)
        rows = _small_rows(v.shape[0])
        parts.append(jnp.pad(v, (0, rows * LANES - v.shape[0])).reshape(rows, LANES))
    if with_loss is not None:
        parts.append(jnp.pad(with_loss.reshape(1, LANES), ((0, 7), (0, 0))))
    else:
        parts.append(jnp.zeros((8, LANES), F32))
    return jnp.concatenate(parts, axis=0)


def _unpack_small(packed, shapes):
    out, at = {}, 0
    for name in SMALL:
        n = int(np.prod(shapes[name]))
        rows = _small_rows(n)
        out[name] = packed[at:at + rows].reshape(-1)[:n].reshape(shapes[name])
        at += rows
    return out, packed[at, 0]


def _gather_finish(group, after, me):
    names, started, wait_name = group
    kind = dict(BIG)
    sent, landed = _scatter_wait(started, after, wait_name, gather=True)
    return {n: _full_from_slots(lax.dynamic_update_slice_in_dim(land, shard[None], me, 0), kind[n])
            for n, shard, land in zip(names, sent, landed)}


def _local_step(x, p, positions, target, w, small, me, entry_token, gathers):
    t, d = x.shape
    gain = lambda n: small[n].reshape(1, d)
    w = dict(w)
    bpad = jnp.pad(small["b_forget"].reshape(1, FOX_HEADS), ((0, 0), (0, LANES - FOX_HEADS)))
    bm = small["b_merge"].reshape(1, 2 * d)
    fox_blk = _tile(t, 1024, 128)

    def ffn_fwd(n, tag, down_gather=None):
        g = _mm([(n, w[f"w_{tag}_gate"], "nt")], BF16, f"{tag}_gate", tn=1408)
        u = _mm([(n, w[f"w_{tag}_up"], "nt")], BF16, f"{tag}_up", tn=1408)
        a = _swiglu_fwd(g, u, f"{tag}_swiglu")
        if down_gather is not None:
            w.update(_gather_finish(down_gather, a, me))
        return g, u, a, _mm([(a, w[f"w_{tag}_down"], "nn")], F32, f"{tag}_down")

    n1 = _rms_fwd(x, gain("ln_ffn1") + entry_token, "rms_ffn1")
    g1, u1, a1, f1 = ffn_fwd(n1, "ffn1", down_gather=gathers[0])
    h1, u = _rms_fwd(x, gain("ln_mix"), "rms_mix", f=f1)
    w.update(_gather_finish(gathers[1], f1, me))
    w_in_t = _pad_w_in(w["w_in"])
    gm = _mm([(u, w["w_merge"], "nn")], F32, "mixer_gates")
    pm = _mm([(u, w_in_t, "nt")], F32, "mixer_in", tn=1792)

    half = jnp.arange(RET_DIM // 2, dtype=F32) / (RET_DIM // 2)
    inv = 1.0 / (ROPE_BASE ** half)
    inv2 = jnp.concatenate([inv, inv]).reshape(1, RET_DIM)
    sign2 = jnp.concatenate([-jnp.ones((RET_DIM // 2,), F32), jnp.ones((RET_DIM // 2,), F32)]).reshape(1, RET_DIM)
    cos2, sin2 = _rope_tables(positions.reshape(t, 1), inv2, sign2)
    consts = _ret_consts()
    y_ret, y_raw, states = _ret_fwd(pm, cos2, sin2, consts)
    w.update(_gather_finish(gathers[2], y_raw, me))
    w_fox_pad = _pad_heads(w["w_fox_out"])
    za = _mm([(y_ret, w["w_ret_out"], "nn")], F32, "ret_out")

    qa, ka, va = _fox_prep(pm, bpad)
    o_fox, y_fox, qa_b = _fox_fwd(qa, ka, va, fox_blk)
    zb = _mm([(y_fox, w_fox_pad, "nn")], F32, "fox_out")

    mix = _merge_fwd(gm, bm, za, zb)
    mo = _mm([(mix, w["w_out"], "nn")], F32, "mix_out")
    h2, n2 = _rms_fwd(h1, gain("ln_ffn2"), "rms_ffn2", f=mo, scale=1.0)
    w.update(_gather_finish(gathers[3], mo, me))
    g2, u2, a2, f2 = ffn_fwd(n2, "ffn2")
    h3, n3 = _rms_fwd(h2, gain("ln_ple"), "rms_ple", f=f2)
    pgl = _mm([(n3, w["w_ple_gate"], "nn")], F32, "ple_gate")
    pb = p.astype(BF16)
    pe = _mm([(pb, w["w_ple"], "nn")], F32, "ple_embed")

    gw, gs = {}, {}
    dh4, dsg, dpe, loss, gs["ln_final"] = _ple_final(h3, pgl, pe, gain("ln_final"), target)
    gw["w_ple_gate"] = _mm([(n3, dsg, "tn")], BF16, "d_w_ple_gate", tn=256)
    gw["w_ple"] = _mm([(pb, dpe, "tn")], BF16, "d_w_ple", tn=256)
    dn3 = _mm([(dsg, w["w_ple_gate"], "nt")], F32, "d_n3")
    dh3, dh3_half, gs["ln_ple"] = _rms_bwd(dn3, h3, gain("ln_ple"), dh4, "rms_ple_bwd", 0.5)

    def ffn_bwd(dh_half, g, u_, a, n, tag, scatter_now=None):
        gw[f"w_{tag}_down"] = _mm([(a, dh_half, "tn")], BF16, f"d_w_{tag}_down", tm=1408, tn=256)
        start = scatter_now if scatter_now is not None else (lambda *_: None)
        token = start((f"w_{tag}_down",), dh_half, "c")
        da = _mm([(dh_half, w[f"w_{tag}_down"], "nt")], BF16, f"d_a_{tag}", tn=1408, after=token)
        dg, du_ = _swiglu_bwd(da, g, u_, f"{tag}_swiglu_bwd")
        gw[f"w_{tag}_gate"] = _mm([(dg, n, "tn")], BF16, f"d_w_{tag}_gate", tm=1408, tn=256)
        token = start((f"w_{tag}_gate",), da, "d")
        gw[f"w_{tag}_up"] = _mm([(du_, n, "tn")], BF16, f"d_w_{tag}_up", tm=1408, tn=256, after=token)
        token = start((f"w_{tag}_up",), da, "e")
        return _mm([(dg, w[f"w_{tag}_gate"], "nn"), (du_, w[f"w_{tag}_up"], "nn")], F32, f"d_n_{tag}", tm=512,
                   after=token)

    dn2 = ffn_bwd(dh3_half, g2, u2, a2, n2, "ffn2")
    dh2, dh2_b, gs["ln_ffn2"] = _rms_bwd(dn2, h2, gain("ln_ffn2"), dh3, "rms_ffn2_bwd", 1.0)

    gw["w_out"] = _mm([(mix, dh2_b, "tn")], BF16, "d_w_out", tn=256)
    dmix = _mm([(dh2_b, w["w_out"], "nt")], F32, "d_mix")
    dza, dzb, dgm, gs["b_merge"] = _merge_bwd(dmix, gm, bm, za, zb)
    gw["w_ret_out"] = _mm([(y_ret, dza, "tn")], BF16, "d_w_ret_out", tn=256)
    gw["w_fox_out"] = _unpad_heads(_mm([(y_fox, dzb, "tn")], BF16, "d_w_fox_out", tn=256))
    dy_ret = _mm([(dza, w["w_ret_out"], "nt")], F32, "d_y_ret")
    do_fox = _mm([(dzb, w_fox_pad, "nt")], F32, "d_y_fox")

    pending = [_scatter_group(gw, EARLY_GROUPS[0], dy_ret, "grads_scatter_a_start")]
    token = pending[0][1][-1][0, 0]
    drq, drk, drv, drg = _ret_bwd(dy_ret, pm, cos2, sin2, y_raw, states, consts[:3] + (consts[3] + token,))

    dqa, dka, dva, ds_rows, ds_cols = _fox_bwd(qa_b, ka, va, do_fox, o_fox, fox_blk)
    dc = jnp.pad((ds_rows + ds_cols).reshape(FOX_HEADS, t).T, ((0, 0), (0, LANES - FOX_HEADS)))
    dff, db_forget = _fox_post(dc, pm, bpad)
    gs["b_forget"] = db_forget[:, :FOX_HEADS]

    dpm = jnp.concatenate([drq, drk, drv, drg, dqa, dka, dva, dff, jnp.zeros((t, LANES), BF16)], axis=1)
    gw["w_merge"] = _mm([(u, dgm, "tn")], BF16, "d_w_merge", tn=512)
    gw["w_in"] = _unpad_w_in(_mm([(dpm, u, "tn")], BF16, "d_w_in", tm=1792, tn=256))
    du = _mm([(dpm, w_in_t, "nn"), (dgm, w["w_merge"], "nt")], F32, "d_u", tm=1024, tn=512)
    pending.append(_scatter_group(gw, EARLY_GROUPS[1], du, "grads_scatter_b_start"))
    token = pending[1][1][-1][0:1, 0:1]
    dh1, dh1_half, gs["ln_mix"] = _rms_bwd(du, h1, gain("ln_mix") + token, dh2, "rms_mix_bwd", 0.5)

    def scatter_now(names, after, tag):
        pending.append(_scatter_group(gw, names, after, f"grads_scatter_{tag}_start"))
        return pending[-1][1][-1]

    dn1 = ffn_bwd(dh1_half, g1, u1, a1, n1, "ffn1", scatter_now=scatter_now)
    dx, _, gs["ln_ffn1"] = _rms_bwd(dn1, x, gain("ln_ffn1"), dh1, "rms_ffn1_bwd", 1.0)
    return loss, dx, gw, gs, pending


WEIGHTS = ("ln_ffn1", "w_ffn1_gate", "w_ffn1_up", "w_ffn1_down", "ln_mix", "w_in", "b_forget", "w_merge", "b_merge",
           "w_ret_out", "w_fox_out", "w_out", "ln_ffn2", "w_ffn2_gate", "w_ffn2_up", "w_ffn2_down", "ln_ple", "w_ple",
           "w_ple_gate", "ln_final")


def kernel(x, p, positions, ln_ffn1, w_ffn1_gate, w_ffn1_up, w_ffn1_down, ln_mix, w_in, b_forget, w_merge, b_merge, w_ret_out, w_fox_out, w_out, ln_ffn2, w_ffn2_gate, w_ffn2_up, w_ffn2_down, ln_ple, w_ple, w_ple_gate, ln_final, loss_target, m_ln_ffn1, m_w_ffn1_gate, m_w_ffn1_up, m_w_ffn1_down, m_ln_mix, m_w_in, m_b_forget, m_w_merge, m_b_merge, m_w_ret_out, m_w_fox_out, m_w_out, m_ln_ffn2, m_w_ffn2_gate, m_w_ffn2_up, m_w_ffn2_down, m_ln_ple, m_w_ple, m_w_ple_gate, m_ln_final, v_ln_ffn1, v_w_ffn1_gate, v_w_ffn1_up, v_w_ffn1_down, v_ln_mix, v_w_in, v_b_forget, v_w_merge, v_b_merge, v_w_ret_out, v_w_fox_out, v_w_out, v_ln_ffn2, v_w_ffn2_gate, v_w_ffn2_up, v_w_ffn2_down, v_ln_ple, v_w_ple, v_w_ple_gate, v_ln_final):
    args = dict(ln_ffn1=ln_ffn1, w_ffn1_gate=w_ffn1_gate, w_ffn1_up=w_ffn1_up, w_ffn1_down=w_ffn1_down, ln_mix=ln_mix, w_in=w_in, b_forget=b_forget, w_merge=w_merge, b_merge=b_merge, w_ret_out=w_ret_out, w_fox_out=w_fox_out, w_out=w_out, ln_ffn2=ln_ffn2, w_ffn2_gate=w_ffn2_gate, w_ffn2_up=w_ffn2_up, w_ffn2_down=w_ffn2_down, ln_ple=ln_ple, w_ple=w_ple, w_ple_gate=w_ple_gate, ln_final=ln_final)
    moms = dict(ln_ffn1=m_ln_ffn1, w_ffn1_gate=m_w_ffn1_gate, w_ffn1_up=m_w_ffn1_up, w_ffn1_down=m_w_ffn1_down, ln_mix=m_ln_mix, w_in=m_w_in, b_forget=m_b_forget, w_merge=m_w_merge, b_merge=m_b_merge, w_ret_out=m_w_ret_out, w_fox_out=m_w_fox_out, w_out=m_w_out, ln_ffn2=m_ln_ffn2, w_ffn2_gate=m_w_ffn2_gate, w_ffn2_up=m_w_ffn2_up, w_ffn2_down=m_w_ffn2_down, ln_ple=m_ln_ple, w_ple=m_w_ple, w_ple_gate=m_w_ple_gate, ln_final=m_ln_final)
    vars_ = dict(ln_ffn1=v_ln_ffn1, w_ffn1_gate=v_w_ffn1_gate, w_ffn1_up=v_w_ffn1_up, w_ffn1_down=v_w_ffn1_down, ln_mix=v_ln_mix, w_in=v_w_in, b_forget=v_b_forget, w_merge=v_w_merge, b_merge=v_b_merge, w_ret_out=v_w_ret_out, w_fox_out=v_w_fox_out, w_out=v_w_out, ln_ffn2=v_ln_ffn2, w_ffn2_gate=v_w_ffn2_gate, w_ffn2_up=v_w_ffn2_up, w_ffn2_down=v_w_ffn2_down, ln_ple=v_ln_ple, w_ple=v_w_ple, w_ple_gate=v_w_ple_gate, ln_final=v_ln_final)
    kinds = ("grad", "delta", "new_m", "new_v")

    me = 4 * lax.axis_index("x") + 2 * lax.axis_index("y") + lax.axis_index("c")
    kind_of = dict(BIG)
    shard = {n: _shard_view(args[n], kind).astype(BF16) for n, kind in BIG}
    first = ("w_ffn1_gate", "w_ffn1_up")
    groups = (("w_ffn1_down",), ("w_in", "w_merge"), ("w_ret_out", "w_fox_out", "w_out"))
    groups += (tuple(n for n, _ in BIG if n not in first + groups[0] + groups[1] + groups[2]),)
    gathered = _all_gather([shard[n] for n in first])
    w_full = {n: _full_from_slots(g, kind_of[n]) for n, g in zip(first, gathered)}
    gathers, after = [], gathered[0]
    for tag, names in zip("zabc", groups):
        started = _scatter_start([shard[n] for n in names], after, f"weights_gather_{tag}_start", gather=True)
        gathers.append((names, started, f"weights_gather_{tag}_wait"))
        after = started[-1]

    small = {n: args[n] for n in SMALL}
    loss_part, dx, gw, gs, pending = _local_step(x[0], p[0, 0], positions[0], loss_target[0], w_full, small, me,
                                                 after[0:1, 0:1], gathers)

    parts_of = {}
    for tag, (names, started) in zip("abcde", pending):
        sent, landed = _scatter_wait(started, dx, f"grads_scatter_{tag}_wait")
        for n, blk, land in zip(names, sent, landed):
            own = lax.dynamic_index_in_dim(blk, me, 0, keepdims=True)
            parts_of[n] = lax.dynamic_update_slice_in_dim(land, own, me, 0)
    late = [(n, kind) for n, kind in BIG if n not in parts_of]
    small_part = _pack_small(gs, with_loss=loss_part)
    blocks = [_slots_from_full(gw[n], kind) for n, kind in late]
    blocks.append(jnp.broadcast_to(small_part, (N_DEV,) + small_part.shape))
    recv = _reduce_scatter_exchange(blocks)
    parts_of.update({n: r for (n, _), r in zip(late, recv)})

    res = {}
    for n, kind in BIG:
        parts = parts_of[n]
        outs = _adamw(parts, _shard_view(args[n], kind), _shard_view(moms[n], kind), _shard_view(vars_[n], kind),
                      f"adamw_{n}")
        for what, o in zip(kinds, outs):
            res[(what, n)] = _unview(o, kind, args[n].shape)
    s_outs = _adamw(recv[-1], _pack_small(small), _pack_small({n: moms[n] for n in SMALL}),
                    _pack_small({n: vars_[n] for n in SMALL}), "adamw_small")
    for what, sm in zip(kinds, s_outs):
        svals, extra = _unpack_small(sm, {n: args[n].shape for n in SMALL})
        if what == "grad":
            loss = extra
        for n in SMALL:
            res[(what, n)] = svals[n]
    return (loss, dx[None], *[res[(what, n)] for what in kinds for n in WEIGHTS])
```

```python
import numpy as np
import jax
import jax.numpy as jnp
from jax import lax
from jax.experimental import pallas as pl
from jax.experimental.pallas import tpu as pltpu

F32 = jnp.float32
BF16 = jnp.bfloat16

N_DEV = 8
EPS = 1e-6
RET_HEADS = 4
RET_DIM = 128
RET_WIDTH = RET_HEADS * RET_DIM
FOX_HEADS = 8
FOX_DIM = 64
FOX_WIDTH = FOX_HEADS * FOX_DIM
CHUNK = 128
ROPE_BASE = 10000.0
LANES = 128
FOX_TILES = FOX_HEADS * LANES
IN_COLS = 4 * RET_WIDTH + 3 * FOX_WIDTH + FOX_HEADS
IN_PAD = 4 * RET_WIDTH + 3 * FOX_TILES + 2 * LANES
TILE_RQ, TILE_RK, TILE_RV, TILE_RG = 0, 4, 8, 12
TILE_FQ, TILE_FK, TILE_FV, TILE_FF = 16, 24, 32, 40
NEG = -1e30

ADAM_LR = 0.001
ADAM_B1 = 0.9
ADAM_B2 = 0.999
ADAM_EPS = 1e-08
ADAM_WD = 0.01
ADAM_STEP = 10

VMEM_LIMIT_BYTES = 56 * 1024 * 1024

MESH = pl.DeviceIdType.MESH


def _tile(dim, pref, mult):
    if dim <= pref:
        return dim
    t = (pref // mult) * mult
    while t >= mult:
        if dim % t == 0:
            return t
        t -= mult
    return dim


def _params(dims):
    return pltpu.CompilerParams(dimension_semantics=dims, vmem_limit_bytes=VMEM_LIMIT_BYTES)


def _pcall(body, *, name, out_shape, grid, in_specs, out_specs, scratch_shapes=(), dims=None):
    return pl.pallas_call(body, name=name, out_shape=out_shape, grid=grid, in_specs=in_specs, out_specs=out_specs,
                          scratch_shapes=list(scratch_shapes), compiler_params=_params(dims))


def _dot(a, b, ca, cb):
    return lax.dot_general(a, b, (((ca,), (cb,)), ((), ())), preferred_element_type=F32)


def _sigmoid(x):
    return 1.0 / (1.0 + jnp.exp(-x))


def _mm(pairs, out_dtype, name, tm=1024, tn=1024, after=None):
    dims = []
    for a, b, mode in pairs:
        m, k = (a.shape[1], a.shape[0]) if mode == "tn" else a.shape
        n, k2 = b.shape if mode == "nt" else (b.shape[1], b.shape[0])
        assert k == k2, (name, a.shape, b.shape, mode)
        dims.append((m, n))
    assert all(d == dims[0] for d in dims), (name, dims)
    m, n = dims[0]
    tm = _tile(m, tm, 128 if any(mode == "tn" for _, _, mode in pairs) else 16)
    tn = _tile(n, tn, 128)
    in_specs, contract, operands = [], [], []
    for a, b, mode in pairs:
        k = a.shape[0] if mode == "tn" else a.shape[1]
        in_specs.append(pl.BlockSpec((k, tm), lambda i, j: (0, i)) if mode == "tn" else
                        pl.BlockSpec((tm, k), lambda i, j: (i, 0)))
        in_specs.append(pl.BlockSpec((tn, k), lambda i, j: (j, 0)) if mode == "nt" else
                        pl.BlockSpec((k, tn), lambda i, j: (0, j)))
        contract.append((0 if mode == "tn" else 1, 1 if mode == "nt" else 0))
        operands += [a, b]
    if after is not None:
        in_specs.append(pl.BlockSpec(memory_space=pl.ANY))
        operands.append(after)

    def body(*refs):
        o_ref = refs[-1]
        acc = None
        for p, (ca, cb) in enumerate(contract):
            part = _dot(refs[2 * p][...], refs[2 * p + 1][...], ca, cb)
            acc = part if acc is None else acc + part
        o_ref[...] = acc.astype(out_dtype)

    return _pcall(body, name=name, out_shape=jax.ShapeDtypeStruct((m, n), out_dtype), grid=(m // tm, n // tn),
                  in_specs=in_specs, out_specs=pl.BlockSpec((tm, tn), lambda i, j: (i, j)),
                  dims=("parallel", "parallel"))(*operands)


def _rms_fwd(h, gain, name, f=None, scale=0.5):
    t, d = h.shape
    tt = _tile(t, 512, 16)
    row = pl.BlockSpec((tt, d), lambda i: (i, 0))
    vec = pl.BlockSpec((1, d), lambda i: (0, 0))

    def norm(hv, g_ref, n_ref):
        r = lax.rsqrt(jnp.mean(hv * hv, axis=-1, keepdims=True) + EPS)
        n_ref[...] = (hv * r * g_ref[...]).astype(BF16)

    if f is None:

        def body(h_ref, g_ref, n_ref):
            norm(h_ref[...], g_ref, n_ref)

        return _pcall(body, name=name, out_shape=jax.ShapeDtypeStruct((t, d), BF16), grid=(t // tt,),
                      in_specs=[row, vec], out_specs=row, dims=("parallel",))(h, gain)

    def body(h_ref, f_ref, g_ref, hn_ref, n_ref):
        hv = h_ref[...] + scale * f_ref[...]
        hn_ref[...] = hv
        norm(hv, g_ref, n_ref)

    return _pcall(body, name=name,
                  out_shape=(jax.ShapeDtypeStruct((t, d), F32), jax.ShapeDtypeStruct((t, d), BF16)),
                  grid=(t // tt,), in_specs=[row, row, vec], out_specs=(row, row), dims=("parallel",))(h, f, gain)


def _rms_bwd(dn, h, gain, dh_in, name, out_scale):
    t, d = h.shape
    tt = _tile(t, 512, 16)
    row = pl.BlockSpec((tt, d), lambda i: (i, 0))
    vec = pl.BlockSpec((1, d), lambda i: (0, 0))

    def body(dn_ref, h_ref, g_ref, dhin_ref, dh_ref, dhb_ref, dg_ref):
        hv = h_ref[...]
        dnv = dn_ref[...].astype(F32)
        r = lax.rsqrt(jnp.mean(hv * hv, axis=-1, keepdims=True) + EPS)
        dng = dnv * g_ref[...]
        dh = dhin_ref[...] + r * dng - hv * (r * r * r) * jnp.mean(dng * hv, axis=-1, keepdims=True)
        dh_ref[...] = dh
        dhb_ref[...] = (out_scale * dh).astype(BF16)
        part = jnp.sum(dnv * hv * r, axis=0, keepdims=True)

        @pl.when(pl.program_id(0) == 0)
        def _():
            dg_ref[...] = part

        @pl.when(pl.program_id(0) > 0)
        def _():
            dg_ref[...] += part

    return _pcall(body, name=name,
                  out_shape=(jax.ShapeDtypeStruct((t, d), F32), jax.ShapeDtypeStruct((t, d), BF16),
                             jax.ShapeDtypeStruct((1, d), F32)),
                  grid=(t // tt,), in_specs=[row, row, vec, row], out_specs=(row, row, vec),
                  dims=("arbitrary",))(dn, h, gain, dh_in)


def _swiglu_fwd(g, u, name):
    t, f = g.shape
    tt = _tile(t, 256, 16)
    row = pl.BlockSpec((tt, f), lambda i: (i, 0))

    def body(g_ref, u_ref, a_ref):
        gv = g_ref[...].astype(F32)
        a_ref[...] = (gv * _sigmoid(gv) * u_ref[...].astype(F32)).astype(BF16)

    return _pcall(body, name=name, out_shape=jax.ShapeDtypeStruct((t, f), BF16), grid=(t // tt,),
                  in_specs=[row, row], out_specs=row, dims=("parallel",))(g, u)


def _swiglu_bwd(da, g, u, name):
    t, f = g.shape
    tt = _tile(t, 256, 16)
    row = pl.BlockSpec((tt, f), lambda i: (i, 0))

    def body(da_ref, g_ref, u_ref, dg_ref, du_ref):
        gv = g_ref[...].astype(F32)
        dav = da_ref[...].astype(F32)
        sg = _sigmoid(gv)
        dg_ref[...] = (dav * u_ref[...].astype(F32) * (sg * (1.0 + gv * (1.0 - sg)))).astype(BF16)
        du_ref[...] = (dav * (gv * sg)).astype(BF16)

    return _pcall(body, name=name, out_shape=(jax.ShapeDtypeStruct((t, f), BF16),) * 2, grid=(t // tt,),
                  in_specs=[row, row, row], out_specs=(row, row), dims=("parallel",))(da, g, u)


def _rope_tables(pos_col, inv2, sign2):
    t = pos_col.shape[0]

    def body(p_ref, inv_ref, sg_ref, c_ref, s_ref):
        ang = p_ref[...].astype(F32) * inv_ref[...]
        c_ref[...] = jnp.cos(ang)
        s_ref[...] = jnp.sin(ang) * sg_ref[...]

    full = lambda shape: pl.BlockSpec(shape, lambda i: (0, 0))
    return _pcall(body, name="rope_tables", out_shape=(jax.ShapeDtypeStruct((t, RET_DIM), F32),) * 2, grid=(1,),
                  in_specs=[full((t, 1)), full((1, RET_DIM)), full((1, RET_DIM))],
                  out_specs=(full((t, RET_DIM)),) * 2, dims=("arbitrary",))(pos_col, inv2, sign2)


def _rot(x, c, s):
    return x * c + pltpu.roll(x, RET_DIM // 2, 1) * s


def _rot_t(g, c, s):
    return g * c + pltpu.roll(g * s, RET_DIM // 2, 1)


def _ret_consts():
    hh = np.arange(RET_HEADS, dtype=np.float32)
    log_gamma = np.log1p(-np.exp2(-5.0 - hh)).astype(np.float32)
    idx = np.arange(CHUNK, dtype=np.float32)
    diff = idx[:, None] - idx[None, :]
    dmask = np.where(diff >= 0, np.exp(log_gamma[:, None, None] * np.maximum(diff, 0.0)), 0.0).astype(np.float32)
    kdec = np.exp(log_gamma[:, None] * (CHUNK - 1 - idx)).astype(np.float32)
    qdec = np.exp(log_gamma[:, None] * (idx + 1.0)).astype(np.float32)
    cdec = np.exp(log_gamma * CHUNK).astype(np.float32)
    bc = lambda v: np.ascontiguousarray(np.broadcast_to(v[:, :, None], (RET_HEADS, CHUNK, RET_DIM)))
    cd = np.ascontiguousarray(np.broadcast_to(cdec[:, None, None], (RET_HEADS, 8, RET_DIM)))
    return jnp.asarray(dmask), jnp.asarray(bc(qdec)), jnp.asarray(bc(kdec)), jnp.asarray(cd)


def _ret_fwd(pm, cos2, sin2, consts):
    t = pm.shape[0]
    n_chunks = t // CHUNK
    dmask, qdec, kdec, cd = consts
    scale = RET_DIM ** -0.5
    wide = lambda c0: pl.BlockSpec((CHUNK, RET_WIDTH), lambda n: (n, c0 // RET_HEADS))
    tab = pl.BlockSpec((CHUNK, RET_DIM), lambda n: (n, 0))
    const = lambda a: pl.BlockSpec(a.shape, lambda n: (0,) * a.ndim)

    def body(q_ref, k_ref, v_ref, g_ref, c_ref, s_ref, dm_ref, qd_ref, kd_ref, cd_ref, y_ref, raw_ref, st_ref, s_acc):
        @pl.when(pl.program_id(0) == 0)
        def _():
            s_acc[...] = jnp.zeros_like(s_acc)

        c, s = c_ref[...], s_ref[...]
        for h in range(RET_HEADS):
            hs = slice(h * RET_DIM, (h + 1) * RET_DIM)
            q = _rot(q_ref[:, hs], c, s)
            k = _rot(k_ref[:, hs], c, s) * scale
            vb = v_ref[:, hs].astype(BF16)
            g = g_ref[:, hs]
            s_in = s_acc[h]
            st_ref[h] = s_in
            a = _dot(q.astype(BF16), k.astype(BF16), 1, 1) * dm_ref[h]
            y = _dot(a.astype(BF16), vb, 1, 0) + _dot((q * qd_ref[h]).astype(BF16), s_in.astype(BF16), 1, 0)
            s_acc[h] = cd_ref[h, 0:1, :] * s_in + _dot((k * kd_ref[h]).astype(BF16), vb, 0, 0)
            raw_ref[:, hs] = y
            mu = jnp.mean(y, axis=-1, keepdims=True)
            yc = y - mu
            rs = lax.rsqrt(jnp.mean(yc * yc, axis=-1, keepdims=True) + EPS)
            y_ref[:, hs] = (yc * rs * (g * _sigmoid(g))).astype(BF16)

    out_blk = pl.BlockSpec((CHUNK, RET_WIDTH), lambda n: (n, 0))
    return _pcall(
        body, name="retention_fwd",
        out_shape=(jax.ShapeDtypeStruct((t, RET_WIDTH), BF16), jax.ShapeDtypeStruct((t, RET_WIDTH), F32),
                   jax.ShapeDtypeStruct((RET_HEADS, n_chunks, RET_DIM, RET_DIM), F32)),
        grid=(n_chunks,),
        in_specs=[wide(TILE_RQ), wide(TILE_RK), wide(TILE_RV), wide(TILE_RG), tab, tab,
                  const(dmask), const(qdec), const(kdec), const(cd)],
        out_specs=(out_blk, out_blk, pl.BlockSpec((RET_HEADS, None, RET_DIM, RET_DIM), lambda n: (0, n, 0, 0))),
        scratch_shapes=[pltpu.VMEM((RET_HEADS, RET_DIM, RET_DIM), F32)],
        dims=("arbitrary",),
    )(pm, pm, pm, pm, cos2, sin2, dmask, qdec, kdec, cd)


def _ret_bwd(dy, pm, cos2, sin2, raw, states, consts):
    t = pm.shape[0]
    n_chunks = t // CHUNK
    dmask, qdec, kdec, cd = consts
    scale = RET_DIM ** -0.5
    rev = lambda n: n_chunks - 1 - n
    wide = lambda c0: pl.BlockSpec((CHUNK, RET_WIDTH), lambda n: (rev(n), c0 // RET_HEADS))
    tab = pl.BlockSpec((CHUNK, RET_DIM), lambda n: (rev(n), 0))
    blk = pl.BlockSpec((CHUNK, RET_WIDTH), lambda n: (rev(n), 0))
    const = lambda a: pl.BlockSpec(a.shape, lambda n: (0,) * a.ndim)

    def body(dy_ref, q_ref, k_ref, v_ref, g_ref, c_ref, s_ref, raw_ref, st_ref, dm_ref, qd_ref, kd_ref, cd_ref,
             dq_ref, dk_ref, dv_ref, dg_ref, ds_acc):
        @pl.when(pl.program_id(0) == 0)
        def _():
            ds_acc[...] = jnp.zeros_like(ds_acc)

        c, s = c_ref[...], s_ref[...]
        for h in range(RET_HEADS):
            hs = slice(h * RET_DIM, (h + 1) * RET_DIM)
            q = _rot(q_ref[:, hs], c, s)
            k = _rot(k_ref[:, hs], c, s) * scale
            qb, kb, vb = q.astype(BF16), k.astype(BF16), v_ref[:, hs].astype(BF16)
            g = g_ref[:, hs]
            dm, qd, kd = dm_ref[h], qd_ref[h], kd_ref[h]
            y = raw_ref[:, hs]
            mu = jnp.mean(y, axis=-1, keepdims=True)
            yc = y - mu
            rs = lax.rsqrt(jnp.mean(yc * yc, axis=-1, keepdims=True) + EPS)
            yn = yc * rs
            sg = _sigmoid(g)
            dyo = dy_ref[:, hs]
            dg_ref[:, hs] = (dyo * yn * (sg * (1.0 + g * (1.0 - sg)))).astype(BF16)
            dyn = dyo * (g * sg)
            dyr = rs * (dyn - jnp.mean(dyn, axis=-1, keepdims=True) - yn * jnp.mean(dyn * yn, axis=-1, keepdims=True))
            dyb = dyr.astype(BF16)
            s_in = st_ref[h].astype(BF16)
            ds_out = ds_acc[h]
            dsb = ds_out.astype(BF16)
            a = _dot(qb, kb, 1, 1) * dm
            da = (_dot(dyb, vb, 1, 1) * dm).astype(BF16)
            kdb = (k * kd).astype(BF16)
            qdb = (q * qd).astype(BF16)
            dv_ref[:, hs] = (_dot(a.astype(BF16), dyb, 0, 0) + _dot(kdb, dsb, 1, 0)).astype(BF16)
            dqh = _dot(da, kb, 1, 0) + _dot(dyb, s_in, 1, 1) * qd
            dkh = _dot(da, qb, 0, 0) + _dot(vb, dsb, 1, 1) * kd
            ds_acc[h] = cd_ref[h, 0:1, :] * ds_out + _dot(qdb, dyb, 0, 0)
            dq_ref[:, hs] = _rot_t(dqh, c, s).astype(BF16)
            dk_ref[:, hs] = (_rot_t(dkh, c, s) * scale).astype(BF16)

    return _pcall(
        body, name="retention_bwd",
        out_shape=(jax.ShapeDtypeStruct((t, RET_WIDTH), BF16),) * 4,
        grid=(n_chunks,),
        in_specs=[blk, wide(TILE_RQ), wide(TILE_RK), wide(TILE_RV), wide(TILE_RG), tab, tab, blk,
                  pl.BlockSpec((RET_HEADS, None, RET_DIM, RET_DIM), lambda n: (0, rev(n), 0, 0)),
                  const(dmask), const(qdec), const(kdec), const(cd)],
        out_specs=(blk,) * 4,
        scratch_shapes=[pltpu.VMEM((RET_HEADS, RET_DIM, RET_DIM), F32)],
        dims=("arbitrary",),
    )(dy, pm, pm, pm, pm, cos2, sin2, raw, states, dmask, qdec, kdec, cd)


FOX_C_LANE = FOX_DIM
FOX_NEGC_LANE = FOX_DIM + 3
FOX_LSE_LANE = FOX_DIM + 6
FOX_L_LANE = FOX_C_LANE
FOX_ROWSUM_LANE = FOX_C_LANE
FOX_COLSUM_LANE = FOX_NEGC_LANE


def _split3(x):
    hi = x.astype(BF16)
    r1 = x - hi.astype(F32)
    mid = r1.astype(BF16)
    lo = (r1 - mid.astype(F32)).astype(BF16)
    return hi, mid, lo


def _tri_dot(tri, x):
    hi, mid, lo = _split3(x)
    return _dot(tri, lo, 1, 0) + _dot(tri, mid, 1, 0) + _dot(tri, hi, 1, 0)


def _log_sigmoid(z):
    return jnp.minimum(z, 0.0) - jnp.log1p(jnp.exp(-jnp.abs(z)))


def _fox_consts():
    place = np.zeros((2, 3, LANES, FOX_TILES), np.float32)
    ones = np.zeros((3, 1, FOX_TILES), np.float32)
    for h in range(FOX_HEADS):
        for part in range(3):
            place[0, part, h, LANES * h + FOX_C_LANE + part] = 1.0
            place[1, part, h, LANES * h + FOX_NEGC_LANE + part] = -1.0
            ones[0, 0, LANES * h + FOX_NEGC_LANE + part] = 1.0
            ones[1, 0, LANES * h + FOX_C_LANE + part] = 1.0
            ones[1, 0, LANES * h + FOX_LSE_LANE + part] = 1.0
            ones[2, 0, LANES * h + FOX_C_LANE + part] = 1.0
    return jnp.asarray(place, BF16), jnp.asarray(ones, F32)


def _fox_prep(pm, bpad):
    t = pm.shape[0]
    tt = _tile(t, 512, LANES)
    place, ones = _fox_consts()
    wide = lambda c0: pl.BlockSpec((tt, FOX_TILES), lambda i: (i, c0 // FOX_HEADS))
    const = lambda a: pl.BlockSpec(a.shape, lambda i: (0,) * a.ndim)

    def body(q_ref, k_ref, v_ref, ff_ref, b_ref, pl_ref, on_ref, qa_ref, ka_ref, va_ref, carry_s):
        @pl.when(pl.program_id(0) == 0)
        def _():
            carry_s[...] = jnp.zeros_like(carry_s)

        r = lax.broadcasted_iota(jnp.int32, (LANES, LANES), 0)
        cc = lax.broadcasted_iota(jnp.int32, (LANES, LANES), 1)
        tri = jnp.where(cc <= r, 1.0, 0.0).astype(BF16)
        bias = b_ref[...]
        for sub in range(tt // LANES):
            rows = pl.ds(sub * LANES, LANES)
            cs = _tri_dot(tri, _log_sigmoid(ff_ref[rows, :] + bias)) + carry_s[...]
            carry_s[...] = cs[LANES - 1:LANES, :]
            parts = _split3(cs)
            eq = sum(_dot(part, pl_ref[0, i], 1, 0) for i, part in enumerate(parts))
            ek = sum(_dot(part, pl_ref[1, i], 1, 0) for i, part in enumerate(parts))
            qa_ref[rows, :] = (q_ref[rows, :] * FOX_DIM ** -0.5 + eq + on_ref[0]).astype(BF16)
            ka_ref[rows, :] = (k_ref[rows, :] + ek + on_ref[1]).astype(BF16)
            va_ref[rows, :] = (v_ref[rows, :] + on_ref[2]).astype(BF16)

    out = pl.BlockSpec((tt, FOX_TILES), lambda i: (i, 0))
    return _pcall(body, name="fox_prep", out_shape=(jax.ShapeDtypeStruct((t, FOX_TILES), BF16),) * 3, grid=(t // tt,),
                  in_specs=[wide(TILE_FQ), wide(TILE_FK), wide(TILE_FV), pl.BlockSpec((tt, LANES), lambda i: (i, TILE_FF)),
                            pl.BlockSpec((1, LANES), lambda i: (0, 0)), const(place), const(ones)],
                  out_specs=(out,) * 3, scratch_shapes=[pltpu.VMEM((1, LANES), F32)],
                  dims=("arbitrary",))(pm, pm, pm, pm, bpad, place, ones)


def _fox_post(dc, pm, bpad):
    t = pm.shape[0]
    nb = t // LANES

    def body(dc_ref, ff_ref, b_ref, d_ref, db_ref):
        r = lax.broadcasted_iota(jnp.int32, (LANES, LANES), 0)
        cc = lax.broadcasted_iota(jnp.int32, (LANES, LANES), 1)
        tri = jnp.where(cc >= r, 1.0, 0.0).astype(BF16)
        bias = b_ref[...]

        def step(i, carry):
            tail, acc = carry
            rows = pl.ds(pl.multiple_of((nb - 1 - i) * LANES, LANES), LANES)
            cs = _tri_dot(tri, dc_ref[rows, :]) + tail
            dff = cs * _sigmoid(-(ff_ref[rows, :] + bias))
            d_ref[rows, :] = dff.astype(BF16)
            return cs[0:1, :], acc + jnp.sum(dff, axis=0, keepdims=True)

        zero = jnp.zeros((1, LANES), F32)
        _, acc = lax.fori_loop(0, nb, step, (zero, zero))
        db_ref[...] = acc

    return _pcall(body, name="fox_forget_bwd",
                  out_shape=(jax.ShapeDtypeStruct((t, LANES), BF16), jax.ShapeDtypeStruct((1, LANES), F32)), grid=(1,),
                  in_specs=[pl.BlockSpec((t, LANES), lambda i: (0, 0)), pl.BlockSpec((t, LANES), lambda i: (0, TILE_FF)),
                            pl.BlockSpec((1, LANES), lambda i: (0, 0))],
                  out_specs=(pl.BlockSpec((t, LANES), lambda i: (0, 0)), pl.BlockSpec((1, LANES), lambda i: (0, 0))),
                  dims=("arbitrary",))(dc, pm, bpad)


def _tri_tables(nb, q_major):
    pairs = [(i, j) for i in range(nb) for j in range(i + 1)] if q_major else \
            [(i, j) for j in range(nb) for i in range(j, nb)]
    return jnp.asarray([a for a, _ in pairs], jnp.int32), jnp.asarray([b for _, b in pairs], jnp.int32)


def _causal(s):
    n = s.shape[0]
    row = lax.broadcasted_iota(jnp.int32, (n, n), 0)
    col = lax.broadcasted_iota(jnp.int32, (n, n), 1)
    return jnp.where(col <= row, s, NEG)


def _lane_col(x, lane):
    sel = lax.broadcasted_iota(jnp.int32, x.shape, 1) == lane
    return jnp.sum(jnp.where(sel, x, 0.0), axis=1, keepdims=True)


def _fox_fwd(qa, ka, va, blk):
    t = qa.shape[0]
    nb = t // blk
    qi, kj = _tri_tables(nb, True)
    q_spec = pl.BlockSpec((blk, LANES), lambda h, s, qi_r, kj_r: (qi_r[s], h))
    k_spec = pl.BlockSpec((blk, LANES), lambda h, s, qi_r, kj_r: (kj_r[s], h))

    def body(qi_r, kj_r, q_ref, k_ref, v_ref, o_ref, ob_ref, qb_ref, m_s, acc_s):
        s_id = pl.program_id(1)
        i, j = qi_r[s_id], kj_r[s_id]

        @pl.when(j == 0)
        def _():
            m_s[...] = jnp.full_like(m_s, NEG)
            acc_s[...] = jnp.zeros_like(acc_s)

        def tile(diagonal):
            s = _dot(q_ref[...], k_ref[...], 1, 1)
            if diagonal:
                s = _causal(s)
            m_old = m_s[...]
            m_new = jnp.maximum(m_old, jnp.max(s, axis=1, keepdims=True))
            p = jnp.exp(s - jnp.tile(m_new, (1, blk // LANES)))
            acc_s[...] = jnp.exp(m_old - m_new) * acc_s[...] + _dot(p.astype(BF16), v_ref[...], 1, 0)
            m_s[...] = m_new

        @pl.when(j < i)
        def _():
            tile(False)

        @pl.when(j == i)
        def _():
            tile(True)
            acc = acc_s[...]
            l = _lane_col(acc, FOX_L_LANE)
            o = acc / l
            o_ref[...] = o
            ob_ref[...] = o.astype(BF16)
            hi, mid, lo = _split3(-(m_s[:, 0:1] + jnp.log(l)))
            lane = lax.broadcasted_iota(jnp.int32, acc.shape, 1)
            qb_ref[...] = jnp.where(lane == FOX_LSE_LANE, hi,
                                    jnp.where(lane == FOX_LSE_LANE + 1, mid,
                                              jnp.where(lane == FOX_LSE_LANE + 2, lo, q_ref[...])))

    wide = (t, FOX_TILES)
    return pl.pallas_call(
        body, name="fox_fwd",
        out_shape=(jax.ShapeDtypeStruct(wide, F32), jax.ShapeDtypeStruct(wide, BF16), jax.ShapeDtypeStruct(wide, BF16)),
        grid_spec=pltpu.PrefetchScalarGridSpec(
            num_scalar_prefetch=2, grid=(FOX_HEADS, qi.shape[0]), in_specs=[q_spec, k_spec, k_spec],
            out_specs=(q_spec,) * 3,
            scratch_shapes=[pltpu.VMEM((blk, LANES), F32), pltpu.VMEM((blk, LANES), F32)]),
        compiler_params=_params(("parallel", "arbitrary")),
    )(qi, kj, qa, ka, va)


def _fox_bwd(qa, ka, va, do, o, blk):
    t = qa.shape[0]
    nb = t // blk
    qi, kj = _tri_tables(nb, False)
    q_spec = pl.BlockSpec((blk, LANES), lambda h, s, qi_r, kj_r: (qi_r[s], h))
    k_spec = pl.BlockSpec((blk, LANES), lambda h, s, qi_r, kj_r: (kj_r[s], h))
    head_spec = pl.BlockSpec((t, LANES), lambda h, s, qi_r, kj_r: (0, h))
    head_col = pl.BlockSpec((None, t, 1), lambda h, s, qi_r, kj_r: (h, 0, 0))
    k_col = pl.BlockSpec((None, blk, 1), lambda h, s, qi_r, kj_r: (h, kj_r[s], 0))
    first_spec = pl.BlockSpec((blk, LANES), lambda h, s, qi_r, kj_r: (jnp.where(kj_r[s] == 0, qi_r[s], nb - 1), h))
    n_steps = int(qi.shape[0])

    def body(qi_r, kj_r, q_ref, k_ref, v_ref, do_ref, o_ref, dq_ref, dk_ref, dv_ref, rs_ref, cs_ref,
             doa_s, dq_s, dk_s, dv_s):
        s_id = pl.program_id(1)
        i, j = qi_r[s_id], kj_r[s_id]
        rows = pl.ds(pl.multiple_of(i * blk, blk), blk)

        @pl.when(j == 0)
        def _():
            dof = do_ref[...]
            hi, mid, lo = _split3(-jnp.sum(dof * o_ref[...], axis=1, keepdims=True))
            lane = lax.broadcasted_iota(jnp.int32, dof.shape, 1)
            doa = jnp.where(lane == FOX_C_LANE, hi.astype(F32),
                            jnp.where(lane == FOX_C_LANE + 1, mid.astype(F32),
                                      jnp.where(lane == FOX_C_LANE + 2, lo.astype(F32), dof)))
            doa_s[rows, :] = doa.astype(BF16)
            dq_s[rows, :] = jnp.zeros((blk, LANES), F32)

        @pl.when(i == j)
        def _():
            dk_s[...] = jnp.zeros_like(dk_s)
            dv_s[...] = jnp.zeros_like(dv_s)

        def tile(diagonal):
            q, k = q_ref[...], k_ref[...]
            s = _dot(q, k, 1, 1)
            if diagonal:
                s = _causal(s)
            p = jnp.exp(s)
            doa = doa_s[rows, :]
            ds = (p * _dot(doa, v_ref[...], 1, 1)).astype(BF16)
            dv_s[...] += _dot(p.astype(BF16), doa, 0, 0)
            dk_s[...] += _dot(ds, q, 0, 0)
            dq_s[rows, :] += _dot(ds, k, 1, 0)

        @pl.when(i > j)
        def _():
            tile(False)

        @pl.when(i == j)
        def _():
            tile(True)

        @pl.when(i == nb - 1)
        def _():
            dk = dk_s[...]
            dk_ref[...] = dk.astype(BF16)
            dv_ref[...] = dv_s[...].astype(BF16)
            cs_ref[...] = -_lane_col(dk, FOX_COLSUM_LANE)

        @pl.when(s_id == n_steps - 1)
        def _():
            dq = dq_s[...]
            dq_ref[...] = (dq * FOX_DIM ** -0.5).astype(BF16)
            rs_ref[...] = _lane_col(dq, FOX_ROWSUM_LANE)

    wide = jax.ShapeDtypeStruct((t, FOX_TILES), BF16)
    cols = jax.ShapeDtypeStruct((FOX_HEADS, t, 1), F32)
    return pl.pallas_call(
        body, name="fox_bwd", out_shape=(wide, wide, wide, cols, cols),
        grid_spec=pltpu.PrefetchScalarGridSpec(
            num_scalar_prefetch=2, grid=(FOX_HEADS, n_steps),
            in_specs=[q_spec, k_spec, k_spec, first_spec, first_spec],
            out_specs=(head_spec, k_spec, k_spec, head_col, k_col),
            scratch_shapes=[pltpu.VMEM((t, LANES), BF16), pltpu.VMEM((t, LANES), F32), pltpu.VMEM((blk, LANES), F32),
                            pltpu.VMEM((blk, LANES), F32)]),
        compiler_params=_params(("parallel", "arbitrary")),
    )(qi, kj, qa, ka, va, do, o)


def _merge_fwd(gm, bm, za, zb):
    t, d = za.shape
    tt = _tile(t, 256, 16)
    row = pl.BlockSpec((tt, d), lambda i: (i, 0))

    def body(gm_ref, b_ref, za_ref, zb_ref, o_ref):
        ga = _sigmoid(gm_ref[:, :d] + b_ref[:, :d])
        gb = _sigmoid(gm_ref[:, d:] + b_ref[:, d:])
        o_ref[...] = (ga * za_ref[...] + gb * zb_ref[...]).astype(BF16)

    return _pcall(body, name="merge_fwd", out_shape=jax.ShapeDtypeStruct((t, d), BF16), grid=(t // tt,),
                  in_specs=[pl.BlockSpec((tt, 2 * d), lambda i: (i, 0)), pl.BlockSpec((1, 2 * d), lambda i: (0, 0)), row, row],
                  out_specs=row, dims=("parallel",))(gm, bm, za, zb)


def _merge_bwd(dmix, gm, bm, za, zb):
    t, d = za.shape
    tt = _tile(t, 256, 16)
    row = pl.BlockSpec((tt, d), lambda i: (i, 0))
    wide = pl.BlockSpec((tt, 2 * d), lambda i: (i, 0))
    vec = pl.BlockSpec((1, 2 * d), lambda i: (0, 0))

    def body(dm_ref, gm_ref, b_ref, za_ref, zb_ref, dza_ref, dzb_ref, dgm_ref, db_ref):
        dm = dm_ref[...]
        ga = _sigmoid(gm_ref[:, :d] + b_ref[:, :d])
        gb = _sigmoid(gm_ref[:, d:] + b_ref[:, d:])
        dza_ref[...] = (dm * ga).astype(BF16)
        dzb_ref[...] = (dm * gb).astype(BF16)
        dla = dm * za_ref[...] * ga * (1.0 - ga)
        dlb = dm * zb_ref[...] * gb * (1.0 - gb)
        dgm_ref[:, :d] = dla.astype(BF16)
        dgm_ref[:, d:] = dlb.astype(BF16)
        pa = jnp.sum(dla, axis=0, keepdims=True)
        pb = jnp.sum(dlb, axis=0, keepdims=True)

        @pl.when(pl.program_id(0) == 0)
        def _():
            db_ref[:, :d] = pa
            db_ref[:, d:] = pb

        @pl.when(pl.program_id(0) > 0)
        def _():
            db_ref[:, :d] += pa
            db_ref[:, d:] += pb

    return _pcall(body, name="merge_bwd",
                  out_shape=(jax.ShapeDtypeStruct((t, d), BF16), jax.ShapeDtypeStruct((t, d), BF16),
                             jax.ShapeDtypeStruct((t, 2 * d), BF16), jax.ShapeDtypeStruct((1, 2 * d), F32)),
                  grid=(t // tt,), in_specs=[row, wide, vec, row, row], out_specs=(row, row, wide, vec),
                  dims=("arbitrary",))(dmix, gm, bm, za, zb)


def _ple_final(h3, pgl, pe, gain, target):
    t, d = h3.shape
    tt = _tile(t, 256, 16)
    row = pl.BlockSpec((tt, d), lambda i: (i, 0))
    vec = pl.BlockSpec((1, d), lambda i: (0, 0))
    lvec = pl.BlockSpec((1, LANES), lambda i: (0, 0))

    def body(h_ref, pgl_ref, pe_ref, g_ref, t_ref, dh_ref, dsg_ref, dpe_ref, loss_ref, dg_ref):
        pg = _sigmoid(pgl_ref[...])
        pe_v = pe_ref[...]
        h4 = h_ref[...] + pg * pe_v
        r = lax.rsqrt(jnp.mean(h4 * h4, axis=-1, keepdims=True) + EPS)
        gv = g_ref[...]
        err = h4 * r * gv - t_ref[...]
        part_loss = 0.5 * jnp.sum(jnp.mean(err * err, axis=-1, keepdims=True), axis=0, keepdims=True)
        dy = err * (1.0 / d)
        part_g = jnp.sum(dy * h4 * r, axis=0, keepdims=True)
        dyg = dy * gv
        dh = r * dyg - h4 * (r * r * r) * jnp.mean(dyg * h4, axis=-1, keepdims=True)
        dh_ref[...] = dh
        dsg_ref[...] = (dh * pe_v * pg * (1.0 - pg)).astype(BF16)
        dpe_ref[...] = (dh * pg).astype(BF16)

        @pl.when(pl.program_id(0) == 0)
        def _():
            loss_ref[...] = jnp.broadcast_to(part_loss, (1, LANES))
            dg_ref[...] = part_g

        @pl.when(pl.program_id(0) > 0)
        def _():
            loss_ref[...] += jnp.broadcast_to(part_loss, (1, LANES))
            dg_ref[...] += part_g

    return _pcall(body, name="ple_final",
                  out_shape=(jax.ShapeDtypeStruct((t, d), F32), jax.ShapeDtypeStruct((t, d), BF16),
                             jax.ShapeDtypeStruct((t, d), BF16), jax.ShapeDtypeStruct((1, LANES), F32),
                             jax.ShapeDtypeStruct((1, d), F32)),
                  grid=(t // tt,), in_specs=[row, row, row, vec, row], out_specs=(row, row, row, lvec, vec),
                  dims=("arbitrary",))(h3, pgl, pe, gain, target)


def _adamw_math(w, g, m, v):
    m = ADAM_B1 * m + (1.0 - ADAM_B1) * g
    v = ADAM_B2 * v + (1.0 - ADAM_B2) * (g * g)
    m_hat = m / (1.0 - ADAM_B1 ** ADAM_STEP)
    v_hat = v / (1.0 - ADAM_B2 ** ADAM_STEP)
    delta = -ADAM_LR * (m_hat / (jnp.sqrt(v_hat) + ADAM_EPS) + ADAM_WD * w)
    return delta, m, v


def _adamw(parts, w, m, v, name):
    n, r, c = parts.shape
    tr = _tile(r, 256, 16)
    row = pl.BlockSpec((tr, c), lambda i: (i, 0))

    def body(p_ref, w_ref, m_ref, v_ref, g_ref, d_ref, mo_ref, vo_ref):
        g = p_ref[0].astype(F32)
        for s in range(1, n):
            g = g + p_ref[s].astype(F32)
        g_ref[...] = g
        d_ref[...], mo_ref[...], vo_ref[...] = _adamw_math(w_ref[...], g, m_ref[...], v_ref[...])

    return _pcall(body, name=name, out_shape=(jax.ShapeDtypeStruct((r, c), F32),) * 4, grid=(r // tr,),
                  in_specs=[pl.BlockSpec((n, tr, c), lambda i: (0, i, 0)), row, row, row], out_specs=(row,) * 4,
                  dims=("parallel",))(parts, w, m, v)


ANY = pl.BlockSpec(memory_space=pl.ANY)


def _all_gather(shards):
    n = len(shards)

    def body(*refs):
        x_refs, out_refs = refs[:n], refs[n:2 * n]
        send_sems, recv_sems, local_sems = refs[2 * n:]
        x, y, cc = lax.axis_index("x"), lax.axis_index("y"), lax.axis_index("c")
        me, sibling = (x, y, cc), (x, y, 1 - cc)
        chips = [(1 - x, y), (x, 1 - y), (1 - x, 1 - y)]

        def slot(a, px, py, pc):
            return out_refs[a].at[4 * px + 2 * py + pc]

        def copy(a, k, block, to, src=None):
            return pltpu.make_async_remote_copy(
                src_ref=slot(a, *block) if src is None else src, dst_ref=slot(a, *block),
                send_sem=send_sems.at[7 * a + k], recv_sem=recv_sems.at[7 * a + k], device_id=to, device_id_type=MESH)

        local, sent = [], []
        for a in range(n):
            local.append(pltpu.make_async_copy(x_refs[a], slot(a, *me), local_sems.at[a]))
            sent.append(copy(a, 0, me, sibling, src=x_refs[a]))
            sent += [copy(a, 1 + j, me, (*chip, cc), src=x_refs[a]) for j, chip in enumerate(chips)]
        for cp in local + sent:
            cp.start()
        for j, chip in enumerate(chips):
            for a in range(n):
                copy(a, 1 + j, (*chip, cc), me).wait_recv()
                sent.append(copy(a, 4 + j, (*chip, cc), sibling))
                sent[-1].start()
        for a in range(n):
            copy(a, 0, sibling, me).wait_recv()
            for j, chip in enumerate(chips):
                copy(a, 4 + j, (*chip, 1 - cc), me).wait_recv()
        for cp in sent:
            cp.wait_send()
        for cp in local:
            cp.wait()

    return pl.pallas_call(
        body, name="weights_all_gather",
        out_shape=tuple(jax.ShapeDtypeStruct((N_DEV,) + s.shape, s.dtype) for s in shards),
        in_specs=[ANY] * n, out_specs=(ANY,) * n,
        scratch_shapes=[pltpu.SemaphoreType.DMA((7 * n,)), pltpu.SemaphoreType.DMA((7 * n,)),
                        pltpu.SemaphoreType.DMA((n,))],
    )(*shards)


def _reduce_scatter_exchange(blocks):
    n = len(blocks)

    def body(*refs):
        g_refs, recv_refs = refs[:n], refs[n:2 * n]
        send_sems, recv_sems, local_sems = refs[2 * n:]
        x, y, cc = lax.axis_index("x"), lax.axis_index("y"), lax.axis_index("c")
        me = 4 * x + 2 * y + cc
        local, sent, landing = [], [], []
        for a in range(n):
            local.append(pltpu.make_async_copy(g_refs[a].at[me], recv_refs[a].at[me], local_sems.at[a]))
        for k in range(1, N_DEV):
            px, py, pc = x ^ (k >> 2), y ^ ((k >> 1) & 1), cc ^ (k & 1)
            peer = 4 * px + 2 * py + pc
            for a in range(n):
                sems = dict(send_sem=send_sems.at[7 * a + k - 1], recv_sem=recv_sems.at[7 * a + k - 1],
                            device_id=(px, py, pc), device_id_type=MESH)
                sent.append(pltpu.make_async_remote_copy(src_ref=g_refs[a].at[peer], dst_ref=recv_refs[a].at[me], **sems))
                landing.append(pltpu.make_async_remote_copy(src_ref=g_refs[a].at[me], dst_ref=recv_refs[a].at[peer], **sems))
        for cp in local + sent:
            cp.start()
        for cp in landing:
            cp.wait_recv()
        for cp in sent:
            cp.wait_send()
        for cp in local:
            cp.wait()

    return pl.pallas_call(
        body, name="grads_reduce_scatter_exchange",
        out_shape=tuple(jax.ShapeDtypeStruct(b.shape, b.dtype) for b in blocks),
        in_specs=[ANY] * n, out_specs=(ANY,) * n,
        scratch_shapes=[pltpu.SemaphoreType.DMA((7 * n,)), pltpu.SemaphoreType.DMA((7 * n,)),
                        pltpu.SemaphoreType.DMA((n,))],
    )(*blocks)


HBM = pl.BlockSpec(memory_space=pltpu.HBM)
SEM = pl.BlockSpec(memory_space=pltpu.SEMAPHORE)
DATAFLOW = pltpu.SideEffectType.DATAFLOW_SIDE_EFFECTING


def _peers():
    x, y, cc = lax.axis_index("x"), lax.axis_index("y"), lax.axis_index("c")
    out = []
    for k in range(1, N_DEV):
        px, py, pc = x ^ (k >> 2), y ^ ((k >> 1) & 1), cc ^ (k & 1)
        out.append((k, (px, py, pc), 4 * px + 2 * py + pc))
    return 4 * x + 2 * y + cc, out


def _scatter_start(blocks, after, name, gather=False):
    n = len(blocks)
    lands = [lax.empty((N_DEV,) + b.shape if gather else b.shape, b.dtype) for b in blocks]

    def body(*refs):
        g_refs, land_refs = refs[:n], refs[n:2 * n]
        send_sems, recv_sems, token = refs[2 * n + 1], refs[2 * n + 2], refs[-1]
        me, peers = _peers()
        for k, peer, slot in peers:
            for a in range(n):
                pltpu.make_async_remote_copy(
                    src_ref=g_refs[a] if gather else g_refs[a].at[slot], dst_ref=land_refs[a].at[me],
                    send_sem=send_sems.at[7 * a + k - 1],
                    recv_sem=recv_sems.at[7 * a + k - 1], device_id=peer, device_id_type=MESH).start()
        token[...] = jnp.zeros_like(token)

    thru = [pltpu.HBM(b.shape, b.dtype) for b in blocks]
    thru_lands = [pltpu.HBM(b.shape, b.dtype) for b in lands]
    return pl.pallas_call(
        body, name=name,
        out_shape=(pltpu.SemaphoreType.DMA((7 * n,)), pltpu.SemaphoreType.DMA((7 * n,)), *thru, *thru_lands,
                   jax.ShapeDtypeStruct((8, LANES), F32)),
        in_specs=[HBM] * (2 * n) + [pl.BlockSpec(memory_space=pl.ANY)],
        out_specs=(SEM, SEM, *[HBM] * (2 * n), pl.BlockSpec(memory_space=pltpu.VMEM)),
        input_output_aliases={i: 2 + i for i in range(2 * n)},
        compiler_params=pltpu.CompilerParams(has_side_effects=DATAFLOW),
    )(*[pltpu.with_memory_space_constraint(a, pltpu.HBM) for a in list(blocks) + lands], after)


def _scatter_wait(started, after, name, gather=False):
    send_sems, recv_sems, *rest = started
    n = (len(rest) - 1) // 2
    thru = rest[:2 * n]

    def body(*refs):
        g_refs, land_refs = refs[:n], refs[n:2 * n]
        send_sems, recv_sems = refs[2 * n], refs[2 * n + 1]
        me, peers = _peers()
        for k, peer, slot in peers:
            for a in range(n):
                copy = pltpu.make_async_remote_copy(
                    src_ref=g_refs[a] if gather else g_refs[a].at[slot], dst_ref=land_refs[a].at[slot],
                    send_sem=send_sems.at[7 * a + k - 1],
                    recv_sem=recv_sems.at[7 * a + k - 1], device_id=peer, device_id_type=MESH)
                copy.wait_send()
                copy.wait_recv()

    out = pl.pallas_call(
        body, name=name, out_shape=tuple(pltpu.HBM(a.shape, a.dtype) for a in thru),
        in_specs=[HBM] * (2 * n) + [SEM, SEM, pl.BlockSpec(memory_space=pl.ANY)], out_specs=tuple([HBM] * (2 * n)),
        input_output_aliases={i: i for i in range(2 * n)},
        compiler_params=pltpu.CompilerParams(has_side_effects=DATAFLOW),
    )(*thru, send_sems, recv_sems, after)
    return out[:n], out[n:]


BIG = (("w_ffn1_gate", "colT"), ("w_ffn1_up", "colT"), ("w_ffn1_down", "row"), ("w_ffn2_gate", "colT"),
       ("w_ffn2_up", "colT"), ("w_ffn2_down", "row"), ("w_in", "colT"), ("w_merge", "col"), ("w_ret_out", "col"),
       ("w_fox_out", "col"), ("w_out", "row"), ("w_ple", "col"), ("w_ple_gate", "row"))


def _shard_view(a, kind):
    a = a.reshape(a.shape[-2:])
    return a.T if kind == "colT" else a


def _unview(a, kind, shape):
    return (a.T if kind == "colT" else a).reshape(shape)


def _full_from_slots(g, kind):
    n, r, c = g.shape
    return g.transpose(1, 0, 2).reshape(r, n * c) if kind == "col" else g.reshape(n * r, c)


def _slots_from_full(f, kind):
    r, c = f.shape
    return f.reshape(r, N_DEV, c // N_DEV).transpose(1, 0, 2) if kind == "col" else f.reshape(N_DEV, r // N_DEV, c)


EARLY_GROUPS = (("w_ple_gate", "w_ple", "w_ffn2_down", "w_ffn2_gate", "w_ffn2_up", "w_out", "w_ret_out", "w_fox_out"),
                ("w_in", "w_merge"))


def _scatter_group(gw, names, after, name):
    kind = dict(BIG)
    return names, _scatter_start([_slots_from_full(gw[n], kind[n]) for n in names], after, name)


def _pad_heads(w):
    d = w.shape[1]
    return jnp.pad(w.reshape(FOX_HEADS, FOX_DIM, d), ((0, 0), (0, LANES - FOX_DIM), (0, 0))).reshape(FOX_TILES, d)


def _unpad_heads(w):
    d = w.shape[1]
    return w.reshape(FOX_HEADS, LANES, d)[:, :FOX_DIM].reshape(FOX_WIDTH, d)


def _deinterleave_rows(w):
    d = w.shape[1]
    return w.reshape(RET_HEADS, RET_DIM // 2, 2, d).transpose(0, 2, 1, 3).reshape(RET_WIDTH, d)


def _interleave_rows(w):
    d = w.shape[1]
    return w.reshape(RET_HEADS, 2, RET_DIM // 2, d).transpose(0, 2, 1, 3).reshape(RET_WIDTH, d)


def _pad_w_in(wt):
    d = wt.shape[1]
    rw, fw = RET_WIDTH, FOX_WIDTH
    fo = 4 * rw
    return jnp.concatenate([
        _deinterleave_rows(wt[:rw]), _deinterleave_rows(wt[rw:2 * rw]), wt[2 * rw:4 * rw],
        _pad_heads(wt[fo:fo + fw]), _pad_heads(wt[fo + fw:fo + 2 * fw]), _pad_heads(wt[fo + 2 * fw:fo + 3 * fw]),
        wt[fo + 3 * fw:], jnp.zeros((2 * LANES - FOX_HEADS, d), wt.dtype)], axis=0)


def _unpad_w_in(g):
    rw = RET_WIDTH
    f0 = 4 * rw
    return jnp.concatenate([
        _interleave_rows(g[:rw]), _interleave_rows(g[rw:2 * rw]), g[2 * rw:4 * rw],
        _unpad_heads(g[f0:f0 + FOX_TILES]), _unpad_heads(g[f0 + FOX_TILES:f0 + 2 * FOX_TILES]),
        _unpad_heads(g[f0 + 2 * FOX_TILES:f0 + 3 * FOX_TILES]),
        g[f0 + 3 * FOX_TILES:f0 + 3 * FOX_TILES + FOX_HEADS]], axis=0)


SMALL = ("ln_ffn1", "ln_mix", "b_forget", "b_merge", "ln_ffn2", "ln_ple", "ln_final")


def _small_rows(n):
    rows = -(-n // LANES)
    return -(-rows // 8) * 8


def _pack_small(vals, with_loss=None):
    parts = []
    for name in SMALL:
        v = vals[name].reshape(-1).astype(F32)
        rows = _small_rows(v.shape[0])
        parts.append(jnp.pad(v, (0, rows * LANES - v.shape[0])).reshape(rows, LANES))
    if with_loss is not None:
        parts.append(jnp.pad(with_loss.reshape(1, LANES), ((0, 7), (0, 0))))
    else:
        parts.append(jnp.zeros((8, LANES), F32))
    return jnp.concatenate(parts, axis=0)


def _unpack_small(packed, shapes):
    out, at = {}, 0
    for name in SMALL:
        n = int(np.prod(shapes[name]))
        rows = _small_rows(n)
        out[name] = packed[at:at + rows].reshape(-1)[:n].reshape(shapes[name])
        at += rows
    return out, packed[at, 0]


def _gather_finish(group, after, me):
    names, started, wait_name = group
    kind = dict(BIG)
    sent, landed = _scatter_wait(started, after, wait_name, gather=True)
    return {n: _full_from_slots(lax.dynamic_update_slice_in_dim(land, shard[None], me, 0), kind[n])
            for n, shard, land in zip(names, sent, landed)}


def _local_step(x, p, positions, target, w, small, me, entry_token, gathers):
    t, d = x.shape
    gain = lambda n: small[n].reshape(1, d)
    w = dict(w)
    bpad = jnp.pad(small["b_forget"].reshape(1, FOX_HEADS), ((0, 0), (0, LANES - FOX_HEADS)))
    bm = small["b_merge"].reshape(1, 2 * d)
    fox_blk = _tile(t, 1024, 128)

    def ffn_fwd(n, tag, down_gather=None):
        g = _mm([(n, w[f"w_{tag}_gate"], "nt")], BF16, f"{tag}_gate", tn=1408)
        u = _mm([(n, w[f"w_{tag}_up"], "nt")], BF16, f"{tag}_up", tn=1408)
        a = _swiglu_fwd(g, u, f"{tag}_swiglu")
        if down_gather is not None:
            w.update(_gather_finish(down_gather, a, me))
        return g, u, a, _mm([(a, w[f"w_{tag}_down"], "nn")], F32, f"{tag}_down")

    n1 = _rms_fwd(x, gain("ln_ffn1") + entry_token, "rms_ffn1")
    g1, u1, a1, f1 = ffn_fwd(n1, "ffn1", down_gather=gathers[0])
    h1, u = _rms_fwd(x, gain("ln_mix"), "rms_mix", f=f1)
    w.update(_gather_finish(gathers[1], f1, me))
    w_in_t = _pad_w_in(w["w_in"])
    pm = _mm([(u, w_in_t, "nt")], F32, "mixer_in", tn=1792)
    w.update(_gather_finish(gathers[2], pm, me))
    gm = _mm([(u, w["w_merge"], "nn")], F32, "mixer_gates")

    half = jnp.arange(RET_DIM // 2, dtype=F32) / (RET_DIM // 2)
    inv = 1.0 / (ROPE_BASE ** half)
    inv2 = jnp.concatenate([inv, inv]).reshape(1, RET_DIM)
    sign2 = jnp.concatenate([-jnp.ones((RET_DIM // 2,), F32), jnp.ones((RET_DIM // 2,), F32)]).reshape(1, RET_DIM)
    cos2, sin2 = _rope_tables(positions.reshape(t, 1), inv2, sign2)
    consts = _ret_consts()
    y_ret, y_raw, states = _ret_fwd(pm, cos2, sin2, consts)
    w.update(_gather_finish(gathers[3], y_raw, me))
    w_fox_pad = _pad_heads(w["w_fox_out"])
    za = _mm([(y_ret, w["w_ret_out"], "nn")], F32, "ret_out")

    qa, ka, va = _fox_prep(pm, bpad)
    o_fox, y_fox, qa_b = _fox_fwd(qa, ka, va, fox_blk)
    zb = _mm([(y_fox, w_fox_pad, "nn")], F32, "fox_out")

    mix = _merge_fwd(gm, bm, za, zb)
    mo = _mm([(mix, w["w_out"], "nn")], F32, "mix_out")
    h2, n2 = _rms_fwd(h1, gain("ln_ffn2"), "rms_ffn2", f=mo, scale=1.0)
    w.update(_gather_finish(gathers[4], mo, me))
    g2, u2, a2, f2 = ffn_fwd(n2, "ffn2")
    h3, n3 = _rms_fwd(h2, gain("ln_ple"), "rms_ple", f=f2)
    pgl = _mm([(n3, w["w_ple_gate"], "nn")], F32, "ple_gate")
    pb = p.astype(BF16)
    pe = _mm([(pb, w["w_ple"], "nn")], F32, "ple_embed")

    gw, gs = {}, {}
    dh4, dsg, dpe, loss, gs["ln_final"] = _ple_final(h3, pgl, pe, gain("ln_final"), target)
    gw["w_ple_gate"] = _mm([(n3, dsg, "tn")], BF16, "d_w_ple_gate", tn=256)
    gw["w_ple"] = _mm([(pb, dpe, "tn")], BF16, "d_w_ple", tn=256)
    dn3 = _mm([(dsg, w["w_ple_gate"], "nt")], F32, "d_n3")
    dh3, dh3_half, gs["ln_ple"] = _rms_bwd(dn3, h3, gain("ln_ple"), dh4, "rms_ple_bwd", 0.5)

    def ffn_bwd(dh_half, g, u_, a, n, tag, scatter_now=None):
        gw[f"w_{tag}_down"] = _mm([(a, dh_half, "tn")], BF16, f"d_w_{tag}_down", tm=1408, tn=256)
        start = scatter_now if scatter_now is not None else (lambda *_: None)
        token = start((f"w_{tag}_down",), dh_half, "c")
        da = _mm([(dh_half, w[f"w_{tag}_down"], "nt")], BF16, f"d_a_{tag}", tn=1408, after=token)
        dg, du_ = _swiglu_bwd(da, g, u_, f"{tag}_swiglu_bwd")
        gw[f"w_{tag}_gate"] = _mm([(dg, n, "tn")], BF16, f"d_w_{tag}_gate", tm=1408, tn=256)
        token = start((f"w_{tag}_gate",), da, "d")
        gw[f"w_{tag}_up"] = _mm([(du_, n, "tn")], BF16, f"d_w_{tag}_up", tm=1408, tn=256, after=token)
        token = start((f"w_{tag}_up",), da, "e")
        return _mm([(dg, w[f"w_{tag}_gate"], "nn"), (du_, w[f"w_{tag}_up"], "nn")], F32, f"d_n_{tag}", tm=512,
                   after=token)

    dn2 = ffn_bwd(dh3_half, g2, u2, a2, n2, "ffn2")
    dh2, dh2_b, gs["ln_ffn2"] = _rms_bwd(dn2, h2, gain("ln_ffn2"), dh3, "rms_ffn2_bwd", 1.0)

    gw["w_out"] = _mm([(mix, dh2_b, "tn")], BF16, "d_w_out", tn=256)
    dmix = _mm([(dh2_b, w["w_out"], "nt")], F32, "d_mix")
    dza, dzb, dgm, gs["b_merge"] = _merge_bwd(dmix, gm, bm, za, zb)
    gw["w_ret_out"] = _mm([(y_ret, dza, "tn")], BF16, "d_w_ret_out", tn=256)
    gw["w_fox_out"] = _unpad_heads(_mm([(y_fox, dzb, "tn")], BF16, "d_w_fox_out", tn=256))
    dy_ret = _mm([(dza, w["w_ret_out"], "nt")], F32, "d_y_ret")
    do_fox = _mm([(dzb, w_fox_pad, "nt")], F32, "d_y_fox")

    pending = [_scatter_group(gw, EARLY_GROUPS[0], dy_ret, "grads_scatter_a_start")]
    token = pending[0][1][-1][0, 0]
    drq, drk, drv, drg = _ret_bwd(dy_ret, pm, cos2, sin2, y_raw, states, consts[:3] + (consts[3] + token,))

    dqa, dka, dva, ds_rows, ds_cols = _fox_bwd(qa_b, ka, va, do_fox, o_fox, fox_blk)
    dc = jnp.pad((ds_rows + ds_cols).reshape(FOX_HEADS, t).T, ((0, 0), (0, LANES - FOX_HEADS)))
    dff, db_forget = _fox_post(dc, pm, bpad)
    gs["b_forget"] = db_forget[:, :FOX_HEADS]

    dpm = jnp.concatenate([drq, drk, drv, drg, dqa, dka, dva, dff, jnp.zeros((t, LANES), BF16)], axis=1)
    gw["w_merge"] = _mm([(u, dgm, "tn")], BF16, "d_w_merge", tn=512)
    gw["w_in"] = _unpad_w_in(_mm([(dpm, u, "tn")], BF16, "d_w_in", tm=1792, tn=256))
    du = _mm([(dpm, w_in_t, "nn"), (dgm, w["w_merge"], "nt")], F32, "d_u", tm=1024, tn=512)
    pending.append(_scatter_group(gw, EARLY_GROUPS[1], du, "grads_scatter_b_start"))
    token = pending[1][1][-1][0:1, 0:1]
    dh1, dh1_half, gs["ln_mix"] = _rms_bwd(du, h1, gain("ln_mix") + token, dh2, "rms_mix_bwd", 0.5)

    def scatter_now(names, after, tag):
        pending.append(_scatter_group(gw, names, after, f"grads_scatter_{tag}_start"))
        return pending[-1][1][-1]

    dn1 = ffn_bwd(dh1_half, g1, u1, a1, n1, "ffn1", scatter_now=scatter_now)
    dx, _, gs["ln_ffn1"] = _rms_bwd(dn1, x, gain("ln_ffn1"), dh1, "rms_ffn1_bwd", 1.0)
    return loss, dx, gw, gs, pending


WEIGHTS = ("ln_ffn1", "w_ffn1_gate", "w_ffn1_up", "w_ffn1_down", "ln_mix", "w_in", "b_forget", "w_merge", "b_merge",
           "w_ret_out", "w_fox_out", "w_out", "ln_ffn2", "w_ffn2_gate", "w_ffn2_up", "w_ffn2_down", "ln_ple", "w_ple",
           "w_ple_gate", "ln_final")


def kernel(x, p, positions, ln_ffn1, w_ffn1_gate, w_ffn1_up, w_ffn1_down, ln_mix, w_in, b_forget, w_merge, b_merge, w_ret_out, w_fox_out, w_out, ln_ffn2, w_ffn2_gate, w_ffn2_up, w_ffn2_down, ln_ple, w_ple, w_ple_gate, ln_final, loss_target, m_ln_ffn1, m_w_ffn1_gate, m_w_ffn1_up, m_w_ffn1_down, m_ln_mix, m_w_in, m_b_forget, m_w_merge, m_b_merge, m_w_ret_out, m_w_fox_out, m_w_out, m_ln_ffn2, m_w_ffn2_gate, m_w_ffn2_up, m_w_ffn2_down, m_ln_ple, m_w_ple, m_w_ple_gate, m_ln_final, v_ln_ffn1, v_w_ffn1_gate, v_w_ffn1_up, v_w_ffn1_down, v_ln_mix, v_w_in, v_b_forget, v_w_merge, v_b_merge, v_w_ret_out, v_w_fox_out, v_w_out, v_ln_ffn2, v_w_ffn2_gate, v_w_ffn2_up, v_w_ffn2_down, v_ln_ple, v_w_ple, v_w_ple_gate, v_ln_final):
    args = dict(ln_ffn1=ln_ffn1, w_ffn1_gate=w_ffn1_gate, w_ffn1_up=w_ffn1_up, w_ffn1_down=w_ffn1_down, ln_mix=ln_mix, w_in=w_in, b_forget=b_forget, w_merge=w_merge, b_merge=b_merge, w_ret_out=w_ret_out, w_fox_out=w_fox_out, w_out=w_out, ln_ffn2=ln_ffn2, w_ffn2_gate=w_ffn2_gate, w_ffn2_up=w_ffn2_up, w_ffn2_down=w_ffn2_down, ln_ple=ln_ple, w_ple=w_ple, w_ple_gate=w_ple_gate, ln_final=ln_final)
    moms = dict(ln_ffn1=m_ln_ffn1, w_ffn1_gate=m_w_ffn1_gate, w_ffn1_up=m_w_ffn1_up, w_ffn1_down=m_w_ffn1_down, ln_mix=m_ln_mix, w_in=m_w_in, b_forget=m_b_forget, w_merge=m_w_merge, b_merge=m_b_merge, w_ret_out=m_w_ret_out, w_fox_out=m_w_fox_out, w_out=m_w_out, ln_ffn2=m_ln_ffn2, w_ffn2_gate=m_w_ffn2_gate, w_ffn2_up=m_w_ffn2_up, w_ffn2_down=m_w_ffn2_down, ln_ple=m_ln_ple, w_ple=m_w_ple, w_ple_gate=m_w_ple_gate, ln_final=m_ln_final)
    vars_ = dict(ln_ffn1=v_ln_ffn1, w_ffn1_gate=v_w_ffn1_gate, w_ffn1_up=v_w_ffn1_up, w_ffn1_down=v_w_ffn1_down, ln_mix=v_ln_mix, w_in=v_w_in, b_forget=v_b_forget, w_merge=v_w_merge, b_merge=v_b_merge, w_ret_out=v_w_ret_out, w_fox_out=v_w_fox_out, w_out=v_w_out, ln_ffn2=v_ln_ffn2, w_ffn2_gate=v_w_ffn2_gate, w_ffn2_up=v_w_ffn2_up, w_ffn2_down=v_w_ffn2_down, ln_ple=v_ln_ple, w_ple=v_w_ple, w_ple_gate=v_w_ple_gate, ln_final=v_ln_final)
    kinds = ("grad", "delta", "new_m", "new_v")

    me = 4 * lax.axis_index("x") + 2 * lax.axis_index("y") + lax.axis_index("c")
    kind_of = dict(BIG)
    shard = {n: _shard_view(args[n], kind).astype(BF16) for n, kind in BIG}
    first = ("w_ffn1_gate", "w_ffn1_up")
    groups = (("w_ffn1_down",), ("w_in",), ("w_merge",), ("w_ret_out", "w_fox_out", "w_out"))
    groups += (tuple(n for n, _ in BIG if n not in first + sum(groups, ())),)
    gathered = _all_gather([shard[n] for n in first])
    w_full = {n: _full_from_slots(g, kind_of[n]) for n, g in zip(first, gathered)}
    gathers, after = [], gathered[0]
    for tag, names in zip("zambc", groups):
        started = _scatter_start([shard[n] for n in names], after, f"weights_gather_{tag}_start", gather=True)
        gathers.append((names, started, f"weights_gather_{tag}_wait"))
        after = started[-1]

    small = {n: args[n] for n in SMALL}
    loss_part, dx, gw, gs, pending = _local_step(x[0], p[0, 0], positions[0], loss_target[0], w_full, small, me,
                                                 after[0:1, 0:1], gathers)

    parts_of = {}
    for tag, (names, started) in zip("abcde", pending):
        sent, landed = _scatter_wait(started, dx, f"grads_scatter_{tag}_wait")
        for n, blk, land in zip(names, sent, landed):
            own = lax.dynamic_index_in_dim(blk, me, 0, keepdims=True)
            parts_of[n] = lax.dynamic_update_slice_in_dim(land, own, me, 0)
    late = [(n, kind) for n, kind in BIG if n not in parts_of]
    small_part = _pack_small(gs, with_loss=loss_part)
    blocks = [_slots_from_full(gw[n], kind) for n, kind in late]
    blocks.append(jnp.broadcast_to(small_part, (N_DEV,) + small_part.shape))
    recv = _reduce_scatter_exchange(blocks)
    parts_of.update({n: r for (n, _), r in zip(late, recv)})

    res = {}
    for n, kind in BIG:
        parts = parts_of[n]
        outs = _adamw(parts, _shard_view(args[n], kind), _shard_view(moms[n], kind), _shard_view(vars_[n], kind),
                      f"adamw_{n}")
        for what, o in zip(kinds, outs):
            res[(what, n)] = _unview(o, kind, args[n].shape)
    s_outs = _adamw(recv[-1], _pack_small(small), _pack_small({n: moms[n] for n in SMALL}),
                    _pack_small({n: vars_[n] for n in SMALL}), "adamw_small")
    for what, sm in zip(kinds, s_outs):
        svals, extra = _unpack_small(sm, {n: args[n].shape for n in SMALL})
        if what == "grad":
            loss = extra
        for n in SMALL:
            res[(what, n)] = svals[n]
    return (loss, dx[None], *[res[(what, n)] for what in kinds for n in WEIGHTS])
```

```python
import numpy as np
import jax
import jax.numpy as jnp
from jax import lax
from jax.experimental import pallas as pl
from jax.experimental.pallas import tpu as pltpu

F32 = jnp.float32
BF16 = jnp.bfloat16

N_DEV = 8
EPS = 1e-6
RET_HEADS = 4
RET_DIM = 128
RET_WIDTH = RET_HEADS * RET_DIM
FOX_HEADS = 8
FOX_DIM = 64
FOX_WIDTH = FOX_HEADS * FOX_DIM
CHUNK = 128
ROPE_BASE = 10000.0
LANES = 128
FOX_TILES = FOX_HEADS * LANES
IN_COLS = 4 * RET_WIDTH + 3 * FOX_WIDTH + FOX_HEADS
IN_PAD = 4 * RET_WIDTH + 3 * FOX_TILES + 2 * LANES
TILE_RQ, TILE_RK, TILE_RV, TILE_RG = 0, 4, 8, 12
TILE_FQ, TILE_FK, TILE_FV, TILE_FF = 16, 24, 32, 40
NEG = -1e30

ADAM_LR = 0.001
ADAM_B1 = 0.9
ADAM_B2 = 0.999
ADAM_EPS = 1e-08
ADAM_WD = 0.01
ADAM_STEP = 10

VMEM_LIMIT_BYTES = 56 * 1024 * 1024

MESH = pl.DeviceIdType.MESH


def _tile(dim, pref, mult):
    if dim <= pref:
        return dim
    t = (pref // mult) * mult
    while t >= mult:
        if dim % t == 0:
            return t
        t -= mult
    return dim


def _params(dims):
    return pltpu.CompilerParams(dimension_semantics=dims, vmem_limit_bytes=VMEM_LIMIT_BYTES)


def _pcall(body, *, name, out_shape, grid, in_specs, out_specs, scratch_shapes=(), dims=None):
    return pl.pallas_call(body, name=name, out_shape=out_shape, grid=grid, in_specs=in_specs, out_specs=out_specs,
                          scratch_shapes=list(scratch_shapes), compiler_params=_params(dims))


def _dot(a, b, ca, cb):
    return lax.dot_general(a, b, (((ca,), (cb,)), ((), ())), preferred_element_type=F32)


def _sigmoid(x):
    return 1.0 / (1.0 + jnp.exp(-x))


def _mm(pairs, out_dtype, name, tm=1024, tn=1024, after=None):
    dims = []
    for a, b, mode in pairs:
        m, k = (a.shape[1], a.shape[0]) if mode == "tn" else a.shape
        n, k2 = b.shape if mode == "nt" else (b.shape[1], b.shape[0])
        assert k == k2, (name, a.shape, b.shape, mode)
        dims.append((m, n))
    assert all(d == dims[0] for d in dims), (name, dims)
    m, n = dims[0]
    tm = _tile(m, tm, 128 if any(mode == "tn" for _, _, mode in pairs) else 16)
    tn = _tile(n, tn, 128)
    in_specs, contract, operands = [], [], []
    for a, b, mode in pairs:
        k = a.shape[0] if mode == "tn" else a.shape[1]
        in_specs.append(pl.BlockSpec((k, tm), lambda i, j: (0, i)) if mode == "tn" else
                        pl.BlockSpec((tm, k), lambda i, j: (i, 0)))
        in_specs.append(pl.BlockSpec((tn, k), lambda i, j: (j, 0)) if mode == "nt" else
                        pl.BlockSpec((k, tn), lambda i, j: (0, j)))
        contract.append((0 if mode == "tn" else 1, 1 if mode == "nt" else 0))
        operands += [a, b]
    if after is not None:
        in_specs.append(pl.BlockSpec(memory_space=pl.ANY))
        operands.append(after)

    def body(*refs):
        o_ref = refs[-1]
        acc = None
        for p, (ca, cb) in enumerate(contract):
            part = _dot(refs[2 * p][...], refs[2 * p + 1][...], ca, cb)
            acc = part if acc is None else acc + part
        o_ref[...] = acc.astype(out_dtype)

    return _pcall(body, name=name, out_shape=jax.ShapeDtypeStruct((m, n), out_dtype), grid=(m // tm, n // tn),
                  in_specs=in_specs, out_specs=pl.BlockSpec((tm, tn), lambda i, j: (i, j)),
                  dims=("parallel", "parallel"))(*operands)


def _rms_fwd(h, gain, name, f=None, scale=0.5):
    t, d = h.shape
    tt = _tile(t, 512, 16)
    row = pl.BlockSpec((tt, d), lambda i: (i, 0))
    vec = pl.BlockSpec((1, d), lambda i: (0, 0))

    def norm(hv, g_ref, n_ref):
        r = lax.rsqrt(jnp.mean(hv * hv, axis=-1, keepdims=True) + EPS)
        n_ref[...] = (hv * r * g_ref[...]).astype(BF16)

    if f is None:

        def body(h_ref, g_ref, n_ref):
            norm(h_ref[...], g_ref, n_ref)

        return _pcall(body, name=name, out_shape=jax.ShapeDtypeStruct((t, d), BF16), grid=(t // tt,),
                      in_specs=[row, vec], out_specs=row, dims=("parallel",))(h, gain)

    def body(h_ref, f_ref, g_ref, hn_ref, n_ref):
        hv = h_ref[...] + scale * f_ref[...]
        hn_ref[...] = hv
        norm(hv, g_ref, n_ref)

    return _pcall(body, name=name,
                  out_shape=(jax.ShapeDtypeStruct((t, d), F32), jax.ShapeDtypeStruct((t, d), BF16)),
                  grid=(t // tt,), in_specs=[row, row, vec], out_specs=(row, row), dims=("parallel",))(h, f, gain)


def _rms_bwd(dn, h, gain, dh_in, name, out_scale):
    t, d = h.shape
    tt = _tile(t, 512, 16)
    row = pl.BlockSpec((tt, d), lambda i: (i, 0))
    vec = pl.BlockSpec((1, d), lambda i: (0, 0))

    def body(dn_ref, h_ref, g_ref, dhin_ref, dh_ref, dhb_ref, dg_ref):
        hv = h_ref[...]
        dnv = dn_ref[...].astype(F32)
        r = lax.rsqrt(jnp.mean(hv * hv, axis=-1, keepdims=True) + EPS)
        dng = dnv * g_ref[...]
        dh = dhin_ref[...] + r * dng - hv * (r * r * r) * jnp.mean(dng * hv, axis=-1, keepdims=True)
        dh_ref[...] = dh
        dhb_ref[...] = (out_scale * dh).astype(BF16)
        part = jnp.sum(dnv * hv * r, axis=0, keepdims=True)

        @pl.when(pl.program_id(0) == 0)
        def _():
            dg_ref[...] = part

        @pl.when(pl.program_id(0) > 0)
        def _():
            dg_ref[...] += part

    return _pcall(body, name=name,
                  out_shape=(jax.ShapeDtypeStruct((t, d), F32), jax.ShapeDtypeStruct((t, d), BF16),
                             jax.ShapeDtypeStruct((1, d), F32)),
                  grid=(t // tt,), in_specs=[row, row, vec, row], out_specs=(row, row, vec),
                  dims=("arbitrary",))(dn, h, gain, dh_in)


def _swiglu_fwd(g, u, name):
    t, f = g.shape
    tt = _tile(t, 512, 16)
    row = pl.BlockSpec((tt, f), lambda i: (i, 0))

    def body(g_ref, u_ref, a_ref):
        gv = g_ref[...].astype(F32)
        a_ref[...] = (gv * _sigmoid(gv) * u_ref[...].astype(F32)).astype(BF16)

    return _pcall(body, name=name, out_shape=jax.ShapeDtypeStruct((t, f), BF16), grid=(t // tt,),
                  in_specs=[row, row], out_specs=row, dims=("parallel",))(g, u)


def _swiglu_bwd(da, g, u, name):
    t, f = g.shape
    tt = _tile(t, 512, 16)
    row = pl.BlockSpec((tt, f), lambda i: (i, 0))

    def body(da_ref, g_ref, u_ref, dg_ref, du_ref):
        gv = g_ref[...].astype(F32)
        dav = da_ref[...].astype(F32)
        sg = _sigmoid(gv)
        dg_ref[...] = (dav * u_ref[...].astype(F32) * (sg * (1.0 + gv * (1.0 - sg)))).astype(BF16)
        du_ref[...] = (dav * (gv * sg)).astype(BF16)

    return _pcall(body, name=name, out_shape=(jax.ShapeDtypeStruct((t, f), BF16),) * 2, grid=(t // tt,),
                  in_specs=[row, row, row], out_specs=(row, row), dims=("parallel",))(da, g, u)


def _rope_tables(pos_col, inv2, sign2):
    t = pos_col.shape[0]

    def body(p_ref, inv_ref, sg_ref, c_ref, s_ref):
        ang = p_ref[...].astype(F32) * inv_ref[...]
        c_ref[...] = jnp.cos(ang)
        s_ref[...] = jnp.sin(ang) * sg_ref[...]

    full = lambda shape: pl.BlockSpec(shape, lambda i: (0, 0))
    return _pcall(body, name="rope_tables", out_shape=(jax.ShapeDtypeStruct((t, RET_DIM), F32),) * 2, grid=(1,),
                  in_specs=[full((t, 1)), full((1, RET_DIM)), full((1, RET_DIM))],
                  out_specs=(full((t, RET_DIM)),) * 2, dims=("arbitrary",))(pos_col, inv2, sign2)


def _rot(x, c, s):
    return x * c + pltpu.roll(x, RET_DIM // 2, 1) * s


def _rot_t(g, c, s):
    return g * c + pltpu.roll(g * s, RET_DIM // 2, 1)


def _ret_consts():
    hh = np.arange(RET_HEADS, dtype=np.float32)
    log_gamma = np.log1p(-np.exp2(-5.0 - hh)).astype(np.float32)
    idx = np.arange(CHUNK, dtype=np.float32)
    diff = idx[:, None] - idx[None, :]
    dmask = np.where(diff >= 0, np.exp(log_gamma[:, None, None] * np.maximum(diff, 0.0)), 0.0).astype(np.float32)
    kdec = np.exp(log_gamma[:, None] * (CHUNK - 1 - idx)).astype(np.float32)
    qdec = np.exp(log_gamma[:, None] * (idx + 1.0)).astype(np.float32)
    cdec = np.exp(log_gamma * CHUNK).astype(np.float32)
    bc = lambda v: np.ascontiguousarray(np.broadcast_to(v[:, :, None], (RET_HEADS, CHUNK, RET_DIM)))
    cd = np.ascontiguousarray(np.broadcast_to(cdec[:, None, None], (RET_HEADS, 8, RET_DIM)))
    return jnp.asarray(dmask), jnp.asarray(bc(qdec)), jnp.asarray(bc(kdec)), jnp.asarray(cd)


def _ret_fwd(pm, cos2, sin2, consts):
    t = pm.shape[0]
    n_chunks = t // CHUNK
    dmask, qdec, kdec, cd = consts
    scale = RET_DIM ** -0.5
    wide = lambda c0: pl.BlockSpec((CHUNK, RET_WIDTH), lambda n: (n, c0 // RET_HEADS))
    tab = pl.BlockSpec((CHUNK, RET_DIM), lambda n: (n, 0))
    const = lambda a: pl.BlockSpec(a.shape, lambda n: (0,) * a.ndim)

    def body(q_ref, k_ref, v_ref, g_ref, c_ref, s_ref, dm_ref, qd_ref, kd_ref, cd_ref, y_ref, raw_ref, st_ref, s_acc):
        @pl.when(pl.program_id(0) == 0)
        def _():
            s_acc[...] = jnp.zeros_like(s_acc)

        c, s = c_ref[...], s_ref[...]
        for h in range(RET_HEADS):
            hs = slice(h * RET_DIM, (h + 1) * RET_DIM)
            q = _rot(q_ref[:, hs], c, s)
            k = _rot(k_ref[:, hs], c, s) * scale
            vb = v_ref[:, hs].astype(BF16)
            g = g_ref[:, hs]
            s_in = s_acc[h]
            st_ref[h] = s_in
            a = _dot(q.astype(BF16), k.astype(BF16), 1, 1) * dm_ref[h]
            y = _dot(a.astype(BF16), vb, 1, 0) + _dot((q * qd_ref[h]).astype(BF16), s_in.astype(BF16), 1, 0)
            s_acc[h] = cd_ref[h, 0:1, :] * s_in + _dot((k * kd_ref[h]).astype(BF16), vb, 0, 0)
            raw_ref[:, hs] = y
            mu = jnp.mean(y, axis=-1, keepdims=True)
            yc = y - mu
            rs = lax.rsqrt(jnp.mean(yc * yc, axis=-1, keepdims=True) + EPS)
            y_ref[:, hs] = (yc * rs * (g * _sigmoid(g))).astype(BF16)

    out_blk = pl.BlockSpec((CHUNK, RET_WIDTH), lambda n: (n, 0))
    return _pcall(
        body, name="retention_fwd",
        out_shape=(jax.ShapeDtypeStruct((t, RET_WIDTH), BF16), jax.ShapeDtypeStruct((t, RET_WIDTH), F32),
                   jax.ShapeDtypeStruct((RET_HEADS, n_chunks, RET_DIM, RET_DIM), F32)),
        grid=(n_chunks,),
        in_specs=[wide(TILE_RQ), wide(TILE_RK), wide(TILE_RV), wide(TILE_RG), tab, tab,
                  const(dmask), const(qdec), const(kdec), const(cd)],
        out_specs=(out_blk, out_blk, pl.BlockSpec((RET_HEADS, None, RET_DIM, RET_DIM), lambda n: (0, n, 0, 0))),
        scratch_shapes=[pltpu.VMEM((RET_HEADS, RET_DIM, RET_DIM), F32)],
        dims=("arbitrary",),
    )(pm, pm, pm, pm, cos2, sin2, dmask, qdec, kdec, cd)


def _ret_bwd(dy, pm, cos2, sin2, raw, states, consts):
    t = pm.shape[0]
    n_chunks = t // CHUNK
    dmask, qdec, kdec, cd = consts
    scale = RET_DIM ** -0.5
    rev = lambda n: n_chunks - 1 - n
    wide = lambda c0: pl.BlockSpec((CHUNK, RET_WIDTH), lambda n: (rev(n), c0 // RET_HEADS))
    tab = pl.BlockSpec((CHUNK, RET_DIM), lambda n: (rev(n), 0))
    blk = pl.BlockSpec((CHUNK, RET_WIDTH), lambda n: (rev(n), 0))
    const = lambda a: pl.BlockSpec(a.shape, lambda n: (0,) * a.ndim)

    def body(dy_ref, q_ref, k_ref, v_ref, g_ref, c_ref, s_ref, raw_ref, st_ref, dm_ref, qd_ref, kd_ref, cd_ref,
             dq_ref, dk_ref, dv_ref, dg_ref, ds_acc):
        @pl.when(pl.program_id(0) == 0)
        def _():
            ds_acc[...] = jnp.zeros_like(ds_acc)

        c, s = c_ref[...], s_ref[...]
        for h in range(RET_HEADS):
            hs = slice(h * RET_DIM, (h + 1) * RET_DIM)
            q = _rot(q_ref[:, hs], c, s)
            k = _rot(k_ref[:, hs], c, s) * scale
            qb, kb, vb = q.astype(BF16), k.astype(BF16), v_ref[:, hs].astype(BF16)
            g = g_ref[:, hs]
            dm, qd, kd = dm_ref[h], qd_ref[h], kd_ref[h]
            y = raw_ref[:, hs]
            mu = jnp.mean(y, axis=-1, keepdims=True)
            yc = y - mu
            rs = lax.rsqrt(jnp.mean(yc * yc, axis=-1, keepdims=True) + EPS)
            yn = yc * rs
            sg = _sigmoid(g)
            dyo = dy_ref[:, hs]
            dg_ref[:, hs] = (dyo * yn * (sg * (1.0 + g * (1.0 - sg)))).astype(BF16)
            dyn = dyo * (g * sg)
            dyr = rs * (dyn - jnp.mean(dyn, axis=-1, keepdims=True) - yn * jnp.mean(dyn * yn, axis=-1, keepdims=True))
            dyb = dyr.astype(BF16)
            s_in = st_ref[h].astype(BF16)
            ds_out = ds_acc[h]
            dsb = ds_out.astype(BF16)
            a = _dot(qb, kb, 1, 1) * dm
            da = (_dot(dyb, vb, 1, 1) * dm).astype(BF16)
            kdb = (k * kd).astype(BF16)
            qdb = (q * qd).astype(BF16)
            dv_ref[:, hs] = (_dot(a.astype(BF16), dyb, 0, 0) + _dot(kdb, dsb, 1, 0)).astype(BF16)
            dqh = _dot(da, kb, 1, 0) + _dot(dyb, s_in, 1, 1) * qd
            dkh = _dot(da, qb, 0, 0) + _dot(vb, dsb, 1, 1) * kd
            ds_acc[h] = cd_ref[h, 0:1, :] * ds_out + _dot(qdb, dyb, 0, 0)
            dq_ref[:, hs] = _rot_t(dqh, c, s).astype(BF16)
            dk_ref[:, hs] = (_rot_t(dkh, c, s) * scale).astype(BF16)

    return _pcall(
        body, name="retention_bwd",
        out_shape=(jax.ShapeDtypeStruct((t, RET_WIDTH), BF16),) * 4,
        grid=(n_chunks,),
        in_specs=[blk, wide(TILE_RQ), wide(TILE_RK), wide(TILE_RV), wide(TILE_RG), tab, tab, blk,
                  pl.BlockSpec((RET_HEADS, None, RET_DIM, RET_DIM), lambda n: (0, rev(n), 0, 0)),
                  const(dmask), const(qdec), const(kdec), const(cd)],
        out_specs=(blk,) * 4,
        scratch_shapes=[pltpu.VMEM((RET_HEADS, RET_DIM, RET_DIM), F32)],
        dims=("arbitrary",),
    )(dy, pm, pm, pm, pm, cos2, sin2, raw, states, dmask, qdec, kdec, cd)


FOX_C_LANE = FOX_DIM
FOX_NEGC_LANE = FOX_DIM + 3
FOX_LSE_LANE = FOX_DIM + 6
FOX_L_LANE = FOX_C_LANE
FOX_ROWSUM_LANE = FOX_C_LANE
FOX_COLSUM_LANE = FOX_NEGC_LANE


def _split3(x):
    hi = x.astype(BF16)
    r1 = x - hi.astype(F32)
    mid = r1.astype(BF16)
    lo = (r1 - mid.astype(F32)).astype(BF16)
    return hi, mid, lo


def _tri_dot(tri, x):
    hi, mid, lo = _split3(x)
    return _dot(tri, lo, 1, 0) + _dot(tri, mid, 1, 0) + _dot(tri, hi, 1, 0)


def _log_sigmoid(z):
    return jnp.minimum(z, 0.0) - jnp.log1p(jnp.exp(-jnp.abs(z)))


def _fox_consts():
    place = np.zeros((2, 3, LANES, FOX_TILES), np.float32)
    ones = np.zeros((3, 1, FOX_TILES), np.float32)
    for h in range(FOX_HEADS):
        for part in range(3):
            place[0, part, h, LANES * h + FOX_C_LANE + part] = 1.0
            place[1, part, h, LANES * h + FOX_NEGC_LANE + part] = -1.0
            ones[0, 0, LANES * h + FOX_NEGC_LANE + part] = 1.0
            ones[1, 0, LANES * h + FOX_C_LANE + part] = 1.0
            ones[1, 0, LANES * h + FOX_LSE_LANE + part] = 1.0
            ones[2, 0, LANES * h + FOX_C_LANE + part] = 1.0
    return jnp.asarray(place, BF16), jnp.asarray(ones, F32)


def _fox_prep(pm, bpad):
    t = pm.shape[0]
    tt = _tile(t, 512, LANES)
    place, ones = _fox_consts()
    wide = lambda c0: pl.BlockSpec((tt, FOX_TILES), lambda i: (i, c0 // FOX_HEADS))
    const = lambda a: pl.BlockSpec(a.shape, lambda i: (0,) * a.ndim)

    def body(q_ref, k_ref, v_ref, ff_ref, b_ref, pl_ref, on_ref, qa_ref, ka_ref, va_ref, carry_s):
        @pl.when(pl.program_id(0) == 0)
        def _():
            carry_s[...] = jnp.zeros_like(carry_s)

        r = lax.broadcasted_iota(jnp.int32, (LANES, LANES), 0)
        cc = lax.broadcasted_iota(jnp.int32, (LANES, LANES), 1)
        tri = jnp.where(cc <= r, 1.0, 0.0).astype(BF16)
        bias = b_ref[...]
        for sub in range(tt // LANES):
            rows = pl.ds(sub * LANES, LANES)
            cs = _tri_dot(tri, _log_sigmoid(ff_ref[rows, :] + bias)) + carry_s[...]
            carry_s[...] = cs[LANES - 1:LANES, :]
            parts = _split3(cs)
            eq = sum(_dot(part, pl_ref[0, i], 1, 0) for i, part in enumerate(parts))
            ek = sum(_dot(part, pl_ref[1, i], 1, 0) for i, part in enumerate(parts))
            qa_ref[rows, :] = (q_ref[rows, :] * FOX_DIM ** -0.5 + eq + on_ref[0]).astype(BF16)
            ka_ref[rows, :] = (k_ref[rows, :] + ek + on_ref[1]).astype(BF16)
            va_ref[rows, :] = (v_ref[rows, :] + on_ref[2]).astype(BF16)

    out = pl.BlockSpec((tt, FOX_TILES), lambda i: (i, 0))
    return _pcall(body, name="fox_prep", out_shape=(jax.ShapeDtypeStruct((t, FOX_TILES), BF16),) * 3, grid=(t // tt,),
                  in_specs=[wide(TILE_FQ), wide(TILE_FK), wide(TILE_FV), pl.BlockSpec((tt, LANES), lambda i: (i, TILE_FF)),
                            pl.BlockSpec((1, LANES), lambda i: (0, 0)), const(place), const(ones)],
                  out_specs=(out,) * 3, scratch_shapes=[pltpu.VMEM((1, LANES), F32)],
                  dims=("arbitrary",))(pm, pm, pm, pm, bpad, place, ones)


def _fox_post(dc, pm, bpad):
    t = pm.shape[0]
    nb = t // LANES

    def body(dc_ref, ff_ref, b_ref, d_ref, db_ref):
        r = lax.broadcasted_iota(jnp.int32, (LANES, LANES), 0)
        cc = lax.broadcasted_iota(jnp.int32, (LANES, LANES), 1)
        tri = jnp.where(cc >= r, 1.0, 0.0).astype(BF16)
        bias = b_ref[...]

        def step(i, carry):
            tail, acc = carry
            rows = pl.ds(pl.multiple_of((nb - 1 - i) * LANES, LANES), LANES)
            cs = _tri_dot(tri, dc_ref[rows, :]) + tail
            dff = cs * _sigmoid(-(ff_ref[rows, :] + bias))
            d_ref[rows, :] = dff.astype(BF16)
            return cs[0:1, :], acc + jnp.sum(dff, axis=0, keepdims=True)

        zero = jnp.zeros((1, LANES), F32)
        _, acc = lax.fori_loop(0, nb, step, (zero, zero))
        db_ref[...] = acc

    return _pcall(body, name="fox_forget_bwd",
                  out_shape=(jax.ShapeDtypeStruct((t, LANES), BF16), jax.ShapeDtypeStruct((1, LANES), F32)), grid=(1,),
                  in_specs=[pl.BlockSpec((t, LANES), lambda i: (0, 0)), pl.BlockSpec((t, LANES), lambda i: (0, TILE_FF)),
                            pl.BlockSpec((1, LANES), lambda i: (0, 0))],
                  out_specs=(pl.BlockSpec((t, LANES), lambda i: (0, 0)), pl.BlockSpec((1, LANES), lambda i: (0, 0))),
                  dims=("arbitrary",))(dc, pm, bpad)


def _tri_tables(nb, q_major):
    pairs = [(i, j) for i in range(nb) for j in range(i + 1)] if q_major else \
            [(i, j) for j in range(nb) for i in range(j, nb)]
    return jnp.asarray([a for a, _ in pairs], jnp.int32), jnp.asarray([b for _, b in pairs], jnp.int32)


def _causal(s):
    n = s.shape[0]
    row = lax.broadcasted_iota(jnp.int32, (n, n), 0)
    col = lax.broadcasted_iota(jnp.int32, (n, n), 1)
    return jnp.where(col <= row, s, NEG)


def _lane_col(x, lane):
    sel = lax.broadcasted_iota(jnp.int32, x.shape, 1) == lane
    return jnp.sum(jnp.where(sel, x, 0.0), axis=1, keepdims=True)


def _fox_fwd(qa, ka, va, blk):
    t = qa.shape[0]
    nb = t // blk
    qi, kj = _tri_tables(nb, True)
    q_spec = pl.BlockSpec((blk, LANES), lambda h, s, qi_r, kj_r: (qi_r[s], h))
    k_spec = pl.BlockSpec((blk, LANES), lambda h, s, qi_r, kj_r: (kj_r[s], h))

    def body(qi_r, kj_r, q_ref, k_ref, v_ref, o_ref, ob_ref, qb_ref, m_s, acc_s):
        s_id = pl.program_id(1)
        i, j = qi_r[s_id], kj_r[s_id]

        @pl.when(j == 0)
        def _():
            m_s[...] = jnp.full_like(m_s, NEG)
            acc_s[...] = jnp.zeros_like(acc_s)

        def tile(diagonal):
            s = _dot(q_ref[...], k_ref[...], 1, 1)
            if diagonal:
                s = _causal(s)
            m_old = m_s[...]
            m_new = jnp.maximum(m_old, jnp.max(s, axis=1, keepdims=True))
            p = jnp.exp(s - jnp.tile(m_new, (1, blk // LANES)))
            acc_s[...] = jnp.exp(m_old - m_new) * acc_s[...] + _dot(p.astype(BF16), v_ref[...], 1, 0)
            m_s[...] = m_new

        @pl.when(j < i)
        def _():
            tile(False)

        @pl.when(j == i)
        def _():
            tile(True)
            acc = acc_s[...]
            l = _lane_col(acc, FOX_L_LANE)
            o = acc / l
            o_ref[...] = o
            ob_ref[...] = o.astype(BF16)
            hi, mid, lo = _split3(-(m_s[:, 0:1] + jnp.log(l)))
            lane = lax.broadcasted_iota(jnp.int32, acc.shape, 1)
            qb_ref[...] = jnp.where(lane == FOX_LSE_LANE, hi,
                                    jnp.where(lane == FOX_LSE_LANE + 1, mid,
                                              jnp.where(lane == FOX_LSE_LANE + 2, lo, q_ref[...])))

    wide = (t, FOX_TILES)
    return pl.pallas_call(
        body, name="fox_fwd",
        out_shape=(jax.ShapeDtypeStruct(wide, F32), jax.ShapeDtypeStruct(wide, BF16), jax.ShapeDtypeStruct(wide, BF16)),
        grid_spec=pltpu.PrefetchScalarGridSpec(
            num_scalar_prefetch=2, grid=(FOX_HEADS, qi.shape[0]), in_specs=[q_spec, k_spec, k_spec],
            out_specs=(q_spec,) * 3,
            scratch_shapes=[pltpu.VMEM((blk, LANES), F32), pltpu.VMEM((blk, LANES), F32)]),
        compiler_params=_params(("parallel", "arbitrary")),
    )(qi, kj, qa, ka, va)


def _fox_bwd(qa, ka, va, do, o, blk):
    t = qa.shape[0]
    nb = t // blk
    qi, kj = _tri_tables(nb, False)
    q_spec = pl.BlockSpec((blk, LANES), lambda h, s, qi_r, kj_r: (qi_r[s], h))
    k_spec = pl.BlockSpec((blk, LANES), lambda h, s, qi_r, kj_r: (kj_r[s], h))
    head_spec = pl.BlockSpec((t, LANES), lambda h, s, qi_r, kj_r: (0, h))
    head_col = pl.BlockSpec((None, t, 1), lambda h, s, qi_r, kj_r: (h, 0, 0))
    k_col = pl.BlockSpec((None, blk, 1), lambda h, s, qi_r, kj_r: (h, kj_r[s], 0))
    first_spec = pl.BlockSpec((blk, LANES), lambda h, s, qi_r, kj_r: (jnp.where(kj_r[s] == 0, qi_r[s], nb - 1), h))
    n_steps = int(qi.shape[0])

    def body(qi_r, kj_r, q_ref, k_ref, v_ref, do_ref, o_ref, dq_ref, dk_ref, dv_ref, rs_ref, cs_ref,
             doa_s, dq_s, dk_s, dv_s):
        s_id = pl.program_id(1)
        i, j = qi_r[s_id], kj_r[s_id]
        rows = pl.ds(pl.multiple_of(i * blk, blk), blk)

        @pl.when(j == 0)
        def _():
            dof = do_ref[...]
            hi, mid, lo = _split3(-jnp.sum(dof * o_ref[...], axis=1, keepdims=True))
            lane = lax.broadcasted_iota(jnp.int32, dof.shape, 1)
            doa = jnp.where(lane == FOX_C_LANE, hi.astype(F32),
                            jnp.where(lane == FOX_C_LANE + 1, mid.astype(F32),
                                      jnp.where(lane == FOX_C_LANE + 2, lo.astype(F32), dof)))
            doa_s[rows, :] = doa.astype(BF16)
            dq_s[rows, :] = jnp.zeros((blk, LANES), F32)

        @pl.when(i == j)
        def _():
            dk_s[...] = jnp.zeros_like(dk_s)
            dv_s[...] = jnp.zeros_like(dv_s)

        def tile(diagonal):
            q, k = q_ref[...], k_ref[...]
            s = _dot(q, k, 1, 1)
            if diagonal:
                s = _causal(s)
            p = jnp.exp(s)
            doa = doa_s[rows, :]
            ds = (p * _dot(doa, v_ref[...], 1, 1)).astype(BF16)
            dv_s[...] += _dot(p.astype(BF16), doa, 0, 0)
            dk_s[...] += _dot(ds, q, 0, 0)
            dq_s[rows, :] += _dot(ds, k, 1, 0)

        @pl.when(i > j)
        def _():
            tile(False)

        @pl.when(i == j)
        def _():
            tile(True)

        @pl.when(i == nb - 1)
        def _():
            dk = dk_s[...]
            dk_ref[...] = dk.astype(BF16)
            dv_ref[...] = dv_s[...].astype(BF16)
            cs_ref[...] = -_lane_col(dk, FOX_COLSUM_LANE)

        @pl.when(s_id == n_steps - 1)
        def _():
            dq = dq_s[...]
            dq_ref[...] = (dq * FOX_DIM ** -0.5).astype(BF16)
            rs_ref[...] = _lane_col(dq, FOX_ROWSUM_LANE)

    wide = jax.ShapeDtypeStruct((t, FOX_TILES), BF16)
    cols = jax.ShapeDtypeStruct((FOX_HEADS, t, 1), F32)
    return pl.pallas_call(
        body, name="fox_bwd", out_shape=(wide, wide, wide, cols, cols),
        grid_spec=pltpu.PrefetchScalarGridSpec(
            num_scalar_prefetch=2, grid=(FOX_HEADS, n_steps),
            in_specs=[q_spec, k_spec, k_spec, first_spec, first_spec],
            out_specs=(head_spec, k_spec, k_spec, head_col, k_col),
            scratch_shapes=[pltpu.VMEM((t, LANES), BF16), pltpu.VMEM((t, LANES), F32), pltpu.VMEM((blk, LANES), F32),
                            pltpu.VMEM((blk, LANES), F32)]),
        compiler_params=_params(("parallel", "arbitrary")),
    )(qi, kj, qa, ka, va, do, o)


def _merge_fwd(gm, bm, za, zb):
    t, d = za.shape
    tt = _tile(t, 512, 16)
    row = pl.BlockSpec((tt, d), lambda i: (i, 0))

    def body(gm_ref, b_ref, za_ref, zb_ref, o_ref):
        ga = _sigmoid(gm_ref[:, :d] + b_ref[:, :d])
        gb = _sigmoid(gm_ref[:, d:] + b_ref[:, d:])
        o_ref[...] = (ga * za_ref[...] + gb * zb_ref[...]).astype(BF16)

    return _pcall(body, name="merge_fwd", out_shape=jax.ShapeDtypeStruct((t, d), BF16), grid=(t // tt,),
                  in_specs=[pl.BlockSpec((tt, 2 * d), lambda i: (i, 0)), pl.BlockSpec((1, 2 * d), lambda i: (0, 0)), row, row],
                  out_specs=row, dims=("parallel",))(gm, bm, za, zb)


def _merge_bwd(dmix, gm, bm, za, zb):
    t, d = za.shape
    tt = _tile(t, 512, 16)
    row = pl.BlockSpec((tt, d), lambda i: (i, 0))
    wide = pl.BlockSpec((tt, 2 * d), lambda i: (i, 0))
    vec = pl.BlockSpec((1, 2 * d), lambda i: (0, 0))

    def body(dm_ref, gm_ref, b_ref, za_ref, zb_ref, dza_ref, dzb_ref, dgm_ref, db_ref):
        dm = dm_ref[...]
        ga = _sigmoid(gm_ref[:, :d] + b_ref[:, :d])
        gb = _sigmoid(gm_ref[:, d:] + b_ref[:, d:])
        dza_ref[...] = (dm * ga).astype(BF16)
        dzb_ref[...] = (dm * gb).astype(BF16)
        dla = dm * za_ref[...] * ga * (1.0 - ga)
        dlb = dm * zb_ref[...] * gb * (1.0 - gb)
        dgm_ref[:, :d] = dla.astype(BF16)
        dgm_ref[:, d:] = dlb.astype(BF16)
        pa = jnp.sum(dla, axis=0, keepdims=True)
        pb = jnp.sum(dlb, axis=0, keepdims=True)

        @pl.when(pl.program_id(0) == 0)
        def _():
            db_ref[:, :d] = pa
            db_ref[:, d:] = pb

        @pl.when(pl.program_id(0) > 0)
        def _():
            db_ref[:, :d] += pa
            db_ref[:, d:] += pb

    return _pcall(body, name="merge_bwd",
                  out_shape=(jax.ShapeDtypeStruct((t, d), BF16), jax.ShapeDtypeStruct((t, d), BF16),
                             jax.ShapeDtypeStruct((t, 2 * d), BF16), jax.ShapeDtypeStruct((1, 2 * d), F32)),
                  grid=(t // tt,), in_specs=[row, wide, vec, row, row], out_specs=(row, row, wide, vec),
                  dims=("arbitrary",))(dmix, gm, bm, za, zb)


def _ple_final(h3, pgl, pe, gain, target):
    t, d = h3.shape
    tt = _tile(t, 512, 16)
    row = pl.BlockSpec((tt, d), lambda i: (i, 0))
    vec = pl.BlockSpec((1, d), lambda i: (0, 0))
    lvec = pl.BlockSpec((1, LANES), lambda i: (0, 0))

    def body(h_ref, pgl_ref, pe_ref, g_ref, t_ref, dh_ref, dsg_ref, dpe_ref, loss_ref, dg_ref):
        pg = _sigmoid(pgl_ref[...])
        pe_v = pe_ref[...]
        h4 = h_ref[...] + pg * pe_v
        r = lax.rsqrt(jnp.mean(h4 * h4, axis=-1, keepdims=True) + EPS)
        gv = g_ref[...]
        err = h4 * r * gv - t_ref[...]
        part_loss = 0.5 * jnp.sum(jnp.mean(err * err, axis=-1, keepdims=True), axis=0, keepdims=True)
        dy = err * (1.0 / d)
        part_g = jnp.sum(dy * h4 * r, axis=0, keepdims=True)
        dyg = dy * gv
        dh = r * dyg - h4 * (r * r * r) * jnp.mean(dyg * h4, axis=-1, keepdims=True)
        dh_ref[...] = dh
        dsg_ref[...] = (dh * pe_v * pg * (1.0 - pg)).astype(BF16)
        dpe_ref[...] = (dh * pg).astype(BF16)

        @pl.when(pl.program_id(0) == 0)
        def _():
            loss_ref[...] = jnp.broadcast_to(part_loss, (1, LANES))
            dg_ref[...] = part_g

        @pl.when(pl.program_id(0) > 0)
        def _():
            loss_ref[...] += jnp.broadcast_to(part_loss, (1, LANES))
            dg_ref[...] += part_g

    return _pcall(body, name="ple_final",
                  out_shape=(jax.ShapeDtypeStruct((t, d), F32), jax.ShapeDtypeStruct((t, d), BF16),
                             jax.ShapeDtypeStruct((t, d), BF16), jax.ShapeDtypeStruct((1, LANES), F32),
                             jax.ShapeDtypeStruct((1, d), F32)),
                  grid=(t // tt,), in_specs=[row, row, row, vec, row], out_specs=(row, row, row, lvec, vec),
                  dims=("arbitrary",))(h3, pgl, pe, gain, target)


def _adamw_math(w, g, m, v):
    m = ADAM_B1 * m + (1.0 - ADAM_B1) * g
    v = ADAM_B2 * v + (1.0 - ADAM_B2) * (g * g)
    m_hat = m / (1.0 - ADAM_B1 ** ADAM_STEP)
    v_hat = v / (1.0 - ADAM_B2 ** ADAM_STEP)
    delta = -ADAM_LR * (m_hat / (jnp.sqrt(v_hat) + ADAM_EPS) + ADAM_WD * w)
    return delta, m, v


def _adamw(parts, w, m, v, name):
    n, r, c = parts.shape
    tr = _tile(r, 256, 16)
    row = pl.BlockSpec((tr, c), lambda i: (i, 0))

    def body(p_ref, w_ref, m_ref, v_ref, g_ref, d_ref, mo_ref, vo_ref):
        g = p_ref[0].astype(F32)
        for s in range(1, n):
            g = g + p_ref[s].astype(F32)
        g_ref[...] = g
        d_ref[...], mo_ref[...], vo_ref[...] = _adamw_math(w_ref[...], g, m_ref[...], v_ref[...])

    return _pcall(body, name=name, out_shape=(jax.ShapeDtypeStruct((r, c), F32),) * 4, grid=(r // tr,),
                  in_specs=[pl.BlockSpec((n, tr, c), lambda i: (0, i, 0)), row, row, row], out_specs=(row,) * 4,
                  dims=("parallel",))(parts, w, m, v)


ANY = pl.BlockSpec(memory_space=pl.ANY)


def _all_gather(shards):
    n = len(shards)

    def body(*refs):
        x_refs, out_refs = refs[:n], refs[n:2 * n]
        send_sems, recv_sems, local_sems = refs[2 * n:]
        x, y, cc = lax.axis_index("x"), lax.axis_index("y"), lax.axis_index("c")
        me, sibling = (x, y, cc), (x, y, 1 - cc)
        chips = [(1 - x, y), (x, 1 - y), (1 - x, 1 - y)]

        def slot(a, px, py, pc):
            return out_refs[a].at[4 * px + 2 * py + pc]

        def copy(a, k, block, to, src=None):
            return pltpu.make_async_remote_copy(
                src_ref=slot(a, *block) if src is None else src, dst_ref=slot(a, *block),
                send_sem=send_sems.at[7 * a + k], recv_sem=recv_sems.at[7 * a + k], device_id=to, device_id_type=MESH)

        local, sent = [], []
        for a in range(n):
            local.append(pltpu.make_async_copy(x_refs[a], slot(a, *me), local_sems.at[a]))
            sent.append(copy(a, 0, me, sibling, src=x_refs[a]))
            sent += [copy(a, 1 + j, me, (*chip, cc), src=x_refs[a]) for j, chip in enumerate(chips)]
        for cp in local + sent:
            cp.start()
        for j, chip in enumerate(chips):
            for a in range(n):
                copy(a, 1 + j, (*chip, cc), me).wait_recv()
                sent.append(copy(a, 4 + j, (*chip, cc), sibling))
                sent[-1].start()
        for a in range(n):
            copy(a, 0, sibling, me).wait_recv()
            for j, chip in enumerate(chips):
                copy(a, 4 + j, (*chip, 1 - cc), me).wait_recv()
        for cp in sent:
            cp.wait_send()
        for cp in local:
            cp.wait()

    return pl.pallas_call(
        body, name="weights_all_gather",
        out_shape=tuple(jax.ShapeDtypeStruct((N_DEV,) + s.shape, s.dtype) for s in shards),
        in_specs=[ANY] * n, out_specs=(ANY,) * n,
        scratch_shapes=[pltpu.SemaphoreType.DMA((7 * n,)), pltpu.SemaphoreType.DMA((7 * n,)),
                        pltpu.SemaphoreType.DMA((n,))],
    )(*shards)


def _reduce_scatter_exchange(blocks):
    n = len(blocks)

    def body(*refs):
        g_refs, recv_refs = refs[:n], refs[n:2 * n]
        send_sems, recv_sems, local_sems = refs[2 * n:]
        x, y, cc = lax.axis_index("x"), lax.axis_index("y"), lax.axis_index("c")
        me = 4 * x + 2 * y + cc
        local, sent, landing = [], [], []
        for a in range(n):
            local.append(pltpu.make_async_copy(g_refs[a].at[me], recv_refs[a].at[me], local_sems.at[a]))
        for k in range(1, N_DEV):
            px, py, pc = x ^ (k >> 2), y ^ ((k >> 1) & 1), cc ^ (k & 1)
            peer = 4 * px + 2 * py + pc
            for a in range(n):
                sems = dict(send_sem=send_sems.at[7 * a + k - 1], recv_sem=recv_sems.at[7 * a + k - 1],
                            device_id=(px, py, pc), device_id_type=MESH)
                sent.append(pltpu.make_async_remote_copy(src_ref=g_refs[a].at[peer], dst_ref=recv_refs[a].at[me], **sems))
                landing.append(pltpu.make_async_remote_copy(src_ref=g_refs[a].at[me], dst_ref=recv_refs[a].at[peer], **sems))
        for cp in local + sent:
            cp.start()
        for cp in landing:
            cp.wait_recv()
        for cp in sent:
            cp.wait_send()
        for cp in local:
            cp.wait()

    return pl.pallas_call(
        body, name="grads_reduce_scatter_exchange",
        out_shape=tuple(jax.ShapeDtypeStruct(b.shape, b.dtype) for b in blocks),
        in_specs=[ANY] * n, out_specs=(ANY,) * n,
        scratch_shapes=[pltpu.SemaphoreType.DMA((7 * n,)), pltpu.SemaphoreType.DMA((7 * n,)),
                        pltpu.SemaphoreType.DMA((n,))],
    )(*blocks)


HBM = pl.BlockSpec(memory_space=pltpu.HBM)
SEM = pl.BlockSpec(memory_space=pltpu.SEMAPHORE)
DATAFLOW = pltpu.SideEffectType.DATAFLOW_SIDE_EFFECTING


def _peers():
    x, y, cc = lax.axis_index("x"), lax.axis_index("y"), lax.axis_index("c")
    out = []
    for k in range(1, N_DEV):
        px, py, pc = x ^ (k >> 2), y ^ ((k >> 1) & 1), cc ^ (k & 1)
        out.append((k, (px, py, pc), 4 * px + 2 * py + pc))
    return 4 * x + 2 * y + cc, out


def _scatter_start(blocks, after, name, gather=False):
    n = len(blocks)
    lands = [lax.empty((N_DEV,) + b.shape if gather else b.shape, b.dtype) for b in blocks]

    def body(*refs):
        g_refs, land_refs = refs[:n], refs[n:2 * n]
        send_sems, recv_sems, token = refs[2 * n + 1], refs[2 * n + 2], refs[-1]
        me, peers = _peers()
        for k, peer, slot in peers:
            for a in range(n):
                pltpu.make_async_remote_copy(
                    src_ref=g_refs[a] if gather else g_refs[a].at[slot], dst_ref=land_refs[a].at[me],
                    send_sem=send_sems.at[7 * a + k - 1],
                    recv_sem=recv_sems.at[7 * a + k - 1], device_id=peer, device_id_type=MESH).start()
        token[...] = jnp.zeros_like(token)

    thru = [pltpu.HBM(b.shape, b.dtype) for b in blocks]
    thru_lands = [pltpu.HBM(b.shape, b.dtype) for b in lands]
    return pl.pallas_call(
        body, name=name,
        out_shape=(pltpu.SemaphoreType.DMA((7 * n,)), pltpu.SemaphoreType.DMA((7 * n,)), *thru, *thru_lands,
                   jax.ShapeDtypeStruct((8, LANES), F32)),
        in_specs=[HBM] * (2 * n) + [pl.BlockSpec(memory_space=pl.ANY)],
        out_specs=(SEM, SEM, *[HBM] * (2 * n), pl.BlockSpec(memory_space=pltpu.VMEM)),
        input_output_aliases={i: 2 + i for i in range(2 * n)},
        compiler_params=pltpu.CompilerParams(has_side_effects=DATAFLOW),
    )(*[pltpu.with_memory_space_constraint(a, pltpu.HBM) for a in list(blocks) + lands], after)


def _scatter_wait(started, after, name, gather=False):
    send_sems, recv_sems, *rest = started
    n = (len(rest) - 1) // 2
    thru = rest[:2 * n]

    def body(*refs):
        g_refs, land_refs = refs[:n], refs[n:2 * n]
        send_sems, recv_sems = refs[2 * n], refs[2 * n + 1]
        me, peers = _peers()
        for k, peer, slot in peers:
            for a in range(n):
                copy = pltpu.make_async_remote_copy(
                    src_ref=g_refs[a] if gather else g_refs[a].at[slot], dst_ref=land_refs[a].at[slot],
                    send_sem=send_sems.at[7 * a + k - 1],
                    recv_sem=recv_sems.at[7 * a + k - 1], device_id=peer, device_id_type=MESH)
                copy.wait_send()
                copy.wait_recv()

    out = pl.pallas_call(
        body, name=name, out_shape=tuple(pltpu.HBM(a.shape, a.dtype) for a in thru),
        in_specs=[HBM] * (2 * n) + [SEM, SEM, pl.BlockSpec(memory_space=pl.ANY)], out_specs=tuple([HBM] * (2 * n)),
        input_output_aliases={i: i for i in range(2 * n)},
        compiler_params=pltpu.CompilerParams(has_side_effects=DATAFLOW),
    )(*thru, send_sems, recv_sems, after)
    return out[:n], out[n:]


BIG = (("w_ffn1_gate", "colT"), ("w_ffn1_up", "colT"), ("w_ffn1_down", "row"), ("w_ffn2_gate", "colT"),
       ("w_ffn2_up", "colT"), ("w_ffn2_down", "row"), ("w_in", "colT"), ("w_merge", "col"), ("w_ret_out", "col"),
       ("w_fox_out", "col"), ("w_out", "row"), ("w_ple", "col"), ("w_ple_gate", "row"))


def _shard_view(a, kind):
    a = a.reshape(a.shape[-2:])
    return a.T if kind == "colT" else a


def _unview(a, kind, shape):
    return (a.T if kind == "colT" else a).reshape(shape)


def _full_from_slots(g, kind):
    n, r, c = g.shape
    return g.transpose(1, 0, 2).reshape(r, n * c) if kind == "col" else g.reshape(n * r, c)


def _slots_from_full(f, kind):
    r, c = f.shape
    return f.reshape(r, N_DEV, c // N_DEV).transpose(1, 0, 2) if kind == "col" else f.reshape(N_DEV, r // N_DEV, c)


EARLY_GROUPS = (("w_ple_gate", "w_ple", "w_ffn2_down", "w_ffn2_gate", "w_ffn2_up", "w_out", "w_ret_out", "w_fox_out"),
                ("w_in", "w_merge"))


def _scatter_group(gw, names, after, name):
    kind = dict(BIG)
    return names, _scatter_start([_slots_from_full(gw[n], kind[n]) for n in names], after, name)


def _pad_heads(w):
    d = w.shape[1]
    return jnp.pad(w.reshape(FOX_HEADS, FOX_DIM, d), ((0, 0), (0, LANES - FOX_DIM), (0, 0))).reshape(FOX_TILES, d)


def _unpad_heads(w):
    d = w.shape[1]
    return w.reshape(FOX_HEADS, LANES, d)[:, :FOX_DIM].reshape(FOX_WIDTH, d)


def _deinterleave_rows(w):
    d = w.shape[1]
    return w.reshape(RET_HEADS, RET_DIM // 2, 2, d).transpose(0, 2, 1, 3).reshape(RET_WIDTH, d)


def _interleave_rows(w):
    d = w.shape[1]
    return w.reshape(RET_HEADS, 2, RET_DIM // 2, d).transpose(0, 2, 1, 3).reshape(RET_WIDTH, d)


def _pad_w_in(wt):
    d = wt.shape[1]
    rw, fw = RET_WIDTH, FOX_WIDTH
    fo = 4 * rw
    return jnp.concatenate([
        _deinterleave_rows(wt[:rw]), _deinterleave_rows(wt[rw:2 * rw]), wt[2 * rw:4 * rw],
        _pad_heads(wt[fo:fo + fw]), _pad_heads(wt[fo + fw:fo + 2 * fw]), _pad_heads(wt[fo + 2 * fw:fo + 3 * fw]),
        wt[fo + 3 * fw:], jnp.zeros((2 * LANES - FOX_HEADS, d), wt.dtype)], axis=0)


def _unpad_w_in(g):
    rw = RET_WIDTH
    f0 = 4 * rw
    return jnp.concatenate([
        _interleave_rows(g[:rw]), _interleave_rows(g[rw:2 * rw]), g[2 * rw:4 * rw],
        _unpad_heads(g[f0:f0 + FOX_TILES]), _unpad_heads(g[f0 + FOX_TILES:f0 + 2 * FOX_TILES]),
        _unpad_heads(g[f0 + 2 * FOX_TILES:f0 + 3 * FOX_TILES]),
        g[f0 + 3 * FOX_TILES:f0 + 3 * FOX_TILES + FOX_HEADS]], axis=0)


SMALL = ("ln_ffn1", "ln_mix", "b_forget", "b_merge", "ln_ffn2", "ln_ple", "ln_final")


def _small_rows(n):
    rows = -(-n // LANES)
    return -(-rows // 8) * 8


def _pack_small(vals, with_loss=None):
    parts = []
    for name in SMALL:
        v = vals[name].reshape(-1).astype(F32)
        rows = _small_rows(v.shape[0])
        parts.append(jnp.pad(v, (0, rows * LANES - v.shape[0])).reshape(rows, LANES))
    if with_loss is not None:
        parts.append(jnp.pad(with_loss.reshape(1, LANES), ((0, 7), (0, 0))))
    else:
        parts.append(jnp.zeros((8, LANES), F32))
    return jnp.concatenate(parts, axis=0)


def _unpack_small(packed, shapes):
    out, at = {}, 0
    for name in SMALL:
        n = int(np.prod(shapes[name]))
        rows = _small_rows(n)
        out[name] = packed[at:at + rows].reshape(-1)[:n].reshape(shapes[name])
        at += rows
    return out, packed[at, 0]


def _gather_finish(group, after, me):
    names, started, wait_name = group
    kind = dict(BIG)
    sent, landed = _scatter_wait(started, after, wait_name, gather=True)
    return {n: _full_from_slots(lax.dynamic_update_slice_in_dim(land, shard[None], me, 0), kind[n])
            for n, shard, land in zip(names, sent, landed)}


def _local_step(x, p, positions, target, w, small, me, entry_token, gathers):
    t, d = x.shape
    gain = lambda n: small[n].reshape(1, d)
    w = dict(w)
    bpad = jnp.pad(small["b_forget"].reshape(1, FOX_HEADS), ((0, 0), (0, LANES - FOX_HEADS)))
    bm = small["b_merge"].reshape(1, 2 * d)
    fox_blk = _tile(t, 1024, 128)

    def ffn_fwd(n, tag, down_gather=None):
        g = _mm([(n, w[f"w_{tag}_gate"], "nt")], BF16, f"{tag}_gate", tn=1408)
        u = _mm([(n, w[f"w_{tag}_up"], "nt")], BF16, f"{tag}_up", tn=1408)
        a = _swiglu_fwd(g, u, f"{tag}_swiglu")
        if down_gather is not None:
            w.update(_gather_finish(down_gather, a, me))
        return g, u, a, _mm([(a, w[f"w_{tag}_down"], "nn")], F32, f"{tag}_down")

    n1 = _rms_fwd(x, gain("ln_ffn1") + entry_token, "rms_ffn1")
    g1, u1, a1, f1 = ffn_fwd(n1, "ffn1", down_gather=gathers[0])
    h1, u = _rms_fwd(x, gain("ln_mix"), "rms_mix", f=f1)
    w.update(_gather_finish(gathers[1], f1, me))
    w_in_t = _pad_w_in(w["w_in"])
    pm = _mm([(u, w_in_t, "nt")], F32, "mixer_in", tn=1792)
    w.update(_gather_finish(gathers[2], pm, me))
    gm = _mm([(u, w["w_merge"], "nn")], F32, "mixer_gates")

    half = jnp.arange(RET_DIM // 2, dtype=F32) / (RET_DIM // 2)
    inv = 1.0 / (ROPE_BASE ** half)
    inv2 = jnp.concatenate([inv, inv]).reshape(1, RET_DIM)
    sign2 = jnp.concatenate([-jnp.ones((RET_DIM // 2,), F32), jnp.ones((RET_DIM // 2,), F32)]).reshape(1, RET_DIM)
    cos2, sin2 = _rope_tables(positions.reshape(t, 1), inv2, sign2)
    consts = _ret_consts()
    y_ret, y_raw, states = _ret_fwd(pm, cos2, sin2, consts)
    w.update(_gather_finish(gathers[3], y_raw, me))
    w_fox_pad = _pad_heads(w["w_fox_out"])
    za = _mm([(y_ret, w["w_ret_out"], "nn")], F32, "ret_out")

    qa, ka, va = _fox_prep(pm, bpad)
    o_fox, y_fox, qa_b = _fox_fwd(qa, ka, va, fox_blk)
    zb = _mm([(y_fox, w_fox_pad, "nn")], F32, "fox_out")

    mix = _merge_fwd(gm, bm, za, zb)
    mo = _mm([(mix, w["w_out"], "nn")], F32, "mix_out")
    h2, n2 = _rms_fwd(h1, gain("ln_ffn2"), "rms_ffn2", f=mo, scale=1.0)
    w.update(_gather_finish(gathers[4], mo, me))
    g2, u2, a2, f2 = ffn_fwd(n2, "ffn2")
    h3, n3 = _rms_fwd(h2, gain("ln_ple"), "rms_ple", f=f2)
    pgl = _mm([(n3, w["w_ple_gate"], "nn")], F32, "ple_gate")
    pb = p.astype(BF16)
    pe = _mm([(pb, w["w_ple"], "nn")], F32, "ple_embed")

    gw, gs = {}, {}
    dh4, dsg, dpe, loss, gs["ln_final"] = _ple_final(h3, pgl, pe, gain("ln_final"), target)
    gw["w_ple_gate"] = _mm([(n3, dsg, "tn")], BF16, "d_w_ple_gate", tn=256)
    gw["w_ple"] = _mm([(pb, dpe, "tn")], BF16, "d_w_ple", tn=256)
    dn3 = _mm([(dsg, w["w_ple_gate"], "nt")], F32, "d_n3")
    dh3, dh3_half, gs["ln_ple"] = _rms_bwd(dn3, h3, gain("ln_ple"), dh4, "rms_ple_bwd", 0.5)

    def ffn_bwd(dh_half, g, u_, a, n, tag, scatter_now=None):
        gw[f"w_{tag}_down"] = _mm([(a, dh_half, "tn")], BF16, f"d_w_{tag}_down", tm=1408, tn=256)
        start = scatter_now if scatter_now is not None else (lambda *_: None)
        token = start((f"w_{tag}_down",), dh_half, "c")
        da = _mm([(dh_half, w[f"w_{tag}_down"], "nt")], BF16, f"d_a_{tag}", tn=1408, after=token)
        dg, du_ = _swiglu_bwd(da, g, u_, f"{tag}_swiglu_bwd")
        gw[f"w_{tag}_gate"] = _mm([(dg, n, "tn")], BF16, f"d_w_{tag}_gate", tm=1408, tn=256)
        token = start((f"w_{tag}_gate",), da, "d")
        gw[f"w_{tag}_up"] = _mm([(du_, n, "tn")], BF16, f"d_w_{tag}_up", tm=1408, tn=256, after=token)
        token = start((f"w_{tag}_up",), da, "e")
        return _mm([(dg, w[f"w_{tag}_gate"], "nn"), (du_, w[f"w_{tag}_up"], "nn")], F32, f"d_n_{tag}", tm=512,
                   after=token)

    dn2 = ffn_bwd(dh3_half, g2, u2, a2, n2, "ffn2")
    dh2, dh2_b, gs["ln_ffn2"] = _rms_bwd(dn2, h2, gain("ln_ffn2"), dh3, "rms_ffn2_bwd", 1.0)

    gw["w_out"] = _mm([(mix, dh2_b, "tn")], BF16, "d_w_out", tn=256)
    dmix = _mm([(dh2_b, w["w_out"], "nt")], F32, "d_mix")
    dza, dzb, dgm, gs["b_merge"] = _merge_bwd(dmix, gm, bm, za, zb)
    gw["w_ret_out"] = _mm([(y_ret, dza, "tn")], BF16, "d_w_ret_out", tn=256)
    gw["w_fox_out"] = _unpad_heads(_mm([(y_fox, dzb, "tn")], BF16, "d_w_fox_out", tn=256))
    dy_ret = _mm([(dza, w["w_ret_out"], "nt")], F32, "d_y_ret")
    do_fox = _mm([(dzb, w_fox_pad, "nt")], F32, "d_y_fox")

    pending = [_scatter_group(gw, EARLY_GROUPS[0], dy_ret, "grads_scatter_a_start")]
    token = pending[0][1][-1][0, 0]
    drq, drk, drv, drg = _ret_bwd(dy_ret, pm, cos2, sin2, y_raw, states, consts[:3] + (consts[3] + token,))

    dqa, dka, dva, ds_rows, ds_cols = _fox_bwd(qa_b, ka, va, do_fox, o_fox, fox_blk)
    dc = jnp.pad((ds_rows + ds_cols).reshape(FOX_HEADS, t).T, ((0, 0), (0, LANES - FOX_HEADS)))
    dff, db_forget = _fox_post(dc, pm, bpad)
    gs["b_forget"] = db_forget[:, :FOX_HEADS]

    dpm = jnp.concatenate([drq, drk, drv, drg, dqa, dka, dva, dff, jnp.zeros((t, LANES), BF16)], axis=1)
    gw["w_merge"] = _mm([(u, dgm, "tn")], BF16, "d_w_merge", tn=512)
    gw["w_in"] = _unpad_w_in(_mm([(dpm, u, "tn")], BF16, "d_w_in", tm=1792, tn=256))
    du = _mm([(dpm, w_in_t, "nn"), (dgm, w["w_merge"], "nt")], F32, "d_u", tm=1024, tn=512)
    pending.append(_scatter_group(gw, EARLY_GROUPS[1], du, "grads_scatter_b_start"))
    token = pending[1][1][-1][0:1, 0:1]
    dh1, dh1_half, gs["ln_mix"] = _rms_bwd(du, h1, gain("ln_mix") + token, dh2, "rms_mix_bwd", 0.5)

    def scatter_now(names, after, tag):
        pending.append(_scatter_group(gw, names, after, f"grads_scatter_{tag}_start"))
        return pending[-1][1][-1]

    dn1 = ffn_bwd(dh1_half, g1, u1, a1, n1, "ffn1", scatter_now=scatter_now)
    dx, _, gs["ln_ffn1"] = _rms_bwd(dn1, x, gain("ln_ffn1"), dh1, "rms_ffn1_bwd", 1.0)
    return loss, dx, gw, gs, pending


WEIGHTS = ("ln_ffn1", "w_ffn1_gate", "w_ffn1_up", "w_ffn1_down", "ln_mix", "w_in", "b_forget", "w_merge", "b_merge",
           "w_ret_out", "w_fox_out", "w_out", "ln_ffn2", "w_ffn2_gate", "w_ffn2_up", "w_ffn2_down", "ln_ple", "w_ple",
           "w_ple_gate", "ln_final")


def kernel(x, p, positions, ln_ffn1, w_ffn1_gate, w_ffn1_up, w_ffn1_down, ln_mix, w_in, b_forget, w_merge, b_merge, w_ret_out, w_fox_out, w_out, ln_ffn2, w_ffn2_gate, w_ffn2_up, w_ffn2_down, ln_ple, w_ple, w_ple_gate, ln_final, loss_target, m_ln_ffn1, m_w_ffn1_gate, m_w_ffn1_up, m_w_ffn1_down, m_ln_mix, m_w_in, m_b_forget, m_w_merge, m_b_merge, m_w_ret_out, m_w_fox_out, m_w_out, m_ln_ffn2, m_w_ffn2_gate, m_w_ffn2_up, m_w_ffn2_down, m_ln_ple, m_w_ple, m_w_ple_gate, m_ln_final, v_ln_ffn1, v_w_ffn1_gate, v_w_ffn1_up, v_w_ffn1_down, v_ln_mix, v_w_in, v_b_forget, v_w_merge, v_b_merge, v_w_ret_out, v_w_fox_out, v_w_out, v_ln_ffn2, v_w_ffn2_gate, v_w_ffn2_up, v_w_ffn2_down, v_ln_ple, v_w_ple, v_w_ple_gate, v_ln_final):
    args = dict(ln_ffn1=ln_ffn1, w_ffn1_gate=w_ffn1_gate, w_ffn1_up=w_ffn1_up, w_ffn1_down=w_ffn1_down, ln_mix=ln_mix, w_in=w_in, b_forget=b_forget, w_merge=w_merge, b_merge=b_merge, w_ret_out=w_ret_out, w_fox_out=w_fox_out, w_out=w_out, ln_ffn2=ln_ffn2, w_ffn2_gate=w_ffn2_gate, w_ffn2_up=w_ffn2_up, w_ffn2_down=w_ffn2_down, ln_ple=ln_ple, w_ple=w_ple, w_ple_gate=w_ple_gate, ln_final=ln_final)
    moms = dict(ln_ffn1=m_ln_ffn1, w_ffn1_gate=m_w_ffn1_gate, w_ffn1_up=m_w_ffn1_up, w_ffn1_down=m_w_ffn1_down, ln_mix=m_ln_mix, w_in=m_w_in, b_forget=m_b_forget, w_merge=m_w_merge, b_merge=m_b_merge, w_ret_out=m_w_ret_out, w_fox_out=m_w_fox_out, w_out=m_w_out, ln_ffn2=m_ln_ffn2, w_ffn2_gate=m_w_ffn2_gate, w_ffn2_up=m_w_ffn2_up, w_ffn2_down=m_w_ffn2_down, ln_ple=m_ln_ple, w_ple=m_w_ple, w_ple_gate=m_w_ple_gate, ln_final=m_ln_final)
    vars_ = dict(ln_ffn1=v_ln_ffn1, w_ffn1_gate=v_w_ffn1_gate, w_ffn1_up=v_w_ffn1_up, w_ffn1_down=v_w_ffn1_down, ln_mix=v_ln_mix, w_in=v_w_in, b_forget=v_b_forget, w_merge=v_w_merge, b_merge=v_b_merge, w_ret_out=v_w_ret_out, w_fox_out=v_w_fox_out, w_out=v_w_out, ln_ffn2=v_ln_ffn2, w_ffn2_gate=v_w_ffn2_gate, w_ffn2_up=v_w_ffn2_up, w_ffn2_down=v_w_ffn2_down, ln_ple=v_ln_ple, w_ple=v_w_ple, w_ple_gate=v_w_ple_gate, ln_final=v_ln_final)
    kinds = ("grad", "delta", "new_m", "new_v")

    me = 4 * lax.axis_index("x") + 2 * lax.axis_index("y") + lax.axis_index("c")
    kind_of = dict(BIG)
    shard = {n: _shard_view(args[n], kind).astype(BF16) for n, kind in BIG}
    first = ("w_ffn1_gate", "w_ffn1_up")
    groups = (("w_ffn1_down",), ("w_in",), ("w_merge",), ("w_ret_out", "w_fox_out", "w_out"))
    groups += (tuple(n for n, _ in BIG if n not in first + sum(groups, ())),)
    gathered = _all_gather([shard[n] for n in first])
    w_full = {n: _full_from_slots(g, kind_of[n]) for n, g in zip(first, gathered)}
    gathers, after = [], gathered[0]
    for tag, names in zip("zambc", groups):
        started = _scatter_start([shard[n] for n in names], after, f"weights_gather_{tag}_start", gather=True)
        gathers.append((names, started, f"weights_gather_{tag}_wait"))
        after = started[-1]

    small = {n: args[n] for n in SMALL}
    loss_part, dx, gw, gs, pending = _local_step(x[0], p[0, 0], positions[0], loss_target[0], w_full, small, me,
                                                 after[0:1, 0:1], gathers)

    parts_of = {}
    for tag, (names, started) in zip("abcde", pending):
        sent, landed = _scatter_wait(started, dx, f"grads_scatter_{tag}_wait")
        for n, blk, land in zip(names, sent, landed):
            own = lax.dynamic_index_in_dim(blk, me, 0, keepdims=True)
            parts_of[n] = lax.dynamic_update_slice_in_dim(land, own, me, 0)
    late = [(n, kind) for n, kind in BIG if n not in parts_of]
    small_part = _pack_small(gs, with_loss=loss_part)
    blocks = [_slots_from_full(gw[n], kind) for n, kind in late]
    blocks.append(jnp.broadcast_to(small_part, (N_DEV,) + small_part.shape))
    recv = _reduce_scatter_exchange(blocks)
    parts_of.update({n: r for (n, _), r in zip(late, recv)})

    res = {}
    for n, kind in BIG:
        parts = parts_of[n]
        outs = _adamw(parts, _shard_view(args[n], kind), _shard_view(moms[n], kind), _shard_view(vars_[n], kind),
                      f"adamw_{n}")
        for what, o in zip(kinds, outs):
            res[(what, n)] = _unview(o, kind, args[n].shape)
    s_outs = _adamw(recv[-1], _pack_small(small), _pack_small({n: moms[n] for n in SMALL}),
                    _pack_small({n: vars_[n] for n in SMALL}), "adamw_small")
    for what, sm in zip(kinds, s_outs):
        svals, extra = _unpack_small(sm, {n: args[n].shape for n in SMALL})
        if what == "grad":
            loss = extra
        for n in SMALL:
            res[(what, n)] = svals[n]
    return (loss, dx[None], *[res[(what, n)] for what in kinds for n in WEIGHTS])
```

```python
import numpy as np
import jax
import jax.numpy as jnp
from jax import lax
from jax.experimental import pallas as pl
from jax.experimental.pallas import tpu as pltpu

F32 = jnp.float32
BF16 = jnp.bfloat16

N_DEV = 8
EPS = 1e-6
RET_HEADS = 4
RET_DIM = 128
RET_WIDTH = RET_HEADS * RET_DIM
FOX_HEADS = 8
FOX_DIM = 64
FOX_WIDTH = FOX_HEADS * FOX_DIM
CHUNK = 128
ROPE_BASE = 10000.0
LANES = 128
FOX_TILES = FOX_HEADS * LANES
IN_COLS = 4 * RET_WIDTH + 3 * FOX_WIDTH + FOX_HEADS
IN_PAD = 4 * RET_WIDTH + 3 * FOX_TILES + 2 * LANES
TILE_RQ, TILE_RK, TILE_RV, TILE_RG = 0, 4, 8, 12
TILE_FQ, TILE_FK, TILE_FV, TILE_FF = 16, 24, 32, 40
NEG = -1e30

ADAM_LR = 0.001
ADAM_B1 = 0.9
ADAM_B2 = 0.999
ADAM_EPS = 1e-08
ADAM_WD = 0.01
ADAM_STEP = 10

VMEM_LIMIT_BYTES = 56 * 1024 * 1024

MESH = pl.DeviceIdType.MESH


def _tile(dim, pref, mult):
    if dim <= pref:
        return dim
    t = (pref // mult) * mult
    while t >= mult:
        if dim % t == 0:
            return t
        t -= mult
    return dim


def _params(dims):
    return pltpu.CompilerParams(dimension_semantics=dims, vmem_limit_bytes=VMEM_LIMIT_BYTES)


def _pcall(body, *, name, out_shape, grid, in_specs, out_specs, scratch_shapes=(), dims=None):
    return pl.pallas_call(body, name=name, out_shape=out_shape, grid=grid, in_specs=in_specs, out_specs=out_specs,
                          scratch_shapes=list(scratch_shapes), compiler_params=_params(dims))


def _dot(a, b, ca, cb):
    return lax.dot_general(a, b, (((ca,), (cb,)), ((), ())), preferred_element_type=F32)


def _sigmoid(x):
    return 1.0 / (1.0 + jnp.exp(-x))


def _mm(pairs, out_dtype, name, tm=1024, tn=1024, after=None):
    dims = []
    for a, b, mode in pairs:
        m, k = (a.shape[1], a.shape[0]) if mode == "tn" else a.shape
        n, k2 = b.shape if mode == "nt" else (b.shape[1], b.shape[0])
        assert k == k2, (name, a.shape, b.shape, mode)
        dims.append((m, n))
    assert all(d == dims[0] for d in dims), (name, dims)
    m, n = dims[0]
    tm = _tile(m, tm, 128 if any(mode == "tn" for _, _, mode in pairs) else 16)
    tn = _tile(n, tn, 128)
    in_specs, contract, operands = [], [], []
    for a, b, mode in pairs:
        k = a.shape[0] if mode == "tn" else a.shape[1]
        in_specs.append(pl.BlockSpec((k, tm), lambda i, j: (0, i)) if mode == "tn" else
                        pl.BlockSpec((tm, k), lambda i, j: (i, 0)))
        in_specs.append(pl.BlockSpec((tn, k), lambda i, j: (j, 0)) if mode == "nt" else
                        pl.BlockSpec((k, tn), lambda i, j: (0, j)))
        contract.append((0 if mode == "tn" else 1, 1 if mode == "nt" else 0))
        operands += [a, b]
    if after is not None:
        in_specs.append(pl.BlockSpec(memory_space=pl.ANY))
        operands.append(after)

    def body(*refs):
        o_ref = refs[-1]
        acc = None
        for p, (ca, cb) in enumerate(contract):
            part = _dot(refs[2 * p][...], refs[2 * p + 1][...], ca, cb)
            acc = part if acc is None else acc + part
        o_ref[...] = acc.astype(out_dtype)

    return _pcall(body, name=name, out_shape=jax.ShapeDtypeStruct((m, n), out_dtype), grid=(m // tm, n // tn),
                  in_specs=in_specs, out_specs=pl.BlockSpec((tm, tn), lambda i, j: (i, j)),
                  dims=("parallel", "parallel"))(*operands)


def _rms_fwd(h, gain, name, f=None, scale=0.5):
    t, d = h.shape
    tt = _tile(t, 1024, 16)
    row = pl.BlockSpec((tt, d), lambda i: (i, 0))
    vec = pl.BlockSpec((1, d), lambda i: (0, 0))

    def norm(hv, g_ref, n_ref):
        r = lax.rsqrt(jnp.mean(hv * hv, axis=-1, keepdims=True) + EPS)
        n_ref[...] = (hv * r * g_ref[...]).astype(BF16)

    if f is None:

        def body(h_ref, g_ref, n_ref):
            norm(h_ref[...], g_ref, n_ref)

        return _pcall(body, name=name, out_shape=jax.ShapeDtypeStruct((t, d), BF16), grid=(t // tt,),
                      in_specs=[row, vec], out_specs=row, dims=("parallel",))(h, gain)

    def body(h_ref, f_ref, g_ref, hn_ref, n_ref):
        hv = h_ref[...] + scale * f_ref[...]
        hn_ref[...] = hv
        norm(hv, g_ref, n_ref)

    return _pcall(body, name=name,
                  out_shape=(jax.ShapeDtypeStruct((t, d), F32), jax.ShapeDtypeStruct((t, d), BF16)),
                  grid=(t // tt,), in_specs=[row, row, vec], out_specs=(row, row), dims=("parallel",))(h, f, gain)


def _rms_bwd(dn, h, gain, dh_in, name, out_scale):
    t, d = h.shape
    tt = _tile(t, 1024, 16)
    row = pl.BlockSpec((tt, d), lambda i: (i, 0))
    vec = pl.BlockSpec((1, d), lambda i: (0, 0))

    def body(dn_ref, h_ref, g_ref, dhin_ref, dh_ref, dhb_ref, dg_ref):
        hv = h_ref[...]
        dnv = dn_ref[...].astype(F32)
        r = lax.rsqrt(jnp.mean(hv * hv, axis=-1, keepdims=True) + EPS)
        dng = dnv * g_ref[...]
        dh = dhin_ref[...] + r * dng - hv * (r * r * r) * jnp.mean(dng * hv, axis=-1, keepdims=True)
        dh_ref[...] = dh
        dhb_ref[...] = (out_scale * dh).astype(BF16)
        part = jnp.sum(dnv * hv * r, axis=0, keepdims=True)

        @pl.when(pl.program_id(0) == 0)
        def _():
            dg_ref[...] = part

        @pl.when(pl.program_id(0) > 0)
        def _():
            dg_ref[...] += part

    return _pcall(body, name=name,
                  out_shape=(jax.ShapeDtypeStruct((t, d), F32), jax.ShapeDtypeStruct((t, d), BF16),
                             jax.ShapeDtypeStruct((1, d), F32)),
                  grid=(t // tt,), in_specs=[row, row, vec, row], out_specs=(row, row, vec),
                  dims=("arbitrary",))(dn, h, gain, dh_in)


def _swiglu_fwd(g, u, name):
    t, f = g.shape
    tt = _tile(t, 512, 16)
    row = pl.BlockSpec((tt, f), lambda i: (i, 0))

    def body(g_ref, u_ref, a_ref):
        gv = g_ref[...].astype(F32)
        a_ref[...] = (gv * _sigmoid(gv) * u_ref[...].astype(F32)).astype(BF16)

    return _pcall(body, name=name, out_shape=jax.ShapeDtypeStruct((t, f), BF16), grid=(t // tt,),
                  in_specs=[row, row], out_specs=row, dims=("parallel",))(g, u)


def _swiglu_bwd(da, g, u, name):
    t, f = g.shape
    tt = _tile(t, 512, 16)
    row = pl.BlockSpec((tt, f), lambda i: (i, 0))

    def body(da_ref, g_ref, u_ref, dg_ref, du_ref):
        gv = g_ref[...].astype(F32)
        dav = da_ref[...].astype(F32)
        sg = _sigmoid(gv)
        dg_ref[...] = (dav * u_ref[...].astype(F32) * (sg * (1.0 + gv * (1.0 - sg)))).astype(BF16)
        du_ref[...] = (dav * (gv * sg)).astype(BF16)

    return _pcall(body, name=name, out_shape=(jax.ShapeDtypeStruct((t, f), BF16),) * 2, grid=(t // tt,),
                  in_specs=[row, row, row], out_specs=(row, row), dims=("parallel",))(da, g, u)


def _rope_tables(pos_col, inv2, sign2):
    t = pos_col.shape[0]

    def body(p_ref, inv_ref, sg_ref, c_ref, s_ref):
        ang = p_ref[...].astype(F32) * inv_ref[...]
        c_ref[...] = jnp.cos(ang)
        s_ref[...] = jnp.sin(ang) * sg_ref[...]

    full = lambda shape: pl.BlockSpec(shape, lambda i: (0, 0))
    return _pcall(body, name="rope_tables", out_shape=(jax.ShapeDtypeStruct((t, RET_DIM), F32),) * 2, grid=(1,),
                  in_specs=[full((t, 1)), full((1, RET_DIM)), full((1, RET_DIM))],
                  out_specs=(full((t, RET_DIM)),) * 2, dims=("arbitrary",))(pos_col, inv2, sign2)


def _rot(x, c, s):
    return x * c + pltpu.roll(x, RET_DIM // 2, 1) * s


def _rot_t(g, c, s):
    return g * c + pltpu.roll(g * s, RET_DIM // 2, 1)


def _ret_consts():
    hh = np.arange(RET_HEADS, dtype=np.float32)
    log_gamma = np.log1p(-np.exp2(-5.0 - hh)).astype(np.float32)
    idx = np.arange(CHUNK, dtype=np.float32)
    diff = idx[:, None] - idx[None, :]
    dmask = np.where(diff >= 0, np.exp(log_gamma[:, None, None] * np.maximum(diff, 0.0)), 0.0).astype(np.float32)
    kdec = np.exp(log_gamma[:, None] * (CHUNK - 1 - idx)).astype(np.float32)
    qdec = np.exp(log_gamma[:, None] * (idx + 1.0)).astype(np.float32)
    cdec = np.exp(log_gamma * CHUNK).astype(np.float32)
    bc = lambda v: np.ascontiguousarray(np.broadcast_to(v[:, :, None], (RET_HEADS, CHUNK, RET_DIM)))
    cd = np.ascontiguousarray(np.broadcast_to(cdec[:, None, None], (RET_HEADS, 8, RET_DIM)))
    return jnp.asarray(dmask), jnp.asarray(bc(qdec)), jnp.asarray(bc(kdec)), jnp.asarray(cd)


def _ret_fwd(pm, cos2, sin2, consts):
    t = pm.shape[0]
    n_chunks = t // CHUNK
    dmask, qdec, kdec, cd = consts
    scale = RET_DIM ** -0.5
    wide = lambda c0: pl.BlockSpec((CHUNK, RET_WIDTH), lambda n: (n, c0 // RET_HEADS))
    tab = pl.BlockSpec((CHUNK, RET_DIM), lambda n: (n, 0))
    const = lambda a: pl.BlockSpec(a.shape, lambda n: (0,) * a.ndim)

    def body(q_ref, k_ref, v_ref, g_ref, c_ref, s_ref, dm_ref, qd_ref, kd_ref, cd_ref, y_ref, raw_ref, st_ref, s_acc):
        @pl.when(pl.program_id(0) == 0)
        def _():
            s_acc[...] = jnp.zeros_like(s_acc)

        c, s = c_ref[...], s_ref[...]
        for h in range(RET_HEADS):
            hs = slice(h * RET_DIM, (h + 1) * RET_DIM)
            q = _rot(q_ref[:, hs], c, s)
            k = _rot(k_ref[:, hs], c, s) * scale
            vb = v_ref[:, hs].astype(BF16)
            g = g_ref[:, hs]
            s_in = s_acc[h]
            st_ref[h] = s_in
            a = _dot(q.astype(BF16), k.astype(BF16), 1, 1) * dm_ref[h]
            y = _dot(a.astype(BF16), vb, 1, 0) + _dot((q * qd_ref[h]).astype(BF16), s_in.astype(BF16), 1, 0)
            s_acc[h] = cd_ref[h, 0:1, :] * s_in + _dot((k * kd_ref[h]).astype(BF16), vb, 0, 0)
            raw_ref[:, hs] = y
            mu = jnp.mean(y, axis=-1, keepdims=True)
            yc = y - mu
            rs = lax.rsqrt(jnp.mean(yc * yc, axis=-1, keepdims=True) + EPS)
            y_ref[:, hs] = (yc * rs * (g * _sigmoid(g))).astype(BF16)

    out_blk = pl.BlockSpec((CHUNK, RET_WIDTH), lambda n: (n, 0))
    return _pcall(
        body, name="retention_fwd",
        out_shape=(jax.ShapeDtypeStruct((t, RET_WIDTH), BF16), jax.ShapeDtypeStruct((t, RET_WIDTH), F32),
                   jax.ShapeDtypeStruct((RET_HEADS, n_chunks, RET_DIM, RET_DIM), F32)),
        grid=(n_chunks,),
        in_specs=[wide(TILE_RQ), wide(TILE_RK), wide(TILE_RV), wide(TILE_RG), tab, tab,
                  const(dmask), const(qdec), const(kdec), const(cd)],
        out_specs=(out_blk, out_blk, pl.BlockSpec((RET_HEADS, None, RET_DIM, RET_DIM), lambda n: (0, n, 0, 0))),
        scratch_shapes=[pltpu.VMEM((RET_HEADS, RET_DIM, RET_DIM), F32)],
        dims=("arbitrary",),
    )(pm, pm, pm, pm, cos2, sin2, dmask, qdec, kdec, cd)


def _ret_bwd(dy, pm, cos2, sin2, raw, states, consts):
    t = pm.shape[0]
    n_chunks = t // CHUNK
    dmask, qdec, kdec, cd = consts
    scale = RET_DIM ** -0.5
    rev = lambda n: n_chunks - 1 - n
    wide = lambda c0: pl.BlockSpec((CHUNK, RET_WIDTH), lambda n: (rev(n), c0 // RET_HEADS))
    tab = pl.BlockSpec((CHUNK, RET_DIM), lambda n: (rev(n), 0))
    blk = pl.BlockSpec((CHUNK, RET_WIDTH), lambda n: (rev(n), 0))
    const = lambda a: pl.BlockSpec(a.shape, lambda n: (0,) * a.ndim)

    def body(dy_ref, q_ref, k_ref, v_ref, g_ref, c_ref, s_ref, raw_ref, st_ref, dm_ref, qd_ref, kd_ref, cd_ref,
             dq_ref, dk_ref, dv_ref, dg_ref, ds_acc):
        @pl.when(pl.program_id(0) == 0)
        def _():
            ds_acc[...] = jnp.zeros_like(ds_acc)

        c, s = c_ref[...], s_ref[...]
        for h in range(RET_HEADS):
            hs = slice(h * RET_DIM, (h + 1) * RET_DIM)
            q = _rot(q_ref[:, hs], c, s)
            k = _rot(k_ref[:, hs], c, s) * scale
            qb, kb, vb = q.astype(BF16), k.astype(BF16), v_ref[:, hs].astype(BF16)
            g = g_ref[:, hs]
            dm, qd, kd = dm_ref[h], qd_ref[h], kd_ref[h]
            y = raw_ref[:, hs]
            mu = jnp.mean(y, axis=-1, keepdims=True)
            yc = y - mu
            rs = lax.rsqrt(jnp.mean(yc * yc, axis=-1, keepdims=True) + EPS)
            yn = yc * rs
            sg = _sigmoid(g)
            dyo = dy_ref[:, hs]
            dg_ref[:, hs] = (dyo * yn * (sg * (1.0 + g * (1.0 - sg)))).astype(BF16)
            dyn = dyo * (g * sg)
            dyr = rs * (dyn - jnp.mean(dyn, axis=-1, keepdims=True) - yn * jnp.mean(dyn * yn, axis=-1, keepdims=True))
            dyb = dyr.astype(BF16)
            s_in = st_ref[h].astype(BF16)
            ds_out = ds_acc[h]
            dsb = ds_out.astype(BF16)
            a = _dot(qb, kb, 1, 1) * dm
            da = (_dot(dyb, vb, 1, 1) * dm).astype(BF16)
            kdb = (k * kd).astype(BF16)
            qdb = (q * qd).astype(BF16)
            dv_ref[:, hs] = (_dot(a.astype(BF16), dyb, 0, 0) + _dot(kdb, dsb, 1, 0)).astype(BF16)
            dqh = _dot(da, kb, 1, 0) + _dot(dyb, s_in, 1, 1) * qd
            dkh = _dot(da, qb, 0, 0) + _dot(vb, dsb, 1, 1) * kd
            ds_acc[h] = cd_ref[h, 0:1, :] * ds_out + _dot(qdb, dyb, 0, 0)
            dq_ref[:, hs] = _rot_t(dqh, c, s).astype(BF16)
            dk_ref[:, hs] = (_rot_t(dkh, c, s) * scale).astype(BF16)

    return _pcall(
        body, name="retention_bwd",
        out_shape=(jax.ShapeDtypeStruct((t, RET_WIDTH), BF16),) * 4,
        grid=(n_chunks,),
        in_specs=[blk, wide(TILE_RQ), wide(TILE_RK), wide(TILE_RV), wide(TILE_RG), tab, tab, blk,
                  pl.BlockSpec((RET_HEADS, None, RET_DIM, RET_DIM), lambda n: (0, rev(n), 0, 0)),
                  const(dmask), const(qdec), const(kdec), const(cd)],
        out_specs=(blk,) * 4,
        scratch_shapes=[pltpu.VMEM((RET_HEADS, RET_DIM, RET_DIM), F32)],
        dims=("arbitrary",),
    )(dy, pm, pm, pm, pm, cos2, sin2, raw, states, dmask, qdec, kdec, cd)


FOX_C_LANE = FOX_DIM
FOX_NEGC_LANE = FOX_DIM + 3
FOX_LSE_LANE = FOX_DIM + 6
FOX_L_LANE = FOX_C_LANE
FOX_ROWSUM_LANE = FOX_C_LANE
FOX_COLSUM_LANE = FOX_NEGC_LANE


def _split3(x):
    hi = x.astype(BF16)
    r1 = x - hi.astype(F32)
    mid = r1.astype(BF16)
    lo = (r1 - mid.astype(F32)).astype(BF16)
    return hi, mid, lo


def _tri_dot(tri, x):
    hi, mid, lo = _split3(x)
    return _dot(tri, lo, 1, 0) + _dot(tri, mid, 1, 0) + _dot(tri, hi, 1, 0)


def _log_sigmoid(z):
    return jnp.minimum(z, 0.0) - jnp.log1p(jnp.exp(-jnp.abs(z)))


def _fox_consts():
    place = np.zeros((2, 3, LANES, FOX_TILES), np.float32)
    ones = np.zeros((3, 1, FOX_TILES), np.float32)
    for h in range(FOX_HEADS):
        for part in range(3):
            place[0, part, h, LANES * h + FOX_C_LANE + part] = 1.0
            place[1, part, h, LANES * h + FOX_NEGC_LANE + part] = -1.0
            ones[0, 0, LANES * h + FOX_NEGC_LANE + part] = 1.0
            ones[1, 0, LANES * h + FOX_C_LANE + part] = 1.0
            ones[1, 0, LANES * h + FOX_LSE_LANE + part] = 1.0
            ones[2, 0, LANES * h + FOX_C_LANE + part] = 1.0
    return jnp.asarray(place, BF16), jnp.asarray(ones, F32)


def _fox_prep(pm, bpad):
    t = pm.shape[0]
    tt = _tile(t, 512, LANES)
    place, ones = _fox_consts()
    wide = lambda c0: pl.BlockSpec((tt, FOX_TILES), lambda i: (i, c0 // FOX_HEADS))
    const = lambda a: pl.BlockSpec(a.shape, lambda i: (0,) * a.ndim)

    def body(q_ref, k_ref, v_ref, ff_ref, b_ref, pl_ref, on_ref, qa_ref, ka_ref, va_ref, carry_s):
        @pl.when(pl.program_id(0) == 0)
        def _():
            carry_s[...] = jnp.zeros_like(carry_s)

        r = lax.broadcasted_iota(jnp.int32, (LANES, LANES), 0)
        cc = lax.broadcasted_iota(jnp.int32, (LANES, LANES), 1)
        tri = jnp.where(cc <= r, 1.0, 0.0).astype(BF16)
        bias = b_ref[...]
        for sub in range(tt // LANES):
            rows = pl.ds(sub * LANES, LANES)
            cs = _tri_dot(tri, _log_sigmoid(ff_ref[rows, :] + bias)) + carry_s[...]
            carry_s[...] = cs[LANES - 1:LANES, :]
            parts = _split3(cs)
            eq = sum(_dot(part, pl_ref[0, i], 1, 0) for i, part in enumerate(parts))
            ek = sum(_dot(part, pl_ref[1, i], 1, 0) for i, part in enumerate(parts))
            qa_ref[rows, :] = (q_ref[rows, :] * FOX_DIM ** -0.5 + eq + on_ref[0]).astype(BF16)
            ka_ref[rows, :] = (k_ref[rows, :] + ek + on_ref[1]).astype(BF16)
            va_ref[rows, :] = (v_ref[rows, :] + on_ref[2]).astype(BF16)

    out = pl.BlockSpec((tt, FOX_TILES), lambda i: (i, 0))
    return _pcall(body, name="fox_prep", out_shape=(jax.ShapeDtypeStruct((t, FOX_TILES), BF16),) * 3, grid=(t // tt,),
                  in_specs=[wide(TILE_FQ), wide(TILE_FK), wide(TILE_FV), pl.BlockSpec((tt, LANES), lambda i: (i, TILE_FF)),
                            pl.BlockSpec((1, LANES), lambda i: (0, 0)), const(place), const(ones)],
                  out_specs=(out,) * 3, scratch_shapes=[pltpu.VMEM((1, LANES), F32)],
                  dims=("arbitrary",))(pm, pm, pm, pm, bpad, place, ones)


def _fox_post(dc, pm, bpad):
    t = pm.shape[0]
    nb = t // LANES

    def body(dc_ref, ff_ref, b_ref, d_ref, db_ref):
        r = lax.broadcasted_iota(jnp.int32, (LANES, LANES), 0)
        cc = lax.broadcasted_iota(jnp.int32, (LANES, LANES), 1)
        tri = jnp.where(cc >= r, 1.0, 0.0).astype(BF16)
        bias = b_ref[...]

        def step(i, carry):
            tail, acc = carry
            rows = pl.ds(pl.multiple_of((nb - 1 - i) * LANES, LANES), LANES)
            cs = _tri_dot(tri, dc_ref[rows, :]) + tail
            dff = cs * _sigmoid(-(ff_ref[rows, :] + bias))
            d_ref[rows, :] = dff.astype(BF16)
            return cs[0:1, :], acc + jnp.sum(dff, axis=0, keepdims=True)

        zero = jnp.zeros((1, LANES), F32)
        _, acc = lax.fori_loop(0, nb, step, (zero, zero))
        db_ref[...] = acc

    return _pcall(body, name="fox_forget_bwd",
                  out_shape=(jax.ShapeDtypeStruct((t, LANES), BF16), jax.ShapeDtypeStruct((1, LANES), F32)), grid=(1,),
                  in_specs=[pl.BlockSpec((t, LANES), lambda i: (0, 0)), pl.BlockSpec((t, LANES), lambda i: (0, TILE_FF)),
                            pl.BlockSpec((1, LANES), lambda i: (0, 0))],
                  out_specs=(pl.BlockSpec((t, LANES), lambda i: (0, 0)), pl.BlockSpec((1, LANES), lambda i: (0, 0))),
                  dims=("arbitrary",))(dc, pm, bpad)


def _tri_tables(nb, q_major):
    pairs = [(i, j) for i in range(nb) for j in range(i + 1)] if q_major else \
            [(i, j) for j in range(nb) for i in range(j, nb)]
    return jnp.asarray([a for a, _ in pairs], jnp.int32), jnp.asarray([b for _, b in pairs], jnp.int32)


def _causal(s):
    n = s.shape[0]
    row = lax.broadcasted_iota(jnp.int32, (n, n), 0)
    col = lax.broadcasted_iota(jnp.int32, (n, n), 1)
    return jnp.where(col <= row, s, NEG)


def _lane_col(x, lane):
    sel = lax.broadcasted_iota(jnp.int32, x.shape, 1) == lane
    return jnp.sum(jnp.where(sel, x, 0.0), axis=1, keepdims=True)


def _fox_fwd(qa, ka, va, blk):
    t = qa.shape[0]
    nb = t // blk
    qi, kj = _tri_tables(nb, True)
    q_spec = pl.BlockSpec((blk, LANES), lambda h, s, qi_r, kj_r: (qi_r[s], h))
    k_spec = pl.BlockSpec((blk, LANES), lambda h, s, qi_r, kj_r: (kj_r[s], h))

    def body(qi_r, kj_r, q_ref, k_ref, v_ref, o_ref, ob_ref, qb_ref, m_s, acc_s):
        s_id = pl.program_id(1)
        i, j = qi_r[s_id], kj_r[s_id]

        @pl.when(j == 0)
        def _():
            m_s[...] = jnp.full_like(m_s, NEG)
            acc_s[...] = jnp.zeros_like(acc_s)

        def tile(diagonal):
            s = _dot(q_ref[...], k_ref[...], 1, 1)
            if diagonal:
                s = _causal(s)
            m_old = m_s[...]
            m_new = jnp.maximum(m_old, jnp.max(s, axis=1, keepdims=True))
            p = jnp.exp(s - jnp.tile(m_new, (1, blk // LANES)))
            acc_s[...] = jnp.exp(m_old - m_new) * acc_s[...] + _dot(p.astype(BF16), v_ref[...], 1, 0)
            m_s[...] = m_new

        @pl.when(j < i)
        def _():
            tile(False)

        @pl.when(j == i)
        def _():
            tile(True)
            acc = acc_s[...]
            l = _lane_col(acc, FOX_L_LANE)
            o = acc / l
            o_ref[...] = o
            ob_ref[...] = o.astype(BF16)
            hi, mid, lo = _split3(-(m_s[:, 0:1] + jnp.log(l)))
            lane = lax.broadcasted_iota(jnp.int32, acc.shape, 1)
            qb_ref[...] = jnp.where(lane == FOX_LSE_LANE, hi,
                                    jnp.where(lane == FOX_LSE_LANE + 1, mid,
                                              jnp.where(lane == FOX_LSE_LANE + 2, lo, q_ref[...])))

    wide = (t, FOX_TILES)
    return pl.pallas_call(
        body, name="fox_fwd",
        out_shape=(jax.ShapeDtypeStruct(wide, F32), jax.ShapeDtypeStruct(wide, BF16), jax.ShapeDtypeStruct(wide, BF16)),
        grid_spec=pltpu.PrefetchScalarGridSpec(
            num_scalar_prefetch=2, grid=(FOX_HEADS, qi.shape[0]), in_specs=[q_spec, k_spec, k_spec],
            out_specs=(q_spec,) * 3,
            scratch_shapes=[pltpu.VMEM((blk, LANES), F32), pltpu.VMEM((blk, LANES), F32)]),
        compiler_params=_params(("parallel", "arbitrary")),
    )(qi, kj, qa, ka, va)


def _fox_bwd(qa, ka, va, do, o, blk):
    t = qa.shape[0]
    nb = t // blk
    qi, kj = _tri_tables(nb, False)
    q_spec = pl.BlockSpec((blk, LANES), lambda h, s, qi_r, kj_r: (qi_r[s], h))
    k_spec = pl.BlockSpec((blk, LANES), lambda h, s, qi_r, kj_r: (kj_r[s], h))
    head_spec = pl.BlockSpec((t, LANES), lambda h, s, qi_r, kj_r: (0, h))
    head_col = pl.BlockSpec((None, t, 1), lambda h, s, qi_r, kj_r: (h, 0, 0))
    k_col = pl.BlockSpec((None, blk, 1), lambda h, s, qi_r, kj_r: (h, kj_r[s], 0))
    first_spec = pl.BlockSpec((blk, LANES), lambda h, s, qi_r, kj_r: (jnp.where(kj_r[s] == 0, qi_r[s], nb - 1), h))
    n_steps = int(qi.shape[0])

    def body(qi_r, kj_r, q_ref, k_ref, v_ref, do_ref, o_ref, dq_ref, dk_ref, dv_ref, rs_ref, cs_ref,
             doa_s, dq_s, dk_s, dv_s):
        s_id = pl.program_id(1)
        i, j = qi_r[s_id], kj_r[s_id]
        rows = pl.ds(pl.multiple_of(i * blk, blk), blk)

        @pl.when(j == 0)
        def _():
            dof = do_ref[...]
            hi, mid, lo = _split3(-jnp.sum(dof * o_ref[...], axis=1, keepdims=True))
            lane = lax.broadcasted_iota(jnp.int32, dof.shape, 1)
            doa = jnp.where(lane == FOX_C_LANE, hi.astype(F32),
                            jnp.where(lane == FOX_C_LANE + 1, mid.astype(F32),
                                      jnp.where(lane == FOX_C_LANE + 2, lo.astype(F32), dof)))
            doa_s[rows, :] = doa.astype(BF16)
            dq_s[rows, :] = jnp.zeros((blk, LANES), F32)

        @pl.when(i == j)
        def _():
            dk_s[...] = jnp.zeros_like(dk_s)
            dv_s[...] = jnp.zeros_like(dv_s)

        def tile(diagonal):
            q, k = q_ref[...], k_ref[...]
            s = _dot(q, k, 1, 1)
            if diagonal:
                s = _causal(s)
            p = jnp.exp(s)
            doa = doa_s[rows, :]
            ds = (p * _dot(doa, v_ref[...], 1, 1)).astype(BF16)
            dv_s[...] += _dot(p.astype(BF16), doa, 0, 0)
            dk_s[...] += _dot(ds, q, 0, 0)
            dq_s[rows, :] += _dot(ds, k, 1, 0)

        @pl.when(i > j)
        def _():
            tile(False)

        @pl.when(i == j)
        def _():
            tile(True)

        @pl.when(i == nb - 1)
        def _():
            dk = dk_s[...]
            dk_ref[...] = dk.astype(BF16)
            dv_ref[...] = dv_s[...].astype(BF16)
            cs_ref[...] = -_lane_col(dk, FOX_COLSUM_LANE)

        @pl.when(s_id == n_steps - 1)
        def _():
            dq = dq_s[...]
            dq_ref[...] = (dq * FOX_DIM ** -0.5).astype(BF16)
            rs_ref[...] = _lane_col(dq, FOX_ROWSUM_LANE)

    wide = jax.ShapeDtypeStruct((t, FOX_TILES), BF16)
    cols = jax.ShapeDtypeStruct((FOX_HEADS, t, 1), F32)
    return pl.pallas_call(
        body, name="fox_bwd", out_shape=(wide, wide, wide, cols, cols),
        grid_spec=pltpu.PrefetchScalarGridSpec(
            num_scalar_prefetch=2, grid=(FOX_HEADS, n_steps),
            in_specs=[q_spec, k_spec, k_spec, first_spec, first_spec],
            out_specs=(head_spec, k_spec, k_spec, head_col, k_col),
            scratch_shapes=[pltpu.VMEM((t, LANES), BF16), pltpu.VMEM((t, LANES), F32), pltpu.VMEM((blk, LANES), F32),
                            pltpu.VMEM((blk, LANES), F32)]),
        compiler_params=_params(("parallel", "arbitrary")),
    )(qi, kj, qa, ka, va, do, o)


def _merge_fwd(gm, bm, za, zb):
    t, d = za.shape
    tt = _tile(t, 512, 16)
    row = pl.BlockSpec((tt, d), lambda i: (i, 0))

    def body(gm_ref, b_ref, za_ref, zb_ref, o_ref):
        ga = _sigmoid(gm_ref[:, :d] + b_ref[:, :d])
        gb = _sigmoid(gm_ref[:, d:] + b_ref[:, d:])
        o_ref[...] = (ga * za_ref[...] + gb * zb_ref[...]).astype(BF16)

    return _pcall(body, name="merge_fwd", out_shape=jax.ShapeDtypeStruct((t, d), BF16), grid=(t // tt,),
                  in_specs=[pl.BlockSpec((tt, 2 * d), lambda i: (i, 0)), pl.BlockSpec((1, 2 * d), lambda i: (0, 0)), row, row],
                  out_specs=row, dims=("parallel",))(gm, bm, za, zb)


def _merge_bwd(dmix, gm, bm, za, zb):
    t, d = za.shape
    tt = _tile(t, 512, 16)
    row = pl.BlockSpec((tt, d), lambda i: (i, 0))
    wide = pl.BlockSpec((tt, 2 * d), lambda i: (i, 0))
    vec = pl.BlockSpec((1, 2 * d), lambda i: (0, 0))

    def body(dm_ref, gm_ref, b_ref, za_ref, zb_ref, dza_ref, dzb_ref, dgm_ref, db_ref):
        dm = dm_ref[...]
        ga = _sigmoid(gm_ref[:, :d] + b_ref[:, :d])
        gb = _sigmoid(gm_ref[:, d:] + b_ref[:, d:])
        dza_ref[...] = (dm * ga).astype(BF16)
        dzb_ref[...] = (dm * gb).astype(BF16)
        dla = dm * za_ref[...] * ga * (1.0 - ga)
        dlb = dm * zb_ref[...] * gb * (1.0 - gb)
        dgm_ref[:, :d] = dla.astype(BF16)
        dgm_ref[:, d:] = dlb.astype(BF16)
        pa = jnp.sum(dla, axis=0, keepdims=True)
        pb = jnp.sum(dlb, axis=0, keepdims=True)

        @pl.when(pl.program_id(0) == 0)
        def _():
            db_ref[:, :d] = pa
            db_ref[:, d:] = pb

        @pl.when(pl.program_id(0) > 0)
        def _():
            db_ref[:, :d] += pa
            db_ref[:, d:] += pb

    return _pcall(body, name="merge_bwd",
                  out_shape=(jax.ShapeDtypeStruct((t, d), BF16), jax.ShapeDtypeStruct((t, d), BF16),
                             jax.ShapeDtypeStruct((t, 2 * d), BF16), jax.ShapeDtypeStruct((1, 2 * d), F32)),
                  grid=(t // tt,), in_specs=[row, wide, vec, row, row], out_specs=(row, row, wide, vec),
                  dims=("arbitrary",))(dmix, gm, bm, za, zb)


def _ple_final(h3, pgl, pe, gain, target):
    t, d = h3.shape
    tt = _tile(t, 512, 16)
    row = pl.BlockSpec((tt, d), lambda i: (i, 0))
    vec = pl.BlockSpec((1, d), lambda i: (0, 0))
    lvec = pl.BlockSpec((1, LANES), lambda i: (0, 0))

    def body(h_ref, pgl_ref, pe_ref, g_ref, t_ref, dh_ref, dsg_ref, dpe_ref, loss_ref, dg_ref):
        pg = _sigmoid(pgl_ref[...])
        pe_v = pe_ref[...]
        h4 = h_ref[...] + pg * pe_v
        r = lax.rsqrt(jnp.mean(h4 * h4, axis=-1, keepdims=True) + EPS)
        gv = g_ref[...]
        err = h4 * r * gv - t_ref[...]
        part_loss = 0.5 * jnp.sum(jnp.mean(err * err, axis=-1, keepdims=True), axis=0, keepdims=True)
        dy = err * (1.0 / d)
        part_g = jnp.sum(dy * h4 * r, axis=0, keepdims=True)
        dyg = dy * gv
        dh = r * dyg - h4 * (r * r * r) * jnp.mean(dyg * h4, axis=-1, keepdims=True)
        dh_ref[...] = dh
        dsg_ref[...] = (dh * pe_v * pg * (1.0 - pg)).astype(BF16)
        dpe_ref[...] = (dh * pg).astype(BF16)

        @pl.when(pl.program_id(0) == 0)
        def _():
            loss_ref[...] = jnp.broadcast_to(part_loss, (1, LANES))
            dg_ref[...] = part_g

        @pl.when(pl.program_id(0) > 0)
        def _():
            loss_ref[...] += jnp.broadcast_to(part_loss, (1, LANES))
            dg_ref[...] += part_g

    return _pcall(body, name="ple_final",
                  out_shape=(jax.ShapeDtypeStruct((t, d), F32), jax.ShapeDtypeStruct((t, d), BF16),
                             jax.ShapeDtypeStruct((t, d), BF16), jax.ShapeDtypeStruct((1, LANES), F32),
                             jax.ShapeDtypeStruct((1, d), F32)),
                  grid=(t // tt,), in_specs=[row, row, row, vec, row], out_specs=(row, row, row, lvec, vec),
                  dims=("arbitrary",))(h3, pgl, pe, gain, target)


def _adamw_math(w, g, m, v):
    m = ADAM_B1 * m + (1.0 - ADAM_B1) * g
    v = ADAM_B2 * v + (1.0 - ADAM_B2) * (g * g)
    m_hat = m / (1.0 - ADAM_B1 ** ADAM_STEP)
    v_hat = v / (1.0 - ADAM_B2 ** ADAM_STEP)
    delta = -ADAM_LR * (m_hat / (jnp.sqrt(v_hat) + ADAM_EPS) + ADAM_WD * w)
    return delta, m, v


def _adamw(parts, w, m, v, name):
    n, r, c = parts.shape
    tr = _tile(r, 512, 16)
    row = pl.BlockSpec((tr, c), lambda i: (i, 0))

    def body(p_ref, w_ref, m_ref, v_ref, g_ref, d_ref, mo_ref, vo_ref):
        g = p_ref[0].astype(F32)
        for s in range(1, n):
            g = g + p_ref[s].astype(F32)
        g_ref[...] = g
        d_ref[...], mo_ref[...], vo_ref[...] = _adamw_math(w_ref[...], g, m_ref[...], v_ref[...])

    return _pcall(body, name=name, out_shape=(jax.ShapeDtypeStruct((r, c), F32),) * 4, grid=(r // tr,),
                  in_specs=[pl.BlockSpec((n, tr, c), lambda i: (0, i, 0)), row, row, row], out_specs=(row,) * 4,
                  dims=("parallel",))(parts, w, m, v)


ANY = pl.BlockSpec(memory_space=pl.ANY)


def _all_gather(shards):
    n = len(shards)

    def body(*refs):
        x_refs, out_refs = refs[:n], refs[n:2 * n]
        send_sems, recv_sems, local_sems = refs[2 * n:]
        x, y, cc = lax.axis_index("x"), lax.axis_index("y"), lax.axis_index("c")
        me, sibling = (x, y, cc), (x, y, 1 - cc)
        chips = [(1 - x, y), (x, 1 - y), (1 - x, 1 - y)]

        def slot(a, px, py, pc):
            return out_refs[a].at[4 * px + 2 * py + pc]

        def copy(a, k, block, to, src=None):
            return pltpu.make_async_remote_copy(
                src_ref=slot(a, *block) if src is None else src, dst_ref=slot(a, *block),
                send_sem=send_sems.at[7 * a + k], recv_sem=recv_sems.at[7 * a + k], device_id=to, device_id_type=MESH)

        local, sent = [], []
        for a in range(n):
            local.append(pltpu.make_async_copy(x_refs[a], slot(a, *me), local_sems.at[a]))
            sent.append(copy(a, 0, me, sibling, src=x_refs[a]))
            sent += [copy(a, 1 + j, me, (*chip, cc), src=x_refs[a]) for j, chip in enumerate(chips)]
        for cp in local + sent:
            cp.start()
        for j, chip in enumerate(chips):
            for a in range(n):
                copy(a, 1 + j, (*chip, cc), me).wait_recv()
                sent.append(copy(a, 4 + j, (*chip, cc), sibling))
                sent[-1].start()
        for a in range(n):
            copy(a, 0, sibling, me).wait_recv()
            for j, chip in enumerate(chips):
                copy(a, 4 + j, (*chip, 1 - cc), me).wait_recv()
        for cp in sent:
            cp.wait_send()
        for cp in local:
            cp.wait()

    return pl.pallas_call(
        body, name="weights_all_gather",
        out_shape=tuple(jax.ShapeDtypeStruct((N_DEV,) + s.shape, s.dtype) for s in shards),
        in_specs=[ANY] * n, out_specs=(ANY,) * n,
        scratch_shapes=[pltpu.SemaphoreType.DMA((7 * n,)), pltpu.SemaphoreType.DMA((7 * n,)),
                        pltpu.SemaphoreType.DMA((n,))],
    )(*shards)


def _reduce_scatter_exchange(blocks):
    n = len(blocks)

    def body(*refs):
        g_refs, recv_refs = refs[:n], refs[n:2 * n]
        send_sems, recv_sems, local_sems = refs[2 * n:]
        x, y, cc = lax.axis_index("x"), lax.axis_index("y"), lax.axis_index("c")
        me = 4 * x + 2 * y + cc
        local, sent, landing = [], [], []
        for a in range(n):
            local.append(pltpu.make_async_copy(g_refs[a].at[me], recv_refs[a].at[me], local_sems.at[a]))
        for k in range(1, N_DEV):
            px, py, pc = x ^ (k >> 2), y ^ ((k >> 1) & 1), cc ^ (k & 1)
            peer = 4 * px + 2 * py + pc
            for a in range(n):
                sems = dict(send_sem=send_sems.at[7 * a + k - 1], recv_sem=recv_sems.at[7 * a + k - 1],
                            device_id=(px, py, pc), device_id_type=MESH)
                sent.append(pltpu.make_async_remote_copy(src_ref=g_refs[a].at[peer], dst_ref=recv_refs[a].at[me], **sems))
                landing.append(pltpu.make_async_remote_copy(src_ref=g_refs[a].at[me], dst_ref=recv_refs[a].at[peer], **sems))
        for cp in local + sent:
            cp.start()
        for cp in landing:
            cp.wait_recv()
        for cp in sent:
            cp.wait_send()
        for cp in local:
            cp.wait()

    return pl.pallas_call(
        body, name="grads_reduce_scatter_exchange",
        out_shape=tuple(jax.ShapeDtypeStruct(b.shape, b.dtype) for b in blocks),
        in_specs=[ANY] * n, out_specs=(ANY,) * n,
        scratch_shapes=[pltpu.SemaphoreType.DMA((7 * n,)), pltpu.SemaphoreType.DMA((7 * n,)),
                        pltpu.SemaphoreType.DMA((n,))],
    )(*blocks)


HBM = pl.BlockSpec(memory_space=pltpu.HBM)
SEM = pl.BlockSpec(memory_space=pltpu.SEMAPHORE)
DATAFLOW = pltpu.SideEffectType.DATAFLOW_SIDE_EFFECTING


def _peers():
    x, y, cc = lax.axis_index("x"), lax.axis_index("y"), lax.axis_index("c")
    out = []
    for k in range(1, N_DEV):
        px, py, pc = x ^ (k >> 2), y ^ ((k >> 1) & 1), cc ^ (k & 1)
        out.append((k, (px, py, pc), 4 * px + 2 * py + pc))
    return 4 * x + 2 * y + cc, out


def _scatter_start(blocks, after, name, gather=False):
    n = len(blocks)
    lands = [lax.empty((N_DEV,) + b.shape if gather else b.shape, b.dtype) for b in blocks]

    def body(*refs):
        g_refs, land_refs = refs[:n], refs[n:2 * n]
        send_sems, recv_sems, token = refs[2 * n + 1], refs[2 * n + 2], refs[-1]
        me, peers = _peers()
        for k, peer, slot in peers:
            for a in range(n):
                pltpu.make_async_remote_copy(
                    src_ref=g_refs[a] if gather else g_refs[a].at[slot], dst_ref=land_refs[a].at[me],
                    send_sem=send_sems.at[7 * a + k - 1],
                    recv_sem=recv_sems.at[7 * a + k - 1], device_id=peer, device_id_type=MESH).start()
        token[...] = jnp.zeros_like(token)

    thru = [pltpu.HBM(b.shape, b.dtype) for b in blocks]
    thru_lands = [pltpu.HBM(b.shape, b.dtype) for b in lands]
    return pl.pallas_call(
        body, name=name,
        out_shape=(pltpu.SemaphoreType.DMA((7 * n,)), pltpu.SemaphoreType.DMA((7 * n,)), *thru, *thru_lands,
                   jax.ShapeDtypeStruct((8, LANES), F32)),
        in_specs=[HBM] * (2 * n) + [pl.BlockSpec(memory_space=pl.ANY)],
        out_specs=(SEM, SEM, *[HBM] * (2 * n), pl.BlockSpec(memory_space=pltpu.VMEM)),
        input_output_aliases={i: 2 + i for i in range(2 * n)},
        compiler_params=pltpu.CompilerParams(has_side_effects=DATAFLOW),
    )(*[pltpu.with_memory_space_constraint(a, pltpu.HBM) for a in list(blocks) + lands], after)


def _scatter_wait(started, after, name, gather=False):
    send_sems, recv_sems, *rest = started
    n = (len(rest) - 1) // 2
    thru = rest[:2 * n]

    def body(*refs):
        g_refs, land_refs = refs[:n], refs[n:2 * n]
        send_sems, recv_sems = refs[2 * n], refs[2 * n + 1]
        me, peers = _peers()
        for k, peer, slot in peers:
            for a in range(n):
                copy = pltpu.make_async_remote_copy(
                    src_ref=g_refs[a] if gather else g_refs[a].at[slot], dst_ref=land_refs[a].at[slot],
                    send_sem=send_sems.at[7 * a + k - 1],
                    recv_sem=recv_sems.at[7 * a + k - 1], device_id=peer, device_id_type=MESH)
                copy.wait_send()
                copy.wait_recv()

    out = pl.pallas_call(
        body, name=name, out_shape=tuple(pltpu.HBM(a.shape, a.dtype) for a in thru),
        in_specs=[HBM] * (2 * n) + [SEM, SEM, pl.BlockSpec(memory_space=pl.ANY)], out_specs=tuple([HBM] * (2 * n)),
        input_output_aliases={i: i for i in range(2 * n)},
        compiler_params=pltpu.CompilerParams(has_side_effects=DATAFLOW),
    )(*thru, send_sems, recv_sems, after)
    return out[:n], out[n:]


BIG = (("w_ffn1_gate", "colT"), ("w_ffn1_up", "colT"), ("w_ffn1_down", "row"), ("w_ffn2_gate", "colT"),
       ("w_ffn2_up", "colT"), ("w_ffn2_down", "row"), ("w_in", "colT"), ("w_merge", "col"), ("w_ret_out", "col"),
       ("w_fox_out", "col"), ("w_out", "row"), ("w_ple", "col"), ("w_ple_gate", "row"))


def _shard_view(a, kind):
    a = a.reshape(a.shape[-2:])
    return a.T if kind == "colT" else a


def _unview(a, kind, shape):
    return (a.T if kind == "colT" else a).reshape(shape)


def _full_from_slots(g, kind):
    n, r, c = g.shape
    return g.transpose(1, 0, 2).reshape(r, n * c) if kind == "col" else g.reshape(n * r, c)


def _slots_from_full(f, kind):
    r, c = f.shape
    return f.reshape(r, N_DEV, c // N_DEV).transpose(1, 0, 2) if kind == "col" else f.reshape(N_DEV, r // N_DEV, c)


EARLY_GROUPS = (("w_ple_gate", "w_ple", "w_ffn2_down", "w_ffn2_gate", "w_ffn2_up", "w_out", "w_ret_out", "w_fox_out"),
                ("w_in", "w_merge"))


def _scatter_group(gw, names, after, name):
    kind = dict(BIG)
    return names, _scatter_start([_slots_from_full(gw[n], kind[n]) for n in names], after, name)


def _pad_heads(w):
    d = w.shape[1]
    return jnp.pad(w.reshape(FOX_HEADS, FOX_DIM, d), ((0, 0), (0, LANES - FOX_DIM), (0, 0))).reshape(FOX_TILES, d)


def _unpad_heads(w):
    d = w.shape[1]
    return w.reshape(FOX_HEADS, LANES, d)[:, :FOX_DIM].reshape(FOX_WIDTH, d)


def _deinterleave_rows(w):
    d = w.shape[1]
    return w.reshape(RET_HEADS, RET_DIM // 2, 2, d).transpose(0, 2, 1, 3).reshape(RET_WIDTH, d)


def _interleave_rows(w):
    d = w.shape[1]
    return w.reshape(RET_HEADS, 2, RET_DIM // 2, d).transpose(0, 2, 1, 3).reshape(RET_WIDTH, d)


def _pad_w_in(wt):
    d = wt.shape[1]
    rw, fw = RET_WIDTH, FOX_WIDTH
    fo = 4 * rw
    return jnp.concatenate([
        _deinterleave_rows(wt[:rw]), _deinterleave_rows(wt[rw:2 * rw]), wt[2 * rw:4 * rw],
        _pad_heads(wt[fo:fo + fw]), _pad_heads(wt[fo + fw:fo + 2 * fw]), _pad_heads(wt[fo + 2 * fw:fo + 3 * fw]),
        wt[fo + 3 * fw:], jnp.zeros((2 * LANES - FOX_HEADS, d), wt.dtype)], axis=0)


def _unpad_w_in(g):
    rw = RET_WIDTH
    f0 = 4 * rw
    return jnp.concatenate([
        _interleave_rows(g[:rw]), _interleave_rows(g[rw:2 * rw]), g[2 * rw:4 * rw],
        _unpad_heads(g[f0:f0 + FOX_TILES]), _unpad_heads(g[f0 + FOX_TILES:f0 + 2 * FOX_TILES]),
        _unpad_heads(g[f0 + 2 * FOX_TILES:f0 + 3 * FOX_TILES]),
        g[f0 + 3 * FOX_TILES:f0 + 3 * FOX_TILES + FOX_HEADS]], axis=0)


SMALL = ("ln_ffn1", "ln_mix", "b_forget", "b_merge", "ln_ffn2", "ln_ple", "ln_final")


def _small_rows(n):
    rows = -(-n // LANES)
    return -(-rows // 8) * 8


def _pack_small(vals, with_loss=None):
    parts = []
    for name in SMALL:
        v = vals[name].reshape(-1).astype(F32)
        rows = _small_rows(v.shape[0])
        parts.append(jnp.pad(v, (0, rows * LANES - v.shape[0])).reshape(rows, LANES))
    if with_loss is not None:
        parts.append(jnp.pad(with_loss.reshape(1, LANES), ((0, 7), (0, 0))))
    else:
        parts.append(jnp.zeros((8, LANES), F32))
    return jnp.concatenate(parts, axis=0)


def _unpack_small(packed, shapes):
    out, at = {}, 0
    for name in SMALL:
        n = int(np.prod(shapes[name]))
        rows = _small_rows(n)
        out[name] = packed[at:at + rows].reshape(-1)[:n].reshape(shapes[name])
        at += rows
    return out, packed[at, 0]


def _gather_finish(group, after, me):
    names, started, wait_name = group
    kind = dict(BIG)
    sent, landed = _scatter_wait(started, after, wait_name, gather=True)
    return {n: _full_from_slots(lax.dynamic_update_slice_in_dim(land, shard[None], me, 0), kind[n])
            for n, shard, land in zip(names, sent, landed)}


def _local_step(x, p, positions, target, w, small, me, entry_token, gathers):
    t, d = x.shape
    gain = lambda n: small[n].reshape(1, d)
    w = dict(w)
    bpad = jnp.pad(small["b_forget"].reshape(1, FOX_HEADS), ((0, 0), (0, LANES - FOX_HEADS)))
    bm = small["b_merge"].reshape(1, 2 * d)
    fox_blk = _tile(t, 1024, 128)

    def ffn_fwd(n, tag, down_gather=None):
        g = _mm([(n, w[f"w_{tag}_gate"], "nt")], BF16, f"{tag}_gate", tm=2048, tn=1408)
        u = _mm([(n, w[f"w_{tag}_up"], "nt")], BF16, f"{tag}_up", tm=2048, tn=1408)
        a = _swiglu_fwd(g, u, f"{tag}_swiglu")
        if down_gather is not None:
            w.update(_gather_finish(down_gather, a, me))
        return g, u, a, _mm([(a, w[f"w_{tag}_down"], "nn")], F32, f"{tag}_down")

    n1 = _rms_fwd(x, gain("ln_ffn1") + entry_token, "rms_ffn1")
    g1, u1, a1, f1 = ffn_fwd(n1, "ffn1", down_gather=gathers[0])
    h1, u = _rms_fwd(x, gain("ln_mix"), "rms_mix", f=f1)
    w.update(_gather_finish(gathers[1], f1, me))
    w_in_t = _pad_w_in(w["w_in"])
    pm = _mm([(u, w_in_t, "nt")], F32, "mixer_in", tn=1792)
    w.update(_gather_finish(gathers[2], pm, me))
    gm = _mm([(u, w["w_merge"], "nn")], F32, "mixer_gates")

    half = jnp.arange(RET_DIM // 2, dtype=F32) / (RET_DIM // 2)
    inv = 1.0 / (ROPE_BASE ** half)
    inv2 = jnp.concatenate([inv, inv]).reshape(1, RET_DIM)
    sign2 = jnp.concatenate([-jnp.ones((RET_DIM // 2,), F32), jnp.ones((RET_DIM // 2,), F32)]).reshape(1, RET_DIM)
    cos2, sin2 = _rope_tables(positions.reshape(t, 1), inv2, sign2)
    consts = _ret_consts()
    y_ret, y_raw, states = _ret_fwd(pm, cos2, sin2, consts)
    w.update(_gather_finish(gathers[3], y_raw, me))
    w_fox_pad = _pad_heads(w["w_fox_out"])
    za = _mm([(y_ret, w["w_ret_out"], "nn")], F32, "ret_out")

    qa, ka, va = _fox_prep(pm, bpad)
    o_fox, y_fox, qa_b = _fox_fwd(qa, ka, va, fox_blk)
    zb = _mm([(y_fox, w_fox_pad, "nn")], F32, "fox_out")

    mix = _merge_fwd(gm, bm, za, zb)
    mo = _mm([(mix, w["w_out"], "nn")], F32, "mix_out")
    h2, n2 = _rms_fwd(h1, gain("ln_ffn2"), "rms_ffn2", f=mo, scale=1.0)
    w.update(_gather_finish(gathers[4], mo, me))
    g2, u2, a2, f2 = ffn_fwd(n2, "ffn2")
    h3, n3 = _rms_fwd(h2, gain("ln_ple"), "rms_ple", f=f2)
    pgl = _mm([(n3, w["w_ple_gate"], "nn")], F32, "ple_gate")
    pb = p.astype(BF16)
    pe = _mm([(pb, w["w_ple"], "nn")], F32, "ple_embed")

    gw, gs = {}, {}
    dh4, dsg, dpe, loss, gs["ln_final"] = _ple_final(h3, pgl, pe, gain("ln_final"), target)
    gw["w_ple_gate"] = _mm([(n3, dsg, "tn")], BF16, "d_w_ple_gate", tn=256)
    gw["w_ple"] = _mm([(pb, dpe, "tn")], BF16, "d_w_ple", tn=256)
    dn3 = _mm([(dsg, w["w_ple_gate"], "nt")], F32, "d_n3")
    dh3, dh3_half, gs["ln_ple"] = _rms_bwd(dn3, h3, gain("ln_ple"), dh4, "rms_ple_bwd", 0.5)

    def ffn_bwd(dh_half, g, u_, a, n, tag, scatter_now=None):
        gw[f"w_{tag}_down"] = _mm([(a, dh_half, "tn")], BF16, f"d_w_{tag}_down", tm=1408, tn=256)
        start = scatter_now if scatter_now is not None else (lambda *_: None)
        token = start((f"w_{tag}_down",), dh_half, "c")
        da = _mm([(dh_half, w[f"w_{tag}_down"], "nt")], BF16, f"d_a_{tag}", tn=1408, after=token)
        dg, du_ = _swiglu_bwd(da, g, u_, f"{tag}_swiglu_bwd")
        gw[f"w_{tag}_gate"] = _mm([(dg, n, "tn")], BF16, f"d_w_{tag}_gate", tm=1408, tn=256)
        token = start((f"w_{tag}_gate",), da, "d")
        gw[f"w_{tag}_up"] = _mm([(du_, n, "tn")], BF16, f"d_w_{tag}_up", tm=1408, tn=256, after=token)
        token = start((f"w_{tag}_up",), da, "e")
        return _mm([(dg, w[f"w_{tag}_gate"], "nn"), (du_, w[f"w_{tag}_up"], "nn")], F32, f"d_n_{tag}", tm=512,
                   after=token)

    dn2 = ffn_bwd(dh3_half, g2, u2, a2, n2, "ffn2")
    dh2, dh2_b, gs["ln_ffn2"] = _rms_bwd(dn2, h2, gain("ln_ffn2"), dh3, "rms_ffn2_bwd", 1.0)

    gw["w_out"] = _mm([(mix, dh2_b, "tn")], BF16, "d_w_out", tn=256)
    dmix = _mm([(dh2_b, w["w_out"], "nt")], F32, "d_mix")
    dza, dzb, dgm, gs["b_merge"] = _merge_bwd(dmix, gm, bm, za, zb)
    gw["w_ret_out"] = _mm([(y_ret, dza, "tn")], BF16, "d_w_ret_out", tn=256)
    gw["w_fox_out"] = _unpad_heads(_mm([(y_fox, dzb, "tn")], BF16, "d_w_fox_out", tn=256))
    dy_ret = _mm([(dza, w["w_ret_out"], "nt")], F32, "d_y_ret")
    do_fox = _mm([(dzb, w_fox_pad, "nt")], F32, "d_y_fox")

    pending = [_scatter_group(gw, EARLY_GROUPS[0], dy_ret, "grads_scatter_a_start")]
    token = pending[0][1][-1][0, 0]
    drq, drk, drv, drg = _ret_bwd(dy_ret, pm, cos2, sin2, y_raw, states, consts[:3] + (consts[3] + token,))

    dqa, dka, dva, ds_rows, ds_cols = _fox_bwd(qa_b, ka, va, do_fox, o_fox, fox_blk)
    dc = jnp.pad((ds_rows + ds_cols).reshape(FOX_HEADS, t).T, ((0, 0), (0, LANES - FOX_HEADS)))
    dff, db_forget = _fox_post(dc, pm, bpad)
    gs["b_forget"] = db_forget[:, :FOX_HEADS]

    dpm = jnp.concatenate([drq, drk, drv, drg, dqa, dka, dva, dff, jnp.zeros((t, LANES), BF16)], axis=1)
    gw["w_merge"] = _mm([(u, dgm, "tn")], BF16, "d_w_merge", tn=512)
    gw["w_in"] = _unpad_w_in(_mm([(dpm, u, "tn")], BF16, "d_w_in", tm=1792, tn=256))
    du = _mm([(dpm, w_in_t, "nn"), (dgm, w["w_merge"], "nt")], F32, "d_u", tm=1024, tn=512)
    pending.append(_scatter_group(gw, EARLY_GROUPS[1], du, "grads_scatter_b_start"))
    token = pending[1][1][-1][0:1, 0:1]
    dh1, dh1_half, gs["ln_mix"] = _rms_bwd(du, h1, gain("ln_mix") + token, dh2, "rms_mix_bwd", 0.5)

    def scatter_now(names, after, tag):
        pending.append(_scatter_group(gw, names, after, f"grads_scatter_{tag}_start"))
        return pending[-1][1][-1]

    dn1 = ffn_bwd(dh1_half, g1, u1, a1, n1, "ffn1", scatter_now=scatter_now)
    dx, _, gs["ln_ffn1"] = _rms_bwd(dn1, x, gain("ln_ffn1"), dh1, "rms_ffn1_bwd", 1.0)
    return loss, dx, gw, gs, pending


WEIGHTS = ("ln_ffn1", "w_ffn1_gate", "w_ffn1_up", "w_ffn1_down", "ln_mix", "w_in", "b_forget", "w_merge", "b_merge",
           "w_ret_out", "w_fox_out", "w_out", "ln_ffn2", "w_ffn2_gate", "w_ffn2_up", "w_ffn2_down", "ln_ple", "w_ple",
           "w_ple_gate", "ln_final")


def kernel(x, p, positions, ln_ffn1, w_ffn1_gate, w_ffn1_up, w_ffn1_down, ln_mix, w_in, b_forget, w_merge, b_merge, w_ret_out, w_fox_out, w_out, ln_ffn2, w_ffn2_gate, w_ffn2_up, w_ffn2_down, ln_ple, w_ple, w_ple_gate, ln_final, loss_target, m_ln_ffn1, m_w_ffn1_gate, m_w_ffn1_up, m_w_ffn1_down, m_ln_mix, m_w_in, m_b_forget, m_w_merge, m_b_merge, m_w_ret_out, m_w_fox_out, m_w_out, m_ln_ffn2, m_w_ffn2_gate, m_w_ffn2_up, m_w_ffn2_down, m_ln_ple, m_w_ple, m_w_ple_gate, m_ln_final, v_ln_ffn1, v_w_ffn1_gate, v_w_ffn1_up, v_w_ffn1_down, v_ln_mix, v_w_in, v_b_forget, v_w_merge, v_b_merge, v_w_ret_out, v_w_fox_out, v_w_out, v_ln_ffn2, v_w_ffn2_gate, v_w_ffn2_up, v_w_ffn2_down, v_ln_ple, v_w_ple, v_w_ple_gate, v_ln_final):
    args = dict(ln_ffn1=ln_ffn1, w_ffn1_gate=w_ffn1_gate, w_ffn1_up=w_ffn1_up, w_ffn1_down=w_ffn1_down, ln_mix=ln_mix, w_in=w_in, b_forget=b_forget, w_merge=w_merge, b_merge=b_merge, w_ret_out=w_ret_out, w_fox_out=w_fox_out, w_out=w_out, ln_ffn2=ln_ffn2, w_ffn2_gate=w_ffn2_gate, w_ffn2_up=w_ffn2_up, w_ffn2_down=w_ffn2_down, ln_ple=ln_ple, w_ple=w_ple, w_ple_gate=w_ple_gate, ln_final=ln_final)
    moms = dict(ln_ffn1=m_ln_ffn1, w_ffn1_gate=m_w_ffn1_gate, w_ffn1_up=m_w_ffn1_up, w_ffn1_down=m_w_ffn1_down, ln_mix=m_ln_mix, w_in=m_w_in, b_forget=m_b_forget, w_merge=m_w_merge, b_merge=m_b_merge, w_ret_out=m_w_ret_out, w_fox_out=m_w_fox_out, w_out=m_w_out, ln_ffn2=m_ln_ffn2, w_ffn2_gate=m_w_ffn2_gate, w_ffn2_up=m_w_ffn2_up, w_ffn2_down=m_w_ffn2_down, ln_ple=m_ln_ple, w_ple=m_w_ple, w_ple_gate=m_w_ple_gate, ln_final=m_ln_final)
    vars_ = dict(ln_ffn1=v_ln_ffn1, w_ffn1_gate=v_w_ffn1_gate, w_ffn1_up=v_w_ffn1_up, w_ffn1_down=v_w_ffn1_down, ln_mix=v_ln_mix, w_in=v_w_in, b_forget=v_b_forget, w_merge=v_w_merge, b_merge=v_b_merge, w_ret_out=v_w_ret_out, w_fox_out=v_w_fox_out, w_out=v_w_out, ln_ffn2=v_ln_ffn2, w_ffn2_gate=v_w_ffn2_gate, w_ffn2_up=v_w_ffn2_up, w_ffn2_down=v_w_ffn2_down, ln_ple=v_ln_ple, w_ple=v_w_ple, w_ple_gate=v_w_ple_gate, ln_final=v_ln_final)
    kinds = ("grad", "delta", "new_m", "new_v")

    me = 4 * lax.axis_index("x") + 2 * lax.axis_index("y") + lax.axis_index("c")
    kind_of = dict(BIG)
    shard = {n: _shard_view(args[n], kind).astype(BF16) for n, kind in BIG}
    first = ("w_ffn1_gate", "w_ffn1_up")
    groups = (("w_ffn1_down",), ("w_in",), ("w_merge",), ("w_ret_out", "w_fox_out", "w_out"))
    groups += (tuple(n for n, _ in BIG if n not in first + sum(groups, ())),)
    gathered = _all_gather([shard[n] for n in first])
    w_full = {n: _full_from_slots(g, kind_of[n]) for n, g in zip(first, gathered)}
    gathers, after = [], gathered[0]
    for tag, names in zip("zambc", groups):
        started = _scatter_start([shard[n] for n in names], after, f"weights_gather_{tag}_start", gather=True)
        gathers.append((names, started, f"weights_gather_{tag}_wait"))
        after = started[-1]

    small = {n: args[n] for n in SMALL}
    loss_part, dx, gw, gs, pending = _local_step(x[0], p[0, 0], positions[0], loss_target[0], w_full, small, me,
                                                 after[0:1, 0:1], gathers)

    parts_of = {}
    for tag, (names, started) in zip("abcde", pending):
        sent, landed = _scatter_wait(started, dx, f"grads_scatter_{tag}_wait")
        for n, blk, land in zip(names, sent, landed):
            own = lax.dynamic_index_in_dim(blk, me, 0, keepdims=True)
            parts_of[n] = lax.dynamic_update_slice_in_dim(land, own, me, 0)
    late = [(n, kind) for n, kind in BIG if n not in parts_of]
    small_part = _pack_small(gs, with_loss=loss_part)
    blocks = [_slots_from_full(gw[n], kind) for n, kind in late]
    blocks.append(jnp.broadcast_to(small_part, (N_DEV,) + small_part.shape))
    recv = _reduce_scatter_exchange(blocks)
    parts_of.update({n: r for (n, _), r in zip(late, recv)})

    res = {}
    for n, kind in BIG:
        parts = parts_of[n]
        outs = _adamw(parts, _shard_view(args[n], kind), _shard_view(moms[n], kind), _shard_view(vars_[n], kind),
                      f"adamw_{n}")
        for what, o in zip(kinds, outs):
            res[(what, n)] = _unview(o, kind, args[n].shape)
    s_outs = _adamw(recv[-1], _pack_small(small), _pack_small({n: moms[n] for n in SMALL}),
                    _pack_small({n: vars_[n] for n in SMALL}), "adamw_small")
    for what, sm in zip(kinds, s_outs):
        svals, extra = _unpack_small(sm, {n: args[n].shape for n in SMALL})
        if what == "grad":
            loss = extra
        for n in SMALL:
            res[(what, n)] = svals[n]
    return (loss, dx[None], *[res[(what, n)] for what in kinds for n in WEIGHTS])
```

```python
import numpy as np
import jax
import jax.numpy as jnp
from jax import lax
from jax.experimental import pallas as pl
from jax.experimental.pallas import tpu as pltpu

F32 = jnp.float32
BF16 = jnp.bfloat16

N_DEV = 8
EPS = 1e-6
RET_HEADS = 4
RET_DIM = 128
RET_WIDTH = RET_HEADS * RET_DIM
FOX_HEADS = 8
FOX_DIM = 64
FOX_WIDTH = FOX_HEADS * FOX_DIM
CHUNK = 128
ROPE_BASE = 10000.0
LANES = 128
FOX_TILES = FOX_HEADS * LANES
IN_COLS = 4 * RET_WIDTH + 3 * FOX_WIDTH + FOX_HEADS
IN_PAD = 4 * RET_WIDTH + 3 * FOX_TILES + 2 * LANES
TILE_RQ, TILE_RK, TILE_RV, TILE_RG = 0, 4, 8, 12
TILE_FQ, TILE_FK, TILE_FV, TILE_FF = 16, 24, 32, 40
NEG = -1e30

ADAM_LR = 0.001
ADAM_B1 = 0.9
ADAM_B2 = 0.999
ADAM_EPS = 1e-08
ADAM_WD = 0.01
ADAM_STEP = 10

VMEM_LIMIT_BYTES = 56 * 1024 * 1024

MESH = pl.DeviceIdType.MESH


def _tile(dim, pref, mult):
    if dim <= pref:
        return dim
    t = (pref // mult) * mult
    while t >= mult:
        if dim % t == 0:
            return t
        t -= mult
    return dim


def _params(dims):
    return pltpu.CompilerParams(dimension_semantics=dims, vmem_limit_bytes=VMEM_LIMIT_BYTES)


def _pcall(body, *, name, out_shape, grid, in_specs, out_specs, scratch_shapes=(), dims=None):
    return pl.pallas_call(body, name=name, out_shape=out_shape, grid=grid, in_specs=in_specs, out_specs=out_specs,
                          scratch_shapes=list(scratch_shapes), compiler_params=_params(dims))


def _dot(a, b, ca, cb):
    return lax.dot_general(a, b, (((ca,), (cb,)), ((), ())), preferred_element_type=F32)


def _sigmoid(x):
    return 1.0 / (1.0 + jnp.exp(-x))


def _mm(pairs, out_dtype, name, tm=1024, tn=1024, after=None):
    dims = []
    for a, b, mode in pairs:
        m, k = (a.shape[1], a.shape[0]) if mode == "tn" else a.shape
        n, k2 = b.shape if mode == "nt" else (b.shape[1], b.shape[0])
        assert k == k2, (name, a.shape, b.shape, mode)
        dims.append((m, n))
    assert all(d == dims[0] for d in dims), (name, dims)
    m, n = dims[0]
    tm = _tile(m, tm, 128 if any(mode == "tn" for _, _, mode in pairs) else 16)
    tn = _tile(n, tn, 128)
    in_specs, contract, operands = [], [], []
    for a, b, mode in pairs:
        k = a.shape[0] if mode == "tn" else a.shape[1]
        in_specs.append(pl.BlockSpec((k, tm), lambda i, j: (0, i)) if mode == "tn" else
                        pl.BlockSpec((tm, k), lambda i, j: (i, 0)))
        in_specs.append(pl.BlockSpec((tn, k), lambda i, j: (j, 0)) if mode == "nt" else
                        pl.BlockSpec((k, tn), lambda i, j: (0, j)))
        contract.append((0 if mode == "tn" else 1, 1 if mode == "nt" else 0))
        operands += [a, b]
    if after is not None:
        in_specs.append(pl.BlockSpec(memory_space=pl.ANY))
        operands.append(after)

    def body(*refs):
        o_ref = refs[-1]
        acc = None
        for p, (ca, cb) in enumerate(contract):
            part = _dot(refs[2 * p][...], refs[2 * p + 1][...], ca, cb)
            acc = part if acc is None else acc + part
        o_ref[...] = acc.astype(out_dtype)

    return _pcall(body, name=name, out_shape=jax.ShapeDtypeStruct((m, n), out_dtype), grid=(m // tm, n // tn),
                  in_specs=in_specs, out_specs=pl.BlockSpec((tm, tn), lambda i, j: (i, j)),
                  dims=("parallel", "parallel"))(*operands)


def _rms_fwd(h, gain, name, f=None, scale=0.5):
    t, d = h.shape
    tt = _tile(t, 512, 16)
    row = pl.BlockSpec((tt, d), lambda i: (i, 0))
    vec = pl.BlockSpec((1, d), lambda i: (0, 0))

    def norm(hv, g_ref, n_ref):
        r = lax.rsqrt(jnp.mean(hv * hv, axis=-1, keepdims=True) + EPS)
        n_ref[...] = (hv * r * g_ref[...]).astype(BF16)

    if f is None:

        def body(h_ref, g_ref, n_ref):
            norm(h_ref[...], g_ref, n_ref)

        return _pcall(body, name=name, out_shape=jax.ShapeDtypeStruct((t, d), BF16), grid=(t // tt,),
                      in_specs=[row, vec], out_specs=row, dims=("parallel",))(h, gain)

    def body(h_ref, f_ref, g_ref, hn_ref, n_ref):
        hv = h_ref[...] + scale * f_ref[...]
        hn_ref[...] = hv
        norm(hv, g_ref, n_ref)

    return _pcall(body, name=name,
                  out_shape=(jax.ShapeDtypeStruct((t, d), F32), jax.ShapeDtypeStruct((t, d), BF16)),
                  grid=(t // tt,), in_specs=[row, row, vec], out_specs=(row, row), dims=("parallel",))(h, f, gain)


def _rms_bwd(dn, h, gain, dh_in, name, out_scale):
    t, d = h.shape
    tt = _tile(t, 512, 16)
    row = pl.BlockSpec((tt, d), lambda i: (i, 0))
    vec = pl.BlockSpec((1, d), lambda i: (0, 0))

    def body(dn_ref, h_ref, g_ref, dhin_ref, dh_ref, dhb_ref, dg_ref):
        hv = h_ref[...]
        dnv = dn_ref[...].astype(F32)
        r = lax.rsqrt(jnp.mean(hv * hv, axis=-1, keepdims=True) + EPS)
        dng = dnv * g_ref[...]
        dh = dhin_ref[...] + r * dng - hv * (r * r * r) * jnp.mean(dng * hv, axis=-1, keepdims=True)
        dh_ref[...] = dh
        dhb_ref[...] = (out_scale * dh).astype(BF16)
        part = jnp.sum(dnv * hv * r, axis=0, keepdims=True)

        @pl.when(pl.program_id(0) == 0)
        def _():
            dg_ref[...] = part

        @pl.when(pl.program_id(0) > 0)
        def _():
            dg_ref[...] += part

    return _pcall(body, name=name,
                  out_shape=(jax.ShapeDtypeStruct((t, d), F32), jax.ShapeDtypeStruct((t, d), BF16),
                             jax.ShapeDtypeStruct((1, d), F32)),
                  grid=(t // tt,), in_specs=[row, row, vec, row], out_specs=(row, row, vec),
                  dims=("arbitrary",))(dn, h, gain, dh_in)


def _swiglu_fwd(g, u, name):
    t, f = g.shape
    tt = _tile(t, 512, 16)
    row = pl.BlockSpec((tt, f), lambda i: (i, 0))

    def body(g_ref, u_ref, a_ref):
        gv = g_ref[...].astype(F32)
        a_ref[...] = (gv * _sigmoid(gv) * u_ref[...].astype(F32)).astype(BF16)

    return _pcall(body, name=name, out_shape=jax.ShapeDtypeStruct((t, f), BF16), grid=(t // tt,),
                  in_specs=[row, row], out_specs=row, dims=("parallel",))(g, u)


def _swiglu_bwd(da, g, u, name):
    t, f = g.shape
    tt = _tile(t, 512, 16)
    row = pl.BlockSpec((tt, f), lambda i: (i, 0))

    def body(da_ref, g_ref, u_ref, dg_ref, du_ref):
        gv = g_ref[...].astype(F32)
        dav = da_ref[...].astype(F32)
        sg = _sigmoid(gv)
        dg_ref[...] = (dav * u_ref[...].astype(F32) * (sg * (1.0 + gv * (1.0 - sg)))).astype(BF16)
        du_ref[...] = (dav * (gv * sg)).astype(BF16)

    return _pcall(body, name=name, out_shape=(jax.ShapeDtypeStruct((t, f), BF16),) * 2, grid=(t // tt,),
                  in_specs=[row, row, row], out_specs=(row, row), dims=("parallel",))(da, g, u)


def _rope_tables(pos_col, inv2, sign2):
    t = pos_col.shape[0]

    def body(p_ref, inv_ref, sg_ref, c_ref, s_ref):
        ang = p_ref[...].astype(F32) * inv_ref[...]
        c_ref[...] = jnp.cos(ang)
        s_ref[...] = jnp.sin(ang) * sg_ref[...]

    full = lambda shape: pl.BlockSpec(shape, lambda i: (0, 0))
    return _pcall(body, name="rope_tables", out_shape=(jax.ShapeDtypeStruct((t, RET_DIM), F32),) * 2, grid=(1,),
                  in_specs=[full((t, 1)), full((1, RET_DIM)), full((1, RET_DIM))],
                  out_specs=(full((t, RET_DIM)),) * 2, dims=("arbitrary",))(pos_col, inv2, sign2)


def _rot(x, c, s):
    return x * c + pltpu.roll(x, RET_DIM // 2, 1) * s


def _rot_t(g, c, s):
    return g * c + pltpu.roll(g * s, RET_DIM // 2, 1)


def _ret_consts():
    hh = np.arange(RET_HEADS, dtype=np.float32)
    log_gamma = np.log1p(-np.exp2(-5.0 - hh)).astype(np.float32)
    idx = np.arange(CHUNK, dtype=np.float32)
    diff = idx[:, None] - idx[None, :]
    dmask = np.where(diff >= 0, np.exp(log_gamma[:, None, None] * np.maximum(diff, 0.0)), 0.0).astype(np.float32)
    kdec = np.exp(log_gamma[:, None] * (CHUNK - 1 - idx)).astype(np.float32)
    qdec = np.exp(log_gamma[:, None] * (idx + 1.0)).astype(np.float32)
    cdec = np.exp(log_gamma * CHUNK).astype(np.float32)
    bc = lambda v: np.ascontiguousarray(np.broadcast_to(v[:, :, None], (RET_HEADS, CHUNK, RET_DIM)))
    cd = np.ascontiguousarray(np.broadcast_to(cdec[:, None, None], (RET_HEADS, 8, RET_DIM)))
    return jnp.asarray(dmask), jnp.asarray(bc(qdec)), jnp.asarray(bc(kdec)), jnp.asarray(cd)


def _ret_fwd(pm, cos2, sin2, consts):
    t = pm.shape[0]
    n_chunks = t // CHUNK
    dmask, qdec, kdec, cd = consts
    scale = RET_DIM ** -0.5
    wide = lambda c0: pl.BlockSpec((CHUNK, RET_WIDTH), lambda n: (n, c0 // RET_HEADS))
    tab = pl.BlockSpec((CHUNK, RET_DIM), lambda n: (n, 0))
    const = lambda a: pl.BlockSpec(a.shape, lambda n: (0,) * a.ndim)

    def body(q_ref, k_ref, v_ref, g_ref, c_ref, s_ref, dm_ref, qd_ref, kd_ref, cd_ref, y_ref, raw_ref, st_ref, s_acc):
        @pl.when(pl.program_id(0) == 0)
        def _():
            s_acc[...] = jnp.zeros_like(s_acc)

        c, s = c_ref[...], s_ref[...]
        for h in range(RET_HEADS):
            hs = slice(h * RET_DIM, (h + 1) * RET_DIM)
            q = _rot(q_ref[:, hs], c, s)
            k = _rot(k_ref[:, hs], c, s) * scale
            vb = v_ref[:, hs].astype(BF16)
            g = g_ref[:, hs]
            s_in = s_acc[h]
            st_ref[h] = s_in
            a = _dot(q.astype(BF16), k.astype(BF16), 1, 1) * dm_ref[h]
            y = _dot(a.astype(BF16), vb, 1, 0) + _dot((q * qd_ref[h]).astype(BF16), s_in.astype(BF16), 1, 0)
            s_acc[h] = cd_ref[h, 0:1, :] * s_in + _dot((k * kd_ref[h]).astype(BF16), vb, 0, 0)
            raw_ref[:, hs] = y
            mu = jnp.mean(y, axis=-1, keepdims=True)
            yc = y - mu
            rs = lax.rsqrt(jnp.mean(yc * yc, axis=-1, keepdims=True) + EPS)
            y_ref[:, hs] = (yc * rs * (g * _sigmoid(g))).astype(BF16)

    out_blk = pl.BlockSpec((CHUNK, RET_WIDTH), lambda n: (n, 0))
    return _pcall(
        body, name="retention_fwd",
        out_shape=(jax.ShapeDtypeStruct((t, RET_WIDTH), BF16), jax.ShapeDtypeStruct((t, RET_WIDTH), F32),
                   jax.ShapeDtypeStruct((RET_HEADS, n_chunks, RET_DIM, RET_DIM), F32)),
        grid=(n_chunks,),
        in_specs=[wide(TILE_RQ), wide(TILE_RK), wide(TILE_RV), wide(TILE_RG), tab, tab,
                  const(dmask), const(qdec), const(kdec), const(cd)],
        out_specs=(out_blk, out_blk, pl.BlockSpec((RET_HEADS, None, RET_DIM, RET_DIM), lambda n: (0, n, 0, 0))),
        scratch_shapes=[pltpu.VMEM((RET_HEADS, RET_DIM, RET_DIM), F32)],
        dims=("arbitrary",),
    )(pm, pm, pm, pm, cos2, sin2, dmask, qdec, kdec, cd)


def _ret_bwd(dy, pm, cos2, sin2, raw, states, consts):
    t = pm.shape[0]
    n_chunks = t // CHUNK
    dmask, qdec, kdec, cd = consts
    scale = RET_DIM ** -0.5
    rev = lambda n: n_chunks - 1 - n
    wide = lambda c0: pl.BlockSpec((CHUNK, RET_WIDTH), lambda n: (rev(n), c0 // RET_HEADS))
    tab = pl.BlockSpec((CHUNK, RET_DIM), lambda n: (rev(n), 0))
    blk = pl.BlockSpec((CHUNK, RET_WIDTH), lambda n: (rev(n), 0))
    const = lambda a: pl.BlockSpec(a.shape, lambda n: (0,) * a.ndim)

    def body(dy_ref, q_ref, k_ref, v_ref, g_ref, c_ref, s_ref, raw_ref, st_ref, dm_ref, qd_ref, kd_ref, cd_ref,
             dq_ref, dk_ref, dv_ref, dg_ref, ds_acc):
        @pl.when(pl.program_id(0) == 0)
        def _():
            ds_acc[...] = jnp.zeros_like(ds_acc)

        c, s = c_ref[...], s_ref[...]
        for h in range(RET_HEADS):
            hs = slice(h * RET_DIM, (h + 1) * RET_DIM)
            q = _rot(q_ref[:, hs], c, s)
            k = _rot(k_ref[:, hs], c, s) * scale
            qb, kb, vb = q.astype(BF16), k.astype(BF16), v_ref[:, hs].astype(BF16)
            g = g_ref[:, hs]
            dm, qd, kd = dm_ref[h], qd_ref[h], kd_ref[h]
            y = raw_ref[:, hs]
            mu = jnp.mean(y, axis=-1, keepdims=True)
            yc = y - mu
            rs = lax.rsqrt(jnp.mean(yc * yc, axis=-1, keepdims=True) + EPS)
            yn = yc * rs
            sg = _sigmoid(g)
            dyo = dy_ref[:, hs]
            dg_ref[:, hs] = (dyo * yn * (sg * (1.0 + g * (1.0 - sg)))).astype(BF16)
            dyn = dyo * (g * sg)
            dyr = rs * (dyn - jnp.mean(dyn, axis=-1, keepdims=True) - yn * jnp.mean(dyn * yn, axis=-1, keepdims=True))
            dyb = dyr.astype(BF16)
            s_in = st_ref[h].astype(BF16)
            ds_out = ds_acc[h]
            dsb = ds_out.astype(BF16)
            a = _dot(qb, kb, 1, 1) * dm
            da = (_dot(dyb, vb, 1, 1) * dm).astype(BF16)
            kdb = (k * kd).astype(BF16)
            qdb = (q * qd).astype(BF16)
            dv_ref[:, hs] = (_dot(a.astype(BF16), dyb, 0, 0) + _dot(kdb, dsb, 1, 0)).astype(BF16)
            dqh = _dot(da, kb, 1, 0) + _dot(dyb, s_in, 1, 1) * qd
            dkh = _dot(da, qb, 0, 0) + _dot(vb, dsb, 1, 1) * kd
            ds_acc[h] = cd_ref[h, 0:1, :] * ds_out + _dot(qdb, dyb, 0, 0)
            dq_ref[:, hs] = _rot_t(dqh, c, s).astype(BF16)
            dk_ref[:, hs] = (_rot_t(dkh, c, s) * scale).astype(BF16)

    return _pcall(
        body, name="retention_bwd",
        out_shape=(jax.ShapeDtypeStruct((t, RET_WIDTH), BF16),) * 4,
        grid=(n_chunks,),
        in_specs=[blk, wide(TILE_RQ), wide(TILE_RK), wide(TILE_RV), wide(TILE_RG), tab, tab, blk,
                  pl.BlockSpec((RET_HEADS, None, RET_DIM, RET_DIM), lambda n: (0, rev(n), 0, 0)),
                  const(dmask), const(qdec), const(kdec), const(cd)],
        out_specs=(blk,) * 4,
        scratch_shapes=[pltpu.VMEM((RET_HEADS, RET_DIM, RET_DIM), F32)],
        dims=("arbitrary",),
    )(dy, pm, pm, pm, pm, cos2, sin2, raw, states, dmask, qdec, kdec, cd)


FOX_C_LANE = FOX_DIM
FOX_NEGC_LANE = FOX_DIM + 3
FOX_LSE_LANE = FOX_DIM + 6
FOX_L_LANE = FOX_C_LANE
FOX_ROWSUM_LANE = FOX_C_LANE
FOX_COLSUM_LANE = FOX_NEGC_LANE


def _split3(x):
    hi = x.astype(BF16)
    r1 = x - hi.astype(F32)
    mid = r1.astype(BF16)
    lo = (r1 - mid.astype(F32)).astype(BF16)
    return hi, mid, lo


def _tri_dot(tri, x):
    hi, mid, lo = _split3(x)
    return _dot(tri, lo, 1, 0) + _dot(tri, mid, 1, 0) + _dot(tri, hi, 1, 0)


def _log_sigmoid(z):
    return jnp.minimum(z, 0.0) - jnp.log1p(jnp.exp(-jnp.abs(z)))


def _fox_consts():
    place = np.zeros((2, 3, LANES, FOX_TILES), np.float32)
    ones = np.zeros((3, 1, FOX_TILES), np.float32)
    for h in range(FOX_HEADS):
        for part in range(3):
            place[0, part, h, LANES * h + FOX_C_LANE + part] = 1.0
            place[1, part, h, LANES * h + FOX_NEGC_LANE + part] = -1.0
            ones[0, 0, LANES * h + FOX_NEGC_LANE + part] = 1.0
            ones[1, 0, LANES * h + FOX_C_LANE + part] = 1.0
            ones[1, 0, LANES * h + FOX_LSE_LANE + part] = 1.0
            ones[2, 0, LANES * h + FOX_C_LANE + part] = 1.0
    return jnp.asarray(place, BF16), jnp.asarray(ones, F32)


def _fox_prep(pm, bpad):
    t = pm.shape[0]
    tt = _tile(t, 512, LANES)
    place, ones = _fox_consts()
    wide = lambda c0: pl.BlockSpec((tt, FOX_TILES), lambda i: (i, c0 // FOX_HEADS))
    const = lambda a: pl.BlockSpec(a.shape, lambda i: (0,) * a.ndim)

    def body(q_ref, k_ref, v_ref, ff_ref, b_ref, pl_ref, on_ref, qa_ref, ka_ref, va_ref, carry_s):
        @pl.when(pl.program_id(0) == 0)
        def _():
            carry_s[...] = jnp.zeros_like(carry_s)

        r = lax.broadcasted_iota(jnp.int32, (LANES, LANES), 0)
        cc = lax.broadcasted_iota(jnp.int32, (LANES, LANES), 1)
        tri = jnp.where(cc <= r, 1.0, 0.0).astype(BF16)
        bias = b_ref[...]
        for sub in range(tt // LANES):
            rows = pl.ds(sub * LANES, LANES)
            cs = _tri_dot(tri, _log_sigmoid(ff_ref[rows, :] + bias)) + carry_s[...]
            carry_s[...] = cs[LANES - 1:LANES, :]
            parts = _split3(cs)
            eq = sum(_dot(part, pl_ref[0, i], 1, 0) for i, part in enumerate(parts))
            ek = sum(_dot(part, pl_ref[1, i], 1, 0) for i, part in enumerate(parts))
            qa_ref[rows, :] = (q_ref[rows, :] * FOX_DIM ** -0.5 + eq + on_ref[0]).astype(BF16)
            ka_ref[rows, :] = (k_ref[rows, :] + ek + on_ref[1]).astype(BF16)
            va_ref[rows, :] = (v_ref[rows, :] + on_ref[2]).astype(BF16)

    out = pl.BlockSpec((tt, FOX_TILES), lambda i: (i, 0))
    return _pcall(body, name="fox_prep", out_shape=(jax.ShapeDtypeStruct((t, FOX_TILES), BF16),) * 3, grid=(t // tt,),
                  in_specs=[wide(TILE_FQ), wide(TILE_FK), wide(TILE_FV), pl.BlockSpec((tt, LANES), lambda i: (i, TILE_FF)),
                            pl.BlockSpec((1, LANES), lambda i: (0, 0)), const(place), const(ones)],
                  out_specs=(out,) * 3, scratch_shapes=[pltpu.VMEM((1, LANES), F32)],
                  dims=("arbitrary",))(pm, pm, pm, pm, bpad, place, ones)


def _fox_post(dc, pm, bpad):
    t = pm.shape[0]
    nb = t // LANES

    def body(dc_ref, ff_ref, b_ref, d_ref, db_ref):
        r = lax.broadcasted_iota(jnp.int32, (LANES, LANES), 0)
        cc = lax.broadcasted_iota(jnp.int32, (LANES, LANES), 1)
        tri = jnp.where(cc >= r, 1.0, 0.0).astype(BF16)
        bias = b_ref[...]

        def step(i, carry):
            tail, acc = carry
            rows = pl.ds(pl.multiple_of((nb - 1 - i) * LANES, LANES), LANES)
            cs = _tri_dot(tri, dc_ref[rows, :]) + tail
            dff = cs * _sigmoid(-(ff_ref[rows, :] + bias))
            d_ref[rows, :] = dff.astype(BF16)
            return cs[0:1, :], acc + jnp.sum(dff, axis=0, keepdims=True)

        zero = jnp.zeros((1, LANES), F32)
        _, acc = lax.fori_loop(0, nb, step, (zero, zero))
        db_ref[...] = acc

    return _pcall(body, name="fox_forget_bwd",
                  out_shape=(jax.ShapeDtypeStruct((t, LANES), BF16), jax.ShapeDtypeStruct((1, LANES), F32)), grid=(1,),
                  in_specs=[pl.BlockSpec((t, LANES), lambda i: (0, 0)), pl.BlockSpec((t, LANES), lambda i: (0, TILE_FF)),
                            pl.BlockSpec((1, LANES), lambda i: (0, 0))],
                  out_specs=(pl.BlockSpec((t, LANES), lambda i: (0, 0)), pl.BlockSpec((1, LANES), lambda i: (0, 0))),
                  dims=("arbitrary",))(dc, pm, bpad)


def _tri_tables(nb, q_major):
    pairs = [(i, j) for i in range(nb) for j in range(i + 1)] if q_major else \
            [(i, j) for j in range(nb) for i in range(j, nb)]
    return jnp.asarray([a for a, _ in pairs], jnp.int32), jnp.asarray([b for _, b in pairs], jnp.int32)


def _causal(s):
    n = s.shape[0]
    row = lax.broadcasted_iota(jnp.int32, (n, n), 0)
    col = lax.broadcasted_iota(jnp.int32, (n, n), 1)
    return jnp.where(col <= row, s, NEG)


def _lane_col(x, lane):
    sel = lax.broadcasted_iota(jnp.int32, x.shape, 1) == lane
    return jnp.sum(jnp.where(sel, x, 0.0), axis=1, keepdims=True)


def _fox_fwd(qa, ka, va, blk):
    t = qa.shape[0]
    nb = t // blk
    qi, kj = _tri_tables(nb, True)
    q_spec = pl.BlockSpec((blk, LANES), lambda h, s, qi_r, kj_r: (qi_r[s], h))
    k_spec = pl.BlockSpec((blk, LANES), lambda h, s, qi_r, kj_r: (kj_r[s], h))

    def body(qi_r, kj_r, q_ref, k_ref, v_ref, o_ref, ob_ref, qb_ref, m_s, acc_s):
        s_id = pl.program_id(1)
        i, j = qi_r[s_id], kj_r[s_id]

        @pl.when(j == 0)
        def _():
            m_s[...] = jnp.full_like(m_s, NEG)
            acc_s[...] = jnp.zeros_like(acc_s)

        def tile(diagonal):
            s = _dot(q_ref[...], k_ref[...], 1, 1)
            if diagonal:
                s = _causal(s)
            m_old = m_s[...]
            m_new = jnp.maximum(m_old, jnp.max(s, axis=1, keepdims=True))
            p = jnp.exp(s - jnp.tile(m_new, (1, blk // LANES)))
            acc_s[...] = jnp.exp(m_old - m_new) * acc_s[...] + _dot(p.astype(BF16), v_ref[...], 1, 0)
            m_s[...] = m_new

        @pl.when(j < i)
        def _():
            tile(False)

        @pl.when(j == i)
        def _():
            tile(True)
            acc = acc_s[...]
            l = _lane_col(acc, FOX_L_LANE)
            o = acc / l
            o_ref[...] = o
            ob_ref[...] = o.astype(BF16)
            hi, mid, lo = _split3(-(m_s[:, 0:1] + jnp.log(l)))
            lane = lax.broadcasted_iota(jnp.int32, acc.shape, 1)
            qb_ref[...] = jnp.where(lane == FOX_LSE_LANE, hi,
                                    jnp.where(lane == FOX_LSE_LANE + 1, mid,
                                              jnp.where(lane == FOX_LSE_LANE + 2, lo, q_ref[...])))

    wide = (t, FOX_TILES)
    return pl.pallas_call(
        body, name="fox_fwd",
        out_shape=(jax.ShapeDtypeStruct(wide, F32), jax.ShapeDtypeStruct(wide, BF16), jax.ShapeDtypeStruct(wide, BF16)),
        grid_spec=pltpu.PrefetchScalarGridSpec(
            num_scalar_prefetch=2, grid=(FOX_HEADS, qi.shape[0]), in_specs=[q_spec, k_spec, k_spec],
            out_specs=(q_spec,) * 3,
            scratch_shapes=[pltpu.VMEM((blk, LANES), F32), pltpu.VMEM((blk, LANES), F32)]),
        compiler_params=_params(("parallel", "arbitrary")),
    )(qi, kj, qa, ka, va)


def _fox_bwd(qa, ka, va, do, o, blk):
    t = qa.shape[0]
    nb = t // blk
    qi, kj = _tri_tables(nb, False)
    q_spec = pl.BlockSpec((blk, LANES), lambda h, s, qi_r, kj_r: (qi_r[s], h))
    k_spec = pl.BlockSpec((blk, LANES), lambda h, s, qi_r, kj_r: (kj_r[s], h))
    head_spec = pl.BlockSpec((t, LANES), lambda h, s, qi_r, kj_r: (0, h))
    head_col = pl.BlockSpec((None, t, 1), lambda h, s, qi_r, kj_r: (h, 0, 0))
    k_col = pl.BlockSpec((None, blk, 1), lambda h, s, qi_r, kj_r: (h, kj_r[s], 0))
    first_spec = pl.BlockSpec((blk, LANES), lambda h, s, qi_r, kj_r: (jnp.where(kj_r[s] == 0, qi_r[s], nb - 1), h))
    n_steps = int(qi.shape[0])

    def body(qi_r, kj_r, q_ref, k_ref, v_ref, do_ref, o_ref, dq_ref, dk_ref, dv_ref, rs_ref, cs_ref,
             doa_s, dq_s, dk_s, dv_s):
        s_id = pl.program_id(1)
        i, j = qi_r[s_id], kj_r[s_id]
        rows = pl.ds(pl.multiple_of(i * blk, blk), blk)

        @pl.when(j == 0)
        def _():
            dof = do_ref[...]
            hi, mid, lo = _split3(-jnp.sum(dof * o_ref[...], axis=1, keepdims=True))
            lane = lax.broadcasted_iota(jnp.int32, dof.shape, 1)
            doa = jnp.where(lane == FOX_C_LANE, hi.astype(F32),
                            jnp.where(lane == FOX_C_LANE + 1, mid.astype(F32),
                                      jnp.where(lane == FOX_C_LANE + 2, lo.astype(F32), dof)))
            doa_s[rows, :] = doa.astype(BF16)
            dq_s[rows, :] = jnp.zeros((blk, LANES), F32)

        @pl.when(i == j)
        def _():
            dk_s[...] = jnp.zeros_like(dk_s)
            dv_s[...] = jnp.zeros_like(dv_s)

        def tile(diagonal):
            q, k = q_ref[...], k_ref[...]
            s = _dot(q, k, 1, 1)
            if diagonal:
                s = _causal(s)
            p = jnp.exp(s)
            doa = doa_s[rows, :]
            ds = (p * _dot(doa, v_ref[...], 1, 1)).astype(BF16)
            dv_s[...] += _dot(p.astype(BF16), doa, 0, 0)
            dk_s[...] += _dot(ds, q, 0, 0)
            dq_s[rows, :] += _dot(ds, k, 1, 0)

        @pl.when(i > j)
        def _():
            tile(False)

        @pl.when(i == j)
        def _():
            tile(True)

        @pl.when(i == nb - 1)
        def _():
            dk = dk_s[...]
            dk_ref[...] = dk.astype(BF16)
            dv_ref[...] = dv_s[...].astype(BF16)
            cs_ref[...] = -_lane_col(dk, FOX_COLSUM_LANE)

        @pl.when(s_id == n_steps - 1)
        def _():
            dq = dq_s[...]
            dq_ref[...] = (dq * FOX_DIM ** -0.5).astype(BF16)
            rs_ref[...] = _lane_col(dq, FOX_ROWSUM_LANE)

    wide = jax.ShapeDtypeStruct((t, FOX_TILES), BF16)
    cols = jax.ShapeDtypeStruct((FOX_HEADS, t, 1), F32)
    return pl.pallas_call(
        body, name="fox_bwd", out_shape=(wide, wide, wide, cols, cols),
        grid_spec=pltpu.PrefetchScalarGridSpec(
            num_scalar_prefetch=2, grid=(FOX_HEADS, n_steps),
            in_specs=[q_spec, k_spec, k_spec, first_spec, first_spec],
            out_specs=(head_spec, k_spec, k_spec, head_col, k_col),
            scratch_shapes=[pltpu.VMEM((t, LANES), BF16), pltpu.VMEM((t, LANES), F32), pltpu.VMEM((blk, LANES), F32),
                            pltpu.VMEM((blk, LANES), F32)]),
        compiler_params=_params(("parallel", "arbitrary")),
    )(qi, kj, qa, ka, va, do, o)


def _merge_fwd(gm, bm, za, zb):
    t, d = za.shape
    tt = _tile(t, 512, 16)
    row = pl.BlockSpec((tt, d), lambda i: (i, 0))

    def body(gm_ref, b_ref, za_ref, zb_ref, o_ref):
        ga = _sigmoid(gm_ref[:, :d] + b_ref[:, :d])
        gb = _sigmoid(gm_ref[:, d:] + b_ref[:, d:])
        o_ref[...] = (ga * za_ref[...] + gb * zb_ref[...]).astype(BF16)

    return _pcall(body, name="merge_fwd", out_shape=jax.ShapeDtypeStruct((t, d), BF16), grid=(t // tt,),
                  in_specs=[pl.BlockSpec((tt, 2 * d), lambda i: (i, 0)), pl.BlockSpec((1, 2 * d), lambda i: (0, 0)), row, row],
                  out_specs=row, dims=("parallel",))(gm, bm, za, zb)


def _merge_bwd(dmix, gm, bm, za, zb):
    t, d = za.shape
    tt = _tile(t, 512, 16)
    row = pl.BlockSpec((tt, d), lambda i: (i, 0))
    wide = pl.BlockSpec((tt, 2 * d), lambda i: (i, 0))
    vec = pl.BlockSpec((1, 2 * d), lambda i: (0, 0))

    def body(dm_ref, gm_ref, b_ref, za_ref, zb_ref, dza_ref, dzb_ref, dgm_ref, db_ref):
        dm = dm_ref[...]
        ga = _sigmoid(gm_ref[:, :d] + b_ref[:, :d])
        gb = _sigmoid(gm_ref[:, d:] + b_ref[:, d:])
        dza_ref[...] = (dm * ga).astype(BF16)
        dzb_ref[...] = (dm * gb).astype(BF16)
        dla = dm * za_ref[...] * ga * (1.0 - ga)
        dlb = dm * zb_ref[...] * gb * (1.0 - gb)
        dgm_ref[:, :d] = dla.astype(BF16)
        dgm_ref[:, d:] = dlb.astype(BF16)
        pa = jnp.sum(dla, axis=0, keepdims=True)
        pb = jnp.sum(dlb, axis=0, keepdims=True)

        @pl.when(pl.program_id(0) == 0)
        def _():
            db_ref[:, :d] = pa
            db_ref[:, d:] = pb

        @pl.when(pl.program_id(0) > 0)
        def _():
            db_ref[:, :d] += pa
            db_ref[:, d:] += pb

    return _pcall(body, name="merge_bwd",
                  out_shape=(jax.ShapeDtypeStruct((t, d), BF16), jax.ShapeDtypeStruct((t, d), BF16),
                             jax.ShapeDtypeStruct((t, 2 * d), BF16), jax.ShapeDtypeStruct((1, 2 * d), F32)),
                  grid=(t // tt,), in_specs=[row, wide, vec, row, row], out_specs=(row, row, wide, vec),
                  dims=("arbitrary",))(dmix, gm, bm, za, zb)


def _ple_final(h3, pgl, pe, gain, target):
    t, d = h3.shape
    tt = _tile(t, 512, 16)
    row = pl.BlockSpec((tt, d), lambda i: (i, 0))
    vec = pl.BlockSpec((1, d), lambda i: (0, 0))
    lvec = pl.BlockSpec((1, LANES), lambda i: (0, 0))

    def body(h_ref, pgl_ref, pe_ref, g_ref, t_ref, dh_ref, dsg_ref, dpe_ref, loss_ref, dg_ref):
        pg = _sigmoid(pgl_ref[...])
        pe_v = pe_ref[...]
        h4 = h_ref[...] + pg * pe_v
        r = lax.rsqrt(jnp.mean(h4 * h4, axis=-1, keepdims=True) + EPS)
        gv = g_ref[...]
        err = h4 * r * gv - t_ref[...]
        part_loss = 0.5 * jnp.sum(jnp.mean(err * err, axis=-1, keepdims=True), axis=0, keepdims=True)
        dy = err * (1.0 / d)
        part_g = jnp.sum(dy * h4 * r, axis=0, keepdims=True)
        dyg = dy * gv
        dh = r * dyg - h4 * (r * r * r) * jnp.mean(dyg * h4, axis=-1, keepdims=True)
        dh_ref[...] = dh
        dsg_ref[...] = (dh * pe_v * pg * (1.0 - pg)).astype(BF16)
        dpe_ref[...] = (dh * pg).astype(BF16)

        @pl.when(pl.program_id(0) == 0)
        def _():
            loss_ref[...] = jnp.broadcast_to(part_loss, (1, LANES))
            dg_ref[...] = part_g

        @pl.when(pl.program_id(0) > 0)
        def _():
            loss_ref[...] += jnp.broadcast_to(part_loss, (1, LANES))
            dg_ref[...] += part_g

    return _pcall(body, name="ple_final",
                  out_shape=(jax.ShapeDtypeStruct((t, d), F32), jax.ShapeDtypeStruct((t, d), BF16),
                             jax.ShapeDtypeStruct((t, d), BF16), jax.ShapeDtypeStruct((1, LANES), F32),
                             jax.ShapeDtypeStruct((1, d), F32)),
                  grid=(t // tt,), in_specs=[row, row, row, vec, row], out_specs=(row, row, row, lvec, vec),
                  dims=("arbitrary",))(h3, pgl, pe, gain, target)


def _adamw_math(w, g, m, v):
    m = ADAM_B1 * m + (1.0 - ADAM_B1) * g
    v = ADAM_B2 * v + (1.0 - ADAM_B2) * (g * g)
    m_hat = m / (1.0 - ADAM_B1 ** ADAM_STEP)
    v_hat = v / (1.0 - ADAM_B2 ** ADAM_STEP)
    delta = -ADAM_LR * (m_hat / (jnp.sqrt(v_hat) + ADAM_EPS) + ADAM_WD * w)
    return delta, m, v


def _adamw(parts, w, m, v, name):
    n, r, c = parts.shape
    tr = _tile(r, 256, 16)
    row = pl.BlockSpec((tr, c), lambda i: (i, 0))

    def body(p_ref, w_ref, m_ref, v_ref, g_ref, d_ref, mo_ref, vo_ref):
        g = p_ref[0].astype(F32)
        for s in range(1, n):
            g = g + p_ref[s].astype(F32)
        g_ref[...] = g
        d_ref[...], mo_ref[...], vo_ref[...] = _adamw_math(w_ref[...], g, m_ref[...], v_ref[...])

    return _pcall(body, name=name, out_shape=(jax.ShapeDtypeStruct((r, c), F32),) * 4, grid=(r // tr,),
                  in_specs=[pl.BlockSpec((n, tr, c), lambda i: (0, i, 0)), row, row, row], out_specs=(row,) * 4,
                  dims=("parallel",))(parts, w, m, v)


ANY = pl.BlockSpec(memory_space=pl.ANY)


def _all_gather(shards):
    n = len(shards)

    def body(*refs):
        x_refs, out_refs = refs[:n], refs[n:2 * n]
        send_sems, recv_sems, local_sems = refs[2 * n:]
        x, y, cc = lax.axis_index("x"), lax.axis_index("y"), lax.axis_index("c")
        me, sibling = (x, y, cc), (x, y, 1 - cc)
        chips = [(1 - x, y), (x, 1 - y), (1 - x, 1 - y)]

        def slot(a, px, py, pc):
            return out_refs[a].at[4 * px + 2 * py + pc]

        def copy(a, k, block, to, src=None):
            return pltpu.make_async_remote_copy(
                src_ref=slot(a, *block) if src is None else src, dst_ref=slot(a, *block),
                send_sem=send_sems.at[7 * a + k], recv_sem=recv_sems.at[7 * a + k], device_id=to, device_id_type=MESH)

        local, sent = [], []
        for a in range(n):
            local.append(pltpu.make_async_copy(x_refs[a], slot(a, *me), local_sems.at[a]))
            sent.append(copy(a, 0, me, sibling, src=x_refs[a]))
            sent += [copy(a, 1 + j, me, (*chip, cc), src=x_refs[a]) for j, chip in enumerate(chips)]
        for cp in local + sent:
            cp.start()
        for j, chip in enumerate(chips):
            for a in range(n):
                copy(a, 1 + j, (*chip, cc), me).wait_recv()
                sent.append(copy(a, 4 + j, (*chip, cc), sibling))
                sent[-1].start()
        for a in range(n):
            copy(a, 0, sibling, me).wait_recv()
            for j, chip in enumerate(chips):
                copy(a, 4 + j, (*chip, 1 - cc), me).wait_recv()
        for cp in sent:
            cp.wait_send()
        for cp in local:
            cp.wait()

    return pl.pallas_call(
        body, name="weights_all_gather",
        out_shape=tuple(jax.ShapeDtypeStruct((N_DEV,) + s.shape, s.dtype) for s in shards),
        in_specs=[ANY] * n, out_specs=(ANY,) * n,
        scratch_shapes=[pltpu.SemaphoreType.DMA((7 * n,)), pltpu.SemaphoreType.DMA((7 * n,)),
                        pltpu.SemaphoreType.DMA((n,))],
    )(*shards)


def _reduce_scatter_exchange(blocks):
    n = len(blocks)

    def body(*refs):
        g_refs, recv_refs = refs[:n], refs[n:2 * n]
        send_sems, recv_sems, local_sems = refs[2 * n:]
        x, y, cc = lax.axis_index("x"), lax.axis_index("y"), lax.axis_index("c")
        me = 4 * x + 2 * y + cc
        local, sent, landing = [], [], []
        for a in range(n):
            local.append(pltpu.make_async_copy(g_refs[a].at[me], recv_refs[a].at[me], local_sems.at[a]))
        for k in range(1, N_DEV):
            px, py, pc = x ^ (k >> 2), y ^ ((k >> 1) & 1), cc ^ (k & 1)
            peer = 4 * px + 2 * py + pc
            for a in range(n):
                sems = dict(send_sem=send_sems.at[7 * a + k - 1], recv_sem=recv_sems.at[7 * a + k - 1],
                            device_id=(px, py, pc), device_id_type=MESH)
                sent.append(pltpu.make_async_remote_copy(src_ref=g_refs[a].at[peer], dst_ref=recv_refs[a].at[me], **sems))
                landing.append(pltpu.make_async_remote_copy(src_ref=g_refs[a].at[me], dst_ref=recv_refs[a].at[peer], **sems))
        for cp in local + sent:
            cp.start()
        for cp in landing:
            cp.wait_recv()
        for cp in sent:
            cp.wait_send()
        for cp in local:
            cp.wait()

    return pl.pallas_call(
        body, name="grads_reduce_scatter_exchange",
        out_shape=tuple(jax.ShapeDtypeStruct(b.shape, b.dtype) for b in blocks),
        in_specs=[ANY] * n, out_specs=(ANY,) * n,
        scratch_shapes=[pltpu.SemaphoreType.DMA((7 * n,)), pltpu.SemaphoreType.DMA((7 * n,)),
                        pltpu.SemaphoreType.DMA((n,))],
    )(*blocks)


HBM = pl.BlockSpec(memory_space=pltpu.HBM)
SEM = pl.BlockSpec(memory_space=pltpu.SEMAPHORE)
DATAFLOW = pltpu.SideEffectType.DATAFLOW_SIDE_EFFECTING


def _peers():
    x, y, cc = lax.axis_index("x"), lax.axis_index("y"), lax.axis_index("c")
    out = []
    for k in range(1, N_DEV):
        px, py, pc = x ^ (k >> 2), y ^ ((k >> 1) & 1), cc ^ (k & 1)
        out.append((k, (px, py, pc), 4 * px + 2 * py + pc))
    return 4 * x + 2 * y + cc, out


def _scatter_start(blocks, after, name, gather=False):
    n = len(blocks)
    lands = [lax.empty((N_DEV,) + b.shape if gather else b.shape, b.dtype) for b in blocks]

    def body(*refs):
        g_refs, land_refs = refs[:n], refs[n:2 * n]
        send_sems, recv_sems, token = refs[2 * n + 1], refs[2 * n + 2], refs[-1]
        me, peers = _peers()
        for k, peer, slot in peers:
            for a in range(n):
                pltpu.make_async_remote_copy(
                    src_ref=g_refs[a] if gather else g_refs[a].at[slot], dst_ref=land_refs[a].at[me],
                    send_sem=send_sems.at[7 * a + k - 1],
                    recv_sem=recv_sems.at[7 * a + k - 1], device_id=peer, device_id_type=MESH).start()
        token[...] = jnp.zeros_like(token)

    thru = [pltpu.HBM(b.shape, b.dtype) for b in blocks]
    thru_lands = [pltpu.HBM(b.shape, b.dtype) for b in lands]
    return pl.pallas_call(
        body, name=name,
        out_shape=(pltpu.SemaphoreType.DMA((7 * n,)), pltpu.SemaphoreType.DMA((7 * n,)), *thru, *thru_lands,
                   jax.ShapeDtypeStruct((8, LANES), F32)),
        in_specs=[HBM] * (2 * n) + [pl.BlockSpec(memory_space=pl.ANY)],
        out_specs=(SEM, SEM, *[HBM] * (2 * n), pl.BlockSpec(memory_space=pltpu.VMEM)),
        input_output_aliases={i: 2 + i for i in range(2 * n)},
        compiler_params=pltpu.CompilerParams(has_side_effects=DATAFLOW),
    )(*[pltpu.with_memory_space_constraint(a, pltpu.HBM) for a in list(blocks) + lands], after)


def _scatter_wait(started, after, name, gather=False):
    send_sems, recv_sems, *rest = started
    n = (len(rest) - 1) // 2
    thru = rest[:2 * n]

    def body(*refs):
        g_refs, land_refs = refs[:n], refs[n:2 * n]
        send_sems, recv_sems = refs[2 * n], refs[2 * n + 1]
        me, peers = _peers()
        for k, peer, slot in peers:
            for a in range(n):
                copy = pltpu.make_async_remote_copy(
                    src_ref=g_refs[a] if gather else g_refs[a].at[slot], dst_ref=land_refs[a].at[slot],
                    send_sem=send_sems.at[7 * a + k - 1],
                    recv_sem=recv_sems.at[7 * a + k - 1], device_id=peer, device_id_type=MESH)
                copy.wait_send()
                copy.wait_recv()

    out = pl.pallas_call(
        body, name=name, out_shape=tuple(pltpu.HBM(a.shape, a.dtype) for a in thru),
        in_specs=[HBM] * (2 * n) + [SEM, SEM, pl.BlockSpec(memory_space=pl.ANY)], out_specs=tuple([HBM] * (2 * n)),
        input_output_aliases={i: i for i in range(2 * n)},
        compiler_params=pltpu.CompilerParams(has_side_effects=DATAFLOW),
    )(*thru, send_sems, recv_sems, after)
    return out[:n], out[n:]


BIG = (("w_ffn1_gate", "colT"), ("w_ffn1_up", "colT"), ("w_ffn1_down", "row"), ("w_ffn2_gate", "colT"),
       ("w_ffn2_up", "colT"), ("w_ffn2_down", "row"), ("w_in", "colT"), ("w_merge", "col"), ("w_ret_out", "col"),
       ("w_fox_out", "col"), ("w_out", "row"), ("w_ple", "col"), ("w_ple_gate", "row"))


def _shard_view(a, kind):
    a = a.reshape(a.shape[-2:])
    return a.T if kind == "colT" else a


def _unview(a, kind, shape):
    return (a.T if kind == "colT" else a).reshape(shape)


def _full_from_slots(g, kind):
    n, r, c = g.shape
    return g.transpose(1, 0, 2).reshape(r, n * c) if kind == "col" else g.reshape(n * r, c)


def _slots_from_full(f, kind):
    r, c = f.shape
    return f.reshape(r, N_DEV, c // N_DEV).transpose(1, 0, 2) if kind == "col" else f.reshape(N_DEV, r // N_DEV, c)


EARLY_GROUPS = (("w_ple_gate", "w_ple", "w_ffn2_down", "w_ffn2_gate", "w_ffn2_up", "w_out", "w_ret_out", "w_fox_out"),
                ("w_in", "w_merge"))


def _scatter_group(gw, names, after, name):
    kind = dict(BIG)
    return names, _scatter_start([_slots_from_full(gw[n], kind[n]) for n in names], after, name)


def _pad_heads(w):
    d = w.shape[1]
    return jnp.pad(w.reshape(FOX_HEADS, FOX_DIM, d), ((0, 0), (0, LANES - FOX_DIM), (0, 0))).reshape(FOX_TILES, d)


def _unpad_heads(w):
    d = w.shape[1]
    return w.reshape(FOX_HEADS, LANES, d)[:, :FOX_DIM].reshape(FOX_WIDTH, d)


def _deinterleave_rows(w):
    d = w.shape[1]
    return w.reshape(RET_HEADS, RET_DIM // 2, 2, d).transpose(0, 2, 1, 3).reshape(RET_WIDTH, d)


def _interleave_rows(w):
    d = w.shape[1]
    return w.reshape(RET_HEADS, 2, RET_DIM // 2, d).transpose(0, 2, 1, 3).reshape(RET_WIDTH, d)


def _pad_w_in(wt):
    d = wt.shape[1]
    rw, fw = RET_WIDTH, FOX_WIDTH
    fo = 4 * rw
    return jnp.concatenate([
        _deinterleave_rows(wt[:rw]), _deinterleave_rows(wt[rw:2 * rw]), wt[2 * rw:4 * rw],
        _pad_heads(wt[fo:fo + fw]), _pad_heads(wt[fo + fw:fo + 2 * fw]), _pad_heads(wt[fo + 2 * fw:fo + 3 * fw]),
        wt[fo + 3 * fw:], jnp.zeros((2 * LANES - FOX_HEADS, d), wt.dtype)], axis=0)


def _unpad_w_in(g):
    rw = RET_WIDTH
    f0 = 4 * rw
    return jnp.concatenate([
        _interleave_rows(g[:rw]), _interleave_rows(g[rw:2 * rw]), g[2 * rw:4 * rw],
        _unpad_heads(g[f0:f0 + FOX_TILES]), _unpad_heads(g[f0 + FOX_TILES:f0 + 2 * FOX_TILES]),
        _unpad_heads(g[f0 + 2 * FOX_TILES:f0 + 3 * FOX_TILES]),
        g[f0 + 3 * FOX_TILES:f0 + 3 * FOX_TILES + FOX_HEADS]], axis=0)


SMALL = ("ln_ffn1", "ln_mix", "b_forget", "b_merge", "ln_ffn2", "ln_ple", "ln_final")


def _small_rows(n):
    rows = -(-n // LANES)
    return -(-rows // 8) * 8


def _pack_small(vals, with_loss=None):
    parts = []
    for name in SMALL:
        v = vals[name].reshape(-1).astype(F32)
        rows = _small_rows(v.shape[0])
        parts.append(jnp.pad(v, (0, rows * LANES - v.shape[0])).reshape(rows, LANES))
    if with_loss is not None:
        parts.append(jnp.pad(with_loss.reshape(1, LANES), ((0, 7), (0, 0))))
    else:
        parts.append(jnp.zeros((8, LANES), F32))
    return jnp.concatenate(parts, axis=0)


def _unpack_small(packed, shapes):
    out, at = {}, 0
    for name in SMALL:
        n = int(np.prod(shapes[name]))
        rows = _small_rows(n)
        out[name] = packed[at:at + rows].reshape(-1)[:n].reshape(shapes[name])
        at += rows
    return out, packed[at, 0]


def _gather_finish(group, after, me):
    names, started, wait_name = group
    kind = dict(BIG)
    sent, landed = _scatter_wait(started, after, wait_name, gather=True)
    return {n: _full_from_slots(lax.dynamic_update_slice_in_dim(land, shard[None], me, 0), kind[n])
            for n, shard, land in zip(names, sent, landed)}


def _local_step(x, p, positions, target, w, small, me, entry_token, gathers):
    t, d = x.shape
    gain = lambda n: small[n].reshape(1, d)
    w = dict(w)
    bpad = jnp.pad(small["b_forget"].reshape(1, FOX_HEADS), ((0, 0), (0, LANES - FOX_HEADS)))
    bm = small["b_merge"].reshape(1, 2 * d)
    fox_blk = _tile(t, 1024, 128)

    def ffn_fwd(n, tag, down_gather=None):
        g = _mm([(n, w[f"w_{tag}_gate"], "nt")], BF16, f"{tag}_gate", tn=1408)
        u = _mm([(n, w[f"w_{tag}_up"], "nt")], BF16, f"{tag}_up", tn=1408)
        a = _swiglu_fwd(g, u, f"{tag}_swiglu")
        if down_gather is not None:
            w.update(_gather_finish(down_gather, a, me))
        return g, u, a, _mm([(a, w[f"w_{tag}_down"], "nn")], F32, f"{tag}_down")

    n1 = _rms_fwd(x, gain("ln_ffn1") + entry_token, "rms_ffn1")
    g1, u1, a1, f1 = ffn_fwd(n1, "ffn1", down_gather=gathers[0])
    h1, u = _rms_fwd(x, gain("ln_mix"), "rms_mix", f=f1)
    w.update(_gather_finish(gathers[1], f1, me))
    w_in_t = _pad_w_in(w["w_in"])
    pm = _mm([(u, w_in_t, "nt")], F32, "mixer_in", tn=1792)
    w.update(_gather_finish(gathers[2], pm, me))
    gm = _mm([(u, w["w_merge"], "nn")], F32, "mixer_gates")

    half = jnp.arange(RET_DIM // 2, dtype=F32) / (RET_DIM // 2)
    inv = 1.0 / (ROPE_BASE ** half)
    inv2 = jnp.concatenate([inv, inv]).reshape(1, RET_DIM)
    sign2 = jnp.concatenate([-jnp.ones((RET_DIM // 2,), F32), jnp.ones((RET_DIM // 2,), F32)]).reshape(1, RET_DIM)
    cos2, sin2 = _rope_tables(positions.reshape(t, 1), inv2, sign2)
    consts = _ret_consts()
    y_ret, y_raw, states = _ret_fwd(pm, cos2, sin2, consts)
    w.update(_gather_finish(gathers[3], y_raw, me))
    w_fox_pad = _pad_heads(w["w_fox_out"])
    za = _mm([(y_ret, w["w_ret_out"], "nn")], F32, "ret_out")

    qa, ka, va = _fox_prep(pm, bpad)
    o_fox, y_fox, qa_b = _fox_fwd(qa, ka, va, fox_blk)
    zb = _mm([(y_fox, w_fox_pad, "nn")], F32, "fox_out")

    mix = _merge_fwd(gm, bm, za, zb)
    mo = _mm([(mix, w["w_out"], "nn")], F32, "mix_out")
    h2, n2 = _rms_fwd(h1, gain("ln_ffn2"), "rms_ffn2", f=mo, scale=1.0)
    w.update(_gather_finish(gathers[4], mo, me))
    g2, u2, a2, f2 = ffn_fwd(n2, "ffn2")
    h3, n3 = _rms_fwd(h2, gain("ln_ple"), "rms_ple", f=f2)
    pgl = _mm([(n3, w["w_ple_gate"], "nn")], F32, "ple_gate")
    pb = p.astype(BF16)
    pe = _mm([(pb, w["w_ple"], "nn")], F32, "ple_embed")

    gw, gs = {}, {}
    dh4, dsg, dpe, loss, gs["ln_final"] = _ple_final(h3, pgl, pe, gain("ln_final"), target)
    gw["w_ple_gate"] = _mm([(n3, dsg, "tn")], BF16, "d_w_ple_gate", tn=256)
    gw["w_ple"] = _mm([(pb, dpe, "tn")], BF16, "d_w_ple", tn=256)
    dn3 = _mm([(dsg, w["w_ple_gate"], "nt")], F32, "d_n3")
    dh3, dh3_half, gs["ln_ple"] = _rms_bwd(dn3, h3, gain("ln_ple"), dh4, "rms_ple_bwd", 0.5)

    def ffn_bwd(dh_half, g, u_, a, n, tag, scatter_now=None):
        gw[f"w_{tag}_down"] = _mm([(a, dh_half, "tn")], BF16, f"d_w_{tag}_down", tm=1408, tn=512)
        start = scatter_now if scatter_now is not None else (lambda *_: None)
        token = start((f"w_{tag}_down",), dh_half, "c")
        da = _mm([(dh_half, w[f"w_{tag}_down"], "nt")], BF16, f"d_a_{tag}", tn=1408, after=token)
        dg, du_ = _swiglu_bwd(da, g, u_, f"{tag}_swiglu_bwd")
        gw[f"w_{tag}_gate"] = _mm([(dg, n, "tn")], BF16, f"d_w_{tag}_gate", tm=1408, tn=512)
        token = start((f"w_{tag}_gate",), da, "d")
        gw[f"w_{tag}_up"] = _mm([(du_, n, "tn")], BF16, f"d_w_{tag}_up", tm=1408, tn=512, after=token)
        token = start((f"w_{tag}_up",), da, "e")
        return _mm([(dg, w[f"w_{tag}_gate"], "nn"), (du_, w[f"w_{tag}_up"], "nn")], F32, f"d_n_{tag}", tm=512,
                   after=token)

    dn2 = ffn_bwd(dh3_half, g2, u2, a2, n2, "ffn2")
    dh2, dh2_b, gs["ln_ffn2"] = _rms_bwd(dn2, h2, gain("ln_ffn2"), dh3, "rms_ffn2_bwd", 1.0)

    gw["w_out"] = _mm([(mix, dh2_b, "tn")], BF16, "d_w_out", tn=256)
    dmix = _mm([(dh2_b, w["w_out"], "nt")], F32, "d_mix")
    dza, dzb, dgm, gs["b_merge"] = _merge_bwd(dmix, gm, bm, za, zb)
    gw["w_ret_out"] = _mm([(y_ret, dza, "tn")], BF16, "d_w_ret_out", tn=256)
    gw["w_fox_out"] = _unpad_heads(_mm([(y_fox, dzb, "tn")], BF16, "d_w_fox_out", tn=256))
    dy_ret = _mm([(dza, w["w_ret_out"], "nt")], F32, "d_y_ret")
    do_fox = _mm([(dzb, w_fox_pad, "nt")], F32, "d_y_fox")

    pending = [_scatter_group(gw, EARLY_GROUPS[0], dy_ret, "grads_scatter_a_start")]
    token = pending[0][1][-1][0, 0]
    drq, drk, drv, drg = _ret_bwd(dy_ret, pm, cos2, sin2, y_raw, states, consts[:3] + (consts[3] + token,))

    dqa, dka, dva, ds_rows, ds_cols = _fox_bwd(qa_b, ka, va, do_fox, o_fox, fox_blk)
    dc = jnp.pad((ds_rows + ds_cols).reshape(FOX_HEADS, t).T, ((0, 0), (0, LANES - FOX_HEADS)))
    dff, db_forget = _fox_post(dc, pm, bpad)
    gs["b_forget"] = db_forget[:, :FOX_HEADS]

    dpm = jnp.concatenate([drq, drk, drv, drg, dqa, dka, dva, dff, jnp.zeros((t, LANES), BF16)], axis=1)
    gw["w_merge"] = _mm([(u, dgm, "tn")], BF16, "d_w_merge", tn=512)
    gw["w_in"] = _unpad_w_in(_mm([(dpm, u, "tn")], BF16, "d_w_in", tm=1792, tn=256))
    du = _mm([(dpm, w_in_t, "nn"), (dgm, w["w_merge"], "nt")], F32, "d_u", tm=1024, tn=512)
    pending.append(_scatter_group(gw, EARLY_GROUPS[1], du, "grads_scatter_b_start"))
    token = pending[1][1][-1][0:1, 0:1]
    dh1, dh1_half, gs["ln_mix"] = _rms_bwd(du, h1, gain("ln_mix") + token, dh2, "rms_mix_bwd", 0.5)

    def scatter_now(names, after, tag):
        pending.append(_scatter_group(gw, names, after, f"grads_scatter_{tag}_start"))
        return pending[-1][1][-1]

    dn1 = ffn_bwd(dh1_half, g1, u1, a1, n1, "ffn1", scatter_now=scatter_now)
    dx, _, gs["ln_ffn1"] = _rms_bwd(dn1, x, gain("ln_ffn1"), dh1, "rms_ffn1_bwd", 1.0)
    return loss, dx, gw, gs, pending


WEIGHTS = ("ln_ffn1", "w_ffn1_gate", "w_ffn1_up", "w_ffn1_down", "ln_mix", "w_in", "b_forget", "w_merge", "b_merge",
           "w_ret_out", "w_fox_out", "w_out", "ln_ffn2", "w_ffn2_gate", "w_ffn2_up", "w_ffn2_down", "ln_ple", "w_ple",
           "w_ple_gate", "ln_final")


def kernel(x, p, positions, ln_ffn1, w_ffn1_gate, w_ffn1_up, w_ffn1_down, ln_mix, w_in, b_forget, w_merge, b_merge, w_ret_out, w_fox_out, w_out, ln_ffn2, w_ffn2_gate, w_ffn2_up, w_ffn2_down, ln_ple, w_ple, w_ple_gate, ln_final, loss_target, m_ln_ffn1, m_w_ffn1_gate, m_w_ffn1_up, m_w_ffn1_down, m_ln_mix, m_w_in, m_b_forget, m_w_merge, m_b_merge, m_w_ret_out, m_w_fox_out, m_w_out, m_ln_ffn2, m_w_ffn2_gate, m_w_ffn2_up, m_w_ffn2_down, m_ln_ple, m_w_ple, m_w_ple_gate, m_ln_final, v_ln_ffn1, v_w_ffn1_gate, v_w_ffn1_up, v_w_ffn1_down, v_ln_mix, v_w_in, v_b_forget, v_w_merge, v_b_merge, v_w_ret_out, v_w_fox_out, v_w_out, v_ln_ffn2, v_w_ffn2_gate, v_w_ffn2_up, v_w_ffn2_down, v_ln_ple, v_w_ple, v_w_ple_gate, v_ln_final):
    args = dict(ln_ffn1=ln_ffn1, w_ffn1_gate=w_ffn1_gate, w_ffn1_up=w_ffn1_up, w_ffn1_down=w_ffn1_down, ln_mix=ln_mix, w_in=w_in, b_forget=b_forget, w_merge=w_merge, b_merge=b_merge, w_ret_out=w_ret_out, w_fox_out=w_fox_out, w_out=w_out, ln_ffn2=ln_ffn2, w_ffn2_gate=w_ffn2_gate, w_ffn2_up=w_ffn2_up, w_ffn2_down=w_ffn2_down, ln_ple=ln_ple, w_ple=w_ple, w_ple_gate=w_ple_gate, ln_final=ln_final)
    moms = dict(ln_ffn1=m_ln_ffn1, w_ffn1_gate=m_w_ffn1_gate, w_ffn1_up=m_w_ffn1_up, w_ffn1_down=m_w_ffn1_down, ln_mix=m_ln_mix, w_in=m_w_in, b_forget=m_b_forget, w_merge=m_w_merge, b_merge=m_b_merge, w_ret_out=m_w_ret_out, w_fox_out=m_w_fox_out, w_out=m_w_out, ln_ffn2=m_ln_ffn2, w_ffn2_gate=m_w_ffn2_gate, w_ffn2_up=m_w_ffn2_up, w_ffn2_down=m_w_ffn2_down, ln_ple=m_ln_ple, w_ple=m_w_ple, w_ple_gate=m_w_ple_gate, ln_final=m_ln_final)
    vars_ = dict(ln_ffn1=v_ln_ffn1, w_ffn1_gate=v_w_ffn1_gate, w_ffn1_up=v_w_ffn1_up, w_ffn1_down=v_w_ffn1_down, ln_mix=v_ln_mix, w_in=v_w_in, b_forget=v_b_forget, w_merge=v_w_merge, b_merge=v_b_merge, w_ret_out=v_w_ret_out, w_fox_out=v_w_fox_out, w_out=v_w_out, ln_ffn2=v_ln_ffn2, w_ffn2_gate=v_w_ffn2_gate, w_ffn2_up=v_w_ffn2_up, w_ffn2_down=v_w_ffn2_down, ln_ple=v_ln_ple, w_ple=v_w_ple, w_ple_gate=v_w_ple_gate, ln_final=v_ln_final)
    kinds = ("grad", "delta", "new_m", "new_v")

    me = 4 * lax.axis_index("x") + 2 * lax.axis_index("y") + lax.axis_index("c")
    kind_of = dict(BIG)
    shard = {n: _shard_view(args[n], kind).astype(BF16) for n, kind in BIG}
    first = ("w_ffn1_gate", "w_ffn1_up")
    groups = (("w_ffn1_down",), ("w_in",), ("w_merge",), ("w_ret_out", "w_fox_out", "w_out"))
    groups += (tuple(n for n, _ in BIG if n not in first + sum(groups, ())),)
    gathered = _all_gather([shard[n] for n in first])
    w_full = {n: _full_from_slots(g, kind_of[n]) for n, g in zip(first, gathered)}
    gathers, after = [], gathered[0]
    for tag, names in zip("zambc", groups):
        started = _scatter_start([shard[n] for n in names], after, f"weights_gather_{tag}_start", gather=True)
        gathers.append((names, started, f"weights_gather_{tag}_wait"))
        after = started[-1]

    small = {n: args[n] for n in SMALL}
    loss_part, dx, gw, gs, pending = _local_step(x[0], p[0, 0], positions[0], loss_target[0], w_full, small, me,
                                                 after[0:1, 0:1], gathers)

    parts_of = {}
    for tag, (names, started) in zip("abcde", pending):
        sent, landed = _scatter_wait(started, dx, f"grads_scatter_{tag}_wait")
        for n, blk, land in zip(names, sent, landed):
            own = lax.dynamic_index_in_dim(blk, me, 0, keepdims=True)
            parts_of[n] = lax.dynamic_update_slice_in_dim(land, own, me, 0)
    late = [(n, kind) for n, kind in BIG if n not in parts_of]
    small_part = _pack_small(gs, with_loss=loss_part)
    blocks = [_slots_from_full(gw[n], kind) for n, kind in late]
    blocks.append(jnp.broadcast_to(small_part, (N_DEV,) + small_part.shape))
    recv = _reduce_scatter_exchange(blocks)
    parts_of.update({n: r for (n, _), r in zip(late, recv)})

    res = {}
    for n, kind in BIG:
        parts = parts_of[n]
        outs = _adamw(parts, _shard_view(args[n], kind), _shard_view(moms[n], kind), _shard_view(vars_[n], kind),
                      f"adamw_{n}")
        for what, o in zip(kinds, outs):
            res[(what, n)] = _unview(o, kind, args[n].shape)
    s_outs = _adamw(recv[-1], _pack_small(small), _pack_small({n: moms[n] for n in SMALL}),
                    _pack_small({n: vars_[n] for n in SMALL}), "adamw_small")
    for what, sm in zip(kinds, s_outs):
        svals, extra = _unpack_small(sm, {n: args[n].shape for n in SMALL})
        if what == "grad":
            loss = extra
        for n in SMALL:
            res[(what, n)] = svals[n]
    return (loss, dx[None], *[res[(what, n)] for what in kinds for n in WEIGHTS])
```

```python
import numpy as np
import jax
import jax.numpy as jnp
from jax import lax
from jax.experimental import pallas as pl
from jax.experimental.pallas import tpu as pltpu

F32 = jnp.float32
BF16 = jnp.bfloat16

N_DEV = 8
EPS = 1e-6
RET_HEADS = 4
RET_DIM = 128
RET_WIDTH = RET_HEADS * RET_DIM
FOX_HEADS = 8
FOX_DIM = 64
FOX_WIDTH = FOX_HEADS * FOX_DIM
CHUNK = 128
ROPE_BASE = 10000.0
LANES = 128
FOX_TILES = FOX_HEADS * LANES
IN_COLS = 4 * RET_WIDTH + 3 * FOX_WIDTH + FOX_HEADS
IN_PAD = 4 * RET_WIDTH + 3 * FOX_TILES + 2 * LANES
TILE_RQ, TILE_RK, TILE_RV, TILE_RG = 0, 4, 8, 12
TILE_FQ, TILE_FK, TILE_FV, TILE_FF = 16, 24, 32, 40
NEG = -1e30

ADAM_LR = 0.001
ADAM_B1 = 0.9
ADAM_B2 = 0.999
ADAM_EPS = 1e-08
ADAM_WD = 0.01
ADAM_STEP = 10

VMEM_LIMIT_BYTES = 56 * 1024 * 1024

MESH = pl.DeviceIdType.MESH


def _tile(dim, pref, mult):
    if dim <= pref:
        return dim
    t = (pref // mult) * mult
    while t >= mult:
        if dim % t == 0:
            return t
        t -= mult
    return dim


def _params(dims):
    return pltpu.CompilerParams(dimension_semantics=dims, vmem_limit_bytes=VMEM_LIMIT_BYTES)


def _pcall(body, *, name, out_shape, grid, in_specs, out_specs, scratch_shapes=(), dims=None):
    return pl.pallas_call(body, name=name, out_shape=out_shape, grid=grid, in_specs=in_specs, out_specs=out_specs,
                          scratch_shapes=list(scratch_shapes), compiler_params=_params(dims))


def _dot(a, b, ca, cb):
    return lax.dot_general(a, b, (((ca,), (cb,)), ((), ())), preferred_element_type=F32)


def _sigmoid(x):
    return 1.0 / (1.0 + jnp.exp(-x))


def _mm(pairs, out_dtype, name, tm=1024, tn=1024, after=None):
    dims = []
    for a, b, mode in pairs:
        m, k = (a.shape[1], a.shape[0]) if mode == "tn" else a.shape
        n, k2 = b.shape if mode == "nt" else (b.shape[1], b.shape[0])
        assert k == k2, (name, a.shape, b.shape, mode)
        dims.append((m, n))
    assert all(d == dims[0] for d in dims), (name, dims)
    m, n = dims[0]
    tm = _tile(m, tm, 128 if any(mode == "tn" for _, _, mode in pairs) else 16)
    tn = _tile(n, tn, 128)
    in_specs, contract, operands = [], [], []
    for a, b, mode in pairs:
        k = a.shape[0] if mode == "tn" else a.shape[1]
        in_specs.append(pl.BlockSpec((k, tm), lambda i, j: (0, i)) if mode == "tn" else
                        pl.BlockSpec((tm, k), lambda i, j: (i, 0)))
        in_specs.append(pl.BlockSpec((tn, k), lambda i, j: (j, 0)) if mode == "nt" else
                        pl.BlockSpec((k, tn), lambda i, j: (0, j)))
        contract.append((0 if mode == "tn" else 1, 1 if mode == "nt" else 0))
        operands += [a, b]
    if after is not None:
        in_specs.append(pl.BlockSpec(memory_space=pl.ANY))
        operands.append(after)

    def body(*refs):
        o_ref = refs[-1]
        acc = None
        for p, (ca, cb) in enumerate(contract):
            part = _dot(refs[2 * p][...], refs[2 * p + 1][...], ca, cb)
            acc = part if acc is None else acc + part
        o_ref[...] = acc.astype(out_dtype)

    return _pcall(body, name=name, out_shape=jax.ShapeDtypeStruct((m, n), out_dtype), grid=(m // tm, n // tn),
                  in_specs=in_specs, out_specs=pl.BlockSpec((tm, tn), lambda i, j: (i, j)),
                  dims=("parallel", "parallel"))(*operands)


def _rms_fwd(h, gain, name, f=None, scale=0.5):
    t, d = h.shape
    tt = _tile(t, 512, 16)
    row = pl.BlockSpec((tt, d), lambda i: (i, 0))
    vec = pl.BlockSpec((1, d), lambda i: (0, 0))

    def norm(hv, g_ref, n_ref):
        r = lax.rsqrt(jnp.mean(hv * hv, axis=-1, keepdims=True) + EPS)
        n_ref[...] = (hv * r * g_ref[...]).astype(BF16)

    if f is None:

        def body(h_ref, g_ref, n_ref):
            norm(h_ref[...], g_ref, n_ref)

        return _pcall(body, name=name, out_shape=jax.ShapeDtypeStruct((t, d), BF16), grid=(t // tt,),
                      in_specs=[row, vec], out_specs=row, dims=("parallel",))(h, gain)

    def body(h_ref, f_ref, g_ref, hn_ref, n_ref):
        hv = h_ref[...] + scale * f_ref[...]
        hn_ref[...] = hv
        norm(hv, g_ref, n_ref)

    return _pcall(body, name=name,
                  out_shape=(jax.ShapeDtypeStruct((t, d), F32), jax.ShapeDtypeStruct((t, d), BF16)),
                  grid=(t // tt,), in_specs=[row, row, vec], out_specs=(row, row), dims=("parallel",))(h, f, gain)


def _rms_bwd(dn, h, gain, dh_in, name, out_scale):
    t, d = h.shape
    tt = _tile(t, 1024, 16)
    row = pl.BlockSpec((tt, d), lambda i: (i, 0))
    vec = pl.BlockSpec((1, d), lambda i: (0, 0))

    def body(dn_ref, h_ref, g_ref, dhin_ref, dh_ref, dhb_ref, dg_ref):
        hv = h_ref[...]
        dnv = dn_ref[...].astype(F32)
        r = lax.rsqrt(jnp.mean(hv * hv, axis=-1, keepdims=True) + EPS)
        dng = dnv * g_ref[...]
        dh = dhin_ref[...] + r * dng - hv * (r * r * r) * jnp.mean(dng * hv, axis=-1, keepdims=True)
        dh_ref[...] = dh
        dhb_ref[...] = (out_scale * dh).astype(BF16)
        part = jnp.sum(dnv * hv * r, axis=0, keepdims=True)

        @pl.when(pl.program_id(0) == 0)
        def _():
            dg_ref[...] = part

        @pl.when(pl.program_id(0) > 0)
        def _():
            dg_ref[...] += part

    return _pcall(body, name=name,
                  out_shape=(jax.ShapeDtypeStruct((t, d), F32), jax.ShapeDtypeStruct((t, d), BF16),
                             jax.ShapeDtypeStruct((1, d), F32)),
                  grid=(t // tt,), in_specs=[row, row, vec, row], out_specs=(row, row, vec),
                  dims=("arbitrary",))(dn, h, gain, dh_in)


def _swiglu_fwd(g, u, name):
    t, f = g.shape
    tt = _tile(t, 512, 16)
    row = pl.BlockSpec((tt, f), lambda i: (i, 0))

    def body(g_ref, u_ref, a_ref):
        gv = g_ref[...].astype(F32)
        a_ref[...] = (gv * _sigmoid(gv) * u_ref[...].astype(F32)).astype(BF16)

    return _pcall(body, name=name, out_shape=jax.ShapeDtypeStruct((t, f), BF16), grid=(t // tt,),
                  in_specs=[row, row], out_specs=row, dims=("parallel",))(g, u)


def _swiglu_bwd(da, g, u, name):
    t, f = g.shape
    tt = _tile(t, 512, 16)
    row = pl.BlockSpec((tt, f), lambda i: (i, 0))

    def body(da_ref, g_ref, u_ref, dg_ref, du_ref):
        gv = g_ref[...].astype(F32)
        dav = da_ref[...].astype(F32)
        sg = _sigmoid(gv)
        dg_ref[...] = (dav * u_ref[...].astype(F32) * (sg * (1.0 + gv * (1.0 - sg)))).astype(BF16)
        du_ref[...] = (dav * (gv * sg)).astype(BF16)

    return _pcall(body, name=name, out_shape=(jax.ShapeDtypeStruct((t, f), BF16),) * 2, grid=(t // tt,),
                  in_specs=[row, row, row], out_specs=(row, row), dims=("parallel",))(da, g, u)


def _rope_tables(pos_col, inv2, sign2):
    t = pos_col.shape[0]

    def body(p_ref, inv_ref, sg_ref, c_ref, s_ref):
        ang = p_ref[...].astype(F32) * inv_ref[...]
        c_ref[...] = jnp.cos(ang)
        s_ref[...] = jnp.sin(ang) * sg_ref[...]

    full = lambda shape: pl.BlockSpec(shape, lambda i: (0, 0))
    return _pcall(body, name="rope_tables", out_shape=(jax.ShapeDtypeStruct((t, RET_DIM), F32),) * 2, grid=(1,),
                  in_specs=[full((t, 1)), full((1, RET_DIM)), full((1, RET_DIM))],
                  out_specs=(full((t, RET_DIM)),) * 2, dims=("arbitrary",))(pos_col, inv2, sign2)


def _rot(x, c, s):
    return x * c + pltpu.roll(x, RET_DIM // 2, 1) * s


def _rot_t(g, c, s):
    return g * c + pltpu.roll(g * s, RET_DIM // 2, 1)


def _ret_consts():
    hh = np.arange(RET_HEADS, dtype=np.float32)
    log_gamma = np.log1p(-np.exp2(-5.0 - hh)).astype(np.float32)
    idx = np.arange(CHUNK, dtype=np.float32)
    diff = idx[:, None] - idx[None, :]
    dmask = np.where(diff >= 0, np.exp(log_gamma[:, None, None] * np.maximum(diff, 0.0)), 0.0).astype(np.float32)
    kdec = np.exp(log_gamma[:, None] * (CHUNK - 1 - idx)).astype(np.float32)
    qdec = np.exp(log_gamma[:, None] * (idx + 1.0)).astype(np.float32)
    cdec = np.exp(log_gamma * CHUNK).astype(np.float32)
    bc = lambda v: np.ascontiguousarray(np.broadcast_to(v[:, :, None], (RET_HEADS, CHUNK, RET_DIM)))
    cd = np.ascontiguousarray(np.broadcast_to(cdec[:, None, None], (RET_HEADS, 8, RET_DIM)))
    return jnp.asarray(dmask), jnp.asarray(bc(qdec)), jnp.asarray(bc(kdec)), jnp.asarray(cd)


def _ret_fwd(pm, cos2, sin2, consts):
    t = pm.shape[0]
    n_chunks = t // CHUNK
    dmask, qdec, kdec, cd = consts
    scale = RET_DIM ** -0.5
    wide = lambda c0: pl.BlockSpec((CHUNK, RET_WIDTH), lambda n: (n, c0 // RET_HEADS))
    tab = pl.BlockSpec((CHUNK, RET_DIM), lambda n: (n, 0))
    const = lambda a: pl.BlockSpec(a.shape, lambda n: (0,) * a.ndim)

    def body(q_ref, k_ref, v_ref, g_ref, c_ref, s_ref, dm_ref, qd_ref, kd_ref, cd_ref, y_ref, raw_ref, st_ref, s_acc):
        @pl.when(pl.program_id(0) == 0)
        def _():
            s_acc[...] = jnp.zeros_like(s_acc)

        c, s = c_ref[...], s_ref[...]
        for h in range(RET_HEADS):
            hs = slice(h * RET_DIM, (h + 1) * RET_DIM)
            q = _rot(q_ref[:, hs], c, s)
            k = _rot(k_ref[:, hs], c, s) * scale
            vb = v_ref[:, hs].astype(BF16)
            g = g_ref[:, hs]
            s_in = s_acc[h]
            st_ref[h] = s_in
            a = _dot(q.astype(BF16), k.astype(BF16), 1, 1) * dm_ref[h]
            y = _dot(a.astype(BF16), vb, 1, 0) + _dot((q * qd_ref[h]).astype(BF16), s_in.astype(BF16), 1, 0)
            s_acc[h] = cd_ref[h, 0:1, :] * s_in + _dot((k * kd_ref[h]).astype(BF16), vb, 0, 0)
            raw_ref[:, hs] = y
            mu = jnp.mean(y, axis=-1, keepdims=True)
            yc = y - mu
            rs = lax.rsqrt(jnp.mean(yc * yc, axis=-1, keepdims=True) + EPS)
            y_ref[:, hs] = (yc * rs * (g * _sigmoid(g))).astype(BF16)

    out_blk = pl.BlockSpec((CHUNK, RET_WIDTH), lambda n: (n, 0))
    return _pcall(
        body, name="retention_fwd",
        out_shape=(jax.ShapeDtypeStruct((t, RET_WIDTH), BF16), jax.ShapeDtypeStruct((t, RET_WIDTH), F32),
                   jax.ShapeDtypeStruct((RET_HEADS, n_chunks, RET_DIM, RET_DIM), F32)),
        grid=(n_chunks,),
        in_specs=[wide(TILE_RQ), wide(TILE_RK), wide(TILE_RV), wide(TILE_RG), tab, tab,
                  const(dmask), const(qdec), const(kdec), const(cd)],
        out_specs=(out_blk, out_blk, pl.BlockSpec((RET_HEADS, None, RET_DIM, RET_DIM), lambda n: (0, n, 0, 0))),
        scratch_shapes=[pltpu.VMEM((RET_HEADS, RET_DIM, RET_DIM), F32)],
        dims=("arbitrary",),
    )(pm, pm, pm, pm, cos2, sin2, dmask, qdec, kdec, cd)


def _ret_bwd(dy, pm, cos2, sin2, raw, states, consts):
    t = pm.shape[0]
    n_chunks = t // CHUNK
    dmask, qdec, kdec, cd = consts
    scale = RET_DIM ** -0.5
    rev = lambda n: n_chunks - 1 - n
    wide = lambda c0: pl.BlockSpec((CHUNK, RET_WIDTH), lambda n: (rev(n), c0 // RET_HEADS))
    tab = pl.BlockSpec((CHUNK, RET_DIM), lambda n: (rev(n), 0))
    blk = pl.BlockSpec((CHUNK, RET_WIDTH), lambda n: (rev(n), 0))
    const = lambda a: pl.BlockSpec(a.shape, lambda n: (0,) * a.ndim)

    def body(dy_ref, q_ref, k_ref, v_ref, g_ref, c_ref, s_ref, raw_ref, st_ref, dm_ref, qd_ref, kd_ref, cd_ref,
             dq_ref, dk_ref, dv_ref, dg_ref, ds_acc):
        @pl.when(pl.program_id(0) == 0)
        def _():
            ds_acc[...] = jnp.zeros_like(ds_acc)

        c, s = c_ref[...], s_ref[...]
        for h in range(RET_HEADS):
            hs = slice(h * RET_DIM, (h + 1) * RET_DIM)
            q = _rot(q_ref[:, hs], c, s)
            k = _rot(k_ref[:, hs], c, s) * scale
            qb, kb, vb = q.astype(BF16), k.astype(BF16), v_ref[:, hs].astype(BF16)
            g = g_ref[:, hs]
            dm, qd, kd = dm_ref[h], qd_ref[h], kd_ref[h]
            y = raw_ref[:, hs]
            mu = jnp.mean(y, axis=-1, keepdims=True)
            yc = y - mu
            rs = lax.rsqrt(jnp.mean(yc * yc, axis=-1, keepdims=True) + EPS)
            yn = yc * rs
            sg = _sigmoid(g)
            dyo = dy_ref[:, hs]
            dg_ref[:, hs] = (dyo * yn * (sg * (1.0 + g * (1.0 - sg)))).astype(BF16)
            dyn = dyo * (g * sg)
            dyr = rs * (dyn - jnp.mean(dyn, axis=-1, keepdims=True) - yn * jnp.mean(dyn * yn, axis=-1, keepdims=True))
            dyb = dyr.astype(BF16)
            s_in = st_ref[h].astype(BF16)
            ds_out = ds_acc[h]
            dsb = ds_out.astype(BF16)
            a = _dot(qb, kb, 1, 1) * dm
            da = (_dot(dyb, vb, 1, 1) * dm).astype(BF16)
            kdb = (k * kd).astype(BF16)
            qdb = (q * qd).astype(BF16)
            dv_ref[:, hs] = (_dot(a.astype(BF16), dyb, 0, 0) + _dot(kdb, dsb, 1, 0)).astype(BF16)
            dqh = _dot(da, kb, 1, 0) + _dot(dyb, s_in, 1, 1) * qd
            dkh = _dot(da, qb, 0, 0) + _dot(vb, dsb, 1, 1) * kd
            ds_acc[h] = cd_ref[h, 0:1, :] * ds_out + _dot(qdb, dyb, 0, 0)
            dq_ref[:, hs] = _rot_t(dqh, c, s).astype(BF16)
            dk_ref[:, hs] = (_rot_t(dkh, c, s) * scale).astype(BF16)

    return _pcall(
        body, name="retention_bwd",
        out_shape=(jax.ShapeDtypeStruct((t, RET_WIDTH), BF16),) * 4,
        grid=(n_chunks,),
        in_specs=[blk, wide(TILE_RQ), wide(TILE_RK), wide(TILE_RV), wide(TILE_RG), tab, tab, blk,
                  pl.BlockSpec((RET_HEADS, None, RET_DIM, RET_DIM), lambda n: (0, rev(n), 0, 0)),
                  const(dmask), const(qdec), const(kdec), const(cd)],
        out_specs=(blk,) * 4,
        scratch_shapes=[pltpu.VMEM((RET_HEADS, RET_DIM, RET_DIM), F32)],
        dims=("arbitrary",),
    )(dy, pm, pm, pm, pm, cos2, sin2, raw, states, dmask, qdec, kdec, cd)


FOX_C_LANE = FOX_DIM
FOX_NEGC_LANE = FOX_DIM + 3
FOX_LSE_LANE = FOX_DIM + 6
FOX_L_LANE = FOX_C_LANE
FOX_ROWSUM_LANE = FOX_C_LANE
FOX_COLSUM_LANE = FOX_NEGC_LANE


def _split3(x):
    hi = x.astype(BF16)
    r1 = x - hi.astype(F32)
    mid = r1.astype(BF16)
    lo = (r1 - mid.astype(F32)).astype(BF16)
    return hi, mid, lo


def _tri_dot(tri, x):
    hi, mid, lo = _split3(x)
    return _dot(tri, lo, 1, 0) + _dot(tri, mid, 1, 0) + _dot(tri, hi, 1, 0)


def _log_sigmoid(z):
    return jnp.minimum(z, 0.0) - jnp.log1p(jnp.exp(-jnp.abs(z)))


def _fox_consts():
    place = np.zeros((2, 3, LANES, FOX_TILES), np.float32)
    ones = np.zeros((3, 1, FOX_TILES), np.float32)
    for h in range(FOX_HEADS):
        for part in range(3):
            place[0, part, h, LANES * h + FOX_C_LANE + part] = 1.0
            place[1, part, h, LANES * h + FOX_NEGC_LANE + part] = -1.0
            ones[0, 0, LANES * h + FOX_NEGC_LANE + part] = 1.0
            ones[1, 0, LANES * h + FOX_C_LANE + part] = 1.0
            ones[1, 0, LANES * h + FOX_LSE_LANE + part] = 1.0
            ones[2, 0, LANES * h + FOX_C_LANE + part] = 1.0
    return jnp.asarray(place, BF16), jnp.asarray(ones, F32)


def _fox_prep(pm, bpad):
    t = pm.shape[0]
    tt = _tile(t, 512, LANES)
    place, ones = _fox_consts()
    wide = lambda c0: pl.BlockSpec((tt, FOX_TILES), lambda i: (i, c0 // FOX_HEADS))
    const = lambda a: pl.BlockSpec(a.shape, lambda i: (0,) * a.ndim)

    def body(q_ref, k_ref, v_ref, ff_ref, b_ref, pl_ref, on_ref, qa_ref, ka_ref, va_ref, carry_s):
        @pl.when(pl.program_id(0) == 0)
        def _():
            carry_s[...] = jnp.zeros_like(carry_s)

        r = lax.broadcasted_iota(jnp.int32, (LANES, LANES), 0)
        cc = lax.broadcasted_iota(jnp.int32, (LANES, LANES), 1)
        tri = jnp.where(cc <= r, 1.0, 0.0).astype(BF16)
        bias = b_ref[...]
        for sub in range(tt // LANES):
            rows = pl.ds(sub * LANES, LANES)
            cs = _tri_dot(tri, _log_sigmoid(ff_ref[rows, :] + bias)) + carry_s[...]
            carry_s[...] = cs[LANES - 1:LANES, :]
            parts = _split3(cs)
            eq = sum(_dot(part, pl_ref[0, i], 1, 0) for i, part in enumerate(parts))
            ek = sum(_dot(part, pl_ref[1, i], 1, 0) for i, part in enumerate(parts))
            qa_ref[rows, :] = (q_ref[rows, :] * FOX_DIM ** -0.5 + eq + on_ref[0]).astype(BF16)
            ka_ref[rows, :] = (k_ref[rows, :] + ek + on_ref[1]).astype(BF16)
            va_ref[rows, :] = (v_ref[rows, :] + on_ref[2]).astype(BF16)

    out = pl.BlockSpec((tt, FOX_TILES), lambda i: (i, 0))
    return _pcall(body, name="fox_prep", out_shape=(jax.ShapeDtypeStruct((t, FOX_TILES), BF16),) * 3, grid=(t // tt,),
                  in_specs=[wide(TILE_FQ), wide(TILE_FK), wide(TILE_FV), pl.BlockSpec((tt, LANES), lambda i: (i, TILE_FF)),
                            pl.BlockSpec((1, LANES), lambda i: (0, 0)), const(place), const(ones)],
                  out_specs=(out,) * 3, scratch_shapes=[pltpu.VMEM((1, LANES), F32)],
                  dims=("arbitrary",))(pm, pm, pm, pm, bpad, place, ones)


def _fox_post(dc, pm, bpad):
    t = pm.shape[0]
    nb = t // LANES

    def body(dc_ref, ff_ref, b_ref, d_ref, db_ref):
        r = lax.broadcasted_iota(jnp.int32, (LANES, LANES), 0)
        cc = lax.broadcasted_iota(jnp.int32, (LANES, LANES), 1)
        tri = jnp.where(cc >= r, 1.0, 0.0).astype(BF16)
        bias = b_ref[...]

        def step(i, carry):
            tail, acc = carry
            rows = pl.ds(pl.multiple_of((nb - 1 - i) * LANES, LANES), LANES)
            cs = _tri_dot(tri, dc_ref[rows, :]) + tail
            dff = cs * _sigmoid(-(ff_ref[rows, :] + bias))
            d_ref[rows, :] = dff.astype(BF16)
            return cs[0:1, :], acc + jnp.sum(dff, axis=0, keepdims=True)

        zero = jnp.zeros((1, LANES), F32)
        _, acc = lax.fori_loop(0, nb, step, (zero, zero))
        db_ref[...] = acc

    return _pcall(body, name="fox_forget_bwd",
                  out_shape=(jax.ShapeDtypeStruct((t, LANES), BF16), jax.ShapeDtypeStruct((1, LANES), F32)), grid=(1,),
                  in_specs=[pl.BlockSpec((t, LANES), lambda i: (0, 0)), pl.BlockSpec((t, LANES), lambda i: (0, TILE_FF)),
                            pl.BlockSpec((1, LANES), lambda i: (0, 0))],
                  out_specs=(pl.BlockSpec((t, LANES), lambda i: (0, 0)), pl.BlockSpec((1, LANES), lambda i: (0, 0))),
                  dims=("arbitrary",))(dc, pm, bpad)


def _tri_tables(nb, q_major):
    pairs = [(i, j) for i in range(nb) for j in range(i + 1)] if q_major else \
            [(i, j) for j in range(nb) for i in range(j, nb)]
    return jnp.asarray([a for a, _ in pairs], jnp.int32), jnp.asarray([b for _, b in pairs], jnp.int32)


def _causal(s):
    n = s.shape[0]
    row = lax.broadcasted_iota(jnp.int32, (n, n), 0)
    col = lax.broadcasted_iota(jnp.int32, (n, n), 1)
    return jnp.where(col <= row, s, NEG)


def _lane_col(x, lane):
    sel = lax.broadcasted_iota(jnp.int32, x.shape, 1) == lane
    return jnp.sum(jnp.where(sel, x, 0.0), axis=1, keepdims=True)


def _fox_fwd(qa, ka, va, blk):
    t = qa.shape[0]
    nb = t // blk
    qi, kj = _tri_tables(nb, True)
    q_spec = pl.BlockSpec((blk, LANES), lambda h, s, qi_r, kj_r: (qi_r[s], h))
    k_spec = pl.BlockSpec((blk, LANES), lambda h, s, qi_r, kj_r: (kj_r[s], h))

    def body(qi_r, kj_r, q_ref, k_ref, v_ref, o_ref, ob_ref, qb_ref, m_s, acc_s):
        s_id = pl.program_id(1)
        i, j = qi_r[s_id], kj_r[s_id]

        @pl.when(j == 0)
        def _():
            m_s[...] = jnp.full_like(m_s, NEG)
            acc_s[...] = jnp.zeros_like(acc_s)

        def tile(diagonal):
            s = _dot(q_ref[...], k_ref[...], 1, 1)
            if diagonal:
                s = _causal(s)
            m_old = m_s[...]
            m_new = jnp.maximum(m_old, jnp.max(s, axis=1, keepdims=True))
            p = jnp.exp(s - jnp.tile(m_new, (1, blk // LANES)))
            acc_s[...] = jnp.exp(m_old - m_new) * acc_s[...] + _dot(p.astype(BF16), v_ref[...], 1, 0)
            m_s[...] = m_new

        @pl.when(j < i)
        def _():
            tile(False)

        @pl.when(j == i)
        def _():
            tile(True)
            acc = acc_s[...]
            l = _lane_col(acc, FOX_L_LANE)
            o = acc / l
            o_ref[...] = o
            ob_ref[...] = o.astype(BF16)
            hi, mid, lo = _split3(-(m_s[:, 0:1] + jnp.log(l)))
            lane = lax.broadcasted_iota(jnp.int32, acc.shape, 1)
            qb_ref[...] = jnp.where(lane == FOX_LSE_LANE, hi,
                                    jnp.where(lane == FOX_LSE_LANE + 1, mid,
                                              jnp.where(lane == FOX_LSE_LANE + 2, lo, q_ref[...])))

    wide = (t, FOX_TILES)
    return pl.pallas_call(
        body, name="fox_fwd",
        out_shape=(jax.ShapeDtypeStruct(wide, F32), jax.ShapeDtypeStruct(wide, BF16), jax.ShapeDtypeStruct(wide, BF16)),
        grid_spec=pltpu.PrefetchScalarGridSpec(
            num_scalar_prefetch=2, grid=(FOX_HEADS, qi.shape[0]), in_specs=[q_spec, k_spec, k_spec],
            out_specs=(q_spec,) * 3,
            scratch_shapes=[pltpu.VMEM((blk, LANES), F32), pltpu.VMEM((blk, LANES), F32)]),
        compiler_params=_params(("parallel", "arbitrary")),
    )(qi, kj, qa, ka, va)


def _fox_bwd(qa, ka, va, do, o, blk):
    t = qa.shape[0]
    nb = t // blk
    qi, kj = _tri_tables(nb, False)
    q_spec = pl.BlockSpec((blk, LANES), lambda h, s, qi_r, kj_r: (qi_r[s], h))
    k_spec = pl.BlockSpec((blk, LANES), lambda h, s, qi_r, kj_r: (kj_r[s], h))
    head_spec = pl.BlockSpec((t, LANES), lambda h, s, qi_r, kj_r: (0, h))
    head_col = pl.BlockSpec((None, t, 1), lambda h, s, qi_r, kj_r: (h, 0, 0))
    k_col = pl.BlockSpec((None, blk, 1), lambda h, s, qi_r, kj_r: (h, kj_r[s], 0))
    first_spec = pl.BlockSpec((blk, LANES), lambda h, s, qi_r, kj_r: (jnp.where(kj_r[s] == 0, qi_r[s], nb - 1), h))
    n_steps = int(qi.shape[0])

    def body(qi_r, kj_r, q_ref, k_ref, v_ref, do_ref, o_ref, dq_ref, dk_ref, dv_ref, rs_ref, cs_ref,
             doa_s, dq_s, dk_s, dv_s):
        s_id = pl.program_id(1)
        i, j = qi_r[s_id], kj_r[s_id]
        rows = pl.ds(pl.multiple_of(i * blk, blk), blk)

        @pl.when(j == 0)
        def _():
            dof = do_ref[...]
            hi, mid, lo = _split3(-jnp.sum(dof * o_ref[...], axis=1, keepdims=True))
            lane = lax.broadcasted_iota(jnp.int32, dof.shape, 1)
            doa = jnp.where(lane == FOX_C_LANE, hi.astype(F32),
                            jnp.where(lane == FOX_C_LANE + 1, mid.astype(F32),
                                      jnp.where(lane == FOX_C_LANE + 2, lo.astype(F32), dof)))
            doa_s[rows, :] = doa.astype(BF16)
            dq_s[rows, :] = jnp.zeros((blk, LANES), F32)

        @pl.when(i == j)
        def _():
            dk_s[...] = jnp.zeros_like(dk_s)
            dv_s[...] = jnp.zeros_like(dv_s)

        def tile(diagonal):
            q, k = q_ref[...], k_ref[...]
            s = _dot(q, k, 1, 1)
            if diagonal:
                s = _causal(s)
            p = jnp.exp(s)
            doa = doa_s[rows, :]
            ds = (p * _dot(doa, v_ref[...], 1, 1)).astype(BF16)
            dv_s[...] += _dot(p.astype(BF16), doa, 0, 0)
            dk_s[...] += _dot(ds, q, 0, 0)
            dq_s[rows, :] += _dot(ds, k, 1, 0)

        @pl.when(i > j)
        def _():
            tile(False)

        @pl.when(i == j)
        def _():
            tile(True)

        @pl.when(i == nb - 1)
        def _():
            dk = dk_s[...]
            dk_ref[...] = dk.astype(BF16)
            dv_ref[...] = dv_s[...].astype(BF16)
            cs_ref[...] = -_lane_col(dk, FOX_COLSUM_LANE)

        @pl.when(s_id == n_steps - 1)
        def _():
            dq = dq_s[...]
            dq_ref[...] = (dq * FOX_DIM ** -0.5).astype(BF16)
            rs_ref[...] = _lane_col(dq, FOX_ROWSUM_LANE)

    wide = jax.ShapeDtypeStruct((t, FOX_TILES), BF16)
    cols = jax.ShapeDtypeStruct((FOX_HEADS, t, 1), F32)
    return pl.pallas_call(
        body, name="fox_bwd", out_shape=(wide, wide, wide, cols, cols),
        grid_spec=pltpu.PrefetchScalarGridSpec(
            num_scalar_prefetch=2, grid=(FOX_HEADS, n_steps),
            in_specs=[q_spec, k_spec, k_spec, first_spec, first_spec],
            out_specs=(head_spec, k_spec, k_spec, head_col, k_col),
            scratch_shapes=[pltpu.VMEM((t, LANES), BF16), pltpu.VMEM((t, LANES), F32), pltpu.VMEM((blk, LANES), F32),
                            pltpu.VMEM((blk, LANES), F32)]),
        compiler_params=_params(("parallel", "arbitrary")),
    )(qi, kj, qa, ka, va, do, o)


def _merge_fwd(gm, bm, za, zb):
    t, d = za.shape
    tt = _tile(t, 512, 16)
    row = pl.BlockSpec((tt, d), lambda i: (i, 0))

    def body(gm_ref, b_ref, za_ref, zb_ref, o_ref):
        ga = _sigmoid(gm_ref[:, :d] + b_ref[:, :d])
        gb = _sigmoid(gm_ref[:, d:] + b_ref[:, d:])
        o_ref[...] = (ga * za_ref[...] + gb * zb_ref[...]).astype(BF16)

    return _pcall(body, name="merge_fwd", out_shape=jax.ShapeDtypeStruct((t, d), BF16), grid=(t // tt,),
                  in_specs=[pl.BlockSpec((tt, 2 * d), lambda i: (i, 0)), pl.BlockSpec((1, 2 * d), lambda i: (0, 0)), row, row],
                  out_specs=row, dims=("parallel",))(gm, bm, za, zb)


def _merge_bwd(dmix, gm, bm, za, zb):
    t, d = za.shape
    tt = _tile(t, 512, 16)
    row = pl.BlockSpec((tt, d), lambda i: (i, 0))
    wide = pl.BlockSpec((tt, 2 * d), lambda i: (i, 0))
    vec = pl.BlockSpec((1, 2 * d), lambda i: (0, 0))

    def body(dm_ref, gm_ref, b_ref, za_ref, zb_ref, dza_ref, dzb_ref, dgm_ref, db_ref):
        dm = dm_ref[...]
        ga = _sigmoid(gm_ref[:, :d] + b_ref[:, :d])
        gb = _sigmoid(gm_ref[:, d:] + b_ref[:, d:])
        dza_ref[...] = (dm * ga).astype(BF16)
        dzb_ref[...] = (dm * gb).astype(BF16)
        dla = dm * za_ref[...] * ga * (1.0 - ga)
        dlb = dm * zb_ref[...] * gb * (1.0 - gb)
        dgm_ref[:, :d] = dla.astype(BF16)
        dgm_ref[:, d:] = dlb.astype(BF16)
        pa = jnp.sum(dla, axis=0, keepdims=True)
        pb = jnp.sum(dlb, axis=0, keepdims=True)

        @pl.when(pl.program_id(0) == 0)
        def _():
            db_ref[:, :d] = pa
            db_ref[:, d:] = pb

        @pl.when(pl.program_id(0) > 0)
        def _():
            db_ref[:, :d] += pa
            db_ref[:, d:] += pb

    return _pcall(body, name="merge_bwd",
                  out_shape=(jax.ShapeDtypeStruct((t, d), BF16), jax.ShapeDtypeStruct((t, d), BF16),
                             jax.ShapeDtypeStruct((t, 2 * d), BF16), jax.ShapeDtypeStruct((1, 2 * d), F32)),
                  grid=(t // tt,), in_specs=[row, wide, vec, row, row], out_specs=(row, row, wide, vec),
                  dims=("arbitrary",))(dmix, gm, bm, za, zb)


def _ple_final(h3, pgl, pe, gain, target):
    t, d = h3.shape
    tt = _tile(t, 512, 16)
    row = pl.BlockSpec((tt, d), lambda i: (i, 0))
    vec = pl.BlockSpec((1, d), lambda i: (0, 0))
    lvec = pl.BlockSpec((1, LANES), lambda i: (0, 0))

    def body(h_ref, pgl_ref, pe_ref, g_ref, t_ref, dh_ref, dsg_ref, dpe_ref, loss_ref, dg_ref):
        pg = _sigmoid(pgl_ref[...])
        pe_v = pe_ref[...]
        h4 = h_ref[...] + pg * pe_v
        r = lax.rsqrt(jnp.mean(h4 * h4, axis=-1, keepdims=True) + EPS)
        gv = g_ref[...]
        err = h4 * r * gv - t_ref[...]
        part_loss = 0.5 * jnp.sum(jnp.mean(err * err, axis=-1, keepdims=True), axis=0, keepdims=True)
        dy = err * (1.0 / d)
        part_g = jnp.sum(dy * h4 * r, axis=0, keepdims=True)
        dyg = dy * gv
        dh = r * dyg - h4 * (r * r * r) * jnp.mean(dyg * h4, axis=-1, keepdims=True)
        dh_ref[...] = dh
        dsg_ref[...] = (dh * pe_v * pg * (1.0 - pg)).astype(BF16)
        dpe_ref[...] = (dh * pg).astype(BF16)

        @pl.when(pl.program_id(0) == 0)
        def _():
            loss_ref[...] = jnp.broadcast_to(part_loss, (1, LANES))
            dg_ref[...] = part_g

        @pl.when(pl.program_id(0) > 0)
        def _():
            loss_ref[...] += jnp.broadcast_to(part_loss, (1, LANES))
            dg_ref[...] += part_g

    return _pcall(body, name="ple_final",
                  out_shape=(jax.ShapeDtypeStruct((t, d), F32), jax.ShapeDtypeStruct((t, d), BF16),
                             jax.ShapeDtypeStruct((t, d), BF16), jax.ShapeDtypeStruct((1, LANES), F32),
                             jax.ShapeDtypeStruct((1, d), F32)),
                  grid=(t // tt,), in_specs=[row, row, row, vec, row], out_specs=(row, row, row, lvec, vec),
                  dims=("arbitrary",))(h3, pgl, pe, gain, target)


def _adamw_math(w, g, m, v):
    m = ADAM_B1 * m + (1.0 - ADAM_B1) * g
    v = ADAM_B2 * v + (1.0 - ADAM_B2) * (g * g)
    m_hat = m / (1.0 - ADAM_B1 ** ADAM_STEP)
    v_hat = v / (1.0 - ADAM_B2 ** ADAM_STEP)
    delta = -ADAM_LR * (m_hat / (jnp.sqrt(v_hat) + ADAM_EPS) + ADAM_WD * w)
    return delta, m, v


def _adamw(parts, w, m, v, name):
    n, r, c = parts.shape
    tr = _tile(r, 256, 16)
    row = pl.BlockSpec((tr, c), lambda i: (i, 0))

    def body(p_ref, w_ref, m_ref, v_ref, g_ref, d_ref, mo_ref, vo_ref):
        g = p_ref[0].astype(F32)
        for s in range(1, n):
            g = g + p_ref[s].astype(F32)
        g_ref[...] = g
        d_ref[...], mo_ref[...], vo_ref[...] = _adamw_math(w_ref[...], g, m_ref[...], v_ref[...])

    return _pcall(body, name=name, out_shape=(jax.ShapeDtypeStruct((r, c), F32),) * 4, grid=(r // tr,),
                  in_specs=[pl.BlockSpec((n, tr, c), lambda i: (0, i, 0)), row, row, row], out_specs=(row,) * 4,
                  dims=("parallel",))(parts, w, m, v)


ANY = pl.BlockSpec(memory_space=pl.ANY)


def _all_gather(shards):
    n = len(shards)

    def body(*refs):
        x_refs, out_refs = refs[:n], refs[n:2 * n]
        send_sems, recv_sems, local_sems = refs[2 * n:]
        x, y, cc = lax.axis_index("x"), lax.axis_index("y"), lax.axis_index("c")
        me, sibling = (x, y, cc), (x, y, 1 - cc)
        chips = [(1 - x, y), (x, 1 - y), (1 - x, 1 - y)]

        def slot(a, px, py, pc):
            return out_refs[a].at[4 * px + 2 * py + pc]

        def copy(a, k, block, to, src=None):
            return pltpu.make_async_remote_copy(
                src_ref=slot(a, *block) if src is None else src, dst_ref=slot(a, *block),
                send_sem=send_sems.at[7 * a + k], recv_sem=recv_sems.at[7 * a + k], device_id=to, device_id_type=MESH)

        local, sent = [], []
        for a in range(n):
            local.append(pltpu.make_async_copy(x_refs[a], slot(a, *me), local_sems.at[a]))
            sent.append(copy(a, 0, me, sibling, src=x_refs[a]))
            sent += [copy(a, 1 + j, me, (*chip, cc), src=x_refs[a]) for j, chip in enumerate(chips)]
        for cp in local + sent:
            cp.start()
        for j, chip in enumerate(chips):
            for a in range(n):
                copy(a, 1 + j, (*chip, cc), me).wait_recv()
                sent.append(copy(a, 4 + j, (*chip, cc), sibling))
                sent[-1].start()
        for a in range(n):
            copy(a, 0, sibling, me).wait_recv()
            for j, chip in enumerate(chips):
                copy(a, 4 + j, (*chip, 1 - cc), me).wait_recv()
        for cp in sent:
            cp.wait_send()
        for cp in local:
            cp.wait()

    return pl.pallas_call(
        body, name="weights_all_gather",
        out_shape=tuple(jax.ShapeDtypeStruct((N_DEV,) + s.shape, s.dtype) for s in shards),
        in_specs=[ANY] * n, out_specs=(ANY,) * n,
        scratch_shapes=[pltpu.SemaphoreType.DMA((7 * n,)), pltpu.SemaphoreType.DMA((7 * n,)),
                        pltpu.SemaphoreType.DMA((n,))],
    )(*shards)


def _reduce_scatter_exchange(blocks):
    n = len(blocks)

    def body(*refs):
        g_refs, recv_refs = refs[:n], refs[n:2 * n]
        send_sems, recv_sems, local_sems = refs[2 * n:]
        x, y, cc = lax.axis_index("x"), lax.axis_index("y"), lax.axis_index("c")
        me = 4 * x + 2 * y + cc
        local, sent, landing = [], [], []
        for a in range(n):
            local.append(pltpu.make_async_copy(g_refs[a].at[me], recv_refs[a].at[me], local_sems.at[a]))
        for k in range(1, N_DEV):
            px, py, pc = x ^ (k >> 2), y ^ ((k >> 1) & 1), cc ^ (k & 1)
            peer = 4 * px + 2 * py + pc
            for a in range(n):
                sems = dict(send_sem=send_sems.at[7 * a + k - 1], recv_sem=recv_sems.at[7 * a + k - 1],
                            device_id=(px, py, pc), device_id_type=MESH)
                sent.append(pltpu.make_async_remote_copy(src_ref=g_refs[a].at[peer], dst_ref=recv_refs[a].at[me], **sems))
                landing.append(pltpu.make_async_remote_copy(src_ref=g_refs[a].at[me], dst_ref=recv_refs[a].at[peer], **sems))
        for cp in local + sent:
            cp.start()
        for cp in landing:
            cp.wait_recv()
        for cp in sent:
            cp.wait_send()
        for cp in local:
            cp.wait()

    return pl.pallas_call(
        body, name="grads_reduce_scatter_exchange",
        out_shape=tuple(jax.ShapeDtypeStruct(b.shape, b.dtype) for b in blocks),
        in_specs=[ANY] * n, out_specs=(ANY,) * n,
        scratch_shapes=[pltpu.SemaphoreType.DMA((7 * n,)), pltpu.SemaphoreType.DMA((7 * n,)),
                        pltpu.SemaphoreType.DMA((n,))],
    )(*blocks)


HBM = pl.BlockSpec(memory_space=pltpu.HBM)
SEM = pl.BlockSpec(memory_space=pltpu.SEMAPHORE)
DATAFLOW = pltpu.SideEffectType.DATAFLOW_SIDE_EFFECTING


def _peers():
    x, y, cc = lax.axis_index("x"), lax.axis_index("y"), lax.axis_index("c")
    out = []
    for k in range(1, N_DEV):
        px, py, pc = x ^ (k >> 2), y ^ ((k >> 1) & 1), cc ^ (k & 1)
        out.append((k, (px, py, pc), 4 * px + 2 * py + pc))
    return 4 * x + 2 * y + cc, out


def _scatter_start(blocks, after, name, gather=False):
    n = len(blocks)
    lands = [lax.empty((N_DEV,) + b.shape if gather else b.shape, b.dtype) for b in blocks]

    def body(*refs):
        g_refs, land_refs = refs[:n], refs[n:2 * n]
        send_sems, recv_sems, token = refs[2 * n + 1], refs[2 * n + 2], refs[-1]
        me, peers = _peers()
        for k, peer, slot in peers:
            for a in range(n):
                pltpu.make_async_remote_copy(
                    src_ref=g_refs[a] if gather else g_refs[a].at[slot], dst_ref=land_refs[a].at[me],
                    send_sem=send_sems.at[7 * a + k - 1],
                    recv_sem=recv_sems.at[7 * a + k - 1], device_id=peer, device_id_type=MESH).start()
        token[...] = jnp.zeros_like(token)

    thru = [pltpu.HBM(b.shape, b.dtype) for b in blocks]
    thru_lands = [pltpu.HBM(b.shape, b.dtype) for b in lands]
    return pl.pallas_call(
        body, name=name,
        out_shape=(pltpu.SemaphoreType.DMA((7 * n,)), pltpu.SemaphoreType.DMA((7 * n,)), *thru, *thru_lands,
                   jax.ShapeDtypeStruct((8, LANES), F32)),
        in_specs=[HBM] * (2 * n) + [pl.BlockSpec(memory_space=pl.ANY)],
        out_specs=(SEM, SEM, *[HBM] * (2 * n), pl.BlockSpec(memory_space=pltpu.VMEM)),
        input_output_aliases={i: 2 + i for i in range(2 * n)},
        compiler_params=pltpu.CompilerParams(has_side_effects=DATAFLOW),
    )(*[pltpu.with_memory_space_constraint(a, pltpu.HBM) for a in list(blocks) + lands], after)


def _scatter_wait(started, after, name, gather=False):
    send_sems, recv_sems, *rest = started
    n = (len(rest) - 1) // 2
    thru = rest[:2 * n]

    def body(*refs):
        g_refs, land_refs = refs[:n], refs[n:2 * n]
        send_sems, recv_sems = refs[2 * n], refs[2 * n + 1]
        me, peers = _peers()
        for k, peer, slot in peers:
            for a in range(n):
                copy = pltpu.make_async_remote_copy(
                    src_ref=g_refs[a] if gather else g_refs[a].at[slot], dst_ref=land_refs[a].at[slot],
                    send_sem=send_sems.at[7 * a + k - 1],
                    recv_sem=recv_sems.at[7 * a + k - 1], device_id=peer, device_id_type=MESH)
                copy.wait_send()
                copy.wait_recv()

    out = pl.pallas_call(
        body, name=name, out_shape=tuple(pltpu.HBM(a.shape, a.dtype) for a in thru),
        in_specs=[HBM] * (2 * n) + [SEM, SEM, pl.BlockSpec(memory_space=pl.ANY)], out_specs=tuple([HBM] * (2 * n)),
        input_output_aliases={i: i for i in range(2 * n)},
        compiler_params=pltpu.CompilerParams(has_side_effects=DATAFLOW),
    )(*thru, send_sems, recv_sems, after)
    return out[:n], out[n:]


BIG = (("w_ffn1_gate", "colT"), ("w_ffn1_up", "colT"), ("w_ffn1_down", "row"), ("w_ffn2_gate", "colT"),
       ("w_ffn2_up", "colT"), ("w_ffn2_down", "row"), ("w_in", "colT"), ("w_merge", "col"), ("w_ret_out", "col"),
       ("w_fox_out", "col"), ("w_out", "row"), ("w_ple", "col"), ("w_ple_gate", "row"))


def _shard_view(a, kind):
    a = a.reshape(a.shape[-2:])
    return a.T if kind == "colT" else a


def _unview(a, kind, shape):
    return (a.T if kind == "colT" else a).reshape(shape)


def _full_from_slots(g, kind):
    n, r, c = g.shape
    return g.transpose(1, 0, 2).reshape(r, n * c) if kind == "col" else g.reshape(n * r, c)


def _slots_from_full(f, kind):
    r, c = f.shape
    return f.reshape(r, N_DEV, c // N_DEV).transpose(1, 0, 2) if kind == "col" else f.reshape(N_DEV, r // N_DEV, c)


EARLY_GROUPS = (("w_ple_gate", "w_ple", "w_ffn2_down", "w_ffn2_gate", "w_ffn2_up", "w_out", "w_ret_out", "w_fox_out"),
                ("w_in", "w_merge"))


def _scatter_group(gw, names, after, name):
    kind = dict(BIG)
    return names, _scatter_start([_slots_from_full(gw[n], kind[n]) for n in names], after, name)


def _pad_heads(w):
    d = w.shape[1]
    return jnp.pad(w.reshape(FOX_HEADS, FOX_DIM, d), ((0, 0), (0, LANES - FOX_DIM), (0, 0))).reshape(FOX_TILES, d)


def _unpad_heads(w):
    d = w.shape[1]
    return w.reshape(FOX_HEADS, LANES, d)[:, :FOX_DIM].reshape(FOX_WIDTH, d)


def _deinterleave_rows(w):
    d = w.shape[1]
    return w.reshape(RET_HEADS, RET_DIM // 2, 2, d).transpose(0, 2, 1, 3).reshape(RET_WIDTH, d)


def _interleave_rows(w):
    d = w.shape[1]
    return w.reshape(RET_HEADS, 2, RET_DIM // 2, d).transpose(0, 2, 1, 3).reshape(RET_WIDTH, d)


def _pad_w_in(wt):
    d = wt.shape[1]
    rw, fw = RET_WIDTH, FOX_WIDTH
    fo = 4 * rw
    return jnp.concatenate([
        _deinterleave_rows(wt[:rw]), _deinterleave_rows(wt[rw:2 * rw]), wt[2 * rw:4 * rw],
        _pad_heads(wt[fo:fo + fw]), _pad_heads(wt[fo + fw:fo + 2 * fw]), _pad_heads(wt[fo + 2 * fw:fo + 3 * fw]),
        wt[fo + 3 * fw:], jnp.zeros((2 * LANES - FOX_HEADS, d), wt.dtype)], axis=0)


def _unpad_w_in(g):
    rw = RET_WIDTH
    f0 = 4 * rw
    return jnp.concatenate([
        _interleave_rows(g[:rw]), _interleave_rows(g[rw:2 * rw]), g[2 * rw:4 * rw],
        _unpad_heads(g[f0:f0 + FOX_TILES]), _unpad_heads(g[f0 + FOX_TILES:f0 + 2 * FOX_TILES]),
        _unpad_heads(g[f0 + 2 * FOX_TILES:f0 + 3 * FOX_TILES]),
        g[f0 + 3 * FOX_TILES:f0 + 3 * FOX_TILES + FOX_HEADS]], axis=0)


SMALL = ("ln_ffn1", "ln_mix", "b_forget", "b_merge", "ln_ffn2", "ln_ple", "ln_final")


def _small_rows(n):
    rows = -(-n // LANES)
    return -(-rows // 8) * 8


def _pack_small(vals, with_loss=None):
    parts = []
    for name in SMALL:
        v = vals[name].reshape(-1).astype(F32)
        rows = _small_rows(v.shape[0])
        parts.append(jnp.pad(v, (0, rows * LANES - v.shape[0])).reshape(rows, LANES))
    if with_loss is not None:
        parts.append(jnp.pad(with_loss.reshape(1, LANES), ((0, 7), (0, 0))))
    else:
        parts.append(jnp.zeros((8, LANES), F32))
    return jnp.concatenate(parts, axis=0)


def _unpack_small(packed, shapes):
    out, at = {}, 0
    for name in SMALL:
        n = int(np.prod(shapes[name]))
        rows = _small_rows(n)
        out[name] = packed[at:at + rows].reshape(-1)[:n].reshape(shapes[name])
        at += rows
    return out, packed[at, 0]


def _gather_finish(group, after, me):
    names, started, wait_name = group
    kind = dict(BIG)
    sent, landed = _scatter_wait(started, after, wait_name, gather=True)
    return {n: _full_from_slots(lax.dynamic_update_slice_in_dim(land, shard[None], me, 0), kind[n])
            for n, shard, land in zip(names, sent, landed)}


def _local_step(x, p, positions, target, w, small, me, entry_token, gathers):
    t, d = x.shape
    gain = lambda n: small[n].reshape(1, d)
    w = dict(w)
    bpad = jnp.pad(small["b_forget"].reshape(1, FOX_HEADS), ((0, 0), (0, LANES - FOX_HEADS)))
    bm = small["b_merge"].reshape(1, 2 * d)
    fox_blk = _tile(t, 1024, 128)

    def ffn_fwd(n, tag, down_gather=None):
        g = _mm([(n, w[f"w_{tag}_gate"], "nt")], BF16, f"{tag}_gate", tn=1408)
        u = _mm([(n, w[f"w_{tag}_up"], "nt")], BF16, f"{tag}_up", tn=1408)
        a = _swiglu_fwd(g, u, f"{tag}_swiglu")
        if down_gather is not None:
            w.update(_gather_finish(down_gather, a, me))
        return g, u, a, _mm([(a, w[f"w_{tag}_down"], "nn")], F32, f"{tag}_down")

    n1 = _rms_fwd(x, gain("ln_ffn1") + entry_token, "rms_ffn1")
    g1, u1, a1, f1 = ffn_fwd(n1, "ffn1", down_gather=gathers[0])
    h1, u = _rms_fwd(x, gain("ln_mix"), "rms_mix", f=f1)
    w.update(_gather_finish(gathers[1], f1, me))
    w_in_t = _pad_w_in(w["w_in"])
    pm = _mm([(u, w_in_t, "nt")], F32, "mixer_in", tn=1792)
    w.update(_gather_finish(gathers[2], pm, me))
    gm = _mm([(u, w["w_merge"], "nn")], F32, "mixer_gates")

    half = jnp.arange(RET_DIM // 2, dtype=F32) / (RET_DIM // 2)
    inv = 1.0 / (ROPE_BASE ** half)
    inv2 = jnp.concatenate([inv, inv]).reshape(1, RET_DIM)
    sign2 = jnp.concatenate([-jnp.ones((RET_DIM // 2,), F32), jnp.ones((RET_DIM // 2,), F32)]).reshape(1, RET_DIM)
    cos2, sin2 = _rope_tables(positions.reshape(t, 1), inv2, sign2)
    consts = _ret_consts()
    y_ret, y_raw, states = _ret_fwd(pm, cos2, sin2, consts)
    w.update(_gather_finish(gathers[3], y_raw, me))
    w_fox_pad = _pad_heads(w["w_fox_out"])
    za = _mm([(y_ret, w["w_ret_out"], "nn")], F32, "ret_out")

    qa, ka, va = _fox_prep(pm, bpad)
    o_fox, y_fox, qa_b = _fox_fwd(qa, ka, va, fox_blk)
    zb = _mm([(y_fox, w_fox_pad, "nn")], F32, "fox_out")

    mix = _merge_fwd(gm, bm, za, zb)
    mo = _mm([(mix, w["w_out"], "nn")], F32, "mix_out")
    h2, n2 = _rms_fwd(h1, gain("ln_ffn2"), "rms_ffn2", f=mo, scale=1.0)
    w.update(_gather_finish(gathers[4], mo, me))
    g2, u2, a2, f2 = ffn_fwd(n2, "ffn2")
    h3, n3 = _rms_fwd(h2, gain("ln_ple"), "rms_ple", f=f2)
    pgl = _mm([(n3, w["w_ple_gate"], "nn")], F32, "ple_gate")
    pb = p.astype(BF16)
    pe = _mm([(pb, w["w_ple"], "nn")], F32, "ple_embed")

    gw, gs = {}, {}
    dh4, dsg, dpe, loss, gs["ln_final"] = _ple_final(h3, pgl, pe, gain("ln_final"), target)
    gw["w_ple_gate"] = _mm([(n3, dsg, "tn")], BF16, "d_w_ple_gate", tn=256)
    gw["w_ple"] = _mm([(pb, dpe, "tn")], BF16, "d_w_ple", tn=256)
    dn3 = _mm([(dsg, w["w_ple_gate"], "nt")], F32, "d_n3")
    dh3, dh3_half, gs["ln_ple"] = _rms_bwd(dn3, h3, gain("ln_ple"), dh4, "rms_ple_bwd", 0.5)

    def ffn_bwd(dh_half, g, u_, a, n, tag, scatter_now=None):
        gw[f"w_{tag}_down"] = _mm([(a, dh_half, "tn")], BF16, f"d_w_{tag}_down", tm=1408, tn=512)
        start = scatter_now if scatter_now is not None else (lambda *_: None)
        token = start((f"w_{tag}_down",), dh_half, "c")
        da = _mm([(dh_half, w[f"w_{tag}_down"], "nt")], BF16, f"d_a_{tag}", tn=1408, after=token)
        dg, du_ = _swiglu_bwd(da, g, u_, f"{tag}_swiglu_bwd")
        gw[f"w_{tag}_gate"] = _mm([(dg, n, "tn")], BF16, f"d_w_{tag}_gate", tm=1408, tn=512)
        token = start((f"w_{tag}_gate",), da, "d")
        gw[f"w_{tag}_up"] = _mm([(du_, n, "tn")], BF16, f"d_w_{tag}_up", tm=1408, tn=512, after=token)
        token = start((f"w_{tag}_up",), da, "e")
        return _mm([(dg, w[f"w_{tag}_gate"], "nn"), (du_, w[f"w_{tag}_up"], "nn")], F32, f"d_n_{tag}", tm=512,
                   after=token)

    dn2 = ffn_bwd(dh3_half, g2, u2, a2, n2, "ffn2")
    dh2, dh2_b, gs["ln_ffn2"] = _rms_bwd(dn2, h2, gain("ln_ffn2"), dh3, "rms_ffn2_bwd", 1.0)

    gw["w_out"] = _mm([(mix, dh2_b, "tn")], BF16, "d_w_out", tn=256)
    dmix = _mm([(dh2_b, w["w_out"], "nt")], F32, "d_mix")
    dza, dzb, dgm, gs["b_merge"] = _merge_bwd(dmix, gm, bm, za, zb)
    gw["w_ret_out"] = _mm([(y_ret, dza, "tn")], BF16, "d_w_ret_out", tn=256)
    gw["w_fox_out"] = _unpad_heads(_mm([(y_fox, dzb, "tn")], BF16, "d_w_fox_out", tn=256))
    dy_ret = _mm([(dza, w["w_ret_out"], "nt")], F32, "d_y_ret")
    do_fox = _mm([(dzb, w_fox_pad, "nt")], F32, "d_y_fox")

    pending = [_scatter_group(gw, EARLY_GROUPS[0], dy_ret, "grads_scatter_a_start")]
    token = pending[0][1][-1][0, 0]
    drq, drk, drv, drg = _ret_bwd(dy_ret, pm, cos2, sin2, y_raw, states, consts[:3] + (consts[3] + token,))

    dqa, dka, dva, ds_rows, ds_cols = _fox_bwd(qa_b, ka, va, do_fox, o_fox, fox_blk)
    dc = jnp.pad((ds_rows + ds_cols).reshape(FOX_HEADS, t).T, ((0, 0), (0, LANES - FOX_HEADS)))
    dff, db_forget = _fox_post(dc, pm, bpad)
    gs["b_forget"] = db_forget[:, :FOX_HEADS]

    dpm = jnp.concatenate([drq, drk, drv, drg, dqa, dka, dva, dff, jnp.zeros((t, LANES), BF16)], axis=1)
    gw["w_merge"] = _mm([(u, dgm, "tn")], BF16, "d_w_merge", tn=512)
    gw["w_in"] = _unpad_w_in(_mm([(dpm, u, "tn")], BF16, "d_w_in", tm=1792, tn=512))
    du = _mm([(dpm, w_in_t, "nn"), (dgm, w["w_merge"], "nt")], F32, "d_u", tm=1024, tn=512)
    pending.append(_scatter_group(gw, EARLY_GROUPS[1], du, "grads_scatter_b_start"))
    token = pending[1][1][-1][0:1, 0:1]
    dh1, dh1_half, gs["ln_mix"] = _rms_bwd(du, h1, gain("ln_mix") + token, dh2, "rms_mix_bwd", 0.5)

    def scatter_now(names, after, tag):
        pending.append(_scatter_group(gw, names, after, f"grads_scatter_{tag}_start"))
        return pending[-1][1][-1]

    dn1 = ffn_bwd(dh1_half, g1, u1, a1, n1, "ffn1", scatter_now=scatter_now)
    dx, _, gs["ln_ffn1"] = _rms_bwd(dn1, x, gain("ln_ffn1"), dh1, "rms_ffn1_bwd", 1.0)
    return loss, dx, gw, gs, pending


WEIGHTS = ("ln_ffn1", "w_ffn1_gate", "w_ffn1_up", "w_ffn1_down", "ln_mix", "w_in", "b_forget", "w_merge", "b_merge",
           "w_ret_out", "w_fox_out", "w_out", "ln_ffn2", "w_ffn2_gate", "w_ffn2_up", "w_ffn2_down", "ln_ple", "w_ple",
           "w_ple_gate", "ln_final")


def kernel(x, p, positions, ln_ffn1, w_ffn1_gate, w_ffn1_up, w_ffn1_down, ln_mix, w_in, b_forget, w_merge, b_merge, w_ret_out, w_fox_out, w_out, ln_ffn2, w_ffn2_gate, w_ffn2_up, w_ffn2_down, ln_ple, w_ple, w_ple_gate, ln_final, loss_target, m_ln_ffn1, m_w_ffn1_gate, m_w_ffn1_up, m_w_ffn1_down, m_ln_mix, m_w_in, m_b_forget, m_w_merge, m_b_merge, m_w_ret_out, m_w_fox_out, m_w_out, m_ln_ffn2, m_w_ffn2_gate, m_w_ffn2_up, m_w_ffn2_down, m_ln_ple, m_w_ple, m_w_ple_gate, m_ln_final, v_ln_ffn1, v_w_ffn1_gate, v_w_ffn1_up, v_w_ffn1_down, v_ln_mix, v_w_in, v_b_forget, v_w_merge, v_b_merge, v_w_ret_out, v_w_fox_out, v_w_out, v_ln_ffn2, v_w_ffn2_gate, v_w_ffn2_up, v_w_ffn2_down, v_ln_ple, v_w_ple, v_w_ple_gate, v_ln_final):
    args = dict(ln_ffn1=ln_ffn1, w_ffn1_gate=w_ffn1_gate, w_ffn1_up=w_ffn1_up, w_ffn1_down=w_ffn1_down, ln_mix=ln_mix, w_in=w_in, b_forget=b_forget, w_merge=w_merge, b_merge=b_merge, w_ret_out=w_ret_out, w_fox_out=w_fox_out, w_out=w_out, ln_ffn2=ln_ffn2, w_ffn2_gate=w_ffn2_gate, w_ffn2_up=w_ffn2_up, w_ffn2_down=w_ffn2_down, ln_ple=ln_ple, w_ple=w_ple, w_ple_gate=w_ple_gate, ln_final=ln_final)
    moms = dict(ln_ffn1=m_ln_ffn1, w_ffn1_gate=m_w_ffn1_gate, w_ffn1_up=m_w_ffn1_up, w_ffn1_down=m_w_ffn1_down, ln_mix=m_ln_mix, w_in=m_w_in, b_forget=m_b_forget, w_merge=m_w_merge, b_merge=m_b_merge, w_ret_out=m_w_ret_out, w_fox_out=m_w_fox_out, w_out=m_w_out, ln_ffn2=m_ln_ffn2, w_ffn2_gate=m_w_ffn2_gate, w_ffn2_up=m_w_ffn2_up, w_ffn2_down=m_w_ffn2_down, ln_ple=m_ln_ple, w_ple=m_w_ple, w_ple_gate=m_w_ple_gate, ln_final=m_ln_final)
    vars_ = dict(ln_ffn1=v_ln_ffn1, w_ffn1_gate=v_w_ffn1_gate, w_ffn1_up=v_w_ffn1_up, w_ffn1_down=v_w_ffn1_down, ln_mix=v_ln_mix, w_in=v_w_in, b_forget=v_b_forget, w_merge=v_w_merge, b_merge=v_b_merge, w_ret_out=v_w_ret_out, w_fox_out=v_w_fox_out, w_out=v_w_out, ln_ffn2=v_ln_ffn2, w_ffn2_gate=v_w_ffn2_gate, w_ffn2_up=v_w_ffn2_up, w_ffn2_down=v_w_ffn2_down, ln_ple=v_ln_ple, w_ple=v_w_ple, w_ple_gate=v_w_ple_gate, ln_final=v_ln_final)
    kinds = ("grad", "delta", "new_m", "new_v")

    me = 4 * lax.axis_index("x") + 2 * lax.axis_index("y") + lax.axis_index("c")
    kind_of = dict(BIG)
    shard = {n: _shard_view(args[n], kind).astype(BF16) for n, kind in BIG}
    first = ("w_ffn1_gate", "w_ffn1_up")
    groups = (("w_ffn1_down",), ("w_in",), ("w_merge",), ("w_ret_out", "w_fox_out", "w_out"))
    groups += (tuple(n for n, _ in BIG if n not in first + sum(groups, ())),)
    gathered = _all_gather([shard[n] for n in first])
    w_full = {n: _full_from_slots(g, kind_of[n]) for n, g in zip(first, gathered)}
    gathers, after = [], gathered[0]
    for tag, names in zip("zambc", groups):
        started = _scatter_start([shard[n] for n in names], after, f"weights_gather_{tag}_start", gather=True)
        gathers.append((names, started, f"weights_gather_{tag}_wait"))
        after = started[-1]

    small = {n: args[n] for n in SMALL}
    loss_part, dx, gw, gs, pending = _local_step(x[0], p[0, 0], positions[0], loss_target[0], w_full, small, me,
                                                 after[0:1, 0:1], gathers)

    parts_of = {}
    for tag, (names, started) in zip("abcde", pending):
        sent, landed = _scatter_wait(started, dx, f"grads_scatter_{tag}_wait")
        for n, blk, land in zip(names, sent, landed):
            own = lax.dynamic_index_in_dim(blk, me, 0, keepdims=True)
            parts_of[n] = lax.dynamic_update_slice_in_dim(land, own, me, 0)
    late = [(n, kind) for n, kind in BIG if n not in parts_of]
    small_part = _pack_small(gs, with_loss=loss_part)
    blocks = [_slots_from_full(gw[n], kind) for n, kind in late]
    blocks.append(jnp.broadcast_to(small_part, (N_DEV,) + small_part.shape))
    recv = _reduce_scatter_exchange(blocks)
    parts_of.update({n: r for (n, _), r in zip(late, recv)})

    res = {}
    for n, kind in BIG:
        parts = parts_of[n]
        outs = _adamw(parts, _shard_view(args[n], kind), _shard_view(moms[n], kind), _shard_view(vars_[n], kind),
                      f"adamw_{n}")
        for what, o in zip(kinds, outs):
            res[(what, n)] = _unview(o, kind, args[n].shape)
    s_outs = _adamw(recv[-1], _pack_small(small), _pack_small({n: moms[n] for n in SMALL}),
                    _pack_small({n: vars_[n] for n in SMALL}), "adamw_small")
    for what, sm in zip(kinds, s_outs):
        svals, extra = _unpack_small(sm, {n: args[n].shape for n in SMALL})
        if what == "grad":
            loss = extra
        for n in SMALL:
            res[(what, n)] = svals[n]
    return (loss, dx[None], *[res[(what, n)] for what in kinds for n in WEIGHTS])
```

```python
import numpy as np
import jax
import jax.numpy as jnp
from jax import lax
from jax.experimental import pallas as pl
from jax.experimental.pallas import tpu as pltpu

F32 = jnp.float32
BF16 = jnp.bfloat16

N_DEV = 8
EPS = 1e-6
RET_HEADS = 4
RET_DIM = 128
RET_WIDTH = RET_HEADS * RET_DIM
FOX_HEADS = 8
FOX_DIM = 64
FOX_WIDTH = FOX_HEADS * FOX_DIM
CHUNK = 128
ROPE_BASE = 10000.0
LANES = 128
FOX_TILES = FOX_HEADS * LANES
IN_COLS = 4 * RET_WIDTH + 3 * FOX_WIDTH + FOX_HEADS
IN_PAD = 4 * RET_WIDTH + 3 * FOX_TILES + 2 * LANES
TILE_RQ, TILE_RK, TILE_RV, TILE_RG = 0, 4, 8, 12
TILE_FQ, TILE_FK, TILE_FV, TILE_FF = 16, 24, 32, 40
NEG = -1e30

ADAM_LR = 0.001
ADAM_B1 = 0.9
ADAM_B2 = 0.999
ADAM_EPS = 1e-08
ADAM_WD = 0.01
ADAM_STEP = 10

VMEM_LIMIT_BYTES = 56 * 1024 * 1024

MESH = pl.DeviceIdType.MESH


def _tile(dim, pref, mult):
    if dim <= pref:
        return dim
    t = (pref // mult) * mult
    while t >= mult:
        if dim % t == 0:
            return t
        t -= mult
    return dim


def _params(dims):
    return pltpu.CompilerParams(dimension_semantics=dims, vmem_limit_bytes=VMEM_LIMIT_BYTES)


def _pcall(body, *, name, out_shape, grid, in_specs, out_specs, scratch_shapes=(), dims=None):
    return pl.pallas_call(body, name=name, out_shape=out_shape, grid=grid, in_specs=in_specs, out_specs=out_specs,
                          scratch_shapes=list(scratch_shapes), compiler_params=_params(dims))


def _dot(a, b, ca, cb):
    return lax.dot_general(a, b, (((ca,), (cb,)), ((), ())), preferred_element_type=F32)


def _sigmoid(x):
    return 1.0 / (1.0 + jnp.exp(-x))


def _mm(pairs, out_dtype, name, tm=1024, tn=1024, after=None):
    dims = []
    for a, b, mode in pairs:
        m, k = (a.shape[1], a.shape[0]) if mode == "tn" else a.shape
        n, k2 = b.shape if mode == "nt" else (b.shape[1], b.shape[0])
        assert k == k2, (name, a.shape, b.shape, mode)
        dims.append((m, n))
    assert all(d == dims[0] for d in dims), (name, dims)
    m, n = dims[0]
    tm = _tile(m, tm, 128 if any(mode == "tn" for _, _, mode in pairs) else 16)
    tn = _tile(n, tn, 128)
    in_specs, contract, operands = [], [], []
    for a, b, mode in pairs:
        k = a.shape[0] if mode == "tn" else a.shape[1]
        in_specs.append(pl.BlockSpec((k, tm), lambda i, j: (0, i)) if mode == "tn" else
                        pl.BlockSpec((tm, k), lambda i, j: (i, 0)))
        in_specs.append(pl.BlockSpec((tn, k), lambda i, j: (j, 0)) if mode == "nt" else
                        pl.BlockSpec((k, tn), lambda i, j: (0, j)))
        contract.append((0 if mode == "tn" else 1, 1 if mode == "nt" else 0))
        operands += [a, b]
    if after is not None:
        in_specs.append(pl.BlockSpec(memory_space=pl.ANY))
        operands.append(after)

    def body(*refs):
        o_ref = refs[-1]
        acc = None
        for p, (ca, cb) in enumerate(contract):
            part = _dot(refs[2 * p][...], refs[2 * p + 1][...], ca, cb)
            acc = part if acc is None else acc + part
        o_ref[...] = acc.astype(out_dtype)

    return _pcall(body, name=name, out_shape=jax.ShapeDtypeStruct((m, n), out_dtype), grid=(m // tm, n // tn),
                  in_specs=in_specs, out_specs=pl.BlockSpec((tm, tn), lambda i, j: (i, j)),
                  dims=("parallel", "parallel"))(*operands)


def _rms_fwd(h, gain, name, f=None, scale=0.5):
    t, d = h.shape
    tt = _tile(t, 1024, 16)
    row = pl.BlockSpec((tt, d), lambda i: (i, 0))
    vec = pl.BlockSpec((1, d), lambda i: (0, 0))

    def norm(hv, g_ref, n_ref):
        r = lax.rsqrt(jnp.mean(hv * hv, axis=-1, keepdims=True) + EPS)
        n_ref[...] = (hv * r * g_ref[...]).astype(BF16)

    if f is None:

        def body(h_ref, g_ref, n_ref):
            norm(h_ref[...], g_ref, n_ref)

        return _pcall(body, name=name, out_shape=jax.ShapeDtypeStruct((t, d), BF16), grid=(t // tt,),
                      in_specs=[row, vec], out_specs=row, dims=("parallel",))(h, gain)

    def body(h_ref, f_ref, g_ref, hn_ref, n_ref):
        hv = h_ref[...] + scale * f_ref[...]
        hn_ref[...] = hv
        norm(hv, g_ref, n_ref)

    return _pcall(body, name=name,
                  out_shape=(jax.ShapeDtypeStruct((t, d), F32), jax.ShapeDtypeStruct((t, d), BF16)),
                  grid=(t // tt,), in_specs=[row, row, vec], out_specs=(row, row), dims=("parallel",))(h, f, gain)


def _rms_bwd(dn, h, gain, dh_in, name, out_scale):
    t, d = h.shape
    tt = _tile(t, 1024, 16)
    row = pl.BlockSpec((tt, d), lambda i: (i, 0))
    vec = pl.BlockSpec((1, d), lambda i: (0, 0))

    def body(dn_ref, h_ref, g_ref, dhin_ref, dh_ref, dhb_ref, dg_ref):
        hv = h_ref[...]
        dnv = dn_ref[...].astype(F32)
        r = lax.rsqrt(jnp.mean(hv * hv, axis=-1, keepdims=True) + EPS)
        dng = dnv * g_ref[...]
        dh = dhin_ref[...] + r * dng - hv * (r * r * r) * jnp.mean(dng * hv, axis=-1, keepdims=True)
        dh_ref[...] = dh
        dhb_ref[...] = (out_scale * dh).astype(BF16)
        part = jnp.sum(dnv * hv * r, axis=0, keepdims=True)

        @pl.when(pl.program_id(0) == 0)
        def _():
            dg_ref[...] = part

        @pl.when(pl.program_id(0) > 0)
        def _():
            dg_ref[...] += part

    return _pcall(body, name=name,
                  out_shape=(jax.ShapeDtypeStruct((t, d), F32), jax.ShapeDtypeStruct((t, d), BF16),
                             jax.ShapeDtypeStruct((1, d), F32)),
                  grid=(t // tt,), in_specs=[row, row, vec, row], out_specs=(row, row, vec),
                  dims=("arbitrary",))(dn, h, gain, dh_in)


def _swiglu_fwd(g, u, name):
    t, f = g.shape
    tt = _tile(t, 512, 16)
    row = pl.BlockSpec((tt, f), lambda i: (i, 0))

    def body(g_ref, u_ref, a_ref):
        gv = g_ref[...].astype(F32)
        a_ref[...] = (gv * _sigmoid(gv) * u_ref[...].astype(F32)).astype(BF16)

    return _pcall(body, name=name, out_shape=jax.ShapeDtypeStruct((t, f), BF16), grid=(t // tt,),
                  in_specs=[row, row], out_specs=row, dims=("parallel",))(g, u)


def _swiglu_bwd(da, g, u, name):
    t, f = g.shape
    tt = _tile(t, 512, 16)
    row = pl.BlockSpec((tt, f), lambda i: (i, 0))

    def body(da_ref, g_ref, u_ref, dg_ref, du_ref):
        gv = g_ref[...].astype(F32)
        dav = da_ref[...].astype(F32)
        sg = _sigmoid(gv)
        dg_ref[...] = (dav * u_ref[...].astype(F32) * (sg * (1.0 + gv * (1.0 - sg)))).astype(BF16)
        du_ref[...] = (dav * (gv * sg)).astype(BF16)

    return _pcall(body, name=name, out_shape=(jax.ShapeDtypeStruct((t, f), BF16),) * 2, grid=(t // tt,),
                  in_specs=[row, row, row], out_specs=(row, row), dims=("parallel",))(da, g, u)


def _rope_tables(pos_col, inv2, sign2):
    t = pos_col.shape[0]

    def body(p_ref, inv_ref, sg_ref, c_ref, s_ref):
        ang = p_ref[...].astype(F32) * inv_ref[...]
        c_ref[...] = jnp.cos(ang)
        s_ref[...] = jnp.sin(ang) * sg_ref[...]

    full = lambda shape: pl.BlockSpec(shape, lambda i: (0, 0))
    return _pcall(body, name="rope_tables", out_shape=(jax.ShapeDtypeStruct((t, RET_DIM), F32),) * 2, grid=(1,),
                  in_specs=[full((t, 1)), full((1, RET_DIM)), full((1, RET_DIM))],
                  out_specs=(full((t, RET_DIM)),) * 2, dims=("arbitrary",))(pos_col, inv2, sign2)


def _rot(x, c, s):
    return x * c + pltpu.roll(x, RET_DIM // 2, 1) * s


def _rot_t(g, c, s):
    return g * c + pltpu.roll(g * s, RET_DIM // 2, 1)


def _ret_consts():
    hh = np.arange(RET_HEADS, dtype=np.float32)
    log_gamma = np.log1p(-np.exp2(-5.0 - hh)).astype(np.float32)
    idx = np.arange(CHUNK, dtype=np.float32)
    diff = idx[:, None] - idx[None, :]
    dmask = np.where(diff >= 0, np.exp(log_gamma[:, None, None] * np.maximum(diff, 0.0)), 0.0).astype(np.float32)
    kdec = np.exp(log_gamma[:, None] * (CHUNK - 1 - idx)).astype(np.float32)
    qdec = np.exp(log_gamma[:, None] * (idx + 1.0)).astype(np.float32)
    cdec = np.exp(log_gamma * CHUNK).astype(np.float32)
    bc = lambda v: np.ascontiguousarray(np.broadcast_to(v[:, :, None], (RET_HEADS, CHUNK, RET_DIM)))
    cd = np.ascontiguousarray(np.broadcast_to(cdec[:, None, None], (RET_HEADS, 8, RET_DIM)))
    return jnp.asarray(dmask), jnp.asarray(bc(qdec)), jnp.asarray(bc(kdec)), jnp.asarray(cd)


def _ret_fwd(pm, cos2, sin2, consts):
    t = pm.shape[0]
    n_chunks = t // CHUNK
    dmask, qdec, kdec, cd = consts
    scale = RET_DIM ** -0.5
    wide = lambda c0: pl.BlockSpec((CHUNK, RET_WIDTH), lambda n: (n, c0 // RET_HEADS))
    tab = pl.BlockSpec((CHUNK, RET_DIM), lambda n: (n, 0))
    const = lambda a: pl.BlockSpec(a.shape, lambda n: (0,) * a.ndim)

    def body(q_ref, k_ref, v_ref, g_ref, c_ref, s_ref, dm_ref, qd_ref, kd_ref, cd_ref, y_ref, raw_ref, st_ref, s_acc):
        @pl.when(pl.program_id(0) == 0)
        def _():
            s_acc[...] = jnp.zeros_like(s_acc)

        c, s = c_ref[...], s_ref[...]
        for h in range(RET_HEADS):
            hs = slice(h * RET_DIM, (h + 1) * RET_DIM)
            q = _rot(q_ref[:, hs], c, s)
            k = _rot(k_ref[:, hs], c, s) * scale
            vb = v_ref[:, hs].astype(BF16)
            g = g_ref[:, hs]
            s_in = s_acc[h]
            st_ref[h] = s_in
            a = _dot(q.astype(BF16), k.astype(BF16), 1, 1) * dm_ref[h]
            y = _dot(a.astype(BF16), vb, 1, 0) + _dot((q * qd_ref[h]).astype(BF16), s_in.astype(BF16), 1, 0)
            s_acc[h] = cd_ref[h, 0:1, :] * s_in + _dot((k * kd_ref[h]).astype(BF16), vb, 0, 0)
            raw_ref[:, hs] = y
            mu = jnp.mean(y, axis=-1, keepdims=True)
            yc = y - mu
            rs = lax.rsqrt(jnp.mean(yc * yc, axis=-1, keepdims=True) + EPS)
            y_ref[:, hs] = (yc * rs * (g * _sigmoid(g))).astype(BF16)

    out_blk = pl.BlockSpec((CHUNK, RET_WIDTH), lambda n: (n, 0))
    return _pcall(
        body, name="retention_fwd",
        out_shape=(jax.ShapeDtypeStruct((t, RET_WIDTH), BF16), jax.ShapeDtypeStruct((t, RET_WIDTH), F32),
                   jax.ShapeDtypeStruct((RET_HEADS, n_chunks, RET_DIM, RET_DIM), F32)),
        grid=(n_chunks,),
        in_specs=[wide(TILE_RQ), wide(TILE_RK), wide(TILE_RV), wide(TILE_RG), tab, tab,
                  const(dmask), const(qdec), const(kdec), const(cd)],
        out_specs=(out_blk, out_blk, pl.BlockSpec((RET_HEADS, None, RET_DIM, RET_DIM), lambda n: (0, n, 0, 0))),
        scratch_shapes=[pltpu.VMEM((RET_HEADS, RET_DIM, RET_DIM), F32)],
        dims=("arbitrary",),
    )(pm, pm, pm, pm, cos2, sin2, dmask, qdec, kdec, cd)


def _ret_bwd(dy, pm, cos2, sin2, raw, states, consts):
    t = pm.shape[0]
    n_chunks = t // CHUNK
    dmask, qdec, kdec, cd = consts
    scale = RET_DIM ** -0.5
    rev = lambda n: n_chunks - 1 - n
    wide = lambda c0: pl.BlockSpec((CHUNK, RET_WIDTH), lambda n: (rev(n), c0 // RET_HEADS))
    tab = pl.BlockSpec((CHUNK, RET_DIM), lambda n: (rev(n), 0))
    blk = pl.BlockSpec((CHUNK, RET_WIDTH), lambda n: (rev(n), 0))
    const = lambda a: pl.BlockSpec(a.shape, lambda n: (0,) * a.ndim)

    def body(dy_ref, q_ref, k_ref, v_ref, g_ref, c_ref, s_ref, raw_ref, st_ref, dm_ref, qd_ref, kd_ref, cd_ref,
             dq_ref, dk_ref, dv_ref, dg_ref, ds_acc):
        @pl.when(pl.program_id(0) == 0)
        def _():
            ds_acc[...] = jnp.zeros_like(ds_acc)

        c, s = c_ref[...], s_ref[...]
        for h in range(RET_HEADS):
            hs = slice(h * RET_DIM, (h + 1) * RET_DIM)
            q = _rot(q_ref[:, hs], c, s)
            k = _rot(k_ref[:, hs], c, s) * scale
            qb, kb, vb = q.astype(BF16), k.astype(BF16), v_ref[:, hs].astype(BF16)
            g = g_ref[:, hs]
            dm, qd, kd = dm_ref[h], qd_ref[h], kd_ref[h]
            y = raw_ref[:, hs]
            mu = jnp.mean(y, axis=-1, keepdims=True)
            yc = y - mu
            rs = lax.rsqrt(jnp.mean(yc * yc, axis=-1, keepdims=True) + EPS)
            yn = yc * rs
            sg = _sigmoid(g)
            dyo = dy_ref[:, hs]
            dg_ref[:, hs] = (dyo * yn * (sg * (1.0 + g * (1.0 - sg)))).astype(BF16)
            dyn = dyo * (g * sg)
            dyr = rs * (dyn - jnp.mean(dyn, axis=-1, keepdims=True) - yn * jnp.mean(dyn * yn, axis=-1, keepdims=True))
            dyb = dyr.astype(BF16)
            s_in = st_ref[h].astype(BF16)
            ds_out = ds_acc[h]
            dsb = ds_out.astype(BF16)
            a = _dot(qb, kb, 1, 1) * dm
            da = (_dot(dyb, vb, 1, 1) * dm).astype(BF16)
            kdb = (k * kd).astype(BF16)
            qdb = (q * qd).astype(BF16)
            dv_ref[:, hs] = (_dot(a.astype(BF16), dyb, 0, 0) + _dot(kdb, dsb, 1, 0)).astype(BF16)
            dqh = _dot(da, kb, 1, 0) + _dot(dyb, s_in, 1, 1) * qd
            dkh = _dot(da, qb, 0, 0) + _dot(vb, dsb, 1, 1) * kd
            ds_acc[h] = cd_ref[h, 0:1, :] * ds_out + _dot(qdb, dyb, 0, 0)
            dq_ref[:, hs] = _rot_t(dqh, c, s).astype(BF16)
            dk_ref[:, hs] = (_rot_t(dkh, c, s) * scale).astype(BF16)

    return _pcall(
        body, name="retention_bwd",
        out_shape=(jax.ShapeDtypeStruct((t, RET_WIDTH), BF16),) * 4,
        grid=(n_chunks,),
        in_specs=[blk, wide(TILE_RQ), wide(TILE_RK), wide(TILE_RV), wide(TILE_RG), tab, tab, blk,
                  pl.BlockSpec((RET_HEADS, None, RET_DIM, RET_DIM), lambda n: (0, rev(n), 0, 0)),
                  const(dmask), const(qdec), const(kdec), const(cd)],
        out_specs=(blk,) * 4,
        scratch_shapes=[pltpu.VMEM((RET_HEADS, RET_DIM, RET_DIM), F32)],
        dims=("arbitrary",),
    )(dy, pm, pm, pm, pm, cos2, sin2, raw, states, dmask, qdec, kdec, cd)


FOX_C_LANE = FOX_DIM
FOX_NEGC_LANE = FOX_DIM + 3
FOX_LSE_LANE = FOX_DIM + 6
FOX_L_LANE = FOX_C_LANE
FOX_ROWSUM_LANE = FOX_C_LANE
FOX_COLSUM_LANE = FOX_NEGC_LANE


def _split3(x):
    hi = x.astype(BF16)
    r1 = x - hi.astype(F32)
    mid = r1.astype(BF16)
    lo = (r1 - mid.astype(F32)).astype(BF16)
    return hi, mid, lo


def _tri_dot(tri, x):
    hi, mid, lo = _split3(x)
    return _dot(tri, lo, 1, 0) + _dot(tri, mid, 1, 0) + _dot(tri, hi, 1, 0)


def _log_sigmoid(z):
    return jnp.minimum(z, 0.0) - jnp.log1p(jnp.exp(-jnp.abs(z)))


def _fox_consts():
    place = np.zeros((2, 3, LANES, FOX_TILES), np.float32)
    ones = np.zeros((3, 1, FOX_TILES), np.float32)
    for h in range(FOX_HEADS):
        for part in range(3):
            place[0, part, h, LANES * h + FOX_C_LANE + part] = 1.0
            place[1, part, h, LANES * h + FOX_NEGC_LANE + part] = -1.0
            ones[0, 0, LANES * h + FOX_NEGC_LANE + part] = 1.0
            ones[1, 0, LANES * h + FOX_C_LANE + part] = 1.0
            ones[1, 0, LANES * h + FOX_LSE_LANE + part] = 1.0
            ones[2, 0, LANES * h + FOX_C_LANE + part] = 1.0
    return jnp.asarray(place, BF16), jnp.asarray(ones, F32)


def _fox_prep(pm, bpad):
    t = pm.shape[0]
    tt = _tile(t, 512, LANES)
    place, ones = _fox_consts()
    wide = lambda c0: pl.BlockSpec((tt, FOX_TILES), lambda i: (i, c0 // FOX_HEADS))
    const = lambda a: pl.BlockSpec(a.shape, lambda i: (0,) * a.ndim)

    def body(q_ref, k_ref, v_ref, ff_ref, b_ref, pl_ref, on_ref, qa_ref, ka_ref, va_ref, carry_s):
        @pl.when(pl.program_id(0) == 0)
        def _():
            carry_s[...] = jnp.zeros_like(carry_s)

        r = lax.broadcasted_iota(jnp.int32, (LANES, LANES), 0)
        cc = lax.broadcasted_iota(jnp.int32, (LANES, LANES), 1)
        tri = jnp.where(cc <= r, 1.0, 0.0).astype(BF16)
        bias = b_ref[...]
        for sub in range(tt // LANES):
            rows = pl.ds(sub * LANES, LANES)
            cs = _tri_dot(tri, _log_sigmoid(ff_ref[rows, :] + bias)) + carry_s[...]
            carry_s[...] = cs[LANES - 1:LANES, :]
            parts = _split3(cs)
            eq = sum(_dot(part, pl_ref[0, i], 1, 0) for i, part in enumerate(parts))
            ek = sum(_dot(part, pl_ref[1, i], 1, 0) for i, part in enumerate(parts))
            qa_ref[rows, :] = (q_ref[rows, :] * FOX_DIM ** -0.5 + eq + on_ref[0]).astype(BF16)
            ka_ref[rows, :] = (k_ref[rows, :] + ek + on_ref[1]).astype(BF16)
            va_ref[rows, :] = (v_ref[rows, :] + on_ref[2]).astype(BF16)

    out = pl.BlockSpec((tt, FOX_TILES), lambda i: (i, 0))
    return _pcall(body, name="fox_prep", out_shape=(jax.ShapeDtypeStruct((t, FOX_TILES), BF16),) * 3, grid=(t // tt,),
                  in_specs=[wide(TILE_FQ), wide(TILE_FK), wide(TILE_FV), pl.BlockSpec((tt, LANES), lambda i: (i, TILE_FF)),
                            pl.BlockSpec((1, LANES), lambda i: (0, 0)), const(place), const(ones)],
                  out_specs=(out,) * 3, scratch_shapes=[pltpu.VMEM((1, LANES), F32)],
                  dims=("arbitrary",))(pm, pm, pm, pm, bpad, place, ones)


def _fox_post(dc, pm, bpad):
    t = pm.shape[0]
    nb = t // LANES

    def body(dc_ref, ff_ref, b_ref, d_ref, db_ref):
        r = lax.broadcasted_iota(jnp.int32, (LANES, LANES), 0)
        cc = lax.broadcasted_iota(jnp.int32, (LANES, LANES), 1)
        tri = jnp.where(cc >= r, 1.0, 0.0).astype(BF16)
        bias = b_ref[...]

        def step(i, carry):
            tail, acc = carry
            rows = pl.ds(pl.multiple_of((nb - 1 - i) * LANES, LANES), LANES)
            cs = _tri_dot(tri, dc_ref[rows, :]) + tail
            dff = cs * _sigmoid(-(ff_ref[rows, :] + bias))
            d_ref[rows, :] = dff.astype(BF16)
            return cs[0:1, :], acc + jnp.sum(dff, axis=0, keepdims=True)

        zero = jnp.zeros((1, LANES), F32)
        _, acc = lax.fori_loop(0, nb, step, (zero, zero))
        db_ref[...] = acc

    return _pcall(body, name="fox_forget_bwd",
                  out_shape=(jax.ShapeDtypeStruct((t, LANES), BF16), jax.ShapeDtypeStruct((1, LANES), F32)), grid=(1,),
                  in_specs=[pl.BlockSpec((t, LANES), lambda i: (0, 0)), pl.BlockSpec((t, LANES), lambda i: (0, TILE_FF)),
                            pl.BlockSpec((1, LANES), lambda i: (0, 0))],
                  out_specs=(pl.BlockSpec((t, LANES), lambda i: (0, 0)), pl.BlockSpec((1, LANES), lambda i: (0, 0))),
                  dims=("arbitrary",))(dc, pm, bpad)


def _tri_tables(nb, q_major):
    pairs = [(i, j) for i in range(nb) for j in range(i + 1)] if q_major else \
            [(i, j) for j in range(nb) for i in range(j, nb)]
    return jnp.asarray([a for a, _ in pairs], jnp.int32), jnp.asarray([b for _, b in pairs], jnp.int32)


def _causal(s):
    n = s.shape[0]
    row = lax.broadcasted_iota(jnp.int32, (n, n), 0)
    col = lax.broadcasted_iota(jnp.int32, (n, n), 1)
    return jnp.where(col <= row, s, NEG)


def _lane_col(x, lane):
    sel = lax.broadcasted_iota(jnp.int32, x.shape, 1) == lane
    return jnp.sum(jnp.where(sel, x, 0.0), axis=1, keepdims=True)


def _fox_fwd(qa, ka, va, blk):
    t = qa.shape[0]
    nb = t // blk
    qi, kj = _tri_tables(nb, True)
    q_spec = pl.BlockSpec((blk, LANES), lambda h, s, qi_r, kj_r: (qi_r[s], h))
    k_spec = pl.BlockSpec((blk, LANES), lambda h, s, qi_r, kj_r: (kj_r[s], h))

    def body(qi_r, kj_r, q_ref, k_ref, v_ref, o_ref, ob_ref, qb_ref, m_s, acc_s):
        s_id = pl.program_id(1)
        i, j = qi_r[s_id], kj_r[s_id]

        @pl.when(j == 0)
        def _():
            m_s[...] = jnp.full_like(m_s, NEG)
            acc_s[...] = jnp.zeros_like(acc_s)

        def tile(diagonal):
            s = _dot(q_ref[...], k_ref[...], 1, 1)
            if diagonal:
                s = _causal(s)
            m_old = m_s[...]
            m_new = jnp.maximum(m_old, jnp.max(s, axis=1, keepdims=True))
            p = jnp.exp(s - jnp.tile(m_new, (1, blk // LANES)))
            acc_s[...] = jnp.exp(m_old - m_new) * acc_s[...] + _dot(p.astype(BF16), v_ref[...], 1, 0)
            m_s[...] = m_new

        @pl.when(j < i)
        def _():
            tile(False)

        @pl.when(j == i)
        def _():
            tile(True)
            acc = acc_s[...]
            l = _lane_col(acc, FOX_L_LANE)
            o = acc / l
            o_ref[...] = o
            ob_ref[...] = o.astype(BF16)
            hi, mid, lo = _split3(-(m_s[:, 0:1] + jnp.log(l)))
            lane = lax.broadcasted_iota(jnp.int32, acc.shape, 1)
            qb_ref[...] = jnp.where(lane == FOX_LSE_LANE, hi,
                                    jnp.where(lane == FOX_LSE_LANE + 1, mid,
                                              jnp.where(lane == FOX_LSE_LANE + 2, lo, q_ref[...])))

    wide = (t, FOX_TILES)
    return pl.pallas_call(
        body, name="fox_fwd",
        out_shape=(jax.ShapeDtypeStruct(wide, F32), jax.ShapeDtypeStruct(wide, BF16), jax.ShapeDtypeStruct(wide, BF16)),
        grid_spec=pltpu.PrefetchScalarGridSpec(
            num_scalar_prefetch=2, grid=(FOX_HEADS, qi.shape[0]), in_specs=[q_spec, k_spec, k_spec],
            out_specs=(q_spec,) * 3,
            scratch_shapes=[pltpu.VMEM((blk, LANES), F32), pltpu.VMEM((blk, LANES), F32)]),
        compiler_params=_params(("parallel", "arbitrary")),
    )(qi, kj, qa, ka, va)


def _fox_bwd(qa, ka, va, do, o, blk):
    t = qa.shape[0]
    nb = t // blk
    qi, kj = _tri_tables(nb, False)
    q_spec = pl.BlockSpec((blk, LANES), lambda h, s, qi_r, kj_r: (qi_r[s], h))
    k_spec = pl.BlockSpec((blk, LANES), lambda h, s, qi_r, kj_r: (kj_r[s], h))
    head_spec = pl.BlockSpec((t, LANES), lambda h, s, qi_r, kj_r: (0, h))
    head_col = pl.BlockSpec((None, t, 1), lambda h, s, qi_r, kj_r: (h, 0, 0))
    k_col = pl.BlockSpec((None, blk, 1), lambda h, s, qi_r, kj_r: (h, kj_r[s], 0))
    first_spec = pl.BlockSpec((blk, LANES), lambda h, s, qi_r, kj_r: (jnp.where(kj_r[s] == 0, qi_r[s], nb - 1), h))
    n_steps = int(qi.shape[0])

    def body(qi_r, kj_r, q_ref, k_ref, v_ref, do_ref, o_ref, dq_ref, dk_ref, dv_ref, rs_ref, cs_ref,
             doa_s, dq_s, dk_s, dv_s):
        s_id = pl.program_id(1)
        i, j = qi_r[s_id], kj_r[s_id]
        rows = pl.ds(pl.multiple_of(i * blk, blk), blk)

        @pl.when(j == 0)
        def _():
            dof = do_ref[...]
            hi, mid, lo = _split3(-jnp.sum(dof * o_ref[...], axis=1, keepdims=True))
            lane = lax.broadcasted_iota(jnp.int32, dof.shape, 1)
            doa = jnp.where(lane == FOX_C_LANE, hi.astype(F32),
                            jnp.where(lane == FOX_C_LANE + 1, mid.astype(F32),
                                      jnp.where(lane == FOX_C_LANE + 2, lo.astype(F32), dof)))
            doa_s[rows, :] = doa.astype(BF16)
            dq_s[rows, :] = jnp.zeros((blk, LANES), F32)

        @pl.when(i == j)
        def _():
            dk_s[...] = jnp.zeros_like(dk_s)
            dv_s[...] = jnp.zeros_like(dv_s)

        def tile(diagonal):
            q, k = q_ref[...], k_ref[...]
            s = _dot(q, k, 1, 1)
            if diagonal:
                s = _causal(s)
            p = jnp.exp(s)
            doa = doa_s[rows, :]
            ds = (p * _dot(doa, v_ref[...], 1, 1)).astype(BF16)
            dv_s[...] += _dot(p.astype(BF16), doa, 0, 0)
            dk_s[...] += _dot(ds, q, 0, 0)
            dq_s[rows, :] += _dot(ds, k, 1, 0)

        @pl.when(i > j)
        def _():
            tile(False)

        @pl.when(i == j)
        def _():
            tile(True)

        @pl.when(i == nb - 1)
        def _():
            dk = dk_s[...]
            dk_ref[...] = dk.astype(BF16)
            dv_ref[...] = dv_s[...].astype(BF16)
            cs_ref[...] = -_lane_col(dk, FOX_COLSUM_LANE)

        @pl.when(s_id == n_steps - 1)
        def _():
            dq = dq_s[...]
            dq_ref[...] = (dq * FOX_DIM ** -0.5).astype(BF16)
            rs_ref[...] = _lane_col(dq, FOX_ROWSUM_LANE)

    wide = jax.ShapeDtypeStruct((t, FOX_TILES), BF16)
    cols = jax.ShapeDtypeStruct((FOX_HEADS, t, 1), F32)
    return pl.pallas_call(
        body, name="fox_bwd", out_shape=(wide, wide, wide, cols, cols),
        grid_spec=pltpu.PrefetchScalarGridSpec(
            num_scalar_prefetch=2, grid=(FOX_HEADS, n_steps),
            in_specs=[q_spec, k_spec, k_spec, first_spec, first_spec],
            out_specs=(head_spec, k_spec, k_spec, head_col, k_col),
            scratch_shapes=[pltpu.VMEM((t, LANES), BF16), pltpu.VMEM((t, LANES), F32), pltpu.VMEM((blk, LANES), F32),
                            pltpu.VMEM((blk, LANES), F32)]),
        compiler_params=_params(("parallel", "arbitrary")),
    )(qi, kj, qa, ka, va, do, o)


def _merge_fwd(gm, bm, za, zb):
    t, d = za.shape
    tt = _tile(t, 512, 16)
    row = pl.BlockSpec((tt, d), lambda i: (i, 0))

    def body(gm_ref, b_ref, za_ref, zb_ref, o_ref):
        ga = _sigmoid(gm_ref[:, :d] + b_ref[:, :d])
        gb = _sigmoid(gm_ref[:, d:] + b_ref[:, d:])
        o_ref[...] = (ga * za_ref[...] + gb * zb_ref[...]).astype(BF16)

    return _pcall(body, name="merge_fwd", out_shape=jax.ShapeDtypeStruct((t, d), BF16), grid=(t // tt,),
                  in_specs=[pl.BlockSpec((tt, 2 * d), lambda i: (i, 0)), pl.BlockSpec((1, 2 * d), lambda i: (0, 0)), row, row],
                  out_specs=row, dims=("parallel",))(gm, bm, za, zb)


def _merge_bwd(dmix, gm, bm, za, zb):
    t, d = za.shape
    tt = _tile(t, 512, 16)
    row = pl.BlockSpec((tt, d), lambda i: (i, 0))
    wide = pl.BlockSpec((tt, 2 * d), lambda i: (i, 0))
    vec = pl.BlockSpec((1, 2 * d), lambda i: (0, 0))

    def body(dm_ref, gm_ref, b_ref, za_ref, zb_ref, dza_ref, dzb_ref, dgm_ref, db_ref):
        dm = dm_ref[...]
        ga = _sigmoid(gm_ref[:, :d] + b_ref[:, :d])
        gb = _sigmoid(gm_ref[:, d:] + b_ref[:, d:])
        dza_ref[...] = (dm * ga).astype(BF16)
        dzb_ref[...] = (dm * gb).astype(BF16)
        dla = dm * za_ref[...] * ga * (1.0 - ga)
        dlb = dm * zb_ref[...] * gb * (1.0 - gb)
        dgm_ref[:, :d] = dla.astype(BF16)
        dgm_ref[:, d:] = dlb.astype(BF16)
        pa = jnp.sum(dla, axis=0, keepdims=True)
        pb = jnp.sum(dlb, axis=0, keepdims=True)

        @pl.when(pl.program_id(0) == 0)
        def _():
            db_ref[:, :d] = pa
            db_ref[:, d:] = pb

        @pl.when(pl.program_id(0) > 0)
        def _():
            db_ref[:, :d] += pa
            db_ref[:, d:] += pb

    return _pcall(body, name="merge_bwd",
                  out_shape=(jax.ShapeDtypeStruct((t, d), BF16), jax.ShapeDtypeStruct((t, d), BF16),
                             jax.ShapeDtypeStruct((t, 2 * d), BF16), jax.ShapeDtypeStruct((1, 2 * d), F32)),
                  grid=(t // tt,), in_specs=[row, wide, vec, row, row], out_specs=(row, row, wide, vec),
                  dims=("arbitrary",))(dmix, gm, bm, za, zb)


def _ple_final(h3, pgl, pe, gain, target):
    t, d = h3.shape
    tt = _tile(t, 512, 16)
    row = pl.BlockSpec((tt, d), lambda i: (i, 0))
    vec = pl.BlockSpec((1, d), lambda i: (0, 0))
    lvec = pl.BlockSpec((1, LANES), lambda i: (0, 0))

    def body(h_ref, pgl_ref, pe_ref, g_ref, t_ref, dh_ref, dsg_ref, dpe_ref, loss_ref, dg_ref):
        pg = _sigmoid(pgl_ref[...])
        pe_v = pe_ref[...]
        h4 = h_ref[...] + pg * pe_v
        r = lax.rsqrt(jnp.mean(h4 * h4, axis=-1, keepdims=True) + EPS)
        gv = g_ref[...]
        err = h4 * r * gv - t_ref[...]
        part_loss = 0.5 * jnp.sum(jnp.mean(err * err, axis=-1, keepdims=True), axis=0, keepdims=True)
        dy = err * (1.0 / d)
        part_g = jnp.sum(dy * h4 * r, axis=0, keepdims=True)
        dyg = dy * gv
        dh = r * dyg - h4 * (r * r * r) * jnp.mean(dyg * h4, axis=-1, keepdims=True)
        dh_ref[...] = dh
        dsg_ref[...] = (dh * pe_v * pg * (1.0 - pg)).astype(BF16)
        dpe_ref[...] = (dh * pg).astype(BF16)

        @pl.when(pl.program_id(0) == 0)
        def _():
            loss_ref[...] = jnp.broadcast_to(part_loss, (1, LANES))
            dg_ref[...] = part_g

        @pl.when(pl.program_id(0) > 0)
        def _():
            loss_ref[...] += jnp.broadcast_to(part_loss, (1, LANES))
            dg_ref[...] += part_g

    return _pcall(body, name="ple_final",
                  out_shape=(jax.ShapeDtypeStruct((t, d), F32), jax.ShapeDtypeStruct((t, d), BF16),
                             jax.ShapeDtypeStruct((t, d), BF16), jax.ShapeDtypeStruct((1, LANES), F32),
                             jax.ShapeDtypeStruct((1, d), F32)),
                  grid=(t // tt,), in_specs=[row, row, row, vec, row], out_specs=(row, row, row, lvec, vec),
                  dims=("arbitrary",))(h3, pgl, pe, gain, target)


def _adamw_math(w, g, m, v):
    m = ADAM_B1 * m + (1.0 - ADAM_B1) * g
    v = ADAM_B2 * v + (1.0 - ADAM_B2) * (g * g)
    m_hat = m / (1.0 - ADAM_B1 ** ADAM_STEP)
    v_hat = v / (1.0 - ADAM_B2 ** ADAM_STEP)
    delta = -ADAM_LR * (m_hat / (jnp.sqrt(v_hat) + ADAM_EPS) + ADAM_WD * w)
    return delta, m, v


def _adamw(parts, w, m, v, name):
    n, r, c = parts.shape
    tr = _tile(r, 256, 16)
    row = pl.BlockSpec((tr, c), lambda i: (i, 0))

    def body(p_ref, w_ref, m_ref, v_ref, g_ref, d_ref, mo_ref, vo_ref):
        g = p_ref[0].astype(F32)
        for s in range(1, n):
            g = g + p_ref[s].astype(F32)
        g_ref[...] = g
        d_ref[...], mo_ref[...], vo_ref[...] = _adamw_math(w_ref[...], g, m_ref[...], v_ref[...])

    return _pcall(body, name=name, out_shape=(jax.ShapeDtypeStruct((r, c), F32),) * 4, grid=(r // tr,),
                  in_specs=[pl.BlockSpec((n, tr, c), lambda i: (0, i, 0)), row, row, row], out_specs=(row,) * 4,
                  dims=("parallel",))(parts, w, m, v)


ANY = pl.BlockSpec(memory_space=pl.ANY)


def _all_gather(shards):
    n = len(shards)

    def body(*refs):
        x_refs, out_refs = refs[:n], refs[n:2 * n]
        send_sems, recv_sems, local_sems = refs[2 * n:]
        x, y, cc = lax.axis_index("x"), lax.axis_index("y"), lax.axis_index("c")
        me, sibling = (x, y, cc), (x, y, 1 - cc)
        chips = [(1 - x, y), (x, 1 - y), (1 - x, 1 - y)]

        def slot(a, px, py, pc):
            return out_refs[a].at[4 * px + 2 * py + pc]

        def copy(a, k, block, to, src=None):
            return pltpu.make_async_remote_copy(
                src_ref=slot(a, *block) if src is None else src, dst_ref=slot(a, *block),
                send_sem=send_sems.at[7 * a + k], recv_sem=recv_sems.at[7 * a + k], device_id=to, device_id_type=MESH)

        local, sent = [], []
        for a in range(n):
            local.append(pltpu.make_async_copy(x_refs[a], slot(a, *me), local_sems.at[a]))
            sent.append(copy(a, 0, me, sibling, src=x_refs[a]))
            sent += [copy(a, 1 + j, me, (*chip, cc), src=x_refs[a]) for j, chip in enumerate(chips)]
        for cp in local + sent:
            cp.start()
        for j, chip in enumerate(chips):
            for a in range(n):
                copy(a, 1 + j, (*chip, cc), me).wait_recv()
                sent.append(copy(a, 4 + j, (*chip, cc), sibling))
                sent[-1].start()
        for a in range(n):
            copy(a, 0, sibling, me).wait_recv()
            for j, chip in enumerate(chips):
                copy(a, 4 + j, (*chip, 1 - cc), me).wait_recv()
        for cp in sent:
            cp.wait_send()
        for cp in local:
            cp.wait()

    return pl.pallas_call(
        body, name="weights_all_gather",
        out_shape=tuple(jax.ShapeDtypeStruct((N_DEV,) + s.shape, s.dtype) for s in shards),
        in_specs=[ANY] * n, out_specs=(ANY,) * n,
        scratch_shapes=[pltpu.SemaphoreType.DMA((7 * n,)), pltpu.SemaphoreType.DMA((7 * n,)),
                        pltpu.SemaphoreType.DMA((n,))],
    )(*shards)


def _reduce_scatter_exchange(blocks):
    n = len(blocks)

    def body(*refs):
        g_refs, recv_refs = refs[:n], refs[n:2 * n]
        send_sems, recv_sems, local_sems = refs[2 * n:]
        x, y, cc = lax.axis_index("x"), lax.axis_index("y"), lax.axis_index("c")
        me = 4 * x + 2 * y + cc
        local, sent, landing = [], [], []
        for a in range(n):
            local.append(pltpu.make_async_copy(g_refs[a].at[me], recv_refs[a].at[me], local_sems.at[a]))
        for k in range(1, N_DEV):
            px, py, pc = x ^ (k >> 2), y ^ ((k >> 1) & 1), cc ^ (k & 1)
            peer = 4 * px + 2 * py + pc
            for a in range(n):
                sems = dict(send_sem=send_sems.at[7 * a + k - 1], recv_sem=recv_sems.at[7 * a + k - 1],
                            device_id=(px, py, pc), device_id_type=MESH)
                sent.append(pltpu.make_async_remote_copy(src_ref=g_refs[a].at[peer], dst_ref=recv_refs[a].at[me], **sems))
                landing.append(pltpu.make_async_remote_copy(src_ref=g_refs[a].at[me], dst_ref=recv_refs[a].at[peer], **sems))
        for cp in local + sent:
            cp.start()
        for cp in landing:
            cp.wait_recv()
        for cp in sent:
            cp.wait_send()
        for cp in local:
            cp.wait()

    return pl.pallas_call(
        body, name="grads_reduce_scatter_exchange",
        out_shape=tuple(jax.ShapeDtypeStruct(b.shape, b.dtype) for b in blocks),
        in_specs=[ANY] * n, out_specs=(ANY,) * n,
        scratch_shapes=[pltpu.SemaphoreType.DMA((7 * n,)), pltpu.SemaphoreType.DMA((7 * n,)),
                        pltpu.SemaphoreType.DMA((n,))],
    )(*blocks)


HBM = pl.BlockSpec(memory_space=pltpu.HBM)
SEM = pl.BlockSpec(memory_space=pltpu.SEMAPHORE)
DATAFLOW = pltpu.SideEffectType.DATAFLOW_SIDE_EFFECTING


def _peers():
    x, y, cc = lax.axis_index("x"), lax.axis_index("y"), lax.axis_index("c")
    out = []
    for k in range(1, N_DEV):
        px, py, pc = x ^ (k >> 2), y ^ ((k >> 1) & 1), cc ^ (k & 1)
        out.append((k, (px, py, pc), 4 * px + 2 * py + pc))
    return 4 * x + 2 * y + cc, out


def _scatter_start(blocks, after, name, gather=False):
    n = len(blocks)
    lands = [lax.empty((N_DEV,) + b.shape if gather else b.shape, b.dtype) for b in blocks]

    def body(*refs):
        g_refs, land_refs = refs[:n], refs[n:2 * n]
        send_sems, recv_sems, token = refs[2 * n + 1], refs[2 * n + 2], refs[-1]
        me, peers = _peers()
        for k, peer, slot in peers:
            for a in range(n):
                pltpu.make_async_remote_copy(
                    src_ref=g_refs[a] if gather else g_refs[a].at[slot], dst_ref=land_refs[a].at[me],
                    send_sem=send_sems.at[7 * a + k - 1],
                    recv_sem=recv_sems.at[7 * a + k - 1], device_id=peer, device_id_type=MESH).start()
        token[...] = jnp.zeros_like(token)

    thru = [pltpu.HBM(b.shape, b.dtype) for b in blocks]
    thru_lands = [pltpu.HBM(b.shape, b.dtype) for b in lands]
    return pl.pallas_call(
        body, name=name,
        out_shape=(pltpu.SemaphoreType.DMA((7 * n,)), pltpu.SemaphoreType.DMA((7 * n,)), *thru, *thru_lands,
                   jax.ShapeDtypeStruct((8, LANES), F32)),
        in_specs=[HBM] * (2 * n) + [pl.BlockSpec(memory_space=pl.ANY)],
        out_specs=(SEM, SEM, *[HBM] * (2 * n), pl.BlockSpec(memory_space=pltpu.VMEM)),
        input_output_aliases={i: 2 + i for i in range(2 * n)},
        compiler_params=pltpu.CompilerParams(has_side_effects=DATAFLOW),
    )(*[pltpu.with_memory_space_constraint(a, pltpu.HBM) for a in list(blocks) + lands], after)


def _scatter_wait(started, after, name, gather=False):
    send_sems, recv_sems, *rest = started
    n = (len(rest) - 1) // 2
    thru = rest[:2 * n]

    def body(*refs):
        g_refs, land_refs = refs[:n], refs[n:2 * n]
        send_sems, recv_sems = refs[2 * n], refs[2 * n + 1]
        me, peers = _peers()
        for k, peer, slot in peers:
            for a in range(n):
                copy = pltpu.make_async_remote_copy(
                    src_ref=g_refs[a] if gather else g_refs[a].at[slot], dst_ref=land_refs[a].at[slot],
                    send_sem=send_sems.at[7 * a + k - 1],
                    recv_sem=recv_sems.at[7 * a + k - 1], device_id=peer, device_id_type=MESH)
                copy.wait_send()
                copy.wait_recv()

    out = pl.pallas_call(
        body, name=name, out_shape=tuple(pltpu.HBM(a.shape, a.dtype) for a in thru),
        in_specs=[HBM] * (2 * n) + [SEM, SEM, pl.BlockSpec(memory_space=pl.ANY)], out_specs=tuple([HBM] * (2 * n)),
        input_output_aliases={i: i for i in range(2 * n)},
        compiler_params=pltpu.CompilerParams(has_side_effects=DATAFLOW),
    )(*thru, send_sems, recv_sems, after)
    return out[:n], out[n:]


BIG = (("w_ffn1_gate", "colT"), ("w_ffn1_up", "colT"), ("w_ffn1_down", "row"), ("w_ffn2_gate", "colT"),
       ("w_ffn2_up", "colT"), ("w_ffn2_down", "row"), ("w_in", "colT"), ("w_merge", "col"), ("w_ret_out", "col"),
       ("w_fox_out", "col"), ("w_out", "row"), ("w_ple", "col"), ("w_ple_gate", "row"))


def _shard_view(a, kind):
    a = a.reshape(a.shape[-2:])
    return a.T if kind == "colT" else a


def _unview(a, kind, shape):
    return (a.T if kind == "colT" else a).reshape(shape)


def _full_from_slots(g, kind):
    n, r, c = g.shape
    return g.transpose(1, 0, 2).reshape(r, n * c) if kind == "col" else g.reshape(n * r, c)


def _slots_from_full(f, kind):
    r, c = f.shape
    return f.reshape(r, N_DEV, c // N_DEV).transpose(1, 0, 2) if kind == "col" else f.reshape(N_DEV, r // N_DEV, c)


EARLY_GROUPS = (("w_ple_gate", "w_ple", "w_ffn2_down", "w_ffn2_gate", "w_ffn2_up", "w_out", "w_ret_out", "w_fox_out"),
                ("w_in", "w_merge"))


def _scatter_group(gw, names, after, name):
    kind = dict(BIG)
    return names, _scatter_start([_slots_from_full(gw[n], kind[n]) for n in names], after, name)


def _pad_heads(w):
    d = w.shape[1]
    return jnp.pad(w.reshape(FOX_HEADS, FOX_DIM, d), ((0, 0), (0, LANES - FOX_DIM), (0, 0))).reshape(FOX_TILES, d)


def _unpad_heads(w):
    d = w.shape[1]
    return w.reshape(FOX_HEADS, LANES, d)[:, :FOX_DIM].reshape(FOX_WIDTH, d)


def _deinterleave_rows(w):
    d = w.shape[1]
    return w.reshape(RET_HEADS, RET_DIM // 2, 2, d).transpose(0, 2, 1, 3).reshape(RET_WIDTH, d)


def _interleave_rows(w):
    d = w.shape[1]
    return w.reshape(RET_HEADS, 2, RET_DIM // 2, d).transpose(0, 2, 1, 3).reshape(RET_WIDTH, d)


def _pad_w_in(wt):
    d = wt.shape[1]
    rw, fw = RET_WIDTH, FOX_WIDTH
    fo = 4 * rw
    return jnp.concatenate([
        _deinterleave_rows(wt[:rw]), _deinterleave_rows(wt[rw:2 * rw]), wt[2 * rw:4 * rw],
        _pad_heads(wt[fo:fo + fw]), _pad_heads(wt[fo + fw:fo + 2 * fw]), _pad_heads(wt[fo + 2 * fw:fo + 3 * fw]),
        wt[fo + 3 * fw:], jnp.zeros((2 * LANES - FOX_HEADS, d), wt.dtype)], axis=0)


def _unpad_w_in(g):
    rw = RET_WIDTH
    f0 = 4 * rw
    return jnp.concatenate([
        _interleave_rows(g[:rw]), _interleave_rows(g[rw:2 * rw]), g[2 * rw:4 * rw],
        _unpad_heads(g[f0:f0 + FOX_TILES]), _unpad_heads(g[f0 + FOX_TILES:f0 + 2 * FOX_TILES]),
        _unpad_heads(g[f0 + 2 * FOX_TILES:f0 + 3 * FOX_TILES]),
        g[f0 + 3 * FOX_TILES:f0 + 3 * FOX_TILES + FOX_HEADS]], axis=0)


SMALL = ("ln_ffn1", "ln_mix", "b_forget", "b_merge", "ln_ffn2", "ln_ple", "ln_final")


def _small_rows(n):
    rows = -(-n // LANES)
    return -(-rows // 8) * 8


def _pack_small(vals, with_loss=None):
    parts = []
    for name in SMALL:
        v = vals[name].reshape(-1).astype(F32)
        rows = _small_rows(v.shape[0])
        parts.append(jnp.pad(v, (0, rows * LANES - v.shape[0])).reshape(rows, LANES))
    if with_loss is not None:
        parts.append(jnp.pad(with_loss.reshape(1, LANES), ((0, 7), (0, 0))))
    else:
        parts.append(jnp.zeros((8, LANES), F32))
    return jnp.concatenate(parts, axis=0)


def _unpack_small(packed, shapes):
    out, at = {}, 0
    for name in SMALL:
        n = int(np.prod(shapes[name]))
        rows = _small_rows(n)
        out[name] = packed[at:at + rows].reshape(-1)[:n].reshape(shapes[name])
        at += rows
    return out, packed[at, 0]


def _gather_finish(group, after, me):
    names, started, wait_name = group
    kind = dict(BIG)
    sent, landed = _scatter_wait(started, after, wait_name, gather=True)
    return {n: _full_from_slots(lax.dynamic_update_slice_in_dim(land, shard[None], me, 0), kind[n])
            for n, shard, land in zip(names, sent, landed)}


def _local_step(x, p, positions, target, w, small, me, entry_token, gathers):
    t, d = x.shape
    gain = lambda n: small[n].reshape(1, d)
    w = dict(w)
    bpad = jnp.pad(small["b_forget"].reshape(1, FOX_HEADS), ((0, 0), (0, LANES - FOX_HEADS)))
    bm = small["b_merge"].reshape(1, 2 * d)
    fox_blk = _tile(t, 1024, 128)

    def ffn_fwd(n, tag, down_gather=None):
        g = _mm([(n, w[f"w_{tag}_gate"], "nt")], BF16, f"{tag}_gate", tn=1408)
        u = _mm([(n, w[f"w_{tag}_up"], "nt")], BF16, f"{tag}_up", tn=1408)
        a = _swiglu_fwd(g, u, f"{tag}_swiglu")
        if down_gather is not None:
            w.update(_gather_finish(down_gather, a, me))
        return g, u, a, _mm([(a, w[f"w_{tag}_down"], "nn")], F32, f"{tag}_down")

    n1 = _rms_fwd(x, gain("ln_ffn1") + entry_token, "rms_ffn1")
    g1, u1, a1, f1 = ffn_fwd(n1, "ffn1", down_gather=gathers[0])
    h1, u = _rms_fwd(x, gain("ln_mix"), "rms_mix", f=f1)
    w.update(_gather_finish(gathers[1], f1, me))
    w_in_t = _pad_w_in(w["w_in"])
    pm = _mm([(u, w_in_t, "nt")], F32, "mixer_in", tn=1792)
    w.update(_gather_finish(gathers[2], pm, me))
    gm = _mm([(u, w["w_merge"], "nn")], F32, "mixer_gates")

    half = jnp.arange(RET_DIM // 2, dtype=F32) / (RET_DIM // 2)
    inv = 1.0 / (ROPE_BASE ** half)
    inv2 = jnp.concatenate([inv, inv]).reshape(1, RET_DIM)
    sign2 = jnp.concatenate([-jnp.ones((RET_DIM // 2,), F32), jnp.ones((RET_DIM // 2,), F32)]).reshape(1, RET_DIM)
    cos2, sin2 = _rope_tables(positions.reshape(t, 1), inv2, sign2)
    consts = _ret_consts()
    y_ret, y_raw, states = _ret_fwd(pm, cos2, sin2, consts)
    w.update(_gather_finish(gathers[3], y_raw, me))
    w_fox_pad = _pad_heads(w["w_fox_out"])
    za = _mm([(y_ret, w["w_ret_out"], "nn")], F32, "ret_out")

    qa, ka, va = _fox_prep(pm, bpad)
    o_fox, y_fox, qa_b = _fox_fwd(qa, ka, va, fox_blk)
    zb = _mm([(y_fox, w_fox_pad, "nn")], F32, "fox_out")

    mix = _merge_fwd(gm, bm, za, zb)
    mo = _mm([(mix, w["w_out"], "nn")], F32, "mix_out")
    h2, n2 = _rms_fwd(h1, gain("ln_ffn2"), "rms_ffn2", f=mo, scale=1.0)
    w.update(_gather_finish(gathers[4], mo, me))
    g2, u2, a2, f2 = ffn_fwd(n2, "ffn2")
    h3, n3 = _rms_fwd(h2, gain("ln_ple"), "rms_ple", f=f2)
    pgl = _mm([(n3, w["w_ple_gate"], "nn")], F32, "ple_gate")
    pb = p.astype(BF16)
    pe = _mm([(pb, w["w_ple"], "nn")], F32, "ple_embed")

    gw, gs = {}, {}
    dh4, dsg, dpe, loss, gs["ln_final"] = _ple_final(h3, pgl, pe, gain("ln_final"), target)
    gw["w_ple_gate"] = _mm([(n3, dsg, "tn")], BF16, "d_w_ple_gate", tn=256)
    gw["w_ple"] = _mm([(pb, dpe, "tn")], BF16, "d_w_ple", tn=256)
    dn3 = _mm([(dsg, w["w_ple_gate"], "nt")], F32, "d_n3")
    dh3, dh3_half, gs["ln_ple"] = _rms_bwd(dn3, h3, gain("ln_ple"), dh4, "rms_ple_bwd", 0.5)

    def ffn_bwd(dh_half, g, u_, a, n, tag, scatter_now=None):
        gw[f"w_{tag}_down"] = _mm([(a, dh_half, "tn")], BF16, f"d_w_{tag}_down", tm=1408, tn=512)
        start = scatter_now if scatter_now is not None else (lambda *_: None)
        token = start((f"w_{tag}_down",), dh_half, "c")
        da = _mm([(dh_half, w[f"w_{tag}_down"], "nt")], BF16, f"d_a_{tag}", tn=1408, after=token)
        dg, du_ = _swiglu_bwd(da, g, u_, f"{tag}_swiglu_bwd")
        gw[f"w_{tag}_gate"] = _mm([(dg, n, "tn")], BF16, f"d_w_{tag}_gate", tm=1408, tn=512)
        token = start((f"w_{tag}_gate",), da, "d")
        gw[f"w_{tag}_up"] = _mm([(du_, n, "tn")], BF16, f"d_w_{tag}_up", tm=1408, tn=512, after=token)
        token = start((f"w_{tag}_up",), da, "e")
        return _mm([(dg, w[f"w_{tag}_gate"], "nn"), (du_, w[f"w_{tag}_up"], "nn")], F32, f"d_n_{tag}", tm=512,
                   after=token)

    dn2 = ffn_bwd(dh3_half, g2, u2, a2, n2, "ffn2")
    dh2, dh2_b, gs["ln_ffn2"] = _rms_bwd(dn2, h2, gain("ln_ffn2"), dh3, "rms_ffn2_bwd", 1.0)

    gw["w_out"] = _mm([(mix, dh2_b, "tn")], BF16, "d_w_out", tn=256)
    dmix = _mm([(dh2_b, w["w_out"], "nt")], F32, "d_mix")
    dza, dzb, dgm, gs["b_merge"] = _merge_bwd(dmix, gm, bm, za, zb)
    gw["w_ret_out"] = _mm([(y_ret, dza, "tn")], BF16, "d_w_ret_out", tn=256)
    gw["w_fox_out"] = _unpad_heads(_mm([(y_fox, dzb, "tn")], BF16, "d_w_fox_out", tn=256))
    dy_ret = _mm([(dza, w["w_ret_out"], "nt")], F32, "d_y_ret")
    do_fox = _mm([(dzb, w_fox_pad, "nt")], F32, "d_y_fox")

    pending = [_scatter_group(gw, EARLY_GROUPS[0], dy_ret, "grads_scatter_a_start")]
    token = pending[0][1][-1][0, 0]
    drq, drk, drv, drg = _ret_bwd(dy_ret, pm, cos2, sin2, y_raw, states, consts[:3] + (consts[3] + token,))

    dqa, dka, dva, ds_rows, ds_cols = _fox_bwd(qa_b, ka, va, do_fox, o_fox, fox_blk)
    dc = jnp.pad((ds_rows + ds_cols).reshape(FOX_HEADS, t).T, ((0, 0), (0, LANES - FOX_HEADS)))
    dff, db_forget = _fox_post(dc, pm, bpad)
    gs["b_forget"] = db_forget[:, :FOX_HEADS]

    dpm = jnp.concatenate([drq, drk, drv, drg, dqa, dka, dva, dff, jnp.zeros((t, LANES), BF16)], axis=1)
    gw["w_merge"] = _mm([(u, dgm, "tn")], BF16, "d_w_merge", tn=512)
    gw["w_in"] = _unpad_w_in(_mm([(dpm, u, "tn")], BF16, "d_w_in", tm=1792, tn=512))
    du = _mm([(dpm, w_in_t, "nn"), (dgm, w["w_merge"], "nt")], F32, "d_u", tm=1024, tn=512)
    pending.append(_scatter_group(gw, EARLY_GROUPS[1], du, "grads_scatter_b_start"))
    token = pending[1][1][-1][0:1, 0:1]
    dh1, dh1_half, gs["ln_mix"] = _rms_bwd(du, h1, gain("ln_mix") + token, dh2, "rms_mix_bwd", 0.5)

    def scatter_now(names, after, tag):
        pending.append(_scatter_group(gw, names, after, f"grads_scatter_{tag}_start"))
        return pending[-1][1][-1]

    dn1 = ffn_bwd(dh1_half, g1, u1, a1, n1, "ffn1", scatter_now=scatter_now)
    dx, _, gs["ln_ffn1"] = _rms_bwd(dn1, x, gain("ln_ffn1"), dh1, "rms_ffn1_bwd", 1.0)
    return loss, dx, gw, gs, pending


WEIGHTS = ("ln_ffn1", "w_ffn1_gate", "w_ffn1_up", "w_ffn1_down", "ln_mix", "w_in", "b_forget", "w_merge", "b_merge",
           "w_ret_out", "w_fox_out", "w_out", "ln_ffn2", "w_ffn2_gate", "w_ffn2_up", "w_ffn2_down", "ln_ple", "w_ple",
           "w_ple_gate", "ln_final")


def kernel(x, p, positions, ln_ffn1, w_ffn1_gate, w_ffn1_up, w_ffn1_down, ln_mix, w_in, b_forget, w_merge, b_merge, w_ret_out, w_fox_out, w_out, ln_ffn2, w_ffn2_gate, w_ffn2_up, w_ffn2_down, ln_ple, w_ple, w_ple_gate, ln_final, loss_target, m_ln_ffn1, m_w_ffn1_gate, m_w_ffn1_up, m_w_ffn1_down, m_ln_mix, m_w_in, m_b_forget, m_w_merge, m_b_merge, m_w_ret_out, m_w_fox_out, m_w_out, m_ln_ffn2, m_w_ffn2_gate, m_w_ffn2_up, m_w_ffn2_down, m_ln_ple, m_w_ple, m_w_ple_gate, m_ln_final, v_ln_ffn1, v_w_ffn1_gate, v_w_ffn1_up, v_w_ffn1_down, v_ln_mix, v_w_in, v_b_forget, v_w_merge, v_b_merge, v_w_ret_out, v_w_fox_out, v_w_out, v_ln_ffn2, v_w_ffn2_gate, v_w_ffn2_up, v_w_ffn2_down, v_ln_ple, v_w_ple, v_w_ple_gate, v_ln_final):
    args = dict(ln_ffn1=ln_ffn1, w_ffn1_gate=w_ffn1_gate, w_ffn1_up=w_ffn1_up, w_ffn1_down=w_ffn1_down, ln_mix=ln_mix, w_in=w_in, b_forget=b_forget, w_merge=w_merge, b_merge=b_merge, w_ret_out=w_ret_out, w_fox_out=w_fox_out, w_out=w_out, ln_ffn2=ln_ffn2, w_ffn2_gate=w_ffn2_gate, w_ffn2_up=w_ffn2_up, w_ffn2_down=w_ffn2_down, ln_ple=ln_ple, w_ple=w_ple, w_ple_gate=w_ple_gate, ln_final=ln_final)
    moms = dict(ln_ffn1=m_ln_ffn1, w_ffn1_gate=m_w_ffn1_gate, w_ffn1_up=m_w_ffn1_up, w_ffn1_down=m_w_ffn1_down, ln_mix=m_ln_mix, w_in=m_w_in, b_forget=m_b_forget, w_merge=m_w_merge, b_merge=m_b_merge, w_ret_out=m_w_ret_out, w_fox_out=m_w_fox_out, w_out=m_w_out, ln_ffn2=m_ln_ffn2, w_ffn2_gate=m_w_ffn2_gate, w_ffn2_up=m_w_ffn2_up, w_ffn2_down=m_w_ffn2_down, ln_ple=m_ln_ple, w_ple=m_w_ple, w_ple_gate=m_w_ple_gate, ln_final=m_ln_final)
    vars_ = dict(ln_ffn1=v_ln_ffn1, w_ffn1_gate=v_w_ffn1_gate, w_ffn1_up=v_w_ffn1_up, w_ffn1_down=v_w_ffn1_down, ln_mix=v_ln_mix, w_in=v_w_in, b_forget=v_b_forget, w_merge=v_w_merge, b_merge=v_b_merge, w_ret_out=v_w_ret_out, w_fox_out=v_w_fox_out, w_out=v_w_out, ln_ffn2=v_ln_ffn2, w_ffn2_gate=v_w_ffn2_gate, w_ffn2_up=v_w_ffn2_up, w_ffn2_down=v_w_ffn2_down, ln_ple=v_ln_ple, w_ple=v_w_ple, w_ple_gate=v_w_ple_gate, ln_final=v_ln_final)
    kinds = ("grad", "delta", "new_m", "new_v")

    me = 4 * lax.axis_index("x") + 2 * lax.axis_index("y") + lax.axis_index("c")
    kind_of = dict(BIG)
    shard = {n: _shard_view(args[n], kind).astype(BF16) for n, kind in BIG}
    first = ("w_ffn1_gate", "w_ffn1_up")
    groups = (("w_ffn1_down",), ("w_in",), ("w_merge",), ("w_ret_out", "w_fox_out", "w_out"))
    groups += (tuple(n for n, _ in BIG if n not in first + sum(groups, ())),)
    gathered = _all_gather([shard[n] for n in first])
    w_full = {n: _full_from_slots(g, kind_of[n]) for n, g in zip(first, gathered)}
    gathers, after = [], gathered[0]
    for tag, names in zip("zambc", groups):
        started = _scatter_start([shard[n] for n in names], after, f"weights_gather_{tag}_start", gather=True)
        gathers.append((names, started, f"weights_gather_{tag}_wait"))
        after = started[-1]

    small = {n: args[n] for n in SMALL}
    loss_part, dx, gw, gs, pending = _local_step(x[0], p[0, 0], positions[0], loss_target[0], w_full, small, me,
                                                 after[0:1, 0:1], gathers)

    parts_of = {}
    for tag, (names, started) in zip("abcde", pending):
        sent, landed = _scatter_wait(started, dx, f"grads_scatter_{tag}_wait")
        for n, blk, land in zip(names, sent, landed):
            own = lax.dynamic_index_in_dim(blk, me, 0, keepdims=True)
            parts_of[n] = lax.dynamic_update_slice_in_dim(land, own, me, 0)
    late = [(n, kind) for n, kind in BIG if n not in parts_of]
    small_part = _pack_small(gs, with_loss=loss_part)
    blocks = [_slots_from_full(gw[n], kind) for n, kind in late]
    blocks.append(jnp.broadcast_to(small_part, (N_DEV,) + small_part.shape))
    recv = _reduce_scatter_exchange(blocks)
    parts_of.update({n: r for (n, _), r in zip(late, recv)})

    res = {}
    for n, kind in BIG:
        parts = parts_of[n]
        outs = _adamw(parts, _shard_view(args[n], kind), _shard_view(moms[n], kind), _shard_view(vars_[n], kind),
                      f"adamw_{n}")
        for what, o in zip(kinds, outs):
            res[(what, n)] = _unview(o, kind, args[n].shape)
    s_outs = _adamw(recv[-1], _pack_small(small), _pack_small({n: moms[n] for n in SMALL}),
                    _pack_small({n: vars_[n] for n in SMALL}), "adamw_small")
    for what, sm in zip(kinds, s_outs):
        svals, extra = _unpack_small(sm, {n: args[n].shape for n in SMALL})
        if what == "grad":
            loss = extra
        for n in SMALL:
            res[(what, n)] = svals[n]
    return (loss, dx[None], *[res[(what, n)] for what in kinds for n in WEIGHTS])
```
